```python
import math
import jax, jax.numpy as jnp
from jax import lax
import numpy as np

D_MODEL = 2048
BATCH = 8
SEQ = 4096
DEPTH = 2

HEAD_DIM = 64
N_Q_HEADS = 16
N_KV_HEADS = 4
GQA_GROUP = N_Q_HEADS // N_KV_HEADS
ATTN_WIDTH = N_Q_HEADS * HEAD_DIM
KV_WIDTH = N_KV_HEADS * HEAD_DIM
WINDOW = 128
BLOCK = 128
SSM_WIDTH = D_MODEL // 2
SSM_GROUP_CH = 16
SSM_GROUPS = SSM_WIDTH // SSM_GROUP_CH
SSM_STATE = 64
DT_MIN = 1e-3
DT_MAX = 1e-1
D_FF = -(-8 * D_MODEL // (3 * 256)) * 256
OFF_Q = 0
OFF_K = OFF_Q + ATTN_WIDTH
OFF_V = OFF_K + KV_WIDTH
OFF_U = OFF_V + KV_WIDTH
OFF_G = OFF_U + SSM_WIDTH
IN_WIDTH = OFF_G + 2 * D_MODEL
RMS_EPS = 1e-6

kernel_name = "hybrid_swa_sink_s5_gated_block"


def rmsnorm(x, g):
    xf = x.astype(jnp.float32)
    y = xf * lax.rsqrt(jnp.mean(xf * xf, axis=-1, keepdims=True) + RMS_EPS)
    return (y * g.astype(jnp.float32)).astype(x.dtype)


def alibi_slopes():
    return jnp.exp2(-8.0 * jnp.arange(1, N_Q_HEADS + 1, dtype=jnp.float32) / N_Q_HEADS)


def sliding_window_attention(q, k, v, q_gain, k_gain, sinks):
    B, L = q.shape[0], q.shape[1]
    nb = L // BLOCK
    q = rmsnorm(q, q_gain).astype(jnp.float32)
    k = rmsnorm(k, k_gain).astype(jnp.float32)
    v = v.astype(jnp.float32)
    qb = q.reshape(B, nb, BLOCK, N_KV_HEADS, GQA_GROUP, HEAD_DIM)
    pad = ((0, 0), (BLOCK, 0), (0, 0), (0, 0))
    kp = jnp.pad(k, pad)[:, :L].reshape(B, nb, BLOCK, N_KV_HEADS, HEAD_DIM)
    vp = jnp.pad(v, pad)[:, :L].reshape(B, nb, BLOCK, N_KV_HEADS, HEAD_DIM)
    kb = jnp.concatenate([kp, k.reshape(B, nb, BLOCK, N_KV_HEADS, HEAD_DIM)], axis=2)
    vb = jnp.concatenate([vp, v.reshape(B, nb, BLOCK, N_KV_HEADS, HEAD_DIM)], axis=2)
    scores = jnp.einsum('bnqkgd,bnskd->bnkgqs', qb, kb) * (HEAD_DIM ** -0.5)
    t_loc = jnp.arange(BLOCK)
    s_loc = jnp.arange(2 * BLOCK) - BLOCK
    dist = (t_loc[:, None] - s_loc[None, :]).astype(jnp.float32)
    s_abs = jnp.arange(nb)[:, None] * BLOCK + s_loc[None, :]
    valid = (dist >= 0)[None] & (dist < WINDOW)[None] & (s_abs >= 0)[:, None, :]
    bias = (-alibi_slopes()[:, None, None] * dist[None]).reshape(N_KV_HEADS, GQA_GROUP, BLOCK, 2 * BLOCK)
    scores = jnp.where(valid[None, :, None, None], scores + bias[None, None], -jnp.inf)
    sink = sinks.astype(jnp.float32).reshape(1, 1, N_KV_HEADS, GQA_GROUP, 1, 1)
    m = jnp.maximum(jnp.max(scores, axis=-1, keepdims=True), sink)
    p = jnp.exp(scores - m)
    denom = jnp.sum(p, axis=-1, keepdims=True) + jnp.exp(sink - m)
    out = jnp.einsum('bnkgqs,bnskd->bnqkgd', p / denom, vb)
    return out.reshape(B, L, ATTN_WIDTH)


def s5_ssm(u, lam_re, lam_im, log_dt, b_re, b_im, c_re, c_im, d_skip):
    B, L = u.shape[0], u.shape[1]
    uf = u.astype(jnp.float32).reshape(B, L, SSM_GROUPS, SSM_GROUP_CH)
    lr = lam_re.astype(jnp.float32)
    li = lam_im.astype(jnp.float32)
    dt = jnp.exp(log_dt.astype(jnp.float32))[:, None]
    mag = jnp.exp(lr * dt)
    ar = mag * jnp.cos(li * dt)
    ai = mag * jnp.sin(li * dt)
    den = lr * lr + li * li
    fr = ((ar - 1.0) * lr + ai * li) / den
    fi = (ai * lr - (ar - 1.0) * li) / den
    br = b_re.astype(jnp.float32)
    bi = b_im.astype(jnp.float32)
    bbar_r = fr[:, :, None] * br - fi[:, :, None] * bi
    bbar_i = fr[:, :, None] * bi + fi[:, :, None] * br
    bu_r = jnp.einsum('blgh,gph->blgp', uf, bbar_r)
    bu_i = jnp.einsum('blgh,gph->blgp', uf, bbar_i)
    a_r = jnp.broadcast_to(ar[None, None], (1, L, SSM_GROUPS, SSM_STATE))
    a_i = jnp.broadcast_to(ai[None, None], (1, L, SSM_GROUPS, SSM_STATE))

    def combine(e1, e2):
        ar1, ai1, br1, bi1 = e1
        ar2, ai2, br2, bi2 = e2
        return (ar2 * ar1 - ai2 * ai1,
                ar2 * ai1 + ai2 * ar1,
                ar2 * br1 - ai2 * bi1 + br2,
                ar2 * bi1 + ai2 * br1 + bi2)

    _, _, s_r, s_i = lax.associative_scan(combine, (a_r, a_i, bu_r, bu_i), axis=1)
    y = (jnp.einsum('blgp,ghp->blgh', s_r, c_re.astype(jnp.float32))
         - jnp.einsum('blgp,ghp->blgh', s_i, c_im.astype(jnp.float32))
         + d_skip.astype(jnp.float32).reshape(SSM_GROUPS, SSM_GROUP_CH) * uf)
    return y.reshape(B, L, SSM_WIDTH)


def _fwd_setup_inputs(seed: int = 0) -> dict:
    key = jax.random.key(seed)
    ks = jax.random.split(key, 24)
    f32 = jnp.float32
    nrm = lambda k, shape, scale: jax.random.normal(k, shape, f32) * scale
    x = jax.random.normal(ks[0], (BATCH, SEQ, D_MODEL), f32)
    norm_mix_g = 1.0 + nrm(ks[1], (DEPTH, D_MODEL), 0.02)
    w_in = nrm(ks[2], (DEPTH, D_MODEL, IN_WIDTH), D_MODEL ** -0.5)
    gate_bias = nrm(ks[3], (DEPTH, 2 * D_MODEL), 0.02)
    q_norm_g = 1.0 + nrm(ks[4], (DEPTH, HEAD_DIM), 0.02)
    k_norm_g = 1.0 + nrm(ks[5], (DEPTH, HEAD_DIM), 0.02)
    attn_sinks = nrm(ks[6], (DEPTH, N_Q_HEADS), 0.5)
    ssm_lambda_re = -0.5 + nrm(ks[7], (DEPTH, SSM_GROUPS, SSM_STATE), 0.01)
    ssm_lambda_im = (math.pi * jnp.arange(SSM_STATE, dtype=f32))[None, None, :] + nrm(ks[8], (DEPTH, SSM_GROUPS, SSM_STATE), 0.01)
    ssm_log_dt = jax.random.uniform(ks[9], (DEPTH, SSM_GROUPS), f32, math.log(DT_MIN), math.log(DT_MAX))
    ssm_b_re = nrm(ks[10], (DEPTH, SSM_GROUPS, SSM_STATE, SSM_GROUP_CH), (2 * SSM_GROUP_CH) ** -0.5)
    ssm_b_im = nrm(ks[11], (DEPTH, SSM_GROUPS, SSM_STATE, SSM_GROUP_CH), (2 * SSM_GROUP_CH) ** -0.5)
    ssm_c_re = nrm(ks[12], (DEPTH, SSM_GROUPS, SSM_GROUP_CH, SSM_STATE), (2 * SSM_STATE) ** -0.5)
    ssm_c_im = nrm(ks[13], (DEPTH, SSM_GROUPS, SSM_GROUP_CH, SSM_STATE), (2 * SSM_STATE) ** -0.5)
    ssm_d = nrm(ks[14], (DEPTH, SSM_WIDTH), 1.0)
    ssm_glu_w = nrm(ks[15], (DEPTH, SSM_WIDTH, SSM_WIDTH), SSM_WIDTH ** -0.5)
    ssm_glu_b = nrm(ks[16], (DEPTH, SSM_WIDTH), 0.02)
    w_attn_branch = nrm(ks[17], (DEPTH, ATTN_WIDTH, D_MODEL), ATTN_WIDTH ** -0.5)
    w_ssm_branch = nrm(ks[18], (DEPTH, SSM_WIDTH, D_MODEL), SSM_WIDTH ** -0.5)
    w_out = nrm(ks[19], (DEPTH, D_MODEL, D_MODEL), D_MODEL ** -0.5)
    norm_ffn_g = 1.0 + nrm(ks[20], (DEPTH, D_MODEL), 0.02)
    w_ffn_in = nrm(ks[21], (DEPTH, D_MODEL, 2 * D_FF), D_MODEL ** -0.5)
    w_ffn_out = nrm(ks[22], (DEPTH, D_FF, D_MODEL), D_FF ** -0.5)
    return {"x": x, "norm_mix_g": norm_mix_g, "w_in": w_in, "gate_bias": gate_bias,
            "q_norm_g": q_norm_g, "k_norm_g": k_norm_g, "attn_sinks": attn_sinks,
            "ssm_lambda_re": ssm_lambda_re, "ssm_lambda_im": ssm_lambda_im, "ssm_log_dt": ssm_log_dt,
            "ssm_b_re": ssm_b_re, "ssm_b_im": ssm_b_im, "ssm_c_re": ssm_c_re, "ssm_c_im": ssm_c_im,
            "ssm_d": ssm_d, "ssm_glu_w": ssm_glu_w, "ssm_glu_b": ssm_glu_b,
            "w_attn_branch": w_attn_branch, "w_ssm_branch": w_ssm_branch, "w_out": w_out,
            "norm_ffn_g": norm_ffn_g, "w_ffn_in": w_ffn_in, "w_ffn_out": w_ffn_out}


def _fwd_reference(x, norm_mix_g, w_in, gate_bias, q_norm_g, k_norm_g, attn_sinks,
              ssm_lambda_re, ssm_lambda_im, ssm_log_dt, ssm_b_re, ssm_b_im, ssm_c_re, ssm_c_im,
              ssm_d, ssm_glu_w, ssm_glu_b, w_attn_branch, w_ssm_branch, w_out,
              norm_ffn_g, w_ffn_in, w_ffn_out):
    B, L = x.shape[0], x.shape[1]
    for l in range(DEPTH):
        h = rmsnorm(x, norm_mix_g[l])
        z = h @ w_in[l]
        q = z[..., OFF_Q:OFF_K].reshape(B, L, N_Q_HEADS, HEAD_DIM)
        k = z[..., OFF_K:OFF_V].reshape(B, L, N_KV_HEADS, HEAD_DIM)
        v = z[..., OFF_V:OFF_U].reshape(B, L, N_KV_HEADS, HEAD_DIM)
        u = z[..., OFF_U:OFF_G]
        gates = jax.nn.sigmoid(z[..., OFF_G:] + gate_bias[l])
        g_attn = gates[..., :D_MODEL]
        g_ssm = gates[..., D_MODEL:]
        y_attn = sliding_window_attention(q, k, v, q_norm_g[l], k_norm_g[l], attn_sinks[l]).astype(x.dtype)
        y_ssm = s5_ssm(u, ssm_lambda_re[l], ssm_lambda_im[l], ssm_log_dt[l], ssm_b_re[l], ssm_b_im[l],
                       ssm_c_re[l], ssm_c_im[l], ssm_d[l])
        y_ssm = jax.nn.gelu(y_ssm)
        y_ssm = (y_ssm * jax.nn.sigmoid(y_ssm @ ssm_glu_w[l].astype(jnp.float32)
                                        + ssm_glu_b[l].astype(jnp.float32))).astype(x.dtype)
        merged = g_attn * (y_attn @ w_attn_branch[l]) + g_ssm * (y_ssm @ w_ssm_branch[l])
        x = x + merged @ w_out[l]
        h2 = rmsnorm(x, norm_ffn_g[l])
        gu = h2 @ w_ffn_in[l]
        x = x + (jax.nn.silu(gu[..., :D_FF]) * gu[..., D_FF:]) @ w_ffn_out[l]
    return x


import jax as _jax
import jax.numpy as _jnp

TWIN_FORMAT = 'train_step'
FWD_PARAMS = ['x', 'norm_mix_g', 'w_in', 'gate_bias', 'q_norm_g', 'k_norm_g', 'attn_sinks', 'ssm_lambda_re', 'ssm_lambda_im', 'ssm_log_dt', 'ssm_b_re', 'ssm_b_im', 'ssm_c_re', 'ssm_c_im', 'ssm_d', 'ssm_glu_w', 'ssm_glu_b', 'w_attn_branch', 'w_ssm_branch', 'w_out', 'norm_ffn_g', 'w_ffn_in', 'w_ffn_out']
TWIN_WEIGHTS = ['norm_mix_g', 'w_in', 'gate_bias', 'q_norm_g', 'k_norm_g', 'attn_sinks', 'ssm_lambda_re', 'ssm_lambda_im', 'ssm_log_dt', 'ssm_b_re', 'ssm_b_im', 'ssm_c_re', 'ssm_c_im', 'ssm_d', 'ssm_glu_w', 'ssm_glu_b', 'w_attn_branch', 'w_ssm_branch', 'w_out', 'norm_ffn_g', 'w_ffn_in', 'w_ffn_out']
TWIN_DIFF_INPUT = 'x'
TWIN_INPUTS = ['x', 'norm_mix_g', 'w_in', 'gate_bias', 'q_norm_g', 'k_norm_g', 'attn_sinks', 'ssm_lambda_re', 'ssm_lambda_im', 'ssm_log_dt', 'ssm_b_re', 'ssm_b_im', 'ssm_c_re', 'ssm_c_im', 'ssm_d', 'ssm_glu_w', 'ssm_glu_b', 'w_attn_branch', 'w_ssm_branch', 'w_out', 'norm_ffn_g', 'w_ffn_in', 'w_ffn_out', 'loss_target', 'm_norm_mix_g', 'm_w_in', 'm_gate_bias', 'm_q_norm_g', 'm_k_norm_g', 'm_attn_sinks', 'm_ssm_lambda_re', 'm_ssm_lambda_im', 'm_ssm_log_dt', 'm_ssm_b_re', 'm_ssm_b_im', 'm_ssm_c_re', 'm_ssm_c_im', 'm_ssm_d', 'm_ssm_glu_w', 'm_ssm_glu_b', 'm_w_attn_branch', 'm_w_ssm_branch', 'm_w_out', 'm_norm_ffn_g', 'm_w_ffn_in', 'm_w_ffn_out', 'v_norm_mix_g', 'v_w_in', 'v_gate_bias', 'v_q_norm_g', 'v_k_norm_g', 'v_attn_sinks', 'v_ssm_lambda_re', 'v_ssm_lambda_im', 'v_ssm_log_dt', 'v_ssm_b_re', 'v_ssm_b_im', 'v_ssm_c_re', 'v_ssm_c_im', 'v_ssm_d', 'v_ssm_glu_w', 'v_ssm_glu_b', 'v_w_attn_branch', 'v_w_ssm_branch', 'v_w_out', 'v_norm_ffn_g', 'v_w_ffn_in', 'v_w_ffn_out']
TWIN_OUTPUTS = ['loss', 'grad_x', 'grad_norm_mix_g', 'grad_w_in', 'grad_gate_bias', 'grad_q_norm_g', 'grad_k_norm_g', 'grad_attn_sinks', 'grad_ssm_lambda_re', 'grad_ssm_lambda_im', 'grad_ssm_log_dt', 'grad_ssm_b_re', 'grad_ssm_b_im', 'grad_ssm_c_re', 'grad_ssm_c_im', 'grad_ssm_d', 'grad_ssm_glu_w', 'grad_ssm_glu_b', 'grad_w_attn_branch', 'grad_w_ssm_branch', 'grad_w_out', 'grad_norm_ffn_g', 'grad_w_ffn_in', 'grad_w_ffn_out', 'delta_norm_mix_g', 'delta_w_in', 'delta_gate_bias', 'delta_q_norm_g', 'delta_k_norm_g', 'delta_attn_sinks', 'delta_ssm_lambda_re', 'delta_ssm_lambda_im', 'delta_ssm_log_dt', 'delta_ssm_b_re', 'delta_ssm_b_im', 'delta_ssm_c_re', 'delta_ssm_c_im', 'delta_ssm_d', 'delta_ssm_glu_w', 'delta_ssm_glu_b', 'delta_w_attn_branch', 'delta_w_ssm_branch', 'delta_w_out', 'delta_norm_ffn_g', 'delta_w_ffn_in', 'delta_w_ffn_out', 'new_m_norm_mix_g', 'new_m_w_in', 'new_m_gate_bias', 'new_m_q_norm_g', 'new_m_k_norm_g', 'new_m_attn_sinks', 'new_m_ssm_lambda_re', 'new_m_ssm_lambda_im', 'new_m_ssm_log_dt', 'new_m_ssm_b_re', 'new_m_ssm_b_im', 'new_m_ssm_c_re', 'new_m_ssm_c_im', 'new_m_ssm_d', 'new_m_ssm_glu_w', 'new_m_ssm_glu_b', 'new_m_w_attn_branch', 'new_m_w_ssm_branch', 'new_m_w_out', 'new_m_norm_ffn_g', 'new_m_w_ffn_in', 'new_m_w_ffn_out', 'new_v_norm_mix_g', 'new_v_w_in', 'new_v_gate_bias', 'new_v_q_norm_g', 'new_v_k_norm_g', 'new_v_attn_sinks', 'new_v_ssm_lambda_re', 'new_v_ssm_lambda_im', 'new_v_ssm_log_dt', 'new_v_ssm_b_re', 'new_v_ssm_b_im', 'new_v_ssm_c_re', 'new_v_ssm_c_im', 'new_v_ssm_d', 'new_v_ssm_glu_w', 'new_v_ssm_glu_b', 'new_v_w_attn_branch', 'new_v_w_ssm_branch', 'new_v_w_out', 'new_v_norm_ffn_g', 'new_v_w_ffn_in', 'new_v_w_ffn_out']
TWIN_LEAF_KINDS = {'loss': 'loss', 'grad_x': 'grad_x', 'grad_norm_mix_g': 'grad_w', 'grad_w_in': 'grad_w', 'grad_gate_bias': 'grad_w', 'grad_q_norm_g': 'grad_w', 'grad_k_norm_g': 'grad_w', 'grad_attn_sinks': 'grad_w', 'grad_ssm_lambda_re': 'grad_w', 'grad_ssm_lambda_im': 'grad_w', 'grad_ssm_log_dt': 'grad_w', 'grad_ssm_b_re': 'grad_w', 'grad_ssm_b_im': 'grad_w', 'grad_ssm_c_re': 'grad_w', 'grad_ssm_c_im': 'grad_w', 'grad_ssm_d': 'grad_w', 'grad_ssm_glu_w': 'grad_w', 'grad_ssm_glu_b': 'grad_w', 'grad_w_attn_branch': 'grad_w', 'grad_w_ssm_branch': 'grad_w', 'grad_w_out': 'grad_w', 'grad_norm_ffn_g': 'grad_w', 'grad_w_ffn_in': 'grad_w', 'grad_w_ffn_out': 'grad_w', 'delta_norm_mix_g': 'delta_w', 'delta_w_in': 'delta_w', 'delta_gate_bias': 'delta_w', 'delta_q_norm_g': 'delta_w', 'delta_k_norm_g': 'delta_w', 'delta_attn_sinks': 'delta_w', 'delta_ssm_lambda_re': 'delta_w', 'delta_ssm_lambda_im': 'delta_w', 'delta_ssm_log_dt': 'delta_w', 'delta_ssm_b_re': 'delta_w', 'delta_ssm_b_im': 'delta_w', 'delta_ssm_c_re': 'delta_w', 'delta_ssm_c_im': 'delta_w', 'delta_ssm_d': 'delta_w', 'delta_ssm_glu_w': 'delta_w', 'delta_ssm_glu_b': 'delta_w', 'delta_w_attn_branch': 'delta_w', 'delta_w_ssm_branch': 'delta_w', 'delta_w_out': 'delta_w', 'delta_norm_ffn_g': 'delta_w', 'delta_w_ffn_in': 'delta_w', 'delta_w_ffn_out': 'delta_w', 'new_m_norm_mix_g': 'new_m', 'new_m_w_in': 'new_m', 'new_m_gate_bias': 'new_m', 'new_m_q_norm_g': 'new_m', 'new_m_k_norm_g': 'new_m', 'new_m_attn_sinks': 'new_m', 'new_m_ssm_lambda_re': 'new_m', 'new_m_ssm_lambda_im': 'new_m', 'new_m_ssm_log_dt': 'new_m', 'new_m_ssm_b_re': 'new_m', 'new_m_ssm_b_im': 'new_m', 'new_m_ssm_c_re': 'new_m', 'new_m_ssm_c_im': 'new_m', 'new_m_ssm_d': 'new_m', 'new_m_ssm_glu_w': 'new_m', 'new_m_ssm_glu_b': 'new_m', 'new_m_w_attn_branch': 'new_m', 'new_m_w_ssm_branch': 'new_m', 'new_m_w_out': 'new_m', 'new_m_norm_ffn_g': 'new_m', 'new_m_w_ffn_in': 'new_m', 'new_m_w_ffn_out': 'new_m', 'new_v_norm_mix_g': 'new_v', 'new_v_w_in': 'new_v', 'new_v_gate_bias': 'new_v', 'new_v_q_norm_g': 'new_v', 'new_v_k_norm_g': 'new_v', 'new_v_attn_sinks': 'new_v', 'new_v_ssm_lambda_re': 'new_v', 'new_v_ssm_lambda_im': 'new_v', 'new_v_ssm_log_dt': 'new_v', 'new_v_ssm_b_re': 'new_v', 'new_v_ssm_b_im': 'new_v', 'new_v_ssm_c_re': 'new_v', 'new_v_ssm_c_im': 'new_v', 'new_v_ssm_d': 'new_v', 'new_v_ssm_glu_w': 'new_v', 'new_v_ssm_glu_b': 'new_v', 'new_v_w_attn_branch': 'new_v', 'new_v_w_ssm_branch': 'new_v', 'new_v_w_out': 'new_v', 'new_v_norm_ffn_g': 'new_v', 'new_v_w_ffn_in': 'new_v', 'new_v_w_ffn_out': 'new_v'}


def _forward(args):
    return _fwd_reference(*[args[k] for k in FWD_PARAMS])


def _output_shape():
    def fwd():
        inp = _fwd_setup_inputs(0)
        return _fwd_reference(*[inp[k] for k in FWD_PARAMS])
    out = _jax.eval_shape(fwd)
    return out.shape, out.dtype

N_MICROBATCH = 1
ADAM_LR = 0.001
ADAM_B1 = 0.9
ADAM_B2 = 0.999
ADAM_EPS = 1e-08
ADAM_WD = 0.01
ADAM_STEP = 10
PER_EXAMPLE_BATCH_AXIS = {'x': 0, 'loss_target': 0}
SHARED_INPUTS = []
_WEIGHT_DTYPES = {'norm_mix_g': _jnp.float32, 'w_in': _jnp.float32, 'gate_bias': _jnp.float32, 'q_norm_g': _jnp.float32, 'k_norm_g': _jnp.float32, 'attn_sinks': _jnp.float32, 'ssm_lambda_re': _jnp.float32, 'ssm_lambda_im': _jnp.float32, 'ssm_log_dt': _jnp.float32, 'ssm_b_re': _jnp.float32, 'ssm_b_im': _jnp.float32, 'ssm_c_re': _jnp.float32, 'ssm_c_im': _jnp.float32, 'ssm_d': _jnp.float32, 'ssm_glu_w': _jnp.float32, 'ssm_glu_b': _jnp.float32, 'w_attn_branch': _jnp.float32, 'w_ssm_branch': _jnp.float32, 'w_out': _jnp.float32, 'norm_ffn_g': _jnp.float32, 'w_ffn_in': _jnp.float32, 'w_ffn_out': _jnp.float32}
MOMENT_SCALE = {'norm_mix_g': 1.216283e+00, 'w_in': 7.749238e-02, 'gate_bias': 2.128820e-01, 'q_norm_g': 5.925701e+00, 'k_norm_g': 5.909889e+00, 'attn_sinks': 1.072151e+01, 'ssm_lambda_re': 7.166769e-03, 'ssm_lambda_im': 8.239343e-03, 'ssm_log_dt': 1.928327e+00, 'ssm_b_re': 5.513539e-03, 'ssm_b_im': 5.116492e-03, 'ssm_c_re': 9.958151e-03, 'ssm_c_im': 1.025711e-02, 'ssm_d': 1.597933e+00, 'ssm_glu_w': 3.180208e-01, 'ssm_glu_b': 9.288096e-01, 'w_attn_branch': 7.835253e-02, 'w_ssm_branch': 3.475467e-01, 'w_out': 3.162376e-01, 'norm_ffn_g': 1.234338e+01, 'w_ffn_in': 1.196781e-01, 'w_ffn_out': 1.447811e-01}


def _to_microbatches(a, axis):
    t = _jnp.moveaxis(a, axis, 0)
    t = t.reshape((N_MICROBATCH, t.shape[0] // N_MICROBATCH) + t.shape[1:])
    return _jnp.moveaxis(t, 1, axis + 1)


def setup_inputs(seed: int = 0) -> dict:
    inp = _fwd_setup_inputs(seed)
    key = _jax.random.fold_in(_jax.random.key(seed), 7919)
    shape, _ = _output_shape()
    out = dict(inp)
    out["loss_target"] = _jax.random.normal(_jax.random.fold_in(key, 0), shape, _jnp.float32)
    for i, name in enumerate(TWIN_WEIGHTS):
        w = inp[name].astype(_jnp.float32)
        if MOMENT_SCALE is None:
            s = _jnp.sqrt(_jnp.mean(_jnp.square(w)) + 1e-30)
        else:
            s = MOMENT_SCALE[name]
        km, kv = _jax.random.split(_jax.random.fold_in(key, i + 1))
        out[name] = w
        out["m_" + name] = s * _jax.random.normal(km, w.shape, _jnp.float32)
        out["v_" + name] = (s * s) * _jax.random.uniform(kv, w.shape, _jnp.float32, 0.5, 1.5)
    if N_MICROBATCH > 1:
        for name, axis in PER_EXAMPLE_BATCH_AXIS.items():
            out[name] = _to_microbatches(out[name], axis)
    return {'x': out['x'], 'norm_mix_g': out['norm_mix_g'], 'w_in': out['w_in'], 'gate_bias': out['gate_bias'], 'q_norm_g': out['q_norm_g'], 'k_norm_g': out['k_norm_g'], 'attn_sinks': out['attn_sinks'], 'ssm_lambda_re': out['ssm_lambda_re'], 'ssm_lambda_im': out['ssm_lambda_im'], 'ssm_log_dt': out['ssm_log_dt'], 'ssm_b_re': out['ssm_b_re'], 'ssm_b_im': out['ssm_b_im'], 'ssm_c_re': out['ssm_c_re'], 'ssm_c_im': out['ssm_c_im'], 'ssm_d': out['ssm_d'], 'ssm_glu_w': out['ssm_glu_w'], 'ssm_glu_b': out['ssm_glu_b'], 'w_attn_branch': out['w_attn_branch'], 'w_ssm_branch': out['w_ssm_branch'], 'w_out': out['w_out'], 'norm_ffn_g': out['norm_ffn_g'], 'w_ffn_in': out['w_ffn_in'], 'w_ffn_out': out['w_ffn_out'], 'loss_target': out['loss_target'], 'm_norm_mix_g': out['m_norm_mix_g'], 'm_w_in': out['m_w_in'], 'm_gate_bias': out['m_gate_bias'], 'm_q_norm_g': out['m_q_norm_g'], 'm_k_norm_g': out['m_k_norm_g'], 'm_attn_sinks': out['m_attn_sinks'], 'm_ssm_lambda_re': out['m_ssm_lambda_re'], 'm_ssm_lambda_im': out['m_ssm_lambda_im'], 'm_ssm_log_dt': out['m_ssm_log_dt'], 'm_ssm_b_re': out['m_ssm_b_re'], 'm_ssm_b_im': out['m_ssm_b_im'], 'm_ssm_c_re': out['m_ssm_c_re'], 'm_ssm_c_im': out['m_ssm_c_im'], 'm_ssm_d': out['m_ssm_d'], 'm_ssm_glu_w': out['m_ssm_glu_w'], 'm_ssm_glu_b': out['m_ssm_glu_b'], 'm_w_attn_branch': out['m_w_attn_branch'], 'm_w_ssm_branch': out['m_w_ssm_branch'], 'm_w_out': out['m_w_out'], 'm_norm_ffn_g': out['m_norm_ffn_g'], 'm_w_ffn_in': out['m_w_ffn_in'], 'm_w_ffn_out': out['m_w_ffn_out'], 'v_norm_mix_g': out['v_norm_mix_g'], 'v_w_in': out['v_w_in'], 'v_gate_bias': out['v_gate_bias'], 'v_q_norm_g': out['v_q_norm_g'], 'v_k_norm_g': out['v_k_norm_g'], 'v_attn_sinks': out['v_attn_sinks'], 'v_ssm_lambda_re': out['v_ssm_lambda_re'], 'v_ssm_lambda_im': out['v_ssm_lambda_im'], 'v_ssm_log_dt': out['v_ssm_log_dt'], 'v_ssm_b_re': out['v_ssm_b_re'], 'v_ssm_b_im': out['v_ssm_b_im'], 'v_ssm_c_re': out['v_ssm_c_re'], 'v_ssm_c_im': out['v_ssm_c_im'], 'v_ssm_d': out['v_ssm_d'], 'v_ssm_glu_w': out['v_ssm_glu_w'], 'v_ssm_glu_b': out['v_ssm_glu_b'], 'v_w_attn_branch': out['v_w_attn_branch'], 'v_w_ssm_branch': out['v_w_ssm_branch'], 'v_w_out': out['v_w_out'], 'v_norm_ffn_g': out['v_norm_ffn_g'], 'v_w_ffn_in': out['v_w_ffn_in'], 'v_w_ffn_out': out['v_w_ffn_out']}


def _loss(weights, diff, rest, loss_target):
    with _jax.named_scope("forward"):
        args = {**rest, TWIN_DIFF_INPUT: diff, **{k: w.astype(_WEIGHT_DTYPES[k]) for k, w in weights.items()}}
        y = _forward(args)
    with _jax.named_scope("loss_head"):
        err = _jnp.square(y.astype(_jnp.float32) - loss_target)
        return 0.5 * _jnp.sum(_jnp.mean(err, axis=-1)) if err.ndim else 0.5 * err


def _adamw(w, g, m, v):
    m = ADAM_B1 * m + (1.0 - ADAM_B1) * g
    v = ADAM_B2 * v + (1.0 - ADAM_B2) * _jnp.square(g)
    m_hat = m / (1.0 - ADAM_B1 ** ADAM_STEP)
    v_hat = v / (1.0 - ADAM_B2 ** ADAM_STEP)
    delta = -ADAM_LR * (m_hat / (_jnp.sqrt(v_hat) + ADAM_EPS) + ADAM_WD * w)
    return delta, m, v


def reference(x, norm_mix_g, w_in, gate_bias, q_norm_g, k_norm_g, attn_sinks, ssm_lambda_re, ssm_lambda_im, ssm_log_dt, ssm_b_re, ssm_b_im, ssm_c_re, ssm_c_im, ssm_d, ssm_glu_w, ssm_glu_b, w_attn_branch, w_ssm_branch, w_out, norm_ffn_g, w_ffn_in, w_ffn_out, loss_target, m_norm_mix_g, m_w_in, m_gate_bias, m_q_norm_g, m_k_norm_g, m_attn_sinks, m_ssm_lambda_re, m_ssm_lambda_im, m_ssm_log_dt, m_ssm_b_re, m_ssm_b_im, m_ssm_c_re, m_ssm_c_im, m_ssm_d, m_ssm_glu_w, m_ssm_glu_b, m_w_attn_branch, m_w_ssm_branch, m_w_out, m_norm_ffn_g, m_w_ffn_in, m_w_ffn_out, v_norm_mix_g, v_w_in, v_gate_bias, v_q_norm_g, v_k_norm_g, v_attn_sinks, v_ssm_lambda_re, v_ssm_lambda_im, v_ssm_log_dt, v_ssm_b_re, v_ssm_b_im, v_ssm_c_re, v_ssm_c_im, v_ssm_d, v_ssm_glu_w, v_ssm_glu_b, v_w_attn_branch, v_w_ssm_branch, v_w_out, v_norm_ffn_g, v_w_ffn_in, v_w_ffn_out):
    given = dict(x=x, norm_mix_g=norm_mix_g, w_in=w_in, gate_bias=gate_bias, q_norm_g=q_norm_g, k_norm_g=k_norm_g, attn_sinks=attn_sinks, ssm_lambda_re=ssm_lambda_re, ssm_lambda_im=ssm_lambda_im, ssm_log_dt=ssm_log_dt, ssm_b_re=ssm_b_re, ssm_b_im=ssm_b_im, ssm_c_re=ssm_c_re, ssm_c_im=ssm_c_im, ssm_d=ssm_d, ssm_glu_w=ssm_glu_w, ssm_glu_b=ssm_glu_b, w_attn_branch=w_attn_branch, w_ssm_branch=w_ssm_branch, w_out=w_out, norm_ffn_g=norm_ffn_g, w_ffn_in=w_ffn_in, w_ffn_out=w_ffn_out, loss_target=loss_target, m_norm_mix_g=m_norm_mix_g, m_w_in=m_w_in, m_gate_bias=m_gate_bias, m_q_norm_g=m_q_norm_g, m_k_norm_g=m_k_norm_g, m_attn_sinks=m_attn_sinks, m_ssm_lambda_re=m_ssm_lambda_re, m_ssm_lambda_im=m_ssm_lambda_im, m_ssm_log_dt=m_ssm_log_dt, m_ssm_b_re=m_ssm_b_re, m_ssm_b_im=m_ssm_b_im, m_ssm_c_re=m_ssm_c_re, m_ssm_c_im=m_ssm_c_im, m_ssm_d=m_ssm_d, m_ssm_glu_w=m_ssm_glu_w, m_ssm_glu_b=m_ssm_glu_b, m_w_attn_branch=m_w_attn_branch, m_w_ssm_branch=m_w_ssm_branch, m_w_out=m_w_out, m_norm_ffn_g=m_norm_ffn_g, m_w_ffn_in=m_w_ffn_in, m_w_ffn_out=m_w_ffn_out, v_norm_mix_g=v_norm_mix_g, v_w_in=v_w_in, v_gate_bias=v_gate_bias, v_q_norm_g=v_q_norm_g, v_k_norm_g=v_k_norm_g, v_attn_sinks=v_attn_sinks, v_ssm_lambda_re=v_ssm_lambda_re, v_ssm_lambda_im=v_ssm_lambda_im, v_ssm_log_dt=v_ssm_log_dt, v_ssm_b_re=v_ssm_b_re, v_ssm_b_im=v_ssm_b_im, v_ssm_c_re=v_ssm_c_re, v_ssm_c_im=v_ssm_c_im, v_ssm_d=v_ssm_d, v_ssm_glu_w=v_ssm_glu_w, v_ssm_glu_b=v_ssm_glu_b, v_w_attn_branch=v_w_attn_branch, v_w_ssm_branch=v_w_ssm_branch, v_w_out=v_w_out, v_norm_ffn_g=v_norm_ffn_g, v_w_ffn_in=v_w_ffn_in, v_w_ffn_out=v_w_ffn_out)
    weights = {n: given[n] for n in TWIN_WEIGHTS}
    shared = {n: given[n] for n in SHARED_INPUTS}
    per_example = {n: given[n] for n in ['x']}
    grad_fn = _jax.value_and_grad(_loss, argnums=(0, 1))

    def one_microbatch(ex, loss_target):
        ex = dict(ex)
        diff = ex.pop(TWIN_DIFF_INPUT)
        return grad_fn(weights, diff, {**shared, **ex}, loss_target)

    if N_MICROBATCH == 1:
        loss, (grad_w, grad_x) = one_microbatch(per_example, given["loss_target"])
    else:
        def body(carry, xs):
            loss_sum, grad_sum = carry
            l_k, (gw_k, gx_k) = one_microbatch(xs[0], xs[1])
            with _jax.named_scope("update"):
                return (loss_sum + l_k, _jax.tree.map(_jnp.add, grad_sum, gw_k)), gx_k

        init = (_jnp.zeros((), _jnp.float32), _jax.tree.map(_jnp.zeros_like, weights))
        (loss, grad_w), grad_x = _jax.lax.scan(body, init, (per_example, given["loss_target"]))
    with _jax.named_scope("update"):
        delta_w, new_m, new_v = {}, {}, {}
        for n in TWIN_WEIGHTS:
            delta_w[n], new_m[n], new_v[n] = _adamw(weights[n], grad_w[n], given["m_" + n], given["v_" + n])
    return (loss, grad_x, *[grad_w[n] for n in TWIN_WEIGHTS], *[delta_w[n] for n in TWIN_WEIGHTS],
            *[new_m[n] for n in TWIN_WEIGHTS], *[new_v[n] for n in TWIN_WEIGHTS])
```

```python
import functools
import math

import jax
import jax.numpy as jnp
from jax import lax
from jax.experimental import pallas as pl
from jax.experimental.pallas import tpu as pltpu

F32, BF16 = jnp.float32, jnp.bfloat16
MESH = pl.DeviceIdType.MESH

D_MODEL = 2048
HEAD_DIM = 64
N_Q_HEADS = 16
N_KV_HEADS = 4
GQA_GROUP = N_Q_HEADS // N_KV_HEADS
ATTN_WIDTH = N_Q_HEADS * HEAD_DIM
KV_WIDTH = N_KV_HEADS * HEAD_DIM
WINDOW = 128
BLOCK = 128
SSM_WIDTH = D_MODEL // 2
SSM_GROUP_CH = 16
SSM_GROUPS = SSM_WIDTH // SSM_GROUP_CH
SSM_STATE = 64
D_FF = 5632
OFF_K = ATTN_WIDTH
OFF_V = OFF_K + KV_WIDTH
OFF_U = OFF_V + KV_WIDTH
OFF_G = OFF_U + SSM_WIDTH
IN_WIDTH = OFF_G + 2 * D_MODEL
RMS_EPS = 1e-6
ATTN_SCALE = HEAD_DIM ** -0.5
NEG_BIG = -1e30

SSM_NGB = 4
SSM_GB_CH = SSM_WIDTH // SSM_NGB
SSM_GB_ST = SSM_GROUPS * SSM_STATE // SSM_NGB
SUBLANES = 8
LANES = 128
SSM_TT = 256

ADAM_LR = 0.001
ADAM_B1 = 0.9
ADAM_B2 = 0.999
ADAM_EPS = 1e-08
ADAM_WD = 0.01
ADAM_STEP = 10

N_DEV = 8
VMEM_LIMIT_BYTES = 52 * 1024 * 1024

BIG = ("w_in", "ssm_glu_w", "w_attn_branch", "w_ssm_branch", "w_out", "w_ffn_in", "w_ffn_out")
COL_SHARDED = ("w_in", "w_attn_branch", "w_ssm_branch", "w_ffn_in")
SMALL = ("norm_mix_g", "gate_bias", "q_norm_g", "k_norm_g", "attn_sinks", "ssm_lambda_re", "ssm_lambda_im",
         "ssm_log_dt", "ssm_b_re", "ssm_b_im", "ssm_c_re", "ssm_c_im", "ssm_d", "ssm_glu_b", "norm_ffn_g")
WEIGHTS = ("norm_mix_g", "w_in", "gate_bias", "q_norm_g", "k_norm_g", "attn_sinks", "ssm_lambda_re", "ssm_lambda_im",
           "ssm_log_dt", "ssm_b_re", "ssm_b_im", "ssm_c_re", "ssm_c_im", "ssm_d", "ssm_glu_w", "ssm_glu_b",
           "w_attn_branch", "w_ssm_branch", "w_out", "norm_ffn_g", "w_ffn_in", "w_ffn_out")


def _pick(n, target, mult=LANES):
    best = None
    for t in range(mult, min(n, target) + 1, mult):
        if n % t == 0:
            best = t
    return n if best is None else best


def _params(*sem):
    return pltpu.CompilerParams(dimension_semantics=sem, vmem_limit_bytes=VMEM_LIMIT_BYTES)


def _sigmoid(v):
    return 1.0 / (1.0 + jnp.exp(-v))


_GELU_C = math.sqrt(2.0 / math.pi)


def _gelu(v):
    return 0.5 * v * (1.0 + jnp.tanh(_GELU_C * (v + 0.044715 * v * v * v)))


def _gelu_grad(v):
    t = jnp.tanh(_GELU_C * (v + 0.044715 * v * v * v))
    return 0.5 * (1.0 + t) + 0.5 * v * (1.0 - t * t) * _GELU_C * (1.0 + 3.0 * 0.044715 * v * v)


_DN = {"nn": (((1,), (0,)), ((), ())), "nt": (((1,), (1,)), ((), ())), "tn": (((0,), (0,)), ((), ()))}


def _dot(a, b, dims="nn"):
    return lax.dot_general(a, b, _DN[dims], preferred_element_type=F32)


def _mm(a, b, dims, out_dtype, name, residual=None, tm=1024, tn=1024, tk=1024):
    if dims == "tn":
        K, M = a.shape
    else:
        M, K = a.shape
    N = b.shape[0] if dims == "nt" else b.shape[1]
    tm, tn, tk = _pick(M, tm), _pick(N, tn), _pick(K, tk)
    nk = K // tk
    has_res = residual is not None

    def body(*refs):
        a_ref, b_ref = refs[0], refs[1]
        o_ref, acc_ref = refs[-2], refs[-1]
        k = pl.program_id(2)
        part = _dot(a_ref[...].astype(BF16), b_ref[...].astype(BF16), dims)

        @pl.when(k == 0)
        def _():
            acc_ref[...] = part

        @pl.when(k > 0)
        def _():
            acc_ref[...] += part

        @pl.when(k == nk - 1)
        def _():
            out = acc_ref[...]
            if has_res:
                out = out + refs[2][...].astype(F32)
            o_ref[...] = out.astype(out_dtype)

    a_spec = (pl.BlockSpec((tk, tm), lambda i, j, k: (k, i)) if dims == "tn"
              else pl.BlockSpec((tm, tk), lambda i, j, k: (i, k)))
    b_spec = (pl.BlockSpec((tn, tk), lambda i, j, k: (j, k)) if dims == "nt"
              else pl.BlockSpec((tk, tn), lambda i, j, k: (k, j)))
    o_spec = pl.BlockSpec((tm, tn), lambda i, j, k: (i, j))
    in_specs = [a_spec, b_spec] + ([o_spec] if has_res else [])
    args = (a, b) + ((residual,) if has_res else ())
    return pl.pallas_call(
        body, name=name, grid=(M // tm, N // tn, nk), in_specs=in_specs, out_specs=o_spec,
        out_shape=jax.ShapeDtypeStruct((M, N), out_dtype), scratch_shapes=[pltpu.VMEM((tm, tn), F32)],
        compiler_params=_params("parallel", "parallel", "arbitrary"))(*args)


def _rms_fwd(x, g, name):
    T, D = x.shape
    tr = _pick(T, 256, SUBLANES)

    def body(x_ref, g_ref, o_ref):
        xf = x_ref[...]
        r = lax.rsqrt(jnp.mean(xf * xf, axis=-1, keepdims=True) + RMS_EPS)
        o_ref[...] = (xf * r * g_ref[...]).astype(BF16)

    return pl.pallas_call(
        body, name=name, grid=(T // tr,),
        in_specs=[pl.BlockSpec((tr, D), lambda i: (i, 0)), pl.BlockSpec((1, D), lambda i: (0, 0))],
        out_specs=pl.BlockSpec((tr, D), lambda i: (i, 0)), out_shape=jax.ShapeDtypeStruct((T, D), BF16),
        compiler_params=_params("parallel"))(x, g)


def _rms_bwd(dh, x, g, dres, name):
    T, D = x.shape
    tr = _pick(T, 256, SUBLANES)

    def body(dh_ref, x_ref, g_ref, dres_ref, dx_ref, dg_ref):
        @pl.when(pl.program_id(0) == 0)
        def _():
            dg_ref[...] = jnp.zeros_like(dg_ref)

        xf = x_ref[...]
        r = lax.rsqrt(jnp.mean(xf * xf, axis=-1, keepdims=True) + RMS_EPS)
        xhat = xf * r
        dhv = dh_ref[...]
        dxh = dhv * g_ref[...]
        dx_ref[...] = dres_ref[...] + r * (dxh - xhat * jnp.mean(dxh * xhat, axis=-1, keepdims=True))
        dg_ref[...] += jnp.sum(dhv * xhat, axis=0, keepdims=True)

    row = pl.BlockSpec((tr, D), lambda i: (i, 0))
    vec = pl.BlockSpec((1, D), lambda i: (0, 0))
    return pl.pallas_call(
        body, name=name, grid=(T // tr,), in_specs=[row, row, vec, row], out_specs=[row, vec],
        out_shape=[jax.ShapeDtypeStruct((T, D), F32), jax.ShapeDtypeStruct((1, D), F32)],
        compiler_params=_params("arbitrary"))(dh, x, g, dres)


def _loss_grad(y, target, name):
    T, D = y.shape
    tr = _pick(T, 256, SUBLANES)

    def body(y_ref, t_ref, dx_ref, loss_ref):
        @pl.when(pl.program_id(0) == 0)
        def _():
            loss_ref[...] = jnp.zeros_like(loss_ref)

        err = y_ref[...] - t_ref[...]
        dx_ref[...] = err * (1.0 / D)
        loss_ref[...] += jnp.sum(jnp.mean(err * err, axis=-1, keepdims=True), axis=0, keepdims=True) * 0.5

    row = pl.BlockSpec((tr, D), lambda i: (i, 0))
    one = pl.BlockSpec((1, 1), lambda i: (0, 0))
    dx, loss = pl.pallas_call(
        body, name=name, grid=(T // tr,), in_specs=[row, row], out_specs=[row, one],
        out_shape=[jax.ShapeDtypeStruct((T, D), F32), jax.ShapeDtypeStruct((1, 1), F32)],
        compiler_params=_params("arbitrary"))(y, target)
    return dx, loss[0, 0]


def _alibi_slope(h):
    return 2.0 ** (-8.0 * (h + 1) / N_Q_HEADS)


def _attn_mask():
    row = lax.broadcasted_iota(jnp.int32, (BLOCK, 2 * BLOCK), 0)
    col = lax.broadcasted_iota(jnp.int32, (BLOCK, 2 * BLOCK), 1)
    dist = row - col + BLOCK
    return dist.astype(F32), (dist >= 0) & (dist < WINDOW), col


def _head_norm(v, gain):
    r = lax.rsqrt(jnp.mean(v * v, axis=-1, keepdims=True) + RMS_EPS)
    vhat = v * r
    return r, vhat, vhat * gain


def _attn_probs(qn16, kn16, slope, dist, valid, sink):
    s = _dot(qn16, kn16, "nt") * ATTN_SCALE - slope * dist
    s = jnp.where(valid, s, NEG_BIG)
    m = jnp.maximum(jnp.max(s, axis=-1, keepdims=True), sink)
    p = jnp.exp(s - m)
    ps = jnp.exp(sink - m)
    den = jnp.sum(p, axis=-1, keepdims=True) + ps
    return p, ps, den


def _attn_specs(T):
    q_spec = pl.BlockSpec((GQA_GROUP, BLOCK, HEAD_DIM), lambda h, n: (h, n, 0))
    cur = pl.BlockSpec((None, BLOCK, HEAD_DIM), lambda h, n: (h, n, 0))
    prev = pl.BlockSpec((None, BLOCK, HEAD_DIM), lambda h, n: (h, jnp.maximum(n - 1, 0), 0))
    gain = pl.BlockSpec((1, HEAD_DIM), lambda h, n: (0, 0))
    sink = pl.BlockSpec((None, GQA_GROUP, LANES), lambda h, n: (h, 0, 0))
    return q_spec, cur, prev, gain, sink


def _sink_rows(sinks):
    return jnp.broadcast_to(sinks.reshape(N_KV_HEADS, GQA_GROUP, 1), (N_KV_HEADS, GQA_GROUP, LANES))


def _attn_fwd(q, k, v, qg, kg, sinks, name):
    T = q.shape[1]
    nb = T // BLOCK

    def body(q_ref, kc_ref, kp_ref, vc_ref, vp_ref, qg_ref, kg_ref, sk_ref, o_ref):
        kv = pl.program_id(0)
        n = pl.program_id(1)
        dist, valid, col = _attn_mask()
        valid = valid & ((col >= BLOCK) | (n > 0))
        kk = jnp.concatenate([kp_ref[...], kc_ref[...]], axis=0)
        _, _, kn = _head_norm(kk, kg_ref[...])
        kn16 = kn.astype(BF16)
        v16 = jnp.concatenate([vp_ref[...], vc_ref[...]], axis=0).astype(BF16)
        for g in range(GQA_GROUP):
            slope = sum(jnp.where(kv == j, _alibi_slope(j * GQA_GROUP + g), 0.0) for j in range(N_KV_HEADS))
            _, _, qn = _head_norm(q_ref[g], qg_ref[...])
            sink = sk_ref[g:g + 1, 0:1]
            p, _, den = _attn_probs(qn.astype(BF16), kn16, slope, dist, valid, sink)
            o_ref[g] = (_dot((p / den).astype(BF16), v16) ).astype(BF16)

    q_spec, cur, prev, gain, sink = _attn_specs(T)
    return pl.pallas_call(
        body, name=name, grid=(N_KV_HEADS, nb), in_specs=[q_spec, cur, prev, cur, prev, gain, gain, sink],
        out_specs=q_spec, out_shape=jax.ShapeDtypeStruct((N_Q_HEADS, T, HEAD_DIM), BF16),
        compiler_params=_params("parallel", "parallel"))(q, k, k, v, v, qg, kg, _sink_rows(sinks))


def _attn_bwd(q, k, v, do, qg, kg, sinks, name):
    T = q.shape[1]
    nb = T // BLOCK

    def body(q_ref, kc_ref, kp_ref, vc_ref, vp_ref, do_ref, qg_ref, kg_ref, sk_ref,
             dq_ref, dk_ref, dv_ref, dqg_ref, dkg_ref, dsk_ref):
        kv = pl.program_id(0)
        n = pl.program_id(1)

        @pl.when(n == 0)
        def _():
            dk_ref[...] = jnp.zeros_like(dk_ref)
            dv_ref[...] = jnp.zeros_like(dv_ref)
            dqg_ref[...] = jnp.zeros_like(dqg_ref)
            dkg_ref[...] = jnp.zeros_like(dkg_ref)
            dsk_ref[...] = jnp.zeros_like(dsk_ref)

        dist, valid, col = _attn_mask()
        valid = valid & ((col >= BLOCK) | (n > 0))
        kk = jnp.concatenate([kp_ref[...], kc_ref[...]], axis=0)
        rk, khat, kn = _head_norm(kk, kg_ref[...])
        kn16 = kn.astype(BF16)
        v16 = jnp.concatenate([vp_ref[...], vc_ref[...]], axis=0).astype(BF16)
        dkn = jnp.zeros((2 * BLOCK, HEAD_DIM), F32)
        dv = jnp.zeros((2 * BLOCK, HEAD_DIM), F32)
        dqg = jnp.zeros((1, HEAD_DIM), F32)
        for g in range(GQA_GROUP):
            slope = sum(jnp.where(kv == j, _alibi_slope(j * GQA_GROUP + g), 0.0) for j in range(N_KV_HEADS))
            rq, qhat, qn = _head_norm(q_ref[g], qg_ref[...])
            qn16 = qn.astype(BF16)
            sink = sk_ref[g:g + 1, 0:1]
            p, ps, den = _attn_probs(qn16, kn16, slope, dist, valid, sink)
            pn = p / den
            do16 = do_ref[g].astype(BF16)
            dp = _dot(do16, v16, "nt")
            delta = jnp.sum(pn * dp, axis=-1, keepdims=True)
            ds16 = (pn * (dp - delta)).astype(BF16)
            dsink = -jnp.sum(ps / den * delta, axis=0, keepdims=True)
            dsk_ref[g:g + 1, :] += jnp.broadcast_to(dsink, (1, LANES))
            dqn = _dot(ds16, kn16) * ATTN_SCALE
            dkn = dkn + _dot(ds16, qn16, "tn") * ATTN_SCALE
            dv = dv + _dot(pn.astype(BF16), do16, "tn")
            dqh = dqn * qg_ref[...]
            dq_ref[g] = (rq * (dqh - qhat * jnp.mean(dqh * qhat, axis=-1, keepdims=True))).astype(BF16)
            dqg = dqg + jnp.sum(dqn * qhat, axis=0, keepdims=True)
        dqg_ref[...] += dqg
        dkg_ref[...] += jnp.sum(dkn * khat, axis=0, keepdims=True)
        dkh = dkn * kg_ref[...]
        dk = rk * (dkh - khat * jnp.mean(dkh * khat, axis=-1, keepdims=True))
        rows = pl.ds(pl.multiple_of(n * BLOCK, BLOCK), BLOCK)
        dk_ref[rows, :] += dk[BLOCK:]
        dv_ref[rows, :] += dv[BLOCK:]

        @pl.when(n > 0)
        def _():
            before = pl.ds(pl.multiple_of((n - 1) * BLOCK, BLOCK), BLOCK)
            dk_ref[before, :] += dk[:BLOCK]
            dv_ref[before, :] += dv[:BLOCK]

    q_spec, cur, prev, gain, sink = _attn_specs(T)
    whole = pl.BlockSpec((None, T, HEAD_DIM), lambda h, n: (h, 0, 0))
    gacc = pl.BlockSpec((None, 1, HEAD_DIM), lambda h, n: (h, 0, 0))
    return pl.pallas_call(
        body, name=name, grid=(N_KV_HEADS, nb), in_specs=[q_spec, cur, prev, cur, prev, q_spec, gain, gain, sink],
        out_specs=[q_spec, whole, whole, gacc, gacc, sink],
        out_shape=[jax.ShapeDtypeStruct((N_Q_HEADS, T, HEAD_DIM), BF16),
                   jax.ShapeDtypeStruct((N_KV_HEADS, T, HEAD_DIM), F32),
                   jax.ShapeDtypeStruct((N_KV_HEADS, T, HEAD_DIM), F32),
                   jax.ShapeDtypeStruct((N_KV_HEADS, 1, HEAD_DIM), F32),
                   jax.ShapeDtypeStruct((N_KV_HEADS, 1, HEAD_DIM), F32),
                   jax.ShapeDtypeStruct((N_KV_HEADS, GQA_GROUP, LANES), F32)],
        compiler_params=_params("parallel", "arbitrary"))(q, k, k, v, v, do, qg, kg, _sink_rows(sinks))


def _ssm_discretize(lam_re, lam_im, log_dt, b_re, b_im, c_re, c_im):
    dt = jnp.exp(log_dt)[:, None]
    mag = jnp.exp(lam_re * dt)
    ar = mag * jnp.cos(lam_im * dt)
    ai = mag * jnp.sin(lam_im * dt)
    den = lam_re * lam_re + lam_im * lam_im
    fr = ((ar - 1.0) * lam_re + ai * lam_im) / den
    fi = (ai * lam_re - (ar - 1.0) * lam_im) / den
    bbar_r = fr[:, :, None] * b_re - fi[:, :, None] * b_im
    bbar_i = fr[:, :, None] * b_im + fi[:, :, None] * b_re
    gl = SSM_GROUPS // SSM_NGB
    eye = jnp.eye(gl, dtype=F32)

    def tiles(a):
        return a.reshape(SSM_NGB, SUBLANES, LANES)

    def bdiag(bb):
        return jnp.einsum("bgph,gk->bghkp", bb.reshape(SSM_NGB, gl, SSM_STATE, SSM_GROUP_CH), eye).reshape(
            SSM_NGB, SSM_GB_CH, SSM_GB_ST)

    def cdiag(cc):
        return jnp.einsum("bghp,gk->bgpkh", cc.reshape(SSM_NGB, gl, SSM_GROUP_CH, SSM_STATE), eye).reshape(
            SSM_NGB, SSM_GB_ST, SSM_GB_CH)

    return tiles(ar), tiles(ai), bdiag(bbar_r), bdiag(bbar_i), cdiag(c_re), cdiag(c_im)


def _to_time_major(dst, val, tt, first_row=0):
    for j in range(SUBLANES):
        dst[pl.ds(first_row + j, tt, stride=SUBLANES), :] = val[:, j * LANES:(j + 1) * LANES]


def _from_time_major(dst, src, tt):
    for j in range(SUBLANES):
        dst[:, j * LANES:(j + 1) * LANES] = src[pl.ds(j, tt, stride=SUBLANES), :]


def _ssm_fwd(z, ar, ai, bbr, bbi, cbr, cbi, dskip, name):
    T = z.shape[0]
    tt = min(SSM_TT, T)
    nt = T // tt

    def body(u_ref, ar_ref, ai_ref, br_ref, bi_ref, cr_ref, ci_ref, d_ref, y_ref, sr_ref, si_ref,
             tmr, tmi, car_r, car_i):
        @pl.when(pl.program_id(1) == 0)
        def _():
            car_r[...] = jnp.zeros_like(car_r)
            car_i[...] = jnp.zeros_like(car_i)

        u = u_ref[...]
        u16 = u.astype(BF16)
        _to_time_major(tmr, _dot(u16, br_ref[...]), tt)
        _to_time_major(tmi, _dot(u16, bi_ref[...]), tt)
        a_r = ar_ref[...]
        a_i = ai_ref[...]

        def step(t, carry):
            s_r, s_i = carry
            rows = pl.ds(pl.multiple_of(t * SUBLANES, SUBLANES), SUBLANES)
            n_r = a_r * s_r - a_i * s_i + tmr[rows, :]
            n_i = a_r * s_i + a_i * s_r + tmi[rows, :]
            tmr[rows, :] = n_r
            tmi[rows, :] = n_i
            return n_r, n_i

        s_r, s_i = lax.fori_loop(0, tt, step, (car_r[...], car_i[...]), unroll=8)
        car_r[...] = s_r
        car_i[...] = s_i
        _from_time_major(sr_ref, tmr, tt)
        _from_time_major(si_ref, tmi, tt)
        y_ref[...] = (_dot(sr_ref[...].astype(BF16), cr_ref[...]) - _dot(si_ref[...].astype(BF16), ci_ref[...])
                      + d_ref[...] * u)

    u_spec = pl.BlockSpec((tt, SSM_GB_CH), lambda b, t: (t, OFF_U // SSM_GB_CH + b))
    a_spec = pl.BlockSpec((None, SUBLANES, LANES), lambda b, t: (b, 0, 0))
    b_spec = pl.BlockSpec((None, SSM_GB_CH, SSM_GB_ST), lambda b, t: (b, 0, 0))
    c_spec = pl.BlockSpec((None, SSM_GB_ST, SSM_GB_CH), lambda b, t: (b, 0, 0))
    d_spec = pl.BlockSpec((1, SSM_GB_CH), lambda b, t: (0, b))
    y_spec = pl.BlockSpec((tt, SSM_GB_CH), lambda b, t: (t, b))
    s_spec = pl.BlockSpec((tt, SSM_GB_ST), lambda b, t: (t, b))
    n_state = SSM_NGB * SSM_GB_ST
    return pl.pallas_call(
        body, name=name, grid=(SSM_NGB, nt),
        in_specs=[u_spec, a_spec, a_spec, b_spec, b_spec, c_spec, c_spec, d_spec],
        out_specs=[y_spec, s_spec, s_spec],
        out_shape=[jax.ShapeDtypeStruct((T, SSM_WIDTH), F32), jax.ShapeDtypeStruct((T, n_state), F32),
                   jax.ShapeDtypeStruct((T, n_state), F32)],
        scratch_shapes=[pltpu.VMEM((tt * SUBLANES, LANES), F32), pltpu.VMEM((tt * SUBLANES, LANES), F32),
                        pltpu.VMEM((SUBLANES, LANES), F32), pltpu.VMEM((SUBLANES, LANES), F32)],
        compiler_params=_params("parallel", "arbitrary"))(z, ar, ai, bbr, bbi, cbr, cbi, dskip)


def _ssm_bwd(dy, z, s_r, s_i, ar, ai, bbr, bbi, cbr, cbi, dskip, name):
    T = z.shape[0]
    tt = min(SSM_TT, T)
    nt = T // tt
    per8 = tt // SUBLANES

    def body(dy_ref, u_ref, sr_ref, si_ref, srp_ref, sip_ref, ar_ref, ai_ref, br_ref, bi_ref, cr_ref, ci_ref, d_ref,
             du_ref, dar_ref, dai_ref, dbr_ref, dbi_ref, dcr_ref, dci_ref, dd_ref,
             tmr, tmi, smr, smi, natr, nati, car_r, car_i):
        tb = pl.program_id(1)
        first_block = tb == nt - 1

        @pl.when(tb == 0)
        def _():
            for ref in (car_r, car_i, dar_ref, dai_ref, dbr_ref, dbi_ref, dcr_ref, dci_ref, dd_ref):
                ref[...] = jnp.zeros_like(ref)

        dy = dy_ref[...]
        dy16 = dy.astype(BF16)
        u = u_ref[...]
        u16 = u.astype(BF16)
        _to_time_major(tmr, _dot(dy16, cr_ref[...], "nt"), tt)
        _to_time_major(tmi, -_dot(dy16, ci_ref[...], "nt"), tt)
        _to_time_major(smr, sr_ref[...], tt, first_row=SUBLANES)
        _to_time_major(smi, si_ref[...], tt, first_row=SUBLANES)
        keep = jnp.where(first_block, 0.0, 1.0)
        for j in range(SUBLANES):
            smr[j:j + 1, :] = srp_ref[SUBLANES - 1:SUBLANES, j * LANES:(j + 1) * LANES] * keep
            smi[j:j + 1, :] = sip_ref[SUBLANES - 1:SUBLANES, j * LANES:(j + 1) * LANES] * keep
        a_r = ar_ref[...]
        a_i = ai_ref[...]

        def step(i, carry):
            n_r, n_i, da_r, da_i = carry
            rows = pl.ds(pl.multiple_of((tt - 1 - i) * SUBLANES, SUBLANES), SUBLANES)
            g_r = tmr[rows, :] + a_r * n_r + a_i * n_i
            g_i = tmi[rows, :] - a_i * n_r + a_r * n_i
            tmr[rows, :] = g_r
            tmi[rows, :] = g_i
            p_r = smr[rows, :]
            p_i = smi[rows, :]
            return g_r, g_i, da_r + g_r * p_r + g_i * p_i, da_i - g_r * p_i + g_i * p_r

        zero = jnp.zeros((SUBLANES, LANES), F32)
        n_r, n_i, da_r, da_i = lax.fori_loop(0, tt, step, (car_r[...], car_i[...], zero, zero), unroll=8)
        car_r[...] = n_r
        car_i[...] = n_i
        dar_ref[...] += da_r
        dai_ref[...] += da_i
        _from_time_major(natr, tmr, tt)
        _from_time_major(nati, tmi, tt)
        dbu_r16 = natr[...].astype(BF16)
        dbu_i16 = nati[...].astype(BF16)
        du_ref[...] = (_dot(dbu_r16, br_ref[...], "nt") + _dot(dbu_i16, bi_ref[...], "nt")
                       + d_ref[...] * dy).astype(BF16)
        dbr_ref[...] += _dot(u16, dbu_r16, "tn")
        dbi_ref[...] += _dot(u16, dbu_i16, "tn")
        dcr_ref[...] += _dot(sr_ref[...].astype(BF16), dy16, "tn")
        dci_ref[...] -= _dot(si_ref[...].astype(BF16), dy16, "tn")
        dd_ref[...] += jnp.sum(dy * u, axis=0, keepdims=True)

    def rev(t):
        return nt - 1 - t

    dy_spec = pl.BlockSpec((tt, SSM_GB_CH), lambda b, t: (rev(t), b))
    u_spec = pl.BlockSpec((tt, SSM_GB_CH), lambda b, t: (rev(t), OFF_U // SSM_GB_CH + b))
    s_spec = pl.BlockSpec((tt, SSM_GB_ST), lambda b, t: (rev(t), b))
    sp_spec = pl.BlockSpec((SUBLANES, SSM_GB_ST), lambda b, t: (jnp.maximum(rev(t) * per8 - 1, 0), b))
    a_spec = pl.BlockSpec((None, SUBLANES, LANES), lambda b, t: (b, 0, 0))
    b_spec = pl.BlockSpec((None, SSM_GB_CH, SSM_GB_ST), lambda b, t: (b, 0, 0))
    c_spec = pl.BlockSpec((None, SSM_GB_ST, SSM_GB_CH), lambda b, t: (b, 0, 0))
    d_spec = pl.BlockSpec((1, SSM_GB_CH), lambda b, t: (0, b))
    tm_shape = pltpu.VMEM((tt * SUBLANES, LANES), F32)
    sm_shape = pltpu.VMEM(((tt + 1) * SUBLANES, LANES), F32)
    nat_shape = pltpu.VMEM((tt, SSM_GB_ST), F32)
    tile = pltpu.VMEM((SUBLANES, LANES), F32)
    return pl.pallas_call(
        body, name=name, grid=(SSM_NGB, nt),
        in_specs=[dy_spec, u_spec, s_spec, s_spec, sp_spec, sp_spec, a_spec, a_spec, b_spec, b_spec, c_spec, c_spec,
                  d_spec],
        out_specs=[dy_spec, a_spec, a_spec, b_spec, b_spec, c_spec, c_spec, d_spec],
        out_shape=[jax.ShapeDtypeStruct((T, SSM_WIDTH), BF16),
                   jax.ShapeDtypeStruct((SSM_NGB, SUBLANES, LANES), F32),
                   jax.ShapeDtypeStruct((SSM_NGB, SUBLANES, LANES), F32),
                   jax.ShapeDtypeStruct((SSM_NGB, SSM_GB_CH, SSM_GB_ST), F32),
                   jax.ShapeDtypeStruct((SSM_NGB, SSM_GB_CH, SSM_GB_ST), F32),
                   jax.ShapeDtypeStruct((SSM_NGB, SSM_GB_ST, SSM_GB_CH), F32),
                   jax.ShapeDtypeStruct((SSM_NGB, SSM_GB_ST, SSM_GB_CH), F32),
                   jax.ShapeDtypeStruct((1, SSM_WIDTH), F32)],
        scratch_shapes=[tm_shape, tm_shape, sm_shape, sm_shape, nat_shape, nat_shape, tile, tile],
        compiler_params=_params("parallel", "arbitrary"))(dy, z, s_r, s_i, s_r, s_i, ar, ai, bbr, bbi, cbr, cbi, dskip)


def _glu_fwd(y, w, b, name):
    T, W = y.shape
    tm = _pick(T, 512)

    def body(y_ref, w_ref, b_ref, pre_ref, y3_ref):
        y2 = _gelu(y_ref[...])
        pre = _dot(y2.astype(BF16), w_ref[...]) + b_ref[...]
        pre_ref[...] = pre
        y3_ref[...] = (y2 * _sigmoid(pre)).astype(BF16)

    row = pl.BlockSpec((tm, W), lambda i: (i, 0))
    return pl.pallas_call(
        body, name=name, grid=(T // tm,),
        in_specs=[row, pl.BlockSpec((W, W), lambda i: (0, 0)), pl.BlockSpec((1, W), lambda i: (0, 0))],
        out_specs=[row, row], out_shape=[jax.ShapeDtypeStruct((T, W), F32), jax.ShapeDtypeStruct((T, W), BF16)],
        compiler_params=_params("parallel"))(y, w, b)


def _glu_bwd_gate(dy3, y, pre, name):
    T, W = y.shape
    tm = _pick(T, 512)

    def body(dy3_ref, y_ref, pre_ref, dpre_ref, t1_ref, y2_ref, db_ref):
        @pl.when(pl.program_id(0) == 0)
        def _():
            db_ref[...] = jnp.zeros_like(db_ref)

        y2 = _gelu(y_ref[...])
        sg = _sigmoid(pre_ref[...])
        dy3 = dy3_ref[...]
        dpre = dy3 * y2 * sg * (1.0 - sg)
        dpre_ref[...] = dpre.astype(BF16)
        t1_ref[...] = dy3 * sg
        y2_ref[...] = y2.astype(BF16)
        db_ref[...] += jnp.sum(dpre, axis=0, keepdims=True)

    row = pl.BlockSpec((tm, W), lambda i: (i, 0))
    vec = pl.BlockSpec((1, W), lambda i: (0, 0))
    return pl.pallas_call(
        body, name=name, grid=(T // tm,), in_specs=[row, row, row], out_specs=[row, row, row, vec],
        out_shape=[jax.ShapeDtypeStruct((T, W), BF16), jax.ShapeDtypeStruct((T, W), F32),
                   jax.ShapeDtypeStruct((T, W), BF16), jax.ShapeDtypeStruct((1, W), F32)],
        compiler_params=_params("arbitrary"))(dy3, y, pre)


def _glu_bwd_in(dpre, w, t1, y, name):
    T, W = y.shape
    tm = _pick(T, 512)

    def body(dpre_ref, w_ref, t1_ref, y_ref, dy_ref):
        dy_ref[...] = (_dot(dpre_ref[...], w_ref[...], "nt") + t1_ref[...]) * _gelu_grad(y_ref[...])

    row = pl.BlockSpec((tm, W), lambda i: (i, 0))
    return pl.pallas_call(
        body, name=name, grid=(T // tm,), in_specs=[row, pl.BlockSpec((W, W), lambda i: (0, 0)), row, row],
        out_specs=row, out_shape=jax.ShapeDtypeStruct((T, W), F32),
        compiler_params=_params("parallel"))(dpre, w, t1, y)


def _merge_fwd(ya, y3, wa, ws, z, bias, name):
    T, W = ya.shape
    D = wa.shape[1]
    tm, tn = _pick(T, 512), _pick(D, 512)

    def body(ya_ref, y3_ref, wa_ref, ws_ref, za_ref, zs_ref, ba_ref, bs_ref, a_ref, b_ref, m_ref):
        a = _dot(ya_ref[...], wa_ref[...])
        b = _dot(y3_ref[...], ws_ref[...])
        a_ref[...] = a
        b_ref[...] = b
        m_ref[...] = (_sigmoid(za_ref[...] + ba_ref[...]) * a + _sigmoid(zs_ref[...] + bs_ref[...]) * b).astype(BF16)

    act = pl.BlockSpec((tm, W), lambda i, j: (i, 0))
    wgt = pl.BlockSpec((W, tn), lambda i, j: (0, j))
    za = pl.BlockSpec((tm, tn), lambda i, j: (i, OFF_G // tn + j))
    zs = pl.BlockSpec((tm, tn), lambda i, j: (i, (OFF_G + D) // tn + j))
    ba = pl.BlockSpec((1, tn), lambda i, j: (0, j))
    bs = pl.BlockSpec((1, tn), lambda i, j: (0, D // tn + j))
    out = pl.BlockSpec((tm, tn), lambda i, j: (i, j))
    return pl.pallas_call(
        body, name=name, grid=(T // tm, D // tn), in_specs=[act, act, wgt, wgt, za, zs, ba, bs],
        out_specs=[out, out, out],
        out_shape=[jax.ShapeDtypeStruct((T, D), F32), jax.ShapeDtypeStruct((T, D), F32),
                   jax.ShapeDtypeStruct((T, D), BF16)],
        compiler_params=_params("parallel", "parallel"))(ya, y3, wa, ws, z, z, bias, bias)


def _merge_bwd(dm, a, b, z, bias, name):
    T, D = dm.shape
    tm, tn = _pick(T, 512), _pick(D, 512)

    def body(dm_ref, a_ref, b_ref, za_ref, zs_ref, ba_ref, bs_ref, da_ref, db_ref, dza_ref, dzs_ref, dba_ref, dbs_ref):
        @pl.when(pl.program_id(1) == 0)
        def _():
            dba_ref[...] = jnp.zeros_like(dba_ref)
            dbs_ref[...] = jnp.zeros_like(dbs_ref)

        dm = dm_ref[...]
        sa = _sigmoid(za_ref[...] + ba_ref[...])
        ss = _sigmoid(zs_ref[...] + bs_ref[...])
        da_ref[...] = (dm * sa).astype(BF16)
        db_ref[...] = (dm * ss).astype(BF16)
        dza = dm * a_ref[...] * sa * (1.0 - sa)
        dzs = dm * b_ref[...] * ss * (1.0 - ss)
        dza_ref[...] = dza.astype(BF16)
        dzs_ref[...] = dzs.astype(BF16)
        dba_ref[...] += jnp.sum(dza, axis=0, keepdims=True)
        dbs_ref[...] += jnp.sum(dzs, axis=0, keepdims=True)

    blk = pl.BlockSpec((tm, tn), lambda j, i: (i, j))
    za = pl.BlockSpec((tm, tn), lambda j, i: (i, OFF_G // tn + j))
    zs = pl.BlockSpec((tm, tn), lambda j, i: (i, (OFF_G + D) // tn + j))
    ba = pl.BlockSpec((1, tn), lambda j, i: (0, j))
    bs = pl.BlockSpec((1, tn), lambda j, i: (0, D // tn + j))
    big = jax.ShapeDtypeStruct((T, D), BF16)
    vec = jax.ShapeDtypeStruct((1, D), F32)
    return pl.pallas_call(
        body, name=name, grid=(D // tn, T // tm), in_specs=[blk, blk, blk, za, zs, ba, bs],
        out_specs=[blk, blk, blk, blk, ba, ba], out_shape=[big, big, big, big, vec, vec],
        compiler_params=_params("parallel", "arbitrary"))(dm, a, b, z, z, bias, bias)


def _ffn_fwd(h, wg, wu, name):
    T, D = h.shape
    F = wg.shape[1]
    tm, tn = _pick(T, 1024), _pick(F, 512)

    def body(h_ref, wg_ref, wu_ref, g_ref, u_ref, act_ref):
        hv = h_ref[...]
        g = _dot(hv, wg_ref[...])
        u = _dot(hv, wu_ref[...])
        g_ref[...] = g
        u_ref[...] = u
        act_ref[...] = (g * _sigmoid(g) * u).astype(BF16)

    wgt = pl.BlockSpec((D, tn), lambda i, j: (0, j))
    out = pl.BlockSpec((tm, tn), lambda i, j: (i, j))
    return pl.pallas_call(
        body, name=name, grid=(T // tm, F // tn), in_specs=[pl.BlockSpec((tm, D), lambda i, j: (i, 0)), wgt, wgt],
        out_specs=[out, out, out],
        out_shape=[jax.ShapeDtypeStruct((T, F), F32), jax.ShapeDtypeStruct((T, F), F32),
                   jax.ShapeDtypeStruct((T, F), BF16)],
        compiler_params=_params("parallel", "parallel"))(h, wg, wu)


def _ffn_bwd_act(dx, wo, g, u, name):
    T, D = dx.shape
    F = wo.shape[0]
    tm, tn = _pick(T, 1024), _pick(F, 512)

    def body(dx_ref, wo_ref, g_ref, u_ref, dg_ref, du_ref):
        dact = _dot(dx_ref[...].astype(BF16), wo_ref[...], "nt")
        gv = g_ref[...]
        sg = _sigmoid(gv)
        dg_ref[...] = (dact * u_ref[...] * sg * (1.0 + gv * (1.0 - sg))).astype(BF16)
        du_ref[...] = (dact * gv * sg).astype(BF16)

    out = pl.BlockSpec((tm, tn), lambda i, j: (i, j))
    big = jax.ShapeDtypeStruct((T, F), BF16)
    return pl.pallas_call(
        body, name=name, grid=(T // tm, F // tn),
        in_specs=[pl.BlockSpec((tm, D), lambda i, j: (i, 0)), pl.BlockSpec((tn, D), lambda i, j: (j, 0)), out, out],
        out_specs=[out, out], out_shape=[big, big],
        compiler_params=_params("parallel", "parallel"))(dx, wo, g, u)


def _place():
    x, y, c = lax.axis_index("x"), lax.axis_index("y"), lax.axis_index("c")
    other_chips = [(1 - x, y), (x, 1 - y), (1 - x, 1 - y)]
    return x, y, c, 2 * x + y, other_chips


_ANY = pl.BlockSpec(memory_space=pl.ANY)
_N_PEER = 4


def _all_gather(shards, name):
    n = len(shards)

    def body(*refs):
        ins, outs = refs[:n], refs[n:2 * n]
        send_sems, recv_sems, local_sems = refs[2 * n:]
        x, y, c, chip, other_chips = _place()
        sibling = (x, y, 1 - c)

        def remote(src, dst, a, j, dev):
            return pltpu.make_async_remote_copy(src_ref=src, dst_ref=dst, send_sem=send_sems.at[a, j],
                                                recv_sem=recv_sems.at[a, j], device_id=dev, device_id_type=MESH)

        sends, local = [], []
        for a in range(n):
            mine = outs[a].at[c, chip]
            local.append(pltpu.make_async_copy(ins[a], mine, local_sems.at[a]))
            local[a].start()
            for j, (ox, oy) in enumerate(other_chips):
                sends.append(remote(ins[a], mine, a, j, (ox, oy, c)))
                sends[-1].start()
        for a in range(n):
            local[a].wait()
            for j, (ox, oy) in enumerate(other_chips):
                remote(ins[a], outs[a].at[c, 2 * ox + oy], a, j, (ox, oy, c)).wait_recv()
            sends.append(remote(outs[a].at[c], outs[a].at[c], a, 3, sibling))
            sends[-1].start()
        for a in range(n):
            remote(outs[a].at[1 - c], outs[a].at[1 - c], a, 3, sibling).wait_recv()
        for cp in sends:
            cp.wait_send()

    return pl.pallas_call(
        body, name=name, in_specs=[_ANY] * n, out_specs=[_ANY] * n,
        out_shape=[jax.ShapeDtypeStruct((2, 4) + s.shape, s.dtype) for s in shards],
        scratch_shapes=[pltpu.SemaphoreType.DMA((n, _N_PEER)), pltpu.SemaphoreType.DMA((n, _N_PEER)),
                        pltpu.SemaphoreType.DMA((n,))])(*shards)


def _pair_exchange(parts, name):
    n = len(parts)

    def body(*refs):
        ins, mine, theirs = refs[:n], refs[n:2 * n], refs[2 * n:3 * n]
        send_sems, recv_sems, local_sems = refs[3 * n:]
        x, y, c, _, _ = _place()
        sibling = (x, y, 1 - c)
        sends, local = [], []
        for a in range(n):
            local.append(pltpu.make_async_copy(ins[a].at[c], mine[a], local_sems.at[a]))
            local[a].start()
            sends.append(pltpu.make_async_remote_copy(
                src_ref=ins[a].at[1 - c], dst_ref=theirs[a], send_sem=send_sems.at[a], recv_sem=recv_sems.at[a],
                device_id=sibling, device_id_type=MESH))
            sends[a].start()
        for a in range(n):
            sends[a].wait_recv()
            sends[a].wait_send()
            local[a].wait()

    half = [jax.ShapeDtypeStruct(p.shape[1:], p.dtype) for p in parts]
    outs = pl.pallas_call(
        body, name=name, in_specs=[_ANY] * n, out_specs=[_ANY] * (2 * n), out_shape=half + half,
        scratch_shapes=[pltpu.SemaphoreType.DMA((n,)), pltpu.SemaphoreType.DMA((n,)),
                        pltpu.SemaphoreType.DMA((n,))])(*parts)
    return outs[:n], outs[n:]


def _chip_exchange(parts, name):
    n = len(parts)

    def body(*refs):
        ins, own, got = refs[:n], refs[n:2 * n], refs[2 * n:3 * n]
        send_sems, recv_sems, local_sems = refs[3 * n:]
        _, _, c, chip, other_chips = _place()
        sends, local = [], []
        for a in range(n):
            local.append(pltpu.make_async_copy(ins[a].at[chip], own[a], local_sems.at[a]))
            local[a].start()
            for j, (ox, oy) in enumerate(other_chips):
                sends.append(pltpu.make_async_remote_copy(
                    src_ref=ins[a].at[2 * ox + oy], dst_ref=got[a].at[j], send_sem=send_sems.at[a, j],
                    recv_sem=recv_sems.at[a, j], device_id=(ox, oy, c), device_id_type=MESH))
                sends[-1].start()
        for cp in sends:
            cp.wait_recv()
            cp.wait_send()
        for a in range(n):
            local[a].wait()

    own = [jax.ShapeDtypeStruct(p.shape[1:], p.dtype) for p in parts]
    got = [jax.ShapeDtypeStruct((3,) + p.shape[1:], p.dtype) for p in parts]
    outs = pl.pallas_call(
        body, name=name, in_specs=[_ANY] * n, out_specs=[_ANY] * (2 * n), out_shape=own + got,
        scratch_shapes=[pltpu.SemaphoreType.DMA((n, 3)), pltpu.SemaphoreType.DMA((n, 3)),
                        pltpu.SemaphoreType.DMA((n,))])(*parts)
    return outs[:n], outs[n:]


def _add_pair(a, b, name):
    k, R, C = a.shape
    tr = _pick(R, 512, 16)

    def body(a_ref, b_ref, o_ref):
        o_ref[...] = (a_ref[...].astype(F32) + b_ref[...].astype(F32)).astype(BF16)

    blk = pl.BlockSpec((None, tr, C), lambda s, i: (s, i, 0))
    return pl.pallas_call(
        body, name=name, grid=(k, R // tr), in_specs=[blk, blk], out_specs=blk,
        out_shape=jax.ShapeDtypeStruct(a.shape, BF16), compiler_params=_params("parallel", "parallel"))(a, b)


def _adamw_math(w, g, m, v):
    m = ADAM_B1 * m + (1.0 - ADAM_B1) * g
    v = ADAM_B2 * v + (1.0 - ADAM_B2) * (g * g)
    m_hat = m / (1.0 - ADAM_B1 ** ADAM_STEP)
    v_hat = v / (1.0 - ADAM_B2 ** ADAM_STEP)
    delta = -ADAM_LR * (m_hat / (jnp.sqrt(v_hat) + ADAM_EPS) + ADAM_WD * w)
    return delta, m, v


def _adamw_shard(w, m, v, layer, own, got, name):
    _, R, C = w.shape
    tr = _pick(R, 256, 16)

    def body(w_ref, m_ref, v_ref, own_ref, g0_ref, g1_ref, g2_ref, g_ref, d_ref, nm_ref, nv_ref):
        g = ((own_ref[...].astype(F32) + g0_ref[...].astype(F32)) + g1_ref[...].astype(F32)) + g2_ref[...].astype(F32)
        delta, nm, nv = _adamw_math(w_ref[...], g, m_ref[...], v_ref[...])
        g_ref[...] = g
        d_ref[...] = delta
        nm_ref[...] = nm
        nv_ref[...] = nv

    state = pl.BlockSpec((None, tr, C), lambda i: (layer, i, 0))
    row = pl.BlockSpec((tr, C), lambda i: (i, 0))
    peer = [pl.BlockSpec((None, tr, C), functools.partial(lambda j, i: (j, i, 0), j)) for j in range(3)]
    out = jax.ShapeDtypeStruct((R, C), F32)
    return pl.pallas_call(
        body, name=name, grid=(R // tr,), in_specs=[state, state, state, row] + peer, out_specs=[row] * 4,
        out_shape=[out] * 4, compiler_params=_params("parallel"))(w, m, v, own, got, got, got)


def _adamw_small(w, m, v, gathered, name):
    R = w.shape[0]
    tr = _pick(R, 512, SUBLANES)

    def body(w_ref, m_ref, v_ref, gg_ref, g_ref, d_ref, nm_ref, nv_ref):
        g = gg_ref[0, 0]
        for c in range(2):
            for k in range(4):
                if c or k:
                    g = g + gg_ref[c, k]
        delta, nm, nv = _adamw_math(w_ref[...], g, m_ref[...], v_ref[...])
        g_ref[...] = g
        d_ref[...] = delta
        nm_ref[...] = nm
        nv_ref[...] = nv

    row = pl.BlockSpec((tr, LANES), lambda i: (i, 0))
    out = jax.ShapeDtypeStruct((R, LANES), F32)
    return pl.pallas_call(
        body, name=name, grid=(R // tr,),
        in_specs=[row, row, row, pl.BlockSpec((2, 4, tr, LANES), lambda i: (0, 0, i, 0))], out_specs=[row] * 4,
        out_shape=[out] * 4, compiler_params=_params("parallel"))(w, m, v, gathered)


def _to_heads(a, n_heads):
    return a.reshape(a.shape[0], n_heads, HEAD_DIM).transpose(1, 0, 2)


def _from_heads(a):
    return a.transpose(1, 0, 2).reshape(a.shape[1], a.shape[0] * HEAD_DIM)


def _layer_fwd(x, w, s, tag):
    h = _rms_fwd(x, s["norm_mix_g"], f"rms_mix_{tag}")
    z = _mm(h, w["w_in"], "nn", F32, f"in_proj_{tag}", tn=512)
    q = _to_heads(z[:, :OFF_K], N_Q_HEADS)
    k = _to_heads(z[:, OFF_K:OFF_V], N_KV_HEADS)
    v = _to_heads(z[:, OFF_V:OFF_U], N_KV_HEADS)
    ya = _from_heads(_attn_fwd(q, k, v, s["q_norm_g"], s["k_norm_g"], s["attn_sinks"], f"attn_fwd_{tag}"))
    y, s_r, s_i = _ssm_fwd(z, *s["ssm16"], s["ssm_d"], f"ssm_fwd_{tag}")
    pre, y3 = _glu_fwd(y, w["ssm_glu_w"], s["ssm_glu_b"], f"glu_fwd_{tag}")
    a, b, merged = _merge_fwd(ya, y3, w["w_attn_branch"], w["w_ssm_branch"], z, s["gate_bias"], f"merge_fwd_{tag}")
    x1 = _mm(merged, w["w_out"], "nn", F32, f"out_proj_{tag}", residual=x)
    h2 = _rms_fwd(x1, s["norm_ffn_g"], f"rms_ffn_{tag}")
    g, u, act = _ffn_fwd(h2, w["w_gate"], w["w_up"], f"ffn_fwd_{tag}")
    x2 = _mm(act, w["w_ffn_out"], "nn", F32, f"ffn_out_{tag}", residual=x1)
    saved = dict(x=x, h=h, z=z, q=q, k=k, v=v, ya=ya, y=y, s_r=s_r, s_i=s_i, pre=pre, y3=y3, a=a, b=b, merged=merged,
                 x1=x1, h2=h2, g=g, u=u, act=act)
    return x2, saved


def _layer_bwd(dx2, sv, w, s, tag):
    gw, gs = {}, {}
    dg16, du16 = _ffn_bwd_act(dx2, w["w_ffn_out"], sv["g"], sv["u"], f"ffn_bwd_act_{tag}")
    gw["w_ffn_out"] = _mm(sv["act"], dx2, "tn", BF16, f"dw_ffn_out_{tag}")
    dh2 = _mm(dg16, w["w_gate"], "nt", F32, f"dh2_gate_{tag}")
    dh2 = _mm(du16, w["w_up"], "nt", F32, f"dh2_up_{tag}", residual=dh2)
    gw["w_gate"] = _mm(sv["h2"], dg16, "tn", BF16, f"dw_gate_{tag}")
    gw["w_up"] = _mm(sv["h2"], du16, "tn", BF16, f"dw_up_{tag}")
    dx1, gs["norm_ffn_g"] = _rms_bwd(dh2, sv["x1"], s["norm_ffn_g"], dx2, f"rms_ffn_bwd_{tag}")
    dm = _mm(dx1, w["w_out"], "nt", F32, f"dmerged_{tag}")
    gw["w_out"] = _mm(sv["merged"], dx1, "tn", BF16, f"dw_out_{tag}")
    da16, db16, dza, dzs, dba, dbs = _merge_bwd(dm, sv["a"], sv["b"], sv["z"], s["gate_bias"], f"merge_bwd_{tag}")
    gs["gate_bias"] = jnp.concatenate([dba, dbs], axis=1)
    dya = _mm(da16, w["w_attn_branch"], "nt", F32, f"dya_{tag}")
    gw["w_attn_branch"] = _mm(sv["ya"], da16, "tn", BF16, f"dw_attn_branch_{tag}")
    dy3 = _mm(db16, w["w_ssm_branch"], "nt", F32, f"dy3_{tag}")
    gw["w_ssm_branch"] = _mm(sv["y3"], db16, "tn", BF16, f"dw_ssm_branch_{tag}")
    dpre16, t1, y2_16, gs["ssm_glu_b"] = _glu_bwd_gate(dy3, sv["y"], sv["pre"], f"glu_bwd_gate_{tag}")
    dy = _glu_bwd_in(dpre16, w["ssm_glu_w"], t1, sv["y"], f"glu_bwd_in_{tag}")
    gw["ssm_glu_w"] = _mm(y2_16, dpre16, "tn", BF16, f"dw_glu_{tag}")
    du_ssm, *gs["ssm_disc"], gs["ssm_d"] = _ssm_bwd(dy, sv["z"], sv["s_r"], sv["s_i"], *s["ssm16"], s["ssm_d"],
                                                  f"ssm_bwd_{tag}")
    dq, dk, dv, dqg, dkg, dsk = _attn_bwd(sv["q"], sv["k"], sv["v"], _to_heads(dya, N_Q_HEADS), s["q_norm_g"],
                                          s["k_norm_g"], s["attn_sinks"], f"attn_bwd_{tag}")
    gs["q_norm_g"] = jnp.sum(dqg, axis=0)
    gs["k_norm_g"] = jnp.sum(dkg, axis=0)
    gs["attn_sinks"] = dsk[:, :, 0].reshape(1, N_Q_HEADS)
    dz = jnp.concatenate([_from_heads(dq), _from_heads(dk).astype(BF16), _from_heads(dv).astype(BF16), du_ssm, dza,
                          dzs], axis=1)
    dh = _mm(dz, w["w_in"], "nt", F32, f"dh_{tag}")
    gw["w_in"] = _mm(sv["h"], dz, "tn", BF16, f"dw_in_{tag}", tn=512)
    dx, gs["norm_mix_g"] = _rms_bwd(dh, sv["x"], s["norm_mix_g"], dx1, f"rms_mix_bwd_{tag}")
    return dx, gw, gs


def _assemble(name, gathered):
    if name in COL_SHARDED:
        rows = gathered.shape[2]
        return gathered.transpose(2, 1, 0, 3).reshape(rows, -1)
    return gathered.transpose(1, 0, 2, 3).reshape(-1, gathered.shape[3])


def _disassemble(name, full):
    if name in COL_SHARDED:
        rows, cols = full.shape
        return full.reshape(rows, 4, 2, cols // N_DEV).transpose(2, 1, 0, 3)
    rows, cols = full.shape
    return full.reshape(4, 2, rows // N_DEV, cols).transpose(1, 0, 2, 3)


def _pack(arrays):
    flat = jnp.concatenate([a.reshape(-1) for a in arrays])
    pad = (-flat.shape[0]) % (SUBLANES * LANES)
    return jnp.pad(flat, (0, pad)).reshape(-1, LANES)


def _unpack(packed, like):
    flat, out, off = packed.reshape(-1), [], 0
    for a in like:
        out.append(flat[off:off + a.size].reshape(a.shape))
        off += a.size
    return out


def kernel(x, norm_mix_g, w_in, gate_bias, q_norm_g, k_norm_g, attn_sinks, ssm_lambda_re, ssm_lambda_im, ssm_log_dt, ssm_b_re, ssm_b_im, ssm_c_re, ssm_c_im, ssm_d, ssm_glu_w, ssm_glu_b, w_attn_branch, w_ssm_branch, w_out, norm_ffn_g, w_ffn_in, w_ffn_out, loss_target, m_norm_mix_g, m_w_in, m_gate_bias, m_q_norm_g, m_k_norm_g, m_attn_sinks, m_ssm_lambda_re, m_ssm_lambda_im, m_ssm_log_dt, m_ssm_b_re, m_ssm_b_im, m_ssm_c_re, m_ssm_c_im, m_ssm_d, m_ssm_glu_w, m_ssm_glu_b, m_w_attn_branch, m_w_ssm_branch, m_w_out, m_norm_ffn_g, m_w_ffn_in, m_w_ffn_out, v_norm_mix_g, v_w_in, v_gate_bias, v_q_norm_g, v_k_norm_g, v_attn_sinks, v_ssm_lambda_re, v_ssm_lambda_im, v_ssm_log_dt, v_ssm_b_re, v_ssm_b_im, v_ssm_c_re, v_ssm_c_im, v_ssm_d, v_ssm_glu_w, v_ssm_glu_b, v_w_attn_branch, v_w_ssm_branch, v_w_out, v_norm_ffn_g, v_w_ffn_in, v_w_ffn_out):
    given = dict(locals())
    wts = {n: given[n] for n in WEIGHTS}
    mom = {n: given["m_" + n] for n in WEIGHTS}
    var = {n: given["v_" + n] for n in WEIGHTS}
    depth = w_in.shape[0]
    xs = x[0]
    target = loss_target[0]

    full = []
    for l in range(depth):
        gathered = _all_gather([wts[n][l].astype(BF16) for n in BIG], f"gather_weights_{l}")
        wl = {n: _assemble(n, g) for n, g in zip(BIG, gathered)}
        wl["w_gate"], wl["w_up"] = wl["w_ffn_in"][:, :D_FF], wl["w_ffn_in"][:, D_FF:]
        full.append(wl)

    small, disc_vjp = [], []
    for l in range(depth):
        s = {n: wts[n][l].reshape(1, -1) for n in ("norm_mix_g", "gate_bias", "q_norm_g", "k_norm_g", "attn_sinks",
                                                   "ssm_d", "ssm_glu_b", "norm_ffn_g")}
        disc, vjp = jax.vjp(_ssm_discretize, *[wts[n][l] for n in ("ssm_lambda_re", "ssm_lambda_im", "ssm_log_dt",
                                                                  "ssm_b_re", "ssm_b_im", "ssm_c_re", "ssm_c_im")])
        s["ssm16"] = (disc[0], disc[1]) + tuple(d.astype(BF16) for d in disc[2:])
        small.append(s)
        disc_vjp.append(vjp)

    act, saved = xs, []
    for l in range(depth):
        act, sv = _layer_fwd(act, full[l], small[l], str(l))
        saved.append(sv)
    dact, loss_local = _loss_grad(act, target, "loss_head")
    big_grads, small_grads = [None] * depth, [None] * depth
    for l in reversed(range(depth)):
        dact, gw, gs = _layer_bwd(dact, saved[l], full[l], small[l], str(l))
        gw["w_ffn_in"] = jnp.concatenate([gw.pop("w_gate"), gw.pop("w_up")], axis=1)
        (gs["ssm_lambda_re"], gs["ssm_lambda_im"], gs["ssm_log_dt"], gs["ssm_b_re"], gs["ssm_b_im"], gs["ssm_c_re"],
         gs["ssm_c_im"]) = disc_vjp[l](tuple(gs.pop("ssm_disc")))
        big_grads[l], small_grads[l] = gw, gs
    loss = lax.psum(loss_local, ("x", "y", "c"))

    out = {"grad": {}, "delta": {}, "new_m": {}, "new_v": {}}
    per_layer = {n: [] for n in BIG}
    for l in range(depth):
        parts = [_disassemble(n, big_grads[l][n]) for n in BIG]
        mine, theirs = _pair_exchange(parts, f"grad_pair_exchange_{l}")
        sums = []
        for n, a, b in zip(BIG, mine, theirs):
            k, r, c_ = a.shape
            sums.append(_add_pair(a, b, f"grad_pair_sum_{n}_{l}"))
        own, got = _chip_exchange(sums, f"grad_chip_exchange_{l}")
        for n, o, g in zip(BIG, own, got):
            per_layer[n].append(_adamw_shard(wts[n], mom[n], var[n], l, o, g, f"adamw_{n}_{l}"))
    for n in BIG:
        for i, kind in enumerate(("grad", "delta", "new_m", "new_v")):
            out[kind][n] = jnp.stack([per_layer[n][l][i] for l in range(depth)])

    like = [wts[n] for n in SMALL]
    g_small = _pack([jnp.stack([small_grads[l][n].reshape(wts[n].shape[1:]) for l in range(depth)]) for n in SMALL])
    (gathered_small,) = _all_gather([g_small], "gather_small_grads")
    res = _adamw_small(_pack(like), _pack([mom[n] for n in SMALL]), _pack([var[n] for n in SMALL]), gathered_small,
                       "adamw_small")
    for kind, packed in zip(("grad", "delta", "new_m", "new_v"), res):
        for n, a in zip(SMALL, _unpack(packed, like)):
            out[kind][n] = a

    grad_x = dact.reshape(x.shape)
    return (loss, grad_x, *[out["grad"][n] for n in WEIGHTS], *[out["delta"][n] for n in WEIGHTS],
            *[out["new_m"][n] for n in WEIGHTS], *[out["new_v"][n] for n in WEIGHTS])
```

```python
import functools
import math

import jax
import jax.numpy as jnp
from jax import lax
from jax.experimental import pallas as pl
from jax.experimental.pallas import tpu as pltpu

F32, BF16 = jnp.float32, jnp.bfloat16
MESH = pl.DeviceIdType.MESH

D_MODEL = 2048
HEAD_DIM = 64
N_Q_HEADS = 16
N_KV_HEADS = 4
GQA_GROUP = N_Q_HEADS // N_KV_HEADS
ATTN_WIDTH = N_Q_HEADS * HEAD_DIM
KV_WIDTH = N_KV_HEADS * HEAD_DIM
WINDOW = 128
BLOCK = 128
SSM_WIDTH = D_MODEL // 2
SSM_GROUP_CH = 16
SSM_GROUPS = SSM_WIDTH // SSM_GROUP_CH
SSM_STATE = 64
D_FF = 5632
OFF_K = ATTN_WIDTH
OFF_V = OFF_K + KV_WIDTH
OFF_U = OFF_V + KV_WIDTH
OFF_G = OFF_U + SSM_WIDTH
IN_WIDTH = OFF_G + 2 * D_MODEL
RMS_EPS = 1e-6
ATTN_SCALE = HEAD_DIM ** -0.5
NEG_BIG = -1e30

SSM_NGB = 4
SSM_GB_CH = SSM_WIDTH // SSM_NGB
SSM_GB_ST = SSM_GROUPS * SSM_STATE // SSM_NGB
SUBLANES = 8
LANES = 128
SSM_TT = 256

ADAM_LR = 0.001
ADAM_B1 = 0.9
ADAM_B2 = 0.999
ADAM_EPS = 1e-08
ADAM_WD = 0.01
ADAM_STEP = 10

N_DEV = 8
VMEM_LIMIT_BYTES = 52 * 1024 * 1024

BIG = ("w_in", "ssm_glu_w", "w_attn_branch", "w_ssm_branch", "w_out", "w_ffn_in", "w_ffn_out")
COL_SHARDED = ("w_in", "w_attn_branch", "w_ssm_branch", "w_ffn_in")
SMALL = ("norm_mix_g", "gate_bias", "q_norm_g", "k_norm_g", "attn_sinks", "ssm_lambda_re", "ssm_lambda_im",
         "ssm_log_dt", "ssm_b_re", "ssm_b_im", "ssm_c_re", "ssm_c_im", "ssm_d", "ssm_glu_b", "norm_ffn_g")
WEIGHTS = ("norm_mix_g", "w_in", "gate_bias", "q_norm_g", "k_norm_g", "attn_sinks", "ssm_lambda_re", "ssm_lambda_im",
           "ssm_log_dt", "ssm_b_re", "ssm_b_im", "ssm_c_re", "ssm_c_im", "ssm_d", "ssm_glu_w", "ssm_glu_b",
           "w_attn_branch", "w_ssm_branch", "w_out", "norm_ffn_g", "w_ffn_in", "w_ffn_out")


def _pick(n, target, mult=LANES):
    best = None
    for t in range(mult, min(n, target) + 1, mult):
        if n % t == 0:
            best = t
    return n if best is None else best


def _params(*sem):
    return pltpu.CompilerParams(dimension_semantics=sem, vmem_limit_bytes=VMEM_LIMIT_BYTES)


def _sigmoid(v):
    return 1.0 / (1.0 + jnp.exp(-v))


_GELU_C = math.sqrt(2.0 / math.pi)


def _gelu(v):
    return 0.5 * v * (1.0 + jnp.tanh(_GELU_C * (v + 0.044715 * v * v * v)))


def _gelu_grad(v):
    t = jnp.tanh(_GELU_C * (v + 0.044715 * v * v * v))
    return 0.5 * (1.0 + t) + 0.5 * v * (1.0 - t * t) * _GELU_C * (1.0 + 3.0 * 0.044715 * v * v)


_DN = {"nn": (((1,), (0,)), ((), ())), "nt": (((1,), (1,)), ((), ())), "tn": (((0,), (0,)), ((), ()))}


def _dot(a, b, dims="nn"):
    return lax.dot_general(a, b, _DN[dims], preferred_element_type=F32)


def _mm(a, b, dims, out_dtype, name, residual=None, tm=1024, tn=1024, tk=2048):
    if dims == "tn":
        K, M = a.shape
    else:
        M, K = a.shape
    N = b.shape[0] if dims == "nt" else b.shape[1]
    tm, tn, tk = _pick(M, tm), _pick(N, tn), _pick(K, tk)
    nk = K // tk
    has_res = residual is not None

    def finish(out, refs):
        if has_res:
            out = out + refs[2][...].astype(F32)
        refs[-2][...] = out.astype(out_dtype)

    def body_single(*refs):
        finish(_dot(refs[0][...].astype(BF16), refs[1][...].astype(BF16), dims), refs)

    def body_multi(*refs):
        acc_ref = refs[-1]
        k = pl.program_id(2)

        @pl.when(k == 0)
        def _():
            acc_ref[...] = jnp.zeros_like(acc_ref)

        acc_ref[...] += _dot(refs[0][...].astype(BF16), refs[1][...].astype(BF16), dims)

        @pl.when(k == nk - 1)
        def _():
            finish(acc_ref[...], refs)

    a_spec = (pl.BlockSpec((tk, tm), lambda i, j, k: (k, i)) if dims == "tn"
              else pl.BlockSpec((tm, tk), lambda i, j, k: (i, k)))
    b_spec = (pl.BlockSpec((tn, tk), lambda i, j, k: (j, k)) if dims == "nt"
              else pl.BlockSpec((tk, tn), lambda i, j, k: (k, j)))
    o_spec = pl.BlockSpec((tm, tn), lambda i, j, k: (i, j))
    in_specs = [a_spec, b_spec] + ([o_spec] if has_res else [])
    args = (a, b) + ((residual,) if has_res else ())
    return pl.pallas_call(
        body_single if nk == 1 else body_multi, name=name, grid=(M // tm, N // tn, nk), in_specs=in_specs,
        out_specs=o_spec, out_shape=jax.ShapeDtypeStruct((M, N), out_dtype),
        scratch_shapes=[pltpu.VMEM((tm, tn) if nk > 1 else (SUBLANES, LANES), F32)],
        compiler_params=_params("parallel", "parallel", "arbitrary"))(*args)


def _rms_fwd(x, g, name):
    T, D = x.shape
    tr = _pick(T, 256, SUBLANES)

    def body(x_ref, g_ref, o_ref):
        xf = x_ref[...]
        r = lax.rsqrt(jnp.mean(xf * xf, axis=-1, keepdims=True) + RMS_EPS)
        o_ref[...] = (xf * r * g_ref[...]).astype(BF16)

    return pl.pallas_call(
        body, name=name, grid=(T // tr,),
        in_specs=[pl.BlockSpec((tr, D), lambda i: (i, 0)), pl.BlockSpec((1, D), lambda i: (0, 0))],
        out_specs=pl.BlockSpec((tr, D), lambda i: (i, 0)), out_shape=jax.ShapeDtypeStruct((T, D), BF16),
        compiler_params=_params("parallel"))(x, g)


def _rms_bwd(dh, x, g, dres, name):
    T, D = x.shape
    tr = _pick(T, 256, SUBLANES)

    def body(dh_ref, x_ref, g_ref, dres_ref, dx_ref, dg_ref):
        @pl.when(pl.program_id(0) == 0)
        def _():
            dg_ref[...] = jnp.zeros_like(dg_ref)

        xf = x_ref[...]
        r = lax.rsqrt(jnp.mean(xf * xf, axis=-1, keepdims=True) + RMS_EPS)
        xhat = xf * r
        dhv = dh_ref[...]
        dxh = dhv * g_ref[...]
        dx_ref[...] = dres_ref[...] + r * (dxh - xhat * jnp.mean(dxh * xhat, axis=-1, keepdims=True))
        dg_ref[...] += jnp.sum(dhv * xhat, axis=0, keepdims=True)

    row = pl.BlockSpec((tr, D), lambda i: (i, 0))
    vec = pl.BlockSpec((1, D), lambda i: (0, 0))
    return pl.pallas_call(
        body, name=name, grid=(T // tr,), in_specs=[row, row, vec, row], out_specs=[row, vec],
        out_shape=[jax.ShapeDtypeStruct((T, D), F32), jax.ShapeDtypeStruct((1, D), F32)],
        compiler_params=_params("arbitrary"))(dh, x, g, dres)


def _loss_grad(y, target, name):
    T, D = y.shape
    tr = _pick(T, 256, SUBLANES)

    def body(y_ref, t_ref, dx_ref, loss_ref):
        @pl.when(pl.program_id(0) == 0)
        def _():
            loss_ref[...] = jnp.zeros_like(loss_ref)

        err = y_ref[...] - t_ref[...]
        dx_ref[...] = err * (1.0 / D)
        loss_ref[...] += jnp.sum(jnp.mean(err * err, axis=-1, keepdims=True), axis=0, keepdims=True) * 0.5

    row = pl.BlockSpec((tr, D), lambda i: (i, 0))
    one = pl.BlockSpec((1, 1), lambda i: (0, 0))
    dx, loss = pl.pallas_call(
        body, name=name, grid=(T // tr,), in_specs=[row, row], out_specs=[row, one],
        out_shape=[jax.ShapeDtypeStruct((T, D), F32), jax.ShapeDtypeStruct((1, 1), F32)],
        compiler_params=_params("arbitrary"))(y, target)
    return dx, loss[0, 0]


def _alibi_slope(h):
    return 2.0 ** (-8.0 * (h + 1) / N_Q_HEADS)


def _attn_mask():
    row = lax.broadcasted_iota(jnp.int32, (BLOCK, 2 * BLOCK), 0)
    col = lax.broadcasted_iota(jnp.int32, (BLOCK, 2 * BLOCK), 1)
    dist = row - col + BLOCK
    return dist.astype(F32), (dist >= 0) & (dist < WINDOW), col


def _head_norm(v, gain):
    r = lax.rsqrt(jnp.mean(v * v, axis=-1, keepdims=True) + RMS_EPS)
    vhat = v * r
    return r, vhat, vhat * gain


def _attn_probs(qn16, kn16, slope, dist, valid, sink):
    s = _dot(qn16, kn16, "nt") * ATTN_SCALE - slope * dist
    s = jnp.where(valid, s, NEG_BIG)
    m = jnp.maximum(jnp.max(s, axis=-1, keepdims=True), sink)
    p = jnp.exp(s - m)
    ps = jnp.exp(sink - m)
    den = jnp.sum(p, axis=-1, keepdims=True) + ps
    return p, ps, den


def _attn_specs(T):
    q_spec = pl.BlockSpec((GQA_GROUP, BLOCK, HEAD_DIM), lambda h, n: (h, n, 0))
    cur = pl.BlockSpec((None, BLOCK, HEAD_DIM), lambda h, n: (h, n, 0))
    prev = pl.BlockSpec((None, BLOCK, HEAD_DIM), lambda h, n: (h, jnp.maximum(n - 1, 0), 0))
    gain = pl.BlockSpec((1, HEAD_DIM), lambda h, n: (0, 0))
    sink = pl.BlockSpec((None, GQA_GROUP, LANES), lambda h, n: (h, 0, 0))
    return q_spec, cur, prev, gain, sink


def _sink_rows(sinks):
    return jnp.broadcast_to(sinks.reshape(N_KV_HEADS, GQA_GROUP, 1), (N_KV_HEADS, GQA_GROUP, LANES))


def _attn_fwd(q, k, v, qg, kg, sinks, name):
    T = q.shape[1]
    nb = T // BLOCK

    def body(q_ref, kc_ref, kp_ref, vc_ref, vp_ref, qg_ref, kg_ref, sk_ref, o_ref):
        kv = pl.program_id(0)
        n = pl.program_id(1)
        dist, valid, col = _attn_mask()
        valid = valid & ((col >= BLOCK) | (n > 0))
        kk = jnp.concatenate([kp_ref[...], kc_ref[...]], axis=0)
        _, _, kn = _head_norm(kk, kg_ref[...])
        kn16 = kn.astype(BF16)
        v16 = jnp.concatenate([vp_ref[...], vc_ref[...]], axis=0).astype(BF16)
        for g in range(GQA_GROUP):
            slope = sum(jnp.where(kv == j, _alibi_slope(j * GQA_GROUP + g), 0.0) for j in range(N_KV_HEADS))
            _, _, qn = _head_norm(q_ref[g], qg_ref[...])
            sink = sk_ref[g:g + 1, 0:1]
            p, _, den = _attn_probs(qn.astype(BF16), kn16, slope, dist, valid, sink)
            o_ref[g] = (_dot((p / den).astype(BF16), v16) ).astype(BF16)

    q_spec, cur, prev, gain, sink = _attn_specs(T)
    return pl.pallas_call(
        body, name=name, grid=(N_KV_HEADS, nb), in_specs=[q_spec, cur, prev, cur, prev, gain, gain, sink],
        out_specs=q_spec, out_shape=jax.ShapeDtypeStruct((N_Q_HEADS, T, HEAD_DIM), BF16),
        compiler_params=_params("parallel", "parallel"))(q, k, k, v, v, qg, kg, _sink_rows(sinks))


def _attn_bwd(q, k, v, do, qg, kg, sinks, name):
    T = q.shape[1]
    nb = T // BLOCK

    def body(q_ref, kc_ref, kp_ref, vc_ref, vp_ref, do_ref, qg_ref, kg_ref, sk_ref,
             dq_ref, dk_ref, dv_ref, dqg_ref, dkg_ref, dsk_ref):
        kv = pl.program_id(0)
        n = pl.program_id(1)

        @pl.when(n == 0)
        def _():
            dk_ref[...] = jnp.zeros_like(dk_ref)
            dv_ref[...] = jnp.zeros_like(dv_ref)
            dqg_ref[...] = jnp.zeros_like(dqg_ref)
            dkg_ref[...] = jnp.zeros_like(dkg_ref)
            dsk_ref[...] = jnp.zeros_like(dsk_ref)

        dist, valid, col = _attn_mask()
        valid = valid & ((col >= BLOCK) | (n > 0))
        kk = jnp.concatenate([kp_ref[...], kc_ref[...]], axis=0)
        rk, khat, kn = _head_norm(kk, kg_ref[...])
        kn16 = kn.astype(BF16)
        v16 = jnp.concatenate([vp_ref[...], vc_ref[...]], axis=0).astype(BF16)
        dkn = jnp.zeros((2 * BLOCK, HEAD_DIM), F32)
        dv = jnp.zeros((2 * BLOCK, HEAD_DIM), F32)
        dqg = jnp.zeros((1, HEAD_DIM), F32)
        for g in range(GQA_GROUP):
            slope = sum(jnp.where(kv == j, _alibi_slope(j * GQA_GROUP + g), 0.0) for j in range(N_KV_HEADS))
            rq, qhat, qn = _head_norm(q_ref[g], qg_ref[...])
            qn16 = qn.astype(BF16)
            sink = sk_ref[g:g + 1, 0:1]
            p, ps, den = _attn_probs(qn16, kn16, slope, dist, valid, sink)
            pn = p / den
            do16 = do_ref[g].astype(BF16)
            dp = _dot(do16, v16, "nt")
            delta = jnp.sum(pn * dp, axis=-1, keepdims=True)
            ds16 = (pn * (dp - delta)).astype(BF16)
            dsink = -jnp.sum(ps / den * delta, axis=0, keepdims=True)
            dsk_ref[g:g + 1, :] += jnp.broadcast_to(dsink, (1, LANES))
            dqn = _dot(ds16, kn16) * ATTN_SCALE
            dkn = dkn + _dot(ds16, qn16, "tn") * ATTN_SCALE
            dv = dv + _dot(pn.astype(BF16), do16, "tn")
            dqh = dqn * qg_ref[...]
            dq_ref[g] = (rq * (dqh - qhat * jnp.mean(dqh * qhat, axis=-1, keepdims=True))).astype(BF16)
            dqg = dqg + jnp.sum(dqn * qhat, axis=0, keepdims=True)
        dqg_ref[...] += dqg
        dkg_ref[...] += jnp.sum(dkn * khat, axis=0, keepdims=True)
        dkh = dkn * kg_ref[...]
        dk = rk * (dkh - khat * jnp.mean(dkh * khat, axis=-1, keepdims=True))
        rows = pl.ds(pl.multiple_of(n * BLOCK, BLOCK), BLOCK)
        dk_ref[rows, :] += dk[BLOCK:]
        dv_ref[rows, :] += dv[BLOCK:]

        @pl.when(n > 0)
        def _():
            before = pl.ds(pl.multiple_of((n - 1) * BLOCK, BLOCK), BLOCK)
            dk_ref[before, :] += dk[:BLOCK]
            dv_ref[before, :] += dv[:BLOCK]

    q_spec, cur, prev, gain, sink = _attn_specs(T)
    whole = pl.BlockSpec((None, T, HEAD_DIM), lambda h, n: (h, 0, 0))
    gacc = pl.BlockSpec((None, 1, HEAD_DIM), lambda h, n: (h, 0, 0))
    return pl.pallas_call(
        body, name=name, grid=(N_KV_HEADS, nb), in_specs=[q_spec, cur, prev, cur, prev, q_spec, gain, gain, sink],
        out_specs=[q_spec, whole, whole, gacc, gacc, sink],
        out_shape=[jax.ShapeDtypeStruct((N_Q_HEADS, T, HEAD_DIM), BF16),
                   jax.ShapeDtypeStruct((N_KV_HEADS, T, HEAD_DIM), F32),
                   jax.ShapeDtypeStruct((N_KV_HEADS, T, HEAD_DIM), F32),
                   jax.ShapeDtypeStruct((N_KV_HEADS, 1, HEAD_DIM), F32),
                   jax.ShapeDtypeStruct((N_KV_HEADS, 1, HEAD_DIM), F32),
                   jax.ShapeDtypeStruct((N_KV_HEADS, GQA_GROUP, LANES), F32)],
        compiler_params=_params("parallel", "arbitrary"))(q, k, k, v, v, do, qg, kg, _sink_rows(sinks))


def _ssm_discretize(lam_re, lam_im, log_dt, b_re, b_im, c_re, c_im):
    dt = jnp.exp(log_dt)[:, None]
    mag = jnp.exp(lam_re * dt)
    ar = mag * jnp.cos(lam_im * dt)
    ai = mag * jnp.sin(lam_im * dt)
    den = lam_re * lam_re + lam_im * lam_im
    fr = ((ar - 1.0) * lam_re + ai * lam_im) / den
    fi = (ai * lam_re - (ar - 1.0) * lam_im) / den
    bbar_r = fr[:, :, None] * b_re - fi[:, :, None] * b_im
    bbar_i = fr[:, :, None] * b_im + fi[:, :, None] * b_re
    gl = SSM_GROUPS // SSM_NGB
    eye = jnp.eye(gl, dtype=F32)

    def tiles(a):
        return a.reshape(SSM_NGB, SUBLANES, LANES)

    def bdiag(bb):
        return jnp.einsum("bgph,gk->bghkp", bb.reshape(SSM_NGB, gl, SSM_STATE, SSM_GROUP_CH), eye).reshape(
            SSM_NGB, SSM_GB_CH, SSM_GB_ST)

    def cdiag(cc):
        return jnp.einsum("bghp,gk->bgpkh", cc.reshape(SSM_NGB, gl, SSM_GROUP_CH, SSM_STATE), eye).reshape(
            SSM_NGB, SSM_GB_ST, SSM_GB_CH)

    return tiles(ar), tiles(ai), bdiag(bbar_r), bdiag(bbar_i), cdiag(c_re), cdiag(c_im)


def _to_time_major(dst, val, tt, first_row=0):
    for j in range(SUBLANES):
        dst[pl.ds(first_row + j, tt, stride=SUBLANES), :] = val[:, j * LANES:(j + 1) * LANES]


def _from_time_major(dst, src, tt):
    for j in range(SUBLANES):
        dst[:, j * LANES:(j + 1) * LANES] = src[pl.ds(j, tt, stride=SUBLANES), :]


def _ssm_fwd(z, ar, ai, bbr, bbi, cbr, cbi, dskip, name):
    T = z.shape[0]
    tt = min(SSM_TT, T)
    nt = T // tt

    def body(u_ref, ar_ref, ai_ref, br_ref, bi_ref, cr_ref, ci_ref, d_ref, y_ref, sr_ref, si_ref,
             tmr, tmi, car_r, car_i):
        @pl.when(pl.program_id(1) == 0)
        def _():
            car_r[...] = jnp.zeros_like(car_r)
            car_i[...] = jnp.zeros_like(car_i)

        u = u_ref[...]
        u16 = u.astype(BF16)
        _to_time_major(tmr, _dot(u16, br_ref[...]), tt)
        _to_time_major(tmi, _dot(u16, bi_ref[...]), tt)
        a_r = ar_ref[...]
        a_i = ai_ref[...]

        def step(t, carry):
            s_r, s_i = carry
            rows = pl.ds(pl.multiple_of(t * SUBLANES, SUBLANES), SUBLANES)
            n_r = a_r * s_r - a_i * s_i + tmr[rows, :]
            n_i = a_r * s_i + a_i * s_r + tmi[rows, :]
            tmr[rows, :] = n_r
            tmi[rows, :] = n_i
            return n_r, n_i

        s_r, s_i = lax.fori_loop(0, tt, step, (car_r[...], car_i[...]), unroll=8)
        car_r[...] = s_r
        car_i[...] = s_i
        _from_time_major(sr_ref, tmr, tt)
        _from_time_major(si_ref, tmi, tt)
        y_ref[...] = (_dot(sr_ref[...].astype(BF16), cr_ref[...]) - _dot(si_ref[...].astype(BF16), ci_ref[...])
                      + d_ref[...] * u)

    u_spec = pl.BlockSpec((tt, SSM_GB_CH), lambda b, t: (t, OFF_U // SSM_GB_CH + b))
    a_spec = pl.BlockSpec((None, SUBLANES, LANES), lambda b, t: (b, 0, 0))
    b_spec = pl.BlockSpec((None, SSM_GB_CH, SSM_GB_ST), lambda b, t: (b, 0, 0))
    c_spec = pl.BlockSpec((None, SSM_GB_ST, SSM_GB_CH), lambda b, t: (b, 0, 0))
    d_spec = pl.BlockSpec((1, SSM_GB_CH), lambda b, t: (0, b))
    y_spec = pl.BlockSpec((tt, SSM_GB_CH), lambda b, t: (t, b))
    s_spec = pl.BlockSpec((tt, SSM_GB_ST), lambda b, t: (t, b))
    n_state = SSM_NGB * SSM_GB_ST
    return pl.pallas_call(
        body, name=name, grid=(SSM_NGB, nt),
        in_specs=[u_spec, a_spec, a_spec, b_spec, b_spec, c_spec, c_spec, d_spec],
        out_specs=[y_spec, s_spec, s_spec],
        out_shape=[jax.ShapeDtypeStruct((T, SSM_WIDTH), F32), jax.ShapeDtypeStruct((T, n_state), F32),
                   jax.ShapeDtypeStruct((T, n_state), F32)],
        scratch_shapes=[pltpu.VMEM((tt * SUBLANES, LANES), F32), pltpu.VMEM((tt * SUBLANES, LANES), F32),
                        pltpu.VMEM((SUBLANES, LANES), F32), pltpu.VMEM((SUBLANES, LANES), F32)],
        compiler_params=_params("parallel", "arbitrary"))(z, ar, ai, bbr, bbi, cbr, cbi, dskip)


def _ssm_bwd(dy, z, s_r, s_i, ar, ai, bbr, bbi, cbr, cbi, dskip, name):
    T = z.shape[0]
    tt = min(SSM_TT, T)
    nt = T // tt
    per8 = tt // SUBLANES

    def body(dy_ref, u_ref, sr_ref, si_ref, srp_ref, sip_ref, ar_ref, ai_ref, br_ref, bi_ref, cr_ref, ci_ref, d_ref,
             du_ref, dar_ref, dai_ref, dbr_ref, dbi_ref, dcr_ref, dci_ref, dd_ref,
             tmr, tmi, smr, smi, natr, nati, car_r, car_i):
        tb = pl.program_id(1)
        first_block = tb == nt - 1

        @pl.when(tb == 0)
        def _():
            for ref in (car_r, car_i, dar_ref, dai_ref, dbr_ref, dbi_ref, dcr_ref, dci_ref, dd_ref):
                ref[...] = jnp.zeros_like(ref)

        dy = dy_ref[...]
        dy16 = dy.astype(BF16)
        u = u_ref[...]
        u16 = u.astype(BF16)
        _to_time_major(tmr, _dot(dy16, cr_ref[...], "nt"), tt)
        _to_time_major(tmi, -_dot(dy16, ci_ref[...], "nt"), tt)
        _to_time_major(smr, sr_ref[...], tt, first_row=SUBLANES)
        _to_time_major(smi, si_ref[...], tt, first_row=SUBLANES)
        keep = jnp.where(first_block, 0.0, 1.0)
        for j in range(SUBLANES):
            smr[j:j + 1, :] = srp_ref[SUBLANES - 1:SUBLANES, j * LANES:(j + 1) * LANES] * keep
            smi[j:j + 1, :] = sip_ref[SUBLANES - 1:SUBLANES, j * LANES:(j + 1) * LANES] * keep
        a_r = ar_ref[...]
        a_i = ai_ref[...]

        def step(i, carry):
            n_r, n_i, da_r, da_i = carry
            rows = pl.ds(pl.multiple_of((tt - 1 - i) * SUBLANES, SUBLANES), SUBLANES)
            g_r = tmr[rows, :] + a_r * n_r + a_i * n_i
            g_i = tmi[rows, :] - a_i * n_r + a_r * n_i
            tmr[rows, :] = g_r
            tmi[rows, :] = g_i
            p_r = smr[rows, :]
            p_i = smi[rows, :]
            return g_r, g_i, da_r + g_r * p_r + g_i * p_i, da_i - g_r * p_i + g_i * p_r

        zero = jnp.zeros((SUBLANES, LANES), F32)
        n_r, n_i, da_r, da_i = lax.fori_loop(0, tt, step, (car_r[...], car_i[...], zero, zero), unroll=8)
        car_r[...] = n_r
        car_i[...] = n_i
        dar_ref[...] += da_r
        dai_ref[...] += da_i
        _from_time_major(natr, tmr, tt)
        _from_time_major(nati, tmi, tt)
        dbu_r16 = natr[...].astype(BF16)
        dbu_i16 = nati[...].astype(BF16)
        du_ref[...] = (_dot(dbu_r16, br_ref[...], "nt") + _dot(dbu_i16, bi_ref[...], "nt")
                       + d_ref[...] * dy).astype(BF16)
        dbr_ref[...] += _dot(u16, dbu_r16, "tn")
        dbi_ref[...] += _dot(u16, dbu_i16, "tn")
        dcr_ref[...] += _dot(sr_ref[...].astype(BF16), dy16, "tn")
        dci_ref[...] -= _dot(si_ref[...].astype(BF16), dy16, "tn")
        dd_ref[...] += jnp.sum(dy * u, axis=0, keepdims=True)

    def rev(t):
        return nt - 1 - t

    dy_spec = pl.BlockSpec((tt, SSM_GB_CH), lambda b, t: (rev(t), b))
    u_spec = pl.BlockSpec((tt, SSM_GB_CH), lambda b, t: (rev(t), OFF_U // SSM_GB_CH + b))
    s_spec = pl.BlockSpec((tt, SSM_GB_ST), lambda b, t: (rev(t), b))
    sp_spec = pl.BlockSpec((SUBLANES, SSM_GB_ST), lambda b, t: (jnp.maximum(rev(t) * per8 - 1, 0), b))
    a_spec = pl.BlockSpec((None, SUBLANES, LANES), lambda b, t: (b, 0, 0))
    b_spec = pl.BlockSpec((None, SSM_GB_CH, SSM_GB_ST), lambda b, t: (b, 0, 0))
    c_spec = pl.BlockSpec((None, SSM_GB_ST, SSM_GB_CH), lambda b, t: (b, 0, 0))
    d_spec = pl.BlockSpec((1, SSM_GB_CH), lambda b, t: (0, b))
    tm_shape = pltpu.VMEM((tt * SUBLANES, LANES), F32)
    sm_shape = pltpu.VMEM(((tt + 1) * SUBLANES, LANES), F32)
    nat_shape = pltpu.VMEM((tt, SSM_GB_ST), F32)
    tile = pltpu.VMEM((SUBLANES, LANES), F32)
    return pl.pallas_call(
        body, name=name, grid=(SSM_NGB, nt),
        in_specs=[dy_spec, u_spec, s_spec, s_spec, sp_spec, sp_spec, a_spec, a_spec, b_spec, b_spec, c_spec, c_spec,
                  d_spec],
        out_specs=[dy_spec, a_spec, a_spec, b_spec, b_spec, c_spec, c_spec, d_spec],
        out_shape=[jax.ShapeDtypeStruct((T, SSM_WIDTH), BF16),
                   jax.ShapeDtypeStruct((SSM_NGB, SUBLANES, LANES), F32),
                   jax.ShapeDtypeStruct((SSM_NGB, SUBLANES, LANES), F32),
                   jax.ShapeDtypeStruct((SSM_NGB, SSM_GB_CH, SSM_GB_ST), F32),
                   jax.ShapeDtypeStruct((SSM_NGB, SSM_GB_CH, SSM_GB_ST), F32),
                   jax.ShapeDtypeStruct((SSM_NGB, SSM_GB_ST, SSM_GB_CH), F32),
                   jax.ShapeDtypeStruct((SSM_NGB, SSM_GB_ST, SSM_GB_CH), F32),
                   jax.ShapeDtypeStruct((1, SSM_WIDTH), F32)],
        scratch_shapes=[tm_shape, tm_shape, sm_shape, sm_shape, nat_shape, nat_shape, tile, tile],
        compiler_params=_params("parallel", "arbitrary"))(dy, z, s_r, s_i, s_r, s_i, ar, ai, bbr, bbi, cbr, cbi, dskip)


def _glu_fwd(y, w, b, name):
    T, W = y.shape
    tm = _pick(T, 512)

    def body(y_ref, w_ref, b_ref, pre_ref, y3_ref):
        y2 = _gelu(y_ref[...])
        pre = _dot(y2.astype(BF16), w_ref[...]) + b_ref[...]
        pre_ref[...] = pre
        y3_ref[...] = (y2 * _sigmoid(pre)).astype(BF16)

    row = pl.BlockSpec((tm, W), lambda i: (i, 0))
    return pl.pallas_call(
        body, name=name, grid=(T // tm,),
        in_specs=[row, pl.BlockSpec((W, W), lambda i: (0, 0)), pl.BlockSpec((1, W), lambda i: (0, 0))],
        out_specs=[row, row], out_shape=[jax.ShapeDtypeStruct((T, W), F32), jax.ShapeDtypeStruct((T, W), BF16)],
        compiler_params=_params("parallel"))(y, w, b)


def _glu_bwd_gate(dy3, y, pre, name):
    T, W = y.shape
    tm = _pick(T, 512)

    def body(dy3_ref, y_ref, pre_ref, dpre_ref, t1_ref, y2_ref, db_ref):
        @pl.when(pl.program_id(0) == 0)
        def _():
            db_ref[...] = jnp.zeros_like(db_ref)

        y2 = _gelu(y_ref[...])
        sg = _sigmoid(pre_ref[...])
        dy3 = dy3_ref[...]
        dpre = dy3 * y2 * sg * (1.0 - sg)
        dpre_ref[...] = dpre.astype(BF16)
        t1_ref[...] = dy3 * sg
        y2_ref[...] = y2.astype(BF16)
        db_ref[...] += jnp.sum(dpre, axis=0, keepdims=True)

    row = pl.BlockSpec((tm, W), lambda i: (i, 0))
    vec = pl.BlockSpec((1, W), lambda i: (0, 0))
    return pl.pallas_call(
        body, name=name, grid=(T // tm,), in_specs=[row, row, row], out_specs=[row, row, row, vec],
        out_shape=[jax.ShapeDtypeStruct((T, W), BF16), jax.ShapeDtypeStruct((T, W), F32),
                   jax.ShapeDtypeStruct((T, W), BF16), jax.ShapeDtypeStruct((1, W), F32)],
        compiler_params=_params("arbitrary"))(dy3, y, pre)


def _glu_bwd_in(dpre, w, t1, y, name):
    T, W = y.shape
    tm = _pick(T, 512)

    def body(dpre_ref, w_ref, t1_ref, y_ref, dy_ref):
        dy_ref[...] = (_dot(dpre_ref[...], w_ref[...], "nt") + t1_ref[...]) * _gelu_grad(y_ref[...])

    row = pl.BlockSpec((tm, W), lambda i: (i, 0))
    return pl.pallas_call(
        body, name=name, grid=(T // tm,), in_specs=[row, pl.BlockSpec((W, W), lambda i: (0, 0)), row, row],
        out_specs=row, out_shape=jax.ShapeDtypeStruct((T, W), F32),
        compiler_params=_params("parallel"))(dpre, w, t1, y)


def _merge_fwd(ya, y3, wa, ws, z, bias, name):
    T, W = ya.shape
    D = wa.shape[1]
    tm, tn = _pick(T, 512), _pick(D, 512)

    def body(ya_ref, y3_ref, wa_ref, ws_ref, za_ref, zs_ref, ba_ref, bs_ref, a_ref, b_ref, m_ref):
        a = _dot(ya_ref[...], wa_ref[...])
        b = _dot(y3_ref[...], ws_ref[...])
        a_ref[...] = a
        b_ref[...] = b
        m_ref[...] = (_sigmoid(za_ref[...] + ba_ref[...]) * a + _sigmoid(zs_ref[...] + bs_ref[...]) * b).astype(BF16)

    act = pl.BlockSpec((tm, W), lambda i, j: (i, 0))
    wgt = pl.BlockSpec((W, tn), lambda i, j: (0, j))
    za = pl.BlockSpec((tm, tn), lambda i, j: (i, OFF_G // tn + j))
    zs = pl.BlockSpec((tm, tn), lambda i, j: (i, (OFF_G + D) // tn + j))
    ba = pl.BlockSpec((1, tn), lambda i, j: (0, j))
    bs = pl.BlockSpec((1, tn), lambda i, j: (0, D // tn + j))
    out = pl.BlockSpec((tm, tn), lambda i, j: (i, j))
    return pl.pallas_call(
        body, name=name, grid=(T // tm, D // tn), in_specs=[act, act, wgt, wgt, za, zs, ba, bs],
        out_specs=[out, out, out],
        out_shape=[jax.ShapeDtypeStruct((T, D), F32), jax.ShapeDtypeStruct((T, D), F32),
                   jax.ShapeDtypeStruct((T, D), BF16)],
        compiler_params=_params("parallel", "parallel"))(ya, y3, wa, ws, z, z, bias, bias)


def _merge_bwd(dm, a, b, z, bias, name):
    T, D = dm.shape
    tm, tn = _pick(T, 512), _pick(D, 512)

    def body(dm_ref, a_ref, b_ref, za_ref, zs_ref, ba_ref, bs_ref, da_ref, db_ref, dza_ref, dzs_ref, dba_ref, dbs_ref):
        @pl.when(pl.program_id(1) == 0)
        def _():
            dba_ref[...] = jnp.zeros_like(dba_ref)
            dbs_ref[...] = jnp.zeros_like(dbs_ref)

        dm = dm_ref[...]
        sa = _sigmoid(za_ref[...] + ba_ref[...])
        ss = _sigmoid(zs_ref[...] + bs_ref[...])
        da_ref[...] = (dm * sa).astype(BF16)
        db_ref[...] = (dm * ss).astype(BF16)
        dza = dm * a_ref[...] * sa * (1.0 - sa)
        dzs = dm * b_ref[...] * ss * (1.0 - ss)
        dza_ref[...] = dza.astype(BF16)
        dzs_ref[...] = dzs.astype(BF16)
        dba_ref[...] += jnp.sum(dza, axis=0, keepdims=True)
        dbs_ref[...] += jnp.sum(dzs, axis=0, keepdims=True)

    blk = pl.BlockSpec((tm, tn), lambda j, i: (i, j))
    za = pl.BlockSpec((tm, tn), lambda j, i: (i, OFF_G // tn + j))
    zs = pl.BlockSpec((tm, tn), lambda j, i: (i, (OFF_G + D) // tn + j))
    ba = pl.BlockSpec((1, tn), lambda j, i: (0, j))
    bs = pl.BlockSpec((1, tn), lambda j, i: (0, D // tn + j))
    big = jax.ShapeDtypeStruct((T, D), BF16)
    vec = jax.ShapeDtypeStruct((1, D), F32)
    return pl.pallas_call(
        body, name=name, grid=(D // tn, T // tm), in_specs=[blk, blk, blk, za, zs, ba, bs],
        out_specs=[blk, blk, blk, blk, ba, ba], out_shape=[big, big, big, big, vec, vec],
        compiler_params=_params("parallel", "arbitrary"))(dm, a, b, z, z, bias, bias)


def _ffn_fwd(h, wg, wu, name):
    T, D = h.shape
    F = wg.shape[1]
    tm, tn = _pick(T, 1024), _pick(F, 512)

    def body(h_ref, wg_ref, wu_ref, g_ref, u_ref, act_ref):
        hv = h_ref[...]
        g = _dot(hv, wg_ref[...])
        u = _dot(hv, wu_ref[...])
        g_ref[...] = g
        u_ref[...] = u
        act_ref[...] = (g * _sigmoid(g) * u).astype(BF16)

    wgt = pl.BlockSpec((D, tn), lambda i, j: (0, j))
    out = pl.BlockSpec((tm, tn), lambda i, j: (i, j))
    return pl.pallas_call(
        body, name=name, grid=(T // tm, F // tn), in_specs=[pl.BlockSpec((tm, D), lambda i, j: (i, 0)), wgt, wgt],
        out_specs=[out, out, out],
        out_shape=[jax.ShapeDtypeStruct((T, F), F32), jax.ShapeDtypeStruct((T, F), F32),
                   jax.ShapeDtypeStruct((T, F), BF16)],
        compiler_params=_params("parallel", "parallel"))(h, wg, wu)


def _ffn_bwd_act(dx, wo, g, u, name):
    T, D = dx.shape
    F = wo.shape[0]
    tm, tn = _pick(T, 1024), _pick(F, 512)

    def body(dx_ref, wo_ref, g_ref, u_ref, dg_ref, du_ref):
        dact = _dot(dx_ref[...].astype(BF16), wo_ref[...], "nt")
        gv = g_ref[...]
        sg = _sigmoid(gv)
        dg_ref[...] = (dact * u_ref[...] * sg * (1.0 + gv * (1.0 - sg))).astype(BF16)
        du_ref[...] = (dact * gv * sg).astype(BF16)

    out = pl.BlockSpec((tm, tn), lambda i, j: (i, j))
    big = jax.ShapeDtypeStruct((T, F), BF16)
    return pl.pallas_call(
        body, name=name, grid=(T // tm, F // tn),
        in_specs=[pl.BlockSpec((tm, D), lambda i, j: (i, 0)), pl.BlockSpec((tn, D), lambda i, j: (j, 0)), out, out],
        out_specs=[out, out], out_shape=[big, big],
        compiler_params=_params("parallel", "parallel"))(dx, wo, g, u)


def _place():
    x, y, c = lax.axis_index("x"), lax.axis_index("y"), lax.axis_index("c")
    other_chips = [(1 - x, y), (x, 1 - y), (1 - x, 1 - y)]
    return x, y, c, 2 * x + y, other_chips


_ANY = pl.BlockSpec(memory_space=pl.ANY)
_N_PEER = 4


def _all_gather(shards, name):
    n = len(shards)

    def body(*refs):
        ins, outs = refs[:n], refs[n:2 * n]
        send_sems, recv_sems, local_sems = refs[2 * n:]
        x, y, c, chip, other_chips = _place()
        sibling = (x, y, 1 - c)

        def remote(src, dst, a, j, dev):
            return pltpu.make_async_remote_copy(src_ref=src, dst_ref=dst, send_sem=send_sems.at[a, j],
                                                recv_sem=recv_sems.at[a, j], device_id=dev, device_id_type=MESH)

        sends, local = [], []
        for a in range(n):
            mine = outs[a].at[c, chip]
            local.append(pltpu.make_async_copy(ins[a], mine, local_sems.at[a]))
            local[a].start()
            for j, (ox, oy) in enumerate(other_chips):
                sends.append(remote(ins[a], mine, a, j, (ox, oy, c)))
                sends[-1].start()
        for a in range(n):
            local[a].wait()
            for j, (ox, oy) in enumerate(other_chips):
                remote(ins[a], outs[a].at[c, 2 * ox + oy], a, j, (ox, oy, c)).wait_recv()
            sends.append(remote(outs[a].at[c], outs[a].at[c], a, 3, sibling))
            sends[-1].start()
        for a in range(n):
            remote(outs[a].at[1 - c], outs[a].at[1 - c], a, 3, sibling).wait_recv()
        for cp in sends:
            cp.wait_send()

    return pl.pallas_call(
        body, name=name, in_specs=[_ANY] * n, out_specs=[_ANY] * n,
        out_shape=[jax.ShapeDtypeStruct((2, 4) + s.shape, s.dtype) for s in shards],
        scratch_shapes=[pltpu.SemaphoreType.DMA((n, _N_PEER)), pltpu.SemaphoreType.DMA((n, _N_PEER)),
                        pltpu.SemaphoreType.DMA((n,))])(*shards)


def _pair_exchange(parts, name):
    n = len(parts)

    def body(*refs):
        ins, theirs = refs[:n], refs[n:2 * n]
        send_sems, recv_sems = refs[2 * n:]
        x, y, c, _, _ = _place()
        sends = []
        for a in range(n):
            sends.append(pltpu.make_async_remote_copy(
                src_ref=ins[a].at[1 - c], dst_ref=theirs[a], send_sem=send_sems.at[a], recv_sem=recv_sems.at[a],
                device_id=(x, y, 1 - c), device_id_type=MESH))
            sends[a].start()
        for a in range(n):
            sends[a].wait_recv()
            sends[a].wait_send()

    return pl.pallas_call(
        body, name=name, in_specs=[_ANY] * n, out_specs=[_ANY] * n,
        out_shape=[jax.ShapeDtypeStruct(p.shape[1:], p.dtype) for p in parts],
        scratch_shapes=[pltpu.SemaphoreType.DMA((n,)), pltpu.SemaphoreType.DMA((n,))])(*parts)


def _chip_exchange(parts, name):
    n = len(parts)

    def body(*refs):
        ins, got = refs[:n], refs[n:2 * n]
        send_sems, recv_sems = refs[2 * n:]
        _, _, c, _, other_chips = _place()
        sends = []
        for a in range(n):
            for j, (ox, oy) in enumerate(other_chips):
                sends.append(pltpu.make_async_remote_copy(
                    src_ref=ins[a].at[2 * ox + oy], dst_ref=got[a].at[j], send_sem=send_sems.at[a, j],
                    recv_sem=recv_sems.at[a, j], device_id=(ox, oy, c), device_id_type=MESH))
                sends[-1].start()
        for cp in sends:
            cp.wait_recv()
            cp.wait_send()

    return pl.pallas_call(
        body, name=name, in_specs=[_ANY] * n, out_specs=[_ANY] * n,
        out_shape=[jax.ShapeDtypeStruct((3,) + p.shape[1:], p.dtype) for p in parts],
        scratch_shapes=[pltpu.SemaphoreType.DMA((n, 3)), pltpu.SemaphoreType.DMA((n, 3))])(*parts)


def _add_pair(core, parts, theirs, name):
    _, k, R, C = parts.shape
    tr = _pick(R, 512, 16)

    def body(core_ref, a_ref, b_ref, o_ref):
        o_ref[...] = (a_ref[...].astype(F32) + b_ref[...].astype(F32)).astype(BF16)

    blk = pl.BlockSpec((None, tr, C), lambda s, i, core_ref: (s, i, 0))
    grid_spec = pltpu.PrefetchScalarGridSpec(
        num_scalar_prefetch=1, grid=(k, R // tr),
        in_specs=[pl.BlockSpec((None, None, tr, C), lambda s, i, core_ref: (core_ref[0], s, i, 0)), blk],
        out_specs=blk)
    return pl.pallas_call(
        body, name=name, grid_spec=grid_spec, out_shape=jax.ShapeDtypeStruct(theirs.shape, BF16),
        compiler_params=_params("parallel", "parallel"))(core, parts, theirs)


def _adamw_math(w, g, m, v):
    m = ADAM_B1 * m + (1.0 - ADAM_B1) * g
    v = ADAM_B2 * v + (1.0 - ADAM_B2) * (g * g)
    m_hat = m / (1.0 - ADAM_B1 ** ADAM_STEP)
    v_hat = v / (1.0 - ADAM_B2 ** ADAM_STEP)
    delta = -ADAM_LR * (m_hat / (jnp.sqrt(v_hat) + ADAM_EPS) + ADAM_WD * w)
    return delta, m, v


def _adamw_shard(chip, w, m, v, layer, sums, got, name):
    _, R, C = w.shape
    tr = _pick(R, 256, 16)

    def body(chip_ref, w_ref, m_ref, v_ref, own_ref, g0_ref, g1_ref, g2_ref, g_ref, d_ref, nm_ref, nv_ref):
        g = ((own_ref[...].astype(F32) + g0_ref[...].astype(F32)) + g1_ref[...].astype(F32)) + g2_ref[...].astype(F32)
        delta, nm, nv = _adamw_math(w_ref[...], g, m_ref[...], v_ref[...])
        g_ref[...] = g
        d_ref[...] = delta
        nm_ref[...] = nm
        nv_ref[...] = nv

    state = pl.BlockSpec((None, tr, C), lambda i, chip_ref: (layer, i, 0))
    own = pl.BlockSpec((None, tr, C), lambda i, chip_ref: (chip_ref[0], i, 0))
    row = pl.BlockSpec((tr, C), lambda i, chip_ref: (i, 0))
    peer = [pl.BlockSpec((None, tr, C), functools.partial(lambda j, i, chip_ref: (j, i, 0), j)) for j in range(3)]
    out = jax.ShapeDtypeStruct((R, C), F32)
    grid_spec = pltpu.PrefetchScalarGridSpec(
        num_scalar_prefetch=1, grid=(R // tr,), in_specs=[state, state, state, own] + peer, out_specs=[row] * 4)
    return pl.pallas_call(
        body, name=name, grid_spec=grid_spec, out_shape=[out] * 4,
        compiler_params=_params("parallel"))(chip, w, m, v, sums, got, got, got)


def _adamw_small(w, m, v, gathered, name):
    R = w.shape[0]
    tr = _pick(R, 512, SUBLANES)

    def body(w_ref, m_ref, v_ref, gg_ref, g_ref, d_ref, nm_ref, nv_ref):
        g = gg_ref[0, 0]
        for c in range(2):
            for k in range(4):
                if c or k:
                    g = g + gg_ref[c, k]
        delta, nm, nv = _adamw_math(w_ref[...], g, m_ref[...], v_ref[...])
        g_ref[...] = g
        d_ref[...] = delta
        nm_ref[...] = nm
        nv_ref[...] = nv

    row = pl.BlockSpec((tr, LANES), lambda i: (i, 0))
    out = jax.ShapeDtypeStruct((R, LANES), F32)
    return pl.pallas_call(
        body, name=name, grid=(R // tr,),
        in_specs=[row, row, row, pl.BlockSpec((2, 4, tr, LANES), lambda i: (0, 0, i, 0))], out_specs=[row] * 4,
        out_shape=[out] * 4, compiler_params=_params("parallel"))(w, m, v, gathered)


def _to_heads(a, n_heads):
    return a.reshape(a.shape[0], n_heads, HEAD_DIM).transpose(1, 0, 2)


def _from_heads(a):
    return a.transpose(1, 0, 2).reshape(a.shape[1], a.shape[0] * HEAD_DIM)


def _layer_fwd(x, w, s, tag):
    h = _rms_fwd(x, s["norm_mix_g"], f"rms_mix_{tag}")
    z = _mm(h, w["w_in"], "nn", F32, f"in_proj_{tag}", tn=512)
    q = _to_heads(z[:, :OFF_K], N_Q_HEADS)
    k = _to_heads(z[:, OFF_K:OFF_V], N_KV_HEADS)
    v = _to_heads(z[:, OFF_V:OFF_U], N_KV_HEADS)
    ya = _from_heads(_attn_fwd(q, k, v, s["q_norm_g"], s["k_norm_g"], s["attn_sinks"], f"attn_fwd_{tag}"))
    y, s_r, s_i = _ssm_fwd(z, *s["ssm16"], s["ssm_d"], f"ssm_fwd_{tag}")
    pre, y3 = _glu_fwd(y, w["ssm_glu_w"], s["ssm_glu_b"], f"glu_fwd_{tag}")
    a, b, merged = _merge_fwd(ya, y3, w["w_attn_branch"], w["w_ssm_branch"], z, s["gate_bias"], f"merge_fwd_{tag}")
    x1 = _mm(merged, w["w_out"], "nn", F32, f"out_proj_{tag}", residual=x)
    h2 = _rms_fwd(x1, s["norm_ffn_g"], f"rms_ffn_{tag}")
    g, u, act = _ffn_fwd(h2, w["w_gate"], w["w_up"], f"ffn_fwd_{tag}")
    x2 = _mm(act, w["w_ffn_out"], "nn", F32, f"ffn_out_{tag}", residual=x1)
    saved = dict(x=x, h=h, z=z, q=q, k=k, v=v, ya=ya, y=y, s_r=s_r, s_i=s_i, pre=pre, y3=y3, a=a, b=b, merged=merged,
                 x1=x1, h2=h2, g=g, u=u, act=act)
    return x2, saved


def _layer_bwd(dx2, sv, w, s, tag):
    gw, gs = {}, {}
    dg16, du16 = _ffn_bwd_act(dx2, w["w_ffn_out"], sv["g"], sv["u"], f"ffn_bwd_act_{tag}")
    gw["w_ffn_out"] = _mm(sv["act"], dx2, "tn", BF16, f"dw_ffn_out_{tag}", tm=1408)
    dh2 = _mm(dg16, w["w_gate"], "nt", F32, f"dh2_gate_{tag}")
    dh2 = _mm(du16, w["w_up"], "nt", F32, f"dh2_up_{tag}", residual=dh2)
    gw["w_gate"] = _mm(sv["h2"], dg16, "tn", BF16, f"dw_gate_{tag}")
    gw["w_up"] = _mm(sv["h2"], du16, "tn", BF16, f"dw_up_{tag}")
    dx1, gs["norm_ffn_g"] = _rms_bwd(dh2, sv["x1"], s["norm_ffn_g"], dx2, f"rms_ffn_bwd_{tag}")
    dm = _mm(dx1, w["w_out"], "nt", F32, f"dmerged_{tag}")
    gw["w_out"] = _mm(sv["merged"], dx1, "tn", BF16, f"dw_out_{tag}")
    da16, db16, dza, dzs, dba, dbs = _merge_bwd(dm, sv["a"], sv["b"], sv["z"], s["gate_bias"], f"merge_bwd_{tag}")
    gs["gate_bias"] = jnp.concatenate([dba, dbs], axis=1)
    dya = _mm(da16, w["w_attn_branch"], "nt", F32, f"dya_{tag}")
    gw["w_attn_branch"] = _mm(sv["ya"], da16, "tn", BF16, f"dw_attn_branch_{tag}")
    dy3 = _mm(db16, w["w_ssm_branch"], "nt", F32, f"dy3_{tag}")
    gw["w_ssm_branch"] = _mm(sv["y3"], db16, "tn", BF16, f"dw_ssm_branch_{tag}")
    dpre16, t1, y2_16, gs["ssm_glu_b"] = _glu_bwd_gate(dy3, sv["y"], sv["pre"], f"glu_bwd_gate_{tag}")
    dy = _glu_bwd_in(dpre16, w["ssm_glu_w"], t1, sv["y"], f"glu_bwd_in_{tag}")
    gw["ssm_glu_w"] = _mm(y2_16, dpre16, "tn", BF16, f"dw_glu_{tag}")
    du_ssm, *gs["ssm_disc"], gs["ssm_d"] = _ssm_bwd(dy, sv["z"], sv["s_r"], sv["s_i"], *s["ssm16"], s["ssm_d"],
                                                  f"ssm_bwd_{tag}")
    dq, dk, dv, dqg, dkg, dsk = _attn_bwd(sv["q"], sv["k"], sv["v"], _to_heads(dya, N_Q_HEADS), s["q_norm_g"],
                                          s["k_norm_g"], s["attn_sinks"], f"attn_bwd_{tag}")
    gs["q_norm_g"] = jnp.sum(dqg, axis=0)
    gs["k_norm_g"] = jnp.sum(dkg, axis=0)
    gs["attn_sinks"] = dsk[:, :, 0].reshape(1, N_Q_HEADS)
    dz = jnp.concatenate([_from_heads(dq), _from_heads(dk).astype(BF16), _from_heads(dv).astype(BF16), du_ssm, dza,
                          dzs], axis=1)
    dh = _mm(dz, w["w_in"], "nt", F32, f"dh_{tag}")
    gw["w_in"] = _mm(sv["h"], dz, "tn", BF16, f"dw_in_{tag}", tn=512)
    dx, gs["norm_mix_g"] = _rms_bwd(dh, sv["x"], s["norm_mix_g"], dx1, f"rms_mix_bwd_{tag}")
    return dx, gw, gs


def _assemble(name, gathered):
    if name in COL_SHARDED:
        rows = gathered.shape[2]
        return gathered.transpose(2, 1, 0, 3).reshape(rows, -1)
    return gathered.transpose(1, 0, 2, 3).reshape(-1, gathered.shape[3])


def _disassemble(name, full):
    if name in COL_SHARDED:
        rows, cols = full.shape
        return full.reshape(rows, 4, 2, cols // N_DEV).transpose(2, 1, 0, 3)
    rows, cols = full.shape
    return full.reshape(4, 2, rows // N_DEV, cols).transpose(1, 0, 2, 3)


def _pack(arrays):
    flat = jnp.concatenate([a.reshape(-1) for a in arrays])
    pad = (-flat.shape[0]) % (SUBLANES * LANES)
    return jnp.pad(flat, (0, pad)).reshape(-1, LANES)


def _unpack(packed, like):
    flat, out, off = packed.reshape(-1), [], 0
    for a in like:
        out.append(flat[off:off + a.size].reshape(a.shape))
        off += a.size
    return out


def kernel(x, norm_mix_g, w_in, gate_bias, q_norm_g, k_norm_g, attn_sinks, ssm_lambda_re, ssm_lambda_im, ssm_log_dt, ssm_b_re, ssm_b_im, ssm_c_re, ssm_c_im, ssm_d, ssm_glu_w, ssm_glu_b, w_attn_branch, w_ssm_branch, w_out, norm_ffn_g, w_ffn_in, w_ffn_out, loss_target, m_norm_mix_g, m_w_in, m_gate_bias, m_q_norm_g, m_k_norm_g, m_attn_sinks, m_ssm_lambda_re, m_ssm_lambda_im, m_ssm_log_dt, m_ssm_b_re, m_ssm_b_im, m_ssm_c_re, m_ssm_c_im, m_ssm_d, m_ssm_glu_w, m_ssm_glu_b, m_w_attn_branch, m_w_ssm_branch, m_w_out, m_norm_ffn_g, m_w_ffn_in, m_w_ffn_out, v_norm_mix_g, v_w_in, v_gate_bias, v_q_norm_g, v_k_norm_g, v_attn_sinks, v_ssm_lambda_re, v_ssm_lambda_im, v_ssm_log_dt, v_ssm_b_re, v_ssm_b_im, v_ssm_c_re, v_ssm_c_im, v_ssm_d, v_ssm_glu_w, v_ssm_glu_b, v_w_attn_branch, v_w_ssm_branch, v_w_out, v_norm_ffn_g, v_w_ffn_in, v_w_ffn_out):
    given = dict(locals())
    wts = {n: given[n] for n in WEIGHTS}
    mom = {n: given["m_" + n] for n in WEIGHTS}
    var = {n: given["v_" + n] for n in WEIGHTS}
    depth = w_in.shape[0]
    xs = x[0]
    target = loss_target[0]

    full = []
    for l in range(depth):
        gathered = _all_gather([wts[n][l].astype(BF16) for n in BIG], f"gather_weights_{l}")
        wl = {n: _assemble(n, g) for n, g in zip(BIG, gathered)}
        wl["w_gate"], wl["w_up"] = wl["w_ffn_in"][:, :D_FF], wl["w_ffn_in"][:, D_FF:]
        full.append(wl)

    small, disc_vjp = [], []
    for l in range(depth):
        s = {n: wts[n][l].reshape(1, -1) for n in ("norm_mix_g", "gate_bias", "q_norm_g", "k_norm_g", "attn_sinks",
                                                   "ssm_d", "ssm_glu_b", "norm_ffn_g")}
        disc, vjp = jax.vjp(_ssm_discretize, *[wts[n][l] for n in ("ssm_lambda_re", "ssm_lambda_im", "ssm_log_dt",
                                                                  "ssm_b_re", "ssm_b_im", "ssm_c_re", "ssm_c_im")])
        s["ssm16"] = (disc[0], disc[1]) + tuple(d.astype(BF16) for d in disc[2:])
        small.append(s)
        disc_vjp.append(vjp)

    act, saved = xs, []
    for l in range(depth):
        act, sv = _layer_fwd(act, full[l], small[l], str(l))
        saved.append(sv)
    dact, loss_local = _loss_grad(act, target, "loss_head")
    big_grads, small_grads = [None] * depth, [None] * depth
    for l in reversed(range(depth)):
        dact, gw, gs = _layer_bwd(dact, saved[l], full[l], small[l], str(l))
        gw["w_ffn_in"] = jnp.concatenate([gw.pop("w_gate"), gw.pop("w_up")], axis=1)
        (gs["ssm_lambda_re"], gs["ssm_lambda_im"], gs["ssm_log_dt"], gs["ssm_b_re"], gs["ssm_b_im"], gs["ssm_c_re"],
         gs["ssm_c_im"]) = disc_vjp[l](tuple(gs.pop("ssm_disc")))
        big_grads[l], small_grads[l] = gw, gs
    loss = lax.psum(loss_local, ("x", "y", "c"))

    out = {"grad": {}, "delta": {}, "new_m": {}, "new_v": {}}
    per_layer = {n: [] for n in BIG}
    core = lax.axis_index("c").astype(jnp.int32).reshape(1)
    chip = (2 * lax.axis_index("x") + lax.axis_index("y")).astype(jnp.int32).reshape(1)
    for l in range(depth):
        parts = [_disassemble(n, big_grads[l][n]) for n in BIG]
        theirs = _pair_exchange(parts, f"grad_pair_exchange_{l}")
        sums = [_add_pair(core, p, t, f"grad_pair_sum_{n}_{l}") for n, p, t in zip(BIG, parts, theirs)]
        got = _chip_exchange(sums, f"grad_chip_exchange_{l}")
        for n, sm, g in zip(BIG, sums, got):
            per_layer[n].append(_adamw_shard(chip, wts[n], mom[n], var[n], l, sm, g, f"adamw_{n}_{l}"))
    for n in BIG:
        for i, kind in enumerate(("grad", "delta", "new_m", "new_v")):
            out[kind][n] = jnp.stack([per_layer[n][l][i] for l in range(depth)])

    like = [wts[n] for n in SMALL]
    g_small = _pack([jnp.stack([small_grads[l][n].reshape(wts[n].shape[1:]) for l in range(depth)]) for n in SMALL])
    (gathered_small,) = _all_gather([g_small], "gather_small_grads")
    res = _adamw_small(_pack(like), _pack([mom[n] for n in SMALL]), _pack([var[n] for n in SMALL]), gathered_small,
                       "adamw_small")
    for kind, packed in zip(("grad", "delta", "new_m", "new_v"), res):
        for n, a in zip(SMALL, _unpack(packed, like)):
            out[kind][n] = a

    grad_x = dact.reshape(x.shape)
    return (loss, grad_x, *[out["grad"][n] for n in WEIGHTS], *[out["delta"][n] for n in WEIGHTS],
            *[out["new_m"][n] for n in WEIGHTS], *[out["new_v"][n] for n in WEIGHTS])
```

```python
import functools
import math

import jax
import jax.numpy as jnp
from jax import lax
from jax.experimental import pallas as pl
from jax.experimental.pallas import tpu as pltpu

F32, BF16 = jnp.float32, jnp.bfloat16
MESH = pl.DeviceIdType.MESH

D_MODEL = 2048
HEAD_DIM = 64
N_Q_HEADS = 16
N_KV_HEADS = 4
GQA_GROUP = N_Q_HEADS // N_KV_HEADS
ATTN_WIDTH = N_Q_HEADS * HEAD_DIM
KV_WIDTH = N_KV_HEADS * HEAD_DIM
WINDOW = 128
BLOCK = 128
SSM_WIDTH = D_MODEL // 2
SSM_GROUP_CH = 16
SSM_GROUPS = SSM_WIDTH // SSM_GROUP_CH
SSM_STATE = 64
D_FF = 5632
OFF_K = ATTN_WIDTH
OFF_V = OFF_K + KV_WIDTH
OFF_U = OFF_V + KV_WIDTH
OFF_G = OFF_U + SSM_WIDTH
IN_WIDTH = OFF_G + 2 * D_MODEL
RMS_EPS = 1e-6
ATTN_SCALE = HEAD_DIM ** -0.5
NEG_BIG = -1e30

SSM_NGB = 4
SSM_GB_CH = SSM_WIDTH // SSM_NGB
SSM_GB_ST = SSM_GROUPS * SSM_STATE // SSM_NGB
SUBLANES = 8
LANES = 128
SSM_TT = 256

ADAM_LR = 0.001
ADAM_B1 = 0.9
ADAM_B2 = 0.999
ADAM_EPS = 1e-08
ADAM_WD = 0.01
ADAM_STEP = 10

N_DEV = 8
VMEM_LIMIT_BYTES = 52 * 1024 * 1024

BIG = ("w_in", "ssm_glu_w", "w_attn_branch", "w_ssm_branch", "w_out", "w_ffn_in", "w_ffn_out")
COL_SHARDED = ("w_in", "w_attn_branch", "w_ssm_branch", "w_ffn_in")
SMALL = ("norm_mix_g", "gate_bias", "q_norm_g", "k_norm_g", "attn_sinks", "ssm_lambda_re", "ssm_lambda_im",
         "ssm_log_dt", "ssm_b_re", "ssm_b_im", "ssm_c_re", "ssm_c_im", "ssm_d", "ssm_glu_b", "norm_ffn_g")
WEIGHTS = ("norm_mix_g", "w_in", "gate_bias", "q_norm_g", "k_norm_g", "attn_sinks", "ssm_lambda_re", "ssm_lambda_im",
           "ssm_log_dt", "ssm_b_re", "ssm_b_im", "ssm_c_re", "ssm_c_im", "ssm_d", "ssm_glu_w", "ssm_glu_b",
           "w_attn_branch", "w_ssm_branch", "w_out", "norm_ffn_g", "w_ffn_in", "w_ffn_out")


def _pick(n, target, mult=LANES):
    best = None
    for t in range(mult, min(n, target) + 1, mult):
        if n % t == 0:
            best = t
    return n if best is None else best


def _params(*sem):
    return pltpu.CompilerParams(dimension_semantics=sem, vmem_limit_bytes=VMEM_LIMIT_BYTES)


def _sigmoid(v):
    return 1.0 / (1.0 + jnp.exp(-v))


_GELU_C = math.sqrt(2.0 / math.pi)


def _gelu(v):
    return 0.5 * v * (1.0 + jnp.tanh(_GELU_C * (v + 0.044715 * v * v * v)))


def _gelu_grad(v):
    t = jnp.tanh(_GELU_C * (v + 0.044715 * v * v * v))
    return 0.5 * (1.0 + t) + 0.5 * v * (1.0 - t * t) * _GELU_C * (1.0 + 3.0 * 0.044715 * v * v)


_DN = {"nn": (((1,), (0,)), ((), ())), "nt": (((1,), (1,)), ((), ())), "tn": (((0,), (0,)), ((), ()))}


def _dot(a, b, dims="nn"):
    return lax.dot_general(a, b, _DN[dims], preferred_element_type=F32)


def _mm(a, b, dims, out_dtype, name, residual=None, tm=1024, tn=1024, tk=2048):
    if dims == "tn":
        K, M = a.shape
    else:
        M, K = a.shape
    N = b.shape[0] if dims == "nt" else b.shape[1]
    tm, tn, tk = _pick(M, tm), _pick(N, tn), _pick(K, tk)
    nk = K // tk
    has_res = residual is not None

    def finish(out, refs):
        if has_res:
            out = out + refs[2][...].astype(F32)
        refs[-2][...] = out.astype(out_dtype)

    def body_single(*refs):
        finish(_dot(refs[0][...].astype(BF16), refs[1][...].astype(BF16), dims), refs)

    def body_multi(*refs):
        acc_ref = refs[-1]
        k = pl.program_id(2)

        @pl.when(k == 0)
        def _():
            acc_ref[...] = jnp.zeros_like(acc_ref)

        acc_ref[...] += _dot(refs[0][...].astype(BF16), refs[1][...].astype(BF16), dims)

        @pl.when(k == nk - 1)
        def _():
            finish(acc_ref[...], refs)

    a_spec = (pl.BlockSpec((tk, tm), lambda i, j, k: (k, i)) if dims == "tn"
              else pl.BlockSpec((tm, tk), lambda i, j, k: (i, k)))
    b_spec = (pl.BlockSpec((tn, tk), lambda i, j, k: (j, k)) if dims == "nt"
              else pl.BlockSpec((tk, tn), lambda i, j, k: (k, j)))
    o_spec = pl.BlockSpec((tm, tn), lambda i, j, k: (i, j))
    in_specs = [a_spec, b_spec] + ([o_spec] if has_res else [])
    args = (a, b) + ((residual,) if has_res else ())
    return pl.pallas_call(
        body_single if nk == 1 else body_multi, name=name, grid=(M // tm, N // tn, nk), in_specs=in_specs,
        out_specs=o_spec, out_shape=jax.ShapeDtypeStruct((M, N), out_dtype),
        scratch_shapes=[pltpu.VMEM((tm, tn) if nk > 1 else (SUBLANES, LANES), F32)],
        compiler_params=_params("parallel", "parallel", "arbitrary"))(*args)


def _rms_fwd(x, g, name):
    T, D = x.shape
    tr = _pick(T, 256, SUBLANES)

    def body(x_ref, g_ref, o_ref):
        xf = x_ref[...]
        r = lax.rsqrt(jnp.mean(xf * xf, axis=-1, keepdims=True) + RMS_EPS)
        o_ref[...] = (xf * r * g_ref[...]).astype(BF16)

    return pl.pallas_call(
        body, name=name, grid=(T // tr,),
        in_specs=[pl.BlockSpec((tr, D), lambda i: (i, 0)), pl.BlockSpec((1, D), lambda i: (0, 0))],
        out_specs=pl.BlockSpec((tr, D), lambda i: (i, 0)), out_shape=jax.ShapeDtypeStruct((T, D), BF16),
        compiler_params=_params("parallel"))(x, g)


def _rms_bwd(dh, x, g, dres, name):
    T, D = x.shape
    tr = _pick(T, 256, SUBLANES)

    def body(dh_ref, x_ref, g_ref, dres_ref, dx_ref, dg_ref):
        @pl.when(pl.program_id(0) == 0)
        def _():
            dg_ref[...] = jnp.zeros_like(dg_ref)

        xf = x_ref[...]
        r = lax.rsqrt(jnp.mean(xf * xf, axis=-1, keepdims=True) + RMS_EPS)
        xhat = xf * r
        dhv = dh_ref[...]
        dxh = dhv * g_ref[...]
        dx_ref[...] = dres_ref[...] + r * (dxh - xhat * jnp.mean(dxh * xhat, axis=-1, keepdims=True))
        dg_ref[...] += jnp.sum(dhv * xhat, axis=0, keepdims=True)

    row = pl.BlockSpec((tr, D), lambda i: (i, 0))
    vec = pl.BlockSpec((1, D), lambda i: (0, 0))
    return pl.pallas_call(
        body, name=name, grid=(T // tr,), in_specs=[row, row, vec, row], out_specs=[row, vec],
        out_shape=[jax.ShapeDtypeStruct((T, D), F32), jax.ShapeDtypeStruct((1, D), F32)],
        compiler_params=_params("arbitrary"))(dh, x, g, dres)


def _loss_grad(y, target, name):
    T, D = y.shape
    tr = _pick(T, 256, SUBLANES)

    def body(y_ref, t_ref, dx_ref, loss_ref):
        @pl.when(pl.program_id(0) == 0)
        def _():
            loss_ref[...] = jnp.zeros_like(loss_ref)

        err = y_ref[...] - t_ref[...]
        dx_ref[...] = err * (1.0 / D)
        loss_ref[...] += jnp.sum(jnp.mean(err * err, axis=-1, keepdims=True), axis=0, keepdims=True) * 0.5

    row = pl.BlockSpec((tr, D), lambda i: (i, 0))
    one = pl.BlockSpec((1, 1), lambda i: (0, 0))
    dx, loss = pl.pallas_call(
        body, name=name, grid=(T // tr,), in_specs=[row, row], out_specs=[row, one],
        out_shape=[jax.ShapeDtypeStruct((T, D), F32), jax.ShapeDtypeStruct((1, 1), F32)],
        compiler_params=_params("arbitrary"))(y, target)
    return dx, loss[0, 0]


def _alibi_slope(h):
    return 2.0 ** (-8.0 * (h + 1) / N_Q_HEADS)


def _attn_mask():
    row = lax.broadcasted_iota(jnp.int32, (BLOCK, 2 * BLOCK), 0)
    col = lax.broadcasted_iota(jnp.int32, (BLOCK, 2 * BLOCK), 1)
    dist = row - col + BLOCK
    return dist.astype(F32), (dist >= 0) & (dist < WINDOW), col


def _head_norm(v, gain):
    r = lax.rsqrt(jnp.mean(v * v, axis=-1, keepdims=True) + RMS_EPS)
    vhat = v * r
    return r, vhat, vhat * gain


def _attn_probs(qn16, kn16, slope, dist, valid, sink):
    s = _dot(qn16, kn16, "nt") * ATTN_SCALE - slope * dist
    s = jnp.where(valid, s, NEG_BIG)
    m = jnp.maximum(jnp.max(s, axis=-1, keepdims=True), sink)
    p = jnp.exp(s - m)
    ps = jnp.exp(sink - m)
    den = jnp.sum(p, axis=-1, keepdims=True) + ps
    return p, ps, den


def _attn_specs(T):
    q_spec = pl.BlockSpec((GQA_GROUP, BLOCK, HEAD_DIM), lambda h, n: (h, n, 0))
    cur = pl.BlockSpec((None, BLOCK, HEAD_DIM), lambda h, n: (h, n, 0))
    prev = pl.BlockSpec((None, BLOCK, HEAD_DIM), lambda h, n: (h, jnp.maximum(n - 1, 0), 0))
    gain = pl.BlockSpec((1, HEAD_DIM), lambda h, n: (0, 0))
    sink = pl.BlockSpec((None, GQA_GROUP, LANES), lambda h, n: (h, 0, 0))
    return q_spec, cur, prev, gain, sink


def _sink_rows(sinks):
    return jnp.broadcast_to(sinks.reshape(N_KV_HEADS, GQA_GROUP, 1), (N_KV_HEADS, GQA_GROUP, LANES))


def _attn_fwd(q, k, v, qg, kg, sinks, name):
    T = q.shape[1]
    nb = T // BLOCK

    def body(q_ref, kc_ref, kp_ref, vc_ref, vp_ref, qg_ref, kg_ref, sk_ref, o_ref):
        kv = pl.program_id(0)
        n = pl.program_id(1)
        dist, valid, col = _attn_mask()
        valid = valid & ((col >= BLOCK) | (n > 0))
        kk = jnp.concatenate([kp_ref[...], kc_ref[...]], axis=0)
        _, _, kn = _head_norm(kk, kg_ref[...])
        kn16 = kn.astype(BF16)
        v16 = jnp.concatenate([vp_ref[...], vc_ref[...]], axis=0).astype(BF16)
        for g in range(GQA_GROUP):
            slope = sum(jnp.where(kv == j, _alibi_slope(j * GQA_GROUP + g), 0.0) for j in range(N_KV_HEADS))
            _, _, qn = _head_norm(q_ref[g], qg_ref[...])
            sink = sk_ref[g:g + 1, 0:1]
            p, _, den = _attn_probs(qn.astype(BF16), kn16, slope, dist, valid, sink)
            o_ref[g] = (_dot((p / den).astype(BF16), v16) ).astype(BF16)

    q_spec, cur, prev, gain, sink = _attn_specs(T)
    return pl.pallas_call(
        body, name=name, grid=(N_KV_HEADS, nb), in_specs=[q_spec, cur, prev, cur, prev, gain, gain, sink],
        out_specs=q_spec, out_shape=jax.ShapeDtypeStruct((N_Q_HEADS, T, HEAD_DIM), BF16),
        compiler_params=_params("parallel", "parallel"))(q, k, k, v, v, qg, kg, _sink_rows(sinks))


def _attn_bwd(q, k, v, do, qg, kg, sinks, name):
    T = q.shape[1]
    nb = T // BLOCK

    def body(q_ref, kc_ref, kp_ref, vc_ref, vp_ref, do_ref, qg_ref, kg_ref, sk_ref,
             dq_ref, dk_ref, dv_ref, dqg_ref, dkg_ref, dsk_ref):
        kv = pl.program_id(0)
        n = pl.program_id(1)

        @pl.when(n == 0)
        def _():
            dk_ref[...] = jnp.zeros_like(dk_ref)
            dv_ref[...] = jnp.zeros_like(dv_ref)
            dqg_ref[...] = jnp.zeros_like(dqg_ref)
            dkg_ref[...] = jnp.zeros_like(dkg_ref)
            dsk_ref[...] = jnp.zeros_like(dsk_ref)

        dist, valid, col = _attn_mask()
        valid = valid & ((col >= BLOCK) | (n > 0))
        kk = jnp.concatenate([kp_ref[...], kc_ref[...]], axis=0)
        rk, khat, kn = _head_norm(kk, kg_ref[...])
        kn16 = kn.astype(BF16)
        v16 = jnp.concatenate([vp_ref[...], vc_ref[...]], axis=0).astype(BF16)
        dkn = jnp.zeros((2 * BLOCK, HEAD_DIM), F32)
        dv = jnp.zeros((2 * BLOCK, HEAD_DIM), F32)
        dqg = jnp.zeros((1, HEAD_DIM), F32)
        for g in range(GQA_GROUP):
            slope = sum(jnp.where(kv == j, _alibi_slope(j * GQA_GROUP + g), 0.0) for j in range(N_KV_HEADS))
            rq, qhat, qn = _head_norm(q_ref[g], qg_ref[...])
            qn16 = qn.astype(BF16)
            sink = sk_ref[g:g + 1, 0:1]
            p, ps, den = _attn_probs(qn16, kn16, slope, dist, valid, sink)
            pn = p / den
            do16 = do_ref[g].astype(BF16)
            dp = _dot(do16, v16, "nt")
            delta = jnp.sum(pn * dp, axis=-1, keepdims=True)
            ds16 = (pn * (dp - delta)).astype(BF16)
            dsink = -jnp.sum(ps / den * delta, axis=0, keepdims=True)
            dsk_ref[g:g + 1, :] += jnp.broadcast_to(dsink, (1, LANES))
            dqn = _dot(ds16, kn16) * ATTN_SCALE
            dkn = dkn + _dot(ds16, qn16, "tn") * ATTN_SCALE
            dv = dv + _dot(pn.astype(BF16), do16, "tn")
            dqh = dqn * qg_ref[...]
            dq_ref[g] = (rq * (dqh - qhat * jnp.mean(dqh * qhat, axis=-1, keepdims=True))).astype(BF16)
            dqg = dqg + jnp.sum(dqn * qhat, axis=0, keepdims=True)
        dqg_ref[...] += dqg
        dkg_ref[...] += jnp.sum(dkn * khat, axis=0, keepdims=True)
        dkh = dkn * kg_ref[...]
        dk = rk * (dkh - khat * jnp.mean(dkh * khat, axis=-1, keepdims=True))
        rows = pl.ds(pl.multiple_of(n * BLOCK, BLOCK), BLOCK)
        dk_ref[rows, :] += dk[BLOCK:]
        dv_ref[rows, :] += dv[BLOCK:]

        @pl.when(n > 0)
        def _():
            before = pl.ds(pl.multiple_of((n - 1) * BLOCK, BLOCK), BLOCK)
            dk_ref[before, :] += dk[:BLOCK]
            dv_ref[before, :] += dv[:BLOCK]

    q_spec, cur, prev, gain, sink = _attn_specs(T)
    whole = pl.BlockSpec((None, T, HEAD_DIM), lambda h, n: (h, 0, 0))
    gacc = pl.BlockSpec((None, 1, HEAD_DIM), lambda h, n: (h, 0, 0))
    return pl.pallas_call(
        body, name=name, grid=(N_KV_HEADS, nb), in_specs=[q_spec, cur, prev, cur, prev, q_spec, gain, gain, sink],
        out_specs=[q_spec, whole, whole, gacc, gacc, sink],
        out_shape=[jax.ShapeDtypeStruct((N_Q_HEADS, T, HEAD_DIM), BF16),
                   jax.ShapeDtypeStruct((N_KV_HEADS, T, HEAD_DIM), F32),
                   jax.ShapeDtypeStruct((N_KV_HEADS, T, HEAD_DIM), F32),
                   jax.ShapeDtypeStruct((N_KV_HEADS, 1, HEAD_DIM), F32),
                   jax.ShapeDtypeStruct((N_KV_HEADS, 1, HEAD_DIM), F32),
                   jax.ShapeDtypeStruct((N_KV_HEADS, GQA_GROUP, LANES), F32)],
        compiler_params=_params("parallel", "arbitrary"))(q, k, k, v, v, do, qg, kg, _sink_rows(sinks))


def _ssm_discretize(lam_re, lam_im, log_dt, b_re, b_im, c_re, c_im):
    dt = jnp.exp(log_dt)[:, None]
    mag = jnp.exp(lam_re * dt)
    ar = mag * jnp.cos(lam_im * dt)
    ai = mag * jnp.sin(lam_im * dt)
    den = lam_re * lam_re + lam_im * lam_im
    fr = ((ar - 1.0) * lam_re + ai * lam_im) / den
    fi = (ai * lam_re - (ar - 1.0) * lam_im) / den
    bbar_r = fr[:, :, None] * b_re - fi[:, :, None] * b_im
    bbar_i = fr[:, :, None] * b_im + fi[:, :, None] * b_re
    gl = SSM_GROUPS // SSM_NGB
    eye = jnp.eye(gl, dtype=F32)

    def tiles(a):
        return a.reshape(SSM_NGB, SUBLANES, LANES)

    def bdiag(bb):
        return jnp.einsum("bgph,gk->bghkp", bb.reshape(SSM_NGB, gl, SSM_STATE, SSM_GROUP_CH), eye).reshape(
            SSM_NGB, SSM_GB_CH, SSM_GB_ST)

    def cdiag(cc):
        return jnp.einsum("bghp,gk->bgpkh", cc.reshape(SSM_NGB, gl, SSM_GROUP_CH, SSM_STATE), eye).reshape(
            SSM_NGB, SSM_GB_ST, SSM_GB_CH)

    return tiles(ar), tiles(ai), bdiag(bbar_r), bdiag(bbar_i), cdiag(c_re), cdiag(c_im)


def _to_time_major(dst, val, tt, first_row=0):
    for j in range(SUBLANES):
        dst[pl.ds(first_row + j, tt, stride=SUBLANES), :] = val[:, j * LANES:(j + 1) * LANES]


def _from_time_major(dst, src, tt):
    for j in range(SUBLANES):
        dst[:, j * LANES:(j + 1) * LANES] = src[pl.ds(j, tt, stride=SUBLANES), :]


def _ssm_fwd(z, ar, ai, bbr, bbi, cbr, cbi, dskip, name):
    T = z.shape[0]
    tt = min(SSM_TT, T)
    nt = T // tt

    def body(u_ref, ar_ref, ai_ref, br_ref, bi_ref, cr_ref, ci_ref, d_ref, y_ref, sr_ref, si_ref,
             tmr, tmi, car_r, car_i):
        @pl.when(pl.program_id(1) == 0)
        def _():
            car_r[...] = jnp.zeros_like(car_r)
            car_i[...] = jnp.zeros_like(car_i)

        u = u_ref[...]
        u16 = u.astype(BF16)
        _to_time_major(tmr, _dot(u16, br_ref[...]), tt)
        _to_time_major(tmi, _dot(u16, bi_ref[...]), tt)
        a_r = ar_ref[...]
        a_i = ai_ref[...]

        def step(t, carry):
            s_r, s_i = carry
            rows = pl.ds(pl.multiple_of(t * SUBLANES, SUBLANES), SUBLANES)
            n_r = a_r * s_r - a_i * s_i + tmr[rows, :]
            n_i = a_r * s_i + a_i * s_r + tmi[rows, :]
            tmr[rows, :] = n_r
            tmi[rows, :] = n_i
            return n_r, n_i

        s_r, s_i = lax.fori_loop(0, tt, step, (car_r[...], car_i[...]), unroll=8)
        car_r[...] = s_r
        car_i[...] = s_i
        _from_time_major(sr_ref, tmr, tt)
        _from_time_major(si_ref, tmi, tt)
        y_ref[...] = (_dot(sr_ref[...].astype(BF16), cr_ref[...]) - _dot(si_ref[...].astype(BF16), ci_ref[...])
                      + d_ref[...] * u)

    u_spec = pl.BlockSpec((tt, SSM_GB_CH), lambda b, t: (t, OFF_U // SSM_GB_CH + b))
    a_spec = pl.BlockSpec((None, SUBLANES, LANES), lambda b, t: (b, 0, 0))
    b_spec = pl.BlockSpec((None, SSM_GB_CH, SSM_GB_ST), lambda b, t: (b, 0, 0))
    c_spec = pl.BlockSpec((None, SSM_GB_ST, SSM_GB_CH), lambda b, t: (b, 0, 0))
    d_spec = pl.BlockSpec((1, SSM_GB_CH), lambda b, t: (0, b))
    y_spec = pl.BlockSpec((tt, SSM_GB_CH), lambda b, t: (t, b))
    s_spec = pl.BlockSpec((tt, SSM_GB_ST), lambda b, t: (t, b))
    n_state = SSM_NGB * SSM_GB_ST
    return pl.pallas_call(
        body, name=name, grid=(SSM_NGB, nt),
        in_specs=[u_spec, a_spec, a_spec, b_spec, b_spec, c_spec, c_spec, d_spec],
        out_specs=[y_spec, s_spec, s_spec],
        out_shape=[jax.ShapeDtypeStruct((T, SSM_WIDTH), F32), jax.ShapeDtypeStruct((T, n_state), F32),
                   jax.ShapeDtypeStruct((T, n_state), F32)],
        scratch_shapes=[pltpu.VMEM((tt * SUBLANES, LANES), F32), pltpu.VMEM((tt * SUBLANES, LANES), F32),
                        pltpu.VMEM((SUBLANES, LANES), F32), pltpu.VMEM((SUBLANES, LANES), F32)],
        compiler_params=_params("parallel", "arbitrary"))(z, ar, ai, bbr, bbi, cbr, cbi, dskip)


def _ssm_bwd(dy, z, s_r, s_i, ar, ai, bbr, bbi, cbr, cbi, dskip, name):
    T = z.shape[0]
    tt = min(SSM_TT, T)
    nt = T // tt
    per8 = tt // SUBLANES

    def body(dy_ref, u_ref, sr_ref, si_ref, srp_ref, sip_ref, ar_ref, ai_ref, br_ref, bi_ref, cr_ref, ci_ref, d_ref,
             du_ref, dar_ref, dai_ref, dbr_ref, dbi_ref, dcr_ref, dci_ref, dd_ref,
             tmr, tmi, smr, smi, natr, nati, car_r, car_i):
        tb = pl.program_id(1)
        first_block = tb == nt - 1

        @pl.when(tb == 0)
        def _():
            for ref in (car_r, car_i, dar_ref, dai_ref, dbr_ref, dbi_ref, dcr_ref, dci_ref, dd_ref):
                ref[...] = jnp.zeros_like(ref)

        dy = dy_ref[...]
        dy16 = dy.astype(BF16)
        u = u_ref[...]
        u16 = u.astype(BF16)
        _to_time_major(tmr, _dot(dy16, cr_ref[...], "nt"), tt)
        _to_time_major(tmi, -_dot(dy16, ci_ref[...], "nt"), tt)
        _to_time_major(smr, sr_ref[...], tt, first_row=SUBLANES)
        _to_time_major(smi, si_ref[...], tt, first_row=SUBLANES)
        keep = jnp.where(first_block, 0.0, 1.0)
        for j in range(SUBLANES):
            smr[j:j + 1, :] = srp_ref[SUBLANES - 1:SUBLANES, j * LANES:(j + 1) * LANES] * keep
            smi[j:j + 1, :] = sip_ref[SUBLANES - 1:SUBLANES, j * LANES:(j + 1) * LANES] * keep
        a_r = ar_ref[...]
        a_i = ai_ref[...]

        def step(i, carry):
            n_r, n_i, da_r, da_i = carry
            rows = pl.ds(pl.multiple_of((tt - 1 - i) * SUBLANES, SUBLANES), SUBLANES)
            g_r = tmr[rows, :] + a_r * n_r + a_i * n_i
            g_i = tmi[rows, :] - a_i * n_r + a_r * n_i
            tmr[rows, :] = g_r
            tmi[rows, :] = g_i
            p_r = smr[rows, :]
            p_i = smi[rows, :]
            return g_r, g_i, da_r + g_r * p_r + g_i * p_i, da_i - g_r * p_i + g_i * p_r

        zero = jnp.zeros((SUBLANES, LANES), F32)
        n_r, n_i, da_r, da_i = lax.fori_loop(0, tt, step, (car_r[...], car_i[...], zero, zero), unroll=8)
        car_r[...] = n_r
        car_i[...] = n_i
        dar_ref[...] += da_r
        dai_ref[...] += da_i
        _from_time_major(natr, tmr, tt)
        _from_time_major(nati, tmi, tt)
        dbu_r16 = natr[...].astype(BF16)
        dbu_i16 = nati[...].astype(BF16)
        du_ref[...] = (_dot(dbu_r16, br_ref[...], "nt") + _dot(dbu_i16, bi_ref[...], "nt")
                       + d_ref[...] * dy).astype(BF16)
        dbr_ref[...] += _dot(u16, dbu_r16, "tn")
        dbi_ref[...] += _dot(u16, dbu_i16, "tn")
        dcr_ref[...] += _dot(sr_ref[...].astype(BF16), dy16, "tn")
        dci_ref[...] -= _dot(si_ref[...].astype(BF16), dy16, "tn")
        dd_ref[...] += jnp.sum(dy * u, axis=0, keepdims=True)

    def rev(t):
        return nt - 1 - t

    dy_spec = pl.BlockSpec((tt, SSM_GB_CH), lambda b, t: (rev(t), b))
    u_spec = pl.BlockSpec((tt, SSM_GB_CH), lambda b, t: (rev(t), OFF_U // SSM_GB_CH + b))
    s_spec = pl.BlockSpec((tt, SSM_GB_ST), lambda b, t: (rev(t), b))
    sp_spec = pl.BlockSpec((SUBLANES, SSM_GB_ST), lambda b, t: (jnp.maximum(rev(t) * per8 - 1, 0), b))
    a_spec = pl.BlockSpec((None, SUBLANES, LANES), lambda b, t: (b, 0, 0))
    b_spec = pl.BlockSpec((None, SSM_GB_CH, SSM_GB_ST), lambda b, t: (b, 0, 0))
    c_spec = pl.BlockSpec((None, SSM_GB_ST, SSM_GB_CH), lambda b, t: (b, 0, 0))
    d_spec = pl.BlockSpec((1, SSM_GB_CH), lambda b, t: (0, b))
    tm_shape = pltpu.VMEM((tt * SUBLANES, LANES), F32)
    sm_shape = pltpu.VMEM(((tt + 1) * SUBLANES, LANES), F32)
    nat_shape = pltpu.VMEM((tt, SSM_GB_ST), F32)
    tile = pltpu.VMEM((SUBLANES, LANES), F32)
    return pl.pallas_call(
        body, name=name, grid=(SSM_NGB, nt),
        in_specs=[dy_spec, u_spec, s_spec, s_spec, sp_spec, sp_spec, a_spec, a_spec, b_spec, b_spec, c_spec, c_spec,
                  d_spec],
        out_specs=[dy_spec, a_spec, a_spec, b_spec, b_spec, c_spec, c_spec, d_spec],
        out_shape=[jax.ShapeDtypeStruct((T, SSM_WIDTH), BF16),
                   jax.ShapeDtypeStruct((SSM_NGB, SUBLANES, LANES), F32),
                   jax.ShapeDtypeStruct((SSM_NGB, SUBLANES, LANES), F32),
                   jax.ShapeDtypeStruct((SSM_NGB, SSM_GB_CH, SSM_GB_ST), F32),
                   jax.ShapeDtypeStruct((SSM_NGB, SSM_GB_CH, SSM_GB_ST), F32),
                   jax.ShapeDtypeStruct((SSM_NGB, SSM_GB_ST, SSM_GB_CH), F32),
                   jax.ShapeDtypeStruct((SSM_NGB, SSM_GB_ST, SSM_GB_CH), F32),
                   jax.ShapeDtypeStruct((1, SSM_WIDTH), F32)],
        scratch_shapes=[tm_shape, tm_shape, sm_shape, sm_shape, nat_shape, nat_shape, tile, tile],
        compiler_params=_params("parallel", "arbitrary"))(dy, z, s_r, s_i, s_r, s_i, ar, ai, bbr, bbi, cbr, cbi, dskip)


def _glu_fwd(y, w, b, name):
    T, W = y.shape
    tm = _pick(T, 512)

    def body(y_ref, w_ref, b_ref, pre_ref, y3_ref):
        y2 = _gelu(y_ref[...])
        pre = _dot(y2.astype(BF16), w_ref[...]) + b_ref[...]
        pre_ref[...] = pre
        y3_ref[...] = (y2 * _sigmoid(pre)).astype(BF16)

    row = pl.BlockSpec((tm, W), lambda i: (i, 0))
    return pl.pallas_call(
        body, name=name, grid=(T // tm,),
        in_specs=[row, pl.BlockSpec((W, W), lambda i: (0, 0)), pl.BlockSpec((1, W), lambda i: (0, 0))],
        out_specs=[row, row], out_shape=[jax.ShapeDtypeStruct((T, W), F32), jax.ShapeDtypeStruct((T, W), BF16)],
        compiler_params=_params("parallel"))(y, w, b)


def _glu_bwd_gate(dy3, y, pre, name):
    T, W = y.shape
    tm = _pick(T, 512)

    def body(dy3_ref, y_ref, pre_ref, dpre_ref, t1_ref, y2_ref, db_ref):
        @pl.when(pl.program_id(0) == 0)
        def _():
            db_ref[...] = jnp.zeros_like(db_ref)

        y2 = _gelu(y_ref[...])
        sg = _sigmoid(pre_ref[...])
        dy3 = dy3_ref[...]
        dpre = dy3 * y2 * sg * (1.0 - sg)
        dpre_ref[...] = dpre.astype(BF16)
        t1_ref[...] = dy3 * sg
        y2_ref[...] = y2.astype(BF16)
        db_ref[...] += jnp.sum(dpre, axis=0, keepdims=True)

    row = pl.BlockSpec((tm, W), lambda i: (i, 0))
    vec = pl.BlockSpec((1, W), lambda i: (0, 0))
    return pl.pallas_call(
        body, name=name, grid=(T // tm,), in_specs=[row, row, row], out_specs=[row, row, row, vec],
        out_shape=[jax.ShapeDtypeStruct((T, W), BF16), jax.ShapeDtypeStruct((T, W), F32),
                   jax.ShapeDtypeStruct((T, W), BF16), jax.ShapeDtypeStruct((1, W), F32)],
        compiler_params=_params("arbitrary"))(dy3, y, pre)


def _glu_bwd_in(dpre, w, t1, y, name):
    T, W = y.shape
    tm = _pick(T, 512)

    def body(dpre_ref, w_ref, t1_ref, y_ref, dy_ref):
        dy_ref[...] = (_dot(dpre_ref[...], w_ref[...], "nt") + t1_ref[...]) * _gelu_grad(y_ref[...])

    row = pl.BlockSpec((tm, W), lambda i: (i, 0))
    return pl.pallas_call(
        body, name=name, grid=(T // tm,), in_specs=[row, pl.BlockSpec((W, W), lambda i: (0, 0)), row, row],
        out_specs=row, out_shape=jax.ShapeDtypeStruct((T, W), F32),
        compiler_params=_params("parallel"))(dpre, w, t1, y)


def _merge_fwd(ya, y3, wa, ws, z, bias, name):
    T, W = ya.shape
    D = wa.shape[1]
    tm, tn = _pick(T, 512), _pick(D, 512)

    def body(ya_ref, y3_ref, wa_ref, ws_ref, za_ref, zs_ref, ba_ref, bs_ref, a_ref, b_ref, m_ref):
        a = _dot(ya_ref[...], wa_ref[...])
        b = _dot(y3_ref[...], ws_ref[...])
        a_ref[...] = a
        b_ref[...] = b
        m_ref[...] = (_sigmoid(za_ref[...] + ba_ref[...]) * a + _sigmoid(zs_ref[...] + bs_ref[...]) * b).astype(BF16)

    act = pl.BlockSpec((tm, W), lambda i, j: (i, 0))
    wgt = pl.BlockSpec((W, tn), lambda i, j: (0, j))
    za = pl.BlockSpec((tm, tn), lambda i, j: (i, OFF_G // tn + j))
    zs = pl.BlockSpec((tm, tn), lambda i, j: (i, (OFF_G + D) // tn + j))
    ba = pl.BlockSpec((1, tn), lambda i, j: (0, j))
    bs = pl.BlockSpec((1, tn), lambda i, j: (0, D // tn + j))
    out = pl.BlockSpec((tm, tn), lambda i, j: (i, j))
    return pl.pallas_call(
        body, name=name, grid=(T // tm, D // tn), in_specs=[act, act, wgt, wgt, za, zs, ba, bs],
        out_specs=[out, out, out],
        out_shape=[jax.ShapeDtypeStruct((T, D), F32), jax.ShapeDtypeStruct((T, D), F32),
                   jax.ShapeDtypeStruct((T, D), BF16)],
        compiler_params=_params("parallel", "parallel"))(ya, y3, wa, ws, z, z, bias, bias)


def _merge_bwd(dm, a, b, z, bias, name):
    T, D = dm.shape
    tm, tn = _pick(T, 512), _pick(D, 512)

    def body(dm_ref, a_ref, b_ref, za_ref, zs_ref, ba_ref, bs_ref, da_ref, db_ref, dza_ref, dzs_ref, dba_ref, dbs_ref):
        @pl.when(pl.program_id(1) == 0)
        def _():
            dba_ref[...] = jnp.zeros_like(dba_ref)
            dbs_ref[...] = jnp.zeros_like(dbs_ref)

        dm = dm_ref[...]
        sa = _sigmoid(za_ref[...] + ba_ref[...])
        ss = _sigmoid(zs_ref[...] + bs_ref[...])
        da_ref[...] = (dm * sa).astype(BF16)
        db_ref[...] = (dm * ss).astype(BF16)
        dza = dm * a_ref[...] * sa * (1.0 - sa)
        dzs = dm * b_ref[...] * ss * (1.0 - ss)
        dza_ref[...] = dza.astype(BF16)
        dzs_ref[...] = dzs.astype(BF16)
        dba_ref[...] += jnp.sum(dza, axis=0, keepdims=True)
        dbs_ref[...] += jnp.sum(dzs, axis=0, keepdims=True)

    blk = pl.BlockSpec((tm, tn), lambda j, i: (i, j))
    za = pl.BlockSpec((tm, tn), lambda j, i: (i, OFF_G // tn + j))
    zs = pl.BlockSpec((tm, tn), lambda j, i: (i, (OFF_G + D) // tn + j))
    ba = pl.BlockSpec((1, tn), lambda j, i: (0, j))
    bs = pl.BlockSpec((1, tn), lambda j, i: (0, D // tn + j))
    big = jax.ShapeDtypeStruct((T, D), BF16)
    vec = jax.ShapeDtypeStruct((1, D), F32)
    return pl.pallas_call(
        body, name=name, grid=(D // tn, T // tm), in_specs=[blk, blk, blk, za, zs, ba, bs],
        out_specs=[blk, blk, blk, blk, ba, ba], out_shape=[big, big, big, big, vec, vec],
        compiler_params=_params("parallel", "arbitrary"))(dm, a, b, z, z, bias, bias)


_HALF = N_DEV // 2


def _wgu_block(d):
    return d // 2, d % 2


def _ffn_fwd(h, wgu, name):
    T, D = h.shape
    n = wgu.shape[3]
    F = _HALF * n
    tm = _pick(T, 512)

    def body(h_ref, wg_ref, wu_ref, g_ref, u_ref, act_ref):
        hv = h_ref[...]
        g = _dot(hv, wg_ref[...])
        u = _dot(hv, wu_ref[...])
        g_ref[...] = g
        u_ref[...] = u
        act_ref[...] = (g * _sigmoid(g) * u).astype(BF16)

    wg = pl.BlockSpec((None, None, D, n), lambda j, i: (*_wgu_block(j), 0, 0))
    wu = pl.BlockSpec((None, None, D, n), lambda j, i: (*_wgu_block(j + _HALF), 0, 0))
    out = pl.BlockSpec((tm, n), lambda j, i: (i, j))
    return pl.pallas_call(
        body, name=name, grid=(_HALF, T // tm), in_specs=[pl.BlockSpec((tm, D), lambda j, i: (i, 0)), wg, wu],
        out_specs=[out, out, out],
        out_shape=[jax.ShapeDtypeStruct((T, F), F32), jax.ShapeDtypeStruct((T, F), F32),
                   jax.ShapeDtypeStruct((T, F), BF16)],
        compiler_params=_params("parallel", "parallel"))(h, wgu, wgu)


def _ffn_bwd_in(dg, du, wgu, name):
    T, F = dg.shape
    D, n = wgu.shape[2], wgu.shape[3]
    tm, tn = _pick(T, 1024), _pick(D, 1024)

    def body(dg_ref, du_ref, w_ref, o_ref, acc_ref):
        k = pl.program_id(2)

        @pl.when(k == 0)
        def _():
            acc_ref[...] = jnp.zeros_like(acc_ref)

        @pl.when(k < _HALF)
        def _():
            acc_ref[...] += _dot(dg_ref[...], w_ref[...], "nt")

        @pl.when(k >= _HALF)
        def _():
            acc_ref[...] += _dot(du_ref[...], w_ref[...], "nt")

        @pl.when(k == N_DEV - 1)
        def _():
            o_ref[...] = acc_ref[...]

    dg_spec = pl.BlockSpec((tm, n), lambda i, j, k: (i, jnp.minimum(k, _HALF - 1)))
    du_spec = pl.BlockSpec((tm, n), lambda i, j, k: (i, jnp.maximum(k - _HALF, 0)))
    w_spec = pl.BlockSpec((None, None, tn, n), lambda i, j, k: (*_wgu_block(k), j, 0))
    o_spec = pl.BlockSpec((tm, tn), lambda i, j, k: (i, j))
    return pl.pallas_call(
        body, name=name, grid=(T // tm, D // tn, N_DEV), in_specs=[dg_spec, du_spec, w_spec], out_specs=o_spec,
        out_shape=jax.ShapeDtypeStruct((T, D), F32), scratch_shapes=[pltpu.VMEM((tm, tn), F32)],
        compiler_params=_params("parallel", "parallel", "arbitrary"))(dg, du, wgu)


def _ffn_bwd_w(h, dg, du, name):
    T, D = h.shape
    n = dg.shape[1] // _HALF
    tm, tk = _pick(D, 1024), _pick(T, 2048)
    nk = T // tk

    def body(h_ref, dg_ref, du_ref, o_ref, acc_ref):
        j, k = pl.program_id(0), pl.program_id(2)

        @pl.when(k == 0)
        def _():
            acc_ref[...] = jnp.zeros_like(acc_ref)

        @pl.when(j < _HALF)
        def _():
            acc_ref[...] += _dot(h_ref[...], dg_ref[...], "tn")

        @pl.when(j >= _HALF)
        def _():
            acc_ref[...] += _dot(h_ref[...], du_ref[...], "tn")

        @pl.when(k == nk - 1)
        def _():
            o_ref[...] = acc_ref[...].astype(BF16)

    h_spec = pl.BlockSpec((tk, tm), lambda j, i, k: (k, i))
    dg_spec = pl.BlockSpec((tk, n), lambda j, i, k: (jnp.where(j < _HALF, k, nk - 1), jnp.minimum(j, _HALF - 1)))
    du_spec = pl.BlockSpec((tk, n), lambda j, i, k: (jnp.where(j >= _HALF, k, 0), jnp.maximum(j - _HALF, 0)))
    o_spec = pl.BlockSpec((None, None, tm, n), lambda j, i, k: (*_wgu_block(j), i, 0))
    return pl.pallas_call(
        body, name=name, grid=(N_DEV, D // tm, nk), in_specs=[h_spec, dg_spec, du_spec], out_specs=o_spec,
        out_shape=jax.ShapeDtypeStruct((_HALF, 2, D, n), BF16), scratch_shapes=[pltpu.VMEM((tm, n), F32)],
        compiler_params=_params("parallel", "parallel", "arbitrary"))(h, dg, du)


def _ffn_bwd_act(dx, wo, g, u, name):
    T, D = dx.shape
    F = wo.shape[0]
    tm, tn = _pick(T, 1024), _pick(F, 512)

    def body(dx_ref, wo_ref, g_ref, u_ref, dg_ref, du_ref):
        dact = _dot(dx_ref[...].astype(BF16), wo_ref[...], "nt")
        gv = g_ref[...]
        sg = _sigmoid(gv)
        dg_ref[...] = (dact * u_ref[...] * sg * (1.0 + gv * (1.0 - sg))).astype(BF16)
        du_ref[...] = (dact * gv * sg).astype(BF16)

    out = pl.BlockSpec((tm, tn), lambda i, j: (i, j))
    big = jax.ShapeDtypeStruct((T, F), BF16)
    return pl.pallas_call(
        body, name=name, grid=(T // tm, F // tn),
        in_specs=[pl.BlockSpec((tm, D), lambda i, j: (i, 0)), pl.BlockSpec((tn, D), lambda i, j: (j, 0)), out, out],
        out_specs=[out, out], out_shape=[big, big],
        compiler_params=_params("parallel", "parallel"))(dx, wo, g, u)


def _place():
    x, y, c = lax.axis_index("x"), lax.axis_index("y"), lax.axis_index("c")
    other_chips = [(1 - x, y), (x, 1 - y), (1 - x, 1 - y)]
    return x, y, c, 2 * x + y, other_chips


_ANY = pl.BlockSpec(memory_space=pl.ANY)
_N_COPIES = 7


def _all_gather(shards, name):
    n = len(shards)

    def body(*refs):
        ins, outs = refs[:n], refs[n:2 * n]
        send_sems, recv_sems, local_sems = refs[2 * n:]
        x, y, c, chip, other_chips = _place()
        sibling = (x, y, 1 - c)

        def remote(src, dst, a, j, dev):
            return pltpu.make_async_remote_copy(src_ref=src, dst_ref=dst, send_sem=send_sems.at[a, j],
                                                recv_sem=recv_sems.at[a, j], device_id=dev, device_id_type=MESH)

        sends, local = [], []
        for a in range(n):
            mine = outs[a].at[chip, c]
            local.append(pltpu.make_async_copy(ins[a], mine, local_sems.at[a]))
            local[a].start()
            for j, (ox, oy) in enumerate(other_chips):
                sends.append(remote(ins[a], mine, a, 1 + j, (ox, oy, c)))
                sends[-1].start()
            sends.append(remote(ins[a], mine, a, 0, sibling))
            sends[-1].start()
        for a in range(n):
            for j, (ox, oy) in enumerate(other_chips):
                slot = outs[a].at[2 * ox + oy, c]
                remote(ins[a], slot, a, 1 + j, (ox, oy, c)).wait_recv()
                sends.append(remote(slot, slot, a, 4 + j, sibling))
                sends[-1].start()
        for a in range(n):
            remote(ins[a], outs[a].at[chip, 1 - c], a, 0, sibling).wait_recv()
            for j, (ox, oy) in enumerate(other_chips):
                remote(ins[a], outs[a].at[2 * ox + oy, 1 - c], a, 4 + j, sibling).wait_recv()
        for cp in sends:
            cp.wait_send()
        for a in range(n):
            local[a].wait()

    return pl.pallas_call(
        body, name=name, in_specs=[_ANY] * n, out_specs=[_ANY] * n,
        out_shape=[jax.ShapeDtypeStruct((4, 2) + s.shape, s.dtype) for s in shards],
        scratch_shapes=[pltpu.SemaphoreType.DMA((n, _N_COPIES)), pltpu.SemaphoreType.DMA((n, _N_COPIES)),
                        pltpu.SemaphoreType.DMA((n,))])(*shards)


def _pair_exchange(parts, name):
    n = len(parts)

    def body(*refs):
        ins, theirs = refs[:n], refs[n:2 * n]
        send_sems, recv_sems = refs[2 * n:]
        x, y, c, _, _ = _place()
        sends = []
        for a in range(n):
            for k in range(4):
                sends.append(pltpu.make_async_remote_copy(
                    src_ref=ins[a].at[k, 1 - c], dst_ref=theirs[a].at[k], send_sem=send_sems.at[a, k],
                    recv_sem=recv_sems.at[a, k], device_id=(x, y, 1 - c), device_id_type=MESH))
                sends[-1].start()
        for cp in sends:
            cp.wait_recv()
            cp.wait_send()

    return pl.pallas_call(
        body, name=name, in_specs=[_ANY] * n, out_specs=[_ANY] * n,
        out_shape=[jax.ShapeDtypeStruct((4,) + p.shape[2:], p.dtype) for p in parts],
        scratch_shapes=[pltpu.SemaphoreType.DMA((n, 4)), pltpu.SemaphoreType.DMA((n, 4))])(*parts)


def _chip_exchange(parts, name):
    n = len(parts)

    def body(*refs):
        ins, got = refs[:n], refs[n:2 * n]
        send_sems, recv_sems = refs[2 * n:]
        _, _, c, _, other_chips = _place()
        sends = []
        for a in range(n):
            for j, (ox, oy) in enumerate(other_chips):
                sends.append(pltpu.make_async_remote_copy(
                    src_ref=ins[a].at[2 * ox + oy], dst_ref=got[a].at[j], send_sem=send_sems.at[a, j],
                    recv_sem=recv_sems.at[a, j], device_id=(ox, oy, c), device_id_type=MESH))
                sends[-1].start()
        for cp in sends:
            cp.wait_recv()
            cp.wait_send()

    return pl.pallas_call(
        body, name=name, in_specs=[_ANY] * n, out_specs=[_ANY] * n,
        out_shape=[jax.ShapeDtypeStruct((3,) + p.shape[1:], p.dtype) for p in parts],
        scratch_shapes=[pltpu.SemaphoreType.DMA((n, 3)), pltpu.SemaphoreType.DMA((n, 3))])(*parts)


def _add_pair(core, parts, theirs, name):
    k, _, R, C = parts.shape
    tr = _pick(R, 512, 16)

    def body(core_ref, a_ref, b_ref, o_ref):
        o_ref[...] = (a_ref[...].astype(F32) + b_ref[...].astype(F32)).astype(BF16)

    blk = pl.BlockSpec((None, tr, C), lambda s, i, core_ref: (s, i, 0))
    grid_spec = pltpu.PrefetchScalarGridSpec(
        num_scalar_prefetch=1, grid=(k, R // tr),
        in_specs=[pl.BlockSpec((None, None, tr, C), lambda s, i, core_ref: (s, core_ref[0], i, 0)), blk],
        out_specs=blk)
    return pl.pallas_call(
        body, name=name, grid_spec=grid_spec, out_shape=jax.ShapeDtypeStruct(theirs.shape, BF16),
        compiler_params=_params("parallel", "parallel"))(core, parts, theirs)


def _adamw_math(w, g, m, v):
    m = ADAM_B1 * m + (1.0 - ADAM_B1) * g
    v = ADAM_B2 * v + (1.0 - ADAM_B2) * (g * g)
    m_hat = m / (1.0 - ADAM_B1 ** ADAM_STEP)
    v_hat = v / (1.0 - ADAM_B2 ** ADAM_STEP)
    delta = -ADAM_LR * (m_hat / (jnp.sqrt(v_hat) + ADAM_EPS) + ADAM_WD * w)
    return delta, m, v


def _scattered_pieces(sums, got):
    return [("own", sums)] + [("peer%d" % j, got) for j in range(3)]


def _piece_spec(kind, tr, C):
    if kind == "own":
        return pl.BlockSpec((None, tr, C), lambda i, chip_ref: (chip_ref[0], i, 0))
    if kind == "plain":
        return pl.BlockSpec((tr, C), lambda i, chip_ref: (i, 0))
    return pl.BlockSpec((None, tr, C), functools.partial(lambda j, i, chip_ref: (j, i, 0), int(kind[-1])))


def _grad_sum(chip, pieces, name):
    R, C = pieces[0][1].shape[-2:]
    tr = _pick(R, 256, 16)

    def body(chip_ref, *refs):
        g = refs[0][...].astype(F32)
        for p in refs[1:-1]:
            g = g + p[...].astype(F32)
        refs[-1][...] = g

    grid_spec = pltpu.PrefetchScalarGridSpec(
        num_scalar_prefetch=1, grid=(R // tr,), in_specs=[_piece_spec(kind, tr, C) for kind, _ in pieces],
        out_specs=_piece_spec("plain", tr, C))
    return pl.pallas_call(
        body, name=name, grid_spec=grid_spec, out_shape=jax.ShapeDtypeStruct((R, C), F32),
        compiler_params=_params("parallel"))(chip, *[a for _, a in pieces])


def _adamw_shard(chip, w, m, v, layer, pieces, so_far, name):
    L, R, C = w.shape
    tr = _pick(R, 256, 16)
    n_p = len(pieces)

    def body(chip_ref, w_ref, m_ref, v_ref, *rest):
        g = rest[0][...].astype(F32)
        for p in rest[1:n_p]:
            g = g + p[...].astype(F32)
        delta, nm, nv = _adamw_math(w_ref[...], g, m_ref[...], v_ref[...])
        g_ref, d_ref, nm_ref, nv_ref = rest[-4:]
        g_ref[...] = g
        d_ref[...] = delta
        nm_ref[...] = nm
        nv_ref[...] = nv

    state = pl.BlockSpec((None, tr, C), lambda i, chip_ref: (layer, i, 0))
    carried = [] if so_far is None else list(so_far)
    first_carried = 1 + 3 + n_p
    grid_spec = pltpu.PrefetchScalarGridSpec(
        num_scalar_prefetch=1, grid=(R // tr,),
        in_specs=[state] * 3 + [_piece_spec(kind, tr, C) for kind, _ in pieces] + [_ANY] * len(carried),
        out_specs=[state] * 4)
    return pl.pallas_call(
        body, name=name, grid_spec=grid_spec, out_shape=[jax.ShapeDtypeStruct((L, R, C), F32)] * 4,
        input_output_aliases={first_carried + t: t for t in range(len(carried))},
        compiler_params=_params("parallel"))(chip, w, m, v, *[a for _, a in pieces], *carried)


def _adamw_small(w, m, v, gathered, name):
    R = w.shape[0]
    tr = _pick(R, 512, SUBLANES)

    def body(w_ref, m_ref, v_ref, gg_ref, g_ref, d_ref, nm_ref, nv_ref):
        g = gg_ref[0, 0]
        for k in range(4):
            for c in range(2):
                if c or k:
                    g = g + gg_ref[k, c]
        delta, nm, nv = _adamw_math(w_ref[...], g, m_ref[...], v_ref[...])
        g_ref[...] = g
        d_ref[...] = delta
        nm_ref[...] = nm
        nv_ref[...] = nv

    row = pl.BlockSpec((tr, LANES), lambda i: (i, 0))
    out = jax.ShapeDtypeStruct((R, LANES), F32)
    return pl.pallas_call(
        body, name=name, grid=(R // tr,),
        in_specs=[row, row, row, pl.BlockSpec((4, 2, tr, LANES), lambda i: (0, 0, i, 0))], out_specs=[row] * 4,
        out_shape=[out] * 4, compiler_params=_params("parallel"))(w, m, v, gathered)


def _to_heads(a, n_heads):
    return a.reshape(a.shape[0], n_heads, HEAD_DIM).transpose(1, 0, 2)


def _from_heads(a):
    return a.transpose(1, 0, 2).reshape(a.shape[1], a.shape[0] * HEAD_DIM)


def _layer_fwd(x, w, s, tag):
    h = _rms_fwd(x, s["norm_mix_g"], f"rms_mix_{tag}")
    z = _mm(h, w["w_in_t"], "nt", F32, f"in_proj_{tag}", tn=512)
    q = _to_heads(z[:, :OFF_K], N_Q_HEADS)
    k = _to_heads(z[:, OFF_K:OFF_V], N_KV_HEADS)
    v = _to_heads(z[:, OFF_V:OFF_U], N_KV_HEADS)
    ya = _from_heads(_attn_fwd(q, k, v, s["q_norm_g"], s["k_norm_g"], s["attn_sinks"], f"attn_fwd_{tag}"))
    y, s_r, s_i = _ssm_fwd(z, *s["ssm16"], s["ssm_d"], f"ssm_fwd_{tag}")
    pre, y3 = _glu_fwd(y, w["ssm_glu_w"], s["ssm_glu_b"], f"glu_fwd_{tag}")
    a, b, merged = _merge_fwd(ya, y3, w["w_attn_branch"], w["w_ssm_branch"], z, s["gate_bias"], f"merge_fwd_{tag}")
    x1 = _mm(merged, w["w_out"], "nn", F32, f"out_proj_{tag}", residual=x)
    h2 = _rms_fwd(x1, s["norm_ffn_g"], f"rms_ffn_{tag}")
    g, u, act = _ffn_fwd(h2, w["w_ffn_in"], f"ffn_fwd_{tag}")
    x2 = _mm(act, w["w_ffn_out"], "nn", F32, f"ffn_out_{tag}", residual=x1)
    saved = dict(x=x, h=h, z=z, q=q, k=k, v=v, ya=ya, y=y, s_r=s_r, s_i=s_i, pre=pre, y3=y3, a=a, b=b, merged=merged,
                 x1=x1, h2=h2, g=g, u=u, act=act)
    return x2, saved


def _layer_bwd(dx2, sv, w, s, tag):
    gw, gs = {}, {}
    dg16, du16 = _ffn_bwd_act(dx2, w["w_ffn_out"], sv["g"], sv["u"], f"ffn_bwd_act_{tag}")
    gw["w_ffn_out"] = _mm(sv["act"], dx2, "tn", BF16, f"dw_ffn_out_{tag}", tm=1408)
    dh2 = _ffn_bwd_in(dg16, du16, w["w_ffn_in"], f"dh2_{tag}")
    gw["w_ffn_in"] = _ffn_bwd_w(sv["h2"], dg16, du16, f"dw_ffn_in_{tag}")
    dx1, gs["norm_ffn_g"] = _rms_bwd(dh2, sv["x1"], s["norm_ffn_g"], dx2, f"rms_ffn_bwd_{tag}")
    dm = _mm(dx1, w["w_out"], "nt", F32, f"dmerged_{tag}")
    gw["w_out"] = _mm(sv["merged"], dx1, "tn", BF16, f"dw_out_{tag}")
    da16, db16, dza, dzs, dba, dbs = _merge_bwd(dm, sv["a"], sv["b"], sv["z"], s["gate_bias"], f"merge_bwd_{tag}")
    gs["gate_bias"] = jnp.concatenate([dba, dbs], axis=1)
    dya = _mm(da16, w["w_attn_branch"], "nt", F32, f"dya_{tag}")
    gw["w_attn_branch"] = _mm(sv["ya"], da16, "tn", BF16, f"dw_attn_branch_{tag}")
    dy3 = _mm(db16, w["w_ssm_branch"], "nt", F32, f"dy3_{tag}")
    gw["w_ssm_branch"] = _mm(sv["y3"], db16, "tn", BF16, f"dw_ssm_branch_{tag}")
    dpre16, t1, y2_16, gs["ssm_glu_b"] = _glu_bwd_gate(dy3, sv["y"], sv["pre"], f"glu_bwd_gate_{tag}")
    dy = _glu_bwd_in(dpre16, w["ssm_glu_w"], t1, sv["y"], f"glu_bwd_in_{tag}")
    gw["ssm_glu_w"] = _mm(y2_16, dpre16, "tn", BF16, f"dw_glu_{tag}")
    du_ssm, *gs["ssm_disc"], gs["ssm_d"] = _ssm_bwd(dy, sv["z"], sv["s_r"], sv["s_i"], *s["ssm16"], s["ssm_d"],
                                                  f"ssm_bwd_{tag}")
    dq, dk, dv, dqg, dkg, dsk = _attn_bwd(sv["q"], sv["k"], sv["v"], _to_heads(dya, N_Q_HEADS), s["q_norm_g"],
                                          s["k_norm_g"], s["attn_sinks"], f"attn_bwd_{tag}")
    gs["q_norm_g"] = jnp.sum(dqg, axis=0)
    gs["k_norm_g"] = jnp.sum(dkg, axis=0)
    gs["attn_sinks"] = dsk[:, :, 0].reshape(1, N_Q_HEADS)
    dz = jnp.concatenate([_from_heads(dq), _from_heads(dk).astype(BF16), _from_heads(dv).astype(BF16), du_ssm, dza,
                          dzs], axis=1)
    dh = _mm(dz, w["w_in_t"], "nn", F32, f"dh_{tag}")
    gw["w_in"] = _mm(dz, sv["h"], "tn", BF16, f"dw_in_t_{tag}", tm=1664)
    dx, gs["norm_mix_g"] = _rms_bwd(dh, sv["x"], s["norm_mix_g"], dx1, f"rms_mix_bwd_{tag}")
    return dx, gw, gs


def _shard_to_send(name, shard):
    return (shard.T if name == "w_in" else shard).astype(BF16)


def _assemble(name, gathered):
    if name == "w_ffn_in":
        return gathered
    if name in COL_SHARDED and name != "w_in":
        rows = gathered.shape[2]
        return gathered.transpose(2, 0, 1, 3).reshape(rows, -1)
    return gathered.reshape(-1, gathered.shape[3])


def _disassemble(name, grad):
    if name == "w_ffn_in":
        return grad
    if name in COL_SHARDED and name != "w_in":
        rows, cols = grad.shape
        return grad.reshape(rows, 4, 2, cols // N_DEV).transpose(1, 2, 0, 3)
    rows, cols = grad.shape
    return grad.reshape(4, 2, rows // N_DEV, cols)


def _pack(arrays):
    flat = jnp.concatenate([a.reshape(-1) for a in arrays])
    pad = (-flat.shape[0]) % (SUBLANES * LANES)
    return jnp.pad(flat, (0, pad)).reshape(-1, LANES)


def _unpack(packed, like):
    flat, out, off = packed.reshape(-1), [], 0
    for a in like:
        out.append(flat[off:off + a.size].reshape(a.shape))
        off += a.size
    return out


def kernel(x, norm_mix_g, w_in, gate_bias, q_norm_g, k_norm_g, attn_sinks, ssm_lambda_re, ssm_lambda_im, ssm_log_dt, ssm_b_re, ssm_b_im, ssm_c_re, ssm_c_im, ssm_d, ssm_glu_w, ssm_glu_b, w_attn_branch, w_ssm_branch, w_out, norm_ffn_g, w_ffn_in, w_ffn_out, loss_target, m_norm_mix_g, m_w_in, m_gate_bias, m_q_norm_g, m_k_norm_g, m_attn_sinks, m_ssm_lambda_re, m_ssm_lambda_im, m_ssm_log_dt, m_ssm_b_re, m_ssm_b_im, m_ssm_c_re, m_ssm_c_im, m_ssm_d, m_ssm_glu_w, m_ssm_glu_b, m_w_attn_branch, m_w_ssm_branch, m_w_out, m_norm_ffn_g, m_w_ffn_in, m_w_ffn_out, v_norm_mix_g, v_w_in, v_gate_bias, v_q_norm_g, v_k_norm_g, v_attn_sinks, v_ssm_lambda_re, v_ssm_lambda_im, v_ssm_log_dt, v_ssm_b_re, v_ssm_b_im, v_ssm_c_re, v_ssm_c_im, v_ssm_d, v_ssm_glu_w, v_ssm_glu_b, v_w_attn_branch, v_w_ssm_branch, v_w_out, v_norm_ffn_g, v_w_ffn_in, v_w_ffn_out):
    given = dict(locals())
    wts = {n: given[n] for n in WEIGHTS}
    mom = {n: given["m_" + n] for n in WEIGHTS}
    var = {n: given["v_" + n] for n in WEIGHTS}
    depth = w_in.shape[0]
    xs = x[0]
    target = loss_target[0]

    full = []
    for l in range(depth):
        gathered = _all_gather([_shard_to_send(n, wts[n][l]) for n in BIG], f"gather_weights_{l}")
        wl = {n: _assemble(n, g) for n, g in zip(BIG, gathered)}
        wl["w_in_t"] = wl.pop("w_in")
        full.append(wl)

    small, disc_vjp = [], []
    for l in range(depth):
        s = {n: wts[n][l].reshape(1, -1) for n in ("norm_mix_g", "gate_bias", "q_norm_g", "k_norm_g", "attn_sinks",
                                                   "ssm_d", "ssm_glu_b", "norm_ffn_g")}
        disc, vjp = jax.vjp(_ssm_discretize, *[wts[n][l] for n in ("ssm_lambda_re", "ssm_lambda_im", "ssm_log_dt",
                                                                  "ssm_b_re", "ssm_b_im", "ssm_c_re", "ssm_c_im")])
        s["ssm16"] = (disc[0], disc[1]) + tuple(d.astype(BF16) for d in disc[2:])
        small.append(s)
        disc_vjp.append(vjp)

    act, saved = xs, []
    for l in range(depth):
        act, sv = _layer_fwd(act, full[l], small[l], str(l))
        saved.append(sv)
    dact, loss_local = _loss_grad(act, target, "loss_head")
    big_grads, small_grads = [None] * depth, [None] * depth
    for l in reversed(range(depth)):
        dact, gw, gs = _layer_bwd(dact, saved[l], full[l], small[l], str(l))
        (gs["ssm_lambda_re"], gs["ssm_lambda_im"], gs["ssm_log_dt"], gs["ssm_b_re"], gs["ssm_b_im"], gs["ssm_c_re"],
         gs["ssm_c_im"]) = disc_vjp[l](tuple(gs.pop("ssm_disc")))
        big_grads[l], small_grads[l] = gw, gs
    loss = lax.psum(loss_local, ("x", "y", "c"))

    out = {"grad": {}, "delta": {}, "new_m": {}, "new_v": {}}
    results = {n: None for n in BIG}
    core = lax.axis_index("c").astype(jnp.int32).reshape(1)
    chip = (2 * lax.axis_index("x") + lax.axis_index("y")).astype(jnp.int32).reshape(1)
    for l in range(depth):
        parts = [_disassemble(n, big_grads[l][n]) for n in BIG]
        theirs = _pair_exchange(parts, f"grad_pair_exchange_{l}")
        sums = [_add_pair(core, p, t, f"grad_pair_sum_{n}_{l}") for n, p, t in zip(BIG, parts, theirs)]
        got = _chip_exchange(sums, f"grad_chip_exchange_{l}")
        for n, sm, g in zip(BIG, sums, got):
            pieces = _scattered_pieces(sm, g)
            if n == "w_in":
                pieces = [("plain", _grad_sum(chip, pieces, f"grad_sum_w_in_{l}").T)]
            results[n] = _adamw_shard(chip, wts[n], mom[n], var[n], l, pieces, results[n], f"adamw_{n}_{l}")
    for n in BIG:
        for kind, res in zip(("grad", "delta", "new_m", "new_v"), results[n]):
            out[kind][n] = res

    like = [wts[n] for n in SMALL]
    g_small = _pack([jnp.stack([small_grads[l][n].reshape(wts[n].shape[1:]) for l in range(depth)]) for n in SMALL])
    (gathered_small,) = _all_gather([g_small], "gather_small_grads")
    res = _adamw_small(_pack(like), _pack([mom[n] for n in SMALL]), _pack([var[n] for n in SMALL]), gathered_small,
                       "adamw_small")
    for kind, packed in zip(("grad", "delta", "new_m", "new_v"), res):
        for n, a in zip(SMALL, _unpack(packed, like)):
            out[kind][n] = a

    grad_x = dact.reshape(x.shape)
    return (loss, grad_x, *[out["grad"][n] for n in WEIGHTS], *[out["delta"][n] for n in WEIGHTS],
            *[out["new_m"][n] for n in WEIGHTS], *[out["new_v"][n] for n in WEIGHTS])
```

```python
import functools
import math

import jax
import jax.numpy as jnp
from jax import lax
from jax.experimental import pallas as pl
from jax.experimental.pallas import tpu as pltpu

F32, BF16 = jnp.float32, jnp.bfloat16
MESH = pl.DeviceIdType.MESH

D_MODEL = 2048
HEAD_DIM = 64
N_Q_HEADS = 16
N_KV_HEADS = 4
GQA_GROUP = N_Q_HEADS // N_KV_HEADS
ATTN_WIDTH = N_Q_HEADS * HEAD_DIM
KV_WIDTH = N_KV_HEADS * HEAD_DIM
WINDOW = 128
BLOCK = 128
SSM_WIDTH = D_MODEL // 2
SSM_GROUP_CH = 16
SSM_GROUPS = SSM_WIDTH // SSM_GROUP_CH
SSM_STATE = 64
D_FF = 5632
OFF_K = ATTN_WIDTH
OFF_V = OFF_K + KV_WIDTH
OFF_U = OFF_V + KV_WIDTH
OFF_G = OFF_U + SSM_WIDTH
IN_WIDTH = OFF_G + 2 * D_MODEL
RMS_EPS = 1e-6
ATTN_SCALE = HEAD_DIM ** -0.5
NEG_BIG = -1e30

SSM_NGB = 4
SSM_GB_CH = SSM_WIDTH // SSM_NGB
SSM_GB_ST = SSM_GROUPS * SSM_STATE // SSM_NGB
SUBLANES = 8
LANES = 128
SSM_TT = 256

ADAM_LR = 0.001
ADAM_B1 = 0.9
ADAM_B2 = 0.999
ADAM_EPS = 1e-08
ADAM_WD = 0.01
ADAM_STEP = 10

N_DEV = 8
VMEM_LIMIT_BYTES = 52 * 1024 * 1024

BIG = ("w_in", "ssm_glu_w", "w_attn_branch", "w_ssm_branch", "w_out", "w_ffn_in", "w_ffn_out")
COL_SHARDED = ("w_in", "w_attn_branch", "w_ssm_branch", "w_ffn_in")
FFN_WEIGHTS = ("w_ffn_in", "w_ffn_out")
MIXER_WEIGHTS = ("w_in", "ssm_glu_w", "w_attn_branch", "w_ssm_branch", "w_out")
GATHER_HOSTS = {"in_proj": ("w_in",), "attn_fwd": ("w_ffn_in",), "ssm_fwd": ("w_ffn_out", "w_out"),
                "merge_fwd": ("ssm_glu_w", "w_attn_branch", "w_ssm_branch")}
SCATTER_HOSTS = {"ffn_bwd_act": ("ssm_glu_w", "w_attn_branch", "w_ssm_branch", "w_out"), "dh2": ("w_in",)}
SMALL = ("norm_mix_g", "gate_bias", "q_norm_g", "k_norm_g", "attn_sinks", "ssm_lambda_re", "ssm_lambda_im",
         "ssm_log_dt", "ssm_b_re", "ssm_b_im", "ssm_c_re", "ssm_c_im", "ssm_d", "ssm_glu_b", "norm_ffn_g")
WEIGHTS = ("norm_mix_g", "w_in", "gate_bias", "q_norm_g", "k_norm_g", "attn_sinks", "ssm_lambda_re", "ssm_lambda_im",
           "ssm_log_dt", "ssm_b_re", "ssm_b_im", "ssm_c_re", "ssm_c_im", "ssm_d", "ssm_glu_w", "ssm_glu_b",
           "w_attn_branch", "w_ssm_branch", "w_out", "norm_ffn_g", "w_ffn_in", "w_ffn_out")


def _pick(n, target, mult=LANES):
    best = None
    for t in range(mult, min(n, target) + 1, mult):
        if n % t == 0:
            best = t
    return n if best is None else best


def _params(*sem):
    return pltpu.CompilerParams(dimension_semantics=sem, vmem_limit_bytes=VMEM_LIMIT_BYTES)


def _sigmoid(v):
    return 1.0 / (1.0 + jnp.exp(-v))


_GELU_C = math.sqrt(2.0 / math.pi)


def _gelu(v):
    return 0.5 * v * (1.0 + jnp.tanh(_GELU_C * (v + 0.044715 * v * v * v)))


def _gelu_grad(v):
    t = jnp.tanh(_GELU_C * (v + 0.044715 * v * v * v))
    return 0.5 * (1.0 + t) + 0.5 * v * (1.0 - t * t) * _GELU_C * (1.0 + 3.0 * 0.044715 * v * v)


_DN = {"nn": (((1,), (0,)), ((), ())), "nt": (((1,), (1,)), ((), ())), "tn": (((0,), (0,)), ((), ()))}


def _dot(a, b, dims="nn"):
    return lax.dot_general(a, b, _DN[dims], preferred_element_type=F32)


class _Comm:
    def __init__(self, args, out_shapes, sems, copies, aliases=None):
        self.args, self.out_shapes, self.sems = list(args), list(out_shapes), list(sems)
        self.copies, self.aliases = copies, dict(aliases or {})


def _call(body, name, grid, in_specs, out_specs, out_shape, scratch_shapes, semantics, args, comm=None):
    in_specs, out_specs, out_shape = list(in_specs), list(out_specs), list(out_shape)
    scratch_shapes = list(scratch_shapes)
    if comm is None:
        res = pl.pallas_call(body, name=name, grid=grid, in_specs=in_specs, out_specs=out_specs, out_shape=out_shape,
                             scratch_shapes=scratch_shapes, compiler_params=_params(*semantics))(*args)
        return list(res), []
    n_in, n_out, n_scr = len(in_specs), len(out_specs), len(scratch_shapes)
    n_cin, n_cout = len(comm.args), len(comm.out_shapes)

    def carrying(*refs):
        ins, cin = refs[:n_in], refs[n_in:n_in + n_cin]
        o0 = n_in + n_cin
        outs, cout = refs[o0:o0 + n_out], refs[o0 + n_out:o0 + n_out + n_cout]
        s0 = o0 + n_out + n_cout
        scr, sems = refs[s0:s0 + n_scr], refs[s0 + n_scr:]
        first = functools.reduce(jnp.logical_and, [pl.program_id(d) == 0 for d in range(len(grid))])
        last = functools.reduce(jnp.logical_and, [pl.program_id(d) == grid[d] - 1 for d in range(len(grid))])

        @pl.when(first)
        def _():
            sends, _, local = comm.copies(cin, cout, sems, False)
            for cp in local + sends:
                cp.start()

        body(*ins, *outs, *scr)

        @pl.when(last)
        def _():
            sends, recvs, local = comm.copies(cin, cout, sems, True)
            for cp in recvs:
                cp.wait_recv()
            for cp in sends:
                cp.wait_send()
            for cp in local:
                cp.wait()

    res = pl.pallas_call(
        carrying, name=name, grid=grid, in_specs=in_specs + [_ANY] * n_cin, out_specs=out_specs + [_ANY] * n_cout,
        out_shape=out_shape + comm.out_shapes, scratch_shapes=scratch_shapes + comm.sems,
        input_output_aliases={n_in + i: n_out + o for i, o in comm.aliases.items()},
        compiler_params=_params(*["arbitrary"] * len(grid)))(*args, *comm.args)
    return list(res[:n_out]), list(res[n_out:])


def _mm(a, b, dims, out_dtype, name, residual=None, tm=1024, tn=1024, tk=2048, comm=None):
    if dims == "tn":
        K, M = a.shape
    else:
        M, K = a.shape
    N = b.shape[0] if dims == "nt" else b.shape[1]
    tm, tn, tk = _pick(M, tm), _pick(N, tn), _pick(K, tk)
    nk = K // tk
    has_res = residual is not None

    def finish(out, refs):
        if has_res:
            out = out + refs[2][...].astype(F32)
        refs[-2][...] = out.astype(out_dtype)

    def body_single(*refs):
        finish(_dot(refs[0][...].astype(BF16), refs[1][...].astype(BF16), dims), refs)

    def body_multi(*refs):
        acc_ref = refs[-1]
        k = pl.program_id(2)

        @pl.when(k == 0)
        def _():
            acc_ref[...] = jnp.zeros_like(acc_ref)

        acc_ref[...] += _dot(refs[0][...].astype(BF16), refs[1][...].astype(BF16), dims)

        @pl.when(k == nk - 1)
        def _():
            finish(acc_ref[...], refs)

    a_spec = (pl.BlockSpec((tk, tm), lambda i, j, k: (k, i)) if dims == "tn"
              else pl.BlockSpec((tm, tk), lambda i, j, k: (i, k)))
    b_spec = (pl.BlockSpec((tn, tk), lambda i, j, k: (j, k)) if dims == "nt"
              else pl.BlockSpec((tk, tn), lambda i, j, k: (k, j)))
    o_spec = pl.BlockSpec((tm, tn), lambda i, j, k: (i, j))
    in_specs = [a_spec, b_spec] + ([o_spec] if has_res else [])
    args = (a, b) + ((residual,) if has_res else ())
    (out,), extra = _call(
        body_single if nk == 1 else body_multi, name, (M // tm, N // tn, nk), in_specs, [o_spec],
        [jax.ShapeDtypeStruct((M, N), out_dtype)], [pltpu.VMEM((tm, tn) if nk > 1 else (SUBLANES, LANES), F32)],
        ("parallel", "parallel", "arbitrary"), args, comm)
    return out if comm is None else (out, extra)


def _rms_fwd(x, g, name):
    T, D = x.shape
    tr = _pick(T, 256, SUBLANES)

    def body(x_ref, g_ref, o_ref):
        xf = x_ref[...]
        r = lax.rsqrt(jnp.mean(xf * xf, axis=-1, keepdims=True) + RMS_EPS)
        o_ref[...] = (xf * r * g_ref[...]).astype(BF16)

    return pl.pallas_call(
        body, name=name, grid=(T // tr,),
        in_specs=[pl.BlockSpec((tr, D), lambda i: (i, 0)), pl.BlockSpec((1, D), lambda i: (0, 0))],
        out_specs=pl.BlockSpec((tr, D), lambda i: (i, 0)), out_shape=jax.ShapeDtypeStruct((T, D), BF16),
        compiler_params=_params("parallel"))(x, g)


def _rms_bwd(dh, x, g, dres, name):
    T, D = x.shape
    tr = _pick(T, 256, SUBLANES)

    def body(dh_ref, x_ref, g_ref, dres_ref, dx_ref, dg_ref):
        @pl.when(pl.program_id(0) == 0)
        def _():
            dg_ref[...] = jnp.zeros_like(dg_ref)

        xf = x_ref[...]
        r = lax.rsqrt(jnp.mean(xf * xf, axis=-1, keepdims=True) + RMS_EPS)
        xhat = xf * r
        dhv = dh_ref[...]
        dxh = dhv * g_ref[...]
        dx_ref[...] = dres_ref[...] + r * (dxh - xhat * jnp.mean(dxh * xhat, axis=-1, keepdims=True))
        dg_ref[...] += jnp.sum(dhv * xhat, axis=0, keepdims=True)

    row = pl.BlockSpec((tr, D), lambda i: (i, 0))
    vec = pl.BlockSpec((1, D), lambda i: (0, 0))
    return pl.pallas_call(
        body, name=name, grid=(T // tr,), in_specs=[row, row, vec, row], out_specs=[row, vec],
        out_shape=[jax.ShapeDtypeStruct((T, D), F32), jax.ShapeDtypeStruct((1, D), F32)],
        compiler_params=_params("arbitrary"))(dh, x, g, dres)


def _loss_grad(y, target, name):
    T, D = y.shape
    tr = _pick(T, 256, SUBLANES)

    def body(y_ref, t_ref, dx_ref, loss_ref):
        @pl.when(pl.program_id(0) == 0)
        def _():
            loss_ref[...] = jnp.zeros_like(loss_ref)

        err = y_ref[...] - t_ref[...]
        dx_ref[...] = err * (1.0 / D)
        loss_ref[...] += jnp.sum(jnp.mean(err * err, axis=-1, keepdims=True), axis=0, keepdims=True) * 0.5

    row = pl.BlockSpec((tr, D), lambda i: (i, 0))
    one = pl.BlockSpec((1, 1), lambda i: (0, 0))
    dx, loss = pl.pallas_call(
        body, name=name, grid=(T // tr,), in_specs=[row, row], out_specs=[row, one],
        out_shape=[jax.ShapeDtypeStruct((T, D), F32), jax.ShapeDtypeStruct((1, 1), F32)],
        compiler_params=_params("arbitrary"))(y, target)
    return dx, loss[0, 0]


def _alibi_slope(h):
    return 2.0 ** (-8.0 * (h + 1) / N_Q_HEADS)


def _attn_mask():
    row = lax.broadcasted_iota(jnp.int32, (BLOCK, 2 * BLOCK), 0)
    col = lax.broadcasted_iota(jnp.int32, (BLOCK, 2 * BLOCK), 1)
    dist = row - col + BLOCK
    return dist.astype(F32), (dist >= 0) & (dist < WINDOW), col


def _head_norm(v, gain):
    r = lax.rsqrt(jnp.mean(v * v, axis=-1, keepdims=True) + RMS_EPS)
    vhat = v * r
    return r, vhat, vhat * gain


def _attn_probs(qn16, kn16, slope, dist, valid, sink):
    s = _dot(qn16, kn16, "nt") * ATTN_SCALE - slope * dist
    s = jnp.where(valid, s, NEG_BIG)
    m = jnp.maximum(jnp.max(s, axis=-1, keepdims=True), sink)
    p = jnp.exp(s - m)
    ps = jnp.exp(sink - m)
    den = jnp.sum(p, axis=-1, keepdims=True) + ps
    return p, ps, den


def _attn_specs(T):
    q_spec = pl.BlockSpec((GQA_GROUP, BLOCK, HEAD_DIM), lambda h, n: (h, n, 0))
    cur = pl.BlockSpec((None, BLOCK, HEAD_DIM), lambda h, n: (h, n, 0))
    prev = pl.BlockSpec((None, BLOCK, HEAD_DIM), lambda h, n: (h, jnp.maximum(n - 1, 0), 0))
    gain = pl.BlockSpec((1, HEAD_DIM), lambda h, n: (0, 0))
    sink = pl.BlockSpec((None, GQA_GROUP, LANES), lambda h, n: (h, 0, 0))
    return q_spec, cur, prev, gain, sink


def _sink_rows(sinks):
    return jnp.broadcast_to(sinks.reshape(N_KV_HEADS, GQA_GROUP, 1), (N_KV_HEADS, GQA_GROUP, LANES))


def _attn_fwd(q, k, v, qg, kg, sinks, name, comm=None):
    T = q.shape[1]
    nb = T // BLOCK

    def body(q_ref, kc_ref, kp_ref, vc_ref, vp_ref, qg_ref, kg_ref, sk_ref, o_ref):
        kv = pl.program_id(0)
        n = pl.program_id(1)
        dist, valid, col = _attn_mask()
        valid = valid & ((col >= BLOCK) | (n > 0))
        kk = jnp.concatenate([kp_ref[...], kc_ref[...]], axis=0)
        _, _, kn = _head_norm(kk, kg_ref[...])
        kn16 = kn.astype(BF16)
        v16 = jnp.concatenate([vp_ref[...], vc_ref[...]], axis=0).astype(BF16)
        for g in range(GQA_GROUP):
            slope = sum(jnp.where(kv == j, _alibi_slope(j * GQA_GROUP + g), 0.0) for j in range(N_KV_HEADS))
            _, _, qn = _head_norm(q_ref[g], qg_ref[...])
            sink = sk_ref[g:g + 1, 0:1]
            p, _, den = _attn_probs(qn.astype(BF16), kn16, slope, dist, valid, sink)
            o_ref[g] = (_dot((p / den).astype(BF16), v16) ).astype(BF16)

    q_spec, cur, prev, gain, sink = _attn_specs(T)
    (out,), extra = _call(
        body, name, (N_KV_HEADS, nb), [q_spec, cur, prev, cur, prev, gain, gain, sink], [q_spec],
        [jax.ShapeDtypeStruct((N_Q_HEADS, T, HEAD_DIM), BF16)], [], ("parallel", "parallel"),
        (q, k, k, v, v, qg, kg, _sink_rows(sinks)), comm)
    return out if comm is None else (out, extra)


def _attn_bwd(q, k, v, do, qg, kg, sinks, name, comm=None):
    T = q.shape[1]
    nb = T // BLOCK

    def body(q_ref, kc_ref, kp_ref, vc_ref, vp_ref, do_ref, qg_ref, kg_ref, sk_ref,
             dq_ref, dk_ref, dv_ref, dqg_ref, dkg_ref, dsk_ref):
        kv = pl.program_id(0)
        n = pl.program_id(1)

        @pl.when(n == 0)
        def _():
            dk_ref[...] = jnp.zeros_like(dk_ref)
            dv_ref[...] = jnp.zeros_like(dv_ref)
            dqg_ref[...] = jnp.zeros_like(dqg_ref)
            dkg_ref[...] = jnp.zeros_like(dkg_ref)
            dsk_ref[...] = jnp.zeros_like(dsk_ref)

        dist, valid, col = _attn_mask()
        valid = valid & ((col >= BLOCK) | (n > 0))
        kk = jnp.concatenate([kp_ref[...], kc_ref[...]], axis=0)
        rk, khat, kn = _head_norm(kk, kg_ref[...])
        kn16 = kn.astype(BF16)
        v16 = jnp.concatenate([vp_ref[...], vc_ref[...]], axis=0).astype(BF16)
        dkn = jnp.zeros((2 * BLOCK, HEAD_DIM), F32)
        dv = jnp.zeros((2 * BLOCK, HEAD_DIM), F32)
        dqg = jnp.zeros((1, HEAD_DIM), F32)
        for g in range(GQA_GROUP):
            slope = sum(jnp.where(kv == j, _alibi_slope(j * GQA_GROUP + g), 0.0) for j in range(N_KV_HEADS))
            rq, qhat, qn = _head_norm(q_ref[g], qg_ref[...])
            qn16 = qn.astype(BF16)
            sink = sk_ref[g:g + 1, 0:1]
            p, ps, den = _attn_probs(qn16, kn16, slope, dist, valid, sink)
            pn = p / den
            do16 = do_ref[g].astype(BF16)
            dp = _dot(do16, v16, "nt")
            delta = jnp.sum(pn * dp, axis=-1, keepdims=True)
            ds16 = (pn * (dp - delta)).astype(BF16)
            dsink = -jnp.sum(ps / den * delta, axis=0, keepdims=True)
            dsk_ref[g:g + 1, :] += jnp.broadcast_to(dsink, (1, LANES))
            dqn = _dot(ds16, kn16) * ATTN_SCALE
            dkn = dkn + _dot(ds16, qn16, "tn") * ATTN_SCALE
            dv = dv + _dot(pn.astype(BF16), do16, "tn")
            dqh = dqn * qg_ref[...]
            dq_ref[g] = (rq * (dqh - qhat * jnp.mean(dqh * qhat, axis=-1, keepdims=True))).astype(BF16)
            dqg = dqg + jnp.sum(dqn * qhat, axis=0, keepdims=True)
        dqg_ref[...] += dqg
        dkg_ref[...] += jnp.sum(dkn * khat, axis=0, keepdims=True)
        dkh = dkn * kg_ref[...]
        dk = rk * (dkh - khat * jnp.mean(dkh * khat, axis=-1, keepdims=True))
        rows = pl.ds(pl.multiple_of(n * BLOCK, BLOCK), BLOCK)
        dk_ref[rows, :] += dk[BLOCK:]
        dv_ref[rows, :] += dv[BLOCK:]

        @pl.when(n > 0)
        def _():
            before = pl.ds(pl.multiple_of((n - 1) * BLOCK, BLOCK), BLOCK)
            dk_ref[before, :] += dk[:BLOCK]
            dv_ref[before, :] += dv[:BLOCK]

    q_spec, cur, prev, gain, sink = _attn_specs(T)
    whole = pl.BlockSpec((None, T, HEAD_DIM), lambda h, n: (h, 0, 0))
    gacc = pl.BlockSpec((None, 1, HEAD_DIM), lambda h, n: (h, 0, 0))
    outs, extra = _call(
        body, name, (N_KV_HEADS, nb), [q_spec, cur, prev, cur, prev, q_spec, gain, gain, sink],
        [q_spec, whole, whole, gacc, gacc, sink],
        [jax.ShapeDtypeStruct((N_Q_HEADS, T, HEAD_DIM), BF16),
         jax.ShapeDtypeStruct((N_KV_HEADS, T, HEAD_DIM), F32),
         jax.ShapeDtypeStruct((N_KV_HEADS, T, HEAD_DIM), F32),
         jax.ShapeDtypeStruct((N_KV_HEADS, 1, HEAD_DIM), F32),
         jax.ShapeDtypeStruct((N_KV_HEADS, 1, HEAD_DIM), F32),
         jax.ShapeDtypeStruct((N_KV_HEADS, GQA_GROUP, LANES), F32)],
        [], ("parallel", "arbitrary"), (q, k, k, v, v, do, qg, kg, _sink_rows(sinks)), comm)
    return outs if comm is None else (outs, extra)


def _ssm_discretize(lam_re, lam_im, log_dt, b_re, b_im, c_re, c_im):
    dt = jnp.exp(log_dt)[:, None]
    mag = jnp.exp(lam_re * dt)
    ar = mag * jnp.cos(lam_im * dt)
    ai = mag * jnp.sin(lam_im * dt)
    den = lam_re * lam_re + lam_im * lam_im
    fr = ((ar - 1.0) * lam_re + ai * lam_im) / den
    fi = (ai * lam_re - (ar - 1.0) * lam_im) / den
    bbar_r = fr[:, :, None] * b_re - fi[:, :, None] * b_im
    bbar_i = fr[:, :, None] * b_im + fi[:, :, None] * b_re
    gl = SSM_GROUPS // SSM_NGB
    eye = jnp.eye(gl, dtype=F32)

    def tiles(a):
        return a.reshape(SSM_NGB, SUBLANES, LANES)

    def bdiag(bb):
        return jnp.einsum("bgph,gk->bghkp", bb.reshape(SSM_NGB, gl, SSM_STATE, SSM_GROUP_CH), eye).reshape(
            SSM_NGB, SSM_GB_CH, SSM_GB_ST)

    def cdiag(cc):
        return jnp.einsum("bghp,gk->bgpkh", cc.reshape(SSM_NGB, gl, SSM_GROUP_CH, SSM_STATE), eye).reshape(
            SSM_NGB, SSM_GB_ST, SSM_GB_CH)

    return tiles(ar), tiles(ai), bdiag(bbar_r), bdiag(bbar_i), cdiag(c_re), cdiag(c_im)


def _to_time_major(dst, val, tt, first_row=0):
    for j in range(SUBLANES):
        dst[pl.ds(first_row + j, tt, stride=SUBLANES), :] = val[:, j * LANES:(j + 1) * LANES]


def _from_time_major(dst, src, tt):
    for j in range(SUBLANES):
        dst[:, j * LANES:(j + 1) * LANES] = src[pl.ds(j, tt, stride=SUBLANES), :]


def _ssm_fwd(z, ar, ai, bbr, bbi, cbr, cbi, dskip, name, comm=None):
    T = z.shape[0]
    tt = min(SSM_TT, T)
    nt = T // tt

    def body(u_ref, ar_ref, ai_ref, br_ref, bi_ref, cr_ref, ci_ref, d_ref, y_ref, sr_ref, si_ref,
             tmr, tmi, car_r, car_i):
        @pl.when(pl.program_id(1) == 0)
        def _():
            car_r[...] = jnp.zeros_like(car_r)
            car_i[...] = jnp.zeros_like(car_i)

        u = u_ref[...]
        u16 = u.astype(BF16)
        _to_time_major(tmr, _dot(u16, br_ref[...]), tt)
        _to_time_major(tmi, _dot(u16, bi_ref[...]), tt)
        a_r = ar_ref[...]
        a_i = ai_ref[...]

        def step(t, carry):
            s_r, s_i = carry
            rows = pl.ds(pl.multiple_of(t * SUBLANES, SUBLANES), SUBLANES)
            n_r = a_r * s_r - a_i * s_i + tmr[rows, :]
            n_i = a_r * s_i + a_i * s_r + tmi[rows, :]
            tmr[rows, :] = n_r
            tmi[rows, :] = n_i
            return n_r, n_i

        s_r, s_i = lax.fori_loop(0, tt, step, (car_r[...], car_i[...]), unroll=8)
        car_r[...] = s_r
        car_i[...] = s_i
        _from_time_major(sr_ref, tmr, tt)
        _from_time_major(si_ref, tmi, tt)
        y_ref[...] = (_dot(sr_ref[...].astype(BF16), cr_ref[...]) - _dot(si_ref[...].astype(BF16), ci_ref[...])
                      + d_ref[...] * u)

    u_spec = pl.BlockSpec((tt, SSM_GB_CH), lambda b, t: (t, OFF_U // SSM_GB_CH + b))
    a_spec = pl.BlockSpec((None, SUBLANES, LANES), lambda b, t: (b, 0, 0))
    b_spec = pl.BlockSpec((None, SSM_GB_CH, SSM_GB_ST), lambda b, t: (b, 0, 0))
    c_spec = pl.BlockSpec((None, SSM_GB_ST, SSM_GB_CH), lambda b, t: (b, 0, 0))
    d_spec = pl.BlockSpec((1, SSM_GB_CH), lambda b, t: (0, b))
    y_spec = pl.BlockSpec((tt, SSM_GB_CH), lambda b, t: (t, b))
    s_spec = pl.BlockSpec((tt, SSM_GB_ST), lambda b, t: (t, b))
    n_state = SSM_NGB * SSM_GB_ST
    outs, extra = _call(
        body, name, (SSM_NGB, nt), [u_spec, a_spec, a_spec, b_spec, b_spec, c_spec, c_spec, d_spec],
        [y_spec, s_spec, s_spec],
        [jax.ShapeDtypeStruct((T, SSM_WIDTH), F32), jax.ShapeDtypeStruct((T, n_state), F32),
         jax.ShapeDtypeStruct((T, n_state), F32)],
        [pltpu.VMEM((tt * SUBLANES, LANES), F32), pltpu.VMEM((tt * SUBLANES, LANES), F32),
         pltpu.VMEM((SUBLANES, LANES), F32), pltpu.VMEM((SUBLANES, LANES), F32)],
        ("parallel", "arbitrary"), (z, ar, ai, bbr, bbi, cbr, cbi, dskip), comm)
    return outs if comm is None else (outs, extra)


def _ssm_bwd(dy, z, s_r, s_i, ar, ai, bbr, bbi, cbr, cbi, dskip, name):
    T = z.shape[0]
    tt = min(SSM_TT, T)
    nt = T // tt
    per8 = tt // SUBLANES

    def body(dy_ref, u_ref, sr_ref, si_ref, srp_ref, sip_ref, ar_ref, ai_ref, br_ref, bi_ref, cr_ref, ci_ref, d_ref,
             du_ref, dar_ref, dai_ref, dbr_ref, dbi_ref, dcr_ref, dci_ref, dd_ref,
             tmr, tmi, smr, smi, natr, nati, car_r, car_i):
        tb = pl.program_id(1)
        first_block = tb == nt - 1

        @pl.when(tb == 0)
        def _():
            for ref in (car_r, car_i, dar_ref, dai_ref, dbr_ref, dbi_ref, dcr_ref, dci_ref, dd_ref):
                ref[...] = jnp.zeros_like(ref)

        dy = dy_ref[...]
        dy16 = dy.astype(BF16)
        u = u_ref[...]
        u16 = u.astype(BF16)
        _to_time_major(tmr, _dot(dy16, cr_ref[...], "nt"), tt)
        _to_time_major(tmi, -_dot(dy16, ci_ref[...], "nt"), tt)
        _to_time_major(smr, sr_ref[...], tt, first_row=SUBLANES)
        _to_time_major(smi, si_ref[...], tt, first_row=SUBLANES)
        keep = jnp.where(first_block, 0.0, 1.0)
        for j in range(SUBLANES):
            smr[j:j + 1, :] = srp_ref[SUBLANES - 1:SUBLANES, j * LANES:(j + 1) * LANES] * keep
            smi[j:j + 1, :] = sip_ref[SUBLANES - 1:SUBLANES, j * LANES:(j + 1) * LANES] * keep
        a_r = ar_ref[...]
        a_i = ai_ref[...]

        def step(i, carry):
            n_r, n_i, da_r, da_i = carry
            rows = pl.ds(pl.multiple_of((tt - 1 - i) * SUBLANES, SUBLANES), SUBLANES)
            g_r = tmr[rows, :] + a_r * n_r + a_i * n_i
            g_i = tmi[rows, :] - a_i * n_r + a_r * n_i
            tmr[rows, :] = g_r
            tmi[rows, :] = g_i
            p_r = smr[rows, :]
            p_i = smi[rows, :]
            return g_r, g_i, da_r + g_r * p_r + g_i * p_i, da_i - g_r * p_i + g_i * p_r

        zero = jnp.zeros((SUBLANES, LANES), F32)
        n_r, n_i, da_r, da_i = lax.fori_loop(0, tt, step, (car_r[...], car_i[...], zero, zero), unroll=8)
        car_r[...] = n_r
        car_i[...] = n_i
        dar_ref[...] += da_r
        dai_ref[...] += da_i
        _from_time_major(natr, tmr, tt)
        _from_time_major(nati, tmi, tt)
        dbu_r16 = natr[...].astype(BF16)
        dbu_i16 = nati[...].astype(BF16)
        du_ref[...] = (_dot(dbu_r16, br_ref[...], "nt") + _dot(dbu_i16, bi_ref[...], "nt")
                       + d_ref[...] * dy).astype(BF16)
        dbr_ref[...] += _dot(u16, dbu_r16, "tn")
        dbi_ref[...] += _dot(u16, dbu_i16, "tn")
        dcr_ref[...] += _dot(sr_ref[...].astype(BF16), dy16, "tn")
        dci_ref[...] -= _dot(si_ref[...].astype(BF16), dy16, "tn")
        dd_ref[...] += jnp.sum(dy * u, axis=0, keepdims=True)

    def rev(t):
        return nt - 1 - t

    dy_spec = pl.BlockSpec((tt, SSM_GB_CH), lambda b, t: (rev(t), b))
    u_spec = pl.BlockSpec((tt, SSM_GB_CH), lambda b, t: (rev(t), OFF_U // SSM_GB_CH + b))
    s_spec = pl.BlockSpec((tt, SSM_GB_ST), lambda b, t: (rev(t), b))
    sp_spec = pl.BlockSpec((SUBLANES, SSM_GB_ST), lambda b, t: (jnp.maximum(rev(t) * per8 - 1, 0), b))
    a_spec = pl.BlockSpec((None, SUBLANES, LANES), lambda b, t: (b, 0, 0))
    b_spec = pl.BlockSpec((None, SSM_GB_CH, SSM_GB_ST), lambda b, t: (b, 0, 0))
    c_spec = pl.BlockSpec((None, SSM_GB_ST, SSM_GB_CH), lambda b, t: (b, 0, 0))
    d_spec = pl.BlockSpec((1, SSM_GB_CH), lambda b, t: (0, b))
    tm_shape = pltpu.VMEM((tt * SUBLANES, LANES), F32)
    sm_shape = pltpu.VMEM(((tt + 1) * SUBLANES, LANES), F32)
    nat_shape = pltpu.VMEM((tt, SSM_GB_ST), F32)
    tile = pltpu.VMEM((SUBLANES, LANES), F32)
    return pl.pallas_call(
        body, name=name, grid=(SSM_NGB, nt),
        in_specs=[dy_spec, u_spec, s_spec, s_spec, sp_spec, sp_spec, a_spec, a_spec, b_spec, b_spec, c_spec, c_spec,
                  d_spec],
        out_specs=[dy_spec, a_spec, a_spec, b_spec, b_spec, c_spec, c_spec, d_spec],
        out_shape=[jax.ShapeDtypeStruct((T, SSM_WIDTH), BF16),
                   jax.ShapeDtypeStruct((SSM_NGB, SUBLANES, LANES), F32),
                   jax.ShapeDtypeStruct((SSM_NGB, SUBLANES, LANES), F32),
                   jax.ShapeDtypeStruct((SSM_NGB, SSM_GB_CH, SSM_GB_ST), F32),
                   jax.ShapeDtypeStruct((SSM_NGB, SSM_GB_CH, SSM_GB_ST), F32),
                   jax.ShapeDtypeStruct((SSM_NGB, SSM_GB_ST, SSM_GB_CH), F32),
                   jax.ShapeDtypeStruct((SSM_NGB, SSM_GB_ST, SSM_GB_CH), F32),
                   jax.ShapeDtypeStruct((1, SSM_WIDTH), F32)],
        scratch_shapes=[tm_shape, tm_shape, sm_shape, sm_shape, nat_shape, nat_shape, tile, tile],
        compiler_params=_params("parallel", "arbitrary"))(dy, z, s_r, s_i, s_r, s_i, ar, ai, bbr, bbi, cbr, cbi, dskip)


def _glu_fwd(y, w, b, name):
    T, W = y.shape
    tm = _pick(T, 512)

    def body(y_ref, w_ref, b_ref, pre_ref, y3_ref):
        y2 = _gelu(y_ref[...])
        pre = _dot(y2.astype(BF16), w_ref[...]) + b_ref[...]
        pre_ref[...] = pre
        y3_ref[...] = (y2 * _sigmoid(pre)).astype(BF16)

    row = pl.BlockSpec((tm, W), lambda i: (i, 0))
    return pl.pallas_call(
        body, name=name, grid=(T // tm,),
        in_specs=[row, pl.BlockSpec((W, W), lambda i: (0, 0)), pl.BlockSpec((1, W), lambda i: (0, 0))],
        out_specs=[row, row], out_shape=[jax.ShapeDtypeStruct((T, W), F32), jax.ShapeDtypeStruct((T, W), BF16)],
        compiler_params=_params("parallel"))(y, w, b)


def _glu_bwd_gate(dy3, y, pre, name):
    T, W = y.shape
    tm = _pick(T, 512)

    def body(dy3_ref, y_ref, pre_ref, dpre_ref, t1_ref, y2_ref, db_ref):
        @pl.when(pl.program_id(0) == 0)
        def _():
            db_ref[...] = jnp.zeros_like(db_ref)

        y2 = _gelu(y_ref[...])
        sg = _sigmoid(pre_ref[...])
        dy3 = dy3_ref[...]
        dpre = dy3 * y2 * sg * (1.0 - sg)
        dpre_ref[...] = dpre.astype(BF16)
        t1_ref[...] = dy3 * sg
        y2_ref[...] = y2.astype(BF16)
        db_ref[...] += jnp.sum(dpre, axis=0, keepdims=True)

    row = pl.BlockSpec((tm, W), lambda i: (i, 0))
    vec = pl.BlockSpec((1, W), lambda i: (0, 0))
    return pl.pallas_call(
        body, name=name, grid=(T // tm,), in_specs=[row, row, row], out_specs=[row, row, row, vec],
        out_shape=[jax.ShapeDtypeStruct((T, W), BF16), jax.ShapeDtypeStruct((T, W), F32),
                   jax.ShapeDtypeStruct((T, W), BF16), jax.ShapeDtypeStruct((1, W), F32)],
        compiler_params=_params("arbitrary"))(dy3, y, pre)


def _glu_bwd_in(dpre, w, t1, y, name):
    T, W = y.shape
    tm = _pick(T, 512)

    def body(dpre_ref, w_ref, t1_ref, y_ref, dy_ref):
        dy_ref[...] = (_dot(dpre_ref[...], w_ref[...], "nt") + t1_ref[...]) * _gelu_grad(y_ref[...])

    row = pl.BlockSpec((tm, W), lambda i: (i, 0))
    return pl.pallas_call(
        body, name=name, grid=(T // tm,), in_specs=[row, pl.BlockSpec((W, W), lambda i: (0, 0)), row, row],
        out_specs=row, out_shape=jax.ShapeDtypeStruct((T, W), F32),
        compiler_params=_params("parallel"))(dpre, w, t1, y)


def _merge_fwd(ya, y3, wa, ws, z, bias, name, comm=None):
    T, W = ya.shape
    D = wa.shape[1]
    tm, tn = _pick(T, 512), _pick(D, 512)

    def body(ya_ref, y3_ref, wa_ref, ws_ref, za_ref, zs_ref, ba_ref, bs_ref, a_ref, b_ref, m_ref):
        a = _dot(ya_ref[...], wa_ref[...])
        b = _dot(y3_ref[...], ws_ref[...])
        a_ref[...] = a
        b_ref[...] = b
        m_ref[...] = (_sigmoid(za_ref[...] + ba_ref[...]) * a + _sigmoid(zs_ref[...] + bs_ref[...]) * b).astype(BF16)

    act = pl.BlockSpec((tm, W), lambda i, j: (i, 0))
    wgt = pl.BlockSpec((W, tn), lambda i, j: (0, j))
    za = pl.BlockSpec((tm, tn), lambda i, j: (i, OFF_G // tn + j))
    zs = pl.BlockSpec((tm, tn), lambda i, j: (i, (OFF_G + D) // tn + j))
    ba = pl.BlockSpec((1, tn), lambda i, j: (0, j))
    bs = pl.BlockSpec((1, tn), lambda i, j: (0, D // tn + j))
    out = pl.BlockSpec((tm, tn), lambda i, j: (i, j))
    outs, extra = _call(
        body, name, (T // tm, D // tn), [act, act, wgt, wgt, za, zs, ba, bs], [out, out, out],
        [jax.ShapeDtypeStruct((T, D), F32), jax.ShapeDtypeStruct((T, D), F32), jax.ShapeDtypeStruct((T, D), BF16)],
        [], ("parallel", "parallel"), (ya, y3, wa, ws, z, z, bias, bias), comm)
    return outs if comm is None else (outs, extra)


def _merge_bwd(dm, a, b, z, bias, name):
    T, D = dm.shape
    tm, tn = _pick(T, 512), _pick(D, 512)

    def body(dm_ref, a_ref, b_ref, za_ref, zs_ref, ba_ref, bs_ref, da_ref, db_ref, dza_ref, dzs_ref, dba_ref, dbs_ref):
        @pl.when(pl.program_id(1) == 0)
        def _():
            dba_ref[...] = jnp.zeros_like(dba_ref)
            dbs_ref[...] = jnp.zeros_like(dbs_ref)

        dm = dm_ref[...]
        sa = _sigmoid(za_ref[...] + ba_ref[...])
        ss = _sigmoid(zs_ref[...] + bs_ref[...])
        da_ref[...] = (dm * sa).astype(BF16)
        db_ref[...] = (dm * ss).astype(BF16)
        dza = dm * a_ref[...] * sa * (1.0 - sa)
        dzs = dm * b_ref[...] * ss * (1.0 - ss)
        dza_ref[...] = dza.astype(BF16)
        dzs_ref[...] = dzs.astype(BF16)
        dba_ref[...] += jnp.sum(dza, axis=0, keepdims=True)
        dbs_ref[...] += jnp.sum(dzs, axis=0, keepdims=True)

    blk = pl.BlockSpec((tm, tn), lambda j, i: (i, j))
    za = pl.BlockSpec((tm, tn), lambda j, i: (i, OFF_G // tn + j))
    zs = pl.BlockSpec((tm, tn), lambda j, i: (i, (OFF_G + D) // tn + j))
    ba = pl.BlockSpec((1, tn), lambda j, i: (0, j))
    bs = pl.BlockSpec((1, tn), lambda j, i: (0, D // tn + j))
    big = jax.ShapeDtypeStruct((T, D), BF16)
    vec = jax.ShapeDtypeStruct((1, D), F32)
    return pl.pallas_call(
        body, name=name, grid=(D // tn, T // tm), in_specs=[blk, blk, blk, za, zs, ba, bs],
        out_specs=[blk, blk, blk, blk, ba, ba], out_shape=[big, big, big, big, vec, vec],
        compiler_params=_params("parallel", "arbitrary"))(dm, a, b, z, z, bias, bias)


_HALF = N_DEV // 2


def _wgu_block(d):
    return d // 2, d % 2


def _ffn_fwd(h, wgu, name, comm=None):
    T, D = h.shape
    n = wgu.shape[3]
    F = _HALF * n
    tm = _pick(T, 512)

    def body(h_ref, wg_ref, wu_ref, g_ref, u_ref, act_ref):
        hv = h_ref[...]
        g = _dot(hv, wg_ref[...])
        u = _dot(hv, wu_ref[...])
        g_ref[...] = g
        u_ref[...] = u
        act_ref[...] = (g * _sigmoid(g) * u).astype(BF16)

    wg = pl.BlockSpec((None, None, D, n), lambda j, i: (*_wgu_block(j), 0, 0))
    wu = pl.BlockSpec((None, None, D, n), lambda j, i: (*_wgu_block(j + _HALF), 0, 0))
    out = pl.BlockSpec((tm, n), lambda j, i: (i, j))
    outs, extra = _call(
        body, name, (_HALF, T // tm), [pl.BlockSpec((tm, D), lambda j, i: (i, 0)), wg, wu], [out, out, out],
        [jax.ShapeDtypeStruct((T, F), F32), jax.ShapeDtypeStruct((T, F), F32), jax.ShapeDtypeStruct((T, F), BF16)],
        [], ("parallel", "parallel"), (h, wgu, wgu), comm)
    return outs if comm is None else (outs, extra)


def _ffn_bwd_in(dg, du, wgu, name, comm=None):
    T, F = dg.shape
    D, n = wgu.shape[2], wgu.shape[3]
    tm, tn = _pick(T, 1024), _pick(D, 1024)

    def body(dg_ref, du_ref, w_ref, o_ref, acc_ref):
        k = pl.program_id(2)

        @pl.when(k == 0)
        def _():
            acc_ref[...] = jnp.zeros_like(acc_ref)

        @pl.when(k < _HALF)
        def _():
            acc_ref[...] += _dot(dg_ref[...], w_ref[...], "nt")

        @pl.when(k >= _HALF)
        def _():
            acc_ref[...] += _dot(du_ref[...], w_ref[...], "nt")

        @pl.when(k == N_DEV - 1)
        def _():
            o_ref[...] = acc_ref[...]

    dg_spec = pl.BlockSpec((tm, n), lambda i, j, k: (i, jnp.minimum(k, _HALF - 1)))
    du_spec = pl.BlockSpec((tm, n), lambda i, j, k: (i, jnp.maximum(k - _HALF, 0)))
    w_spec = pl.BlockSpec((None, None, tn, n), lambda i, j, k: (*_wgu_block(k), j, 0))
    o_spec = pl.BlockSpec((tm, tn), lambda i, j, k: (i, j))
    (out,), extra = _call(
        body, name, (T // tm, D // tn, N_DEV), [dg_spec, du_spec, w_spec], [o_spec],
        [jax.ShapeDtypeStruct((T, D), F32)], [pltpu.VMEM((tm, tn), F32)], ("parallel", "parallel", "arbitrary"),
        (dg, du, wgu), comm)
    return out if comm is None else (out, extra)


def _ffn_bwd_w(h, dg, du, name, comm=None):
    T, D = h.shape
    n = dg.shape[1] // _HALF
    tm, tk = _pick(D, 1024), _pick(T, 2048)
    nk = T // tk

    def body(h_ref, dg_ref, du_ref, o_ref, acc_ref):
        j, k = pl.program_id(0), pl.program_id(2)

        @pl.when(k == 0)
        def _():
            acc_ref[...] = jnp.zeros_like(acc_ref)

        @pl.when(j < _HALF)
        def _():
            acc_ref[...] += _dot(h_ref[...], dg_ref[...], "tn")

        @pl.when(j >= _HALF)
        def _():
            acc_ref[...] += _dot(h_ref[...], du_ref[...], "tn")

        @pl.when(k == nk - 1)
        def _():
            o_ref[...] = acc_ref[...].astype(BF16)

    h_spec = pl.BlockSpec((tk, tm), lambda j, i, k: (k, i))
    dg_spec = pl.BlockSpec((tk, n), lambda j, i, k: (jnp.where(j < _HALF, k, nk - 1), jnp.minimum(j, _HALF - 1)))
    du_spec = pl.BlockSpec((tk, n), lambda j, i, k: (jnp.where(j >= _HALF, k, 0), jnp.maximum(j - _HALF, 0)))
    o_spec = pl.BlockSpec((None, None, tm, n), lambda j, i, k: (*_wgu_block(j), i, 0))
    (out,), extra = _call(
        body, name, (N_DEV, D // tm, nk), [h_spec, dg_spec, du_spec], [o_spec],
        [jax.ShapeDtypeStruct((_HALF, 2, D, n), BF16)], [pltpu.VMEM((tm, n), F32)],
        ("parallel", "parallel", "arbitrary"), (h, dg, du), comm)
    return out if comm is None else (out, extra)


def _ffn_bwd_act(dx, wo, g, u, name, comm=None):
    T, D = dx.shape
    F = wo.shape[0]
    tm, tn = _pick(T, 1024), _pick(F, 512)

    def body(dx_ref, wo_ref, g_ref, u_ref, dg_ref, du_ref):
        dact = _dot(dx_ref[...].astype(BF16), wo_ref[...], "nt")
        gv = g_ref[...]
        sg = _sigmoid(gv)
        dg_ref[...] = (dact * u_ref[...] * sg * (1.0 + gv * (1.0 - sg))).astype(BF16)
        du_ref[...] = (dact * gv * sg).astype(BF16)

    out = pl.BlockSpec((tm, tn), lambda i, j: (i, j))
    big = jax.ShapeDtypeStruct((T, F), BF16)
    outs, extra = _call(
        body, name, (T // tm, F // tn),
        [pl.BlockSpec((tm, D), lambda i, j: (i, 0)), pl.BlockSpec((tn, D), lambda i, j: (j, 0)), out, out],
        [out, out], [big, big], [], ("parallel", "parallel"), (dx, wo, g, u), comm)
    return outs if comm is None else (outs, extra)


def _place():
    x, y, c = lax.axis_index("x"), lax.axis_index("y"), lax.axis_index("c")
    other_chips = [(1 - x, y), (x, 1 - y), (1 - x, 1 - y)]
    return x, y, c, 2 * x + y, other_chips


_ANY = pl.BlockSpec(memory_space=pl.ANY)
_N_COPIES = 7


def _all_gather(shards, name):
    n = len(shards)

    def body(*refs):
        ins, outs = refs[:n], refs[n:2 * n]
        send_sems, recv_sems, local_sems = refs[2 * n:]
        x, y, c, chip, other_chips = _place()
        sibling = (x, y, 1 - c)

        def remote(src, dst, a, j, dev):
            return pltpu.make_async_remote_copy(src_ref=src, dst_ref=dst, send_sem=send_sems.at[a, j],
                                                recv_sem=recv_sems.at[a, j], device_id=dev, device_id_type=MESH)

        sends, local = [], []
        for a in range(n):
            mine = outs[a].at[chip, c]
            local.append(pltpu.make_async_copy(ins[a], mine, local_sems.at[a]))
            local[a].start()
            for j, (ox, oy) in enumerate(other_chips):
                sends.append(remote(ins[a], mine, a, 1 + j, (ox, oy, c)))
                sends[-1].start()
            sends.append(remote(ins[a], mine, a, 0, sibling))
            sends[-1].start()
        for a in range(n):
            for j, (ox, oy) in enumerate(other_chips):
                slot = outs[a].at[2 * ox + oy, c]
                remote(ins[a], slot, a, 1 + j, (ox, oy, c)).wait_recv()
                sends.append(remote(slot, slot, a, 4 + j, sibling))
                sends[-1].start()
        for a in range(n):
            remote(ins[a], outs[a].at[chip, 1 - c], a, 0, sibling).wait_recv()
            for j, (ox, oy) in enumerate(other_chips):
                remote(ins[a], outs[a].at[2 * ox + oy, 1 - c], a, 4 + j, sibling).wait_recv()
        for cp in sends:
            cp.wait_send()
        for a in range(n):
            local[a].wait()

    return pl.pallas_call(
        body, name=name, in_specs=[_ANY] * n, out_specs=[_ANY] * n,
        out_shape=[jax.ShapeDtypeStruct((4, 2) + s.shape, s.dtype) for s in shards],
        scratch_shapes=[pltpu.SemaphoreType.DMA((n, _N_COPIES)), pltpu.SemaphoreType.DMA((n, _N_COPIES)),
                        pltpu.SemaphoreType.DMA((n,))])(*shards)


def _pair_exchange(parts, name):
    n = len(parts)

    def body(*refs):
        ins, theirs = refs[:n], refs[n:2 * n]
        send_sems, recv_sems = refs[2 * n:]
        x, y, c, _, _ = _place()
        sends = []
        for a in range(n):
            for k in range(4):
                sends.append(pltpu.make_async_remote_copy(
                    src_ref=ins[a].at[k, 1 - c], dst_ref=theirs[a].at[k], send_sem=send_sems.at[a, k],
                    recv_sem=recv_sems.at[a, k], device_id=(x, y, 1 - c), device_id_type=MESH))
                sends[-1].start()
        for cp in sends:
            cp.wait_recv()
            cp.wait_send()

    return pl.pallas_call(
        body, name=name, in_specs=[_ANY] * n, out_specs=[_ANY] * n,
        out_shape=[jax.ShapeDtypeStruct((4,) + p.shape[2:], p.dtype) for p in parts],
        scratch_shapes=[pltpu.SemaphoreType.DMA((n, 4)), pltpu.SemaphoreType.DMA((n, 4))])(*parts)


def _chip_exchange(parts, name):
    n = len(parts)

    def body(*refs):
        ins, got = refs[:n], refs[n:2 * n]
        send_sems, recv_sems = refs[2 * n:]
        _, _, c, _, other_chips = _place()
        sends = []
        for a in range(n):
            for j, (ox, oy) in enumerate(other_chips):
                sends.append(pltpu.make_async_remote_copy(
                    src_ref=ins[a].at[2 * ox + oy], dst_ref=got[a].at[j], send_sem=send_sems.at[a, j],
                    recv_sem=recv_sems.at[a, j], device_id=(ox, oy, c), device_id_type=MESH))
                sends[-1].start()
        for cp in sends:
            cp.wait_recv()
            cp.wait_send()

    return pl.pallas_call(
        body, name=name, in_specs=[_ANY] * n, out_specs=[_ANY] * n,
        out_shape=[jax.ShapeDtypeStruct((3,) + p.shape[1:], p.dtype) for p in parts],
        scratch_shapes=[pltpu.SemaphoreType.DMA((n, 3)), pltpu.SemaphoreType.DMA((n, 3))])(*parts)


def _remote(src, dst, send_sems, recv_sems, a, j, dev):
    return pltpu.make_async_remote_copy(src_ref=src, dst_ref=dst, send_sem=send_sems.at[a, j],
                                        recv_sem=recv_sems.at[a, j], device_id=dev, device_id_type=MESH)


def _gather_send(shards):
    n = len(shards)

    def copies(cin, cout, sems, arriving):
        send_sems, recv_sems, local_sems = sems
        x, y, c, chip, other_chips = _place()
        sibling = (x, y, 1 - c)
        peers = [(0, sibling, (chip, 1 - c))] + [(1 + j, (ox, oy, c), (2 * ox + oy, c))
                                                 for j, (ox, oy) in enumerate(other_chips)]
        sends, recvs, local = [], [], []
        for a in range(n):
            mine = cout[a].at[chip, c]
            local.append(pltpu.make_async_copy(cin[a], mine, local_sems.at[a]))
            for j, dev, slot in peers:
                sends.append(_remote(cin[a], mine, send_sems, recv_sems, a, j, dev))
                if arriving:
                    recvs.append(_remote(cin[a], cout[a].at[slot], send_sems, recv_sems, a, j, dev))
        return sends, recvs, local

    return _Comm(shards, [jax.ShapeDtypeStruct((4, 2) + s.shape, s.dtype) for s in shards],
                 [pltpu.SemaphoreType.DMA((n, 4)), pltpu.SemaphoreType.DMA((n, 4)), pltpu.SemaphoreType.DMA((n,))],
                 copies)


def _gather_pass(gathered):
    n = len(gathered)

    def copies(cin, cout, sems, arriving):
        send_sems, recv_sems = sems
        x, y, c, _, other_chips = _place()
        sibling = (x, y, 1 - c)
        sends, recvs = [], []
        for a in range(n):
            for j, (ox, oy) in enumerate(other_chips):
                k = 2 * ox + oy
                sends.append(_remote(cout[a].at[k, c], cout[a].at[k, c], send_sems, recv_sems, a, j, sibling))
                if arriving:
                    recvs.append(_remote(cout[a].at[k, c], cout[a].at[k, 1 - c], send_sems, recv_sems, a, j, sibling))
        return sends, recvs, []

    return _Comm(gathered, [jax.ShapeDtypeStruct(g.shape, g.dtype) for g in gathered],
                 [pltpu.SemaphoreType.DMA((n, 3)), pltpu.SemaphoreType.DMA((n, 3))], copies,
                 aliases={i: i for i in range(n)})


def _scatter_send(sums):
    n = len(sums)

    def copies(cin, cout, sems, arriving):
        send_sems, recv_sems = sems
        _, _, c, _, other_chips = _place()
        sends = [_remote(cin[a].at[2 * ox + oy], cout[a].at[j], send_sems, recv_sems, a, j, (ox, oy, c))
                 for a in range(n) for j, (ox, oy) in enumerate(other_chips)]
        return sends, sends, []

    return _Comm(sums, [jax.ShapeDtypeStruct((3,) + s.shape[1:], s.dtype) for s in sums],
                 [pltpu.SemaphoreType.DMA((n, 3)), pltpu.SemaphoreType.DMA((n, 3))], copies)


class _Behind:
    def __init__(self, make, operands, hosts):
        self.make, self.operands, self.hosts, self.results = make, operands, hosts, {}

    def comm(self, host):
        names = self.hosts.get(host)
        return None if names is None else self.make([self.operands[n] for n in names])

    def collect(self, host, extra):
        self.results.update(zip(self.hosts[host], extra))


def _carry(plans, host, fn, *args, **kwargs):
    for plan in plans:
        comm = plan.comm(host)
        if comm is not None:
            out, extra = fn(*args, comm=comm, **kwargs)
            plan.collect(host, extra)
            return out
    return fn(*args, **kwargs)


def _add_pair(core, parts, theirs, name):
    k, _, R, C = parts.shape
    tr = _pick(R, 512, 16)

    def body(core_ref, a_ref, b_ref, o_ref):
        o_ref[...] = (a_ref[...].astype(F32) + b_ref[...].astype(F32)).astype(BF16)

    blk = pl.BlockSpec((None, tr, C), lambda s, i, core_ref: (s, i, 0))
    grid_spec = pltpu.PrefetchScalarGridSpec(
        num_scalar_prefetch=1, grid=(k, R // tr),
        in_specs=[pl.BlockSpec((None, None, tr, C), lambda s, i, core_ref: (s, core_ref[0], i, 0)), blk],
        out_specs=blk)
    return pl.pallas_call(
        body, name=name, grid_spec=grid_spec, out_shape=jax.ShapeDtypeStruct(theirs.shape, BF16),
        compiler_params=_params("parallel", "parallel"))(core, parts, theirs)


def _adamw_math(w, g, m, v):
    m = ADAM_B1 * m + (1.0 - ADAM_B1) * g
    v = ADAM_B2 * v + (1.0 - ADAM_B2) * (g * g)
    m_hat = m / (1.0 - ADAM_B1 ** ADAM_STEP)
    v_hat = v / (1.0 - ADAM_B2 ** ADAM_STEP)
    delta = -ADAM_LR * (m_hat / (jnp.sqrt(v_hat) + ADAM_EPS) + ADAM_WD * w)
    return delta, m, v


def _scattered_pieces(sums, got):
    return [("own", sums)] + [("peer%d" % j, got) for j in range(3)]


def _piece_spec(kind, tr, C):
    if kind == "own":
        return pl.BlockSpec((None, tr, C), lambda i, chip_ref: (chip_ref[0], i, 0))
    if kind == "plain":
        return pl.BlockSpec((tr, C), lambda i, chip_ref: (i, 0))
    return pl.BlockSpec((None, tr, C), functools.partial(lambda j, i, chip_ref: (j, i, 0), int(kind[-1])))


def _grad_sum(chip, pieces, name):
    R, C = pieces[0][1].shape[-2:]
    tr = _pick(R, 256, 16)

    def body(chip_ref, *refs):
        g = refs[0][...].astype(F32)
        for p in refs[1:-1]:
            g = g + p[...].astype(F32)
        refs[-1][...] = g

    grid_spec = pltpu.PrefetchScalarGridSpec(
        num_scalar_prefetch=1, grid=(R // tr,), in_specs=[_piece_spec(kind, tr, C) for kind, _ in pieces],
        out_specs=_piece_spec("plain", tr, C))
    return pl.pallas_call(
        body, name=name, grid_spec=grid_spec, out_shape=jax.ShapeDtypeStruct((R, C), F32),
        compiler_params=_params("parallel"))(chip, *[a for _, a in pieces])


def _adamw_shard(chip, w, m, v, layer, pieces, so_far, name):
    L, R, C = w.shape
    tr = _pick(R, 256, 16)
    n_p = len(pieces)

    def body(chip_ref, w_ref, m_ref, v_ref, *rest):
        g = rest[0][...].astype(F32)
        for p in rest[1:n_p]:
            g = g + p[...].astype(F32)
        delta, nm, nv = _adamw_math(w_ref[...], g, m_ref[...], v_ref[...])
        g_ref, d_ref, nm_ref, nv_ref = rest[-4:]
        g_ref[...] = g
        d_ref[...] = delta
        nm_ref[...] = nm
        nv_ref[...] = nv

    state = pl.BlockSpec((None, tr, C), lambda i, chip_ref: (layer, i, 0))
    carried = [] if so_far is None else list(so_far)
    first_carried = 1 + 3 + n_p
    grid_spec = pltpu.PrefetchScalarGridSpec(
        num_scalar_prefetch=1, grid=(R // tr,),
        in_specs=[state] * 3 + [_piece_spec(kind, tr, C) for kind, _ in pieces] + [_ANY] * len(carried),
        out_specs=[state] * 4)
    return pl.pallas_call(
        body, name=name, grid_spec=grid_spec, out_shape=[jax.ShapeDtypeStruct((L, R, C), F32)] * 4,
        input_output_aliases={first_carried + t: t for t in range(len(carried))},
        compiler_params=_params("parallel"))(chip, w, m, v, *[a for _, a in pieces], *carried)


def _adamw_small(w, m, v, gathered, name):
    R = w.shape[0]
    tr = _pick(R, 512, SUBLANES)

    def body(w_ref, m_ref, v_ref, gg_ref, g_ref, d_ref, nm_ref, nv_ref):
        g = gg_ref[0, 0]
        for k in range(4):
            for c in range(2):
                if c or k:
                    g = g + gg_ref[k, c]
        delta, nm, nv = _adamw_math(w_ref[...], g, m_ref[...], v_ref[...])
        g_ref[...] = g
        d_ref[...] = delta
        nm_ref[...] = nm
        nv_ref[...] = nv

    row = pl.BlockSpec((tr, LANES), lambda i: (i, 0))
    out = jax.ShapeDtypeStruct((R, LANES), F32)
    return pl.pallas_call(
        body, name=name, grid=(R // tr,),
        in_specs=[row, row, row, pl.BlockSpec((4, 2, tr, LANES), lambda i: (0, 0, i, 0))], out_specs=[row] * 4,
        out_shape=[out] * 4, compiler_params=_params("parallel"))(w, m, v, gathered)


def _to_heads(a, n_heads):
    return a.reshape(a.shape[0], n_heads, HEAD_DIM).transpose(1, 0, 2)


def _from_heads(a):
    return a.transpose(1, 0, 2).reshape(a.shape[1], a.shape[0] * HEAD_DIM)


def _layer_fwd(x, w, s, tag, plans=()):
    h = _rms_fwd(x, s["norm_mix_g"], f"rms_mix_{tag}")
    z = _carry(plans, "in_proj", _mm, h, w["w_in_t"], "nt", F32, f"in_proj_{tag}", tn=512)
    q = _to_heads(z[:, :OFF_K], N_Q_HEADS)
    k = _to_heads(z[:, OFF_K:OFF_V], N_KV_HEADS)
    v = _to_heads(z[:, OFF_V:OFF_U], N_KV_HEADS)
    ya = _from_heads(_carry(plans, "attn_fwd", _attn_fwd, q, k, v, s["q_norm_g"], s["k_norm_g"], s["attn_sinks"],
                            f"attn_fwd_{tag}"))
    y, s_r, s_i = _carry(plans, "ssm_fwd", _ssm_fwd, z, *s["ssm16"], s["ssm_d"], f"ssm_fwd_{tag}")
    pre, y3 = _glu_fwd(y, w["ssm_glu_w"], s["ssm_glu_b"], f"glu_fwd_{tag}")
    a, b, merged = _carry(plans, "merge_fwd", _merge_fwd, ya, y3, w["w_attn_branch"], w["w_ssm_branch"], z,
                          s["gate_bias"], f"merge_fwd_{tag}")
    x1 = _mm(merged, w["w_out"], "nn", F32, f"out_proj_{tag}", residual=x)
    h2 = _rms_fwd(x1, s["norm_ffn_g"], f"rms_ffn_{tag}")
    g, u, act = _carry(plans, "ffn_fwd", _ffn_fwd, h2, w["w_ffn_in"], f"ffn_fwd_{tag}")
    x2 = _mm(act, w["w_ffn_out"], "nn", F32, f"ffn_out_{tag}", residual=x1)
    saved = dict(x=x, h=h, z=z, q=q, k=k, v=v, ya=ya, y=y, s_r=s_r, s_i=s_i, pre=pre, y3=y3, a=a, b=b, merged=merged,
                 x1=x1, h2=h2, g=g, u=u, act=act)
    return x2, saved


def _layer_bwd(dx2, sv, w, s, tag, plans=(), scatter_ffn=None):
    gw, gs = {}, {}
    dg16, du16 = _carry(plans, "ffn_bwd_act", _ffn_bwd_act, dx2, w["w_ffn_out"], sv["g"], sv["u"],
                        f"ffn_bwd_act_{tag}")
    gw["w_ffn_out"] = _mm(sv["act"], dx2, "tn", BF16, f"dw_ffn_out_{tag}", tm=1408)
    dh2 = _carry(plans, "dh2", _ffn_bwd_in, dg16, du16, w["w_ffn_in"], f"dh2_{tag}")
    gw["w_ffn_in"] = _ffn_bwd_w(sv["h2"], dg16, du16, f"dw_ffn_in_{tag}")
    own = () if scatter_ffn is None else (scatter_ffn({n: gw[n] for n in FFN_WEIGHTS}),)
    dx1, gs["norm_ffn_g"] = _rms_bwd(dh2, sv["x1"], s["norm_ffn_g"], dx2, f"rms_ffn_bwd_{tag}")
    dm = _mm(dx1, w["w_out"], "nt", F32, f"dmerged_{tag}")
    gw["w_out"] = _mm(sv["merged"], dx1, "tn", BF16, f"dw_out_{tag}")
    da16, db16, dza, dzs, dba, dbs = _merge_bwd(dm, sv["a"], sv["b"], sv["z"], s["gate_bias"], f"merge_bwd_{tag}")
    gs["gate_bias"] = jnp.concatenate([dba, dbs], axis=1)
    dya = _mm(da16, w["w_attn_branch"], "nt", F32, f"dya_{tag}")
    gw["w_attn_branch"] = _mm(sv["ya"], da16, "tn", BF16, f"dw_attn_branch_{tag}")
    dy3 = _mm(db16, w["w_ssm_branch"], "nt", F32, f"dy3_{tag}")
    gw["w_ssm_branch"] = _mm(sv["y3"], db16, "tn", BF16, f"dw_ssm_branch_{tag}")
    dpre16, t1, y2_16, gs["ssm_glu_b"] = _glu_bwd_gate(dy3, sv["y"], sv["pre"], f"glu_bwd_gate_{tag}")
    dy = _glu_bwd_in(dpre16, w["ssm_glu_w"], t1, sv["y"], f"glu_bwd_in_{tag}")
    gw["ssm_glu_w"] = _mm(y2_16, dpre16, "tn", BF16, f"dw_glu_{tag}")
    du_ssm, *gs["ssm_disc"], gs["ssm_d"] = _ssm_bwd(dy, sv["z"], sv["s_r"], sv["s_i"], *s["ssm16"], s["ssm_d"],
                                                  f"ssm_bwd_{tag}")
    dq, dk, dv, dqg, dkg, dsk = _carry(own, "attn_bwd", _attn_bwd, sv["q"], sv["k"], sv["v"],
                                       _to_heads(dya, N_Q_HEADS), s["q_norm_g"], s["k_norm_g"], s["attn_sinks"],
                                       f"attn_bwd_{tag}")
    gs["q_norm_g"] = jnp.sum(dqg, axis=0)
    gs["k_norm_g"] = jnp.sum(dkg, axis=0)
    gs["attn_sinks"] = dsk[:, :, 0].reshape(1, N_Q_HEADS)
    dz = jnp.concatenate([_from_heads(dq), _from_heads(dk).astype(BF16), _from_heads(dv).astype(BF16), du_ssm, dza,
                          dzs], axis=1)
    dh = _mm(dz, w["w_in_t"], "nn", F32, f"dh_{tag}")
    gw["w_in"] = _mm(dz, sv["h"], "tn", BF16, f"dw_in_t_{tag}", tm=1664)
    dx, gs["norm_mix_g"] = _rms_bwd(dh, sv["x"], s["norm_mix_g"], dx1, f"rms_mix_bwd_{tag}")
    return dx, gw, gs, own


def _shard_to_send(name, shard):
    return (shard.T if name == "w_in" else shard).astype(BF16)


def _assemble(name, gathered):
    if name == "w_ffn_in":
        return gathered
    if name in COL_SHARDED and name != "w_in":
        rows = gathered.shape[2]
        return gathered.transpose(2, 0, 1, 3).reshape(rows, -1)
    return gathered.reshape(-1, gathered.shape[3])


def _disassemble(name, grad):
    if name == "w_ffn_in":
        return grad
    if name in COL_SHARDED and name != "w_in":
        rows, cols = grad.shape
        return grad.reshape(rows, 4, 2, cols // N_DEV).transpose(1, 2, 0, 3)
    rows, cols = grad.shape
    return grad.reshape(4, 2, rows // N_DEV, cols)


def _pack(arrays):
    flat = jnp.concatenate([a.reshape(-1) for a in arrays])
    pad = (-flat.shape[0]) % (SUBLANES * LANES)
    return jnp.pad(flat, (0, pad)).reshape(-1, LANES)


def _unpack(packed, like):
    flat, out, off = packed.reshape(-1), [], 0
    for a in like:
        out.append(flat[off:off + a.size].reshape(a.shape))
        off += a.size
    return out


def kernel(x, norm_mix_g, w_in, gate_bias, q_norm_g, k_norm_g, attn_sinks, ssm_lambda_re, ssm_lambda_im, ssm_log_dt, ssm_b_re, ssm_b_im, ssm_c_re, ssm_c_im, ssm_d, ssm_glu_w, ssm_glu_b, w_attn_branch, w_ssm_branch, w_out, norm_ffn_g, w_ffn_in, w_ffn_out, loss_target, m_norm_mix_g, m_w_in, m_gate_bias, m_q_norm_g, m_k_norm_g, m_attn_sinks, m_ssm_lambda_re, m_ssm_lambda_im, m_ssm_log_dt, m_ssm_b_re, m_ssm_b_im, m_ssm_c_re, m_ssm_c_im, m_ssm_d, m_ssm_glu_w, m_ssm_glu_b, m_w_attn_branch, m_w_ssm_branch, m_w_out, m_norm_ffn_g, m_w_ffn_in, m_w_ffn_out, v_norm_mix_g, v_w_in, v_gate_bias, v_q_norm_g, v_k_norm_g, v_attn_sinks, v_ssm_lambda_re, v_ssm_lambda_im, v_ssm_log_dt, v_ssm_b_re, v_ssm_b_im, v_ssm_c_re, v_ssm_c_im, v_ssm_d, v_ssm_glu_w, v_ssm_glu_b, v_w_attn_branch, v_w_ssm_branch, v_w_out, v_norm_ffn_g, v_w_ffn_in, v_w_ffn_out):
    given = dict(locals())
    wts = {n: given[n] for n in WEIGHTS}
    mom = {n: given["m_" + n] for n in WEIGHTS}
    var = {n: given["v_" + n] for n in WEIGHTS}
    depth = w_in.shape[0]
    xs = x[0]
    target = loss_target[0]

    shards = [{n: _shard_to_send(n, wts[n][l]) for n in BIG} for l in range(depth)]

    def weights_of(gathered):
        wl = {n: _assemble(n, gathered[n]) for n in BIG}
        wl["w_in_t"] = wl.pop("w_in")
        return wl

    full = [weights_of(dict(zip(BIG, _all_gather([shards[0][n] for n in BIG], "gather_weights_0"))))]

    small, disc_vjp = [], []
    for l in range(depth):
        s = {n: wts[n][l].reshape(1, -1) for n in ("norm_mix_g", "gate_bias", "q_norm_g", "k_norm_g", "attn_sinks",
                                                   "ssm_d", "ssm_glu_b", "norm_ffn_g")}
        disc, vjp = jax.vjp(_ssm_discretize, *[wts[n][l] for n in ("ssm_lambda_re", "ssm_lambda_im", "ssm_log_dt",
                                                                  "ssm_b_re", "ssm_b_im", "ssm_c_re", "ssm_c_im")])
        s["ssm16"] = (disc[0], disc[1]) + tuple(d.astype(BF16) for d in disc[2:])
        small.append(s)
        disc_vjp.append(vjp)

    act, saved = xs, []
    for l in range(depth):
        plans = ()
        if l + 1 < depth:
            send = _Behind(_gather_send, shards[l + 1], GATHER_HOSTS)
            plans = (send, _Behind(_gather_pass, send.results, {"ffn_fwd": BIG}))
        act, sv = _layer_fwd(act, full[l], small[l], str(l), plans)
        saved.append(sv)
        if plans:
            full.append(weights_of(plans[1].results))
    dact, loss_local = _loss_grad(act, target, "loss_head")

    out = {"grad": {}, "delta": {}, "new_m": {}, "new_v": {}}
    results = {n: None for n in BIG}
    core = lax.axis_index("c").astype(jnp.int32).reshape(1)
    chip = (2 * lax.axis_index("x") + lax.axis_index("y")).astype(jnp.int32).reshape(1)

    def pair_sums(names, grads, tag):
        parts = [_disassemble(n, grads[n]) for n in names]
        theirs = _pair_exchange(parts, f"grad_pair_exchange_{tag}")
        return {n: _add_pair(core, p, t, f"grad_pair_sum_{n}_{tag}") for n, p, t in zip(names, parts, theirs)}

    def update(layer, sums, got):
        for n in got:
            pieces = _scattered_pieces(sums[n], got[n])
            if n == "w_in":
                pieces = [("plain", _grad_sum(chip, pieces, f"grad_sum_w_in_{layer}").T)]
            results[n] = _adamw_shard(chip, wts[n], mom[n], var[n], layer, pieces, results[n], f"adamw_{n}_{layer}")

    small_grads = [None] * depth
    plans = ()
    for l in reversed(range(depth)):
        def scatter_ffn(grads, tag=str(l)):
            return _Behind(_scatter_send, pair_sums(FFN_WEIGHTS, grads, "ffn_" + tag), {"attn_bwd": FFN_WEIGHTS})

        dact, gw, gs, (ffn_plan,) = _layer_bwd(dact, saved[l], full[l], small[l], str(l), plans, scatter_ffn)
        (gs["ssm_lambda_re"], gs["ssm_lambda_im"], gs["ssm_log_dt"], gs["ssm_b_re"], gs["ssm_b_im"], gs["ssm_c_re"],
         gs["ssm_c_im"]) = disc_vjp[l](tuple(gs.pop("ssm_disc")))
        small_grads[l] = gs
        if plans:
            update(l + 1, plans[0].operands, plans[0].results)
        update(l, ffn_plan.operands, ffn_plan.results)
        mixer = pair_sums(MIXER_WEIGHTS, gw, f"mixer_{l}")
        if l > 0:
            plans = (_Behind(_scatter_send, mixer, SCATTER_HOSTS),)
        else:
            got = _chip_exchange([mixer[n] for n in MIXER_WEIGHTS], "grad_chip_exchange_mixer_0")
            update(0, mixer, dict(zip(MIXER_WEIGHTS, got)))
    loss = lax.psum(loss_local, ("x", "y", "c"))
    for n in BIG:
        for kind, res in zip(("grad", "delta", "new_m", "new_v"), results[n]):
            out[kind][n] = res

    like = [wts[n] for n in SMALL]
    g_small = _pack([jnp.stack([small_grads[l][n].reshape(wts[n].shape[1:]) for l in range(depth)]) for n in SMALL])
    (gathered_small,) = _all_gather([g_small], "gather_small_grads")
    res = _adamw_small(_pack(like), _pack([mom[n] for n in SMALL]), _pack([var[n] for n in SMALL]), gathered_small,
                       "adamw_small")
    for kind, packed in zip(("grad", "delta", "new_m", "new_v"), res):
        for n, a in zip(SMALL, _unpack(packed, like)):
            out[kind][n] = a

    grad_x = dact.reshape(x.shape)
    return (loss, grad_x, *[out["grad"][n] for n in WEIGHTS], *[out["delta"][n] for n in WEIGHTS],
            *[out["new_m"][n] for n in WEIGHTS], *[out["new_v"][n] for n in WEIGHTS])
```

```python
import functools
import math

import jax
import jax.numpy as jnp
from jax import lax
from jax.experimental import pallas as pl
from jax.experimental.pallas import tpu as pltpu

F32, BF16 = jnp.float32, jnp.bfloat16
MESH = pl.DeviceIdType.MESH

D_MODEL = 2048
HEAD_DIM = 64
N_Q_HEADS = 16
N_KV_HEADS = 4
GQA_GROUP = N_Q_HEADS // N_KV_HEADS
ATTN_WIDTH = N_Q_HEADS * HEAD_DIM
KV_WIDTH = N_KV_HEADS * HEAD_DIM
WINDOW = 128
BLOCK = 128
SSM_WIDTH = D_MODEL // 2
SSM_GROUP_CH = 16
SSM_GROUPS = SSM_WIDTH // SSM_GROUP_CH
SSM_STATE = 64
D_FF = 5632
OFF_K = ATTN_WIDTH
OFF_V = OFF_K + KV_WIDTH
OFF_U = OFF_V + KV_WIDTH
OFF_G = OFF_U + SSM_WIDTH
IN_WIDTH = OFF_G + 2 * D_MODEL
RMS_EPS = 1e-6
ATTN_SCALE = HEAD_DIM ** -0.5
NEG_BIG = -1e30

SSM_NGB = 4
SSM_GB_CH = SSM_WIDTH // SSM_NGB
SSM_GB_ST = SSM_GROUPS * SSM_STATE // SSM_NGB
SUBLANES = 8
LANES = 128
SSM_TT = 256

ADAM_LR = 0.001
ADAM_B1 = 0.9
ADAM_B2 = 0.999
ADAM_EPS = 1e-08
ADAM_WD = 0.01
ADAM_STEP = 10

N_DEV = 8
VMEM_LIMIT_BYTES = 52 * 1024 * 1024

BIG = ("w_in", "ssm_glu_w", "w_attn_branch", "w_ssm_branch", "w_out", "w_ffn_in", "w_ffn_out")
COL_SHARDED = ("w_in", "w_attn_branch", "w_ssm_branch", "w_ffn_in")
FFN_WEIGHTS = ("w_ffn_in", "w_ffn_out")
MIXER_WEIGHTS = ("w_in", "ssm_glu_w", "w_attn_branch", "w_ssm_branch", "w_out")
SMALL = ("norm_mix_g", "gate_bias", "q_norm_g", "k_norm_g", "attn_sinks", "ssm_lambda_re", "ssm_lambda_im",
         "ssm_log_dt", "ssm_b_re", "ssm_b_im", "ssm_c_re", "ssm_c_im", "ssm_d", "ssm_glu_b", "norm_ffn_g")
WEIGHTS = ("norm_mix_g", "w_in", "gate_bias", "q_norm_g", "k_norm_g", "attn_sinks", "ssm_lambda_re", "ssm_lambda_im",
           "ssm_log_dt", "ssm_b_re", "ssm_b_im", "ssm_c_re", "ssm_c_im", "ssm_d", "ssm_glu_w", "ssm_glu_b",
           "w_attn_branch", "w_ssm_branch", "w_out", "norm_ffn_g", "w_ffn_in", "w_ffn_out")


def _pick(n, target, mult=LANES):
    best = None
    for t in range(mult, min(n, target) + 1, mult):
        if n % t == 0:
            best = t
    return n if best is None else best


def _params(*sem):
    return pltpu.CompilerParams(dimension_semantics=sem, vmem_limit_bytes=VMEM_LIMIT_BYTES)


def _sigmoid(v):
    return 1.0 / (1.0 + jnp.exp(-v))


_GELU_C = math.sqrt(2.0 / math.pi)


def _gelu(v):
    return 0.5 * v * (1.0 + jnp.tanh(_GELU_C * (v + 0.044715 * v * v * v)))


def _gelu_grad(v):
    t = jnp.tanh(_GELU_C * (v + 0.044715 * v * v * v))
    return 0.5 * (1.0 + t) + 0.5 * v * (1.0 - t * t) * _GELU_C * (1.0 + 3.0 * 0.044715 * v * v)


_DN = {"nn": (((1,), (0,)), ((), ())), "nt": (((1,), (1,)), ((), ())), "tn": (((0,), (0,)), ((), ()))}


def _dot(a, b, dims="nn"):
    return lax.dot_general(a, b, _DN[dims], preferred_element_type=F32)


class _Comm:
    def __init__(self, args, out_shapes, sems, copies, aliases=None):
        self.args, self.out_shapes, self.sems = list(args), list(out_shapes), list(sems)
        self.copies, self.aliases = copies, dict(aliases or {})


def _call(body, name, grid, in_specs, out_specs, out_shape, scratch_shapes, semantics, args, comm=None):
    in_specs, out_specs, out_shape = list(in_specs), list(out_specs), list(out_shape)
    scratch_shapes = list(scratch_shapes)
    if comm is None:
        res = pl.pallas_call(body, name=name, grid=grid, in_specs=in_specs, out_specs=out_specs, out_shape=out_shape,
                             scratch_shapes=scratch_shapes, compiler_params=_params(*semantics))(*args)
        return list(res), []
    n_in, n_out, n_scr = len(in_specs), len(out_specs), len(scratch_shapes)
    n_cin, n_cout = len(comm.args), len(comm.out_shapes)

    def carrying(*refs):
        ins, cin = refs[:n_in], refs[n_in:n_in + n_cin]
        o0 = n_in + n_cin
        outs, cout = refs[o0:o0 + n_out], refs[o0 + n_out:o0 + n_out + n_cout]
        s0 = o0 + n_out + n_cout
        scr, sems = refs[s0:s0 + n_scr], refs[s0 + n_scr:]
        first = functools.reduce(jnp.logical_and, [pl.program_id(d) == 0 for d in range(len(grid))])
        last = functools.reduce(jnp.logical_and, [pl.program_id(d) == grid[d] - 1 for d in range(len(grid))])

        @pl.when(first)
        def _():
            sends, _, local = comm.copies(cin, cout, sems, False)
            for cp in local + sends:
                cp.start()

        body(*ins, *outs, *scr)

        @pl.when(last)
        def _():
            sends, recvs, local = comm.copies(cin, cout, sems, True)
            for cp in recvs:
                cp.wait_recv()
            for cp in sends:
                cp.wait_send()
            for cp in local:
                cp.wait()

    res = pl.pallas_call(
        carrying, name=name, grid=grid, in_specs=in_specs + [_ANY] * n_cin, out_specs=out_specs + [_ANY] * n_cout,
        out_shape=out_shape + comm.out_shapes, scratch_shapes=scratch_shapes + comm.sems,
        input_output_aliases={n_in + i: n_out + o for i, o in comm.aliases.items()},
        compiler_params=_params(*["arbitrary"] * len(grid)))(*args, *comm.args)
    return list(res[:n_out]), list(res[n_out:])


def _mm(a, b, dims, out_dtype, name, residual=None, tm=1024, tn=1024, tk=2048, comm=None):
    if dims == "tn":
        K, M = a.shape
    else:
        M, K = a.shape
    N = b.shape[0] if dims == "nt" else b.shape[1]
    tm, tn, tk = _pick(M, tm), _pick(N, tn), _pick(K, tk)
    nk = K // tk
    has_res = residual is not None

    def finish(out, refs):
        if has_res:
            out = out + refs[2][...].astype(F32)
        refs[-2][...] = out.astype(out_dtype)

    def body_single(*refs):
        finish(_dot(refs[0][...].astype(BF16), refs[1][...].astype(BF16), dims), refs)

    def body_multi(*refs):
        acc_ref = refs[-1]
        k = pl.program_id(2)

        @pl.when(k == 0)
        def _():
            acc_ref[...] = jnp.zeros_like(acc_ref)

        acc_ref[...] += _dot(refs[0][...].astype(BF16), refs[1][...].astype(BF16), dims)

        @pl.when(k == nk - 1)
        def _():
            finish(acc_ref[...], refs)

    a_spec = (pl.BlockSpec((tk, tm), lambda i, j, k: (k, i)) if dims == "tn"
              else pl.BlockSpec((tm, tk), lambda i, j, k: (i, k)))
    b_spec = (pl.BlockSpec((tn, tk), lambda i, j, k: (j, k)) if dims == "nt"
              else pl.BlockSpec((tk, tn), lambda i, j, k: (k, j)))
    o_spec = pl.BlockSpec((tm, tn), lambda i, j, k: (i, j))
    in_specs = [a_spec, b_spec] + ([o_spec] if has_res else [])
    args = (a, b) + ((residual,) if has_res else ())
    (out,), extra = _call(
        body_single if nk == 1 else body_multi, name, (M // tm, N // tn, nk), in_specs, [o_spec],
        [jax.ShapeDtypeStruct((M, N), out_dtype)], [pltpu.VMEM((tm, tn) if nk > 1 else (SUBLANES, LANES), F32)],
        ("parallel", "parallel", "arbitrary"), args, comm)
    return out if comm is None else (out, extra)


def _rms_fwd(x, g, name):
    T, D = x.shape
    tr = _pick(T, 256, SUBLANES)

    def body(x_ref, g_ref, o_ref):
        xf = x_ref[...]
        r = lax.rsqrt(jnp.mean(xf * xf, axis=-1, keepdims=True) + RMS_EPS)
        o_ref[...] = (xf * r * g_ref[...]).astype(BF16)

    return pl.pallas_call(
        body, name=name, grid=(T // tr,),
        in_specs=[pl.BlockSpec((tr, D), lambda i: (i, 0)), pl.BlockSpec((1, D), lambda i: (0, 0))],
        out_specs=pl.BlockSpec((tr, D), lambda i: (i, 0)), out_shape=jax.ShapeDtypeStruct((T, D), BF16),
        compiler_params=_params("parallel"))(x, g)


def _rms_bwd(dh, x, g, dres, name):
    T, D = x.shape
    tr = _pick(T, 256, SUBLANES)

    def body(dh_ref, x_ref, g_ref, dres_ref, dx_ref, dg_ref):
        @pl.when(pl.program_id(0) == 0)
        def _():
            dg_ref[...] = jnp.zeros_like(dg_ref)

        xf = x_ref[...]
        r = lax.rsqrt(jnp.mean(xf * xf, axis=-1, keepdims=True) + RMS_EPS)
        xhat = xf * r
        dhv = dh_ref[...]
        dxh = dhv * g_ref[...]
        dx_ref[...] = dres_ref[...] + r * (dxh - xhat * jnp.mean(dxh * xhat, axis=-1, keepdims=True))
        dg_ref[...] += jnp.sum(dhv * xhat, axis=0, keepdims=True)

    row = pl.BlockSpec((tr, D), lambda i: (i, 0))
    vec = pl.BlockSpec((1, D), lambda i: (0, 0))
    return pl.pallas_call(
        body, name=name, grid=(T // tr,), in_specs=[row, row, vec, row], out_specs=[row, vec],
        out_shape=[jax.ShapeDtypeStruct((T, D), F32), jax.ShapeDtypeStruct((1, D), F32)],
        compiler_params=_params("arbitrary"))(dh, x, g, dres)


def _loss_grad(y, target, name):
    T, D = y.shape
    tr = _pick(T, 256, SUBLANES)

    def body(y_ref, t_ref, dx_ref, loss_ref):
        @pl.when(pl.program_id(0) == 0)
        def _():
            loss_ref[...] = jnp.zeros_like(loss_ref)

        err = y_ref[...] - t_ref[...]
        dx_ref[...] = err * (1.0 / D)
        loss_ref[...] += jnp.sum(jnp.mean(err * err, axis=-1, keepdims=True), axis=0, keepdims=True) * 0.5

    row = pl.BlockSpec((tr, D), lambda i: (i, 0))
    one = pl.BlockSpec((1, 1), lambda i: (0, 0))
    dx, loss = pl.pallas_call(
        body, name=name, grid=(T // tr,), in_specs=[row, row], out_specs=[row, one],
        out_shape=[jax.ShapeDtypeStruct((T, D), F32), jax.ShapeDtypeStruct((1, 1), F32)],
        compiler_params=_params("arbitrary"))(y, target)
    return dx, loss[0, 0]


def _alibi_slope(h):
    return 2.0 ** (-8.0 * (h + 1) / N_Q_HEADS)


def _attn_mask():
    row = lax.broadcasted_iota(jnp.int32, (BLOCK, 2 * BLOCK), 0)
    col = lax.broadcasted_iota(jnp.int32, (BLOCK, 2 * BLOCK), 1)
    dist = row - col + BLOCK
    return dist.astype(F32), (dist >= 0) & (dist < WINDOW), col


def _head_norm(v, gain):
    r = lax.rsqrt(jnp.mean(v * v, axis=-1, keepdims=True) + RMS_EPS)
    vhat = v * r
    return r, vhat, vhat * gain


def _attn_probs(qn16, kn16, slope, dist, valid, sink):
    s = _dot(qn16, kn16, "nt") * ATTN_SCALE - slope * dist
    s = jnp.where(valid, s, NEG_BIG)
    m = jnp.maximum(jnp.max(s, axis=-1, keepdims=True), sink)
    p = jnp.exp(s - m)
    ps = jnp.exp(sink - m)
    den = jnp.sum(p, axis=-1, keepdims=True) + ps
    return p, ps, den


def _attn_specs(T):
    q_spec = pl.BlockSpec((GQA_GROUP, BLOCK, HEAD_DIM), lambda h, n: (h, n, 0))
    cur = pl.BlockSpec((None, BLOCK, HEAD_DIM), lambda h, n: (h, n, 0))
    prev = pl.BlockSpec((None, BLOCK, HEAD_DIM), lambda h, n: (h, jnp.maximum(n - 1, 0), 0))
    gain = pl.BlockSpec((1, HEAD_DIM), lambda h, n: (0, 0))
    sink = pl.BlockSpec((None, GQA_GROUP, LANES), lambda h, n: (h, 0, 0))
    return q_spec, cur, prev, gain, sink


def _sink_rows(sinks):
    return jnp.broadcast_to(sinks.reshape(N_KV_HEADS, GQA_GROUP, 1), (N_KV_HEADS, GQA_GROUP, LANES))


def _attn_fwd(q, k, v, qg, kg, sinks, name, comm=None):
    T = q.shape[1]
    nb = T // BLOCK

    def body(q_ref, kc_ref, kp_ref, vc_ref, vp_ref, qg_ref, kg_ref, sk_ref, o_ref):
        kv = pl.program_id(0)
        n = pl.program_id(1)
        dist, valid, col = _attn_mask()
        valid = valid & ((col >= BLOCK) | (n > 0))
        kk = jnp.concatenate([kp_ref[...], kc_ref[...]], axis=0)
        _, _, kn = _head_norm(kk, kg_ref[...])
        kn16 = kn.astype(BF16)
        v16 = jnp.concatenate([vp_ref[...], vc_ref[...]], axis=0).astype(BF16)
        for g in range(GQA_GROUP):
            slope = sum(jnp.where(kv == j, _alibi_slope(j * GQA_GROUP + g), 0.0) for j in range(N_KV_HEADS))
            _, _, qn = _head_norm(q_ref[g], qg_ref[...])
            sink = sk_ref[g:g + 1, 0:1]
            p, _, den = _attn_probs(qn.astype(BF16), kn16, slope, dist, valid, sink)
            o_ref[g] = (_dot((p / den).astype(BF16), v16) ).astype(BF16)

    q_spec, cur, prev, gain, sink = _attn_specs(T)
    (out,), extra = _call(
        body, name, (N_KV_HEADS, nb), [q_spec, cur, prev, cur, prev, gain, gain, sink], [q_spec],
        [jax.ShapeDtypeStruct((N_Q_HEADS, T, HEAD_DIM), BF16)], [], ("parallel", "parallel"),
        (q, k, k, v, v, qg, kg, _sink_rows(sinks)), comm)
    return out if comm is None else (out, extra)


def _attn_bwd(q, k, v, do, qg, kg, sinks, name, comm=None):
    T = q.shape[1]
    nb = T // BLOCK

    def body(q_ref, kc_ref, kp_ref, vc_ref, vp_ref, do_ref, qg_ref, kg_ref, sk_ref,
             dq_ref, dk_ref, dv_ref, dqg_ref, dkg_ref, dsk_ref):
        kv = pl.program_id(0)
        n = pl.program_id(1)

        @pl.when(n == 0)
        def _():
            dk_ref[...] = jnp.zeros_like(dk_ref)
            dv_ref[...] = jnp.zeros_like(dv_ref)
            dqg_ref[...] = jnp.zeros_like(dqg_ref)
            dkg_ref[...] = jnp.zeros_like(dkg_ref)
            dsk_ref[...] = jnp.zeros_like(dsk_ref)

        dist, valid, col = _attn_mask()
        valid = valid & ((col >= BLOCK) | (n > 0))
        kk = jnp.concatenate([kp_ref[...], kc_ref[...]], axis=0)
        rk, khat, kn = _head_norm(kk, kg_ref[...])
        kn16 = kn.astype(BF16)
        v16 = jnp.concatenate([vp_ref[...], vc_ref[...]], axis=0).astype(BF16)
        dkn = jnp.zeros((2 * BLOCK, HEAD_DIM), F32)
        dv = jnp.zeros((2 * BLOCK, HEAD_DIM), F32)
        dqg = jnp.zeros((1, HEAD_DIM), F32)
        for g in range(GQA_GROUP):
            slope = sum(jnp.where(kv == j, _alibi_slope(j * GQA_GROUP + g), 0.0) for j in range(N_KV_HEADS))
            rq, qhat, qn = _head_norm(q_ref[g], qg_ref[...])
            qn16 = qn.astype(BF16)
            sink = sk_ref[g:g + 1, 0:1]
            p, ps, den = _attn_probs(qn16, kn16, slope, dist, valid, sink)
            pn = p / den
            do16 = do_ref[g].astype(BF16)
            dp = _dot(do16, v16, "nt")
            delta = jnp.sum(pn * dp, axis=-1, keepdims=True)
            ds16 = (pn * (dp - delta)).astype(BF16)
            dsink = -jnp.sum(ps / den * delta, axis=0, keepdims=True)
            dsk_ref[g:g + 1, :] += jnp.broadcast_to(dsink, (1, LANES))
            dqn = _dot(ds16, kn16) * ATTN_SCALE
            dkn = dkn + _dot(ds16, qn16, "tn") * ATTN_SCALE
            dv = dv + _dot(pn.astype(BF16), do16, "tn")
            dqh = dqn * qg_ref[...]
            dq_ref[g] = (rq * (dqh - qhat * jnp.mean(dqh * qhat, axis=-1, keepdims=True))).astype(BF16)
            dqg = dqg + jnp.sum(dqn * qhat, axis=0, keepdims=True)
        dqg_ref[...] += dqg
        dkg_ref[...] += jnp.sum(dkn * khat, axis=0, keepdims=True)
        dkh = dkn * kg_ref[...]
        dk = rk * (dkh - khat * jnp.mean(dkh * khat, axis=-1, keepdims=True))
        rows = pl.ds(pl.multiple_of(n * BLOCK, BLOCK), BLOCK)
        dk_ref[rows, :] += dk[BLOCK:]
        dv_ref[rows, :] += dv[BLOCK:]

        @pl.when(n > 0)
        def _():
            before = pl.ds(pl.multiple_of((n - 1) * BLOCK, BLOCK), BLOCK)
            dk_ref[before, :] += dk[:BLOCK]
            dv_ref[before, :] += dv[:BLOCK]

    q_spec, cur, prev, gain, sink = _attn_specs(T)
    whole = pl.BlockSpec((None, T, HEAD_DIM), lambda h, n: (h, 0, 0))
    gacc = pl.BlockSpec((None, 1, HEAD_DIM), lambda h, n: (h, 0, 0))
    outs, extra = _call(
        body, name, (N_KV_HEADS, nb), [q_spec, cur, prev, cur, prev, q_spec, gain, gain, sink],
        [q_spec, whole, whole, gacc, gacc, sink],
        [jax.ShapeDtypeStruct((N_Q_HEADS, T, HEAD_DIM), BF16),
         jax.ShapeDtypeStruct((N_KV_HEADS, T, HEAD_DIM), F32),
         jax.ShapeDtypeStruct((N_KV_HEADS, T, HEAD_DIM), F32),
         jax.ShapeDtypeStruct((N_KV_HEADS, 1, HEAD_DIM), F32),
         jax.ShapeDtypeStruct((N_KV_HEADS, 1, HEAD_DIM), F32),
         jax.ShapeDtypeStruct((N_KV_HEADS, GQA_GROUP, LANES), F32)],
        [], ("parallel", "arbitrary"), (q, k, k, v, v, do, qg, kg, _sink_rows(sinks)), comm)
    return outs if comm is None else (outs, extra)


def _ssm_discretize(lam_re, lam_im, log_dt, b_re, b_im, c_re, c_im):
    dt = jnp.exp(log_dt)[:, None]
    mag = jnp.exp(lam_re * dt)
    ar = mag * jnp.cos(lam_im * dt)
    ai = mag * jnp.sin(lam_im * dt)
    den = lam_re * lam_re + lam_im * lam_im
    fr = ((ar - 1.0) * lam_re + ai * lam_im) / den
    fi = (ai * lam_re - (ar - 1.0) * lam_im) / den
    bbar_r = fr[:, :, None] * b_re - fi[:, :, None] * b_im
    bbar_i = fr[:, :, None] * b_im + fi[:, :, None] * b_re
    gl = SSM_GROUPS // SSM_NGB
    eye = jnp.eye(gl, dtype=F32)

    def tiles(a):
        return a.reshape(SSM_NGB, SUBLANES, LANES)

    def bdiag(bb):
        return jnp.einsum("bgph,gk->bghkp", bb.reshape(SSM_NGB, gl, SSM_STATE, SSM_GROUP_CH), eye).reshape(
            SSM_NGB, SSM_GB_CH, SSM_GB_ST)

    def cdiag(cc):
        return jnp.einsum("bghp,gk->bgpkh", cc.reshape(SSM_NGB, gl, SSM_GROUP_CH, SSM_STATE), eye).reshape(
            SSM_NGB, SSM_GB_ST, SSM_GB_CH)

    return tiles(ar), tiles(ai), bdiag(bbar_r), bdiag(bbar_i), cdiag(c_re), cdiag(c_im)


def _to_time_major(dst, val, tt, first_row=0):
    for j in range(SUBLANES):
        dst[pl.ds(first_row + j, tt, stride=SUBLANES), :] = val[:, j * LANES:(j + 1) * LANES]


def _from_time_major(dst, src, tt):
    for j in range(SUBLANES):
        dst[:, j * LANES:(j + 1) * LANES] = src[pl.ds(j, tt, stride=SUBLANES), :]


def _ssm_fwd(z, ar, ai, bbr, bbi, cbr, cbi, dskip, name, comm=None):
    T = z.shape[0]
    tt = min(SSM_TT, T)
    nt = T // tt

    def body(u_ref, ar_ref, ai_ref, br_ref, bi_ref, cr_ref, ci_ref, d_ref, y_ref, sr_ref, si_ref,
             tmr, tmi, car_r, car_i):
        @pl.when(pl.program_id(1) == 0)
        def _():
            car_r[...] = jnp.zeros_like(car_r)
            car_i[...] = jnp.zeros_like(car_i)

        u = u_ref[...]
        u16 = u.astype(BF16)
        _to_time_major(tmr, _dot(u16, br_ref[...]), tt)
        _to_time_major(tmi, _dot(u16, bi_ref[...]), tt)
        a_r = ar_ref[...]
        a_i = ai_ref[...]

        def step(t, carry):
            s_r, s_i = carry
            rows = pl.ds(pl.multiple_of(t * SUBLANES, SUBLANES), SUBLANES)
            n_r = a_r * s_r - a_i * s_i + tmr[rows, :]
            n_i = a_r * s_i + a_i * s_r + tmi[rows, :]
            tmr[rows, :] = n_r
            tmi[rows, :] = n_i
            return n_r, n_i

        s_r, s_i = lax.fori_loop(0, tt, step, (car_r[...], car_i[...]), unroll=8)
        car_r[...] = s_r
        car_i[...] = s_i
        _from_time_major(sr_ref, tmr, tt)
        _from_time_major(si_ref, tmi, tt)
        y_ref[...] = (_dot(sr_ref[...].astype(BF16), cr_ref[...]) - _dot(si_ref[...].astype(BF16), ci_ref[...])
                      + d_ref[...] * u)

    u_spec = pl.BlockSpec((tt, SSM_GB_CH), lambda b, t: (t, OFF_U // SSM_GB_CH + b))
    a_spec = pl.BlockSpec((None, SUBLANES, LANES), lambda b, t: (b, 0, 0))
    b_spec = pl.BlockSpec((None, SSM_GB_CH, SSM_GB_ST), lambda b, t: (b, 0, 0))
    c_spec = pl.BlockSpec((None, SSM_GB_ST, SSM_GB_CH), lambda b, t: (b, 0, 0))
    d_spec = pl.BlockSpec((1, SSM_GB_CH), lambda b, t: (0, b))
    y_spec = pl.BlockSpec((tt, SSM_GB_CH), lambda b, t: (t, b))
    s_spec = pl.BlockSpec((tt, SSM_GB_ST), lambda b, t: (t, b))
    n_state = SSM_NGB * SSM_GB_ST
    outs, extra = _call(
        body, name, (SSM_NGB, nt), [u_spec, a_spec, a_spec, b_spec, b_spec, c_spec, c_spec, d_spec],
        [y_spec, s_spec, s_spec],
        [jax.ShapeDtypeStruct((T, SSM_WIDTH), F32), jax.ShapeDtypeStruct((T, n_state), F32),
         jax.ShapeDtypeStruct((T, n_state), F32)],
        [pltpu.VMEM((tt * SUBLANES, LANES), F32), pltpu.VMEM((tt * SUBLANES, LANES), F32),
         pltpu.VMEM((SUBLANES, LANES), F32), pltpu.VMEM((SUBLANES, LANES), F32)],
        ("parallel", "arbitrary"), (z, ar, ai, bbr, bbi, cbr, cbi, dskip), comm)
    return outs if comm is None else (outs, extra)


def _ssm_bwd(dy, z, s_r, s_i, ar, ai, bbr, bbi, cbr, cbi, dskip, name, comm=None):
    T = z.shape[0]
    tt = min(SSM_TT, T)
    nt = T // tt
    per8 = tt // SUBLANES

    def body(dy_ref, u_ref, sr_ref, si_ref, srp_ref, sip_ref, ar_ref, ai_ref, br_ref, bi_ref, cr_ref, ci_ref, d_ref,
             du_ref, dar_ref, dai_ref, dbr_ref, dbi_ref, dcr_ref, dci_ref, dd_ref,
             tmr, tmi, smr, smi, natr, nati, car_r, car_i):
        tb = pl.program_id(1)
        first_block = tb == nt - 1

        @pl.when(tb == 0)
        def _():
            for ref in (car_r, car_i, dar_ref, dai_ref, dbr_ref, dbi_ref, dcr_ref, dci_ref, dd_ref):
                ref[...] = jnp.zeros_like(ref)

        dy = dy_ref[...]
        dy16 = dy.astype(BF16)
        u = u_ref[...]
        u16 = u.astype(BF16)
        _to_time_major(tmr, _dot(dy16, cr_ref[...], "nt"), tt)
        _to_time_major(tmi, -_dot(dy16, ci_ref[...], "nt"), tt)
        _to_time_major(smr, sr_ref[...], tt, first_row=SUBLANES)
        _to_time_major(smi, si_ref[...], tt, first_row=SUBLANES)
        keep = jnp.where(first_block, 0.0, 1.0)
        for j in range(SUBLANES):
            smr[j:j + 1, :] = srp_ref[SUBLANES - 1:SUBLANES, j * LANES:(j + 1) * LANES] * keep
            smi[j:j + 1, :] = sip_ref[SUBLANES - 1:SUBLANES, j * LANES:(j + 1) * LANES] * keep
        a_r = ar_ref[...]
        a_i = ai_ref[...]

        def step(i, carry):
            n_r, n_i, da_r, da_i = carry
            rows = pl.ds(pl.multiple_of((tt - 1 - i) * SUBLANES, SUBLANES), SUBLANES)
            g_r = tmr[rows, :] + a_r * n_r + a_i * n_i
            g_i = tmi[rows, :] - a_i * n_r + a_r * n_i
            tmr[rows, :] = g_r
            tmi[rows, :] = g_i
            p_r = smr[rows, :]
            p_i = smi[rows, :]
            return g_r, g_i, da_r + g_r * p_r + g_i * p_i, da_i - g_r * p_i + g_i * p_r

        zero = jnp.zeros((SUBLANES, LANES), F32)
        n_r, n_i, da_r, da_i = lax.fori_loop(0, tt, step, (car_r[...], car_i[...], zero, zero), unroll=8)
        car_r[...] = n_r
        car_i[...] = n_i
        dar_ref[...] += da_r
        dai_ref[...] += da_i
        _from_time_major(natr, tmr, tt)
        _from_time_major(nati, tmi, tt)
        dbu_r16 = natr[...].astype(BF16)
        dbu_i16 = nati[...].astype(BF16)
        du_ref[...] = (_dot(dbu_r16, br_ref[...], "nt") + _dot(dbu_i16, bi_ref[...], "nt")
                       + d_ref[...] * dy).astype(BF16)
        dbr_ref[...] += _dot(u16, dbu_r16, "tn")
        dbi_ref[...] += _dot(u16, dbu_i16, "tn")
        dcr_ref[...] += _dot(sr_ref[...].astype(BF16), dy16, "tn")
        dci_ref[...] -= _dot(si_ref[...].astype(BF16), dy16, "tn")
        dd_ref[...] += jnp.sum(dy * u, axis=0, keepdims=True)

    def rev(t):
        return nt - 1 - t

    dy_spec = pl.BlockSpec((tt, SSM_GB_CH), lambda b, t: (rev(t), b))
    u_spec = pl.BlockSpec((tt, SSM_GB_CH), lambda b, t: (rev(t), OFF_U // SSM_GB_CH + b))
    s_spec = pl.BlockSpec((tt, SSM_GB_ST), lambda b, t: (rev(t), b))
    sp_spec = pl.BlockSpec((SUBLANES, SSM_GB_ST), lambda b, t: (jnp.maximum(rev(t) * per8 - 1, 0), b))
    a_spec = pl.BlockSpec((None, SUBLANES, LANES), lambda b, t: (b, 0, 0))
    b_spec = pl.BlockSpec((None, SSM_GB_CH, SSM_GB_ST), lambda b, t: (b, 0, 0))
    c_spec = pl.BlockSpec((None, SSM_GB_ST, SSM_GB_CH), lambda b, t: (b, 0, 0))
    d_spec = pl.BlockSpec((1, SSM_GB_CH), lambda b, t: (0, b))
    tm_shape = pltpu.VMEM((tt * SUBLANES, LANES), F32)
    sm_shape = pltpu.VMEM(((tt + 1) * SUBLANES, LANES), F32)
    nat_shape = pltpu.VMEM((tt, SSM_GB_ST), F32)
    tile = pltpu.VMEM((SUBLANES, LANES), F32)
    outs, extra = _call(
        body, name, (SSM_NGB, nt),
        [dy_spec, u_spec, s_spec, s_spec, sp_spec, sp_spec, a_spec, a_spec, b_spec, b_spec, c_spec, c_spec, d_spec],
        [dy_spec, a_spec, a_spec, b_spec, b_spec, c_spec, c_spec, d_spec],
        [jax.ShapeDtypeStruct((T, SSM_WIDTH), BF16),
         jax.ShapeDtypeStruct((SSM_NGB, SUBLANES, LANES), F32),
         jax.ShapeDtypeStruct((SSM_NGB, SUBLANES, LANES), F32),
         jax.ShapeDtypeStruct((SSM_NGB, SSM_GB_CH, SSM_GB_ST), F32),
         jax.ShapeDtypeStruct((SSM_NGB, SSM_GB_CH, SSM_GB_ST), F32),
         jax.ShapeDtypeStruct((SSM_NGB, SSM_GB_ST, SSM_GB_CH), F32),
         jax.ShapeDtypeStruct((SSM_NGB, SSM_GB_ST, SSM_GB_CH), F32),
         jax.ShapeDtypeStruct((1, SSM_WIDTH), F32)],
        [tm_shape, tm_shape, sm_shape, sm_shape, nat_shape, nat_shape, tile, tile], ("parallel", "arbitrary"),
        (dy, z, s_r, s_i, s_r, s_i, ar, ai, bbr, bbi, cbr, cbi, dskip), comm)
    return outs if comm is None else (outs, extra)


def _glu_fwd(y, w, b, name):
    T, W = y.shape
    tm = _pick(T, 512)

    def body(y_ref, w_ref, b_ref, pre_ref, y3_ref):
        y2 = _gelu(y_ref[...])
        pre = _dot(y2.astype(BF16), w_ref[...]) + b_ref[...]
        pre_ref[...] = pre
        y3_ref[...] = (y2 * _sigmoid(pre)).astype(BF16)

    row = pl.BlockSpec((tm, W), lambda i: (i, 0))
    return pl.pallas_call(
        body, name=name, grid=(T // tm,),
        in_specs=[row, pl.BlockSpec((W, W), lambda i: (0, 0)), pl.BlockSpec((1, W), lambda i: (0, 0))],
        out_specs=[row, row], out_shape=[jax.ShapeDtypeStruct((T, W), F32), jax.ShapeDtypeStruct((T, W), BF16)],
        compiler_params=_params("parallel"))(y, w, b)


def _glu_bwd_gate(dy3, y, pre, name):
    T, W = y.shape
    tm = _pick(T, 512)

    def body(dy3_ref, y_ref, pre_ref, dpre_ref, t1_ref, y2_ref, db_ref):
        @pl.when(pl.program_id(0) == 0)
        def _():
            db_ref[...] = jnp.zeros_like(db_ref)

        y2 = _gelu(y_ref[...])
        sg = _sigmoid(pre_ref[...])
        dy3 = dy3_ref[...]
        dpre = dy3 * y2 * sg * (1.0 - sg)
        dpre_ref[...] = dpre.astype(BF16)
        t1_ref[...] = dy3 * sg
        y2_ref[...] = y2.astype(BF16)
        db_ref[...] += jnp.sum(dpre, axis=0, keepdims=True)

    row = pl.BlockSpec((tm, W), lambda i: (i, 0))
    vec = pl.BlockSpec((1, W), lambda i: (0, 0))
    return pl.pallas_call(
        body, name=name, grid=(T // tm,), in_specs=[row, row, row], out_specs=[row, row, row, vec],
        out_shape=[jax.ShapeDtypeStruct((T, W), BF16), jax.ShapeDtypeStruct((T, W), F32),
                   jax.ShapeDtypeStruct((T, W), BF16), jax.ShapeDtypeStruct((1, W), F32)],
        compiler_params=_params("arbitrary"))(dy3, y, pre)


def _glu_bwd_in(dpre, w, t1, y, name):
    T, W = y.shape
    tm = _pick(T, 512)

    def body(dpre_ref, w_ref, t1_ref, y_ref, dy_ref):
        dy_ref[...] = (_dot(dpre_ref[...], w_ref[...], "nt") + t1_ref[...]) * _gelu_grad(y_ref[...])

    row = pl.BlockSpec((tm, W), lambda i: (i, 0))
    return pl.pallas_call(
        body, name=name, grid=(T // tm,), in_specs=[row, pl.BlockSpec((W, W), lambda i: (0, 0)), row, row],
        out_specs=row, out_shape=jax.ShapeDtypeStruct((T, W), F32),
        compiler_params=_params("parallel"))(dpre, w, t1, y)


def _merge_fwd(ya, y3, wa, ws, z, bias, name, comm=None):
    T, W = ya.shape
    D = wa.shape[1]
    tm, tn = _pick(T, 512), _pick(D, 512)

    def body(ya_ref, y3_ref, wa_ref, ws_ref, za_ref, zs_ref, ba_ref, bs_ref, a_ref, b_ref, m_ref):
        a = _dot(ya_ref[...], wa_ref[...])
        b = _dot(y3_ref[...], ws_ref[...])
        a_ref[...] = a
        b_ref[...] = b
        m_ref[...] = (_sigmoid(za_ref[...] + ba_ref[...]) * a + _sigmoid(zs_ref[...] + bs_ref[...]) * b).astype(BF16)

    act = pl.BlockSpec((tm, W), lambda i, j: (i, 0))
    wgt = pl.BlockSpec((W, tn), lambda i, j: (0, j))
    za = pl.BlockSpec((tm, tn), lambda i, j: (i, OFF_G // tn + j))
    zs = pl.BlockSpec((tm, tn), lambda i, j: (i, (OFF_G + D) // tn + j))
    ba = pl.BlockSpec((1, tn), lambda i, j: (0, j))
    bs = pl.BlockSpec((1, tn), lambda i, j: (0, D // tn + j))
    out = pl.BlockSpec((tm, tn), lambda i, j: (i, j))
    outs, extra = _call(
        body, name, (T // tm, D // tn), [act, act, wgt, wgt, za, zs, ba, bs], [out, out, out],
        [jax.ShapeDtypeStruct((T, D), F32), jax.ShapeDtypeStruct((T, D), F32), jax.ShapeDtypeStruct((T, D), BF16)],
        [], ("parallel", "parallel"), (ya, y3, wa, ws, z, z, bias, bias), comm)
    return outs if comm is None else (outs, extra)


def _merge_bwd(dm, a, b, z, bias, name):
    T, D = dm.shape
    tm, tn = _pick(T, 512), _pick(D, 512)

    def body(dm_ref, a_ref, b_ref, za_ref, zs_ref, ba_ref, bs_ref, da_ref, db_ref, dza_ref, dzs_ref, dba_ref, dbs_ref):
        @pl.when(pl.program_id(1) == 0)
        def _():
            dba_ref[...] = jnp.zeros_like(dba_ref)
            dbs_ref[...] = jnp.zeros_like(dbs_ref)

        dm = dm_ref[...]
        sa = _sigmoid(za_ref[...] + ba_ref[...])
        ss = _sigmoid(zs_ref[...] + bs_ref[...])
        da_ref[...] = (dm * sa).astype(BF16)
        db_ref[...] = (dm * ss).astype(BF16)
        dza = dm * a_ref[...] * sa * (1.0 - sa)
        dzs = dm * b_ref[...] * ss * (1.0 - ss)
        dza_ref[...] = dza.astype(BF16)
        dzs_ref[...] = dzs.astype(BF16)
        dba_ref[...] += jnp.sum(dza, axis=0, keepdims=True)
        dbs_ref[...] += jnp.sum(dzs, axis=0, keepdims=True)

    blk = pl.BlockSpec((tm, tn), lambda j, i: (i, j))
    za = pl.BlockSpec((tm, tn), lambda j, i: (i, OFF_G // tn + j))
    zs = pl.BlockSpec((tm, tn), lambda j, i: (i, (OFF_G + D) // tn + j))
    ba = pl.BlockSpec((1, tn), lambda j, i: (0, j))
    bs = pl.BlockSpec((1, tn), lambda j, i: (0, D // tn + j))
    big = jax.ShapeDtypeStruct((T, D), BF16)
    vec = jax.ShapeDtypeStruct((1, D), F32)
    return pl.pallas_call(
        body, name=name, grid=(D // tn, T // tm), in_specs=[blk, blk, blk, za, zs, ba, bs],
        out_specs=[blk, blk, blk, blk, ba, ba], out_shape=[big, big, big, big, vec, vec],
        compiler_params=_params("parallel", "arbitrary"))(dm, a, b, z, z, bias, bias)


_HALF = N_DEV // 2


def _wgu_block(d):
    return d // 2, d % 2


def _ffn_fwd(h, wgu, name, comm=None):
    T, D = h.shape
    n = wgu.shape[3]
    F = _HALF * n
    tm = _pick(T, 512)

    def body(h_ref, wg_ref, wu_ref, g_ref, u_ref, act_ref):
        hv = h_ref[...]
        g = _dot(hv, wg_ref[...])
        u = _dot(hv, wu_ref[...])
        g_ref[...] = g
        u_ref[...] = u
        act_ref[...] = (g * _sigmoid(g) * u).astype(BF16)

    wg = pl.BlockSpec((None, None, D, n), lambda j, i: (*_wgu_block(j), 0, 0))
    wu = pl.BlockSpec((None, None, D, n), lambda j, i: (*_wgu_block(j + _HALF), 0, 0))
    out = pl.BlockSpec((tm, n), lambda j, i: (i, j))
    outs, extra = _call(
        body, name, (_HALF, T // tm), [pl.BlockSpec((tm, D), lambda j, i: (i, 0)), wg, wu], [out, out, out],
        [jax.ShapeDtypeStruct((T, F), F32), jax.ShapeDtypeStruct((T, F), F32), jax.ShapeDtypeStruct((T, F), BF16)],
        [], ("parallel", "parallel"), (h, wgu, wgu), comm)
    return outs if comm is None else (outs, extra)


def _ffn_bwd_in(dg, du, wgu, name, comm=None):
    T, F = dg.shape
    D, n = wgu.shape[2], wgu.shape[3]
    tm, tn = _pick(T, 1024), _pick(D, 1024)

    def body(dg_ref, du_ref, w_ref, o_ref, acc_ref):
        k = pl.program_id(2)

        @pl.when(k == 0)
        def _():
            acc_ref[...] = jnp.zeros_like(acc_ref)

        @pl.when(k < _HALF)
        def _():
            acc_ref[...] += _dot(dg_ref[...], w_ref[...], "nt")

        @pl.when(k >= _HALF)
        def _():
            acc_ref[...] += _dot(du_ref[...], w_ref[...], "nt")

        @pl.when(k == N_DEV - 1)
        def _():
            o_ref[...] = acc_ref[...]

    dg_spec = pl.BlockSpec((tm, n), lambda i, j, k: (i, jnp.minimum(k, _HALF - 1)))
    du_spec = pl.BlockSpec((tm, n), lambda i, j, k: (i, jnp.maximum(k - _HALF, 0)))
    w_spec = pl.BlockSpec((None, None, tn, n), lambda i, j, k: (*_wgu_block(k), j, 0))
    o_spec = pl.BlockSpec((tm, tn), lambda i, j, k: (i, j))
    (out,), extra = _call(
        body, name, (T // tm, D // tn, N_DEV), [dg_spec, du_spec, w_spec], [o_spec],
        [jax.ShapeDtypeStruct((T, D), F32)], [pltpu.VMEM((tm, tn), F32)], ("parallel", "parallel", "arbitrary"),
        (dg, du, wgu), comm)
    return out if comm is None else (out, extra)


def _ffn_bwd_w(h, dg, du, name, comm=None):
    T, D = h.shape
    n = dg.shape[1] // _HALF
    tm, tk = _pick(D, 1024), _pick(T, 2048)
    nk = T // tk

    def body(h_ref, dg_ref, du_ref, o_ref, acc_ref):
        j, k = pl.program_id(0), pl.program_id(2)

        @pl.when(k == 0)
        def _():
            acc_ref[...] = jnp.zeros_like(acc_ref)

        @pl.when(j < _HALF)
        def _():
            acc_ref[...] += _dot(h_ref[...], dg_ref[...], "tn")

        @pl.when(j >= _HALF)
        def _():
            acc_ref[...] += _dot(h_ref[...], du_ref[...], "tn")

        @pl.when(k == nk - 1)
        def _():
            o_ref[...] = acc_ref[...].astype(BF16)

    h_spec = pl.BlockSpec((tk, tm), lambda j, i, k: (k, i))
    dg_spec = pl.BlockSpec((tk, n), lambda j, i, k: (jnp.where(j < _HALF, k, nk - 1), jnp.minimum(j, _HALF - 1)))
    du_spec = pl.BlockSpec((tk, n), lambda j, i, k: (jnp.where(j >= _HALF, k, 0), jnp.maximum(j - _HALF, 0)))
    o_spec = pl.BlockSpec((None, None, tm, n), lambda j, i, k: (*_wgu_block(j), i, 0))
    (out,), extra = _call(
        body, name, (N_DEV, D // tm, nk), [h_spec, dg_spec, du_spec], [o_spec],
        [jax.ShapeDtypeStruct((_HALF, 2, D, n), BF16)], [pltpu.VMEM((tm, n), F32)],
        ("parallel", "parallel", "arbitrary"), (h, dg, du), comm)
    return out if comm is None else (out, extra)


def _ffn_bwd_act(dx, wo, g, u, name, comm=None):
    T, D = dx.shape
    F = wo.shape[0]
    tm, tn = _pick(T, 1024), _pick(F, 512)

    def body(dx_ref, wo_ref, g_ref, u_ref, dg_ref, du_ref):
        dact = _dot(dx_ref[...].astype(BF16), wo_ref[...], "nt")
        gv = g_ref[...]
        sg = _sigmoid(gv)
        dg_ref[...] = (dact * u_ref[...] * sg * (1.0 + gv * (1.0 - sg))).astype(BF16)
        du_ref[...] = (dact * gv * sg).astype(BF16)

    out = pl.BlockSpec((tm, tn), lambda i, j: (i, j))
    big = jax.ShapeDtypeStruct((T, F), BF16)
    outs, extra = _call(
        body, name, (T // tm, F // tn),
        [pl.BlockSpec((tm, D), lambda i, j: (i, 0)), pl.BlockSpec((tn, D), lambda i, j: (j, 0)), out, out],
        [out, out], [big, big], [], ("parallel", "parallel"), (dx, wo, g, u), comm)
    return outs if comm is None else (outs, extra)


def _place():
    x, y, c = lax.axis_index("x"), lax.axis_index("y"), lax.axis_index("c")
    other_chips = [(1 - x, y), (x, 1 - y), (1 - x, 1 - y)]
    return x, y, c, 2 * x + y, other_chips


_ANY = pl.BlockSpec(memory_space=pl.ANY)
_N_COPIES = 7


def _all_gather(shards, name):
    n = len(shards)

    def body(*refs):
        ins, outs = refs[:n], refs[n:2 * n]
        send_sems, recv_sems, local_sems = refs[2 * n:]
        x, y, c, chip, other_chips = _place()
        sibling = (x, y, 1 - c)

        def remote(src, dst, a, j, dev):
            return pltpu.make_async_remote_copy(src_ref=src, dst_ref=dst, send_sem=send_sems.at[a, j],
                                                recv_sem=recv_sems.at[a, j], device_id=dev, device_id_type=MESH)

        sends, local = [], []
        for a in range(n):
            mine = outs[a].at[chip, c]
            local.append(pltpu.make_async_copy(ins[a], mine, local_sems.at[a]))
            local[a].start()
            for j, (ox, oy) in enumerate(other_chips):
                sends.append(remote(ins[a], mine, a, 1 + j, (ox, oy, c)))
                sends[-1].start()
            sends.append(remote(ins[a], mine, a, 0, sibling))
            sends[-1].start()
        for a in range(n):
            for j, (ox, oy) in enumerate(other_chips):
                slot = outs[a].at[2 * ox + oy, c]
                remote(ins[a], slot, a, 1 + j, (ox, oy, c)).wait_recv()
                sends.append(remote(slot, slot, a, 4 + j, sibling))
                sends[-1].start()
        for a in range(n):
            remote(ins[a], outs[a].at[chip, 1 - c], a, 0, sibling).wait_recv()
            for j, (ox, oy) in enumerate(other_chips):
                remote(ins[a], outs[a].at[2 * ox + oy, 1 - c], a, 4 + j, sibling).wait_recv()
        for cp in sends:
            cp.wait_send()
        for a in range(n):
            local[a].wait()

    return pl.pallas_call(
        body, name=name, in_specs=[_ANY] * n, out_specs=[_ANY] * n,
        out_shape=[jax.ShapeDtypeStruct((4, 2) + s.shape, s.dtype) for s in shards],
        scratch_shapes=[pltpu.SemaphoreType.DMA((n, _N_COPIES)), pltpu.SemaphoreType.DMA((n, _N_COPIES)),
                        pltpu.SemaphoreType.DMA((n,))])(*shards)


def _pair_exchange(parts, name):
    n = len(parts)

    def body(*refs):
        ins, theirs = refs[:n], refs[n:2 * n]
        send_sems, recv_sems = refs[2 * n:]
        x, y, c, _, _ = _place()
        sends = []
        for a in range(n):
            for k in range(4):
                sends.append(pltpu.make_async_remote_copy(
                    src_ref=ins[a].at[k, 1 - c], dst_ref=theirs[a].at[k], send_sem=send_sems.at[a, k],
                    recv_sem=recv_sems.at[a, k], device_id=(x, y, 1 - c), device_id_type=MESH))
                sends[-1].start()
        for cp in sends:
            cp.wait_recv()
            cp.wait_send()

    return pl.pallas_call(
        body, name=name, in_specs=[_ANY] * n, out_specs=[_ANY] * n,
        out_shape=[jax.ShapeDtypeStruct((4,) + p.shape[2:], p.dtype) for p in parts],
        scratch_shapes=[pltpu.SemaphoreType.DMA((n, 4)), pltpu.SemaphoreType.DMA((n, 4))])(*parts)


def _chip_exchange(parts, name):
    n = len(parts)

    def body(*refs):
        ins, got = refs[:n], refs[n:2 * n]
        send_sems, recv_sems = refs[2 * n:]
        _, _, c, _, other_chips = _place()
        sends = []
        for a in range(n):
            for j, (ox, oy) in enumerate(other_chips):
                sends.append(pltpu.make_async_remote_copy(
                    src_ref=ins[a].at[2 * ox + oy], dst_ref=got[a].at[j], send_sem=send_sems.at[a, j],
                    recv_sem=recv_sems.at[a, j], device_id=(ox, oy, c), device_id_type=MESH))
                sends[-1].start()
        for cp in sends:
            cp.wait_recv()
            cp.wait_send()

    return pl.pallas_call(
        body, name=name, in_specs=[_ANY] * n, out_specs=[_ANY] * n,
        out_shape=[jax.ShapeDtypeStruct((3,) + p.shape[1:], p.dtype) for p in parts],
        scratch_shapes=[pltpu.SemaphoreType.DMA((n, 3)), pltpu.SemaphoreType.DMA((n, 3))])(*parts)


def _remote(src, dst, send_sems, recv_sems, a, j, dev):
    return pltpu.make_async_remote_copy(src_ref=src, dst_ref=dst, send_sem=send_sems.at[a, j],
                                        recv_sem=recv_sems.at[a, j], device_id=dev, device_id_type=MESH)


def _gather_send(shards):
    n = len(shards)

    def copies(cin, cout, sems, arriving):
        send_sems, recv_sems, local_sems = sems
        x, y, c, chip, other_chips = _place()
        sibling = (x, y, 1 - c)
        peers = [(0, sibling, (chip, 1 - c))] + [(1 + j, (ox, oy, c), (2 * ox + oy, c))
                                                 for j, (ox, oy) in enumerate(other_chips)]
        sends, recvs, local = [], [], []
        for a in range(n):
            mine = cout[a].at[chip, c]
            local.append(pltpu.make_async_copy(cin[a], mine, local_sems.at[a]))
            for j, dev, slot in peers:
                sends.append(_remote(cin[a], mine, send_sems, recv_sems, a, j, dev))
                if arriving:
                    recvs.append(_remote(cin[a], cout[a].at[slot], send_sems, recv_sems, a, j, dev))
        return sends, recvs, local

    return _Comm(shards, [jax.ShapeDtypeStruct((4, 2) + s.shape, s.dtype) for s in shards],
                 [pltpu.SemaphoreType.DMA((n, 4)), pltpu.SemaphoreType.DMA((n, 4)), pltpu.SemaphoreType.DMA((n,))],
                 copies)


def _gather_pass(gathered):
    n = len(gathered)

    def copies(cin, cout, sems, arriving):
        send_sems, recv_sems = sems
        x, y, c, _, other_chips = _place()
        sibling = (x, y, 1 - c)
        sends, recvs = [], []
        for a in range(n):
            for j, (ox, oy) in enumerate(other_chips):
                k = 2 * ox + oy
                sends.append(_remote(cout[a].at[k, c], cout[a].at[k, c], send_sems, recv_sems, a, j, sibling))
                if arriving:
                    recvs.append(_remote(cout[a].at[k, c], cout[a].at[k, 1 - c], send_sems, recv_sems, a, j, sibling))
        return sends, recvs, []

    return _Comm(gathered, [jax.ShapeDtypeStruct(g.shape, g.dtype) for g in gathered],
                 [pltpu.SemaphoreType.DMA((n, 3)), pltpu.SemaphoreType.DMA((n, 3))], copies,
                 aliases={i: i for i in range(n)})


def _scatter_send(sums):
    n = len(sums)

    def copies(cin, cout, sems, arriving):
        send_sems, recv_sems = sems
        _, _, c, _, other_chips = _place()
        sends = [_remote(cin[a].at[2 * ox + oy], cout[a].at[j], send_sems, recv_sems, a, j, (ox, oy, c))
                 for a in range(n) for j, (ox, oy) in enumerate(other_chips)]
        return sends, sends, []

    return _Comm(sums, [jax.ShapeDtypeStruct((3,) + s.shape[1:], s.dtype) for s in sums],
                 [pltpu.SemaphoreType.DMA((n, 3)), pltpu.SemaphoreType.DMA((n, 3))], copies)


def _join(comms):
    if len(comms) == 1:
        return comms[0]
    args, outs, sems, aliases, spans = [], [], [], {}, []
    for cm in comms:
        spans.append((len(args), len(outs), len(sems)))
        aliases.update({len(args) + i: len(outs) + o for i, o in cm.aliases.items()})
        args, outs, sems = args + cm.args, outs + cm.out_shapes, sems + cm.sems

    def copies(cin, cout, sem_refs, arriving):
        sends, recvs, local = [], [], []
        for cm, (a0, o0, s0) in zip(comms, spans):
            part = cm.copies(cin[a0:a0 + len(cm.args)], cout[o0:o0 + len(cm.out_shapes)],
                             sem_refs[s0:s0 + len(cm.sems)], arriving)
            sends, recvs, local = sends + part[0], recvs + part[1], local + part[2]
        return sends, recvs, local

    return _Comm(args, outs, sems, copies, aliases)


class _Schedule:
    def __init__(self):
        self.rides = {}

    def ride(self, host, make, names, operands):
        results = {}
        self.rides.setdefault(host, []).append((make, names, operands, results))
        return results

    def carry(self, host, fn, *args, **kwargs):
        rides = self.rides.get(host)
        if not rides:
            return fn(*args, host, **kwargs)
        comm = _join([make([operands[n] for n in names]) for make, names, operands, _ in rides])
        out, extra = fn(*args, host, comm=comm, **kwargs)
        for _, names, _, results in rides:
            results.update(zip(names, extra[:len(names)]))
            extra = extra[len(names):]
        return out


def _add_pair(core, parts, theirs, name):
    k, _, R, C = parts.shape
    tr = _pick(R, 512, 16)

    def body(core_ref, a_ref, b_ref, o_ref):
        o_ref[...] = (a_ref[...].astype(F32) + b_ref[...].astype(F32)).astype(BF16)

    blk = pl.BlockSpec((None, tr, C), lambda s, i, core_ref: (s, i, 0))
    grid_spec = pltpu.PrefetchScalarGridSpec(
        num_scalar_prefetch=1, grid=(k, R // tr),
        in_specs=[pl.BlockSpec((None, None, tr, C), lambda s, i, core_ref: (s, core_ref[0], i, 0)), blk],
        out_specs=blk)
    return pl.pallas_call(
        body, name=name, grid_spec=grid_spec, out_shape=jax.ShapeDtypeStruct(theirs.shape, BF16),
        compiler_params=_params("parallel", "parallel"))(core, parts, theirs)


def _adamw_math(w, g, m, v):
    m = ADAM_B1 * m + (1.0 - ADAM_B1) * g
    v = ADAM_B2 * v + (1.0 - ADAM_B2) * (g * g)
    m_hat = m / (1.0 - ADAM_B1 ** ADAM_STEP)
    v_hat = v / (1.0 - ADAM_B2 ** ADAM_STEP)
    delta = -ADAM_LR * (m_hat / (jnp.sqrt(v_hat) + ADAM_EPS) + ADAM_WD * w)
    return delta, m, v


def _scattered_pieces(sums, got):
    return [("own", sums)] + [("peer%d" % j, got) for j in range(3)]


def _piece_spec(kind, tr, C):
    if kind == "own":
        return pl.BlockSpec((None, tr, C), lambda i, chip_ref: (chip_ref[0], i, 0))
    if kind == "plain":
        return pl.BlockSpec((tr, C), lambda i, chip_ref: (i, 0))
    return pl.BlockSpec((None, tr, C), functools.partial(lambda j, i, chip_ref: (j, i, 0), int(kind[-1])))


def _grad_sum(chip, pieces, name):
    R, C = pieces[0][1].shape[-2:]
    tr = _pick(R, 256, 16)

    def body(chip_ref, *refs):
        g = refs[0][...].astype(F32)
        for p in refs[1:-1]:
            g = g + p[...].astype(F32)
        refs[-1][...] = g

    grid_spec = pltpu.PrefetchScalarGridSpec(
        num_scalar_prefetch=1, grid=(R // tr,), in_specs=[_piece_spec(kind, tr, C) for kind, _ in pieces],
        out_specs=_piece_spec("plain", tr, C))
    return pl.pallas_call(
        body, name=name, grid_spec=grid_spec, out_shape=jax.ShapeDtypeStruct((R, C), F32),
        compiler_params=_params("parallel"))(chip, *[a for _, a in pieces])


def _adamw_shard(chip, w, m, v, layer, pieces, so_far, name):
    L, R, C = w.shape
    tr = _pick(R, 256, 16)
    n_p = len(pieces)

    def body(chip_ref, w_ref, m_ref, v_ref, *rest):
        g = rest[0][...].astype(F32)
        for p in rest[1:n_p]:
            g = g + p[...].astype(F32)
        delta, nm, nv = _adamw_math(w_ref[...], g, m_ref[...], v_ref[...])
        g_ref, d_ref, nm_ref, nv_ref = rest[-4:]
        g_ref[...] = g
        d_ref[...] = delta
        nm_ref[...] = nm
        nv_ref[...] = nv

    state = pl.BlockSpec((None, tr, C), lambda i, chip_ref: (layer, i, 0))
    carried = [] if so_far is None else list(so_far)
    first_carried = 1 + 3 + n_p
    grid_spec = pltpu.PrefetchScalarGridSpec(
        num_scalar_prefetch=1, grid=(R // tr,),
        in_specs=[state] * 3 + [_piece_spec(kind, tr, C) for kind, _ in pieces] + [_ANY] * len(carried),
        out_specs=[state] * 4)
    return pl.pallas_call(
        body, name=name, grid_spec=grid_spec, out_shape=[jax.ShapeDtypeStruct((L, R, C), F32)] * 4,
        input_output_aliases={first_carried + t: t for t in range(len(carried))},
        compiler_params=_params("parallel"))(chip, w, m, v, *[a for _, a in pieces], *carried)


def _adamw_small(w, m, v, gathered, name):
    R = w.shape[0]
    tr = _pick(R, 512, SUBLANES)

    def body(w_ref, m_ref, v_ref, gg_ref, g_ref, d_ref, nm_ref, nv_ref):
        g = gg_ref[0, 0]
        for k in range(4):
            for c in range(2):
                if c or k:
                    g = g + gg_ref[k, c]
        delta, nm, nv = _adamw_math(w_ref[...], g, m_ref[...], v_ref[...])
        g_ref[...] = g
        d_ref[...] = delta
        nm_ref[...] = nm
        nv_ref[...] = nv

    row = pl.BlockSpec((tr, LANES), lambda i: (i, 0))
    out = jax.ShapeDtypeStruct((R, LANES), F32)
    return pl.pallas_call(
        body, name=name, grid=(R // tr,),
        in_specs=[row, row, row, pl.BlockSpec((4, 2, tr, LANES), lambda i: (0, 0, i, 0))], out_specs=[row] * 4,
        out_shape=[out] * 4, compiler_params=_params("parallel"))(w, m, v, gathered)


def _to_heads(a, n_heads):
    return a.reshape(a.shape[0], n_heads, HEAD_DIM).transpose(1, 0, 2)


def _from_heads(a):
    return a.transpose(1, 0, 2).reshape(a.shape[1], a.shape[0] * HEAD_DIM)


def _layer_fwd(x, w, s, tag, sched):
    h = _rms_fwd(x, s["norm_mix_g"], f"rms_mix_{tag}")
    z = sched.carry(f"in_proj_{tag}", _mm, h, w["w_in_t"], "nt", F32, tn=512)
    q = _to_heads(z[:, :OFF_K], N_Q_HEADS)
    k = _to_heads(z[:, OFF_K:OFF_V], N_KV_HEADS)
    v = _to_heads(z[:, OFF_V:OFF_U], N_KV_HEADS)
    ya = _from_heads(sched.carry(f"attn_fwd_{tag}", _attn_fwd, q, k, v, s["q_norm_g"], s["k_norm_g"],
                                 s["attn_sinks"]))
    y, s_r, s_i = sched.carry(f"ssm_fwd_{tag}", _ssm_fwd, z, *s["ssm16"], s["ssm_d"])
    pre, y3 = _glu_fwd(y, w["ssm_glu_w"], s["ssm_glu_b"], f"glu_fwd_{tag}")
    a, b, merged = sched.carry(f"merge_fwd_{tag}", _merge_fwd, ya, y3, w["w_attn_branch"], w["w_ssm_branch"], z,
                               s["gate_bias"])
    x1 = _mm(merged, w["w_out"], "nn", F32, f"out_proj_{tag}", residual=x)
    h2 = _rms_fwd(x1, s["norm_ffn_g"], f"rms_ffn_{tag}")
    g, u, act = sched.carry(f"ffn_fwd_{tag}", _ffn_fwd, h2, w["w_ffn_in"])
    x2 = sched.carry(f"ffn_out_{tag}", _mm, act, w["w_ffn_out"], "nn", F32, residual=x1)
    saved = dict(x=x, h=h, z=z, q=q, k=k, v=v, ya=ya, y=y, s_r=s_r, s_i=s_i, pre=pre, y3=y3, a=a, b=b, merged=merged,
                 x1=x1, h2=h2, g=g, u=u, act=act)
    return x2, saved


def _layer_bwd(dx2, sv, w, s, tag, sched, scatter):
    gw, gs = {}, {}
    dg16, du16 = _ffn_bwd_act(dx2, w["w_ffn_out"], sv["g"], sv["u"], f"ffn_bwd_act_{tag}")
    gw["w_ffn_out"] = _mm(sv["act"], dx2, "tn", BF16, f"dw_ffn_out_{tag}", tm=1408)
    dh2 = _ffn_bwd_in(dg16, du16, w["w_ffn_in"], f"dh2_{tag}")
    gw["w_ffn_in"] = _ffn_bwd_w(sv["h2"], dg16, du16, f"dw_ffn_in_{tag}")
    scatter(FFN_WEIGHTS, gw, f"attn_bwd_{tag}")
    dx1, gs["norm_ffn_g"] = _rms_bwd(dh2, sv["x1"], s["norm_ffn_g"], dx2, f"rms_ffn_bwd_{tag}")
    dm = _mm(dx1, w["w_out"], "nt", F32, f"dmerged_{tag}")
    gw["w_out"] = _mm(sv["merged"], dx1, "tn", BF16, f"dw_out_{tag}")
    da16, db16, dza, dzs, dba, dbs = _merge_bwd(dm, sv["a"], sv["b"], sv["z"], s["gate_bias"], f"merge_bwd_{tag}")
    gs["gate_bias"] = jnp.concatenate([dba, dbs], axis=1)
    dya = _mm(da16, w["w_attn_branch"], "nt", F32, f"dya_{tag}")
    gw["w_attn_branch"] = _mm(sv["ya"], da16, "tn", BF16, f"dw_attn_branch_{tag}")
    dy3 = _mm(db16, w["w_ssm_branch"], "nt", F32, f"dy3_{tag}")
    gw["w_ssm_branch"] = _mm(sv["y3"], db16, "tn", BF16, f"dw_ssm_branch_{tag}")
    dpre16, t1, y2_16, gs["ssm_glu_b"] = _glu_bwd_gate(dy3, sv["y"], sv["pre"], f"glu_bwd_gate_{tag}")
    dy = _glu_bwd_in(dpre16, w["ssm_glu_w"], t1, sv["y"], f"glu_bwd_in_{tag}")
    gw["ssm_glu_w"] = _mm(y2_16, dpre16, "tn", BF16, f"dw_glu_{tag}")
    scatter(MIXER_WEIGHTS[1:], gw, f"ssm_bwd_{tag}")
    du_ssm, *gs["ssm_disc"], gs["ssm_d"] = sched.carry(f"ssm_bwd_{tag}", _ssm_bwd, dy, sv["z"], sv["s_r"], sv["s_i"],
                                                       *s["ssm16"], s["ssm_d"])
    dq, dk, dv, dqg, dkg, dsk = sched.carry(f"attn_bwd_{tag}", _attn_bwd, sv["q"], sv["k"], sv["v"],
                                            _to_heads(dya, N_Q_HEADS), s["q_norm_g"], s["k_norm_g"], s["attn_sinks"])
    gs["q_norm_g"] = jnp.sum(dqg, axis=0)
    gs["k_norm_g"] = jnp.sum(dkg, axis=0)
    gs["attn_sinks"] = dsk[:, :, 0].reshape(1, N_Q_HEADS)
    dz = jnp.concatenate([_from_heads(dq), _from_heads(dk).astype(BF16), _from_heads(dv).astype(BF16), du_ssm, dza,
                          dzs], axis=1)
    gw["w_in"] = _mm(dz, sv["h"], "tn", BF16, f"dw_in_t_{tag}", tm=1664)
    scatter(MIXER_WEIGHTS[:1], gw, f"dh_{tag}")
    dh = sched.carry(f"dh_{tag}", _mm, dz, w["w_in_t"], "nn", F32)
    dx, gs["norm_mix_g"] = _rms_bwd(dh, sv["x"], s["norm_mix_g"], dx1, f"rms_mix_bwd_{tag}")
    return dx, gs


def _shard_to_send(name, shard):
    return (shard.T if name == "w_in" else shard).astype(BF16)


def _assemble(name, gathered):
    if name == "w_ffn_in":
        return gathered
    if name in COL_SHARDED and name != "w_in":
        rows = gathered.shape[2]
        return gathered.transpose(2, 0, 1, 3).reshape(rows, -1)
    return gathered.reshape(-1, gathered.shape[3])


class _Weights:
    def __init__(self):
        self.sources, self.ready = [], {}

    def __getitem__(self, name):
        if name not in self.ready:
            key = "w_in" if name == "w_in_t" else name
            (gathered,) = [src[key] for src in self.sources if key in src]
            self.ready[name] = _assemble(key, gathered)
        return self.ready[name]


def _disassemble(name, grad):
    if name == "w_ffn_in":
        return grad
    if name in COL_SHARDED and name != "w_in":
        rows, cols = grad.shape
        return grad.reshape(rows, 4, 2, cols // N_DEV).transpose(1, 2, 0, 3)
    rows, cols = grad.shape
    return grad.reshape(4, 2, rows // N_DEV, cols)


def _pack(arrays):
    flat = jnp.concatenate([a.reshape(-1) for a in arrays])
    pad = (-flat.shape[0]) % (SUBLANES * LANES)
    return jnp.pad(flat, (0, pad)).reshape(-1, LANES)


def _unpack(packed, like):
    flat, out, off = packed.reshape(-1), [], 0
    for a in like:
        out.append(flat[off:off + a.size].reshape(a.shape))
        off += a.size
    return out


def kernel(x, norm_mix_g, w_in, gate_bias, q_norm_g, k_norm_g, attn_sinks, ssm_lambda_re, ssm_lambda_im, ssm_log_dt, ssm_b_re, ssm_b_im, ssm_c_re, ssm_c_im, ssm_d, ssm_glu_w, ssm_glu_b, w_attn_branch, w_ssm_branch, w_out, norm_ffn_g, w_ffn_in, w_ffn_out, loss_target, m_norm_mix_g, m_w_in, m_gate_bias, m_q_norm_g, m_k_norm_g, m_attn_sinks, m_ssm_lambda_re, m_ssm_lambda_im, m_ssm_log_dt, m_ssm_b_re, m_ssm_b_im, m_ssm_c_re, m_ssm_c_im, m_ssm_d, m_ssm_glu_w, m_ssm_glu_b, m_w_attn_branch, m_w_ssm_branch, m_w_out, m_norm_ffn_g, m_w_ffn_in, m_w_ffn_out, v_norm_mix_g, v_w_in, v_gate_bias, v_q_norm_g, v_k_norm_g, v_attn_sinks, v_ssm_lambda_re, v_ssm_lambda_im, v_ssm_log_dt, v_ssm_b_re, v_ssm_b_im, v_ssm_c_re, v_ssm_c_im, v_ssm_d, v_ssm_glu_w, v_ssm_glu_b, v_w_attn_branch, v_w_ssm_branch, v_w_out, v_norm_ffn_g, v_w_ffn_in, v_w_ffn_out):
    given = dict(locals())
    wts = {n: given[n] for n in WEIGHTS}
    mom = {n: given["m_" + n] for n in WEIGHTS}
    var = {n: given["v_" + n] for n in WEIGHTS}
    depth = w_in.shape[0]
    xs = x[0]
    target = loss_target[0]

    sched = _Schedule()
    shards = [{n: _shard_to_send(n, wts[n][l]) for n in BIG} for l in range(depth)]
    full = [_Weights() for _ in range(depth)]

    def gather(layer, names, send_host, pass_host):
        sent = sched.ride(send_host, _gather_send, names, shards[layer])
        full[layer].sources.append(sched.ride(pass_host, _gather_pass, names, sent))

    first = MIXER_WEIGHTS[:1]
    full[0].sources.append(dict(zip(first, _all_gather([shards[0][n] for n in first], "gather_w_in_0"))))
    gather(0, ("w_ffn_in",), "attn_fwd_0", "ssm_fwd_0")
    gather(0, ("w_ffn_out",), "ssm_fwd_0", "merge_fwd_0")
    for l in range(depth):
        gather(l, MIXER_WEIGHTS[1:], f"in_proj_{l}", f"attn_fwd_{l}")
        if l + 1 < depth:
            gather(l + 1, first, f"merge_fwd_{l}", f"ffn_fwd_{l}")
            gather(l + 1, ("w_ffn_in",), f"ffn_fwd_{l}", f"ffn_out_{l}")
            gather(l + 1, ("w_ffn_out",), f"ffn_out_{l}", f"in_proj_{l + 1}")

    small, disc_vjp = [], []
    for l in range(depth):
        s = {n: wts[n][l].reshape(1, -1) for n in ("norm_mix_g", "gate_bias", "q_norm_g", "k_norm_g", "attn_sinks",
                                                   "ssm_d", "ssm_glu_b", "norm_ffn_g")}
        disc, vjp = jax.vjp(_ssm_discretize, *[wts[n][l] for n in ("ssm_lambda_re", "ssm_lambda_im", "ssm_log_dt",
                                                                  "ssm_b_re", "ssm_b_im", "ssm_c_re", "ssm_c_im")])
        s["ssm16"] = (disc[0], disc[1]) + tuple(d.astype(BF16) for d in disc[2:])
        small.append(s)
        disc_vjp.append(vjp)

    act, saved = xs, []
    for l in range(depth):
        act, sv = _layer_fwd(act, full[l], small[l], str(l), sched)
        saved.append(sv)
    dact, loss_local = _loss_grad(act, target, "loss_head")

    out = {"grad": {}, "delta": {}, "new_m": {}, "new_v": {}}
    results = {n: None for n in BIG}
    core = lax.axis_index("c").astype(jnp.int32).reshape(1)
    chip = (2 * lax.axis_index("x") + lax.axis_index("y")).astype(jnp.int32).reshape(1)
    small_grads = [None] * depth
    for l in reversed(range(depth)):
        scattered = []

        def scatter(names, grads, host):
            parts = [_disassemble(n, grads[n]) for n in names]
            theirs = _pair_exchange(parts, f"grad_pair_exchange_{host}")
            sums = {n: _add_pair(core, p, t, f"grad_pair_sum_{n}_{host}") for n, p, t in zip(names, parts, theirs)}
            scattered.append((sums, sched.ride(host, _scatter_send, names, sums)))

        dact, gs = _layer_bwd(dact, saved[l], full[l], small[l], str(l), sched, scatter)
        (gs["ssm_lambda_re"], gs["ssm_lambda_im"], gs["ssm_log_dt"], gs["ssm_b_re"], gs["ssm_b_im"], gs["ssm_c_re"],
         gs["ssm_c_im"]) = disc_vjp[l](tuple(gs.pop("ssm_disc")))
        small_grads[l] = gs
        for sums, got in scattered:
            for n in got:
                pieces = _scattered_pieces(sums[n], got[n])
                if n == "w_in":
                    pieces = [("plain", _grad_sum(chip, pieces, f"grad_sum_w_in_{l}").T)]
                results[n] = _adamw_shard(chip, wts[n], mom[n], var[n], l, pieces, results[n], f"adamw_{n}_{l}")
    loss = lax.psum(loss_local, ("x", "y", "c"))
    for n in BIG:
        for kind, res in zip(("grad", "delta", "new_m", "new_v"), results[n]):
            out[kind][n] = res

    like = [wts[n] for n in SMALL]
    g_small = _pack([jnp.stack([small_grads[l][n].reshape(wts[n].shape[1:]) for l in range(depth)]) for n in SMALL])
    (gathered_small,) = _all_gather([g_small], "gather_small_grads")
    res = _adamw_small(_pack(like), _pack([mom[n] for n in SMALL]), _pack([var[n] for n in SMALL]), gathered_small,
                       "adamw_small")
    for kind, packed in zip(("grad", "delta", "new_m", "new_v"), res):
        for n, a in zip(SMALL, _unpack(packed, like)):
            out[kind][n] = a

    grad_x = dact.reshape(x.shape)
    return (loss, grad_x, *[out["grad"][n] for n in WEIGHTS], *[out["delta"][n] for n in WEIGHTS],
            *[out["new_m"][n] for n in WEIGHTS], *[out["new_v"][n] for n in WEIGHTS])
```

```python
import functools
import math

import jax
import jax.numpy as jnp
from jax import lax
from jax.experimental import pallas as pl
from jax.experimental.pallas import tpu as pltpu

F32, BF16 = jnp.float32, jnp.bfloat16
MESH = pl.DeviceIdType.MESH

D_MODEL = 2048
HEAD_DIM = 64
N_Q_HEADS = 16
N_KV_HEADS = 4
GQA_GROUP = N_Q_HEADS // N_KV_HEADS
ATTN_WIDTH = N_Q_HEADS * HEAD_DIM
KV_WIDTH = N_KV_HEADS * HEAD_DIM
WINDOW = 128
BLOCK = 128
SSM_WIDTH = D_MODEL // 2
SSM_GROUP_CH = 16
SSM_GROUPS = SSM_WIDTH // SSM_GROUP_CH
SSM_STATE = 64
D_FF = 5632
OFF_K = ATTN_WIDTH
OFF_V = OFF_K + KV_WIDTH
OFF_U = OFF_V + KV_WIDTH
OFF_G = OFF_U + SSM_WIDTH
IN_WIDTH = OFF_G + 2 * D_MODEL
RMS_EPS = 1e-6
ATTN_SCALE = HEAD_DIM ** -0.5
NEG_BIG = -1e30

SSM_NGB = 4
SSM_GB_CH = SSM_WIDTH // SSM_NGB
SSM_GB_ST = SSM_GROUPS * SSM_STATE // SSM_NGB
SUBLANES = 8
LANES = 128
SSM_TT = 256

ADAM_LR = 0.001
ADAM_B1 = 0.9
ADAM_B2 = 0.999
ADAM_EPS = 1e-08
ADAM_WD = 0.01
ADAM_STEP = 10

N_DEV = 8
VMEM_LIMIT_BYTES = 52 * 1024 * 1024

BIG = ("w_in", "ssm_glu_w", "w_attn_branch", "w_ssm_branch", "w_out", "w_ffn_in", "w_ffn_out")
COL_SHARDED = ("w_in", "w_attn_branch", "w_ssm_branch", "w_ffn_in")
FFN_WEIGHTS = ("w_ffn_in", "w_ffn_out")
MIXER_WEIGHTS = ("w_in", "ssm_glu_w", "w_attn_branch", "w_ssm_branch", "w_out")
SMALL = ("norm_mix_g", "gate_bias", "q_norm_g", "k_norm_g", "attn_sinks", "ssm_lambda_re", "ssm_lambda_im",
         "ssm_log_dt", "ssm_b_re", "ssm_b_im", "ssm_c_re", "ssm_c_im", "ssm_d", "ssm_glu_b", "norm_ffn_g")
WEIGHTS = ("norm_mix_g", "w_in", "gate_bias", "q_norm_g", "k_norm_g", "attn_sinks", "ssm_lambda_re", "ssm_lambda_im",
           "ssm_log_dt", "ssm_b_re", "ssm_b_im", "ssm_c_re", "ssm_c_im", "ssm_d", "ssm_glu_w", "ssm_glu_b",
           "w_attn_branch", "w_ssm_branch", "w_out", "norm_ffn_g", "w_ffn_in", "w_ffn_out")


def _pick(n, target, mult=LANES):
    best = None
    for t in range(mult, min(n, target) + 1, mult):
        if n % t == 0:
            best = t
    return n if best is None else best


def _params(*sem):
    return pltpu.CompilerParams(dimension_semantics=sem, vmem_limit_bytes=VMEM_LIMIT_BYTES)


def _sigmoid(v):
    return 1.0 / (1.0 + jnp.exp(-v))


_GELU_C = math.sqrt(2.0 / math.pi)


def _gelu(v):
    return 0.5 * v * (1.0 + jnp.tanh(_GELU_C * (v + 0.044715 * v * v * v)))


def _gelu_grad(v):
    t = jnp.tanh(_GELU_C * (v + 0.044715 * v * v * v))
    return 0.5 * (1.0 + t) + 0.5 * v * (1.0 - t * t) * _GELU_C * (1.0 + 3.0 * 0.044715 * v * v)


_DN = {"nn": (((1,), (0,)), ((), ())), "nt": (((1,), (1,)), ((), ())), "tn": (((0,), (0,)), ((), ()))}


def _dot(a, b, dims="nn"):
    return lax.dot_general(a, b, _DN[dims], preferred_element_type=F32)


class _Comm:
    def __init__(self, args, out_shapes, sems, copies, aliases=None):
        self.args, self.out_shapes, self.sems = list(args), list(out_shapes), list(sems)
        self.copies, self.aliases = copies, dict(aliases or {})


def _call(body, name, grid, in_specs, out_specs, out_shape, scratch_shapes, semantics, args, comm=None):
    in_specs, out_specs, out_shape = list(in_specs), list(out_specs), list(out_shape)
    scratch_shapes = list(scratch_shapes)
    if comm is None:
        res = pl.pallas_call(body, name=name, grid=grid, in_specs=in_specs, out_specs=out_specs, out_shape=out_shape,
                             scratch_shapes=scratch_shapes, compiler_params=_params(*semantics))(*args)
        return list(res), []
    n_in, n_out, n_scr = len(in_specs), len(out_specs), len(scratch_shapes)
    n_cin, n_cout = len(comm.args), len(comm.out_shapes)

    def carrying(*refs):
        ins, cin = refs[:n_in], refs[n_in:n_in + n_cin]
        o0 = n_in + n_cin
        outs, cout = refs[o0:o0 + n_out], refs[o0 + n_out:o0 + n_out + n_cout]
        s0 = o0 + n_out + n_cout
        scr, sems = refs[s0:s0 + n_scr], refs[s0 + n_scr:]
        first = functools.reduce(jnp.logical_and, [pl.program_id(d) == 0 for d in range(len(grid))])
        last = functools.reduce(jnp.logical_and, [pl.program_id(d) == grid[d] - 1 for d in range(len(grid))])

        @pl.when(first)
        def _():
            sends, _, local = comm.copies(cin, cout, sems, False)
            for cp in local + sends:
                cp.start()

        body(*ins, *outs, *scr)

        @pl.when(last)
        def _():
            sends, recvs, local = comm.copies(cin, cout, sems, True)
            for cp in recvs:
                cp.wait_recv()
            for cp in sends:
                cp.wait_send()
            for cp in local:
                cp.wait()

    res = pl.pallas_call(
        carrying, name=name, grid=grid, in_specs=in_specs + [_ANY] * n_cin, out_specs=out_specs + [_ANY] * n_cout,
        out_shape=out_shape + comm.out_shapes, scratch_shapes=scratch_shapes + comm.sems,
        input_output_aliases={n_in + i: n_out + o for i, o in comm.aliases.items()},
        compiler_params=_params(*["arbitrary"] * len(grid)))(*args, *comm.args)
    return list(res[:n_out]), list(res[n_out:])


def _mm(a, b, dims, out_dtype, name, residual=None, tm=1024, tn=1024, tk=2048, comm=None):
    if dims == "tn":
        K, M = a.shape
    else:
        M, K = a.shape
    N = b.shape[0] if dims == "nt" else b.shape[1]
    tm, tn, tk = _pick(M, tm), _pick(N, tn), _pick(K, tk)
    nk = K // tk
    has_res = residual is not None

    def finish(out, refs):
        if has_res:
            out = out + refs[2][...].astype(F32)
        refs[-2][...] = out.astype(out_dtype)

    def body_single(*refs):
        finish(_dot(refs[0][...].astype(BF16), refs[1][...].astype(BF16), dims), refs)

    def body_multi(*refs):
        acc_ref = refs[-1]
        k = pl.program_id(2)

        @pl.when(k == 0)
        def _():
            acc_ref[...] = jnp.zeros_like(acc_ref)

        acc_ref[...] += _dot(refs[0][...].astype(BF16), refs[1][...].astype(BF16), dims)

        @pl.when(k == nk - 1)
        def _():
            finish(acc_ref[...], refs)

    a_spec = (pl.BlockSpec((tk, tm), lambda i, j, k: (k, i)) if dims == "tn"
              else pl.BlockSpec((tm, tk), lambda i, j, k: (i, k)))
    b_spec = (pl.BlockSpec((tn, tk), lambda i, j, k: (j, k)) if dims == "nt"
              else pl.BlockSpec((tk, tn), lambda i, j, k: (k, j)))
    o_spec = pl.BlockSpec((tm, tn), lambda i, j, k: (i, j))
    in_specs = [a_spec, b_spec] + ([o_spec] if has_res else [])
    args = (a, b) + ((residual,) if has_res else ())
    (out,), extra = _call(
        body_single if nk == 1 else body_multi, name, (M // tm, N // tn, nk), in_specs, [o_spec],
        [jax.ShapeDtypeStruct((M, N), out_dtype)], [pltpu.VMEM((tm, tn) if nk > 1 else (SUBLANES, LANES), F32)],
        ("parallel", "parallel", "arbitrary"), args, comm)
    return out if comm is None else (out, extra)


def _rms_fwd(x, g, name):
    T, D = x.shape
    tr = _pick(T, 256, SUBLANES)

    def body(x_ref, g_ref, o_ref):
        xf = x_ref[...]
        r = lax.rsqrt(jnp.mean(xf * xf, axis=-1, keepdims=True) + RMS_EPS)
        o_ref[...] = (xf * r * g_ref[...]).astype(BF16)

    return pl.pallas_call(
        body, name=name, grid=(T // tr,),
        in_specs=[pl.BlockSpec((tr, D), lambda i: (i, 0)), pl.BlockSpec((1, D), lambda i: (0, 0))],
        out_specs=pl.BlockSpec((tr, D), lambda i: (i, 0)), out_shape=jax.ShapeDtypeStruct((T, D), BF16),
        compiler_params=_params("parallel"))(x, g)


def _rms_bwd(dh, x, g, dres, name):
    T, D = x.shape
    tr = _pick(T, 256, SUBLANES)

    def body(dh_ref, x_ref, g_ref, dres_ref, dx_ref, dg_ref):
        @pl.when(pl.program_id(0) == 0)
        def _():
            dg_ref[...] = jnp.zeros_like(dg_ref)

        xf = x_ref[...]
        r = lax.rsqrt(jnp.mean(xf * xf, axis=-1, keepdims=True) + RMS_EPS)
        xhat = xf * r
        dhv = dh_ref[...]
        dxh = dhv * g_ref[...]
        dx_ref[...] = dres_ref[...] + r * (dxh - xhat * jnp.mean(dxh * xhat, axis=-1, keepdims=True))
        dg_ref[...] += jnp.sum(dhv * xhat, axis=0, keepdims=True)

    row = pl.BlockSpec((tr, D), lambda i: (i, 0))
    vec = pl.BlockSpec((1, D), lambda i: (0, 0))
    return pl.pallas_call(
        body, name=name, grid=(T // tr,), in_specs=[row, row, vec, row], out_specs=[row, vec],
        out_shape=[jax.ShapeDtypeStruct((T, D), F32), jax.ShapeDtypeStruct((1, D), F32)],
        compiler_params=_params("arbitrary"))(dh, x, g, dres)


def _loss_grad(y, target, name):
    T, D = y.shape
    tr = _pick(T, 256, SUBLANES)

    def body(y_ref, t_ref, dx_ref, loss_ref):
        @pl.when(pl.program_id(0) == 0)
        def _():
            loss_ref[...] = jnp.zeros_like(loss_ref)

        err = y_ref[...] - t_ref[...]
        dx_ref[...] = err * (1.0 / D)
        loss_ref[...] += jnp.sum(jnp.mean(err * err, axis=-1, keepdims=True), axis=0, keepdims=True) * 0.5

    row = pl.BlockSpec((tr, D), lambda i: (i, 0))
    one = pl.BlockSpec((1, 1), lambda i: (0, 0))
    dx, loss = pl.pallas_call(
        body, name=name, grid=(T // tr,), in_specs=[row, row], out_specs=[row, one],
        out_shape=[jax.ShapeDtypeStruct((T, D), F32), jax.ShapeDtypeStruct((1, 1), F32)],
        compiler_params=_params("arbitrary"))(y, target)
    return dx, loss[0, 0]


_STACK = GQA_GROUP * BLOCK


def _attn_mask(n):
    row = lax.broadcasted_iota(jnp.int32, (_STACK, 2 * BLOCK), 0)
    col = lax.broadcasted_iota(jnp.int32, (_STACK, 2 * BLOCK), 1)
    dist = (row % BLOCK) - col + BLOCK
    valid = (dist >= 0) & (dist < WINDOW) & ((col >= BLOCK) | (n > 0))
    return dist.astype(F32), valid


def _per_row_head(kv, sk_ref):
    g = lax.broadcasted_iota(jnp.int32, (_STACK, 1), 0) // BLOCK
    head = (kv * GQA_GROUP + g + 1).astype(F32)
    slope = jnp.exp(head * (-8.0 / N_Q_HEADS * math.log(2.0)))
    sink = jnp.concatenate([jnp.broadcast_to(sk_ref[j:j + 1, 0:1], (BLOCK, 1)) for j in range(GQA_GROUP)], axis=0)
    return slope, sink


def _head_norm(v, gain):
    r = lax.rsqrt(jnp.mean(v * v, axis=-1, keepdims=True) + RMS_EPS)
    vhat = v * r
    return r, vhat, vhat * gain


def _attn_probs(qn16, kn16, slope, dist, valid, sink):
    s = _dot(qn16, kn16, "nt") * ATTN_SCALE - slope * dist
    s = jnp.where(valid, s, NEG_BIG)
    m = jnp.maximum(jnp.max(s, axis=-1, keepdims=True), sink)
    p = jnp.exp(s - m)
    ps = jnp.exp(sink - m)
    den = jnp.sum(p, axis=-1, keepdims=True) + ps
    return p, ps, den


def _attn_specs(T):
    q_spec = pl.BlockSpec((GQA_GROUP, BLOCK, HEAD_DIM), lambda h, n: (h, n, 0))
    cur = pl.BlockSpec((None, BLOCK, HEAD_DIM), lambda h, n: (h, n, 0))
    prev = pl.BlockSpec((None, BLOCK, HEAD_DIM), lambda h, n: (h, jnp.maximum(n - 1, 0), 0))
    gain = pl.BlockSpec((1, HEAD_DIM), lambda h, n: (0, 0))
    sink = pl.BlockSpec((None, GQA_GROUP, LANES), lambda h, n: (h, 0, 0))
    return q_spec, cur, prev, gain, sink


def _sink_rows(sinks):
    return jnp.broadcast_to(sinks.reshape(N_KV_HEADS, GQA_GROUP, 1), (N_KV_HEADS, GQA_GROUP, LANES))


def _attn_fwd(q, k, v, qg, kg, sinks, name, comm=None):
    T = q.shape[1]
    nb = T // BLOCK

    def body(q_ref, kc_ref, kp_ref, vc_ref, vp_ref, qg_ref, kg_ref, sk_ref, o_ref):
        dist, valid = _attn_mask(pl.program_id(1))
        slope, sink = _per_row_head(pl.program_id(0), sk_ref)
        kk = jnp.concatenate([kp_ref[...], kc_ref[...]], axis=0)
        _, _, kn = _head_norm(kk, kg_ref[...])
        v16 = jnp.concatenate([vp_ref[...], vc_ref[...]], axis=0).astype(BF16)
        _, _, qn = _head_norm(q_ref[...].reshape(_STACK, HEAD_DIM), qg_ref[...])
        p, _, den = _attn_probs(qn.astype(BF16), kn.astype(BF16), slope, dist, valid, sink)
        o_ref[...] = _dot((p / den).astype(BF16), v16).reshape(GQA_GROUP, BLOCK, HEAD_DIM).astype(BF16)

    q_spec, cur, prev, gain, sink = _attn_specs(T)
    (out,), extra = _call(
        body, name, (N_KV_HEADS, nb), [q_spec, cur, prev, cur, prev, gain, gain, sink], [q_spec],
        [jax.ShapeDtypeStruct((N_Q_HEADS, T, HEAD_DIM), BF16)], [], ("parallel", "parallel"),
        (q, k, k, v, v, qg, kg, _sink_rows(sinks)), comm)
    return out if comm is None else (out, extra)


def _attn_bwd(q, k, v, do, qg, kg, sinks, name, comm=None):
    T = q.shape[1]
    nb = T // BLOCK

    def body(q_ref, kc_ref, kp_ref, vc_ref, vp_ref, do_ref, qg_ref, kg_ref, sk_ref,
             dq_ref, dk_ref, dv_ref, dqg_ref, dkg_ref, dsk_ref):
        n = pl.program_id(1)

        @pl.when(n == 0)
        def _():
            dk_ref[...] = jnp.zeros_like(dk_ref)
            dv_ref[...] = jnp.zeros_like(dv_ref)
            dqg_ref[...] = jnp.zeros_like(dqg_ref)
            dkg_ref[...] = jnp.zeros_like(dkg_ref)
            dsk_ref[...] = jnp.zeros_like(dsk_ref)

        dist, valid = _attn_mask(n)
        slope, sink = _per_row_head(pl.program_id(0), sk_ref)
        kk = jnp.concatenate([kp_ref[...], kc_ref[...]], axis=0)
        rk, khat, kn = _head_norm(kk, kg_ref[...])
        kn16 = kn.astype(BF16)
        v16 = jnp.concatenate([vp_ref[...], vc_ref[...]], axis=0).astype(BF16)
        rq, qhat, qn = _head_norm(q_ref[...].reshape(_STACK, HEAD_DIM), qg_ref[...])
        qn16 = qn.astype(BF16)
        p, ps, den = _attn_probs(qn16, kn16, slope, dist, valid, sink)
        pn = p / den
        do16 = do_ref[...].reshape(_STACK, HEAD_DIM).astype(BF16)
        dp = _dot(do16, v16, "nt")
        delta = jnp.sum(pn * dp, axis=-1, keepdims=True)
        ds16 = (pn * (dp - delta)).astype(BF16)
        sink_pull = ps / den * delta
        for g in range(GQA_GROUP):
            dsink = jnp.sum(sink_pull[g * BLOCK:(g + 1) * BLOCK], axis=0, keepdims=True)
            dsk_ref[g:g + 1, :] -= jnp.broadcast_to(dsink, (1, LANES))
        dqn = _dot(ds16, kn16) * ATTN_SCALE
        dkn = _dot(ds16, qn16, "tn") * ATTN_SCALE
        dv = _dot(pn.astype(BF16), do16, "tn")
        dqh = dqn * qg_ref[...]
        dq = rq * (dqh - qhat * jnp.mean(dqh * qhat, axis=-1, keepdims=True))
        dq_ref[...] = dq.reshape(GQA_GROUP, BLOCK, HEAD_DIM).astype(BF16)
        dqg_ref[...] += jnp.sum(dqn * qhat, axis=0, keepdims=True)
        dkg_ref[...] += jnp.sum(dkn * khat, axis=0, keepdims=True)
        dkh = dkn * kg_ref[...]
        dk = rk * (dkh - khat * jnp.mean(dkh * khat, axis=-1, keepdims=True))
        rows = pl.ds(pl.multiple_of(n * BLOCK, BLOCK), BLOCK)
        dk_ref[rows, :] += dk[BLOCK:]
        dv_ref[rows, :] += dv[BLOCK:]

        @pl.when(n > 0)
        def _():
            before = pl.ds(pl.multiple_of((n - 1) * BLOCK, BLOCK), BLOCK)
            dk_ref[before, :] += dk[:BLOCK]
            dv_ref[before, :] += dv[:BLOCK]

    q_spec, cur, prev, gain, sink = _attn_specs(T)
    whole = pl.BlockSpec((None, T, HEAD_DIM), lambda h, n: (h, 0, 0))
    gacc = pl.BlockSpec((None, 1, HEAD_DIM), lambda h, n: (h, 0, 0))
    outs, extra = _call(
        body, name, (N_KV_HEADS, nb), [q_spec, cur, prev, cur, prev, q_spec, gain, gain, sink],
        [q_spec, whole, whole, gacc, gacc, sink],
        [jax.ShapeDtypeStruct((N_Q_HEADS, T, HEAD_DIM), BF16),
         jax.ShapeDtypeStruct((N_KV_HEADS, T, HEAD_DIM), F32),
         jax.ShapeDtypeStruct((N_KV_HEADS, T, HEAD_DIM), F32),
         jax.ShapeDtypeStruct((N_KV_HEADS, 1, HEAD_DIM), F32),
         jax.ShapeDtypeStruct((N_KV_HEADS, 1, HEAD_DIM), F32),
         jax.ShapeDtypeStruct((N_KV_HEADS, GQA_GROUP, LANES), F32)],
        [], ("parallel", "arbitrary"), (q, k, k, v, v, do, qg, kg, _sink_rows(sinks)), comm)
    return outs if comm is None else (outs, extra)


def _ssm_discretize(lam_re, lam_im, log_dt, b_re, b_im, c_re, c_im):
    dt = jnp.exp(log_dt)[:, None]
    mag = jnp.exp(lam_re * dt)
    ar = mag * jnp.cos(lam_im * dt)
    ai = mag * jnp.sin(lam_im * dt)
    den = lam_re * lam_re + lam_im * lam_im
    fr = ((ar - 1.0) * lam_re + ai * lam_im) / den
    fi = (ai * lam_re - (ar - 1.0) * lam_im) / den
    bbar_r = fr[:, :, None] * b_re - fi[:, :, None] * b_im
    bbar_i = fr[:, :, None] * b_im + fi[:, :, None] * b_re
    gl = SSM_GROUPS // SSM_NGB
    eye = jnp.eye(gl, dtype=F32)

    def tiles(a):
        return a.reshape(SSM_NGB, SUBLANES, LANES)

    def bdiag(bb):
        return jnp.einsum("bgph,gk->bghkp", bb.reshape(SSM_NGB, gl, SSM_STATE, SSM_GROUP_CH), eye).reshape(
            SSM_NGB, SSM_GB_CH, SSM_GB_ST)

    def cdiag(cc):
        return jnp.einsum("bghp,gk->bgpkh", cc.reshape(SSM_NGB, gl, SSM_GROUP_CH, SSM_STATE), eye).reshape(
            SSM_NGB, SSM_GB_ST, SSM_GB_CH)

    return tiles(ar), tiles(ai), bdiag(bbar_r), bdiag(bbar_i), cdiag(c_re), cdiag(c_im)


def _to_time_major(dst, val, tt, first_row=0):
    for j in range(SUBLANES):
        dst[pl.ds(first_row + j, tt, stride=SUBLANES), :] = val[:, j * LANES:(j + 1) * LANES]


def _from_time_major(dst, src, tt):
    for j in range(SUBLANES):
        dst[:, j * LANES:(j + 1) * LANES] = src[pl.ds(j, tt, stride=SUBLANES), :]


def _ssm_fwd(z, ar, ai, bbr, bbi, cbr, cbi, dskip, name, comm=None):
    T = z.shape[0]
    tt = min(SSM_TT, T)
    nt = T // tt

    def body(u_ref, ar_ref, ai_ref, br_ref, bi_ref, cr_ref, ci_ref, d_ref, y_ref, sr_ref, si_ref,
             tmr, tmi, car_r, car_i):
        @pl.when(pl.program_id(1) == 0)
        def _():
            car_r[...] = jnp.zeros_like(car_r)
            car_i[...] = jnp.zeros_like(car_i)

        u = u_ref[...]
        u16 = u.astype(BF16)
        _to_time_major(tmr, _dot(u16, br_ref[...]), tt)
        _to_time_major(tmi, _dot(u16, bi_ref[...]), tt)
        a_r = ar_ref[...]
        a_i = ai_ref[...]

        def step(t, carry):
            s_r, s_i = carry
            rows = pl.ds(pl.multiple_of(t * SUBLANES, SUBLANES), SUBLANES)
            n_r = a_r * s_r - a_i * s_i + tmr[rows, :]
            n_i = a_r * s_i + a_i * s_r + tmi[rows, :]
            tmr[rows, :] = n_r
            tmi[rows, :] = n_i
            return n_r, n_i

        s_r, s_i = lax.fori_loop(0, tt, step, (car_r[...], car_i[...]), unroll=8)
        car_r[...] = s_r
        car_i[...] = s_i
        _from_time_major(sr_ref, tmr, tt)
        _from_time_major(si_ref, tmi, tt)
        y_ref[...] = (_dot(sr_ref[...].astype(BF16), cr_ref[...]) - _dot(si_ref[...].astype(BF16), ci_ref[...])
                      + d_ref[...] * u)

    u_spec = pl.BlockSpec((tt, SSM_GB_CH), lambda b, t: (t, OFF_U // SSM_GB_CH + b))
    a_spec = pl.BlockSpec((None, SUBLANES, LANES), lambda b, t: (b, 0, 0))
    b_spec = pl.BlockSpec((None, SSM_GB_CH, SSM_GB_ST), lambda b, t: (b, 0, 0))
    c_spec = pl.BlockSpec((None, SSM_GB_ST, SSM_GB_CH), lambda b, t: (b, 0, 0))
    d_spec = pl.BlockSpec((1, SSM_GB_CH), lambda b, t: (0, b))
    y_spec = pl.BlockSpec((tt, SSM_GB_CH), lambda b, t: (t, b))
    s_spec = pl.BlockSpec((tt, SSM_GB_ST), lambda b, t: (t, b))
    n_state = SSM_NGB * SSM_GB_ST
    outs, extra = _call(
        body, name, (SSM_NGB, nt), [u_spec, a_spec, a_spec, b_spec, b_spec, c_spec, c_spec, d_spec],
        [y_spec, s_spec, s_spec],
        [jax.ShapeDtypeStruct((T, SSM_WIDTH), F32), jax.ShapeDtypeStruct((T, n_state), F32),
         jax.ShapeDtypeStruct((T, n_state), F32)],
        [pltpu.VMEM((tt * SUBLANES, LANES), F32), pltpu.VMEM((tt * SUBLANES, LANES), F32),
         pltpu.VMEM((SUBLANES, LANES), F32), pltpu.VMEM((SUBLANES, LANES), F32)],
        ("parallel", "arbitrary"), (z, ar, ai, bbr, bbi, cbr, cbi, dskip), comm)
    return outs if comm is None else (outs, extra)


def _ssm_bwd(dy, z, s_r, s_i, ar, ai, bbr, bbi, cbr, cbi, dskip, name, comm=None):
    T = z.shape[0]
    tt = min(SSM_TT, T)
    nt = T // tt
    per8 = tt // SUBLANES

    def body(dy_ref, u_ref, sr_ref, si_ref, srp_ref, sip_ref, ar_ref, ai_ref, br_ref, bi_ref, cr_ref, ci_ref, d_ref,
             du_ref, dar_ref, dai_ref, dbr_ref, dbi_ref, dcr_ref, dci_ref, dd_ref,
             tmr, tmi, smr, smi, natr, nati, car_r, car_i):
        tb = pl.program_id(1)
        first_block = tb == nt - 1

        @pl.when(tb == 0)
        def _():
            for ref in (car_r, car_i, dar_ref, dai_ref, dbr_ref, dbi_ref, dcr_ref, dci_ref, dd_ref):
                ref[...] = jnp.zeros_like(ref)

        dy = dy_ref[...]
        dy16 = dy.astype(BF16)
        u = u_ref[...]
        u16 = u.astype(BF16)
        _to_time_major(tmr, _dot(dy16, cr_ref[...], "nt"), tt)
        _to_time_major(tmi, -_dot(dy16, ci_ref[...], "nt"), tt)
        _to_time_major(smr, sr_ref[...], tt, first_row=SUBLANES)
        _to_time_major(smi, si_ref[...], tt, first_row=SUBLANES)
        keep = jnp.where(first_block, 0.0, 1.0)
        for j in range(SUBLANES):
            smr[j:j + 1, :] = srp_ref[SUBLANES - 1:SUBLANES, j * LANES:(j + 1) * LANES] * keep
            smi[j:j + 1, :] = sip_ref[SUBLANES - 1:SUBLANES, j * LANES:(j + 1) * LANES] * keep
        a_r = ar_ref[...]
        a_i = ai_ref[...]

        def step(i, carry):
            n_r, n_i, da_r, da_i = carry
            rows = pl.ds(pl.multiple_of((tt - 1 - i) * SUBLANES, SUBLANES), SUBLANES)
            g_r = tmr[rows, :] + a_r * n_r + a_i * n_i
            g_i = tmi[rows, :] - a_i * n_r + a_r * n_i
            tmr[rows, :] = g_r
            tmi[rows, :] = g_i
            p_r = smr[rows, :]
            p_i = smi[rows, :]
            return g_r, g_i, da_r + g_r * p_r + g_i * p_i, da_i - g_r * p_i + g_i * p_r

        zero = jnp.zeros((SUBLANES, LANES), F32)
        n_r, n_i, da_r, da_i = lax.fori_loop(0, tt, step, (car_r[...], car_i[...], zero, zero), unroll=8)
        car_r[...] = n_r
        car_i[...] = n_i
        dar_ref[...] += da_r
        dai_ref[...] += da_i
        _from_time_major(natr, tmr, tt)
        _from_time_major(nati, tmi, tt)
        dbu_r16 = natr[...].astype(BF16)
        dbu_i16 = nati[...].astype(BF16)
        du_ref[...] = (_dot(dbu_r16, br_ref[...], "nt") + _dot(dbu_i16, bi_ref[...], "nt")
                       + d_ref[...] * dy).astype(BF16)
        dbr_ref[...] += _dot(u16, dbu_r16, "tn")
        dbi_ref[...] += _dot(u16, dbu_i16, "tn")
        dcr_ref[...] += _dot(sr_ref[...].astype(BF16), dy16, "tn")
        dci_ref[...] -= _dot(si_ref[...].astype(BF16), dy16, "tn")
        dd_ref[...] += jnp.sum(dy * u, axis=0, keepdims=True)

    def rev(t):
        return nt - 1 - t

    dy_spec = pl.BlockSpec((tt, SSM_GB_CH), lambda b, t: (rev(t), b))
    u_spec = pl.BlockSpec((tt, SSM_GB_CH), lambda b, t: (rev(t), OFF_U // SSM_GB_CH + b))
    s_spec = pl.BlockSpec((tt, SSM_GB_ST), lambda b, t: (rev(t), b))
    sp_spec = pl.BlockSpec((SUBLANES, SSM_GB_ST), lambda b, t: (jnp.maximum(rev(t) * per8 - 1, 0), b))
    a_spec = pl.BlockSpec((None, SUBLANES, LANES), lambda b, t: (b, 0, 0))
    b_spec = pl.BlockSpec((None, SSM_GB_CH, SSM_GB_ST), lambda b, t: (b, 0, 0))
    c_spec = pl.BlockSpec((None, SSM_GB_ST, SSM_GB_CH), lambda b, t: (b, 0, 0))
    d_spec = pl.BlockSpec((1, SSM_GB_CH), lambda b, t: (0, b))
    tm_shape = pltpu.VMEM((tt * SUBLANES, LANES), F32)
    sm_shape = pltpu.VMEM(((tt + 1) * SUBLANES, LANES), F32)
    nat_shape = pltpu.VMEM((tt, SSM_GB_ST), F32)
    tile = pltpu.VMEM((SUBLANES, LANES), F32)
    outs, extra = _call(
        body, name, (SSM_NGB, nt),
        [dy_spec, u_spec, s_spec, s_spec, sp_spec, sp_spec, a_spec, a_spec, b_spec, b_spec, c_spec, c_spec, d_spec],
        [dy_spec, a_spec, a_spec, b_spec, b_spec, c_spec, c_spec, d_spec],
        [jax.ShapeDtypeStruct((T, SSM_WIDTH), BF16),
         jax.ShapeDtypeStruct((SSM_NGB, SUBLANES, LANES), F32),
         jax.ShapeDtypeStruct((SSM_NGB, SUBLANES, LANES), F32),
         jax.ShapeDtypeStruct((SSM_NGB, SSM_GB_CH, SSM_GB_ST), F32),
         jax.ShapeDtypeStruct((SSM_NGB, SSM_GB_CH, SSM_GB_ST), F32),
         jax.ShapeDtypeStruct((SSM_NGB, SSM_GB_ST, SSM_GB_CH), F32),
         jax.ShapeDtypeStruct((SSM_NGB, SSM_GB_ST, SSM_GB_CH), F32),
         jax.ShapeDtypeStruct((1, SSM_WIDTH), F32)],
        [tm_shape, tm_shape, sm_shape, sm_shape, nat_shape, nat_shape, tile, tile], ("parallel", "arbitrary"),
        (dy, z, s_r, s_i, s_r, s_i, ar, ai, bbr, bbi, cbr, cbi, dskip), comm)
    return outs if comm is None else (outs, extra)


def _glu_fwd(y, w, b, name):
    T, W = y.shape
    tm = _pick(T, 512)

    def body(y_ref, w_ref, b_ref, pre_ref, y3_ref):
        y2 = _gelu(y_ref[...])
        pre = _dot(y2.astype(BF16), w_ref[...]) + b_ref[...]
        pre_ref[...] = pre
        y3_ref[...] = (y2 * _sigmoid(pre)).astype(BF16)

    row = pl.BlockSpec((tm, W), lambda i: (i, 0))
    return pl.pallas_call(
        body, name=name, grid=(T // tm,),
        in_specs=[row, pl.BlockSpec((W, W), lambda i: (0, 0)), pl.BlockSpec((1, W), lambda i: (0, 0))],
        out_specs=[row, row], out_shape=[jax.ShapeDtypeStruct((T, W), F32), jax.ShapeDtypeStruct((T, W), BF16)],
        compiler_params=_params("parallel"))(y, w, b)


def _glu_bwd_gate(dy3, y, pre, name):
    T, W = y.shape
    tm = _pick(T, 512)

    def body(dy3_ref, y_ref, pre_ref, dpre_ref, t1_ref, y2_ref, db_ref):
        @pl.when(pl.program_id(0) == 0)
        def _():
            db_ref[...] = jnp.zeros_like(db_ref)

        y2 = _gelu(y_ref[...])
        sg = _sigmoid(pre_ref[...])
        dy3 = dy3_ref[...]
        dpre = dy3 * y2 * sg * (1.0 - sg)
        dpre_ref[...] = dpre.astype(BF16)
        t1_ref[...] = dy3 * sg
        y2_ref[...] = y2.astype(BF16)
        db_ref[...] += jnp.sum(dpre, axis=0, keepdims=True)

    row = pl.BlockSpec((tm, W), lambda i: (i, 0))
    vec = pl.BlockSpec((1, W), lambda i: (0, 0))
    return pl.pallas_call(
        body, name=name, grid=(T // tm,), in_specs=[row, row, row], out_specs=[row, row, row, vec],
        out_shape=[jax.ShapeDtypeStruct((T, W), BF16), jax.ShapeDtypeStruct((T, W), F32),
                   jax.ShapeDtypeStruct((T, W), BF16), jax.ShapeDtypeStruct((1, W), F32)],
        compiler_params=_params("arbitrary"))(dy3, y, pre)


def _glu_bwd_in(dpre, w, t1, y, name):
    T, W = y.shape
    tm = _pick(T, 512)

    def body(dpre_ref, w_ref, t1_ref, y_ref, dy_ref):
        dy_ref[...] = (_dot(dpre_ref[...], w_ref[...], "nt") + t1_ref[...]) * _gelu_grad(y_ref[...])

    row = pl.BlockSpec((tm, W), lambda i: (i, 0))
    return pl.pallas_call(
        body, name=name, grid=(T // tm,), in_specs=[row, pl.BlockSpec((W, W), lambda i: (0, 0)), row, row],
        out_specs=row, out_shape=jax.ShapeDtypeStruct((T, W), F32),
        compiler_params=_params("parallel"))(dpre, w, t1, y)


def _merge_fwd(ya, y3, wa, ws, z, bias, name, comm=None):
    T, W = ya.shape
    D = wa.shape[1]
    tm, tn = _pick(T, 512), _pick(D, 512)

    def body(ya_ref, y3_ref, wa_ref, ws_ref, za_ref, zs_ref, ba_ref, bs_ref, a_ref, b_ref, m_ref):
        a = _dot(ya_ref[...], wa_ref[...])
        b = _dot(y3_ref[...], ws_ref[...])
        a_ref[...] = a
        b_ref[...] = b
        m_ref[...] = (_sigmoid(za_ref[...] + ba_ref[...]) * a + _sigmoid(zs_ref[...] + bs_ref[...]) * b).astype(BF16)

    act = pl.BlockSpec((tm, W), lambda i, j: (i, 0))
    wgt = pl.BlockSpec((W, tn), lambda i, j: (0, j))
    za = pl.BlockSpec((tm, tn), lambda i, j: (i, OFF_G // tn + j))
    zs = pl.BlockSpec((tm, tn), lambda i, j: (i, (OFF_G + D) // tn + j))
    ba = pl.BlockSpec((1, tn), lambda i, j: (0, j))
    bs = pl.BlockSpec((1, tn), lambda i, j: (0, D // tn + j))
    out = pl.BlockSpec((tm, tn), lambda i, j: (i, j))
    outs, extra = _call(
        body, name, (T // tm, D // tn), [act, act, wgt, wgt, za, zs, ba, bs], [out, out, out],
        [jax.ShapeDtypeStruct((T, D), F32), jax.ShapeDtypeStruct((T, D), F32), jax.ShapeDtypeStruct((T, D), BF16)],
        [], ("parallel", "parallel"), (ya, y3, wa, ws, z, z, bias, bias), comm)
    return outs if comm is None else (outs, extra)


def _merge_bwd(dm, a, b, z, bias, name):
    T, D = dm.shape
    tm, tn = _pick(T, 512), _pick(D, 512)

    def body(dm_ref, a_ref, b_ref, za_ref, zs_ref, ba_ref, bs_ref, da_ref, db_ref, dza_ref, dzs_ref, dba_ref, dbs_ref):
        @pl.when(pl.program_id(1) == 0)
        def _():
            dba_ref[...] = jnp.zeros_like(dba_ref)
            dbs_ref[...] = jnp.zeros_like(dbs_ref)

        dm = dm_ref[...]
        sa = _sigmoid(za_ref[...] + ba_ref[...])
        ss = _sigmoid(zs_ref[...] + bs_ref[...])
        da_ref[...] = (dm * sa).astype(BF16)
        db_ref[...] = (dm * ss).astype(BF16)
        dza = dm * a_ref[...] * sa * (1.0 - sa)
        dzs = dm * b_ref[...] * ss * (1.0 - ss)
        dza_ref[...] = dza.astype(BF16)
        dzs_ref[...] = dzs.astype(BF16)
        dba_ref[...] += jnp.sum(dza, axis=0, keepdims=True)
        dbs_ref[...] += jnp.sum(dzs, axis=0, keepdims=True)

    blk = pl.BlockSpec((tm, tn), lambda j, i: (i, j))
    za = pl.BlockSpec((tm, tn), lambda j, i: (i, OFF_G // tn + j))
    zs = pl.BlockSpec((tm, tn), lambda j, i: (i, (OFF_G + D) // tn + j))
    ba = pl.BlockSpec((1, tn), lambda j, i: (0, j))
    bs = pl.BlockSpec((1, tn), lambda j, i: (0, D // tn + j))
    big = jax.ShapeDtypeStruct((T, D), BF16)
    vec = jax.ShapeDtypeStruct((1, D), F32)
    return pl.pallas_call(
        body, name=name, grid=(D // tn, T // tm), in_specs=[blk, blk, blk, za, zs, ba, bs],
        out_specs=[blk, blk, blk, blk, ba, ba], out_shape=[big, big, big, big, vec, vec],
        compiler_params=_params("parallel", "arbitrary"))(dm, a, b, z, z, bias, bias)


_HALF = N_DEV // 2


def _wgu_block(d):
    return d // 2, d % 2


def _ffn_fwd(h, wgu, name, comm=None):
    T, D = h.shape
    n = wgu.shape[3]
    F = _HALF * n
    tm = _pick(T, 512)

    def body(h_ref, wg_ref, wu_ref, g_ref, u_ref, act_ref):
        hv = h_ref[...]
        g = _dot(hv, wg_ref[...])
        u = _dot(hv, wu_ref[...])
        g_ref[...] = g
        u_ref[...] = u
        act_ref[...] = (g * _sigmoid(g) * u).astype(BF16)

    wg = pl.BlockSpec((None, None, D, n), lambda j, i: (*_wgu_block(j), 0, 0))
    wu = pl.BlockSpec((None, None, D, n), lambda j, i: (*_wgu_block(j + _HALF), 0, 0))
    out = pl.BlockSpec((tm, n), lambda j, i: (i, j))
    outs, extra = _call(
        body, name, (_HALF, T // tm), [pl.BlockSpec((tm, D), lambda j, i: (i, 0)), wg, wu], [out, out, out],
        [jax.ShapeDtypeStruct((T, F), F32), jax.ShapeDtypeStruct((T, F), F32), jax.ShapeDtypeStruct((T, F), BF16)],
        [], ("parallel", "parallel"), (h, wgu, wgu), comm)
    return outs if comm is None else (outs, extra)


def _ffn_bwd_in(dg, du, wgu, name, comm=None):
    T, F = dg.shape
    D, n = wgu.shape[2], wgu.shape[3]
    tm, tn = _pick(T, 1024), _pick(D, 1024)

    def body(dg_ref, du_ref, w_ref, o_ref, acc_ref):
        k = pl.program_id(2)

        @pl.when(k == 0)
        def _():
            acc_ref[...] = jnp.zeros_like(acc_ref)

        @pl.when(k < _HALF)
        def _():
            acc_ref[...] += _dot(dg_ref[...], w_ref[...], "nt")

        @pl.when(k >= _HALF)
        def _():
            acc_ref[...] += _dot(du_ref[...], w_ref[...], "nt")

        @pl.when(k == N_DEV - 1)
        def _():
            o_ref[...] = acc_ref[...]

    dg_spec = pl.BlockSpec((tm, n), lambda i, j, k: (i, jnp.minimum(k, _HALF - 1)))
    du_spec = pl.BlockSpec((tm, n), lambda i, j, k: (i, jnp.maximum(k - _HALF, 0)))
    w_spec = pl.BlockSpec((None, None, tn, n), lambda i, j, k: (*_wgu_block(k), j, 0))
    o_spec = pl.BlockSpec((tm, tn), lambda i, j, k: (i, j))
    (out,), extra = _call(
        body, name, (T // tm, D // tn, N_DEV), [dg_spec, du_spec, w_spec], [o_spec],
        [jax.ShapeDtypeStruct((T, D), F32)], [pltpu.VMEM((tm, tn), F32)], ("parallel", "parallel", "arbitrary"),
        (dg, du, wgu), comm)
    return out if comm is None else (out, extra)


def _ffn_bwd_w(h, dg, du, name, comm=None):
    T, D = h.shape
    n = dg.shape[1] // _HALF
    tm, tk = _pick(D, 1024), _pick(T, 2048)
    nk = T // tk

    def body(h_ref, dg_ref, du_ref, o_ref, acc_ref):
        j, k = pl.program_id(0), pl.program_id(2)

        @pl.when(k == 0)
        def _():
            acc_ref[...] = jnp.zeros_like(acc_ref)

        @pl.when(j < _HALF)
        def _():
            acc_ref[...] += _dot(h_ref[...], dg_ref[...], "tn")

        @pl.when(j >= _HALF)
        def _():
            acc_ref[...] += _dot(h_ref[...], du_ref[...], "tn")

        @pl.when(k == nk - 1)
        def _():
            o_ref[...] = acc_ref[...].astype(BF16)

    h_spec = pl.BlockSpec((tk, tm), lambda j, i, k: (k, i))
    dg_spec = pl.BlockSpec((tk, n), lambda j, i, k: (jnp.where(j < _HALF, k, nk - 1), jnp.minimum(j, _HALF - 1)))
    du_spec = pl.BlockSpec((tk, n), lambda j, i, k: (jnp.where(j >= _HALF, k, 0), jnp.maximum(j - _HALF, 0)))
    o_spec = pl.BlockSpec((None, None, tm, n), lambda j, i, k: (*_wgu_block(j), i, 0))
    (out,), extra = _call(
        body, name, (N_DEV, D // tm, nk), [h_spec, dg_spec, du_spec], [o_spec],
        [jax.ShapeDtypeStruct((_HALF, 2, D, n), BF16)], [pltpu.VMEM((tm, n), F32)],
        ("parallel", "parallel", "arbitrary"), (h, dg, du), comm)
    return out if comm is None else (out, extra)


def _ffn_bwd_act(dx, wo, g, u, name, comm=None):
    T, D = dx.shape
    F = wo.shape[0]
    tm, tn = _pick(T, 1024), _pick(F, 512)

    def body(dx_ref, wo_ref, g_ref, u_ref, dg_ref, du_ref):
        dact = _dot(dx_ref[...].astype(BF16), wo_ref[...], "nt")
        gv = g_ref[...]
        sg = _sigmoid(gv)
        dg_ref[...] = (dact * u_ref[...] * sg * (1.0 + gv * (1.0 - sg))).astype(BF16)
        du_ref[...] = (dact * gv * sg).astype(BF16)

    out = pl.BlockSpec((tm, tn), lambda i, j: (i, j))
    big = jax.ShapeDtypeStruct((T, F), BF16)
    outs, extra = _call(
        body, name, (T // tm, F // tn),
        [pl.BlockSpec((tm, D), lambda i, j: (i, 0)), pl.BlockSpec((tn, D), lambda i, j: (j, 0)), out, out],
        [out, out], [big, big], [], ("parallel", "parallel"), (dx, wo, g, u), comm)
    return outs if comm is None else (outs, extra)


def _place():
    x, y, c = lax.axis_index("x"), lax.axis_index("y"), lax.axis_index("c")
    other_chips = [(1 - x, y), (x, 1 - y), (1 - x, 1 - y)]
    return x, y, c, 2 * x + y, other_chips


_ANY = pl.BlockSpec(memory_space=pl.ANY)
_N_COPIES = 7


def _all_gather(shards, name):
    n = len(shards)

    def body(*refs):
        ins, outs = refs[:n], refs[n:2 * n]
        send_sems, recv_sems, local_sems = refs[2 * n:]
        x, y, c, chip, other_chips = _place()
        sibling = (x, y, 1 - c)

        def remote(src, dst, a, j, dev):
            return pltpu.make_async_remote_copy(src_ref=src, dst_ref=dst, send_sem=send_sems.at[a, j],
                                                recv_sem=recv_sems.at[a, j], device_id=dev, device_id_type=MESH)

        sends, local = [], []
        for a in range(n):
            mine = outs[a].at[chip, c]
            local.append(pltpu.make_async_copy(ins[a], mine, local_sems.at[a]))
            local[a].start()
            for j, (ox, oy) in enumerate(other_chips):
                sends.append(remote(ins[a], mine, a, 1 + j, (ox, oy, c)))
                sends[-1].start()
            sends.append(remote(ins[a], mine, a, 0, sibling))
            sends[-1].start()
        for a in range(n):
            for j, (ox, oy) in enumerate(other_chips):
                slot = outs[a].at[2 * ox + oy, c]
                remote(ins[a], slot, a, 1 + j, (ox, oy, c)).wait_recv()
                sends.append(remote(slot, slot, a, 4 + j, sibling))
                sends[-1].start()
        for a in range(n):
            remote(ins[a], outs[a].at[chip, 1 - c], a, 0, sibling).wait_recv()
            for j, (ox, oy) in enumerate(other_chips):
                remote(ins[a], outs[a].at[2 * ox + oy, 1 - c], a, 4 + j, sibling).wait_recv()
        for cp in sends:
            cp.wait_send()
        for a in range(n):
            local[a].wait()

    return pl.pallas_call(
        body, name=name, in_specs=[_ANY] * n, out_specs=[_ANY] * n,
        out_shape=[jax.ShapeDtypeStruct((4, 2) + s.shape, s.dtype) for s in shards],
        scratch_shapes=[pltpu.SemaphoreType.DMA((n, _N_COPIES)), pltpu.SemaphoreType.DMA((n, _N_COPIES)),
                        pltpu.SemaphoreType.DMA((n,))])(*shards)


def _pair_exchange(parts, name):
    n = len(parts)

    def body(*refs):
        ins, theirs = refs[:n], refs[n:2 * n]
        send_sems, recv_sems = refs[2 * n:]
        x, y, c, _, _ = _place()
        sends = []
        for a in range(n):
            for k in range(4):
                sends.append(pltpu.make_async_remote_copy(
                    src_ref=ins[a].at[k, 1 - c], dst_ref=theirs[a].at[k], send_sem=send_sems.at[a, k],
                    recv_sem=recv_sems.at[a, k], device_id=(x, y, 1 - c), device_id_type=MESH))
                sends[-1].start()
        for cp in sends:
            cp.wait_recv()
            cp.wait_send()

    return pl.pallas_call(
        body, name=name, in_specs=[_ANY] * n, out_specs=[_ANY] * n,
        out_shape=[jax.ShapeDtypeStruct((4,) + p.shape[2:], p.dtype) for p in parts],
        scratch_shapes=[pltpu.SemaphoreType.DMA((n, 4)), pltpu.SemaphoreType.DMA((n, 4))])(*parts)


def _chip_exchange(parts, name):
    n = len(parts)

    def body(*refs):
        ins, got = refs[:n], refs[n:2 * n]
        send_sems, recv_sems = refs[2 * n:]
        _, _, c, _, other_chips = _place()
        sends = []
        for a in range(n):
            for j, (ox, oy) in enumerate(other_chips):
                sends.append(pltpu.make_async_remote_copy(
                    src_ref=ins[a].at[2 * ox + oy], dst_ref=got[a].at[j], send_sem=send_sems.at[a, j],
                    recv_sem=recv_sems.at[a, j], device_id=(ox, oy, c), device_id_type=MESH))
                sends[-1].start()
        for cp in sends:
            cp.wait_recv()
            cp.wait_send()

    return pl.pallas_call(
        body, name=name, in_specs=[_ANY] * n, out_specs=[_ANY] * n,
        out_shape=[jax.ShapeDtypeStruct((3,) + p.shape[1:], p.dtype) for p in parts],
        scratch_shapes=[pltpu.SemaphoreType.DMA((n, 3)), pltpu.SemaphoreType.DMA((n, 3))])(*parts)


def _remote(src, dst, send_sems, recv_sems, a, j, dev):
    return pltpu.make_async_remote_copy(src_ref=src, dst_ref=dst, send_sem=send_sems.at[a, j],
                                        recv_sem=recv_sems.at[a, j], device_id=dev, device_id_type=MESH)


def _gather_send(shards):
    n = len(shards)

    def copies(cin, cout, sems, arriving):
        send_sems, recv_sems, local_sems = sems
        x, y, c, chip, other_chips = _place()
        sibling = (x, y, 1 - c)
        peers = [(0, sibling, (chip, 1 - c))] + [(1 + j, (ox, oy, c), (2 * ox + oy, c))
                                                 for j, (ox, oy) in enumerate(other_chips)]
        sends, recvs, local = [], [], []
        for a in range(n):
            mine = cout[a].at[chip, c]
            local.append(pltpu.make_async_copy(cin[a], mine, local_sems.at[a]))
            for j, dev, slot in peers:
                sends.append(_remote(cin[a], mine, send_sems, recv_sems, a, j, dev))
                if arriving:
                    recvs.append(_remote(cin[a], cout[a].at[slot], send_sems, recv_sems, a, j, dev))
        return sends, recvs, local

    return _Comm(shards, [jax.ShapeDtypeStruct((4, 2) + s.shape, s.dtype) for s in shards],
                 [pltpu.SemaphoreType.DMA((n, 4)), pltpu.SemaphoreType.DMA((n, 4)), pltpu.SemaphoreType.DMA((n,))],
                 copies)


def _gather_pass(gathered):
    n = len(gathered)

    def copies(cin, cout, sems, arriving):
        send_sems, recv_sems = sems
        x, y, c, _, other_chips = _place()
        sibling = (x, y, 1 - c)
        sends, recvs = [], []
        for a in range(n):
            for j, (ox, oy) in enumerate(other_chips):
                k = 2 * ox + oy
                sends.append(_remote(cout[a].at[k, c], cout[a].at[k, c], send_sems, recv_sems, a, j, sibling))
                if arriving:
                    recvs.append(_remote(cout[a].at[k, c], cout[a].at[k, 1 - c], send_sems, recv_sems, a, j, sibling))
        return sends, recvs, []

    return _Comm(gathered, [jax.ShapeDtypeStruct(g.shape, g.dtype) for g in gathered],
                 [pltpu.SemaphoreType.DMA((n, 3)), pltpu.SemaphoreType.DMA((n, 3))], copies,
                 aliases={i: i for i in range(n)})


def _scatter_send(sums):
    n = len(sums)

    def copies(cin, cout, sems, arriving):
        send_sems, recv_sems = sems
        _, _, c, _, other_chips = _place()
        sends = [_remote(cin[a].at[2 * ox + oy], cout[a].at[j], send_sems, recv_sems, a, j, (ox, oy, c))
                 for a in range(n) for j, (ox, oy) in enumerate(other_chips)]
        return sends, sends, []

    return _Comm(sums, [jax.ShapeDtypeStruct((3,) + s.shape[1:], s.dtype) for s in sums],
                 [pltpu.SemaphoreType.DMA((n, 3)), pltpu.SemaphoreType.DMA((n, 3))], copies)


def _join(comms):
    if len(comms) == 1:
        return comms[0]
    args, outs, sems, aliases, spans = [], [], [], {}, []
    for cm in comms:
        spans.append((len(args), len(outs), len(sems)))
        aliases.update({len(args) + i: len(outs) + o for i, o in cm.aliases.items()})
        args, outs, sems = args + cm.args, outs + cm.out_shapes, sems + cm.sems

    def copies(cin, cout, sem_refs, arriving):
        sends, recvs, local = [], [], []
        for cm, (a0, o0, s0) in zip(comms, spans):
            part = cm.copies(cin[a0:a0 + len(cm.args)], cout[o0:o0 + len(cm.out_shapes)],
                             sem_refs[s0:s0 + len(cm.sems)], arriving)
            sends, recvs, local = sends + part[0], recvs + part[1], local + part[2]
        return sends, recvs, local

    return _Comm(args, outs, sems, copies, aliases)


class _Schedule:
    def __init__(self):
        self.rides = {}

    def ride(self, host, make, names, operands):
        results = {}
        self.rides.setdefault(host, []).append((make, names, operands, results))
        return results

    def carry(self, host, fn, *args, **kwargs):
        rides = self.rides.get(host)
        if not rides:
            return fn(*args, host, **kwargs)
        comm = _join([make([operands[n] for n in names]) for make, names, operands, _ in rides])
        out, extra = fn(*args, host, comm=comm, **kwargs)
        for _, names, _, results in rides:
            results.update(zip(names, extra[:len(names)]))
            extra = extra[len(names):]
        return out


def _add_pair(core, parts, theirs, name):
    k, _, R, C = parts.shape
    tr = _pick(R, 512, 16)

    def body(core_ref, a_ref, b_ref, o_ref):
        o_ref[...] = (a_ref[...].astype(F32) + b_ref[...].astype(F32)).astype(BF16)

    blk = pl.BlockSpec((None, tr, C), lambda s, i, core_ref: (s, i, 0))
    grid_spec = pltpu.PrefetchScalarGridSpec(
        num_scalar_prefetch=1, grid=(k, R // tr),
        in_specs=[pl.BlockSpec((None, None, tr, C), lambda s, i, core_ref: (s, core_ref[0], i, 0)), blk],
        out_specs=blk)
    return pl.pallas_call(
        body, name=name, grid_spec=grid_spec, out_shape=jax.ShapeDtypeStruct(theirs.shape, BF16),
        compiler_params=_params("parallel", "parallel"))(core, parts, theirs)


def _adamw_math(w, g, m, v):
    m = ADAM_B1 * m + (1.0 - ADAM_B1) * g
    v = ADAM_B2 * v + (1.0 - ADAM_B2) * (g * g)
    m_hat = m / (1.0 - ADAM_B1 ** ADAM_STEP)
    v_hat = v / (1.0 - ADAM_B2 ** ADAM_STEP)
    delta = -ADAM_LR * (m_hat / (jnp.sqrt(v_hat) + ADAM_EPS) + ADAM_WD * w)
    return delta, m, v


def _scattered_pieces(sums, got):
    return [("own", sums)] + [("peer%d" % j, got) for j in range(3)]


def _piece_spec(kind, tr, C):
    if kind == "own":
        return pl.BlockSpec((None, tr, C), lambda i, chip_ref: (chip_ref[0], i, 0))
    if kind == "plain":
        return pl.BlockSpec((tr, C), lambda i, chip_ref: (i, 0))
    return pl.BlockSpec((None, tr, C), functools.partial(lambda j, i, chip_ref: (j, i, 0), int(kind[-1])))


def _grad_sum(chip, pieces, name):
    R, C = pieces[0][1].shape[-2:]
    tr = _pick(R, 256, 16)

    def body(chip_ref, *refs):
        g = refs[0][...].astype(F32)
        for p in refs[1:-1]:
            g = g + p[...].astype(F32)
        refs[-1][...] = g

    grid_spec = pltpu.PrefetchScalarGridSpec(
        num_scalar_prefetch=1, grid=(R // tr,), in_specs=[_piece_spec(kind, tr, C) for kind, _ in pieces],
        out_specs=_piece_spec("plain", tr, C))
    return pl.pallas_call(
        body, name=name, grid_spec=grid_spec, out_shape=jax.ShapeDtypeStruct((R, C), F32),
        compiler_params=_params("parallel"))(chip, *[a for _, a in pieces])


def _adamw_shard(chip, w, m, v, layer, pieces, so_far, name):
    L, R, C = w.shape
    tr = _pick(R, 256, 16)
    n_p = len(pieces)

    def body(chip_ref, w_ref, m_ref, v_ref, *rest):
        g = rest[0][...].astype(F32)
        for p in rest[1:n_p]:
            g = g + p[...].astype(F32)
        delta, nm, nv = _adamw_math(w_ref[...], g, m_ref[...], v_ref[...])
        g_ref, d_ref, nm_ref, nv_ref = rest[-4:]
        g_ref[...] = g
        d_ref[...] = delta
        nm_ref[...] = nm
        nv_ref[...] = nv

    state = pl.BlockSpec((None, tr, C), lambda i, chip_ref: (layer, i, 0))
    carried = [] if so_far is None else list(so_far)
    first_carried = 1 + 3 + n_p
    grid_spec = pltpu.PrefetchScalarGridSpec(
        num_scalar_prefetch=1, grid=(R // tr,),
        in_specs=[state] * 3 + [_piece_spec(kind, tr, C) for kind, _ in pieces] + [_ANY] * len(carried),
        out_specs=[state] * 4)
    return pl.pallas_call(
        body, name=name, grid_spec=grid_spec, out_shape=[jax.ShapeDtypeStruct((L, R, C), F32)] * 4,
        input_output_aliases={first_carried + t: t for t in range(len(carried))},
        compiler_params=_params("parallel"))(chip, w, m, v, *[a for _, a in pieces], *carried)


def _adamw_small(w, m, v, gathered, name):
    R = w.shape[0]
    tr = _pick(R, 512, SUBLANES)

    def body(w_ref, m_ref, v_ref, gg_ref, g_ref, d_ref, nm_ref, nv_ref):
        g = gg_ref[0, 0]
        for k in range(4):
            for c in range(2):
                if c or k:
                    g = g + gg_ref[k, c]
        delta, nm, nv = _adamw_math(w_ref[...], g, m_ref[...], v_ref[...])
        g_ref[...] = g
        d_ref[...] = delta
        nm_ref[...] = nm
        nv_ref[...] = nv

    row = pl.BlockSpec((tr, LANES), lambda i: (i, 0))
    out = jax.ShapeDtypeStruct((R, LANES), F32)
    return pl.pallas_call(
        body, name=name, grid=(R // tr,),
        in_specs=[row, row, row, pl.BlockSpec((4, 2, tr, LANES), lambda i: (0, 0, i, 0))], out_specs=[row] * 4,
        out_shape=[out] * 4, compiler_params=_params("parallel"))(w, m, v, gathered)


def _to_heads(a, n_heads):
    return a.reshape(a.shape[0], n_heads, HEAD_DIM).transpose(1, 0, 2)


def _from_heads(a):
    return a.transpose(1, 0, 2).reshape(a.shape[1], a.shape[0] * HEAD_DIM)


def _layer_fwd(x, w, s, tag, sched):
    h = _rms_fwd(x, s["norm_mix_g"], f"rms_mix_{tag}")
    z = sched.carry(f"in_proj_{tag}", _mm, h, w["w_in_t"], "nt", F32, tn=512)
    q = _to_heads(z[:, :OFF_K], N_Q_HEADS)
    k = _to_heads(z[:, OFF_K:OFF_V], N_KV_HEADS)
    v = _to_heads(z[:, OFF_V:OFF_U], N_KV_HEADS)
    ya = _from_heads(sched.carry(f"attn_fwd_{tag}", _attn_fwd, q, k, v, s["q_norm_g"], s["k_norm_g"],
                                 s["attn_sinks"]))
    y, s_r, s_i = sched.carry(f"ssm_fwd_{tag}", _ssm_fwd, z, *s["ssm16"], s["ssm_d"])
    pre, y3 = _glu_fwd(y, w["ssm_glu_w"], s["ssm_glu_b"], f"glu_fwd_{tag}")
    a, b, merged = sched.carry(f"merge_fwd_{tag}", _merge_fwd, ya, y3, w["w_attn_branch"], w["w_ssm_branch"], z,
                               s["gate_bias"])
    x1 = _mm(merged, w["w_out"], "nn", F32, f"out_proj_{tag}", residual=x)
    h2 = _rms_fwd(x1, s["norm_ffn_g"], f"rms_ffn_{tag}")
    g, u, act = sched.carry(f"ffn_fwd_{tag}", _ffn_fwd, h2, w["w_ffn_in"])
    x2 = sched.carry(f"ffn_out_{tag}", _mm, act, w["w_ffn_out"], "nn", F32, residual=x1)
    saved = dict(x=x, h=h, z=z, q=q, k=k, v=v, ya=ya, y=y, s_r=s_r, s_i=s_i, pre=pre, y3=y3, a=a, b=b, merged=merged,
                 x1=x1, h2=h2, g=g, u=u, act=act)
    return x2, saved


def _layer_bwd(dx2, sv, w, s, tag, sched, scatter):
    gw, gs = {}, {}
    dg16, du16 = _ffn_bwd_act(dx2, w["w_ffn_out"], sv["g"], sv["u"], f"ffn_bwd_act_{tag}")
    gw["w_ffn_out"] = _mm(sv["act"], dx2, "tn", BF16, f"dw_ffn_out_{tag}", tm=1408)
    dh2 = _ffn_bwd_in(dg16, du16, w["w_ffn_in"], f"dh2_{tag}")
    gw["w_ffn_in"] = _ffn_bwd_w(sv["h2"], dg16, du16, f"dw_ffn_in_{tag}")
    scatter(FFN_WEIGHTS, gw, f"attn_bwd_{tag}")
    dx1, gs["norm_ffn_g"] = _rms_bwd(dh2, sv["x1"], s["norm_ffn_g"], dx2, f"rms_ffn_bwd_{tag}")
    dm = _mm(dx1, w["w_out"], "nt", F32, f"dmerged_{tag}")
    gw["w_out"] = _mm(sv["merged"], dx1, "tn", BF16, f"dw_out_{tag}")
    da16, db16, dza, dzs, dba, dbs = _merge_bwd(dm, sv["a"], sv["b"], sv["z"], s["gate_bias"], f"merge_bwd_{tag}")
    gs["gate_bias"] = jnp.concatenate([dba, dbs], axis=1)
    dya = _mm(da16, w["w_attn_branch"], "nt", F32, f"dya_{tag}")
    gw["w_attn_branch"] = _mm(sv["ya"], da16, "tn", BF16, f"dw_attn_branch_{tag}")
    dy3 = _mm(db16, w["w_ssm_branch"], "nt", F32, f"dy3_{tag}")
    gw["w_ssm_branch"] = _mm(sv["y3"], db16, "tn", BF16, f"dw_ssm_branch_{tag}")
    dpre16, t1, y2_16, gs["ssm_glu_b"] = _glu_bwd_gate(dy3, sv["y"], sv["pre"], f"glu_bwd_gate_{tag}")
    dy = _glu_bwd_in(dpre16, w["ssm_glu_w"], t1, sv["y"], f"glu_bwd_in_{tag}")
    gw["ssm_glu_w"] = _mm(y2_16, dpre16, "tn", BF16, f"dw_glu_{tag}")
    scatter(MIXER_WEIGHTS[1:], gw, f"ssm_bwd_{tag}")
    du_ssm, *gs["ssm_disc"], gs["ssm_d"] = sched.carry(f"ssm_bwd_{tag}", _ssm_bwd, dy, sv["z"], sv["s_r"], sv["s_i"],
                                                       *s["ssm16"], s["ssm_d"])
    dq, dk, dv, dqg, dkg, dsk = sched.carry(f"attn_bwd_{tag}", _attn_bwd, sv["q"], sv["k"], sv["v"],
                                            _to_heads(dya, N_Q_HEADS), s["q_norm_g"], s["k_norm_g"], s["attn_sinks"])
    gs["q_norm_g"] = jnp.sum(dqg, axis=0)
    gs["k_norm_g"] = jnp.sum(dkg, axis=0)
    gs["attn_sinks"] = dsk[:, :, 0].reshape(1, N_Q_HEADS)
    dz = jnp.concatenate([_from_heads(dq), _from_heads(dk).astype(BF16), _from_heads(dv).astype(BF16), du_ssm, dza,
                          dzs], axis=1)
    gw["w_in"] = _mm(dz, sv["h"], "tn", BF16, f"dw_in_t_{tag}", tm=1664)
    scatter(MIXER_WEIGHTS[:1], gw, f"dh_{tag}")
    dh = sched.carry(f"dh_{tag}", _mm, dz, w["w_in_t"], "nn", F32)
    dx, gs["norm_mix_g"] = _rms_bwd(dh, sv["x"], s["norm_mix_g"], dx1, f"rms_mix_bwd_{tag}")
    return dx, gs


def _shard_to_send(name, shard):
    return (shard.T if name == "w_in" else shard).astype(BF16)


def _assemble(name, gathered):
    if name == "w_ffn_in":
        return gathered
    if name in COL_SHARDED and name != "w_in":
        rows = gathered.shape[2]
        return gathered.transpose(2, 0, 1, 3).reshape(rows, -1)
    return gathered.reshape(-1, gathered.shape[3])


class _Weights:
    def __init__(self):
        self.sources, self.ready = [], {}

    def __getitem__(self, name):
        if name not in self.ready:
            key = "w_in" if name == "w_in_t" else name
            (gathered,) = [src[key] for src in self.sources if key in src]
            self.ready[name] = _assemble(key, gathered)
        return self.ready[name]


def _disassemble(name, grad):
    if name == "w_ffn_in":
        return grad
    if name in COL_SHARDED and name != "w_in":
        rows, cols = grad.shape
        return grad.reshape(rows, 4, 2, cols // N_DEV).transpose(1, 2, 0, 3)
    rows, cols = grad.shape
    return grad.reshape(4, 2, rows // N_DEV, cols)


def _pack(arrays):
    flat = jnp.concatenate([a.reshape(-1) for a in arrays])
    pad = (-flat.shape[0]) % (SUBLANES * LANES)
    return jnp.pad(flat, (0, pad)).reshape(-1, LANES)


def _unpack(packed, like):
    flat, out, off = packed.reshape(-1), [], 0
    for a in like:
        out.append(flat[off:off + a.size].reshape(a.shape))
        off += a.size
    return out


def kernel(x, norm_mix_g, w_in, gate_bias, q_norm_g, k_norm_g, attn_sinks, ssm_lambda_re, ssm_lambda_im, ssm_log_dt, ssm_b_re, ssm_b_im, ssm_c_re, ssm_c_im, ssm_d, ssm_glu_w, ssm_glu_b, w_attn_branch, w_ssm_branch, w_out, norm_ffn_g, w_ffn_in, w_ffn_out, loss_target, m_norm_mix_g, m_w_in, m_gate_bias, m_q_norm_g, m_k_norm_g, m_attn_sinks, m_ssm_lambda_re, m_ssm_lambda_im, m_ssm_log_dt, m_ssm_b_re, m_ssm_b_im, m_ssm_c_re, m_ssm_c_im, m_ssm_d, m_ssm_glu_w, m_ssm_glu_b, m_w_attn_branch, m_w_ssm_branch, m_w_out, m_norm_ffn_g, m_w_ffn_in, m_w_ffn_out, v_norm_mix_g, v_w_in, v_gate_bias, v_q_norm_g, v_k_norm_g, v_attn_sinks, v_ssm_lambda_re, v_ssm_lambda_im, v_ssm_log_dt, v_ssm_b_re, v_ssm_b_im, v_ssm_c_re, v_ssm_c_im, v_ssm_d, v_ssm_glu_w, v_ssm_glu_b, v_w_attn_branch, v_w_ssm_branch, v_w_out, v_norm_ffn_g, v_w_ffn_in, v_w_ffn_out):
    given = dict(locals())
    wts = {n: given[n] for n in WEIGHTS}
    mom = {n: given["m_" + n] for n in WEIGHTS}
    var = {n: given["v_" + n] for n in WEIGHTS}
    depth = w_in.shape[0]
    xs = x[0]
    target = loss_target[0]

    sched = _Schedule()
    shards = [{n: _shard_to_send(n, wts[n][l]) for n in BIG} for l in range(depth)]
    full = [_Weights() for _ in range(depth)]

    def gather(layer, names, send_host, pass_host):
        sent = sched.ride(send_host, _gather_send, names, shards[layer])
        full[layer].sources.append(sched.ride(pass_host, _gather_pass, names, sent))

    first = MIXER_WEIGHTS[:1]
    full[0].sources.append(dict(zip(first, _all_gather([shards[0][n] for n in first], "gather_w_in_0"))))
    gather(0, ("w_ffn_in",), "attn_fwd_0", "ssm_fwd_0")
    gather(0, ("w_ffn_out",), "ssm_fwd_0", "merge_fwd_0")
    for l in range(depth):
        gather(l, MIXER_WEIGHTS[1:], f"in_proj_{l}", f"attn_fwd_{l}")
        if l + 1 < depth:
            gather(l + 1, first, f"merge_fwd_{l}", f"ffn_fwd_{l}")
            gather(l + 1, ("w_ffn_in",), f"ffn_fwd_{l}", f"ffn_out_{l}")
            gather(l + 1, ("w_ffn_out",), f"ffn_out_{l}", f"in_proj_{l + 1}")

    small, disc_vjp = [], []
    for l in range(depth):
        s = {n: wts[n][l].reshape(1, -1) for n in ("norm_mix_g", "gate_bias", "q_norm_g", "k_norm_g", "attn_sinks",
                                                   "ssm_d", "ssm_glu_b", "norm_ffn_g")}
        disc, vjp = jax.vjp(_ssm_discretize, *[wts[n][l] for n in ("ssm_lambda_re", "ssm_lambda_im", "ssm_log_dt",
                                                                  "ssm_b_re", "ssm_b_im", "ssm_c_re", "ssm_c_im")])
        s["ssm16"] = (disc[0], disc[1]) + tuple(d.astype(BF16) for d in disc[2:])
        small.append(s)
        disc_vjp.append(vjp)

    act, saved = xs, []
    for l in range(depth):
        act, sv = _layer_fwd(act, full[l], small[l], str(l), sched)
        saved.append(sv)
    dact, loss_local = _loss_grad(act, target, "loss_head")

    out = {"grad": {}, "delta": {}, "new_m": {}, "new_v": {}}
    results = {n: None for n in BIG}
    core = lax.axis_index("c").astype(jnp.int32).reshape(1)
    chip = (2 * lax.axis_index("x") + lax.axis_index("y")).astype(jnp.int32).reshape(1)
    small_grads = [None] * depth
    for l in reversed(range(depth)):
        scattered = []

        def scatter(names, grads, host):
            parts = [_disassemble(n, grads[n]) for n in names]
            theirs = _pair_exchange(parts, f"grad_pair_exchange_{host}")
            sums = {n: _add_pair(core, p, t, f"grad_pair_sum_{n}_{host}") for n, p, t in zip(names, parts, theirs)}
            scattered.append((sums, sched.ride(host, _scatter_send, names, sums)))

        dact, gs = _layer_bwd(dact, saved[l], full[l], small[l], str(l), sched, scatter)
        (gs["ssm_lambda_re"], gs["ssm_lambda_im"], gs["ssm_log_dt"], gs["ssm_b_re"], gs["ssm_b_im"], gs["ssm_c_re"],
         gs["ssm_c_im"]) = disc_vjp[l](tuple(gs.pop("ssm_disc")))
        small_grads[l] = gs
        for sums, got in scattered:
            for n in got:
                pieces = _scattered_pieces(sums[n], got[n])
                if n == "w_in":
                    pieces = [("plain", _grad_sum(chip, pieces, f"grad_sum_w_in_{l}").T)]
                results[n] = _adamw_shard(chip, wts[n], mom[n], var[n], l, pieces, results[n], f"adamw_{n}_{l}")
    loss = lax.psum(loss_local, ("x", "y", "c"))
    for n in BIG:
        for kind, res in zip(("grad", "delta", "new_m", "new_v"), results[n]):
            out[kind][n] = res

    like = [wts[n] for n in SMALL]
    g_small = _pack([jnp.stack([small_grads[l][n].reshape(wts[n].shape[1:]) for l in range(depth)]) for n in SMALL])
    (gathered_small,) = _all_gather([g_small], "gather_small_grads")
    res = _adamw_small(_pack(like), _pack([mom[n] for n in SMALL]), _pack([var[n] for n in SMALL]), gathered_small,
                       "adamw_small")
    for kind, packed in zip(("grad", "delta", "new_m", "new_v"), res):
        for n, a in zip(SMALL, _unpack(packed, like)):
            out[kind][n] = a

    grad_x = dact.reshape(x.shape)
    return (loss, grad_x, *[out["grad"][n] for n in WEIGHTS], *[out["delta"][n] for n in WEIGHTS],
            *[out["new_m"][n] for n in WEIGHTS], *[out["new_v"][n] for n in WEIGHTS])
```

```python
import functools
import math

import jax
import jax.numpy as jnp
from jax import lax
from jax.experimental import pallas as pl
from jax.experimental.pallas import tpu as pltpu

F32, BF16 = jnp.float32, jnp.bfloat16
MESH = pl.DeviceIdType.MESH

D_MODEL = 2048
HEAD_DIM = 64
N_Q_HEADS = 16
N_KV_HEADS = 4
GQA_GROUP = N_Q_HEADS // N_KV_HEADS
ATTN_WIDTH = N_Q_HEADS * HEAD_DIM
KV_WIDTH = N_KV_HEADS * HEAD_DIM
WINDOW = 128
BLOCK = 128
SSM_WIDTH = D_MODEL // 2
SSM_GROUP_CH = 16
SSM_GROUPS = SSM_WIDTH // SSM_GROUP_CH
SSM_STATE = 64
D_FF = 5632
OFF_K = ATTN_WIDTH
OFF_V = OFF_K + KV_WIDTH
OFF_U = OFF_V + KV_WIDTH
OFF_G = OFF_U + SSM_WIDTH
IN_WIDTH = OFF_G + 2 * D_MODEL
RMS_EPS = 1e-6
ATTN_SCALE = HEAD_DIM ** -0.5
NEG_BIG = -1e30

SSM_NGB = 4
SSM_GB_CH = SSM_WIDTH // SSM_NGB
SSM_GB_ST = SSM_GROUPS * SSM_STATE // SSM_NGB
SUBLANES = 8
LANES = 128
SSM_TT = 256

ADAM_LR = 0.001
ADAM_B1 = 0.9
ADAM_B2 = 0.999
ADAM_EPS = 1e-08
ADAM_WD = 0.01
ADAM_STEP = 10

N_DEV = 8
VMEM_LIMIT_BYTES = 52 * 1024 * 1024

BIG = ("w_in", "ssm_glu_w", "w_attn_branch", "w_ssm_branch", "w_out", "w_ffn_in", "w_ffn_out")
COL_SHARDED = ("w_in", "w_attn_branch", "w_ssm_branch", "w_ffn_in")
FFN_WEIGHTS = ("w_ffn_in", "w_ffn_out")
MIXER_WEIGHTS = ("w_in", "ssm_glu_w", "w_attn_branch", "w_ssm_branch", "w_out")
SMALL = ("norm_mix_g", "gate_bias", "q_norm_g", "k_norm_g", "attn_sinks", "ssm_lambda_re", "ssm_lambda_im",
         "ssm_log_dt", "ssm_b_re", "ssm_b_im", "ssm_c_re", "ssm_c_im", "ssm_d", "ssm_glu_b", "norm_ffn_g")
WEIGHTS = ("norm_mix_g", "w_in", "gate_bias", "q_norm_g", "k_norm_g", "attn_sinks", "ssm_lambda_re", "ssm_lambda_im",
           "ssm_log_dt", "ssm_b_re", "ssm_b_im", "ssm_c_re", "ssm_c_im", "ssm_d", "ssm_glu_w", "ssm_glu_b",
           "w_attn_branch", "w_ssm_branch", "w_out", "norm_ffn_g", "w_ffn_in", "w_ffn_out")


def _pick(n, target, mult=LANES):
    best = None
    for t in range(mult, min(n, target) + 1, mult):
        if n % t == 0:
            best = t
    return n if best is None else best


def _params(*sem):
    return pltpu.CompilerParams(dimension_semantics=sem, vmem_limit_bytes=VMEM_LIMIT_BYTES)


def _sigmoid(v):
    return 1.0 / (1.0 + jnp.exp(-v))


_GELU_C = math.sqrt(2.0 / math.pi)


def _gelu(v):
    return 0.5 * v * (1.0 + jnp.tanh(_GELU_C * (v + 0.044715 * v * v * v)))


def _gelu_grad(v):
    t = jnp.tanh(_GELU_C * (v + 0.044715 * v * v * v))
    return 0.5 * (1.0 + t) + 0.5 * v * (1.0 - t * t) * _GELU_C * (1.0 + 3.0 * 0.044715 * v * v)


_DN = {"nn": (((1,), (0,)), ((), ())), "nt": (((1,), (1,)), ((), ())), "tn": (((0,), (0,)), ((), ()))}


def _dot(a, b, dims="nn"):
    return lax.dot_general(a, b, _DN[dims], preferred_element_type=F32)


class _Comm:
    def __init__(self, args, out_shapes, sems, copies, aliases=None):
        self.args, self.out_shapes, self.sems = list(args), list(out_shapes), list(sems)
        self.copies, self.aliases = copies, dict(aliases or {})


def _call(body, name, grid, in_specs, out_specs, out_shape, scratch_shapes, semantics, args, comm=None):
    in_specs, out_specs, out_shape = list(in_specs), list(out_specs), list(out_shape)
    scratch_shapes = list(scratch_shapes)
    if comm is None:
        res = pl.pallas_call(body, name=name, grid=grid, in_specs=in_specs, out_specs=out_specs, out_shape=out_shape,
                             scratch_shapes=scratch_shapes, compiler_params=_params(*semantics))(*args)
        return list(res), []
    n_in, n_out, n_scr = len(in_specs), len(out_specs), len(scratch_shapes)
    n_cin, n_cout = len(comm.args), len(comm.out_shapes)

    def carrying(*refs):
        ins, cin = refs[:n_in], refs[n_in:n_in + n_cin]
        o0 = n_in + n_cin
        outs, cout = refs[o0:o0 + n_out], refs[o0 + n_out:o0 + n_out + n_cout]
        s0 = o0 + n_out + n_cout
        scr, sems = refs[s0:s0 + n_scr], refs[s0 + n_scr:]
        first = functools.reduce(jnp.logical_and, [pl.program_id(d) == 0 for d in range(len(grid))])
        last = functools.reduce(jnp.logical_and, [pl.program_id(d) == grid[d] - 1 for d in range(len(grid))])

        @pl.when(first)
        def _():
            sends, _, local = comm.copies(cin, cout, sems, False)
            for cp in local + sends:
                cp.start()

        body(*ins, *outs, *scr)

        @pl.when(last)
        def _():
            sends, recvs, local = comm.copies(cin, cout, sems, True)
            for cp in recvs:
                cp.wait_recv()
            for cp in sends:
                cp.wait_send()
            for cp in local:
                cp.wait()

    res = pl.pallas_call(
        carrying, name=name, grid=grid, in_specs=in_specs + [_ANY] * n_cin, out_specs=out_specs + [_ANY] * n_cout,
        out_shape=out_shape + comm.out_shapes, scratch_shapes=scratch_shapes + comm.sems,
        input_output_aliases={n_in + i: n_out + o for i, o in comm.aliases.items()},
        compiler_params=_params(*["arbitrary"] * len(grid)))(*args, *comm.args)
    return list(res[:n_out]), list(res[n_out:])


def _mm(a, b, dims, out_dtype, name, residual=None, tm=1024, tn=1024, tk=2048, comm=None):
    if dims == "tn":
        K, M = a.shape
    else:
        M, K = a.shape
    N = b.shape[0] if dims == "nt" else b.shape[1]
    tm, tn, tk = _pick(M, tm), _pick(N, tn), _pick(K, tk)
    nk = K // tk
    has_res = residual is not None

    def finish(out, refs):
        if has_res:
            out = out + refs[2][...].astype(F32)
        refs[-2][...] = out.astype(out_dtype)

    def body_single(*refs):
        finish(_dot(refs[0][...].astype(BF16), refs[1][...].astype(BF16), dims), refs)

    def body_multi(*refs):
        acc_ref = refs[-1]
        k = pl.program_id(2)

        @pl.when(k == 0)
        def _():
            acc_ref[...] = jnp.zeros_like(acc_ref)

        acc_ref[...] += _dot(refs[0][...].astype(BF16), refs[1][...].astype(BF16), dims)

        @pl.when(k == nk - 1)
        def _():
            finish(acc_ref[...], refs)

    a_spec = (pl.BlockSpec((tk, tm), lambda i, j, k: (k, i)) if dims == "tn"
              else pl.BlockSpec((tm, tk), lambda i, j, k: (i, k)))
    b_spec = (pl.BlockSpec((tn, tk), lambda i, j, k: (j, k)) if dims == "nt"
              else pl.BlockSpec((tk, tn), lambda i, j, k: (k, j)))
    o_spec = pl.BlockSpec((tm, tn), lambda i, j, k: (i, j))
    in_specs = [a_spec, b_spec] + ([o_spec] if has_res else [])
    args = (a, b) + ((residual,) if has_res else ())
    (out,), extra = _call(
        body_single if nk == 1 else body_multi, name, (M // tm, N // tn, nk), in_specs, [o_spec],
        [jax.ShapeDtypeStruct((M, N), out_dtype)], [pltpu.VMEM((tm, tn) if nk > 1 else (SUBLANES, LANES), F32)],
        ("parallel", "parallel", "arbitrary"), args, comm)
    return out if comm is None else (out, extra)


PIECE_BLOCK = 512


def _piece_blocks(pieces):
    counts = [p.shape[1] // PIECE_BLOCK for p in pieces]
    return counts, [sum(counts[:i]) for i in range(len(counts))]


def _mm_cols_nn(pieces, b, name, comm=None, tm=1024, tn=1024):
    M, N = pieces[0].shape[0], b.shape[1]
    tm, tn = _pick(M, tm), _pick(N, tn)
    counts, starts = _piece_blocks(pieces)
    n, nk = len(pieces), sum(counts)

    def body(*refs):
        b_ref, o_ref, acc_ref = refs[n:]
        k = pl.program_id(2)

        @pl.when(k == 0)
        def _():
            acc_ref[...] = jnp.zeros_like(acc_ref)

        for a_ref, c, s in zip(refs[:n], counts, starts):
            @pl.when((k >= s) & (k < s + c))
            def _(a_ref=a_ref):
                acc_ref[...] += _dot(a_ref[...], b_ref[...])

        @pl.when(k == nk - 1)
        def _():
            o_ref[...] = acc_ref[...]

    a_specs = [pl.BlockSpec((tm, PIECE_BLOCK), functools.partial(lambda s, c, i, j, k: (i, jnp.clip(k - s, 0, c - 1)), s, c))
               for c, s in zip(counts, starts)]
    b_spec = pl.BlockSpec((PIECE_BLOCK, tn), lambda i, j, k: (k, j))
    o_spec = pl.BlockSpec((tm, tn), lambda i, j, k: (i, j))
    (out,), extra = _call(body, name, (M // tm, N // tn, nk), a_specs + [b_spec], [o_spec],
                          [jax.ShapeDtypeStruct((M, N), F32)], [pltpu.VMEM((tm, tn), F32)],
                          ("parallel", "parallel", "arbitrary"), (*pieces, b), comm)
    return out if comm is None else (out, extra)


def _mm_cols_tn(pieces, b, name, tn=1024, tk=2048):
    K, N = b.shape
    tn, tk = _pick(N, tn), _pick(K, tk)
    counts, starts = _piece_blocks(pieces)
    n, nk = len(pieces), K // tk

    def body(*refs):
        b_ref, o_ref, acc_ref = refs[n:]
        i, k = pl.program_id(0), pl.program_id(2)

        @pl.when(k == 0)
        def _():
            acc_ref[...] = jnp.zeros_like(acc_ref)

        for a_ref, c, s in zip(refs[:n], counts, starts):
            @pl.when((i >= s) & (i < s + c))
            def _(a_ref=a_ref):
                acc_ref[...] += _dot(a_ref[...], b_ref[...], "tn")

        @pl.when(k == nk - 1)
        def _():
            o_ref[...] = acc_ref[...].astype(BF16)

    def a_index(s, c, i, j, k):
        mine = (i >= s) & (i < s + c)
        return jnp.where(mine, k, 0), jnp.clip(i - s, 0, c - 1)

    a_specs = [pl.BlockSpec((tk, PIECE_BLOCK), functools.partial(a_index, s, c)) for c, s in zip(counts, starts)]
    b_spec = pl.BlockSpec((tk, tn), lambda i, j, k: (k, j))
    o_spec = pl.BlockSpec((PIECE_BLOCK, tn), lambda i, j, k: (i, j))
    (out,), _ = _call(body, name, (sum(counts), N // tn, nk), a_specs + [b_spec], [o_spec],
                      [jax.ShapeDtypeStruct((sum(counts) * PIECE_BLOCK, N), BF16)],
                      [pltpu.VMEM((PIECE_BLOCK, tn), F32)], ("parallel", "parallel", "arbitrary"), (*pieces, b))
    return out


def _rms_fwd(x, g, name):
    T, D = x.shape
    tr = _pick(T, 256, SUBLANES)

    def body(x_ref, g_ref, o_ref):
        xf = x_ref[...]
        r = lax.rsqrt(jnp.mean(xf * xf, axis=-1, keepdims=True) + RMS_EPS)
        o_ref[...] = (xf * r * g_ref[...]).astype(BF16)

    return pl.pallas_call(
        body, name=name, grid=(T // tr,),
        in_specs=[pl.BlockSpec((tr, D), lambda i: (i, 0)), pl.BlockSpec((1, D), lambda i: (0, 0))],
        out_specs=pl.BlockSpec((tr, D), lambda i: (i, 0)), out_shape=jax.ShapeDtypeStruct((T, D), BF16),
        compiler_params=_params("parallel"))(x, g)


def _rms_bwd(dh, x, g, dres, name):
    T, D = x.shape
    tr = _pick(T, 256, SUBLANES)

    def body(dh_ref, x_ref, g_ref, dres_ref, dx_ref, dg_ref):
        @pl.when(pl.program_id(0) == 0)
        def _():
            dg_ref[...] = jnp.zeros_like(dg_ref)

        xf = x_ref[...]
        r = lax.rsqrt(jnp.mean(xf * xf, axis=-1, keepdims=True) + RMS_EPS)
        xhat = xf * r
        dhv = dh_ref[...]
        dxh = dhv * g_ref[...]
        dx_ref[...] = dres_ref[...] + r * (dxh - xhat * jnp.mean(dxh * xhat, axis=-1, keepdims=True))
        dg_ref[...] += jnp.sum(dhv * xhat, axis=0, keepdims=True)

    row = pl.BlockSpec((tr, D), lambda i: (i, 0))
    vec = pl.BlockSpec((1, D), lambda i: (0, 0))
    return pl.pallas_call(
        body, name=name, grid=(T // tr,), in_specs=[row, row, vec, row], out_specs=[row, vec],
        out_shape=[jax.ShapeDtypeStruct((T, D), F32), jax.ShapeDtypeStruct((1, D), F32)],
        compiler_params=_params("arbitrary"))(dh, x, g, dres)


def _loss_grad(y, target, name):
    T, D = y.shape
    tr = _pick(T, 256, SUBLANES)

    def body(y_ref, t_ref, dx_ref, loss_ref):
        @pl.when(pl.program_id(0) == 0)
        def _():
            loss_ref[...] = jnp.zeros_like(loss_ref)

        err = y_ref[...] - t_ref[...]
        dx_ref[...] = err * (1.0 / D)
        loss_ref[...] += jnp.sum(jnp.mean(err * err, axis=-1, keepdims=True), axis=0, keepdims=True) * 0.5

    row = pl.BlockSpec((tr, D), lambda i: (i, 0))
    one = pl.BlockSpec((1, 1), lambda i: (0, 0))
    dx, loss = pl.pallas_call(
        body, name=name, grid=(T // tr,), in_specs=[row, row], out_specs=[row, one],
        out_shape=[jax.ShapeDtypeStruct((T, D), F32), jax.ShapeDtypeStruct((1, 1), F32)],
        compiler_params=_params("arbitrary"))(y, target)
    return dx, loss[0, 0]


_STACK = GQA_GROUP * BLOCK


def _attn_mask(n):
    row = lax.broadcasted_iota(jnp.int32, (_STACK, 2 * BLOCK), 0)
    col = lax.broadcasted_iota(jnp.int32, (_STACK, 2 * BLOCK), 1)
    dist = (row % BLOCK) - col + BLOCK
    valid = (dist >= 0) & (dist < WINDOW) & ((col >= BLOCK) | (n > 0))
    return dist.astype(F32), valid


def _per_row_head(kv, sk_ref):
    g = lax.broadcasted_iota(jnp.int32, (_STACK, 1), 0) // BLOCK
    head = (kv * GQA_GROUP + g + 1).astype(F32)
    slope = jnp.exp(head * (-8.0 / N_Q_HEADS * math.log(2.0)))
    sink = jnp.concatenate([jnp.broadcast_to(sk_ref[j:j + 1, 0:1], (BLOCK, 1)) for j in range(GQA_GROUP)], axis=0)
    return slope, sink


def _head_norm(v, gain):
    r = lax.rsqrt(jnp.mean(v * v, axis=-1, keepdims=True) + RMS_EPS)
    vhat = v * r
    return r, vhat, vhat * gain


def _attn_probs(qn16, kn16, slope, dist, valid, sink):
    s = _dot(qn16, kn16, "nt") * ATTN_SCALE - slope * dist
    s = jnp.where(valid, s, NEG_BIG)
    m = jnp.maximum(jnp.max(s, axis=-1, keepdims=True), sink)
    p = jnp.exp(s - m)
    ps = jnp.exp(sink - m)
    den = jnp.sum(p, axis=-1, keepdims=True) + ps
    return p, ps, den


def _attn_specs(T):
    q_spec = pl.BlockSpec((GQA_GROUP, BLOCK, HEAD_DIM), lambda h, n: (h, n, 0))
    cur = pl.BlockSpec((None, BLOCK, HEAD_DIM), lambda h, n: (h, n, 0))
    prev = pl.BlockSpec((None, BLOCK, HEAD_DIM), lambda h, n: (h, jnp.maximum(n - 1, 0), 0))
    gain = pl.BlockSpec((1, HEAD_DIM), lambda h, n: (0, 0))
    sink = pl.BlockSpec((None, GQA_GROUP, LANES), lambda h, n: (h, 0, 0))
    return q_spec, cur, prev, gain, sink


def _sink_rows(sinks):
    return jnp.broadcast_to(sinks.reshape(N_KV_HEADS, GQA_GROUP, 1), (N_KV_HEADS, GQA_GROUP, LANES))


def _attn_fwd(q, k, v, qg, kg, sinks, name, comm=None):
    T = q.shape[1]
    nb = T // BLOCK

    def body(q_ref, kc_ref, kp_ref, vc_ref, vp_ref, qg_ref, kg_ref, sk_ref, o_ref):
        dist, valid = _attn_mask(pl.program_id(1))
        slope, sink = _per_row_head(pl.program_id(0), sk_ref)
        kk = jnp.concatenate([kp_ref[...], kc_ref[...]], axis=0)
        _, _, kn = _head_norm(kk, kg_ref[...])
        v16 = jnp.concatenate([vp_ref[...], vc_ref[...]], axis=0).astype(BF16)
        _, _, qn = _head_norm(q_ref[...].reshape(_STACK, HEAD_DIM), qg_ref[...])
        p, _, den = _attn_probs(qn.astype(BF16), kn.astype(BF16), slope, dist, valid, sink)
        o_ref[...] = _dot((p / den).astype(BF16), v16).reshape(GQA_GROUP, BLOCK, HEAD_DIM).astype(BF16)

    q_spec, cur, prev, gain, sink = _attn_specs(T)
    (out,), extra = _call(
        body, name, (N_KV_HEADS, nb), [q_spec, cur, prev, cur, prev, gain, gain, sink], [q_spec],
        [jax.ShapeDtypeStruct((N_Q_HEADS, T, HEAD_DIM), BF16)], [], ("parallel", "parallel"),
        (q, k, k, v, v, qg, kg, _sink_rows(sinks)), comm)
    return out if comm is None else (out, extra)


def _attn_bwd(q, k, v, do, qg, kg, sinks, name, comm=None):
    T = q.shape[1]
    nb = T // BLOCK

    def body(q_ref, kc_ref, kp_ref, vc_ref, vp_ref, do_ref, qg_ref, kg_ref, sk_ref,
             dq_ref, dk_ref, dv_ref, dqg_ref, dkg_ref, dsk_ref):
        n = pl.program_id(1)

        @pl.when(n == 0)
        def _():
            dk_ref[...] = jnp.zeros_like(dk_ref)
            dv_ref[...] = jnp.zeros_like(dv_ref)
            dqg_ref[...] = jnp.zeros_like(dqg_ref)
            dkg_ref[...] = jnp.zeros_like(dkg_ref)
            dsk_ref[...] = jnp.zeros_like(dsk_ref)

        dist, valid = _attn_mask(n)
        slope, sink = _per_row_head(pl.program_id(0), sk_ref)
        kk = jnp.concatenate([kp_ref[...], kc_ref[...]], axis=0)
        rk, khat, kn = _head_norm(kk, kg_ref[...])
        kn16 = kn.astype(BF16)
        v16 = jnp.concatenate([vp_ref[...], vc_ref[...]], axis=0).astype(BF16)
        rq, qhat, qn = _head_norm(q_ref[...].reshape(_STACK, HEAD_DIM), qg_ref[...])
        qn16 = qn.astype(BF16)
        p, ps, den = _attn_probs(qn16, kn16, slope, dist, valid, sink)
        pn = p / den
        do16 = do_ref[...].reshape(_STACK, HEAD_DIM).astype(BF16)
        dp = _dot(do16, v16, "nt")
        delta = jnp.sum(pn * dp, axis=-1, keepdims=True)
        ds16 = (pn * (dp - delta)).astype(BF16)
        sink_pull = ps / den * delta
        for g in range(GQA_GROUP):
            dsink = jnp.sum(sink_pull[g * BLOCK:(g + 1) * BLOCK], axis=0, keepdims=True)
            dsk_ref[g:g + 1, :] -= jnp.broadcast_to(dsink, (1, LANES))
        dqn = _dot(ds16, kn16) * ATTN_SCALE
        dkn = _dot(ds16, qn16, "tn") * ATTN_SCALE
        dv = _dot(pn.astype(BF16), do16, "tn")
        dqh = dqn * qg_ref[...]
        dq = rq * (dqh - qhat * jnp.mean(dqh * qhat, axis=-1, keepdims=True))
        dq_ref[...] = dq.reshape(GQA_GROUP, BLOCK, HEAD_DIM).astype(BF16)
        dqg_ref[...] += jnp.sum(dqn * qhat, axis=0, keepdims=True)
        dkg_ref[...] += jnp.sum(dkn * khat, axis=0, keepdims=True)
        dkh = dkn * kg_ref[...]
        dk = rk * (dkh - khat * jnp.mean(dkh * khat, axis=-1, keepdims=True))
        rows = pl.ds(pl.multiple_of(n * BLOCK, BLOCK), BLOCK)
        before = pl.ds(pl.multiple_of(jnp.maximum(n - 1, 0) * BLOCK, BLOCK), BLOCK)
        dk_ref[rows, :] += dk[BLOCK:]
        dv_ref[rows, :] += dv[BLOCK:]
        dk_ref[before, :] += dk[:BLOCK]
        dv_ref[before, :] += dv[:BLOCK]

    q_spec, cur, prev, gain, sink = _attn_specs(T)
    whole = pl.BlockSpec((None, T, HEAD_DIM), lambda h, n: (h, 0, 0))
    gacc = pl.BlockSpec((None, 1, HEAD_DIM), lambda h, n: (h, 0, 0))
    outs, extra = _call(
        body, name, (N_KV_HEADS, nb), [q_spec, cur, prev, cur, prev, q_spec, gain, gain, sink],
        [q_spec, whole, whole, gacc, gacc, sink],
        [jax.ShapeDtypeStruct((N_Q_HEADS, T, HEAD_DIM), BF16),
         jax.ShapeDtypeStruct((N_KV_HEADS, T, HEAD_DIM), F32),
         jax.ShapeDtypeStruct((N_KV_HEADS, T, HEAD_DIM), F32),
         jax.ShapeDtypeStruct((N_KV_HEADS, 1, HEAD_DIM), F32),
         jax.ShapeDtypeStruct((N_KV_HEADS, 1, HEAD_DIM), F32),
         jax.ShapeDtypeStruct((N_KV_HEADS, GQA_GROUP, LANES), F32)],
        [], ("parallel", "arbitrary"), (q, k, k, v, v, do, qg, kg, _sink_rows(sinks)), comm)
    return outs if comm is None else (outs, extra)


def _ssm_discretize(lam_re, lam_im, log_dt, b_re, b_im, c_re, c_im):
    dt = jnp.exp(log_dt)[:, None]
    mag = jnp.exp(lam_re * dt)
    ar = mag * jnp.cos(lam_im * dt)
    ai = mag * jnp.sin(lam_im * dt)
    den = lam_re * lam_re + lam_im * lam_im
    fr = ((ar - 1.0) * lam_re + ai * lam_im) / den
    fi = (ai * lam_re - (ar - 1.0) * lam_im) / den
    bbar_r = fr[:, :, None] * b_re - fi[:, :, None] * b_im
    bbar_i = fr[:, :, None] * b_im + fi[:, :, None] * b_re
    gl = SSM_GROUPS // SSM_NGB
    eye = jnp.eye(gl, dtype=F32)

    def tiles(a):
        return a.reshape(SSM_NGB, SUBLANES, LANES)

    def bdiag(bb):
        return jnp.einsum("bgph,gk->bghkp", bb.reshape(SSM_NGB, gl, SSM_STATE, SSM_GROUP_CH), eye).reshape(
            SSM_NGB, SSM_GB_CH, SSM_GB_ST)

    def cdiag(cc):
        return jnp.einsum("bghp,gk->bgpkh", cc.reshape(SSM_NGB, gl, SSM_GROUP_CH, SSM_STATE), eye).reshape(
            SSM_NGB, SSM_GB_ST, SSM_GB_CH)

    return tiles(ar), tiles(ai), bdiag(bbar_r), bdiag(bbar_i), cdiag(c_re), cdiag(c_im)


def _to_time_major(dst, val, tt, first_row=0):
    for j in range(SUBLANES):
        dst[pl.ds(first_row + j, tt, stride=SUBLANES), :] = val[:, j * LANES:(j + 1) * LANES]


def _from_time_major(dst, src, tt):
    for j in range(SUBLANES):
        dst[:, j * LANES:(j + 1) * LANES] = src[pl.ds(j, tt, stride=SUBLANES), :]


def _ssm_fwd(z, ar, ai, bbr, bbi, cbr, cbi, dskip, name, comm=None):
    T = z.shape[0]
    tt = min(SSM_TT, T)
    nt = T // tt

    def body(u_ref, ar_ref, ai_ref, br_ref, bi_ref, cr_ref, ci_ref, d_ref, y_ref, sr_ref, si_ref,
             tmr, tmi, car_r, car_i):
        @pl.when(pl.program_id(1) == 0)
        def _():
            car_r[...] = jnp.zeros_like(car_r)
            car_i[...] = jnp.zeros_like(car_i)

        u = u_ref[...]
        u16 = u.astype(BF16)
        _to_time_major(tmr, _dot(u16, br_ref[...]), tt)
        _to_time_major(tmi, _dot(u16, bi_ref[...]), tt)
        a_r = ar_ref[...]
        a_i = ai_ref[...]

        def step(t, carry):
            s_r, s_i = carry
            rows = pl.ds(pl.multiple_of(t * SUBLANES, SUBLANES), SUBLANES)
            n_r = a_r * s_r - a_i * s_i + tmr[rows, :]
            n_i = a_r * s_i + a_i * s_r + tmi[rows, :]
            tmr[rows, :] = n_r
            tmi[rows, :] = n_i
            return n_r, n_i

        s_r, s_i = lax.fori_loop(0, tt, step, (car_r[...], car_i[...]), unroll=8)
        car_r[...] = s_r
        car_i[...] = s_i
        _from_time_major(sr_ref, tmr, tt)
        _from_time_major(si_ref, tmi, tt)
        y_ref[...] = (_dot(sr_ref[...].astype(BF16), cr_ref[...]) - _dot(si_ref[...].astype(BF16), ci_ref[...])
                      + d_ref[...] * u)

    u_spec = pl.BlockSpec((tt, SSM_GB_CH), lambda b, t: (t, OFF_U // SSM_GB_CH + b))
    a_spec = pl.BlockSpec((None, SUBLANES, LANES), lambda b, t: (b, 0, 0))
    b_spec = pl.BlockSpec((None, SSM_GB_CH, SSM_GB_ST), lambda b, t: (b, 0, 0))
    c_spec = pl.BlockSpec((None, SSM_GB_ST, SSM_GB_CH), lambda b, t: (b, 0, 0))
    d_spec = pl.BlockSpec((1, SSM_GB_CH), lambda b, t: (0, b))
    y_spec = pl.BlockSpec((tt, SSM_GB_CH), lambda b, t: (t, b))
    s_spec = pl.BlockSpec((tt, SSM_GB_ST), lambda b, t: (t, b))
    n_state = SSM_NGB * SSM_GB_ST
    outs, extra = _call(
        body, name, (SSM_NGB, nt), [u_spec, a_spec, a_spec, b_spec, b_spec, c_spec, c_spec, d_spec],
        [y_spec, s_spec, s_spec],
        [jax.ShapeDtypeStruct((T, SSM_WIDTH), F32), jax.ShapeDtypeStruct((T, n_state), F32),
         jax.ShapeDtypeStruct((T, n_state), F32)],
        [pltpu.VMEM((tt * SUBLANES, LANES), F32), pltpu.VMEM((tt * SUBLANES, LANES), F32),
         pltpu.VMEM((SUBLANES, LANES), F32), pltpu.VMEM((SUBLANES, LANES), F32)],
        ("parallel", "arbitrary"), (z, ar, ai, bbr, bbi, cbr, cbi, dskip), comm)
    return outs if comm is None else (outs, extra)


def _ssm_bwd(dy, z, s_r, s_i, ar, ai, bbr, bbi, cbr, cbi, dskip, name, comm=None):
    T = z.shape[0]
    tt = min(SSM_TT, T)
    nt = T // tt
    per8 = tt // SUBLANES

    def body(dy_ref, u_ref, sr_ref, si_ref, srp_ref, sip_ref, ar_ref, ai_ref, br_ref, bi_ref, cr_ref, ci_ref, d_ref,
             du_ref, dar_ref, dai_ref, dbr_ref, dbi_ref, dcr_ref, dci_ref, dd_ref,
             tmr, tmi, smr, smi, natr, nati, car_r, car_i):
        tb = pl.program_id(1)
        first_block = tb == nt - 1

        @pl.when(tb == 0)
        def _():
            for ref in (car_r, car_i, dar_ref, dai_ref, dbr_ref, dbi_ref, dcr_ref, dci_ref, dd_ref):
                ref[...] = jnp.zeros_like(ref)

        dy = dy_ref[...]
        dy16 = dy.astype(BF16)
        u = u_ref[...]
        u16 = u.astype(BF16)
        _to_time_major(tmr, _dot(dy16, cr_ref[...], "nt"), tt)
        _to_time_major(tmi, -_dot(dy16, ci_ref[...], "nt"), tt)
        _to_time_major(smr, sr_ref[...], tt, first_row=SUBLANES)
        _to_time_major(smi, si_ref[...], tt, first_row=SUBLANES)
        keep = jnp.where(first_block, 0.0, 1.0)
        for j in range(SUBLANES):
            smr[j:j + 1, :] = srp_ref[SUBLANES - 1:SUBLANES, j * LANES:(j + 1) * LANES] * keep
            smi[j:j + 1, :] = sip_ref[SUBLANES - 1:SUBLANES, j * LANES:(j + 1) * LANES] * keep
        a_r = ar_ref[...]
        a_i = ai_ref[...]

        def step(i, carry):
            n_r, n_i, da_r, da_i = carry
            rows = pl.ds(pl.multiple_of((tt - 1 - i) * SUBLANES, SUBLANES), SUBLANES)
            g_r = tmr[rows, :] + a_r * n_r + a_i * n_i
            g_i = tmi[rows, :] - a_i * n_r + a_r * n_i
            tmr[rows, :] = g_r
            tmi[rows, :] = g_i
            p_r = smr[rows, :]
            p_i = smi[rows, :]
            return g_r, g_i, da_r + g_r * p_r + g_i * p_i, da_i - g_r * p_i + g_i * p_r

        zero = jnp.zeros((SUBLANES, LANES), F32)
        n_r, n_i, da_r, da_i = lax.fori_loop(0, tt, step, (car_r[...], car_i[...], zero, zero), unroll=8)
        car_r[...] = n_r
        car_i[...] = n_i
        dar_ref[...] += da_r
        dai_ref[...] += da_i
        _from_time_major(natr, tmr, tt)
        _from_time_major(nati, tmi, tt)
        dbu_r16 = natr[...].astype(BF16)
        dbu_i16 = nati[...].astype(BF16)
        du_ref[...] = (_dot(dbu_r16, br_ref[...], "nt") + _dot(dbu_i16, bi_ref[...], "nt")
                       + d_ref[...] * dy).astype(BF16)
        dbr_ref[...] += _dot(u16, dbu_r16, "tn")
        dbi_ref[...] += _dot(u16, dbu_i16, "tn")
        dcr_ref[...] += _dot(sr_ref[...].astype(BF16), dy16, "tn")
        dci_ref[...] -= _dot(si_ref[...].astype(BF16), dy16, "tn")
        dd_ref[...] += jnp.sum(dy * u, axis=0, keepdims=True)

    def rev(t):
        return nt - 1 - t

    dy_spec = pl.BlockSpec((tt, SSM_GB_CH), lambda b, t: (rev(t), b))
    u_spec = pl.BlockSpec((tt, SSM_GB_CH), lambda b, t: (rev(t), OFF_U // SSM_GB_CH + b))
    s_spec = pl.BlockSpec((tt, SSM_GB_ST), lambda b, t: (rev(t), b))
    sp_spec = pl.BlockSpec((SUBLANES, SSM_GB_ST), lambda b, t: (jnp.maximum(rev(t) * per8 - 1, 0), b))
    a_spec = pl.BlockSpec((None, SUBLANES, LANES), lambda b, t: (b, 0, 0))
    b_spec = pl.BlockSpec((None, SSM_GB_CH, SSM_GB_ST), lambda b, t: (b, 0, 0))
    c_spec = pl.BlockSpec((None, SSM_GB_ST, SSM_GB_CH), lambda b, t: (b, 0, 0))
    d_spec = pl.BlockSpec((1, SSM_GB_CH), lambda b, t: (0, b))
    tm_shape = pltpu.VMEM((tt * SUBLANES, LANES), F32)
    sm_shape = pltpu.VMEM(((tt + 1) * SUBLANES, LANES), F32)
    nat_shape = pltpu.VMEM((tt, SSM_GB_ST), F32)
    tile = pltpu.VMEM((SUBLANES, LANES), F32)
    outs, extra = _call(
        body, name, (SSM_NGB, nt),
        [dy_spec, u_spec, s_spec, s_spec, sp_spec, sp_spec, a_spec, a_spec, b_spec, b_spec, c_spec, c_spec, d_spec],
        [dy_spec, a_spec, a_spec, b_spec, b_spec, c_spec, c_spec, d_spec],
        [jax.ShapeDtypeStruct((T, SSM_WIDTH), BF16),
         jax.ShapeDtypeStruct((SSM_NGB, SUBLANES, LANES), F32),
         jax.ShapeDtypeStruct((SSM_NGB, SUBLANES, LANES), F32),
         jax.ShapeDtypeStruct((SSM_NGB, SSM_GB_CH, SSM_GB_ST), F32),
         jax.ShapeDtypeStruct((SSM_NGB, SSM_GB_CH, SSM_GB_ST), F32),
         jax.ShapeDtypeStruct((SSM_NGB, SSM_GB_ST, SSM_GB_CH), F32),
         jax.ShapeDtypeStruct((SSM_NGB, SSM_GB_ST, SSM_GB_CH), F32),
         jax.ShapeDtypeStruct((1, SSM_WIDTH), F32)],
        [tm_shape, tm_shape, sm_shape, sm_shape, nat_shape, nat_shape, tile, tile], ("parallel", "arbitrary"),
        (dy, z, s_r, s_i, s_r, s_i, ar, ai, bbr, bbi, cbr, cbi, dskip), comm)
    return outs if comm is None else (outs, extra)


def _glu_fwd(y, w, b, name):
    T, W = y.shape
    tm = _pick(T, 512)

    def body(y_ref, w_ref, b_ref, pre_ref, y3_ref):
        y2 = _gelu(y_ref[...])
        pre = _dot(y2.astype(BF16), w_ref[...]) + b_ref[...]
        pre_ref[...] = pre
        y3_ref[...] = (y2 * _sigmoid(pre)).astype(BF16)

    row = pl.BlockSpec((tm, W), lambda i: (i, 0))
    return pl.pallas_call(
        body, name=name, grid=(T // tm,),
        in_specs=[row, pl.BlockSpec((W, W), lambda i: (0, 0)), pl.BlockSpec((1, W), lambda i: (0, 0))],
        out_specs=[row, row], out_shape=[jax.ShapeDtypeStruct((T, W), F32), jax.ShapeDtypeStruct((T, W), BF16)],
        compiler_params=_params("parallel"))(y, w, b)


def _glu_bwd_gate(dy3, y, pre, name):
    T, W = y.shape
    tm = _pick(T, 512)

    def body(dy3_ref, y_ref, pre_ref, dpre_ref, t1_ref, y2_ref, db_ref):
        @pl.when(pl.program_id(0) == 0)
        def _():
            db_ref[...] = jnp.zeros_like(db_ref)

        y2 = _gelu(y_ref[...])
        sg = _sigmoid(pre_ref[...])
        dy3 = dy3_ref[...]
        dpre = dy3 * y2 * sg * (1.0 - sg)
        dpre_ref[...] = dpre.astype(BF16)
        t1_ref[...] = dy3 * sg
        y2_ref[...] = y2.astype(BF16)
        db_ref[...] += jnp.sum(dpre, axis=0, keepdims=True)

    row = pl.BlockSpec((tm, W), lambda i: (i, 0))
    vec = pl.BlockSpec((1, W), lambda i: (0, 0))
    return pl.pallas_call(
        body, name=name, grid=(T // tm,), in_specs=[row, row, row], out_specs=[row, row, row, vec],
        out_shape=[jax.ShapeDtypeStruct((T, W), BF16), jax.ShapeDtypeStruct((T, W), F32),
                   jax.ShapeDtypeStruct((T, W), BF16), jax.ShapeDtypeStruct((1, W), F32)],
        compiler_params=_params("arbitrary"))(dy3, y, pre)


def _glu_bwd_in(dpre, w, t1, y, name):
    T, W = y.shape
    tm = _pick(T, 512)

    def body(dpre_ref, w_ref, t1_ref, y_ref, dy_ref):
        dy_ref[...] = (_dot(dpre_ref[...], w_ref[...], "nt") + t1_ref[...]) * _gelu_grad(y_ref[...])

    row = pl.BlockSpec((tm, W), lambda i: (i, 0))
    return pl.pallas_call(
        body, name=name, grid=(T // tm,), in_specs=[row, pl.BlockSpec((W, W), lambda i: (0, 0)), row, row],
        out_specs=row, out_shape=jax.ShapeDtypeStruct((T, W), F32),
        compiler_params=_params("parallel"))(dpre, w, t1, y)


def _merge_fwd(ya, y3, wa, ws, z, bias, name, comm=None):
    T, W = ya.shape
    D = wa.shape[1]
    tm, tn = _pick(T, 512), _pick(D, 512)

    def body(ya_ref, y3_ref, wa_ref, ws_ref, za_ref, zs_ref, ba_ref, bs_ref, a_ref, b_ref, m_ref):
        a = _dot(ya_ref[...], wa_ref[...])
        b = _dot(y3_ref[...], ws_ref[...])
        a_ref[...] = a
        b_ref[...] = b
        m_ref[...] = (_sigmoid(za_ref[...] + ba_ref[...]) * a + _sigmoid(zs_ref[...] + bs_ref[...]) * b).astype(BF16)

    act = pl.BlockSpec((tm, W), lambda i, j: (i, 0))
    wgt = pl.BlockSpec((W, tn), lambda i, j: (0, j))
    za = pl.BlockSpec((tm, tn), lambda i, j: (i, OFF_G // tn + j))
    zs = pl.BlockSpec((tm, tn), lambda i, j: (i, (OFF_G + D) // tn + j))
    ba = pl.BlockSpec((1, tn), lambda i, j: (0, j))
    bs = pl.BlockSpec((1, tn), lambda i, j: (0, D // tn + j))
    out = pl.BlockSpec((tm, tn), lambda i, j: (i, j))
    outs, extra = _call(
        body, name, (T // tm, D // tn), [act, act, wgt, wgt, za, zs, ba, bs], [out, out, out],
        [jax.ShapeDtypeStruct((T, D), F32), jax.ShapeDtypeStruct((T, D), F32), jax.ShapeDtypeStruct((T, D), BF16)],
        [], ("parallel", "parallel"), (ya, y3, wa, ws, z, z, bias, bias), comm)
    return outs if comm is None else (outs, extra)


def _merge_bwd(dm, a, b, z, bias, name):
    T, D = dm.shape
    tm, tn = _pick(T, 512), _pick(D, 512)

    def body(dm_ref, a_ref, b_ref, za_ref, zs_ref, ba_ref, bs_ref, da_ref, db_ref, dza_ref, dzs_ref, dba_ref, dbs_ref):
        @pl.when(pl.program_id(1) == 0)
        def _():
            dba_ref[...] = jnp.zeros_like(dba_ref)
            dbs_ref[...] = jnp.zeros_like(dbs_ref)

        dm = dm_ref[...]
        sa = _sigmoid(za_ref[...] + ba_ref[...])
        ss = _sigmoid(zs_ref[...] + bs_ref[...])
        da_ref[...] = (dm * sa).astype(BF16)
        db_ref[...] = (dm * ss).astype(BF16)
        dza = dm * a_ref[...] * sa * (1.0 - sa)
        dzs = dm * b_ref[...] * ss * (1.0 - ss)
        dza_ref[...] = dza.astype(BF16)
        dzs_ref[...] = dzs.astype(BF16)
        dba_ref[...] += jnp.sum(dza, axis=0, keepdims=True)
        dbs_ref[...] += jnp.sum(dzs, axis=0, keepdims=True)

    blk = pl.BlockSpec((tm, tn), lambda j, i: (i, j))
    za = pl.BlockSpec((tm, tn), lambda j, i: (i, OFF_G // tn + j))
    zs = pl.BlockSpec((tm, tn), lambda j, i: (i, (OFF_G + D) // tn + j))
    ba = pl.BlockSpec((1, tn), lambda j, i: (0, j))
    bs = pl.BlockSpec((1, tn), lambda j, i: (0, D // tn + j))
    big = jax.ShapeDtypeStruct((T, D), BF16)
    vec = jax.ShapeDtypeStruct((1, D), F32)
    return pl.pallas_call(
        body, name=name, grid=(D // tn, T // tm), in_specs=[blk, blk, blk, za, zs, ba, bs],
        out_specs=[blk, blk, blk, blk, ba, ba], out_shape=[big, big, big, big, vec, vec],
        compiler_params=_params("parallel", "arbitrary"))(dm, a, b, z, z, bias, bias)


_HALF = N_DEV // 2


def _wgu_block(d):
    return d // 2, d % 2


def _ffn_fwd(h, wgu, name, comm=None):
    T, D = h.shape
    n = wgu.shape[3]
    F = _HALF * n
    tm = _pick(T, 512)

    def body(h_ref, wg_ref, wu_ref, g_ref, u_ref, act_ref):
        hv = h_ref[...]
        g = _dot(hv, wg_ref[...])
        u = _dot(hv, wu_ref[...])
        g_ref[...] = g
        u_ref[...] = u
        act_ref[...] = (g * _sigmoid(g) * u).astype(BF16)

    wg = pl.BlockSpec((None, None, D, n), lambda j, i: (*_wgu_block(j), 0, 0))
    wu = pl.BlockSpec((None, None, D, n), lambda j, i: (*_wgu_block(j + _HALF), 0, 0))
    out = pl.BlockSpec((tm, n), lambda j, i: (i, j))
    outs, extra = _call(
        body, name, (_HALF, T // tm), [pl.BlockSpec((tm, D), lambda j, i: (i, 0)), wg, wu], [out, out, out],
        [jax.ShapeDtypeStruct((T, F), F32), jax.ShapeDtypeStruct((T, F), F32), jax.ShapeDtypeStruct((T, F), BF16)],
        [], ("parallel", "parallel"), (h, wgu, wgu), comm)
    return outs if comm is None else (outs, extra)


def _ffn_bwd_in(dg, du, wgu, name, comm=None):
    T, F = dg.shape
    D, n = wgu.shape[2], wgu.shape[3]
    tm, tn = _pick(T, 1024), _pick(D, 1024)

    def body(dg_ref, du_ref, w_ref, o_ref, acc_ref):
        k = pl.program_id(2)

        @pl.when(k == 0)
        def _():
            acc_ref[...] = jnp.zeros_like(acc_ref)

        @pl.when(k < _HALF)
        def _():
            acc_ref[...] += _dot(dg_ref[...], w_ref[...], "nt")

        @pl.when(k >= _HALF)
        def _():
            acc_ref[...] += _dot(du_ref[...], w_ref[...], "nt")

        @pl.when(k == N_DEV - 1)
        def _():
            o_ref[...] = acc_ref[...]

    dg_spec = pl.BlockSpec((tm, n), lambda i, j, k: (i, jnp.minimum(k, _HALF - 1)))
    du_spec = pl.BlockSpec((tm, n), lambda i, j, k: (i, jnp.maximum(k - _HALF, 0)))
    w_spec = pl.BlockSpec((None, None, tn, n), lambda i, j, k: (*_wgu_block(k), j, 0))
    o_spec = pl.BlockSpec((tm, tn), lambda i, j, k: (i, j))
    (out,), extra = _call(
        body, name, (T // tm, D // tn, N_DEV), [dg_spec, du_spec, w_spec], [o_spec],
        [jax.ShapeDtypeStruct((T, D), F32)], [pltpu.VMEM((tm, tn), F32)], ("parallel", "parallel", "arbitrary"),
        (dg, du, wgu), comm)
    return out if comm is None else (out, extra)


def _ffn_bwd_w(h, dg, du, name, comm=None):
    T, D = h.shape
    n = dg.shape[1] // _HALF
    tm, tk = _pick(D, 1024), _pick(T, 2048)
    nk = T // tk

    def body(h_ref, dg_ref, du_ref, o_ref, acc_ref):
        j, k = pl.program_id(0), pl.program_id(2)

        @pl.when(k == 0)
        def _():
            acc_ref[...] = jnp.zeros_like(acc_ref)

        @pl.when(j < _HALF)
        def _():
            acc_ref[...] += _dot(h_ref[...], dg_ref[...], "tn")

        @pl.when(j >= _HALF)
        def _():
            acc_ref[...] += _dot(h_ref[...], du_ref[...], "tn")

        @pl.when(k == nk - 1)
        def _():
            o_ref[...] = acc_ref[...].astype(BF16)

    h_spec = pl.BlockSpec((tk, tm), lambda j, i, k: (k, i))
    dg_spec = pl.BlockSpec((tk, n), lambda j, i, k: (jnp.where(j < _HALF, k, nk - 1), jnp.minimum(j, _HALF - 1)))
    du_spec = pl.BlockSpec((tk, n), lambda j, i, k: (jnp.where(j >= _HALF, k, 0), jnp.maximum(j - _HALF, 0)))
    o_spec = pl.BlockSpec((None, None, tm, n), lambda j, i, k: (*_wgu_block(j), i, 0))
    (out,), extra = _call(
        body, name, (N_DEV, D // tm, nk), [h_spec, dg_spec, du_spec], [o_spec],
        [jax.ShapeDtypeStruct((_HALF, 2, D, n), BF16)], [pltpu.VMEM((tm, n), F32)],
        ("parallel", "parallel", "arbitrary"), (h, dg, du), comm)
    return out if comm is None else (out, extra)


def _ffn_bwd_act(dx, wo, g, u, name, comm=None):
    T, D = dx.shape
    F = wo.shape[0]
    tm, tn = _pick(T, 1024), _pick(F, 512)

    def body(dx_ref, wo_ref, g_ref, u_ref, dg_ref, du_ref):
        dact = _dot(dx_ref[...].astype(BF16), wo_ref[...], "nt")
        gv = g_ref[...]
        sg = _sigmoid(gv)
        dg_ref[...] = (dact * u_ref[...] * sg * (1.0 + gv * (1.0 - sg))).astype(BF16)
        du_ref[...] = (dact * gv * sg).astype(BF16)

    out = pl.BlockSpec((tm, tn), lambda i, j: (i, j))
    big = jax.ShapeDtypeStruct((T, F), BF16)
    outs, extra = _call(
        body, name, (T // tm, F // tn),
        [pl.BlockSpec((tm, D), lambda i, j: (i, 0)), pl.BlockSpec((tn, D), lambda i, j: (j, 0)), out, out],
        [out, out], [big, big], [], ("parallel", "parallel"), (dx, wo, g, u), comm)
    return outs if comm is None else (outs, extra)


def _place():
    x, y, c = lax.axis_index("x"), lax.axis_index("y"), lax.axis_index("c")
    other_chips = [(1 - x, y), (x, 1 - y), (1 - x, 1 - y)]
    return x, y, c, 2 * x + y, other_chips


_ANY = pl.BlockSpec(memory_space=pl.ANY)
_N_COPIES = 7


def _all_gather(shards, name):
    n = len(shards)

    def body(*refs):
        ins, outs = refs[:n], refs[n:2 * n]
        send_sems, recv_sems, local_sems = refs[2 * n:]
        x, y, c, chip, other_chips = _place()
        sibling = (x, y, 1 - c)

        def remote(src, dst, a, j, dev):
            return pltpu.make_async_remote_copy(src_ref=src, dst_ref=dst, send_sem=send_sems.at[a, j],
                                                recv_sem=recv_sems.at[a, j], device_id=dev, device_id_type=MESH)

        sends, local = [], []
        for a in range(n):
            mine = outs[a].at[chip, c]
            local.append(pltpu.make_async_copy(ins[a], mine, local_sems.at[a]))
            local[a].start()
            for j, (ox, oy) in enumerate(other_chips):
                sends.append(remote(ins[a], mine, a, 1 + j, (ox, oy, c)))
                sends[-1].start()
            sends.append(remote(ins[a], mine, a, 0, sibling))
            sends[-1].start()
        for a in range(n):
            for j, (ox, oy) in enumerate(other_chips):
                slot = outs[a].at[2 * ox + oy, c]
                remote(ins[a], slot, a, 1 + j, (ox, oy, c)).wait_recv()
                sends.append(remote(slot, slot, a, 4 + j, sibling))
                sends[-1].start()
        for a in range(n):
            remote(ins[a], outs[a].at[chip, 1 - c], a, 0, sibling).wait_recv()
            for j, (ox, oy) in enumerate(other_chips):
                remote(ins[a], outs[a].at[2 * ox + oy, 1 - c], a, 4 + j, sibling).wait_recv()
        for cp in sends:
            cp.wait_send()
        for a in range(n):
            local[a].wait()

    return pl.pallas_call(
        body, name=name, in_specs=[_ANY] * n, out_specs=[_ANY] * n,
        out_shape=[jax.ShapeDtypeStruct((4, 2) + s.shape, s.dtype) for s in shards],
        scratch_shapes=[pltpu.SemaphoreType.DMA((n, _N_COPIES)), pltpu.SemaphoreType.DMA((n, _N_COPIES)),
                        pltpu.SemaphoreType.DMA((n,))])(*shards)


def _pair_exchange(parts, name):
    n = len(parts)

    def body(*refs):
        ins, theirs = refs[:n], refs[n:2 * n]
        send_sems, recv_sems = refs[2 * n:]
        x, y, c, _, _ = _place()
        sends = []
        for a in range(n):
            for k in range(4):
                sends.append(pltpu.make_async_remote_copy(
                    src_ref=ins[a].at[k, 1 - c], dst_ref=theirs[a].at[k], send_sem=send_sems.at[a, k],
                    recv_sem=recv_sems.at[a, k], device_id=(x, y, 1 - c), device_id_type=MESH))
                sends[-1].start()
        for cp in sends:
            cp.wait_recv()
            cp.wait_send()

    return pl.pallas_call(
        body, name=name, in_specs=[_ANY] * n, out_specs=[_ANY] * n,
        out_shape=[jax.ShapeDtypeStruct((4,) + p.shape[2:], p.dtype) for p in parts],
        scratch_shapes=[pltpu.SemaphoreType.DMA((n, 4)), pltpu.SemaphoreType.DMA((n, 4))])(*parts)


def _chip_exchange(parts, name):
    n = len(parts)

    def body(*refs):
        ins, got = refs[:n], refs[n:2 * n]
        send_sems, recv_sems = refs[2 * n:]
        _, _, c, _, other_chips = _place()
        sends = []
        for a in range(n):
            for j, (ox, oy) in enumerate(other_chips):
                sends.append(pltpu.make_async_remote_copy(
                    src_ref=ins[a].at[2 * ox + oy], dst_ref=got[a].at[j], send_sem=send_sems.at[a, j],
                    recv_sem=recv_sems.at[a, j], device_id=(ox, oy, c), device_id_type=MESH))
                sends[-1].start()
        for cp in sends:
            cp.wait_recv()
            cp.wait_send()

    return pl.pallas_call(
        body, name=name, in_specs=[_ANY] * n, out_specs=[_ANY] * n,
        out_shape=[jax.ShapeDtypeStruct((3,) + p.shape[1:], p.dtype) for p in parts],
        scratch_shapes=[pltpu.SemaphoreType.DMA((n, 3)), pltpu.SemaphoreType.DMA((n, 3))])(*parts)


def _remote(src, dst, send_sems, recv_sems, a, j, dev):
    return pltpu.make_async_remote_copy(src_ref=src, dst_ref=dst, send_sem=send_sems.at[a, j],
                                        recv_sem=recv_sems.at[a, j], device_id=dev, device_id_type=MESH)


def _gather_send(shards):
    n = len(shards)

    def copies(cin, cout, sems, arriving):
        send_sems, recv_sems, local_sems = sems
        x, y, c, chip, other_chips = _place()
        sibling = (x, y, 1 - c)
        peers = [(0, sibling, (chip, 1 - c))] + [(1 + j, (ox, oy, c), (2 * ox + oy, c))
                                                 for j, (ox, oy) in enumerate(other_chips)]
        sends, recvs, local = [], [], []
        for a in range(n):
            mine = cout[a].at[chip, c]
            local.append(pltpu.make_async_copy(cin[a], mine, local_sems.at[a]))
            for j, dev, slot in peers:
                sends.append(_remote(cin[a], mine, send_sems, recv_sems, a, j, dev))
                if arriving:
                    recvs.append(_remote(cin[a], cout[a].at[slot], send_sems, recv_sems, a, j, dev))
        return sends, recvs, local

    return _Comm(shards, [jax.ShapeDtypeStruct((4, 2) + s.shape, s.dtype) for s in shards],
                 [pltpu.SemaphoreType.DMA((n, 4)), pltpu.SemaphoreType.DMA((n, 4)), pltpu.SemaphoreType.DMA((n,))],
                 copies)


def _gather_pass(gathered):
    n = len(gathered)

    def copies(cin, cout, sems, arriving):
        send_sems, recv_sems = sems
        x, y, c, _, other_chips = _place()
        sibling = (x, y, 1 - c)
        sends, recvs = [], []
        for a in range(n):
            for j, (ox, oy) in enumerate(other_chips):
                k = 2 * ox + oy
                sends.append(_remote(cout[a].at[k, c], cout[a].at[k, c], send_sems, recv_sems, a, j, sibling))
                if arriving:
                    recvs.append(_remote(cout[a].at[k, c], cout[a].at[k, 1 - c], send_sems, recv_sems, a, j, sibling))
        return sends, recvs, []

    return _Comm(gathered, [jax.ShapeDtypeStruct(g.shape, g.dtype) for g in gathered],
                 [pltpu.SemaphoreType.DMA((n, 3)), pltpu.SemaphoreType.DMA((n, 3))], copies,
                 aliases={i: i for i in range(n)})


def _scatter_send(sums):
    n = len(sums)

    def copies(cin, cout, sems, arriving):
        send_sems, recv_sems = sems
        _, _, c, _, other_chips = _place()
        sends = [_remote(cin[a].at[2 * ox + oy], cout[a].at[j], send_sems, recv_sems, a, j, (ox, oy, c))
                 for a in range(n) for j, (ox, oy) in enumerate(other_chips)]
        return sends, sends, []

    return _Comm(sums, [jax.ShapeDtypeStruct((3,) + s.shape[1:], s.dtype) for s in sums],
                 [pltpu.SemaphoreType.DMA((n, 3)), pltpu.SemaphoreType.DMA((n, 3))], copies)


def _join(comms):
    if len(comms) == 1:
        return comms[0]
    args, outs, sems, aliases, spans = [], [], [], {}, []
    for cm in comms:
        spans.append((len(args), len(outs), len(sems)))
        aliases.update({len(args) + i: len(outs) + o for i, o in cm.aliases.items()})
        args, outs, sems = args + cm.args, outs + cm.out_shapes, sems + cm.sems

    def copies(cin, cout, sem_refs, arriving):
        sends, recvs, local = [], [], []
        for cm, (a0, o0, s0) in zip(comms, spans):
            part = cm.copies(cin[a0:a0 + len(cm.args)], cout[o0:o0 + len(cm.out_shapes)],
                             sem_refs[s0:s0 + len(cm.sems)], arriving)
            sends, recvs, local = sends + part[0], recvs + part[1], local + part[2]
        return sends, recvs, local

    return _Comm(args, outs, sems, copies, aliases)


class _Schedule:
    def __init__(self):
        self.rides = {}

    def ride(self, host, make, names, operands):
        results = {}
        self.rides.setdefault(host, []).append((make, names, operands, results))
        return results

    def carry(self, host, fn, *args, **kwargs):
        rides = self.rides.get(host)
        if not rides:
            return fn(*args, host, **kwargs)
        comm = _join([make([operands[n] for n in names]) for make, names, operands, _ in rides])
        out, extra = fn(*args, host, comm=comm, **kwargs)
        for _, names, _, results in rides:
            results.update(zip(names, extra[:len(names)]))
            extra = extra[len(names):]
        return out


def _add_pair(core, parts, theirs, name):
    k, _, R, C = parts.shape
    tr = _pick(R, 512, 16)

    def body(core_ref, a_ref, b_ref, o_ref):
        o_ref[...] = (a_ref[...].astype(F32) + b_ref[...].astype(F32)).astype(BF16)

    blk = pl.BlockSpec((None, tr, C), lambda s, i, core_ref: (s, i, 0))
    grid_spec = pltpu.PrefetchScalarGridSpec(
        num_scalar_prefetch=1, grid=(k, R // tr),
        in_specs=[pl.BlockSpec((None, None, tr, C), lambda s, i, core_ref: (s, core_ref[0], i, 0)), blk],
        out_specs=blk)
    return pl.pallas_call(
        body, name=name, grid_spec=grid_spec, out_shape=jax.ShapeDtypeStruct(theirs.shape, BF16),
        compiler_params=_params("parallel", "parallel"))(core, parts, theirs)


def _adamw_math(w, g, m, v):
    m = ADAM_B1 * m + (1.0 - ADAM_B1) * g
    v = ADAM_B2 * v + (1.0 - ADAM_B2) * (g * g)
    m_hat = m / (1.0 - ADAM_B1 ** ADAM_STEP)
    v_hat = v / (1.0 - ADAM_B2 ** ADAM_STEP)
    delta = -ADAM_LR * (m_hat / (jnp.sqrt(v_hat) + ADAM_EPS) + ADAM_WD * w)
    return delta, m, v


def _scattered_pieces(sums, got):
    return [("own", sums)] + [("peer%d" % j, got) for j in range(3)]


def _piece_spec(kind, tr, C):
    if kind == "own":
        return pl.BlockSpec((None, tr, C), lambda i, chip_ref: (chip_ref[0], i, 0))
    if kind == "plain":
        return pl.BlockSpec((tr, C), lambda i, chip_ref: (i, 0))
    return pl.BlockSpec((None, tr, C), functools.partial(lambda j, i, chip_ref: (j, i, 0), int(kind[-1])))


def _adamw_shard(chip, w, m, v, layer, pieces, so_far, name):
    L, R, C = w.shape
    tr = _pick(R, 256, 16)
    n_p = len(pieces)

    def body(chip_ref, w_ref, m_ref, v_ref, *rest):
        g = rest[0][...].astype(F32)
        for p in rest[1:n_p]:
            g = g + p[...].astype(F32)
        delta, nm, nv = _adamw_math(w_ref[...], g, m_ref[...], v_ref[...])
        g_ref, d_ref, nm_ref, nv_ref = rest[-4:]
        g_ref[...] = g
        d_ref[...] = delta
        nm_ref[...] = nm
        nv_ref[...] = nv

    state = pl.BlockSpec((None, tr, C), lambda i, chip_ref: (layer, i, 0))
    carried = [] if so_far is None else list(so_far)
    first_carried = 1 + 3 + n_p
    grid_spec = pltpu.PrefetchScalarGridSpec(
        num_scalar_prefetch=1, grid=(R // tr,),
        in_specs=[state] * 3 + [_piece_spec(kind, tr, C) for kind, _ in pieces] + [_ANY] * len(carried),
        out_specs=[state] * 4)
    return pl.pallas_call(
        body, name=name, grid_spec=grid_spec, out_shape=[jax.ShapeDtypeStruct((L, R, C), F32)] * 4,
        input_output_aliases={first_carried + t: t for t in range(len(carried))},
        compiler_params=_params("parallel"))(chip, w, m, v, *[a for _, a in pieces], *carried)


def _adamw_small(w, m, v, gathered, name):
    R = w.shape[0]
    tr = _pick(R, 512, SUBLANES)

    def body(w_ref, m_ref, v_ref, gg_ref, g_ref, d_ref, nm_ref, nv_ref):
        g = gg_ref[0, 0]
        for k in range(4):
            for c in range(2):
                if c or k:
                    g = g + gg_ref[k, c]
        delta, nm, nv = _adamw_math(w_ref[...], g, m_ref[...], v_ref[...])
        g_ref[...] = g
        d_ref[...] = delta
        nm_ref[...] = nm
        nv_ref[...] = nv

    row = pl.BlockSpec((tr, LANES), lambda i: (i, 0))
    out = jax.ShapeDtypeStruct((R, LANES), F32)
    return pl.pallas_call(
        body, name=name, grid=(R // tr,),
        in_specs=[row, row, row, pl.BlockSpec((4, 2, tr, LANES), lambda i: (0, 0, i, 0))], out_specs=[row] * 4,
        out_shape=[out] * 4, compiler_params=_params("parallel"))(w, m, v, gathered)


def _to_heads(a, n_heads):
    return a.reshape(a.shape[0], n_heads, HEAD_DIM).transpose(1, 0, 2)


def _from_heads(a):
    return a.transpose(1, 0, 2).reshape(a.shape[1], a.shape[0] * HEAD_DIM)


def _layer_fwd(x, w, s, tag, sched):
    h = _rms_fwd(x, s["norm_mix_g"], f"rms_mix_{tag}")
    z = sched.carry(f"in_proj_{tag}", _mm, h, w["w_in_t"], "nt", F32, tn=512)
    q = _to_heads(z[:, :OFF_K], N_Q_HEADS)
    k = _to_heads(z[:, OFF_K:OFF_V], N_KV_HEADS)
    v = _to_heads(z[:, OFF_V:OFF_U], N_KV_HEADS)
    ya = _from_heads(sched.carry(f"attn_fwd_{tag}", _attn_fwd, q, k, v, s["q_norm_g"], s["k_norm_g"],
                                 s["attn_sinks"]))
    y, s_r, s_i = sched.carry(f"ssm_fwd_{tag}", _ssm_fwd, z, *s["ssm16"], s["ssm_d"])
    pre, y3 = _glu_fwd(y, w["ssm_glu_w"], s["ssm_glu_b"], f"glu_fwd_{tag}")
    a, b, merged = sched.carry(f"merge_fwd_{tag}", _merge_fwd, ya, y3, w["w_attn_branch"], w["w_ssm_branch"], z,
                               s["gate_bias"])
    x1 = _mm(merged, w["w_out"], "nn", F32, f"out_proj_{tag}", residual=x)
    h2 = _rms_fwd(x1, s["norm_ffn_g"], f"rms_ffn_{tag}")
    g, u, act = sched.carry(f"ffn_fwd_{tag}", _ffn_fwd, h2, w["w_ffn_in"])
    x2 = sched.carry(f"ffn_out_{tag}", _mm, act, w["w_ffn_out"], "nn", F32, residual=x1)
    saved = dict(x=x, h=h, z=z, q=q, k=k, v=v, ya=ya, y=y, s_r=s_r, s_i=s_i, pre=pre, y3=y3, a=a, b=b, merged=merged,
                 x1=x1, h2=h2, g=g, u=u, act=act)
    return x2, saved


def _layer_bwd(dx2, sv, w, s, tag, sched, scatter):
    gw, gs = {}, {}
    dg16, du16 = _ffn_bwd_act(dx2, w["w_ffn_out"], sv["g"], sv["u"], f"ffn_bwd_act_{tag}")
    gw["w_ffn_out"] = _mm(sv["act"], dx2, "tn", BF16, f"dw_ffn_out_{tag}", tm=1408)
    dh2 = _ffn_bwd_in(dg16, du16, w["w_ffn_in"], f"dh2_{tag}")
    gw["w_ffn_in"] = _ffn_bwd_w(sv["h2"], dg16, du16, f"dw_ffn_in_{tag}")
    scatter(FFN_WEIGHTS, gw, f"attn_bwd_{tag}")
    dx1, gs["norm_ffn_g"] = _rms_bwd(dh2, sv["x1"], s["norm_ffn_g"], dx2, f"rms_ffn_bwd_{tag}")
    dm = _mm(dx1, w["w_out"], "nt", F32, f"dmerged_{tag}")
    gw["w_out"] = _mm(sv["merged"], dx1, "tn", BF16, f"dw_out_{tag}")
    da16, db16, dza, dzs, dba, dbs = _merge_bwd(dm, sv["a"], sv["b"], sv["z"], s["gate_bias"], f"merge_bwd_{tag}")
    gs["gate_bias"] = jnp.concatenate([dba, dbs], axis=1)
    dya = _mm(da16, w["w_attn_branch"], "nt", F32, f"dya_{tag}")
    gw["w_attn_branch"] = _mm(sv["ya"], da16, "tn", BF16, f"dw_attn_branch_{tag}")
    dy3 = _mm(db16, w["w_ssm_branch"], "nt", F32, f"dy3_{tag}")
    gw["w_ssm_branch"] = _mm(sv["y3"], db16, "tn", BF16, f"dw_ssm_branch_{tag}")
    dpre16, t1, y2_16, gs["ssm_glu_b"] = _glu_bwd_gate(dy3, sv["y"], sv["pre"], f"glu_bwd_gate_{tag}")
    dy = _glu_bwd_in(dpre16, w["ssm_glu_w"], t1, sv["y"], f"glu_bwd_in_{tag}")
    gw["ssm_glu_w"] = _mm(y2_16, dpre16, "tn", BF16, f"dw_glu_{tag}")
    scatter(MIXER_WEIGHTS[1:], gw, f"ssm_bwd_{tag}")
    du_ssm, *gs["ssm_disc"], gs["ssm_d"] = sched.carry(f"ssm_bwd_{tag}", _ssm_bwd, dy, sv["z"], sv["s_r"], sv["s_i"],
                                                       *s["ssm16"], s["ssm_d"])
    dq, dk, dv, dqg, dkg, dsk = sched.carry(f"attn_bwd_{tag}", _attn_bwd, sv["q"], sv["k"], sv["v"],
                                            _to_heads(dya, N_Q_HEADS), s["q_norm_g"], s["k_norm_g"], s["attn_sinks"])
    gs["q_norm_g"] = jnp.sum(dqg, axis=0)
    gs["k_norm_g"] = jnp.sum(dkg, axis=0)
    gs["attn_sinks"] = dsk[:, :, 0].reshape(1, N_Q_HEADS)
    dkv = jnp.concatenate([_from_heads(dk), _from_heads(dv)], axis=1).astype(BF16)
    dz = [_from_heads(dq), dkv, du_ssm, dza, dzs]
    gw["w_in"] = _mm_cols_tn(dz, sv["h"], f"dw_in_t_{tag}")
    scatter(MIXER_WEIGHTS[:1], gw, f"dh_{tag}")
    dh = sched.carry(f"dh_{tag}", _mm_cols_nn, dz, w["w_in_t"])
    dx, gs["norm_mix_g"] = _rms_bwd(dh, sv["x"], s["norm_mix_g"], dx1, f"rms_mix_bwd_{tag}")
    return dx, gs


def _shard_to_send(name, shard):
    return (shard.T if name == "w_in" else shard).astype(BF16)


def _assemble(name, gathered):
    if name == "w_ffn_in":
        return gathered
    if name in COL_SHARDED and name != "w_in":
        rows = gathered.shape[2]
        return gathered.transpose(2, 0, 1, 3).reshape(rows, -1)
    return gathered.reshape(-1, gathered.shape[3])


class _Weights:
    def __init__(self):
        self.sources, self.ready = [], {}

    def __getitem__(self, name):
        if name not in self.ready:
            key = "w_in" if name == "w_in_t" else name
            (gathered,) = [src[key] for src in self.sources if key in src]
            self.ready[name] = _assemble(key, gathered)
        return self.ready[name]


def _disassemble(name, grad):
    if name == "w_ffn_in":
        return grad
    if name in COL_SHARDED and name != "w_in":
        rows, cols = grad.shape
        return grad.reshape(rows, 4, 2, cols // N_DEV).transpose(1, 2, 0, 3)
    rows, cols = grad.shape
    return grad.reshape(4, 2, rows // N_DEV, cols)


def _pack(arrays):
    flat = jnp.concatenate([a.reshape(-1) for a in arrays])
    pad = (-flat.shape[0]) % (SUBLANES * LANES)
    return jnp.pad(flat, (0, pad)).reshape(-1, LANES)


def _unpack(packed, like):
    flat, out, off = packed.reshape(-1), [], 0
    for a in like:
        out.append(flat[off:off + a.size].reshape(a.shape))
        off += a.size
    return out


def kernel(x, norm_mix_g, w_in, gate_bias, q_norm_g, k_norm_g, attn_sinks, ssm_lambda_re, ssm_lambda_im, ssm_log_dt, ssm_b_re, ssm_b_im, ssm_c_re, ssm_c_im, ssm_d, ssm_glu_w, ssm_glu_b, w_attn_branch, w_ssm_branch, w_out, norm_ffn_g, w_ffn_in, w_ffn_out, loss_target, m_norm_mix_g, m_w_in, m_gate_bias, m_q_norm_g, m_k_norm_g, m_attn_sinks, m_ssm_lambda_re, m_ssm_lambda_im, m_ssm_log_dt, m_ssm_b_re, m_ssm_b_im, m_ssm_c_re, m_ssm_c_im, m_ssm_d, m_ssm_glu_w, m_ssm_glu_b, m_w_attn_branch, m_w_ssm_branch, m_w_out, m_norm_ffn_g, m_w_ffn_in, m_w_ffn_out, v_norm_mix_g, v_w_in, v_gate_bias, v_q_norm_g, v_k_norm_g, v_attn_sinks, v_ssm_lambda_re, v_ssm_lambda_im, v_ssm_log_dt, v_ssm_b_re, v_ssm_b_im, v_ssm_c_re, v_ssm_c_im, v_ssm_d, v_ssm_glu_w, v_ssm_glu_b, v_w_attn_branch, v_w_ssm_branch, v_w_out, v_norm_ffn_g, v_w_ffn_in, v_w_ffn_out):
    given = dict(locals())
    wts = {n: given[n] for n in WEIGHTS}
    mom = {n: given["m_" + n] for n in WEIGHTS}
    var = {n: given["v_" + n] for n in WEIGHTS}
    depth = w_in.shape[0]
    xs = x[0]
    target = loss_target[0]

    sched = _Schedule()
    shards = [{n: _shard_to_send(n, wts[n][l]) for n in BIG} for l in range(depth)]
    full = [_Weights() for _ in range(depth)]

    def gather(layer, names, send_host, pass_host):
        sent = sched.ride(send_host, _gather_send, names, shards[layer])
        full[layer].sources.append(sched.ride(pass_host, _gather_pass, names, sent))

    first = MIXER_WEIGHTS[:1]
    full[0].sources.append(dict(zip(first, _all_gather([shards[0][n] for n in first], "gather_w_in_0"))))
    gather(0, ("w_ffn_in",), "attn_fwd_0", "ssm_fwd_0")
    gather(0, ("w_ffn_out",), "ssm_fwd_0", "merge_fwd_0")
    for l in range(depth):
        gather(l, MIXER_WEIGHTS[1:], f"in_proj_{l}", f"attn_fwd_{l}")
        if l + 1 < depth:
            gather(l + 1, first, f"merge_fwd_{l}", f"ffn_fwd_{l}")
            gather(l + 1, ("w_ffn_in",), f"ffn_fwd_{l}", f"ffn_out_{l}")
            gather(l + 1, ("w_ffn_out",), f"ffn_out_{l}", f"in_proj_{l + 1}")

    small, disc_vjp = [], []
    for l in range(depth):
        s = {n: wts[n][l].reshape(1, -1) for n in ("norm_mix_g", "gate_bias", "q_norm_g", "k_norm_g", "attn_sinks",
                                                   "ssm_d", "ssm_glu_b", "norm_ffn_g")}
        disc, vjp = jax.vjp(_ssm_discretize, *[wts[n][l] for n in ("ssm_lambda_re", "ssm_lambda_im", "ssm_log_dt",
                                                                  "ssm_b_re", "ssm_b_im", "ssm_c_re", "ssm_c_im")])
        s["ssm16"] = (disc[0], disc[1]) + tuple(d.astype(BF16) for d in disc[2:])
        small.append(s)
        disc_vjp.append(vjp)

    act, saved = xs, []
    for l in range(depth):
        act, sv = _layer_fwd(act, full[l], small[l], str(l), sched)
        saved.append(sv)
    dact, loss_local = _loss_grad(act, target, "loss_head")

    out = {"grad": {}, "delta": {}, "new_m": {}, "new_v": {}}
    results = {n: None for n in BIG}
    core = lax.axis_index("c").astype(jnp.int32).reshape(1)
    chip = (2 * lax.axis_index("x") + lax.axis_index("y")).astype(jnp.int32).reshape(1)
    state = {n: [a.transpose(0, 2, 1) if n == "w_in" else a for a in (wts[n], mom[n], var[n])] for n in BIG}
    small_grads = [None] * depth
    for l in reversed(range(depth)):
        scattered = []

        def scatter(names, grads, host):
            parts = [_disassemble(n, grads[n]) for n in names]
            theirs = _pair_exchange(parts, f"grad_pair_exchange_{host}")
            sums = {n: _add_pair(core, p, t, f"grad_pair_sum_{n}_{host}") for n, p, t in zip(names, parts, theirs)}
            scattered.append((sums, sched.ride(host, _scatter_send, names, sums)))

        dact, gs = _layer_bwd(dact, saved[l], full[l], small[l], str(l), sched, scatter)
        (gs["ssm_lambda_re"], gs["ssm_lambda_im"], gs["ssm_log_dt"], gs["ssm_b_re"], gs["ssm_b_im"], gs["ssm_c_re"],
         gs["ssm_c_im"]) = disc_vjp[l](tuple(gs.pop("ssm_disc")))
        small_grads[l] = gs
        for sums, got in scattered:
            for n in got:
                results[n] = _adamw_shard(chip, state[n][0], state[n][1], state[n][2], l,
                                          _scattered_pieces(sums[n], got[n]), results[n], f"adamw_{n}_{l}")
    loss = lax.psum(loss_local, ("x", "y", "c"))
    for n in BIG:
        for kind, res in zip(("grad", "delta", "new_m", "new_v"), results[n]):
            out[kind][n] = res.transpose(0, 2, 1) if n == "w_in" else res

    like = [wts[n] for n in SMALL]
    g_small = _pack([jnp.stack([small_grads[l][n].reshape(wts[n].shape[1:]) for l in range(depth)]) for n in SMALL])
    (gathered_small,) = _all_gather([g_small], "gather_small_grads")
    res = _adamw_small(_pack(like), _pack([mom[n] for n in SMALL]), _pack([var[n] for n in SMALL]), gathered_small,
                       "adamw_small")
    for kind, packed in zip(("grad", "delta", "new_m", "new_v"), res):
        for n, a in zip(SMALL, _unpack(packed, like)):
            out[kind][n] = a

    grad_x = dact.reshape(x.shape)
    return (loss, grad_x, *[out["grad"][n] for n in WEIGHTS], *[out["delta"][n] for n in WEIGHTS],
            *[out["new_m"][n] for n in WEIGHTS], *[out["new_v"][n] for n in WEIGHTS])
```

```python
import functools
import math

import jax
import jax.numpy as jnp
from jax import lax
from jax.experimental import pallas as pl
from jax.experimental.pallas import tpu as pltpu

F32, BF16 = jnp.float32, jnp.bfloat16
MESH = pl.DeviceIdType.MESH

D_MODEL = 2048
HEAD_DIM = 64
N_Q_HEADS = 16
N_KV_HEADS = 4
GQA_GROUP = N_Q_HEADS // N_KV_HEADS
ATTN_WIDTH = N_Q_HEADS * HEAD_DIM
KV_WIDTH = N_KV_HEADS * HEAD_DIM
WINDOW = 128
BLOCK = 128
SSM_WIDTH = D_MODEL // 2
SSM_GROUP_CH = 16
SSM_GROUPS = SSM_WIDTH // SSM_GROUP_CH
SSM_STATE = 64
D_FF = 5632
OFF_K = ATTN_WIDTH
OFF_V = OFF_K + KV_WIDTH
OFF_U = OFF_V + KV_WIDTH
OFF_G = OFF_U + SSM_WIDTH
IN_WIDTH = OFF_G + 2 * D_MODEL
RMS_EPS = 1e-6
ATTN_SCALE = HEAD_DIM ** -0.5
NEG_BIG = -1e30

SSM_NGB = 4
SSM_GB_CH = SSM_WIDTH // SSM_NGB
SSM_GB_ST = SSM_GROUPS * SSM_STATE // SSM_NGB
SUBLANES = 8
LANES = 128
SSM_TT = 256

ADAM_LR = 0.001
ADAM_B1 = 0.9
ADAM_B2 = 0.999
ADAM_EPS = 1e-08
ADAM_WD = 0.01
ADAM_STEP = 10

N_DEV = 8
VMEM_LIMIT_BYTES = 52 * 1024 * 1024

BIG = ("w_in", "ssm_glu_w", "w_attn_branch", "w_ssm_branch", "w_out", "w_ffn_in", "w_ffn_out")
COL_SHARDED = ("w_in", "w_attn_branch", "w_ssm_branch", "w_ffn_in")
FFN_WEIGHTS = ("w_ffn_in", "w_ffn_out")
MIXER_WEIGHTS = ("w_in", "ssm_glu_w", "w_attn_branch", "w_ssm_branch", "w_out")
SMALL = ("norm_mix_g", "gate_bias", "q_norm_g", "k_norm_g", "attn_sinks", "ssm_lambda_re", "ssm_lambda_im",
         "ssm_log_dt", "ssm_b_re", "ssm_b_im", "ssm_c_re", "ssm_c_im", "ssm_d", "ssm_glu_b", "norm_ffn_g")
WEIGHTS = ("norm_mix_g", "w_in", "gate_bias", "q_norm_g", "k_norm_g", "attn_sinks", "ssm_lambda_re", "ssm_lambda_im",
           "ssm_log_dt", "ssm_b_re", "ssm_b_im", "ssm_c_re", "ssm_c_im", "ssm_d", "ssm_glu_w", "ssm_glu_b",
           "w_attn_branch", "w_ssm_branch", "w_out", "norm_ffn_g", "w_ffn_in", "w_ffn_out")


def _pick(n, target, mult=LANES):
    best = None
    for t in range(mult, min(n, target) + 1, mult):
        if n % t == 0:
            best = t
    return n if best is None else best


def _params(*sem):
    return pltpu.CompilerParams(dimension_semantics=sem, vmem_limit_bytes=VMEM_LIMIT_BYTES)


def _sigmoid(v):
    return 1.0 / (1.0 + jnp.exp(-v))


_GELU_C = math.sqrt(2.0 / math.pi)


def _gelu(v):
    return 0.5 * v * (1.0 + jnp.tanh(_GELU_C * (v + 0.044715 * v * v * v)))


def _gelu_grad(v):
    t = jnp.tanh(_GELU_C * (v + 0.044715 * v * v * v))
    return 0.5 * (1.0 + t) + 0.5 * v * (1.0 - t * t) * _GELU_C * (1.0 + 3.0 * 0.044715 * v * v)


_DN = {"nn": (((1,), (0,)), ((), ())), "nt": (((1,), (1,)), ((), ())), "tn": (((0,), (0,)), ((), ()))}


def _dot(a, b, dims="nn"):
    return lax.dot_general(a, b, _DN[dims], preferred_element_type=F32)


class _Comm:
    def __init__(self, args, out_shapes, sems, copies, aliases=None):
        self.args, self.out_shapes, self.sems = list(args), list(out_shapes), list(sems)
        self.copies, self.aliases = copies, dict(aliases or {})


def _call(body, name, grid, in_specs, out_specs, out_shape, scratch_shapes, semantics, args, comm=None):
    in_specs, out_specs, out_shape = list(in_specs), list(out_specs), list(out_shape)
    scratch_shapes = list(scratch_shapes)
    if comm is None:
        res = pl.pallas_call(body, name=name, grid=grid, in_specs=in_specs, out_specs=out_specs, out_shape=out_shape,
                             scratch_shapes=scratch_shapes, compiler_params=_params(*semantics))(*args)
        return list(res), []
    n_in, n_out, n_scr = len(in_specs), len(out_specs), len(scratch_shapes)
    n_cin, n_cout = len(comm.args), len(comm.out_shapes)

    def carrying(*refs):
        ins, cin = refs[:n_in], refs[n_in:n_in + n_cin]
        o0 = n_in + n_cin
        outs, cout = refs[o0:o0 + n_out], refs[o0 + n_out:o0 + n_out + n_cout]
        s0 = o0 + n_out + n_cout
        scr, sems = refs[s0:s0 + n_scr], refs[s0 + n_scr:]
        first = functools.reduce(jnp.logical_and, [pl.program_id(d) == 0 for d in range(len(grid))])
        last = functools.reduce(jnp.logical_and, [pl.program_id(d) == grid[d] - 1 for d in range(len(grid))])

        @pl.when(first)
        def _():
            sends, _, local = comm.copies(cin, cout, sems, False)
            for cp in local + sends:
                cp.start()

        body(*ins, *outs, *scr)

        @pl.when(last)
        def _():
            sends, recvs, local = comm.copies(cin, cout, sems, True)
            for cp in recvs:
                cp.wait_recv()
            for cp in sends:
                cp.wait_send()
            for cp in local:
                cp.wait()

    res = pl.pallas_call(
        carrying, name=name, grid=grid, in_specs=in_specs + [_ANY] * n_cin, out_specs=out_specs + [_ANY] * n_cout,
        out_shape=out_shape + comm.out_shapes, scratch_shapes=scratch_shapes + comm.sems,
        input_output_aliases={n_in + i: n_out + o for i, o in comm.aliases.items()},
        compiler_params=_params(*["arbitrary"] * len(grid)))(*args, *comm.args)
    return list(res[:n_out]), list(res[n_out:])


def _mm(a, b, dims, out_dtype, name, residual=None, tm=1024, tn=1024, tk=2048, comm=None):
    if dims == "tn":
        K, M = a.shape
    else:
        M, K = a.shape
    N = b.shape[0] if dims == "nt" else b.shape[1]
    tm, tn, tk = _pick(M, tm), _pick(N, tn), _pick(K, tk)
    nk = K // tk
    has_res = residual is not None

    def finish(out, refs):
        if has_res:
            out = out + refs[2][...].astype(F32)
        refs[-2][...] = out.astype(out_dtype)

    def body_single(*refs):
        finish(_dot(refs[0][...].astype(BF16), refs[1][...].astype(BF16), dims), refs)

    def body_multi(*refs):
        acc_ref = refs[-1]
        k = pl.program_id(2)

        @pl.when(k == 0)
        def _():
            acc_ref[...] = jnp.zeros_like(acc_ref)

        acc_ref[...] += _dot(refs[0][...].astype(BF16), refs[1][...].astype(BF16), dims)

        @pl.when(k == nk - 1)
        def _():
            finish(acc_ref[...], refs)

    a_spec = (pl.BlockSpec((tk, tm), lambda i, j, k: (k, i)) if dims == "tn"
              else pl.BlockSpec((tm, tk), lambda i, j, k: (i, k)))
    b_spec = (pl.BlockSpec((tn, tk), lambda i, j, k: (j, k)) if dims == "nt"
              else pl.BlockSpec((tk, tn), lambda i, j, k: (k, j)))
    o_spec = pl.BlockSpec((tm, tn), lambda i, j, k: (i, j))
    in_specs = [a_spec, b_spec] + ([o_spec] if has_res else [])
    args = (a, b) + ((residual,) if has_res else ())
    (out,), extra = _call(
        body_single if nk == 1 else body_multi, name, (M // tm, N // tn, nk), in_specs, [o_spec],
        [jax.ShapeDtypeStruct((M, N), out_dtype)], [pltpu.VMEM((tm, tn) if nk > 1 else (SUBLANES, LANES), F32)],
        ("parallel", "parallel", "arbitrary"), args, comm)
    return out if comm is None else (out, extra)


PIECE_BLOCK = 512


def _piece_blocks(pieces):
    counts = [p.shape[1] // PIECE_BLOCK for p in pieces]
    return counts, [sum(counts[:i]) for i in range(len(counts))]


def _mm_cols_nn(pieces, b, name, comm=None, tm=1024, tn=1024):
    M, N = pieces[0].shape[0], b.shape[1]
    tm, tn = _pick(M, tm), _pick(N, tn)
    counts, starts = _piece_blocks(pieces)
    n, nk = len(pieces), sum(counts)

    def body(*refs):
        b_ref, o_ref, acc_ref = refs[n:]
        k = pl.program_id(2)

        @pl.when(k == 0)
        def _():
            acc_ref[...] = jnp.zeros_like(acc_ref)

        for a_ref, c, s in zip(refs[:n], counts, starts):
            @pl.when((k >= s) & (k < s + c))
            def _(a_ref=a_ref):
                acc_ref[...] += _dot(a_ref[...], b_ref[...])

        @pl.when(k == nk - 1)
        def _():
            o_ref[...] = acc_ref[...]

    a_specs = [pl.BlockSpec((tm, PIECE_BLOCK), functools.partial(lambda s, c, i, j, k: (i, jnp.clip(k - s, 0, c - 1)), s, c))
               for c, s in zip(counts, starts)]
    b_spec = pl.BlockSpec((PIECE_BLOCK, tn), lambda i, j, k: (k, j))
    o_spec = pl.BlockSpec((tm, tn), lambda i, j, k: (i, j))
    (out,), extra = _call(body, name, (M // tm, N // tn, nk), a_specs + [b_spec], [o_spec],
                          [jax.ShapeDtypeStruct((M, N), F32)], [pltpu.VMEM((tm, tn), F32)],
                          ("parallel", "parallel", "arbitrary"), (*pieces, b), comm)
    return out if comm is None else (out, extra)


def _mm_cols_tn(pieces, b, name, tn=1024, tk=2048):
    K, N = b.shape
    tn, tk = _pick(N, tn), _pick(K, tk)
    counts, starts = _piece_blocks(pieces)
    n, nk = len(pieces), K // tk

    def body(*refs):
        b_ref, o_ref, acc_ref = refs[n:]
        i, k = pl.program_id(0), pl.program_id(2)

        @pl.when(k == 0)
        def _():
            acc_ref[...] = jnp.zeros_like(acc_ref)

        for a_ref, c, s in zip(refs[:n], counts, starts):
            @pl.when((i >= s) & (i < s + c))
            def _(a_ref=a_ref):
                acc_ref[...] += _dot(a_ref[...], b_ref[...], "tn")

        @pl.when(k == nk - 1)
        def _():
            o_ref[...] = acc_ref[...].astype(BF16)

    def a_index(s, c, i, j, k):
        mine = (i >= s) & (i < s + c)
        return jnp.where(mine, k, 0), jnp.clip(i - s, 0, c - 1)

    a_specs = [pl.BlockSpec((tk, PIECE_BLOCK), functools.partial(a_index, s, c)) for c, s in zip(counts, starts)]
    b_spec = pl.BlockSpec((tk, tn), lambda i, j, k: (k, j))
    o_spec = pl.BlockSpec((PIECE_BLOCK, tn), lambda i, j, k: (i, j))
    (out,), _ = _call(body, name, (sum(counts), N // tn, nk), a_specs + [b_spec], [o_spec],
                      [jax.ShapeDtypeStruct((sum(counts) * PIECE_BLOCK, N), BF16)],
                      [pltpu.VMEM((PIECE_BLOCK, tn), F32)], ("parallel", "parallel", "arbitrary"), (*pieces, b))
    return out


def _rms_fwd(x, g, name):
    T, D = x.shape
    tr = _pick(T, 256, SUBLANES)

    def body(x_ref, g_ref, o_ref):
        xf = x_ref[...]
        r = lax.rsqrt(jnp.mean(xf * xf, axis=-1, keepdims=True) + RMS_EPS)
        o_ref[...] = (xf * r * g_ref[...]).astype(BF16)

    return pl.pallas_call(
        body, name=name, grid=(T // tr,),
        in_specs=[pl.BlockSpec((tr, D), lambda i: (i, 0)), pl.BlockSpec((1, D), lambda i: (0, 0))],
        out_specs=pl.BlockSpec((tr, D), lambda i: (i, 0)), out_shape=jax.ShapeDtypeStruct((T, D), BF16),
        compiler_params=_params("parallel"))(x, g)


def _rms_bwd(dh, x, g, dres, name):
    T, D = x.shape
    tr = _pick(T, 256, SUBLANES)

    def body(dh_ref, x_ref, g_ref, dres_ref, dx_ref, dg_ref):
        @pl.when(pl.program_id(0) == 0)
        def _():
            dg_ref[...] = jnp.zeros_like(dg_ref)

        xf = x_ref[...]
        r = lax.rsqrt(jnp.mean(xf * xf, axis=-1, keepdims=True) + RMS_EPS)
        xhat = xf * r
        dhv = dh_ref[...]
        dxh = dhv * g_ref[...]
        dx_ref[...] = dres_ref[...] + r * (dxh - xhat * jnp.mean(dxh * xhat, axis=-1, keepdims=True))
        dg_ref[...] += jnp.sum(dhv * xhat, axis=0, keepdims=True)

    row = pl.BlockSpec((tr, D), lambda i: (i, 0))
    vec = pl.BlockSpec((1, D), lambda i: (0, 0))
    return pl.pallas_call(
        body, name=name, grid=(T // tr,), in_specs=[row, row, vec, row], out_specs=[row, vec],
        out_shape=[jax.ShapeDtypeStruct((T, D), F32), jax.ShapeDtypeStruct((1, D), F32)],
        compiler_params=_params("arbitrary"))(dh, x, g, dres)


def _loss_grad(y, target, name):
    T, D = y.shape
    tr = _pick(T, 256, SUBLANES)

    def body(y_ref, t_ref, dx_ref, loss_ref):
        @pl.when(pl.program_id(0) == 0)
        def _():
            loss_ref[...] = jnp.zeros_like(loss_ref)

        err = y_ref[...] - t_ref[...]
        dx_ref[...] = err * (1.0 / D)
        loss_ref[...] += jnp.sum(jnp.mean(err * err, axis=-1, keepdims=True), axis=0, keepdims=True) * 0.5

    row = pl.BlockSpec((tr, D), lambda i: (i, 0))
    one = pl.BlockSpec((1, 1), lambda i: (0, 0))
    dx, loss = pl.pallas_call(
        body, name=name, grid=(T // tr,), in_specs=[row, row], out_specs=[row, one],
        out_shape=[jax.ShapeDtypeStruct((T, D), F32), jax.ShapeDtypeStruct((1, 1), F32)],
        compiler_params=_params("arbitrary"))(y, target)
    return dx, loss[0, 0]


_STACK = GQA_GROUP * BLOCK


def _attn_mask(n):
    row = lax.broadcasted_iota(jnp.int32, (_STACK, 2 * BLOCK), 0)
    col = lax.broadcasted_iota(jnp.int32, (_STACK, 2 * BLOCK), 1)
    dist = (row % BLOCK) - col + BLOCK
    valid = (dist >= 0) & (dist < WINDOW) & ((col >= BLOCK) | (n > 0))
    return dist.astype(F32), valid


def _per_row_head(kv, sk_ref):
    heads = [kv * GQA_GROUP + g for g in range(GQA_GROUP)]
    slope = jnp.concatenate([jnp.full((BLOCK, 1), 2.0 ** (-8.0 * (h + 1) / N_Q_HEADS), F32) for h in heads], axis=0)
    sink = jnp.concatenate([jnp.broadcast_to(sk_ref[h:h + 1, 0:1], (BLOCK, 1)) for h in heads], axis=0)
    return slope, sink


def _head_cols(h):
    return slice(h * HEAD_DIM, (h + 1) * HEAD_DIM)


def _stack_heads(x, kv):
    return jnp.concatenate([x[:, _head_cols(kv * GQA_GROUP + g)] for g in range(GQA_GROUP)], axis=0)


def _head_norm(v, gain):
    r = lax.rsqrt(jnp.mean(v * v, axis=-1, keepdims=True) + RMS_EPS)
    vhat = v * r
    return r, vhat, vhat * gain


def _attn_probs(qn16, kn16, slope, dist, valid, sink):
    s = _dot(qn16, kn16, "nt") * ATTN_SCALE - slope * dist
    s = jnp.where(valid, s, NEG_BIG)
    m = jnp.maximum(jnp.max(s, axis=-1, keepdims=True), sink)
    p = jnp.exp(s - m)
    ps = jnp.exp(sink - m)
    den = jnp.sum(p, axis=-1, keepdims=True) + ps
    return p, ps, den


def _attn_specs():
    wide = pl.BlockSpec((BLOCK, ATTN_WIDTH), lambda n: (n, 0))
    cur = [pl.BlockSpec((BLOCK, KV_WIDTH), functools.partial(lambda c, n: (n, c), off // KV_WIDTH))
           for off in (OFF_K, OFF_V)]
    prev = [pl.BlockSpec((BLOCK, KV_WIDTH), functools.partial(lambda c, n: (jnp.maximum(n - 1, 0), c), off // KV_WIDTH))
            for off in (OFF_K, OFF_V)]
    gain = pl.BlockSpec((1, HEAD_DIM), lambda n: (0, 0))
    sink = pl.BlockSpec((N_Q_HEADS, LANES), lambda n: (0, 0))
    return wide, [cur[0], prev[0], cur[1], prev[1]], gain, sink


def _sink_rows(sinks):
    return jnp.broadcast_to(sinks.reshape(N_Q_HEADS, 1), (N_Q_HEADS, LANES))


def _attn_fwd(q, k, v, qg, kg, sinks, name, comm=None):
    T = q.shape[1]

    def body(q_ref, kc_ref, kp_ref, vc_ref, vp_ref, qg_ref, kg_ref, sk_ref, o_ref):
        dist, valid = _attn_mask(pl.program_id(1))
        row_head = pl.program_id(0) * GQA_GROUP + lax.broadcasted_iota(jnp.int32, (_STACK, 1), 0) // BLOCK
        slope = jnp.exp((row_head + 1).astype(F32) * (-8.0 / N_Q_HEADS * math.log(2.0)))
        first_head = pl.program_id(0) * GQA_GROUP
        sink = jnp.concatenate([jnp.broadcast_to(sk_ref[pl.ds(first_head + g, 1), 0:1], (BLOCK, 1))
                                for g in range(GQA_GROUP)], axis=0)
        kk = jnp.concatenate([kp_ref[...], kc_ref[...]], axis=0)
        _, _, kn = _head_norm(kk, kg_ref[...])
        v16 = jnp.concatenate([vp_ref[...], vc_ref[...]], axis=0).astype(BF16)
        _, _, qn = _head_norm(q_ref[...].reshape(_STACK, HEAD_DIM), qg_ref[...])
        p, _, den = _attn_probs(qn.astype(BF16), kn.astype(BF16), slope, dist, valid, sink)
        o_ref[...] = _dot((p / den).astype(BF16), v16).reshape(GQA_GROUP, BLOCK, HEAD_DIM).astype(BF16)

    q_spec = pl.BlockSpec((GQA_GROUP, BLOCK, HEAD_DIM), lambda h, n: (h, n, 0))
    cur = pl.BlockSpec((None, BLOCK, HEAD_DIM), lambda h, n: (h, n, 0))
    prev = pl.BlockSpec((None, BLOCK, HEAD_DIM), lambda h, n: (h, jnp.maximum(n - 1, 0), 0))
    gain = pl.BlockSpec((1, HEAD_DIM), lambda h, n: (0, 0))
    sink = pl.BlockSpec((N_Q_HEADS, LANES), lambda h, n: (0, 0))
    (out,), extra = _call(
        body, name, (N_KV_HEADS, T // BLOCK), [q_spec, cur, prev, cur, prev, gain, gain, sink], [q_spec],
        [jax.ShapeDtypeStruct((N_Q_HEADS, T, HEAD_DIM), BF16)], [], ("parallel", "parallel"),
        (q, k, k, v, v, qg, kg, _sink_rows(sinks)), comm)
    return out if comm is None else (out, extra)


def _attn_bwd(z, do, qg, kg, sinks, name, comm=None):
    T = z.shape[0]

    def body(q_ref, kc_ref, kp_ref, vc_ref, vp_ref, do_ref, qg_ref, kg_ref, sk_ref,
             dq_ref, dk_ref, dv_ref, dqg_ref, dkg_ref, dsk_ref):
        n = pl.program_id(0)

        @pl.when(n == 0)
        def _():
            for ref in (dk_ref, dv_ref, dqg_ref, dkg_ref, dsk_ref):
                ref[...] = jnp.zeros_like(ref)

        dist, valid = _attn_mask(n)
        rows = pl.ds(pl.multiple_of(n * BLOCK, BLOCK), BLOCK)
        before = pl.ds(pl.multiple_of(jnp.maximum(n - 1, 0) * BLOCK, BLOCK), BLOCK)
        q = q_ref[...]
        do = do_ref[...]
        kk = jnp.concatenate([kp_ref[...], kc_ref[...]], axis=0)
        vv = jnp.concatenate([vp_ref[...], vc_ref[...]], axis=0)
        for kv in range(N_KV_HEADS):
            slope, sink = _per_row_head(kv, sk_ref)
            rk, khat, kn = _head_norm(kk[:, _head_cols(kv)], kg_ref[...])
            kn16 = kn.astype(BF16)
            v16 = vv[:, _head_cols(kv)].astype(BF16)
            rq, qhat, qn = _head_norm(_stack_heads(q, kv), qg_ref[...])
            qn16 = qn.astype(BF16)
            p, ps, den = _attn_probs(qn16, kn16, slope, dist, valid, sink)
            pn = p / den
            do16 = _stack_heads(do, kv).astype(BF16)
            dp = _dot(do16, v16, "nt")
            delta = jnp.sum(pn * dp, axis=-1, keepdims=True)
            ds16 = (pn * (dp - delta)).astype(BF16)
            sink_pull = ps / den * delta
            dqn = _dot(ds16, kn16) * ATTN_SCALE
            dkn = _dot(ds16, qn16, "tn") * ATTN_SCALE
            dv = _dot(pn.astype(BF16), do16, "tn")
            dqh = dqn * qg_ref[...]
            dq = rq * (dqh - qhat * jnp.mean(dqh * qhat, axis=-1, keepdims=True))
            for g in range(GQA_GROUP):
                h = kv * GQA_GROUP + g
                dsink = jnp.sum(sink_pull[g * BLOCK:(g + 1) * BLOCK], axis=0, keepdims=True)
                dsk_ref[h:h + 1, :] -= jnp.broadcast_to(dsink, (1, LANES))
                dq_ref[:, _head_cols(h)] = dq[g * BLOCK:(g + 1) * BLOCK].astype(BF16)
            dqg_ref[...] += jnp.sum(dqn * qhat, axis=0, keepdims=True)
            dkg_ref[...] += jnp.sum(dkn * khat, axis=0, keepdims=True)
            dkh = dkn * kg_ref[...]
            dk = rk * (dkh - khat * jnp.mean(dkh * khat, axis=-1, keepdims=True))
            dk_ref[rows, _head_cols(kv)] += dk[BLOCK:]
            dv_ref[rows, _head_cols(kv)] += dv[BLOCK:]
            dk_ref[before, _head_cols(kv)] += dk[:BLOCK]
            dv_ref[before, _head_cols(kv)] += dv[:BLOCK]

    wide, kv_specs, gain, sink = _attn_specs()
    whole = pl.BlockSpec((T, KV_WIDTH), lambda n: (0, 0))
    outs, extra = _call(
        body, name, (T // BLOCK,), [wide] + kv_specs + [wide, gain, gain, sink],
        [wide, whole, whole, gain, gain, sink],
        [jax.ShapeDtypeStruct((T, ATTN_WIDTH), BF16), jax.ShapeDtypeStruct((T, KV_WIDTH), F32),
         jax.ShapeDtypeStruct((T, KV_WIDTH), F32), jax.ShapeDtypeStruct((1, HEAD_DIM), F32),
         jax.ShapeDtypeStruct((1, HEAD_DIM), F32), jax.ShapeDtypeStruct((N_Q_HEADS, LANES), F32)],
        [], ("arbitrary",), (z, z, z, z, z, do, qg, kg, _sink_rows(sinks)), comm)
    return outs if comm is None else (outs, extra)


def _ssm_discretize(lam_re, lam_im, log_dt, b_re, b_im, c_re, c_im):
    dt = jnp.exp(log_dt)[:, None]
    mag = jnp.exp(lam_re * dt)
    ar = mag * jnp.cos(lam_im * dt)
    ai = mag * jnp.sin(lam_im * dt)
    den = lam_re * lam_re + lam_im * lam_im
    fr = ((ar - 1.0) * lam_re + ai * lam_im) / den
    fi = (ai * lam_re - (ar - 1.0) * lam_im) / den
    bbar_r = fr[:, :, None] * b_re - fi[:, :, None] * b_im
    bbar_i = fr[:, :, None] * b_im + fi[:, :, None] * b_re
    gl = SSM_GROUPS // SSM_NGB
    eye = jnp.eye(gl, dtype=F32)

    def tiles(a):
        return a.reshape(SSM_NGB, SUBLANES, LANES)

    def bdiag(bb):
        return jnp.einsum("bgph,gk->bghkp", bb.reshape(SSM_NGB, gl, SSM_STATE, SSM_GROUP_CH), eye).reshape(
            SSM_NGB, SSM_GB_CH, SSM_GB_ST)

    def cdiag(cc):
        return jnp.einsum("bghp,gk->bgpkh", cc.reshape(SSM_NGB, gl, SSM_GROUP_CH, SSM_STATE), eye).reshape(
            SSM_NGB, SSM_GB_ST, SSM_GB_CH)

    return tiles(ar), tiles(ai), bdiag(bbar_r), bdiag(bbar_i), cdiag(c_re), cdiag(c_im)


def _to_time_major(dst, val, tt, first_row=0):
    for j in range(SUBLANES):
        dst[pl.ds(first_row + j, tt, stride=SUBLANES), :] = val[:, j * LANES:(j + 1) * LANES]


def _from_time_major(dst, src, tt):
    for j in range(SUBLANES):
        dst[:, j * LANES:(j + 1) * LANES] = src[pl.ds(j, tt, stride=SUBLANES), :]


def _ssm_fwd(z, ar, ai, bbr, bbi, cbr, cbi, dskip, name, comm=None):
    T = z.shape[0]
    tt = min(SSM_TT, T)
    nt = T // tt

    def body(u_ref, ar_ref, ai_ref, br_ref, bi_ref, cr_ref, ci_ref, d_ref, y_ref, sr_ref, si_ref,
             tmr, tmi, car_r, car_i):
        @pl.when(pl.program_id(1) == 0)
        def _():
            car_r[...] = jnp.zeros_like(car_r)
            car_i[...] = jnp.zeros_like(car_i)

        u = u_ref[...]
        u16 = u.astype(BF16)
        _to_time_major(tmr, _dot(u16, br_ref[...]), tt)
        _to_time_major(tmi, _dot(u16, bi_ref[...]), tt)
        a_r = ar_ref[...]
        a_i = ai_ref[...]

        def step(t, carry):
            s_r, s_i = carry
            rows = pl.ds(pl.multiple_of(t * SUBLANES, SUBLANES), SUBLANES)
            n_r = a_r * s_r - a_i * s_i + tmr[rows, :]
            n_i = a_r * s_i + a_i * s_r + tmi[rows, :]
            tmr[rows, :] = n_r
            tmi[rows, :] = n_i
            return n_r, n_i

        s_r, s_i = lax.fori_loop(0, tt, step, (car_r[...], car_i[...]), unroll=8)
        car_r[...] = s_r
        car_i[...] = s_i
        _from_time_major(sr_ref, tmr, tt)
        _from_time_major(si_ref, tmi, tt)
        y_ref[...] = (_dot(sr_ref[...].astype(BF16), cr_ref[...]) - _dot(si_ref[...].astype(BF16), ci_ref[...])
                      + d_ref[...] * u)

    u_spec = pl.BlockSpec((tt, SSM_GB_CH), lambda b, t: (t, OFF_U // SSM_GB_CH + b))
    a_spec = pl.BlockSpec((None, SUBLANES, LANES), lambda b, t: (b, 0, 0))
    b_spec = pl.BlockSpec((None, SSM_GB_CH, SSM_GB_ST), lambda b, t: (b, 0, 0))
    c_spec = pl.BlockSpec((None, SSM_GB_ST, SSM_GB_CH), lambda b, t: (b, 0, 0))
    d_spec = pl.BlockSpec((1, SSM_GB_CH), lambda b, t: (0, b))
    y_spec = pl.BlockSpec((tt, SSM_GB_CH), lambda b, t: (t, b))
    s_spec = pl.BlockSpec((tt, SSM_GB_ST), lambda b, t: (t, b))
    n_state = SSM_NGB * SSM_GB_ST
    outs, extra = _call(
        body, name, (SSM_NGB, nt), [u_spec, a_spec, a_spec, b_spec, b_spec, c_spec, c_spec, d_spec],
        [y_spec, s_spec, s_spec],
        [jax.ShapeDtypeStruct((T, SSM_WIDTH), F32), jax.ShapeDtypeStruct((T, n_state), F32),
         jax.ShapeDtypeStruct((T, n_state), F32)],
        [pltpu.VMEM((tt * SUBLANES, LANES), F32), pltpu.VMEM((tt * SUBLANES, LANES), F32),
         pltpu.VMEM((SUBLANES, LANES), F32), pltpu.VMEM((SUBLANES, LANES), F32)],
        ("parallel", "arbitrary"), (z, ar, ai, bbr, bbi, cbr, cbi, dskip), comm)
    return outs if comm is None else (outs, extra)


def _ssm_bwd(dy, z, s_r, s_i, ar, ai, bbr, bbi, cbr, cbi, dskip, name, comm=None):
    T = z.shape[0]
    tt = min(SSM_TT, T)
    nt = T // tt
    per8 = tt // SUBLANES

    def body(dy_ref, u_ref, sr_ref, si_ref, srp_ref, sip_ref, ar_ref, ai_ref, br_ref, bi_ref, cr_ref, ci_ref, d_ref,
             du_ref, dar_ref, dai_ref, dbr_ref, dbi_ref, dcr_ref, dci_ref, dd_ref,
             tmr, tmi, smr, smi, natr, nati, car_r, car_i):
        tb = pl.program_id(1)
        first_block = tb == nt - 1

        @pl.when(tb == 0)
        def _():
            for ref in (car_r, car_i, dar_ref, dai_ref, dbr_ref, dbi_ref, dcr_ref, dci_ref, dd_ref):
                ref[...] = jnp.zeros_like(ref)

        dy = dy_ref[...]
        dy16 = dy.astype(BF16)
        u = u_ref[...]
        u16 = u.astype(BF16)
        _to_time_major(tmr, _dot(dy16, cr_ref[...], "nt"), tt)
        _to_time_major(tmi, -_dot(dy16, ci_ref[...], "nt"), tt)
        _to_time_major(smr, sr_ref[...], tt, first_row=SUBLANES)
        _to_time_major(smi, si_ref[...], tt, first_row=SUBLANES)
        keep = jnp.where(first_block, 0.0, 1.0)
        for j in range(SUBLANES):
            smr[j:j + 1, :] = srp_ref[SUBLANES - 1:SUBLANES, j * LANES:(j + 1) * LANES] * keep
            smi[j:j + 1, :] = sip_ref[SUBLANES - 1:SUBLANES, j * LANES:(j + 1) * LANES] * keep
        a_r = ar_ref[...]
        a_i = ai_ref[...]

        def step(i, carry):
            n_r, n_i, da_r, da_i = carry
            rows = pl.ds(pl.multiple_of((tt - 1 - i) * SUBLANES, SUBLANES), SUBLANES)
            g_r = tmr[rows, :] + a_r * n_r + a_i * n_i
            g_i = tmi[rows, :] - a_i * n_r + a_r * n_i
            tmr[rows, :] = g_r
            tmi[rows, :] = g_i
            p_r = smr[rows, :]
            p_i = smi[rows, :]
            return g_r, g_i, da_r + g_r * p_r + g_i * p_i, da_i - g_r * p_i + g_i * p_r

        zero = jnp.zeros((SUBLANES, LANES), F32)
        n_r, n_i, da_r, da_i = lax.fori_loop(0, tt, step, (car_r[...], car_i[...], zero, zero), unroll=8)
        car_r[...] = n_r
        car_i[...] = n_i
        dar_ref[...] += da_r
        dai_ref[...] += da_i
        _from_time_major(natr, tmr, tt)
        _from_time_major(nati, tmi, tt)
        dbu_r16 = natr[...].astype(BF16)
        dbu_i16 = nati[...].astype(BF16)
        du_ref[...] = (_dot(dbu_r16, br_ref[...], "nt") + _dot(dbu_i16, bi_ref[...], "nt")
                       + d_ref[...] * dy).astype(BF16)
        dbr_ref[...] += _dot(u16, dbu_r16, "tn")
        dbi_ref[...] += _dot(u16, dbu_i16, "tn")
        dcr_ref[...] += _dot(sr_ref[...].astype(BF16), dy16, "tn")
        dci_ref[...] -= _dot(si_ref[...].astype(BF16), dy16, "tn")
        dd_ref[...] += jnp.sum(dy * u, axis=0, keepdims=True)

    def rev(t):
        return nt - 1 - t

    dy_spec = pl.BlockSpec((tt, SSM_GB_CH), lambda b, t: (rev(t), b))
    u_spec = pl.BlockSpec((tt, SSM_GB_CH), lambda b, t: (rev(t), OFF_U // SSM_GB_CH + b))
    s_spec = pl.BlockSpec((tt, SSM_GB_ST), lambda b, t: (rev(t), b))
    sp_spec = pl.BlockSpec((SUBLANES, SSM_GB_ST), lambda b, t: (jnp.maximum(rev(t) * per8 - 1, 0), b))
    a_spec = pl.BlockSpec((None, SUBLANES, LANES), lambda b, t: (b, 0, 0))
    b_spec = pl.BlockSpec((None, SSM_GB_CH, SSM_GB_ST), lambda b, t: (b, 0, 0))
    c_spec = pl.BlockSpec((None, SSM_GB_ST, SSM_GB_CH), lambda b, t: (b, 0, 0))
    d_spec = pl.BlockSpec((1, SSM_GB_CH), lambda b, t: (0, b))
    tm_shape = pltpu.VMEM((tt * SUBLANES, LANES), F32)
    sm_shape = pltpu.VMEM(((tt + 1) * SUBLANES, LANES), F32)
    nat_shape = pltpu.VMEM((tt, SSM_GB_ST), F32)
    tile = pltpu.VMEM((SUBLANES, LANES), F32)
    outs, extra = _call(
        body, name, (SSM_NGB, nt),
        [dy_spec, u_spec, s_spec, s_spec, sp_spec, sp_spec, a_spec, a_spec, b_spec, b_spec, c_spec, c_spec, d_spec],
        [dy_spec, a_spec, a_spec, b_spec, b_spec, c_spec, c_spec, d_spec],
        [jax.ShapeDtypeStruct((T, SSM_WIDTH), BF16),
         jax.ShapeDtypeStruct((SSM_NGB, SUBLANES, LANES), F32),
         jax.ShapeDtypeStruct((SSM_NGB, SUBLANES, LANES), F32),
         jax.ShapeDtypeStruct((SSM_NGB, SSM_GB_CH, SSM_GB_ST), F32),
         jax.ShapeDtypeStruct((SSM_NGB, SSM_GB_CH, SSM_GB_ST), F32),
         jax.ShapeDtypeStruct((SSM_NGB, SSM_GB_ST, SSM_GB_CH), F32),
         jax.ShapeDtypeStruct((SSM_NGB, SSM_GB_ST, SSM_GB_CH), F32),
         jax.ShapeDtypeStruct((1, SSM_WIDTH), F32)],
        [tm_shape, tm_shape, sm_shape, sm_shape, nat_shape, nat_shape, tile, tile], ("parallel", "arbitrary"),
        (dy, z, s_r, s_i, s_r, s_i, ar, ai, bbr, bbi, cbr, cbi, dskip), comm)
    return outs if comm is None else (outs, extra)


def _glu_fwd(y, w, b, name):
    T, W = y.shape
    tm = _pick(T, 512)

    def body(y_ref, w_ref, b_ref, pre_ref, y3_ref):
        y2 = _gelu(y_ref[...])
        pre = _dot(y2.astype(BF16), w_ref[...]) + b_ref[...]
        pre_ref[...] = pre
        y3_ref[...] = (y2 * _sigmoid(pre)).astype(BF16)

    row = pl.BlockSpec((tm, W), lambda i: (i, 0))
    return pl.pallas_call(
        body, name=name, grid=(T // tm,),
        in_specs=[row, pl.BlockSpec((W, W), lambda i: (0, 0)), pl.BlockSpec((1, W), lambda i: (0, 0))],
        out_specs=[row, row], out_shape=[jax.ShapeDtypeStruct((T, W), F32), jax.ShapeDtypeStruct((T, W), BF16)],
        compiler_params=_params("parallel"))(y, w, b)


def _glu_bwd_gate(dy3, y, pre, name):
    T, W = y.shape
    tm = _pick(T, 512)

    def body(dy3_ref, y_ref, pre_ref, dpre_ref, t1_ref, y2_ref, db_ref):
        @pl.when(pl.program_id(0) == 0)
        def _():
            db_ref[...] = jnp.zeros_like(db_ref)

        y2 = _gelu(y_ref[...])
        sg = _sigmoid(pre_ref[...])
        dy3 = dy3_ref[...]
        dpre = dy3 * y2 * sg * (1.0 - sg)
        dpre_ref[...] = dpre.astype(BF16)
        t1_ref[...] = dy3 * sg
        y2_ref[...] = y2.astype(BF16)
        db_ref[...] += jnp.sum(dpre, axis=0, keepdims=True)

    row = pl.BlockSpec((tm, W), lambda i: (i, 0))
    vec = pl.BlockSpec((1, W), lambda i: (0, 0))
    return pl.pallas_call(
        body, name=name, grid=(T // tm,), in_specs=[row, row, row], out_specs=[row, row, row, vec],
        out_shape=[jax.ShapeDtypeStruct((T, W), BF16), jax.ShapeDtypeStruct((T, W), F32),
                   jax.ShapeDtypeStruct((T, W), BF16), jax.ShapeDtypeStruct((1, W), F32)],
        compiler_params=_params("arbitrary"))(dy3, y, pre)


def _glu_bwd_in(dpre, w, t1, y, name):
    T, W = y.shape
    tm = _pick(T, 512)

    def body(dpre_ref, w_ref, t1_ref, y_ref, dy_ref):
        dy_ref[...] = (_dot(dpre_ref[...], w_ref[...], "nt") + t1_ref[...]) * _gelu_grad(y_ref[...])

    row = pl.BlockSpec((tm, W), lambda i: (i, 0))
    return pl.pallas_call(
        body, name=name, grid=(T // tm,), in_specs=[row, pl.BlockSpec((W, W), lambda i: (0, 0)), row, row],
        out_specs=row, out_shape=jax.ShapeDtypeStruct((T, W), F32),
        compiler_params=_params("parallel"))(dpre, w, t1, y)


def _merge_fwd(ya, y3, wa, ws, z, bias, name, comm=None):
    T, W = ya.shape
    D = wa.shape[1]
    tm, tn = _pick(T, 512), _pick(D, 512)

    def body(ya_ref, y3_ref, wa_ref, ws_ref, za_ref, zs_ref, ba_ref, bs_ref, a_ref, b_ref, m_ref):
        a = _dot(ya_ref[...], wa_ref[...])
        b = _dot(y3_ref[...], ws_ref[...])
        a_ref[...] = a
        b_ref[...] = b
        m_ref[...] = (_sigmoid(za_ref[...] + ba_ref[...]) * a + _sigmoid(zs_ref[...] + bs_ref[...]) * b).astype(BF16)

    act = pl.BlockSpec((tm, W), lambda i, j: (i, 0))
    wgt = pl.BlockSpec((W, tn), lambda i, j: (0, j))
    za = pl.BlockSpec((tm, tn), lambda i, j: (i, OFF_G // tn + j))
    zs = pl.BlockSpec((tm, tn), lambda i, j: (i, (OFF_G + D) // tn + j))
    ba = pl.BlockSpec((1, tn), lambda i, j: (0, j))
    bs = pl.BlockSpec((1, tn), lambda i, j: (0, D // tn + j))
    out = pl.BlockSpec((tm, tn), lambda i, j: (i, j))
    outs, extra = _call(
        body, name, (T // tm, D // tn), [act, act, wgt, wgt, za, zs, ba, bs], [out, out, out],
        [jax.ShapeDtypeStruct((T, D), F32), jax.ShapeDtypeStruct((T, D), F32), jax.ShapeDtypeStruct((T, D), BF16)],
        [], ("parallel", "parallel"), (ya, y3, wa, ws, z, z, bias, bias), comm)
    return outs if comm is None else (outs, extra)


def _merge_bwd(dm, a, b, z, bias, name):
    T, D = dm.shape
    tm, tn = _pick(T, 512), _pick(D, 512)

    def body(dm_ref, a_ref, b_ref, za_ref, zs_ref, ba_ref, bs_ref, da_ref, db_ref, dza_ref, dzs_ref, dba_ref, dbs_ref):
        @pl.when(pl.program_id(1) == 0)
        def _():
            dba_ref[...] = jnp.zeros_like(dba_ref)
            dbs_ref[...] = jnp.zeros_like(dbs_ref)

        dm = dm_ref[...]
        sa = _sigmoid(za_ref[...] + ba_ref[...])
        ss = _sigmoid(zs_ref[...] + bs_ref[...])
        da_ref[...] = (dm * sa).astype(BF16)
        db_ref[...] = (dm * ss).astype(BF16)
        dza = dm * a_ref[...] * sa * (1.0 - sa)
        dzs = dm * b_ref[...] * ss * (1.0 - ss)
        dza_ref[...] = dza.astype(BF16)
        dzs_ref[...] = dzs.astype(BF16)
        dba_ref[...] += jnp.sum(dza, axis=0, keepdims=True)
        dbs_ref[...] += jnp.sum(dzs, axis=0, keepdims=True)

    blk = pl.BlockSpec((tm, tn), lambda j, i: (i, j))
    za = pl.BlockSpec((tm, tn), lambda j, i: (i, OFF_G // tn + j))
    zs = pl.BlockSpec((tm, tn), lambda j, i: (i, (OFF_G + D) // tn + j))
    ba = pl.BlockSpec((1, tn), lambda j, i: (0, j))
    bs = pl.BlockSpec((1, tn), lambda j, i: (0, D // tn + j))
    big = jax.ShapeDtypeStruct((T, D), BF16)
    vec = jax.ShapeDtypeStruct((1, D), F32)
    return pl.pallas_call(
        body, name=name, grid=(D // tn, T // tm), in_specs=[blk, blk, blk, za, zs, ba, bs],
        out_specs=[blk, blk, blk, blk, ba, ba], out_shape=[big, big, big, big, vec, vec],
        compiler_params=_params("parallel", "arbitrary"))(dm, a, b, z, z, bias, bias)


_HALF = N_DEV // 2


def _wgu_block(d):
    return d // 2, d % 2


def _ffn_fwd(h, wgu, name, comm=None):
    T, D = h.shape
    n = wgu.shape[3]
    F = _HALF * n
    tm = _pick(T, 512)

    def body(h_ref, wg_ref, wu_ref, g_ref, u_ref, act_ref):
        hv = h_ref[...]
        g = _dot(hv, wg_ref[...])
        u = _dot(hv, wu_ref[...])
        g_ref[...] = g
        u_ref[...] = u
        act_ref[...] = (g * _sigmoid(g) * u).astype(BF16)

    wg = pl.BlockSpec((None, None, D, n), lambda j, i: (*_wgu_block(j), 0, 0))
    wu = pl.BlockSpec((None, None, D, n), lambda j, i: (*_wgu_block(j + _HALF), 0, 0))
    out = pl.BlockSpec((tm, n), lambda j, i: (i, j))
    outs, extra = _call(
        body, name, (_HALF, T // tm), [pl.BlockSpec((tm, D), lambda j, i: (i, 0)), wg, wu], [out, out, out],
        [jax.ShapeDtypeStruct((T, F), F32), jax.ShapeDtypeStruct((T, F), F32), jax.ShapeDtypeStruct((T, F), BF16)],
        [], ("parallel", "parallel"), (h, wgu, wgu), comm)
    return outs if comm is None else (outs, extra)


def _ffn_bwd_in(dg, du, wgu, name, comm=None):
    T, F = dg.shape
    D, n = wgu.shape[2], wgu.shape[3]
    tm, tn = _pick(T, 1024), _pick(D, 1024)

    def body(dg_ref, du_ref, w_ref, o_ref, acc_ref):
        k = pl.program_id(2)

        @pl.when(k == 0)
        def _():
            acc_ref[...] = jnp.zeros_like(acc_ref)

        @pl.when(k < _HALF)
        def _():
            acc_ref[...] += _dot(dg_ref[...], w_ref[...], "nt")

        @pl.when(k >= _HALF)
        def _():
            acc_ref[...] += _dot(du_ref[...], w_ref[...], "nt")

        @pl.when(k == N_DEV - 1)
        def _():
            o_ref[...] = acc_ref[...]

    dg_spec = pl.BlockSpec((tm, n), lambda i, j, k: (i, jnp.minimum(k, _HALF - 1)))
    du_spec = pl.BlockSpec((tm, n), lambda i, j, k: (i, jnp.maximum(k - _HALF, 0)))
    w_spec = pl.BlockSpec((None, None, tn, n), lambda i, j, k: (*_wgu_block(k), j, 0))
    o_spec = pl.BlockSpec((tm, tn), lambda i, j, k: (i, j))
    (out,), extra = _call(
        body, name, (T // tm, D // tn, N_DEV), [dg_spec, du_spec, w_spec], [o_spec],
        [jax.ShapeDtypeStruct((T, D), F32)], [pltpu.VMEM((tm, tn), F32)], ("parallel", "parallel", "arbitrary"),
        (dg, du, wgu), comm)
    return out if comm is None else (out, extra)


def _ffn_bwd_w(h, dg, du, name, comm=None):
    T, D = h.shape
    n = dg.shape[1] // _HALF
    tm, tk = _pick(D, 1024), _pick(T, 2048)
    nk = T // tk

    def body(h_ref, dg_ref, du_ref, o_ref, acc_ref):
        j, k = pl.program_id(0), pl.program_id(2)

        @pl.when(k == 0)
        def _():
            acc_ref[...] = jnp.zeros_like(acc_ref)

        @pl.when(j < _HALF)
        def _():
            acc_ref[...] += _dot(h_ref[...], dg_ref[...], "tn")

        @pl.when(j >= _HALF)
        def _():
            acc_ref[...] += _dot(h_ref[...], du_ref[...], "tn")

        @pl.when(k == nk - 1)
        def _():
            o_ref[...] = acc_ref[...].astype(BF16)

    h_spec = pl.BlockSpec((tk, tm), lambda j, i, k: (k, i))
    dg_spec = pl.BlockSpec((tk, n), lambda j, i, k: (jnp.where(j < _HALF, k, nk - 1), jnp.minimum(j, _HALF - 1)))
    du_spec = pl.BlockSpec((tk, n), lambda j, i, k: (jnp.where(j >= _HALF, k, 0), jnp.maximum(j - _HALF, 0)))
    o_spec = pl.BlockSpec((None, None, tm, n), lambda j, i, k: (*_wgu_block(j), i, 0))
    (out,), extra = _call(
        body, name, (N_DEV, D // tm, nk), [h_spec, dg_spec, du_spec], [o_spec],
        [jax.ShapeDtypeStruct((_HALF, 2, D, n), BF16)], [pltpu.VMEM((tm, n), F32)],
        ("parallel", "parallel", "arbitrary"), (h, dg, du), comm)
    return out if comm is None else (out, extra)


def _ffn_bwd_act(dx, wo, g, u, name, comm=None):
    T, D = dx.shape
    F = wo.shape[0]
    tm, tn = _pick(T, 1024), _pick(F, 512)

    def body(dx_ref, wo_ref, g_ref, u_ref, dg_ref, du_ref):
        dact = _dot(dx_ref[...].astype(BF16), wo_ref[...], "nt")
        gv = g_ref[...]
        sg = _sigmoid(gv)
        dg_ref[...] = (dact * u_ref[...] * sg * (1.0 + gv * (1.0 - sg))).astype(BF16)
        du_ref[...] = (dact * gv * sg).astype(BF16)

    out = pl.BlockSpec((tm, tn), lambda i, j: (i, j))
    big = jax.ShapeDtypeStruct((T, F), BF16)
    outs, extra = _call(
        body, name, (T // tm, F // tn),
        [pl.BlockSpec((tm, D), lambda i, j: (i, 0)), pl.BlockSpec((tn, D), lambda i, j: (j, 0)), out, out],
        [out, out], [big, big], [], ("parallel", "parallel"), (dx, wo, g, u), comm)
    return outs if comm is None else (outs, extra)


def _place():
    x, y, c = lax.axis_index("x"), lax.axis_index("y"), lax.axis_index("c")
    other_chips = [(1 - x, y), (x, 1 - y), (1 - x, 1 - y)]
    return x, y, c, 2 * x + y, other_chips


_ANY = pl.BlockSpec(memory_space=pl.ANY)
_N_COPIES = 7


def _all_gather(shards, name):
    n = len(shards)

    def body(*refs):
        ins, outs = refs[:n], refs[n:2 * n]
        send_sems, recv_sems, local_sems = refs[2 * n:]
        x, y, c, chip, other_chips = _place()
        sibling = (x, y, 1 - c)

        def remote(src, dst, a, j, dev):
            return pltpu.make_async_remote_copy(src_ref=src, dst_ref=dst, send_sem=send_sems.at[a, j],
                                                recv_sem=recv_sems.at[a, j], device_id=dev, device_id_type=MESH)

        sends, local = [], []
        for a in range(n):
            mine = outs[a].at[chip, c]
            local.append(pltpu.make_async_copy(ins[a], mine, local_sems.at[a]))
            local[a].start()
            for j, (ox, oy) in enumerate(other_chips):
                sends.append(remote(ins[a], mine, a, 1 + j, (ox, oy, c)))
                sends[-1].start()
            sends.append(remote(ins[a], mine, a, 0, sibling))
            sends[-1].start()
        for a in range(n):
            for j, (ox, oy) in enumerate(other_chips):
                slot = outs[a].at[2 * ox + oy, c]
                remote(ins[a], slot, a, 1 + j, (ox, oy, c)).wait_recv()
                sends.append(remote(slot, slot, a, 4 + j, sibling))
                sends[-1].start()
        for a in range(n):
            remote(ins[a], outs[a].at[chip, 1 - c], a, 0, sibling).wait_recv()
            for j, (ox, oy) in enumerate(other_chips):
                remote(ins[a], outs[a].at[2 * ox + oy, 1 - c], a, 4 + j, sibling).wait_recv()
        for cp in sends:
            cp.wait_send()
        for a in range(n):
            local[a].wait()

    return pl.pallas_call(
        body, name=name, in_specs=[_ANY] * n, out_specs=[_ANY] * n,
        out_shape=[jax.ShapeDtypeStruct((4, 2) + s.shape, s.dtype) for s in shards],
        scratch_shapes=[pltpu.SemaphoreType.DMA((n, _N_COPIES)), pltpu.SemaphoreType.DMA((n, _N_COPIES)),
                        pltpu.SemaphoreType.DMA((n,))])(*shards)


def _pair_exchange(parts, name):
    n = len(parts)

    def body(*refs):
        ins, theirs = refs[:n], refs[n:2 * n]
        send_sems, recv_sems = refs[2 * n:]
        x, y, c, _, _ = _place()
        sends = []
        for a in range(n):
            for k in range(4):
                sends.append(pltpu.make_async_remote_copy(
                    src_ref=ins[a].at[k, 1 - c], dst_ref=theirs[a].at[k], send_sem=send_sems.at[a, k],
                    recv_sem=recv_sems.at[a, k], device_id=(x, y, 1 - c), device_id_type=MESH))
                sends[-1].start()
        for cp in sends:
            cp.wait_recv()
            cp.wait_send()

    return pl.pallas_call(
        body, name=name, in_specs=[_ANY] * n, out_specs=[_ANY] * n,
        out_shape=[jax.ShapeDtypeStruct((4,) + p.shape[2:], p.dtype) for p in parts],
        scratch_shapes=[pltpu.SemaphoreType.DMA((n, 4)), pltpu.SemaphoreType.DMA((n, 4))])(*parts)


def _chip_exchange(parts, name):
    n = len(parts)

    def body(*refs):
        ins, got = refs[:n], refs[n:2 * n]
        send_sems, recv_sems = refs[2 * n:]
        _, _, c, _, other_chips = _place()
        sends = []
        for a in range(n):
            for j, (ox, oy) in enumerate(other_chips):
                sends.append(pltpu.make_async_remote_copy(
                    src_ref=ins[a].at[2 * ox + oy], dst_ref=got[a].at[j], send_sem=send_sems.at[a, j],
                    recv_sem=recv_sems.at[a, j], device_id=(ox, oy, c), device_id_type=MESH))
                sends[-1].start()
        for cp in sends:
            cp.wait_recv()
            cp.wait_send()

    return pl.pallas_call(
        body, name=name, in_specs=[_ANY] * n, out_specs=[_ANY] * n,
        out_shape=[jax.ShapeDtypeStruct((3,) + p.shape[1:], p.dtype) for p in parts],
        scratch_shapes=[pltpu.SemaphoreType.DMA((n, 3)), pltpu.SemaphoreType.DMA((n, 3))])(*parts)


def _remote(src, dst, send_sems, recv_sems, a, j, dev):
    return pltpu.make_async_remote_copy(src_ref=src, dst_ref=dst, send_sem=send_sems.at[a, j],
                                        recv_sem=recv_sems.at[a, j], device_id=dev, device_id_type=MESH)


def _gather_send(shards):
    n = len(shards)

    def copies(cin, cout, sems, arriving):
        send_sems, recv_sems, local_sems = sems
        x, y, c, chip, other_chips = _place()
        sibling = (x, y, 1 - c)
        peers = [(0, sibling, (chip, 1 - c))] + [(1 + j, (ox, oy, c), (2 * ox + oy, c))
                                                 for j, (ox, oy) in enumerate(other_chips)]
        sends, recvs, local = [], [], []
        for a in range(n):
            mine = cout[a].at[chip, c]
            local.append(pltpu.make_async_copy(cin[a], mine, local_sems.at[a]))
            for j, dev, slot in peers:
                sends.append(_remote(cin[a], mine, send_sems, recv_sems, a, j, dev))
                if arriving:
                    recvs.append(_remote(cin[a], cout[a].at[slot], send_sems, recv_sems, a, j, dev))
        return sends, recvs, local

    return _Comm(shards, [jax.ShapeDtypeStruct((4, 2) + s.shape, s.dtype) for s in shards],
                 [pltpu.SemaphoreType.DMA((n, 4)), pltpu.SemaphoreType.DMA((n, 4)), pltpu.SemaphoreType.DMA((n,))],
                 copies)


def _gather_pass(gathered):
    n = len(gathered)

    def copies(cin, cout, sems, arriving):
        send_sems, recv_sems = sems
        x, y, c, _, other_chips = _place()
        sibling = (x, y, 1 - c)
        sends, recvs = [], []
        for a in range(n):
            for j, (ox, oy) in enumerate(other_chips):
                k = 2 * ox + oy
                sends.append(_remote(cout[a].at[k, c], cout[a].at[k, c], send_sems, recv_sems, a, j, sibling))
                if arriving:
                    recvs.append(_remote(cout[a].at[k, c], cout[a].at[k, 1 - c], send_sems, recv_sems, a, j, sibling))
        return sends, recvs, []

    return _Comm(gathered, [jax.ShapeDtypeStruct(g.shape, g.dtype) for g in gathered],
                 [pltpu.SemaphoreType.DMA((n, 3)), pltpu.SemaphoreType.DMA((n, 3))], copies,
                 aliases={i: i for i in range(n)})


def _scatter_send(sums):
    n = len(sums)

    def copies(cin, cout, sems, arriving):
        send_sems, recv_sems = sems
        _, _, c, _, other_chips = _place()
        sends = [_remote(cin[a].at[2 * ox + oy], cout[a].at[j], send_sems, recv_sems, a, j, (ox, oy, c))
                 for a in range(n) for j, (ox, oy) in enumerate(other_chips)]
        return sends, sends, []

    return _Comm(sums, [jax.ShapeDtypeStruct((3,) + s.shape[1:], s.dtype) for s in sums],
                 [pltpu.SemaphoreType.DMA((n, 3)), pltpu.SemaphoreType.DMA((n, 3))], copies)


def _join(comms):
    if len(comms) == 1:
        return comms[0]
    args, outs, sems, aliases, spans = [], [], [], {}, []
    for cm in comms:
        spans.append((len(args), len(outs), len(sems)))
        aliases.update({len(args) + i: len(outs) + o for i, o in cm.aliases.items()})
        args, outs, sems = args + cm.args, outs + cm.out_shapes, sems + cm.sems

    def copies(cin, cout, sem_refs, arriving):
        sends, recvs, local = [], [], []
        for cm, (a0, o0, s0) in zip(comms, spans):
            part = cm.copies(cin[a0:a0 + len(cm.args)], cout[o0:o0 + len(cm.out_shapes)],
                             sem_refs[s0:s0 + len(cm.sems)], arriving)
            sends, recvs, local = sends + part[0], recvs + part[1], local + part[2]
        return sends, recvs, local

    return _Comm(args, outs, sems, copies, aliases)


class _Schedule:
    def __init__(self):
        self.rides = {}

    def ride(self, host, make, names, operands):
        results = {}
        self.rides.setdefault(host, []).append((make, names, operands, results))
        return results

    def carry(self, host, fn, *args, **kwargs):
        rides = self.rides.get(host)
        if not rides:
            return fn(*args, host, **kwargs)
        comm = _join([make([operands[n] for n in names]) for make, names, operands, _ in rides])
        out, extra = fn(*args, host, comm=comm, **kwargs)
        for _, names, _, results in rides:
            results.update(zip(names, extra[:len(names)]))
            extra = extra[len(names):]
        return out


def _add_pair(core, parts, theirs, name):
    k, _, R, C = parts.shape
    tr = _pick(R, 512, 16)

    def body(core_ref, a_ref, b_ref, o_ref):
        o_ref[...] = (a_ref[...].astype(F32) + b_ref[...].astype(F32)).astype(BF16)

    blk = pl.BlockSpec((None, tr, C), lambda s, i, core_ref: (s, i, 0))
    grid_spec = pltpu.PrefetchScalarGridSpec(
        num_scalar_prefetch=1, grid=(k, R // tr),
        in_specs=[pl.BlockSpec((None, None, tr, C), lambda s, i, core_ref: (s, core_ref[0], i, 0)), blk],
        out_specs=blk)
    return pl.pallas_call(
        body, name=name, grid_spec=grid_spec, out_shape=jax.ShapeDtypeStruct(theirs.shape, BF16),
        compiler_params=_params("parallel", "parallel"))(core, parts, theirs)


def _adamw_math(w, g, m, v):
    m = ADAM_B1 * m + (1.0 - ADAM_B1) * g
    v = ADAM_B2 * v + (1.0 - ADAM_B2) * (g * g)
    m_hat = m / (1.0 - ADAM_B1 ** ADAM_STEP)
    v_hat = v / (1.0 - ADAM_B2 ** ADAM_STEP)
    delta = -ADAM_LR * (m_hat / (jnp.sqrt(v_hat) + ADAM_EPS) + ADAM_WD * w)
    return delta, m, v


def _scattered_pieces(sums, got):
    return [("own", sums)] + [("peer%d" % j, got) for j in range(3)]


def _piece_spec(kind, tr, C):
    if kind == "own":
        return pl.BlockSpec((None, tr, C), lambda i, chip_ref: (chip_ref[0], i, 0))
    if kind == "plain":
        return pl.BlockSpec((tr, C), lambda i, chip_ref: (i, 0))
    return pl.BlockSpec((None, tr, C), functools.partial(lambda j, i, chip_ref: (j, i, 0), int(kind[-1])))


def _adamw_shard(chip, w, m, v, layer, pieces, so_far, name):
    L, R, C = w.shape
    tr = _pick(R, 256, 16)
    n_p = len(pieces)

    def body(chip_ref, w_ref, m_ref, v_ref, *rest):
        g = rest[0][...].astype(F32)
        for p in rest[1:n_p]:
            g = g + p[...].astype(F32)
        delta, nm, nv = _adamw_math(w_ref[...], g, m_ref[...], v_ref[...])
        g_ref, d_ref, nm_ref, nv_ref = rest[-4:]
        g_ref[...] = g
        d_ref[...] = delta
        nm_ref[...] = nm
        nv_ref[...] = nv

    state = pl.BlockSpec((None, tr, C), lambda i, chip_ref: (layer, i, 0))
    carried = [] if so_far is None else list(so_far)
    first_carried = 1 + 3 + n_p
    grid_spec = pltpu.PrefetchScalarGridSpec(
        num_scalar_prefetch=1, grid=(R // tr,),
        in_specs=[state] * 3 + [_piece_spec(kind, tr, C) for kind, _ in pieces] + [_ANY] * len(carried),
        out_specs=[state] * 4)
    return pl.pallas_call(
        body, name=name, grid_spec=grid_spec, out_shape=[jax.ShapeDtypeStruct((L, R, C), F32)] * 4,
        input_output_aliases={first_carried + t: t for t in range(len(carried))},
        compiler_params=_params("parallel"))(chip, w, m, v, *[a for _, a in pieces], *carried)


def _adamw_small(w, m, v, gathered, name):
    R = w.shape[0]
    tr = _pick(R, 512, SUBLANES)

    def body(w_ref, m_ref, v_ref, gg_ref, g_ref, d_ref, nm_ref, nv_ref):
        g = gg_ref[0, 0]
        for k in range(4):
            for c in range(2):
                if c or k:
                    g = g + gg_ref[k, c]
        delta, nm, nv = _adamw_math(w_ref[...], g, m_ref[...], v_ref[...])
        g_ref[...] = g
        d_ref[...] = delta
        nm_ref[...] = nm
        nv_ref[...] = nv

    row = pl.BlockSpec((tr, LANES), lambda i: (i, 0))
    out = jax.ShapeDtypeStruct((R, LANES), F32)
    return pl.pallas_call(
        body, name=name, grid=(R // tr,),
        in_specs=[row, row, row, pl.BlockSpec((4, 2, tr, LANES), lambda i: (0, 0, i, 0))], out_specs=[row] * 4,
        out_shape=[out] * 4, compiler_params=_params("parallel"))(w, m, v, gathered)


def _layer_fwd(x, w, s, tag, sched):
    h = _rms_fwd(x, s["norm_mix_g"], f"rms_mix_{tag}")
    z = sched.carry(f"in_proj_{tag}", _mm, h, w["w_in_t"], "nt", F32, tn=512)
    q, k, v = (z[:, a:b].reshape(z.shape[0], -1, HEAD_DIM).transpose(1, 0, 2)
               for a, b in ((0, OFF_K), (OFF_K, OFF_V), (OFF_V, OFF_U)))
    ya = sched.carry(f"attn_fwd_{tag}", _attn_fwd, q, k, v, s["q_norm_g"], s["k_norm_g"], s["attn_sinks"])
    ya = ya.transpose(1, 0, 2).reshape(z.shape[0], ATTN_WIDTH)
    y, s_r, s_i = sched.carry(f"ssm_fwd_{tag}", _ssm_fwd, z, *s["ssm16"], s["ssm_d"])
    pre, y3 = _glu_fwd(y, w["ssm_glu_w"], s["ssm_glu_b"], f"glu_fwd_{tag}")
    a, b, merged = sched.carry(f"merge_fwd_{tag}", _merge_fwd, ya, y3, w["w_attn_branch"], w["w_ssm_branch"], z,
                               s["gate_bias"])
    x1 = _mm(merged, w["w_out"], "nn", F32, f"out_proj_{tag}", residual=x)
    h2 = _rms_fwd(x1, s["norm_ffn_g"], f"rms_ffn_{tag}")
    g, u, act = sched.carry(f"ffn_fwd_{tag}", _ffn_fwd, h2, w["w_ffn_in"])
    x2 = sched.carry(f"ffn_out_{tag}", _mm, act, w["w_ffn_out"], "nn", F32, residual=x1)
    saved = dict(x=x, h=h, z=z, ya=ya, y=y, s_r=s_r, s_i=s_i, pre=pre, y3=y3, a=a, b=b, merged=merged,
                 x1=x1, h2=h2, g=g, u=u, act=act)
    return x2, saved


def _layer_bwd(dx2, sv, w, s, tag, sched, scatter):
    gw, gs = {}, {}
    dg16, du16 = _ffn_bwd_act(dx2, w["w_ffn_out"], sv["g"], sv["u"], f"ffn_bwd_act_{tag}")
    gw["w_ffn_out"] = _mm(sv["act"], dx2, "tn", BF16, f"dw_ffn_out_{tag}", tm=1408)
    dh2 = _ffn_bwd_in(dg16, du16, w["w_ffn_in"], f"dh2_{tag}")
    gw["w_ffn_in"] = _ffn_bwd_w(sv["h2"], dg16, du16, f"dw_ffn_in_{tag}")
    scatter(FFN_WEIGHTS, gw, f"attn_bwd_{tag}")
    dx1, gs["norm_ffn_g"] = _rms_bwd(dh2, sv["x1"], s["norm_ffn_g"], dx2, f"rms_ffn_bwd_{tag}")
    dm = _mm(dx1, w["w_out"], "nt", F32, f"dmerged_{tag}")
    gw["w_out"] = _mm(sv["merged"], dx1, "tn", BF16, f"dw_out_{tag}")
    da16, db16, dza, dzs, dba, dbs = _merge_bwd(dm, sv["a"], sv["b"], sv["z"], s["gate_bias"], f"merge_bwd_{tag}")
    gs["gate_bias"] = jnp.concatenate([dba, dbs], axis=1)
    dya = _mm(da16, w["w_attn_branch"], "nt", F32, f"dya_{tag}")
    gw["w_attn_branch"] = _mm(sv["ya"], da16, "tn", BF16, f"dw_attn_branch_{tag}")
    dy3 = _mm(db16, w["w_ssm_branch"], "nt", F32, f"dy3_{tag}")
    gw["w_ssm_branch"] = _mm(sv["y3"], db16, "tn", BF16, f"dw_ssm_branch_{tag}")
    dpre16, t1, y2_16, gs["ssm_glu_b"] = _glu_bwd_gate(dy3, sv["y"], sv["pre"], f"glu_bwd_gate_{tag}")
    dy = _glu_bwd_in(dpre16, w["ssm_glu_w"], t1, sv["y"], f"glu_bwd_in_{tag}")
    gw["ssm_glu_w"] = _mm(y2_16, dpre16, "tn", BF16, f"dw_glu_{tag}")
    scatter(MIXER_WEIGHTS[1:], gw, f"ssm_bwd_{tag}")
    du_ssm, *gs["ssm_disc"], gs["ssm_d"] = sched.carry(f"ssm_bwd_{tag}", _ssm_bwd, dy, sv["z"], sv["s_r"], sv["s_i"],
                                                       *s["ssm16"], s["ssm_d"])
    dq, dk, dv, gs["q_norm_g"], gs["k_norm_g"], dsk = sched.carry(
        f"attn_bwd_{tag}", _attn_bwd, sv["z"], dya, s["q_norm_g"], s["k_norm_g"], s["attn_sinks"])
    gs["attn_sinks"] = dsk[:, 0].reshape(1, N_Q_HEADS)
    dz = [dq, jnp.concatenate([dk, dv], axis=1).astype(BF16), du_ssm, dza, dzs]
    gw["w_in"] = _mm_cols_tn(dz, sv["h"], f"dw_in_t_{tag}", tn=2048)
    scatter(MIXER_WEIGHTS[:1], gw, f"dh_{tag}")
    dh = sched.carry(f"dh_{tag}", _mm_cols_nn, dz, w["w_in_t"])
    dx, gs["norm_mix_g"] = _rms_bwd(dh, sv["x"], s["norm_mix_g"], dx1, f"rms_mix_bwd_{tag}")
    return dx, gs


def _shard_to_send(name, shard):
    return (shard.T if name == "w_in" else shard).astype(BF16)


def _assemble(name, gathered):
    if name == "w_ffn_in":
        return gathered
    if name in COL_SHARDED and name != "w_in":
        rows = gathered.shape[2]
        return gathered.transpose(2, 0, 1, 3).reshape(rows, -1)
    return gathered.reshape(-1, gathered.shape[3])


class _Weights:
    def __init__(self):
        self.sources, self.ready = [], {}

    def __getitem__(self, name):
        if name not in self.ready:
            key = "w_in" if name == "w_in_t" else name
            (gathered,) = [src[key] for src in self.sources if key in src]
            self.ready[name] = _assemble(key, gathered)
        return self.ready[name]


def _disassemble(name, grad):
    if name == "w_ffn_in":
        return grad
    if name in COL_SHARDED and name != "w_in":
        rows, cols = grad.shape
        return grad.reshape(rows, 4, 2, cols // N_DEV).transpose(1, 2, 0, 3)
    rows, cols = grad.shape
    return grad.reshape(4, 2, rows // N_DEV, cols)


def _pack(arrays):
    flat = jnp.concatenate([a.reshape(-1) for a in arrays])
    pad = (-flat.shape[0]) % (SUBLANES * LANES)
    return jnp.pad(flat, (0, pad)).reshape(-1, LANES)


def _unpack(packed, like):
    flat, out, off = packed.reshape(-1), [], 0
    for a in like:
        out.append(flat[off:off + a.size].reshape(a.shape))
        off += a.size
    return out


def kernel(x, norm_mix_g, w_in, gate_bias, q_norm_g, k_norm_g, attn_sinks, ssm_lambda_re, ssm_lambda_im, ssm_log_dt, ssm_b_re, ssm_b_im, ssm_c_re, ssm_c_im, ssm_d, ssm_glu_w, ssm_glu_b, w_attn_branch, w_ssm_branch, w_out, norm_ffn_g, w_ffn_in, w_ffn_out, loss_target, m_norm_mix_g, m_w_in, m_gate_bias, m_q_norm_g, m_k_norm_g, m_attn_sinks, m_ssm_lambda_re, m_ssm_lambda_im, m_ssm_log_dt, m_ssm_b_re, m_ssm_b_im, m_ssm_c_re, m_ssm_c_im, m_ssm_d, m_ssm_glu_w, m_ssm_glu_b, m_w_attn_branch, m_w_ssm_branch, m_w_out, m_norm_ffn_g, m_w_ffn_in, m_w_ffn_out, v_norm_mix_g, v_w_in, v_gate_bias, v_q_norm_g, v_k_norm_g, v_attn_sinks, v_ssm_lambda_re, v_ssm_lambda_im, v_ssm_log_dt, v_ssm_b_re, v_ssm_b_im, v_ssm_c_re, v_ssm_c_im, v_ssm_d, v_ssm_glu_w, v_ssm_glu_b, v_w_attn_branch, v_w_ssm_branch, v_w_out, v_norm_ffn_g, v_w_ffn_in, v_w_ffn_out):
    given = dict(locals())
    wts = {n: given[n] for n in WEIGHTS}
    mom = {n: given["m_" + n] for n in WEIGHTS}
    var = {n: given["v_" + n] for n in WEIGHTS}
    depth = w_in.shape[0]
    xs = x[0]
    target = loss_target[0]

    sched = _Schedule()
    shards = [{n: _shard_to_send(n, wts[n][l]) for n in BIG} for l in range(depth)]
    full = [_Weights() for _ in range(depth)]

    def gather(layer, names, send_host, pass_host):
        sent = sched.ride(send_host, _gather_send, names, shards[layer])
        full[layer].sources.append(sched.ride(pass_host, _gather_pass, names, sent))

    first = MIXER_WEIGHTS[:1]
    full[0].sources.append(dict(zip(first, _all_gather([shards[0][n] for n in first], "gather_w_in_0"))))
    gather(0, MIXER_WEIGHTS[1:], "in_proj_0", "attn_fwd_0")
    gather(0, ("w_ffn_out",), "attn_fwd_0", "ssm_fwd_0")
    gather(0, ("w_ffn_in",), "ssm_fwd_0", "merge_fwd_0")
    for l in range(1, depth):
        gather(l, first, f"ffn_fwd_{l - 1}", f"ffn_out_{l - 1}")
        gather(l, MIXER_WEIGHTS[1:], f"ffn_out_{l - 1}", f"in_proj_{l}")
        gather(l, ("w_ffn_out",), f"in_proj_{l}", f"attn_fwd_{l}")
        gather(l, ("w_ffn_in",), f"attn_fwd_{l}", f"ssm_fwd_{l}")

    small, disc_vjp = [], []
    for l in range(depth):
        s = {n: wts[n][l].reshape(1, -1) for n in ("norm_mix_g", "gate_bias", "q_norm_g", "k_norm_g", "attn_sinks",
                                                   "ssm_d", "ssm_glu_b", "norm_ffn_g")}
        disc, vjp = jax.vjp(_ssm_discretize, *[wts[n][l] for n in ("ssm_lambda_re", "ssm_lambda_im", "ssm_log_dt",
                                                                  "ssm_b_re", "ssm_b_im", "ssm_c_re", "ssm_c_im")])
        s["ssm16"] = (disc[0], disc[1]) + tuple(d.astype(BF16) for d in disc[2:])
        small.append(s)
        disc_vjp.append(vjp)

    act, saved = xs, []
    for l in range(depth):
        act, sv = _layer_fwd(act, full[l], small[l], str(l), sched)
        saved.append(sv)
    dact, loss_local = _loss_grad(act, target, "loss_head")

    out = {"grad": {}, "delta": {}, "new_m": {}, "new_v": {}}
    results = {n: None for n in BIG}
    core = lax.axis_index("c").astype(jnp.int32).reshape(1)
    chip = (2 * lax.axis_index("x") + lax.axis_index("y")).astype(jnp.int32).reshape(1)
    state = {n: [a.transpose(0, 2, 1) if n == "w_in" else a for a in (wts[n], mom[n], var[n])] for n in BIG}
    small_grads = [None] * depth
    for l in reversed(range(depth)):
        scattered = []

        def scatter(names, grads, host):
            parts = [_disassemble(n, grads[n]) for n in names]
            theirs = _pair_exchange(parts, f"grad_pair_exchange_{host}")
            sums = {n: _add_pair(core, p, t, f"grad_pair_sum_{n}_{host}") for n, p, t in zip(names, parts, theirs)}
            scattered.append((sums, sched.ride(host, _scatter_send, names, sums)))

        dact, gs = _layer_bwd(dact, saved[l], full[l], small[l], str(l), sched, scatter)
        (gs["ssm_lambda_re"], gs["ssm_lambda_im"], gs["ssm_log_dt"], gs["ssm_b_re"], gs["ssm_b_im"], gs["ssm_c_re"],
         gs["ssm_c_im"]) = disc_vjp[l](tuple(gs.pop("ssm_disc")))
        small_grads[l] = gs
        for sums, got in scattered:
            for n in got:
                results[n] = _adamw_shard(chip, state[n][0], state[n][1], state[n][2], l,
                                          _scattered_pieces(sums[n], got[n]), results[n], f"adamw_{n}_{l}")
    loss = lax.psum(loss_local, ("x", "y", "c"))
    for n in BIG:
        for kind, res in zip(("grad", "delta", "new_m", "new_v"), results[n]):
            out[kind][n] = res.transpose(0, 2, 1) if n == "w_in" else res

    like = [wts[n] for n in SMALL]
    g_small = _pack([jnp.stack([small_grads[l][n].reshape(wts[n].shape[1:]) for l in range(depth)]) for n in SMALL])
    (gathered_small,) = _all_gather([g_small], "gather_small_grads")
    res = _adamw_small(_pack(like), _pack([mom[n] for n in SMALL]), _pack([var[n] for n in SMALL]), gathered_small,
                       "adamw_small")
    for kind, packed in zip(("grad", "delta", "new_m", "new_v"), res):
        for n, a in zip(SMALL, _unpack(packed, like)):
            out[kind][n] = a

    grad_x = dact.reshape(x.shape)
    return (loss, grad_x, *[out["grad"][n] for n in WEIGHTS], *[out["delta"][n] for n in WEIGHTS],
            *[out["new_m"][n] for n in WEIGHTS], *[out["new_v"][n] for n in WEIGHTS])
```

```python
import functools
import math

import jax
import jax.numpy as jnp
from jax import lax
from jax.experimental import pallas as pl
from jax.experimental.pallas import tpu as pltpu

F32, BF16 = jnp.float32, jnp.bfloat16
MESH = pl.DeviceIdType.MESH

D_MODEL = 2048
HEAD_DIM = 64
N_Q_HEADS = 16
N_KV_HEADS = 4
GQA_GROUP = N_Q_HEADS // N_KV_HEADS
ATTN_WIDTH = N_Q_HEADS * HEAD_DIM
KV_WIDTH = N_KV_HEADS * HEAD_DIM
WINDOW = 128
BLOCK = 128
SSM_WIDTH = D_MODEL // 2
SSM_GROUP_CH = 16
SSM_GROUPS = SSM_WIDTH // SSM_GROUP_CH
SSM_STATE = 64
D_FF = 5632
OFF_K = ATTN_WIDTH
OFF_V = OFF_K + KV_WIDTH
OFF_U = OFF_V + KV_WIDTH
OFF_G = OFF_U + SSM_WIDTH
IN_WIDTH = OFF_G + 2 * D_MODEL
RMS_EPS = 1e-6
ATTN_SCALE = HEAD_DIM ** -0.5
NEG_BIG = -1e30

SSM_NGB = 4
SSM_GB_CH = SSM_WIDTH // SSM_NGB
SSM_GB_ST = SSM_GROUPS * SSM_STATE // SSM_NGB
SUBLANES = 8
LANES = 128
SSM_TT = 256

ADAM_LR = 0.001
ADAM_B1 = 0.9
ADAM_B2 = 0.999
ADAM_EPS = 1e-08
ADAM_WD = 0.01
ADAM_STEP = 10

N_DEV = 8
VMEM_LIMIT_BYTES = 52 * 1024 * 1024

BIG = ("w_in", "ssm_glu_w", "w_attn_branch", "w_ssm_branch", "w_out", "w_ffn_in", "w_ffn_out")
COL_SHARDED = ("w_in", "w_attn_branch", "w_ssm_branch", "w_ffn_in")
FFN_WEIGHTS = ("w_ffn_in", "w_ffn_out")
MIXER_WEIGHTS = ("w_in", "ssm_glu_w", "w_attn_branch", "w_ssm_branch", "w_out")
SMALL = ("norm_mix_g", "gate_bias", "q_norm_g", "k_norm_g", "attn_sinks", "ssm_lambda_re", "ssm_lambda_im",
         "ssm_log_dt", "ssm_b_re", "ssm_b_im", "ssm_c_re", "ssm_c_im", "ssm_d", "ssm_glu_b", "norm_ffn_g")
WEIGHTS = ("norm_mix_g", "w_in", "gate_bias", "q_norm_g", "k_norm_g", "attn_sinks", "ssm_lambda_re", "ssm_lambda_im",
           "ssm_log_dt", "ssm_b_re", "ssm_b_im", "ssm_c_re", "ssm_c_im", "ssm_d", "ssm_glu_w", "ssm_glu_b",
           "w_attn_branch", "w_ssm_branch", "w_out", "norm_ffn_g", "w_ffn_in", "w_ffn_out")


def _pick(n, target, mult=LANES):
    best = None
    for t in range(mult, min(n, target) + 1, mult):
        if n % t == 0:
            best = t
    return n if best is None else best


def _params(*sem):
    return pltpu.CompilerParams(dimension_semantics=sem, vmem_limit_bytes=VMEM_LIMIT_BYTES)


def _sigmoid(v):
    return 1.0 / (1.0 + jnp.exp(-v))


_GELU_C = math.sqrt(2.0 / math.pi)


def _gelu(v):
    return 0.5 * v * (1.0 + jnp.tanh(_GELU_C * (v + 0.044715 * v * v * v)))


def _gelu_grad(v):
    t = jnp.tanh(_GELU_C * (v + 0.044715 * v * v * v))
    return 0.5 * (1.0 + t) + 0.5 * v * (1.0 - t * t) * _GELU_C * (1.0 + 3.0 * 0.044715 * v * v)


_DN = {"nn": (((1,), (0,)), ((), ())), "nt": (((1,), (1,)), ((), ())), "tn": (((0,), (0,)), ((), ()))}


def _dot(a, b, dims="nn"):
    return lax.dot_general(a, b, _DN[dims], preferred_element_type=F32)


class _Comm:
    def __init__(self, args, out_shapes, sems, copies, aliases=None):
        self.args, self.out_shapes, self.sems = list(args), list(out_shapes), list(sems)
        self.copies, self.aliases = copies, dict(aliases or {})


def _call(body, name, grid, in_specs, out_specs, out_shape, scratch_shapes, semantics, args, comm=None):
    in_specs, out_specs, out_shape = list(in_specs), list(out_specs), list(out_shape)
    scratch_shapes = list(scratch_shapes)
    if comm is None:
        res = pl.pallas_call(body, name=name, grid=grid, in_specs=in_specs, out_specs=out_specs, out_shape=out_shape,
                             scratch_shapes=scratch_shapes, compiler_params=_params(*semantics))(*args)
        return list(res), []
    n_in, n_out, n_scr = len(in_specs), len(out_specs), len(scratch_shapes)
    n_cin, n_cout = len(comm.args), len(comm.out_shapes)

    def carrying(*refs):
        ins, cin = refs[:n_in], refs[n_in:n_in + n_cin]
        o0 = n_in + n_cin
        outs, cout = refs[o0:o0 + n_out], refs[o0 + n_out:o0 + n_out + n_cout]
        s0 = o0 + n_out + n_cout
        scr, sems = refs[s0:s0 + n_scr], refs[s0 + n_scr:]
        first = functools.reduce(jnp.logical_and, [pl.program_id(d) == 0 for d in range(len(grid))])
        last = functools.reduce(jnp.logical_and, [pl.program_id(d) == grid[d] - 1 for d in range(len(grid))])

        @pl.when(first)
        def _():
            sends, _, local = comm.copies(cin, cout, sems, False)
            for cp in local + sends:
                cp.start()

        body(*ins, *outs, *scr)

        @pl.when(last)
        def _():
            sends, recvs, local = comm.copies(cin, cout, sems, True)
            for cp in recvs:
                cp.wait_recv()
            for cp in sends:
                cp.wait_send()
            for cp in local:
                cp.wait()

    res = pl.pallas_call(
        carrying, name=name, grid=grid, in_specs=in_specs + [_ANY] * n_cin, out_specs=out_specs + [_ANY] * n_cout,
        out_shape=out_shape + comm.out_shapes, scratch_shapes=scratch_shapes + comm.sems,
        input_output_aliases={n_in + i: n_out + o for i, o in comm.aliases.items()},
        compiler_params=_params(*["arbitrary"] * len(grid)))(*args, *comm.args)
    return list(res[:n_out]), list(res[n_out:])


def _mm(a, b, dims, out_dtype, name, residual=None, tm=1024, tn=1024, tk=2048, comm=None):
    if dims == "tn":
        K, M = a.shape
    else:
        M, K = a.shape
    N = b.shape[0] if dims == "nt" else b.shape[1]
    tm, tn, tk = _pick(M, tm), _pick(N, tn), _pick(K, tk)
    nk = K // tk
    has_res = residual is not None

    def finish(out, refs):
        if has_res:
            out = out + refs[2][...].astype(F32)
        refs[-2][...] = out.astype(out_dtype)

    def body_single(*refs):
        finish(_dot(refs[0][...].astype(BF16), refs[1][...].astype(BF16), dims), refs)

    def body_multi(*refs):
        acc_ref = refs[-1]
        k = pl.program_id(2)

        @pl.when(k == 0)
        def _():
            acc_ref[...] = jnp.zeros_like(acc_ref)

        acc_ref[...] += _dot(refs[0][...].astype(BF16), refs[1][...].astype(BF16), dims)

        @pl.when(k == nk - 1)
        def _():
            finish(acc_ref[...], refs)

    a_spec = (pl.BlockSpec((tk, tm), lambda i, j, k: (k, i)) if dims == "tn"
              else pl.BlockSpec((tm, tk), lambda i, j, k: (i, k)))
    b_spec = (pl.BlockSpec((tn, tk), lambda i, j, k: (j, k)) if dims == "nt"
              else pl.BlockSpec((tk, tn), lambda i, j, k: (k, j)))
    o_spec = pl.BlockSpec((tm, tn), lambda i, j, k: (i, j))
    in_specs = [a_spec, b_spec] + ([o_spec] if has_res else [])
    args = (a, b) + ((residual,) if has_res else ())
    (out,), extra = _call(
        body_single if nk == 1 else body_multi, name, (M // tm, N // tn, nk), in_specs, [o_spec],
        [jax.ShapeDtypeStruct((M, N), out_dtype)], [pltpu.VMEM((tm, tn) if nk > 1 else (SUBLANES, LANES), F32)],
        ("parallel", "parallel", "arbitrary"), args, comm)
    return out if comm is None else (out, extra)


PIECE_BLOCK = 512


def _piece_blocks(pieces):
    counts = [p.shape[1] // PIECE_BLOCK for p in pieces]
    return counts, [sum(counts[:i]) for i in range(len(counts))]


def _mm_cols_nn(pieces, b, name, comm=None, tm=1024, tn=1024):
    M, N = pieces[0].shape[0], b.shape[1]
    tm, tn = _pick(M, tm), _pick(N, tn)
    counts, starts = _piece_blocks(pieces)
    n, nk = len(pieces), sum(counts)

    def body(*refs):
        b_ref, o_ref, acc_ref = refs[n:]
        k = pl.program_id(2)

        @pl.when(k == 0)
        def _():
            acc_ref[...] = jnp.zeros_like(acc_ref)

        for a_ref, c, s in zip(refs[:n], counts, starts):
            @pl.when((k >= s) & (k < s + c))
            def _(a_ref=a_ref):
                acc_ref[...] += _dot(a_ref[...], b_ref[...])

        @pl.when(k == nk - 1)
        def _():
            o_ref[...] = acc_ref[...]

    a_specs = [pl.BlockSpec((tm, PIECE_BLOCK), functools.partial(lambda s, c, i, j, k: (i, jnp.clip(k - s, 0, c - 1)), s, c))
               for c, s in zip(counts, starts)]
    b_spec = pl.BlockSpec((PIECE_BLOCK, tn), lambda i, j, k: (k, j))
    o_spec = pl.BlockSpec((tm, tn), lambda i, j, k: (i, j))
    (out,), extra = _call(body, name, (M // tm, N // tn, nk), a_specs + [b_spec], [o_spec],
                          [jax.ShapeDtypeStruct((M, N), F32)], [pltpu.VMEM((tm, tn), F32)],
                          ("parallel", "parallel", "arbitrary"), (*pieces, b), comm)
    return out if comm is None else (out, extra)


def _mm_cols_tn(pieces, b, name, tn=1024, tk=2048):
    K, N = b.shape
    tn, tk = _pick(N, tn), _pick(K, tk)
    counts, starts = _piece_blocks(pieces)
    n, nk = len(pieces), K // tk

    def body(*refs):
        b_ref, o_ref, acc_ref = refs[n:]
        i, k = pl.program_id(0), pl.program_id(2)

        @pl.when(k == 0)
        def _():
            acc_ref[...] = jnp.zeros_like(acc_ref)

        for a_ref, c, s in zip(refs[:n], counts, starts):
            @pl.when((i >= s) & (i < s + c))
            def _(a_ref=a_ref):
                acc_ref[...] += _dot(a_ref[...], b_ref[...], "tn")

        @pl.when(k == nk - 1)
        def _():
            o_ref[...] = acc_ref[...].astype(BF16)

    def a_index(s, c, i, j, k):
        mine = (i >= s) & (i < s + c)
        return jnp.where(mine, k, 0), jnp.clip(i - s, 0, c - 1)

    a_specs = [pl.BlockSpec((tk, PIECE_BLOCK), functools.partial(a_index, s, c)) for c, s in zip(counts, starts)]
    b_spec = pl.BlockSpec((tk, tn), lambda i, j, k: (k, j))
    o_spec = pl.BlockSpec((PIECE_BLOCK, tn), lambda i, j, k: (i, j))
    (out,), _ = _call(body, name, (sum(counts), N // tn, nk), a_specs + [b_spec], [o_spec],
                      [jax.ShapeDtypeStruct((sum(counts) * PIECE_BLOCK, N), BF16)],
                      [pltpu.VMEM((PIECE_BLOCK, tn), F32)], ("parallel", "parallel", "arbitrary"), (*pieces, b))
    return out


def _rms_fwd(x, g, name):
    T, D = x.shape
    tr = _pick(T, 256, SUBLANES)

    def body(x_ref, g_ref, o_ref):
        xf = x_ref[...]
        r = lax.rsqrt(jnp.mean(xf * xf, axis=-1, keepdims=True) + RMS_EPS)
        o_ref[...] = (xf * r * g_ref[...]).astype(BF16)

    return pl.pallas_call(
        body, name=name, grid=(T // tr,),
        in_specs=[pl.BlockSpec((tr, D), lambda i: (i, 0)), pl.BlockSpec((1, D), lambda i: (0, 0))],
        out_specs=pl.BlockSpec((tr, D), lambda i: (i, 0)), out_shape=jax.ShapeDtypeStruct((T, D), BF16),
        compiler_params=_params("parallel"))(x, g)


def _rms_bwd(dh, x, g, dres, name):
    T, D = x.shape
    tr = _pick(T, 256, SUBLANES)

    def body(dh_ref, x_ref, g_ref, dres_ref, dx_ref, dg_ref):
        @pl.when(pl.program_id(0) == 0)
        def _():
            dg_ref[...] = jnp.zeros_like(dg_ref)

        xf = x_ref[...]
        r = lax.rsqrt(jnp.mean(xf * xf, axis=-1, keepdims=True) + RMS_EPS)
        xhat = xf * r
        dhv = dh_ref[...]
        dxh = dhv * g_ref[...]
        dx_ref[...] = dres_ref[...] + r * (dxh - xhat * jnp.mean(dxh * xhat, axis=-1, keepdims=True))
        dg_ref[...] += jnp.sum(dhv * xhat, axis=0, keepdims=True)

    row = pl.BlockSpec((tr, D), lambda i: (i, 0))
    vec = pl.BlockSpec((1, D), lambda i: (0, 0))
    return pl.pallas_call(
        body, name=name, grid=(T // tr,), in_specs=[row, row, vec, row], out_specs=[row, vec],
        out_shape=[jax.ShapeDtypeStruct((T, D), F32), jax.ShapeDtypeStruct((1, D), F32)],
        compiler_params=_params("arbitrary"))(dh, x, g, dres)


def _loss_grad(y, target, name):
    T, D = y.shape
    tr = _pick(T, 256, SUBLANES)

    def body(y_ref, t_ref, dx_ref, loss_ref):
        @pl.when(pl.program_id(0) == 0)
        def _():
            loss_ref[...] = jnp.zeros_like(loss_ref)

        err = y_ref[...] - t_ref[...]
        dx_ref[...] = err * (1.0 / D)
        loss_ref[...] += jnp.sum(jnp.mean(err * err, axis=-1, keepdims=True), axis=0, keepdims=True) * 0.5

    row = pl.BlockSpec((tr, D), lambda i: (i, 0))
    one = pl.BlockSpec((1, 1), lambda i: (0, 0))
    dx, loss = pl.pallas_call(
        body, name=name, grid=(T // tr,), in_specs=[row, row], out_specs=[row, one],
        out_shape=[jax.ShapeDtypeStruct((T, D), F32), jax.ShapeDtypeStruct((1, 1), F32)],
        compiler_params=_params("arbitrary"))(y, target)
    return dx, loss[0, 0]


_STACK = GQA_GROUP * BLOCK


def _attn_mask(n):
    row = lax.broadcasted_iota(jnp.int32, (_STACK, 2 * BLOCK), 0)
    col = lax.broadcasted_iota(jnp.int32, (_STACK, 2 * BLOCK), 1)
    dist = (row % BLOCK) - col + BLOCK
    valid = (dist >= 0) & (dist < WINDOW) & ((col >= BLOCK) | (n > 0))
    return dist.astype(F32), valid


def _per_row_head(kv, sk_ref):
    heads = [kv * GQA_GROUP + g for g in range(GQA_GROUP)]
    slope = jnp.concatenate([jnp.full((BLOCK, 1), 2.0 ** (-8.0 * (h + 1) / N_Q_HEADS), F32) for h in heads], axis=0)
    sink = jnp.concatenate([jnp.broadcast_to(sk_ref[h:h + 1, 0:1], (BLOCK, 1)) for h in heads], axis=0)
    return slope, sink


def _head_cols(h):
    return slice(h * HEAD_DIM, (h + 1) * HEAD_DIM)


def _stack_heads(x, kv):
    return jnp.concatenate([x[:, _head_cols(kv * GQA_GROUP + g)] for g in range(GQA_GROUP)], axis=0)


def _head_norm(v, gain):
    r = lax.rsqrt(jnp.mean(v * v, axis=-1, keepdims=True) + RMS_EPS)
    vhat = v * r
    return r, vhat, vhat * gain


def _attn_probs(qn16, kn16, slope, dist, valid, sink):
    s = _dot(qn16, kn16, "nt") * ATTN_SCALE - slope * dist
    s = jnp.where(valid, s, NEG_BIG)
    m = jnp.maximum(jnp.max(s, axis=-1, keepdims=True), sink)
    p = jnp.exp(s - m)
    ps = jnp.exp(sink - m)
    den = jnp.sum(p, axis=-1, keepdims=True) + ps
    return p, ps, den


def _attn_specs():
    wide = pl.BlockSpec((BLOCK, ATTN_WIDTH), lambda n: (n, 0))
    cur = [pl.BlockSpec((BLOCK, KV_WIDTH), functools.partial(lambda c, n: (n, c), off // KV_WIDTH))
           for off in (OFF_K, OFF_V)]
    prev = [pl.BlockSpec((BLOCK, KV_WIDTH), functools.partial(lambda c, n: (jnp.maximum(n - 1, 0), c), off // KV_WIDTH))
            for off in (OFF_K, OFF_V)]
    gain = pl.BlockSpec((1, HEAD_DIM), lambda n: (0, 0))
    sink = pl.BlockSpec((N_Q_HEADS, LANES), lambda n: (0, 0))
    return wide, [cur[0], prev[0], cur[1], prev[1]], gain, sink


def _sink_rows(sinks):
    return jnp.broadcast_to(sinks.reshape(N_Q_HEADS, 1), (N_Q_HEADS, LANES))


def _attn_fwd(q, k, v, qg, kg, sinks, name, comm=None):
    T = q.shape[1]

    def body(q_ref, kc_ref, kp_ref, vc_ref, vp_ref, qg_ref, kg_ref, sk_ref, o_ref):
        dist, valid = _attn_mask(pl.program_id(1))
        row_head = pl.program_id(0) * GQA_GROUP + lax.broadcasted_iota(jnp.int32, (_STACK, 1), 0) // BLOCK
        slope = jnp.exp((row_head + 1).astype(F32) * (-8.0 / N_Q_HEADS * math.log(2.0)))
        first_head = pl.program_id(0) * GQA_GROUP
        sink = jnp.concatenate([jnp.broadcast_to(sk_ref[pl.ds(first_head + g, 1), 0:1], (BLOCK, 1))
                                for g in range(GQA_GROUP)], axis=0)
        kk = jnp.concatenate([kp_ref[...], kc_ref[...]], axis=0)
        _, _, kn = _head_norm(kk, kg_ref[...])
        v16 = jnp.concatenate([vp_ref[...], vc_ref[...]], axis=0).astype(BF16)
        _, _, qn = _head_norm(q_ref[...].reshape(_STACK, HEAD_DIM), qg_ref[...])
        p, _, den = _attn_probs(qn.astype(BF16), kn.astype(BF16), slope, dist, valid, sink)
        o_ref[...] = _dot((p / den).astype(BF16), v16).reshape(GQA_GROUP, BLOCK, HEAD_DIM).astype(BF16)

    q_spec = pl.BlockSpec((GQA_GROUP, BLOCK, HEAD_DIM), lambda h, n: (h, n, 0))
    cur = pl.BlockSpec((None, BLOCK, HEAD_DIM), lambda h, n: (h, n, 0))
    prev = pl.BlockSpec((None, BLOCK, HEAD_DIM), lambda h, n: (h, jnp.maximum(n - 1, 0), 0))
    gain = pl.BlockSpec((1, HEAD_DIM), lambda h, n: (0, 0))
    sink = pl.BlockSpec((N_Q_HEADS, LANES), lambda h, n: (0, 0))
    (out,), extra = _call(
        body, name, (N_KV_HEADS, T // BLOCK), [q_spec, cur, prev, cur, prev, gain, gain, sink], [q_spec],
        [jax.ShapeDtypeStruct((N_Q_HEADS, T, HEAD_DIM), BF16)], [], ("parallel", "parallel"),
        (q, k, k, v, v, qg, kg, _sink_rows(sinks)), comm)
    return out if comm is None else (out, extra)


def _attn_bwd(z, do, qg, kg, sinks, name, comm=None):
    T = z.shape[0]

    def body(q_ref, kc_ref, kp_ref, vc_ref, vp_ref, do_ref, qg_ref, kg_ref, sk_ref,
             dq_ref, dk_ref, dv_ref, dqg_ref, dkg_ref, dsk_ref):
        n = pl.program_id(0)

        @pl.when(n == 0)
        def _():
            for ref in (dk_ref, dv_ref, dqg_ref, dkg_ref, dsk_ref):
                ref[...] = jnp.zeros_like(ref)

        dist, valid = _attn_mask(n)
        rows = pl.ds(pl.multiple_of(n * BLOCK, BLOCK), BLOCK)
        before = pl.ds(pl.multiple_of(jnp.maximum(n - 1, 0) * BLOCK, BLOCK), BLOCK)
        q = q_ref[...]
        do = do_ref[...]
        kk = jnp.concatenate([kp_ref[...], kc_ref[...]], axis=0)
        vv = jnp.concatenate([vp_ref[...], vc_ref[...]], axis=0)
        for kv in range(N_KV_HEADS):
            slope, sink = _per_row_head(kv, sk_ref)
            rk, khat, kn = _head_norm(kk[:, _head_cols(kv)], kg_ref[...])
            kn16 = kn.astype(BF16)
            v16 = vv[:, _head_cols(kv)].astype(BF16)
            rq, qhat, qn = _head_norm(_stack_heads(q, kv), qg_ref[...])
            qn16 = qn.astype(BF16)
            p, ps, den = _attn_probs(qn16, kn16, slope, dist, valid, sink)
            pn = p / den
            do16 = _stack_heads(do, kv).astype(BF16)
            dp = _dot(do16, v16, "nt")
            delta = jnp.sum(pn * dp, axis=-1, keepdims=True)
            ds16 = (pn * (dp - delta)).astype(BF16)
            sink_pull = ps / den * delta
            dqn = _dot(ds16, kn16) * ATTN_SCALE
            dkn = _dot(ds16, qn16, "tn") * ATTN_SCALE
            dv = _dot(pn.astype(BF16), do16, "tn")
            dqh = dqn * qg_ref[...]
            dq = rq * (dqh - qhat * jnp.mean(dqh * qhat, axis=-1, keepdims=True))
            for g in range(GQA_GROUP):
                h = kv * GQA_GROUP + g
                dsink = jnp.sum(sink_pull[g * BLOCK:(g + 1) * BLOCK], axis=0, keepdims=True)
                dsk_ref[h:h + 1, :] -= jnp.broadcast_to(dsink, (1, LANES))
                dq_ref[:, _head_cols(h)] = dq[g * BLOCK:(g + 1) * BLOCK].astype(BF16)
            dqg_ref[...] += jnp.sum(dqn * qhat, axis=0, keepdims=True)
            dkg_ref[...] += jnp.sum(dkn * khat, axis=0, keepdims=True)
            dkh = dkn * kg_ref[...]
            dk = rk * (dkh - khat * jnp.mean(dkh * khat, axis=-1, keepdims=True))
            dk_ref[rows, _head_cols(kv)] += dk[BLOCK:]
            dv_ref[rows, _head_cols(kv)] += dv[BLOCK:]
            dk_ref[before, _head_cols(kv)] += dk[:BLOCK]
            dv_ref[before, _head_cols(kv)] += dv[:BLOCK]

    wide, kv_specs, gain, sink = _attn_specs()
    whole = pl.BlockSpec((T, KV_WIDTH), lambda n: (0, 0))
    outs, extra = _call(
        body, name, (T // BLOCK,), [wide] + kv_specs + [wide, gain, gain, sink],
        [wide, whole, whole, gain, gain, sink],
        [jax.ShapeDtypeStruct((T, ATTN_WIDTH), BF16), jax.ShapeDtypeStruct((T, KV_WIDTH), F32),
         jax.ShapeDtypeStruct((T, KV_WIDTH), F32), jax.ShapeDtypeStruct((1, HEAD_DIM), F32),
         jax.ShapeDtypeStruct((1, HEAD_DIM), F32), jax.ShapeDtypeStruct((N_Q_HEADS, LANES), F32)],
        [], ("arbitrary",), (z, z, z, z, z, do, qg, kg, _sink_rows(sinks)), comm)
    return outs if comm is None else (outs, extra)


def _ssm_discretize(lam_re, lam_im, log_dt, b_re, b_im, c_re, c_im):
    dt = jnp.exp(log_dt)[:, None]
    mag = jnp.exp(lam_re * dt)
    ar = mag * jnp.cos(lam_im * dt)
    ai = mag * jnp.sin(lam_im * dt)
    den = lam_re * lam_re + lam_im * lam_im
    fr = ((ar - 1.0) * lam_re + ai * lam_im) / den
    fi = (ai * lam_re - (ar - 1.0) * lam_im) / den
    bbar_r = fr[:, :, None] * b_re - fi[:, :, None] * b_im
    bbar_i = fr[:, :, None] * b_im + fi[:, :, None] * b_re
    gl = SSM_GROUPS // SSM_NGB
    eye = jnp.eye(gl, dtype=F32)

    def tiles(a):
        return a.reshape(SSM_NGB, SUBLANES, LANES)

    def bdiag(bb):
        return jnp.einsum("bgph,gk->bghkp", bb.reshape(SSM_NGB, gl, SSM_STATE, SSM_GROUP_CH), eye).reshape(
            SSM_NGB, SSM_GB_CH, SSM_GB_ST)

    def cdiag(cc):
        return jnp.einsum("bghp,gk->bgpkh", cc.reshape(SSM_NGB, gl, SSM_GROUP_CH, SSM_STATE), eye).reshape(
            SSM_NGB, SSM_GB_ST, SSM_GB_CH)

    return tiles(ar), tiles(ai), bdiag(bbar_r), bdiag(bbar_i), cdiag(c_re), cdiag(c_im)


def _to_time_major(dst, val, tt, first_row=0):
    for j in range(SUBLANES):
        dst[pl.ds(first_row + j, tt, stride=SUBLANES), :] = val[:, j * LANES:(j + 1) * LANES]


def _from_time_major(dst, src, tt):
    for j in range(SUBLANES):
        dst[:, j * LANES:(j + 1) * LANES] = src[pl.ds(j, tt, stride=SUBLANES), :]


def _ssm_fwd(z, ar, ai, bbr, bbi, cbr, cbi, dskip, name, comm=None):
    T = z.shape[0]
    tt = min(SSM_TT, T)
    nt = T // tt

    def body(u_ref, ar_ref, ai_ref, br_ref, bi_ref, cr_ref, ci_ref, d_ref, y_ref, sr_ref, si_ref,
             tmr, tmi, car_r, car_i):
        @pl.when(pl.program_id(1) == 0)
        def _():
            car_r[...] = jnp.zeros_like(car_r)
            car_i[...] = jnp.zeros_like(car_i)

        u = u_ref[...]
        u16 = u.astype(BF16)
        _to_time_major(tmr, _dot(u16, br_ref[...]), tt)
        _to_time_major(tmi, _dot(u16, bi_ref[...]), tt)
        a_r = ar_ref[...]
        a_i = ai_ref[...]

        def step(t, carry):
            s_r, s_i = carry
            rows = pl.ds(pl.multiple_of(t * SUBLANES, SUBLANES), SUBLANES)
            n_r = a_r * s_r - a_i * s_i + tmr[rows, :]
            n_i = a_r * s_i + a_i * s_r + tmi[rows, :]
            tmr[rows, :] = n_r
            tmi[rows, :] = n_i
            return n_r, n_i

        s_r, s_i = lax.fori_loop(0, tt, step, (car_r[...], car_i[...]), unroll=8)
        car_r[...] = s_r
        car_i[...] = s_i
        _from_time_major(sr_ref, tmr, tt)
        _from_time_major(si_ref, tmi, tt)
        y_ref[...] = (_dot(sr_ref[...].astype(BF16), cr_ref[...]) - _dot(si_ref[...].astype(BF16), ci_ref[...])
                      + d_ref[...] * u)

    u_spec = pl.BlockSpec((tt, SSM_GB_CH), lambda b, t: (t, OFF_U // SSM_GB_CH + b))
    a_spec = pl.BlockSpec((None, SUBLANES, LANES), lambda b, t: (b, 0, 0))
    b_spec = pl.BlockSpec((None, SSM_GB_CH, SSM_GB_ST), lambda b, t: (b, 0, 0))
    c_spec = pl.BlockSpec((None, SSM_GB_ST, SSM_GB_CH), lambda b, t: (b, 0, 0))
    d_spec = pl.BlockSpec((1, SSM_GB_CH), lambda b, t: (0, b))
    y_spec = pl.BlockSpec((tt, SSM_GB_CH), lambda b, t: (t, b))
    s_spec = pl.BlockSpec((tt, SSM_GB_ST), lambda b, t: (t, b))
    n_state = SSM_NGB * SSM_GB_ST
    outs, extra = _call(
        body, name, (SSM_NGB, nt), [u_spec, a_spec, a_spec, b_spec, b_spec, c_spec, c_spec, d_spec],
        [y_spec, s_spec, s_spec],
        [jax.ShapeDtypeStruct((T, SSM_WIDTH), F32), jax.ShapeDtypeStruct((T, n_state), F32),
         jax.ShapeDtypeStruct((T, n_state), F32)],
        [pltpu.VMEM((tt * SUBLANES, LANES), F32), pltpu.VMEM((tt * SUBLANES, LANES), F32),
         pltpu.VMEM((SUBLANES, LANES), F32), pltpu.VMEM((SUBLANES, LANES), F32)],
        ("parallel", "arbitrary"), (z, ar, ai, bbr, bbi, cbr, cbi, dskip), comm)
    return outs if comm is None else (outs, extra)


def _ssm_bwd(dy, z, s_r, s_i, ar, ai, bbr, bbi, cbr, cbi, dskip, name, comm=None):
    T = z.shape[0]
    tt = min(SSM_TT, T)
    nt = T // tt
    per8 = tt // SUBLANES

    def body(dy_ref, u_ref, sr_ref, si_ref, srp_ref, sip_ref, ar_ref, ai_ref, br_ref, bi_ref, cr_ref, ci_ref, d_ref,
             du_ref, dar_ref, dai_ref, dbr_ref, dbi_ref, dcr_ref, dci_ref, dd_ref,
             tmr, tmi, smr, smi, natr, nati, car_r, car_i):
        tb = pl.program_id(1)
        first_block = tb == nt - 1

        @pl.when(tb == 0)
        def _():
            for ref in (car_r, car_i, dar_ref, dai_ref, dbr_ref, dbi_ref, dcr_ref, dci_ref, dd_ref):
                ref[...] = jnp.zeros_like(ref)

        dy = dy_ref[...]
        dy16 = dy.astype(BF16)
        u = u_ref[...]
        u16 = u.astype(BF16)
        _to_time_major(tmr, _dot(dy16, cr_ref[...], "nt"), tt)
        _to_time_major(tmi, -_dot(dy16, ci_ref[...], "nt"), tt)
        _to_time_major(smr, sr_ref[...], tt, first_row=SUBLANES)
        _to_time_major(smi, si_ref[...], tt, first_row=SUBLANES)
        keep = jnp.where(first_block, 0.0, 1.0)
        for j in range(SUBLANES):
            smr[j:j + 1, :] = srp_ref[SUBLANES - 1:SUBLANES, j * LANES:(j + 1) * LANES] * keep
            smi[j:j + 1, :] = sip_ref[SUBLANES - 1:SUBLANES, j * LANES:(j + 1) * LANES] * keep
        a_r = ar_ref[...]
        a_i = ai_ref[...]

        def step(i, carry):
            n_r, n_i, da_r, da_i = carry
            rows = pl.ds(pl.multiple_of((tt - 1 - i) * SUBLANES, SUBLANES), SUBLANES)
            g_r = tmr[rows, :] + a_r * n_r + a_i * n_i
            g_i = tmi[rows, :] - a_i * n_r + a_r * n_i
            tmr[rows, :] = g_r
            tmi[rows, :] = g_i
            p_r = smr[rows, :]
            p_i = smi[rows, :]
            return g_r, g_i, da_r + g_r * p_r + g_i * p_i, da_i - g_r * p_i + g_i * p_r

        zero = jnp.zeros((SUBLANES, LANES), F32)
        n_r, n_i, da_r, da_i = lax.fori_loop(0, tt, step, (car_r[...], car_i[...], zero, zero), unroll=8)
        car_r[...] = n_r
        car_i[...] = n_i
        dar_ref[...] += da_r
        dai_ref[...] += da_i
        _from_time_major(natr, tmr, tt)
        _from_time_major(nati, tmi, tt)
        dbu_r16 = natr[...].astype(BF16)
        dbu_i16 = nati[...].astype(BF16)
        du_ref[...] = (_dot(dbu_r16, br_ref[...], "nt") + _dot(dbu_i16, bi_ref[...], "nt")
                       + d_ref[...] * dy).astype(BF16)
        dbr_ref[...] += _dot(u16, dbu_r16, "tn")
        dbi_ref[...] += _dot(u16, dbu_i16, "tn")
        dcr_ref[...] += _dot(sr_ref[...].astype(BF16), dy16, "tn")
        dci_ref[...] -= _dot(si_ref[...].astype(BF16), dy16, "tn")
        dd_ref[...] += jnp.sum(dy * u, axis=0, keepdims=True)

    def rev(t):
        return nt - 1 - t

    dy_spec = pl.BlockSpec((tt, SSM_GB_CH), lambda b, t: (rev(t), b))
    u_spec = pl.BlockSpec((tt, SSM_GB_CH), lambda b, t: (rev(t), OFF_U // SSM_GB_CH + b))
    s_spec = pl.BlockSpec((tt, SSM_GB_ST), lambda b, t: (rev(t), b))
    sp_spec = pl.BlockSpec((SUBLANES, SSM_GB_ST), lambda b, t: (jnp.maximum(rev(t) * per8 - 1, 0), b))
    a_spec = pl.BlockSpec((None, SUBLANES, LANES), lambda b, t: (b, 0, 0))
    b_spec = pl.BlockSpec((None, SSM_GB_CH, SSM_GB_ST), lambda b, t: (b, 0, 0))
    c_spec = pl.BlockSpec((None, SSM_GB_ST, SSM_GB_CH), lambda b, t: (b, 0, 0))
    d_spec = pl.BlockSpec((1, SSM_GB_CH), lambda b, t: (0, b))
    tm_shape = pltpu.VMEM((tt * SUBLANES, LANES), F32)
    sm_shape = pltpu.VMEM(((tt + 1) * SUBLANES, LANES), F32)
    nat_shape = pltpu.VMEM((tt, SSM_GB_ST), F32)
    tile = pltpu.VMEM((SUBLANES, LANES), F32)
    outs, extra = _call(
        body, name, (SSM_NGB, nt),
        [dy_spec, u_spec, s_spec, s_spec, sp_spec, sp_spec, a_spec, a_spec, b_spec, b_spec, c_spec, c_spec, d_spec],
        [dy_spec, a_spec, a_spec, b_spec, b_spec, c_spec, c_spec, d_spec],
        [jax.ShapeDtypeStruct((T, SSM_WIDTH), BF16),
         jax.ShapeDtypeStruct((SSM_NGB, SUBLANES, LANES), F32),
         jax.ShapeDtypeStruct((SSM_NGB, SUBLANES, LANES), F32),
         jax.ShapeDtypeStruct((SSM_NGB, SSM_GB_CH, SSM_GB_ST), F32),
         jax.ShapeDtypeStruct((SSM_NGB, SSM_GB_CH, SSM_GB_ST), F32),
         jax.ShapeDtypeStruct((SSM_NGB, SSM_GB_ST, SSM_GB_CH), F32),
         jax.ShapeDtypeStruct((SSM_NGB, SSM_GB_ST, SSM_GB_CH), F32),
         jax.ShapeDtypeStruct((1, SSM_WIDTH), F32)],
        [tm_shape, tm_shape, sm_shape, sm_shape, nat_shape, nat_shape, tile, tile], ("parallel", "arbitrary"),
        (dy, z, s_r, s_i, s_r, s_i, ar, ai, bbr, bbi, cbr, cbi, dskip), comm)
    return outs if comm is None else (outs, extra)


def _glu_fwd(y, w, b, name):
    T, W = y.shape
    tm = _pick(T, 512)

    def body(y_ref, w_ref, b_ref, pre_ref, y3_ref):
        y2 = _gelu(y_ref[...])
        pre = _dot(y2.astype(BF16), w_ref[...]) + b_ref[...]
        pre_ref[...] = pre
        y3_ref[...] = (y2 * _sigmoid(pre)).astype(BF16)

    row = pl.BlockSpec((tm, W), lambda i: (i, 0))
    return pl.pallas_call(
        body, name=name, grid=(T // tm,),
        in_specs=[row, pl.BlockSpec((W, W), lambda i: (0, 0)), pl.BlockSpec((1, W), lambda i: (0, 0))],
        out_specs=[row, row], out_shape=[jax.ShapeDtypeStruct((T, W), F32), jax.ShapeDtypeStruct((T, W), BF16)],
        compiler_params=_params("parallel"))(y, w, b)


def _glu_bwd_gate(dy3, y, pre, name):
    T, W = y.shape
    tm = _pick(T, 512)

    def body(dy3_ref, y_ref, pre_ref, dpre_ref, t1_ref, y2_ref, db_ref):
        @pl.when(pl.program_id(0) == 0)
        def _():
            db_ref[...] = jnp.zeros_like(db_ref)

        y2 = _gelu(y_ref[...])
        sg = _sigmoid(pre_ref[...])
        dy3 = dy3_ref[...]
        dpre = dy3 * y2 * sg * (1.0 - sg)
        dpre_ref[...] = dpre.astype(BF16)
        t1_ref[...] = dy3 * sg
        y2_ref[...] = y2.astype(BF16)
        db_ref[...] += jnp.sum(dpre, axis=0, keepdims=True)

    row = pl.BlockSpec((tm, W), lambda i: (i, 0))
    vec = pl.BlockSpec((1, W), lambda i: (0, 0))
    return pl.pallas_call(
        body, name=name, grid=(T // tm,), in_specs=[row, row, row], out_specs=[row, row, row, vec],
        out_shape=[jax.ShapeDtypeStruct((T, W), BF16), jax.ShapeDtypeStruct((T, W), F32),
                   jax.ShapeDtypeStruct((T, W), BF16), jax.ShapeDtypeStruct((1, W), F32)],
        compiler_params=_params("arbitrary"))(dy3, y, pre)


def _glu_bwd_in(dpre, w, t1, y, name):
    T, W = y.shape
    tm = _pick(T, 512)

    def body(dpre_ref, w_ref, t1_ref, y_ref, dy_ref):
        dy_ref[...] = (_dot(dpre_ref[...], w_ref[...], "nt") + t1_ref[...]) * _gelu_grad(y_ref[...])

    row = pl.BlockSpec((tm, W), lambda i: (i, 0))
    return pl.pallas_call(
        body, name=name, grid=(T // tm,), in_specs=[row, pl.BlockSpec((W, W), lambda i: (0, 0)), row, row],
        out_specs=row, out_shape=jax.ShapeDtypeStruct((T, W), F32),
        compiler_params=_params("parallel"))(dpre, w, t1, y)


def _merge_fwd(ya, y3, wa, ws, z, bias, name, comm=None):
    T, W = ya.shape
    D = wa.shape[1]
    tm, tn = _pick(T, 512), _pick(D, 512)

    def body(ya_ref, y3_ref, wa_ref, ws_ref, za_ref, zs_ref, ba_ref, bs_ref, a_ref, b_ref, m_ref):
        a = _dot(ya_ref[...], wa_ref[...])
        b = _dot(y3_ref[...], ws_ref[...])
        a_ref[...] = a
        b_ref[...] = b
        m_ref[...] = (_sigmoid(za_ref[...] + ba_ref[...]) * a + _sigmoid(zs_ref[...] + bs_ref[...]) * b).astype(BF16)

    act = pl.BlockSpec((tm, W), lambda i, j: (i, 0))
    wgt = pl.BlockSpec((W, tn), lambda i, j: (0, j))
    za = pl.BlockSpec((tm, tn), lambda i, j: (i, OFF_G // tn + j))
    zs = pl.BlockSpec((tm, tn), lambda i, j: (i, (OFF_G + D) // tn + j))
    ba = pl.BlockSpec((1, tn), lambda i, j: (0, j))
    bs = pl.BlockSpec((1, tn), lambda i, j: (0, D // tn + j))
    out = pl.BlockSpec((tm, tn), lambda i, j: (i, j))
    outs, extra = _call(
        body, name, (T // tm, D // tn), [act, act, wgt, wgt, za, zs, ba, bs], [out, out, out],
        [jax.ShapeDtypeStruct((T, D), F32), jax.ShapeDtypeStruct((T, D), F32), jax.ShapeDtypeStruct((T, D), BF16)],
        [], ("parallel", "parallel"), (ya, y3, wa, ws, z, z, bias, bias), comm)
    return outs if comm is None else (outs, extra)


def _merge_bwd(dm, a, b, z, bias, name):
    T, D = dm.shape
    tm, tn = _pick(T, 512), _pick(D, 512)

    def body(dm_ref, a_ref, b_ref, za_ref, zs_ref, ba_ref, bs_ref, da_ref, db_ref, dza_ref, dzs_ref, dba_ref, dbs_ref):
        @pl.when(pl.program_id(1) == 0)
        def _():
            dba_ref[...] = jnp.zeros_like(dba_ref)
            dbs_ref[...] = jnp.zeros_like(dbs_ref)

        dm = dm_ref[...]
        sa = _sigmoid(za_ref[...] + ba_ref[...])
        ss = _sigmoid(zs_ref[...] + bs_ref[...])
        da_ref[...] = (dm * sa).astype(BF16)
        db_ref[...] = (dm * ss).astype(BF16)
        dza = dm * a_ref[...] * sa * (1.0 - sa)
        dzs = dm * b_ref[...] * ss * (1.0 - ss)
        dza_ref[...] = dza.astype(BF16)
        dzs_ref[...] = dzs.astype(BF16)
        dba_ref[...] += jnp.sum(dza, axis=0, keepdims=True)
        dbs_ref[...] += jnp.sum(dzs, axis=0, keepdims=True)

    blk = pl.BlockSpec((tm, tn), lambda j, i: (i, j))
    za = pl.BlockSpec((tm, tn), lambda j, i: (i, OFF_G // tn + j))
    zs = pl.BlockSpec((tm, tn), lambda j, i: (i, (OFF_G + D) // tn + j))
    ba = pl.BlockSpec((1, tn), lambda j, i: (0, j))
    bs = pl.BlockSpec((1, tn), lambda j, i: (0, D // tn + j))
    big = jax.ShapeDtypeStruct((T, D), BF16)
    vec = jax.ShapeDtypeStruct((1, D), F32)
    return pl.pallas_call(
        body, name=name, grid=(D // tn, T // tm), in_specs=[blk, blk, blk, za, zs, ba, bs],
        out_specs=[blk, blk, blk, blk, ba, ba], out_shape=[big, big, big, big, vec, vec],
        compiler_params=_params("parallel", "arbitrary"))(dm, a, b, z, z, bias, bias)


_HALF = N_DEV // 2


def _wgu_block(d):
    return d // 2, d % 2


def _ffn_fwd(h, wgu, name, comm=None):
    T, D = h.shape
    n = wgu.shape[3]
    F = _HALF * n
    tm = _pick(T, 512)

    def body(h_ref, wg_ref, wu_ref, g_ref, u_ref, act_ref):
        hv = h_ref[...]
        g = _dot(hv, wg_ref[...])
        u = _dot(hv, wu_ref[...])
        g_ref[...] = g
        u_ref[...] = u
        act_ref[...] = (g * _sigmoid(g) * u).astype(BF16)

    wg = pl.BlockSpec((None, None, D, n), lambda j, i: (*_wgu_block(j), 0, 0))
    wu = pl.BlockSpec((None, None, D, n), lambda j, i: (*_wgu_block(j + _HALF), 0, 0))
    out = pl.BlockSpec((tm, n), lambda j, i: (i, j))
    outs, extra = _call(
        body, name, (_HALF, T // tm), [pl.BlockSpec((tm, D), lambda j, i: (i, 0)), wg, wu], [out, out, out],
        [jax.ShapeDtypeStruct((T, F), F32), jax.ShapeDtypeStruct((T, F), F32), jax.ShapeDtypeStruct((T, F), BF16)],
        [], ("parallel", "parallel"), (h, wgu, wgu), comm)
    return outs if comm is None else (outs, extra)


def _ffn_bwd_in(dg, du, wgu, name, comm=None):
    T, F = dg.shape
    D, n = wgu.shape[2], wgu.shape[3]
    tm, tn = _pick(T, 1024), _pick(D, 1024)

    def body(dg_ref, du_ref, w_ref, o_ref, acc_ref):
        k = pl.program_id(2)

        @pl.when(k == 0)
        def _():
            acc_ref[...] = jnp.zeros_like(acc_ref)

        @pl.when(k < _HALF)
        def _():
            acc_ref[...] += _dot(dg_ref[...], w_ref[...], "nt")

        @pl.when(k >= _HALF)
        def _():
            acc_ref[...] += _dot(du_ref[...], w_ref[...], "nt")

        @pl.when(k == N_DEV - 1)
        def _():
            o_ref[...] = acc_ref[...]

    dg_spec = pl.BlockSpec((tm, n), lambda i, j, k: (i, jnp.minimum(k, _HALF - 1)))
    du_spec = pl.BlockSpec((tm, n), lambda i, j, k: (i, jnp.maximum(k - _HALF, 0)))
    w_spec = pl.BlockSpec((None, None, tn, n), lambda i, j, k: (*_wgu_block(k), j, 0))
    o_spec = pl.BlockSpec((tm, tn), lambda i, j, k: (i, j))
    (out,), extra = _call(
        body, name, (T // tm, D // tn, N_DEV), [dg_spec, du_spec, w_spec], [o_spec],
        [jax.ShapeDtypeStruct((T, D), F32)], [pltpu.VMEM((tm, tn), F32)], ("parallel", "parallel", "arbitrary"),
        (dg, du, wgu), comm)
    return out if comm is None else (out, extra)


def _ffn_bwd_w(h, dg, du, name, comm=None):
    T, D = h.shape
    n = dg.shape[1] // _HALF
    tm, tk = _pick(D, 1024), _pick(T, 2048)
    nk = T // tk

    def body(h_ref, dg_ref, du_ref, o_ref, acc_ref):
        j, k = pl.program_id(0), pl.program_id(2)

        @pl.when(k == 0)
        def _():
            acc_ref[...] = jnp.zeros_like(acc_ref)

        @pl.when(j < _HALF)
        def _():
            acc_ref[...] += _dot(h_ref[...], dg_ref[...], "tn")

        @pl.when(j >= _HALF)
        def _():
            acc_ref[...] += _dot(h_ref[...], du_ref[...], "tn")

        @pl.when(k == nk - 1)
        def _():
            o_ref[...] = acc_ref[...].astype(BF16)

    h_spec = pl.BlockSpec((tk, tm), lambda j, i, k: (k, i))
    dg_spec = pl.BlockSpec((tk, n), lambda j, i, k: (jnp.where(j < _HALF, k, nk - 1), jnp.minimum(j, _HALF - 1)))
    du_spec = pl.BlockSpec((tk, n), lambda j, i, k: (jnp.where(j >= _HALF, k, 0), jnp.maximum(j - _HALF, 0)))
    o_spec = pl.BlockSpec((None, None, tm, n), lambda j, i, k: (*_wgu_block(j), i, 0))
    (out,), extra = _call(
        body, name, (N_DEV, D // tm, nk), [h_spec, dg_spec, du_spec], [o_spec],
        [jax.ShapeDtypeStruct((_HALF, 2, D, n), BF16)], [pltpu.VMEM((tm, n), F32)],
        ("parallel", "parallel", "arbitrary"), (h, dg, du), comm)
    return out if comm is None else (out, extra)


def _ffn_bwd_act(dx, wo, g, u, name, comm=None):
    T, D = dx.shape
    F = wo.shape[0]
    tm, tn = _pick(T, 1024), _pick(F, 512)

    def body(dx_ref, wo_ref, g_ref, u_ref, dg_ref, du_ref):
        dact = _dot(dx_ref[...].astype(BF16), wo_ref[...], "nt")
        gv = g_ref[...]
        sg = _sigmoid(gv)
        dg_ref[...] = (dact * u_ref[...] * sg * (1.0 + gv * (1.0 - sg))).astype(BF16)
        du_ref[...] = (dact * gv * sg).astype(BF16)

    out = pl.BlockSpec((tm, tn), lambda i, j: (i, j))
    big = jax.ShapeDtypeStruct((T, F), BF16)
    outs, extra = _call(
        body, name, (T // tm, F // tn),
        [pl.BlockSpec((tm, D), lambda i, j: (i, 0)), pl.BlockSpec((tn, D), lambda i, j: (j, 0)), out, out],
        [out, out], [big, big], [], ("parallel", "parallel"), (dx, wo, g, u), comm)
    return outs if comm is None else (outs, extra)


def _place():
    x, y, c = lax.axis_index("x"), lax.axis_index("y"), lax.axis_index("c")
    other_chips = [(1 - x, y), (x, 1 - y), (1 - x, 1 - y)]
    return x, y, c, 2 * x + y, other_chips


_ANY = pl.BlockSpec(memory_space=pl.ANY)
_N_COPIES = 7


def _all_gather(shards, name):
    n = len(shards)

    def body(*refs):
        ins, outs = refs[:n], refs[n:2 * n]
        send_sems, recv_sems, local_sems = refs[2 * n:]
        x, y, c, chip, other_chips = _place()
        sibling = (x, y, 1 - c)

        def remote(src, dst, a, j, dev):
            return pltpu.make_async_remote_copy(src_ref=src, dst_ref=dst, send_sem=send_sems.at[a, j],
                                                recv_sem=recv_sems.at[a, j], device_id=dev, device_id_type=MESH)

        sends, local = [], []
        for a in range(n):
            mine = outs[a].at[chip, c]
            local.append(pltpu.make_async_copy(ins[a], mine, local_sems.at[a]))
            local[a].start()
            for j, (ox, oy) in enumerate(other_chips):
                sends.append(remote(ins[a], mine, a, 1 + j, (ox, oy, c)))
                sends[-1].start()
            sends.append(remote(ins[a], mine, a, 0, sibling))
            sends[-1].start()
        for a in range(n):
            for j, (ox, oy) in enumerate(other_chips):
                slot = outs[a].at[2 * ox + oy, c]
                remote(ins[a], slot, a, 1 + j, (ox, oy, c)).wait_recv()
                sends.append(remote(slot, slot, a, 4 + j, sibling))
                sends[-1].start()
        for a in range(n):
            remote(ins[a], outs[a].at[chip, 1 - c], a, 0, sibling).wait_recv()
            for j, (ox, oy) in enumerate(other_chips):
                remote(ins[a], outs[a].at[2 * ox + oy, 1 - c], a, 4 + j, sibling).wait_recv()
        for cp in sends:
            cp.wait_send()
        for a in range(n):
            local[a].wait()

    return pl.pallas_call(
        body, name=name, in_specs=[_ANY] * n, out_specs=[_ANY] * n,
        out_shape=[jax.ShapeDtypeStruct((4, 2) + s.shape, s.dtype) for s in shards],
        scratch_shapes=[pltpu.SemaphoreType.DMA((n, _N_COPIES)), pltpu.SemaphoreType.DMA((n, _N_COPIES)),
                        pltpu.SemaphoreType.DMA((n,))])(*shards)


def _pair_exchange(parts, name):
    n = len(parts)

    def body(*refs):
        ins, theirs = refs[:n], refs[n:2 * n]
        send_sems, recv_sems = refs[2 * n:]
        x, y, c, _, _ = _place()
        sends = []
        for a in range(n):
            for k in range(4):
                sends.append(pltpu.make_async_remote_copy(
                    src_ref=ins[a].at[k, 1 - c], dst_ref=theirs[a].at[k], send_sem=send_sems.at[a, k],
                    recv_sem=recv_sems.at[a, k], device_id=(x, y, 1 - c), device_id_type=MESH))
                sends[-1].start()
        for cp in sends:
            cp.wait_recv()
            cp.wait_send()

    return pl.pallas_call(
        body, name=name, in_specs=[_ANY] * n, out_specs=[_ANY] * n,
        out_shape=[jax.ShapeDtypeStruct((4,) + p.shape[2:], p.dtype) for p in parts],
        scratch_shapes=[pltpu.SemaphoreType.DMA((n, 4)), pltpu.SemaphoreType.DMA((n, 4))])(*parts)


def _chip_exchange(parts, name):
    n = len(parts)

    def body(*refs):
        ins, got = refs[:n], refs[n:2 * n]
        send_sems, recv_sems = refs[2 * n:]
        _, _, c, _, other_chips = _place()
        sends = []
        for a in range(n):
            for j, (ox, oy) in enumerate(other_chips):
                sends.append(pltpu.make_async_remote_copy(
                    src_ref=ins[a].at[2 * ox + oy], dst_ref=got[a].at[j], send_sem=send_sems.at[a, j],
                    recv_sem=recv_sems.at[a, j], device_id=(ox, oy, c), device_id_type=MESH))
                sends[-1].start()
        for cp in sends:
            cp.wait_recv()
            cp.wait_send()

    return pl.pallas_call(
        body, name=name, in_specs=[_ANY] * n, out_specs=[_ANY] * n,
        out_shape=[jax.ShapeDtypeStruct((3,) + p.shape[1:], p.dtype) for p in parts],
        scratch_shapes=[pltpu.SemaphoreType.DMA((n, 3)), pltpu.SemaphoreType.DMA((n, 3))])(*parts)


def _remote(src, dst, send_sems, recv_sems, a, j, dev):
    return pltpu.make_async_remote_copy(src_ref=src, dst_ref=dst, send_sem=send_sems.at[a, j],
                                        recv_sem=recv_sems.at[a, j], device_id=dev, device_id_type=MESH)


def _gather_send(shards):
    n = len(shards)

    def copies(cin, cout, sems, arriving):
        send_sems, recv_sems, local_sems = sems
        x, y, c, chip, other_chips = _place()
        sibling = (x, y, 1 - c)
        peers = [(0, sibling, (chip, 1 - c))] + [(1 + j, (ox, oy, c), (2 * ox + oy, c))
                                                 for j, (ox, oy) in enumerate(other_chips)]
        sends, recvs, local = [], [], []
        for a in range(n):
            mine = cout[a].at[chip, c]
            local.append(pltpu.make_async_copy(cin[a], mine, local_sems.at[a]))
            for j, dev, slot in peers:
                sends.append(_remote(cin[a], mine, send_sems, recv_sems, a, j, dev))
                if arriving:
                    recvs.append(_remote(cin[a], cout[a].at[slot], send_sems, recv_sems, a, j, dev))
        return sends, recvs, local

    return _Comm(shards, [jax.ShapeDtypeStruct((4, 2) + s.shape, s.dtype) for s in shards],
                 [pltpu.SemaphoreType.DMA((n, 4)), pltpu.SemaphoreType.DMA((n, 4)), pltpu.SemaphoreType.DMA((n,))],
                 copies)


def _gather_pass(gathered):
    n = len(gathered)

    def copies(cin, cout, sems, arriving):
        send_sems, recv_sems = sems
        x, y, c, _, other_chips = _place()
        sibling = (x, y, 1 - c)
        sends, recvs = [], []
        for a in range(n):
            for j, (ox, oy) in enumerate(other_chips):
                k = 2 * ox + oy
                sends.append(_remote(cout[a].at[k, c], cout[a].at[k, c], send_sems, recv_sems, a, j, sibling))
                if arriving:
                    recvs.append(_remote(cout[a].at[k, c], cout[a].at[k, 1 - c], send_sems, recv_sems, a, j, sibling))
        return sends, recvs, []

    return _Comm(gathered, [jax.ShapeDtypeStruct(g.shape, g.dtype) for g in gathered],
                 [pltpu.SemaphoreType.DMA((n, 3)), pltpu.SemaphoreType.DMA((n, 3))], copies,
                 aliases={i: i for i in range(n)})


def _scatter_send(sums):
    n = len(sums)

    def copies(cin, cout, sems, arriving):
        send_sems, recv_sems = sems
        _, _, c, _, other_chips = _place()
        sends = [_remote(cin[a].at[2 * ox + oy], cout[a].at[j], send_sems, recv_sems, a, j, (ox, oy, c))
                 for a in range(n) for j, (ox, oy) in enumerate(other_chips)]
        return sends, sends, []

    return _Comm(sums, [jax.ShapeDtypeStruct((3,) + s.shape[1:], s.dtype) for s in sums],
                 [pltpu.SemaphoreType.DMA((n, 3)), pltpu.SemaphoreType.DMA((n, 3))], copies)


def _join(comms):
    if len(comms) == 1:
        return comms[0]
    args, outs, sems, aliases, spans = [], [], [], {}, []
    for cm in comms:
        spans.append((len(args), len(outs), len(sems)))
        aliases.update({len(args) + i: len(outs) + o for i, o in cm.aliases.items()})
        args, outs, sems = args + cm.args, outs + cm.out_shapes, sems + cm.sems

    def copies(cin, cout, sem_refs, arriving):
        sends, recvs, local = [], [], []
        for cm, (a0, o0, s0) in zip(comms, spans):
            part = cm.copies(cin[a0:a0 + len(cm.args)], cout[o0:o0 + len(cm.out_shapes)],
                             sem_refs[s0:s0 + len(cm.sems)], arriving)
            sends, recvs, local = sends + part[0], recvs + part[1], local + part[2]
        return sends, recvs, local

    return _Comm(args, outs, sems, copies, aliases)


class _Schedule:
    def __init__(self):
        self.rides = {}

    def ride(self, host, make, names, operands):
        results = {}
        self.rides.setdefault(host, []).append((make, names, operands, results))
        return results

    def carry(self, host, fn, *args, **kwargs):
        rides = self.rides.get(host)
        if not rides:
            return fn(*args, host, **kwargs)
        comm = _join([make([operands[n] for n in names]) for make, names, operands, _ in rides])
        out, extra = fn(*args, host, comm=comm, **kwargs)
        for _, names, _, results in rides:
            results.update(zip(names, extra[:len(names)]))
            extra = extra[len(names):]
        return out


def _add_pair(core, parts, theirs, name):
    k, _, R, C = parts.shape
    tr = _pick(R, 512, 16)

    def body(core_ref, a_ref, b_ref, o_ref):
        o_ref[...] = (a_ref[...].astype(F32) + b_ref[...].astype(F32)).astype(BF16)

    blk = pl.BlockSpec((None, tr, C), lambda s, i, core_ref: (s, i, 0))
    grid_spec = pltpu.PrefetchScalarGridSpec(
        num_scalar_prefetch=1, grid=(k, R // tr),
        in_specs=[pl.BlockSpec((None, None, tr, C), lambda s, i, core_ref: (s, core_ref[0], i, 0)), blk],
        out_specs=blk)
    return pl.pallas_call(
        body, name=name, grid_spec=grid_spec, out_shape=jax.ShapeDtypeStruct(theirs.shape, BF16),
        compiler_params=_params("parallel", "parallel"))(core, parts, theirs)


def _adamw_math(w, g, m, v):
    m = ADAM_B1 * m + (1.0 - ADAM_B1) * g
    v = ADAM_B2 * v + (1.0 - ADAM_B2) * (g * g)
    m_hat = m / (1.0 - ADAM_B1 ** ADAM_STEP)
    v_hat = v / (1.0 - ADAM_B2 ** ADAM_STEP)
    delta = -ADAM_LR * (m_hat / (jnp.sqrt(v_hat) + ADAM_EPS) + ADAM_WD * w)
    return delta, m, v


def _scattered_pieces(sums, got):
    return [("own", sums)] + [("peer%d" % j, got) for j in range(3)]


def _piece_spec(kind, tr, C):
    if kind == "own":
        return pl.BlockSpec((None, tr, C), lambda i, chip_ref: (chip_ref[0], i, 0))
    if kind == "plain":
        return pl.BlockSpec((tr, C), lambda i, chip_ref: (i, 0))
    return pl.BlockSpec((None, tr, C), functools.partial(lambda j, i, chip_ref: (j, i, 0), int(kind[-1])))


def _adamw_shard(chip, w, m, v, layer, pieces, so_far, name):
    L, R, C = w.shape
    tr = _pick(R, 256, 16)
    n_p = len(pieces)

    def body(chip_ref, w_ref, m_ref, v_ref, *rest):
        g = rest[0][...].astype(F32)
        for p in rest[1:n_p]:
            g = g + p[...].astype(F32)
        delta, nm, nv = _adamw_math(w_ref[...], g, m_ref[...], v_ref[...])
        g_ref, d_ref, nm_ref, nv_ref = rest[-4:]
        g_ref[...] = g
        d_ref[...] = delta
        nm_ref[...] = nm
        nv_ref[...] = nv

    state = pl.BlockSpec((None, tr, C), lambda i, chip_ref: (layer, i, 0))
    carried = [] if so_far is None else list(so_far)
    first_carried = 1 + 3 + n_p
    grid_spec = pltpu.PrefetchScalarGridSpec(
        num_scalar_prefetch=1, grid=(R // tr,),
        in_specs=[state] * 3 + [_piece_spec(kind, tr, C) for kind, _ in pieces] + [_ANY] * len(carried),
        out_specs=[state] * 4)
    return pl.pallas_call(
        body, name=name, grid_spec=grid_spec, out_shape=[jax.ShapeDtypeStruct((L, R, C), F32)] * 4,
        input_output_aliases={first_carried + t: t for t in range(len(carried))},
        compiler_params=_params("parallel"))(chip, w, m, v, *[a for _, a in pieces], *carried)


def _adamw_small(w, m, v, gathered, name):
    R = w.shape[0]
    tr = _pick(R, 512, SUBLANES)

    def body(w_ref, m_ref, v_ref, gg_ref, g_ref, d_ref, nm_ref, nv_ref):
        g = gg_ref[0, 0]
        for k in range(4):
            for c in range(2):
                if c or k:
                    g = g + gg_ref[k, c]
        delta, nm, nv = _adamw_math(w_ref[...], g, m_ref[...], v_ref[...])
        g_ref[...] = g
        d_ref[...] = delta
        nm_ref[...] = nm
        nv_ref[...] = nv

    row = pl.BlockSpec((tr, LANES), lambda i: (i, 0))
    out = jax.ShapeDtypeStruct((R, LANES), F32)
    return pl.pallas_call(
        body, name=name, grid=(R // tr,),
        in_specs=[row, row, row, pl.BlockSpec((4, 2, tr, LANES), lambda i: (0, 0, i, 0))], out_specs=[row] * 4,
        out_shape=[out] * 4, compiler_params=_params("parallel"))(w, m, v, gathered)


def _layer_fwd(x, w, s, tag, sched):
    h = _rms_fwd(x, s["norm_mix_g"], f"rms_mix_{tag}")
    z = sched.carry(f"in_proj_{tag}", _mm, h, w["w_in_t"], "nt", F32, tn=512)
    q, k, v = (z[:, a:b].reshape(z.shape[0], -1, HEAD_DIM).transpose(1, 0, 2)
               for a, b in ((0, OFF_K), (OFF_K, OFF_V), (OFF_V, OFF_U)))
    ya = sched.carry(f"attn_fwd_{tag}", _attn_fwd, q, k, v, s["q_norm_g"], s["k_norm_g"], s["attn_sinks"])
    ya = ya.transpose(1, 0, 2).reshape(z.shape[0], ATTN_WIDTH)
    y, s_r, s_i = sched.carry(f"ssm_fwd_{tag}", _ssm_fwd, z, *s["ssm16"], s["ssm_d"])
    pre, y3 = _glu_fwd(y, w["ssm_glu_w"], s["ssm_glu_b"], f"glu_fwd_{tag}")
    a, b, merged = sched.carry(f"merge_fwd_{tag}", _merge_fwd, ya, y3, w["w_attn_branch"], w["w_ssm_branch"], z,
                               s["gate_bias"])
    x1 = _mm(merged, w["w_out"], "nn", F32, f"out_proj_{tag}", residual=x)
    h2 = _rms_fwd(x1, s["norm_ffn_g"], f"rms_ffn_{tag}")
    g, u, act = sched.carry(f"ffn_fwd_{tag}", _ffn_fwd, h2, w["w_ffn_in"])
    x2 = sched.carry(f"ffn_out_{tag}", _mm, act, w["w_ffn_out"], "nn", F32, residual=x1)
    saved = dict(x=x, h=h, z=z, ya=ya, y=y, s_r=s_r, s_i=s_i, pre=pre, y3=y3, a=a, b=b, merged=merged,
                 x1=x1, h2=h2, g=g, u=u, act=act)
    return x2, saved


def _layer_bwd(dx2, sv, w, s, tag, sched, scatter):
    gw, gs = {}, {}
    dg16, du16 = _ffn_bwd_act(dx2, w["w_ffn_out"], sv["g"], sv["u"], f"ffn_bwd_act_{tag}")
    gw["w_ffn_out"] = _mm(sv["act"], dx2, "tn", BF16, f"dw_ffn_out_{tag}", tm=1408)
    dh2 = _ffn_bwd_in(dg16, du16, w["w_ffn_in"], f"dh2_{tag}")
    gw["w_ffn_in"] = _ffn_bwd_w(sv["h2"], dg16, du16, f"dw_ffn_in_{tag}")
    scatter(FFN_WEIGHTS[:1], gw, f"attn_bwd_{tag}")
    dx1, gs["norm_ffn_g"] = _rms_bwd(dh2, sv["x1"], s["norm_ffn_g"], dx2, f"rms_ffn_bwd_{tag}")
    dm = _mm(dx1, w["w_out"], "nt", F32, f"dmerged_{tag}")
    gw["w_out"] = _mm(sv["merged"], dx1, "tn", BF16, f"dw_out_{tag}")
    da16, db16, dza, dzs, dba, dbs = _merge_bwd(dm, sv["a"], sv["b"], sv["z"], s["gate_bias"], f"merge_bwd_{tag}")
    gs["gate_bias"] = jnp.concatenate([dba, dbs], axis=1)
    dya = _mm(da16, w["w_attn_branch"], "nt", F32, f"dya_{tag}")
    gw["w_attn_branch"] = _mm(sv["ya"], da16, "tn", BF16, f"dw_attn_branch_{tag}")
    dy3 = _mm(db16, w["w_ssm_branch"], "nt", F32, f"dy3_{tag}")
    gw["w_ssm_branch"] = _mm(sv["y3"], db16, "tn", BF16, f"dw_ssm_branch_{tag}")
    dpre16, t1, y2_16, gs["ssm_glu_b"] = _glu_bwd_gate(dy3, sv["y"], sv["pre"], f"glu_bwd_gate_{tag}")
    dy = _glu_bwd_in(dpre16, w["ssm_glu_w"], t1, sv["y"], f"glu_bwd_in_{tag}")
    gw["ssm_glu_w"] = _mm(y2_16, dpre16, "tn", BF16, f"dw_glu_{tag}")
    scatter(FFN_WEIGHTS[1:] + MIXER_WEIGHTS[1:], gw, f"ssm_bwd_{tag}")
    du_ssm, *gs["ssm_disc"], gs["ssm_d"] = sched.carry(f"ssm_bwd_{tag}", _ssm_bwd, dy, sv["z"], sv["s_r"], sv["s_i"],
                                                       *s["ssm16"], s["ssm_d"])
    dq, dk, dv, gs["q_norm_g"], gs["k_norm_g"], dsk = sched.carry(
        f"attn_bwd_{tag}", _attn_bwd, sv["z"], dya, s["q_norm_g"], s["k_norm_g"], s["attn_sinks"])
    gs["attn_sinks"] = dsk[:, 0].reshape(1, N_Q_HEADS)
    dz = [dq, jnp.concatenate([dk, dv], axis=1).astype(BF16), du_ssm, dza, dzs]
    gw["w_in"] = _mm_cols_tn(dz, sv["h"], f"dw_in_t_{tag}", tn=2048)
    scatter(MIXER_WEIGHTS[:1], gw, f"dh_{tag}")
    dh = sched.carry(f"dh_{tag}", _mm_cols_nn, dz, w["w_in_t"])
    dx, gs["norm_mix_g"] = _rms_bwd(dh, sv["x"], s["norm_mix_g"], dx1, f"rms_mix_bwd_{tag}")
    return dx, gs


def _shard_to_send(name, shard):
    return (shard.T if name == "w_in" else shard).astype(BF16)


def _assemble(name, gathered):
    if name == "w_ffn_in":
        return gathered
    if name in COL_SHARDED and name != "w_in":
        rows = gathered.shape[2]
        return gathered.transpose(2, 0, 1, 3).reshape(rows, -1)
    return gathered.reshape(-1, gathered.shape[3])


class _Weights:
    def __init__(self):
        self.sources, self.ready = [], {}

    def __getitem__(self, name):
        if name not in self.ready:
            key = "w_in" if name == "w_in_t" else name
            (gathered,) = [src[key] for src in self.sources if key in src]
            self.ready[name] = _assemble(key, gathered)
        return self.ready[name]


def _disassemble(name, grad):
    if name == "w_ffn_in":
        return grad
    if name in COL_SHARDED and name != "w_in":
        rows, cols = grad.shape
        return grad.reshape(rows, 4, 2, cols // N_DEV).transpose(1, 2, 0, 3)
    rows, cols = grad.shape
    return grad.reshape(4, 2, rows // N_DEV, cols)


def _pack(arrays):
    flat = jnp.concatenate([a.reshape(-1) for a in arrays])
    pad = (-flat.shape[0]) % (SUBLANES * LANES)
    return jnp.pad(flat, (0, pad)).reshape(-1, LANES)


def _unpack(packed, like):
    flat, out, off = packed.reshape(-1), [], 0
    for a in like:
        out.append(flat[off:off + a.size].reshape(a.shape))
        off += a.size
    return out


def kernel(x, norm_mix_g, w_in, gate_bias, q_norm_g, k_norm_g, attn_sinks, ssm_lambda_re, ssm_lambda_im, ssm_log_dt, ssm_b_re, ssm_b_im, ssm_c_re, ssm_c_im, ssm_d, ssm_glu_w, ssm_glu_b, w_attn_branch, w_ssm_branch, w_out, norm_ffn_g, w_ffn_in, w_ffn_out, loss_target, m_norm_mix_g, m_w_in, m_gate_bias, m_q_norm_g, m_k_norm_g, m_attn_sinks, m_ssm_lambda_re, m_ssm_lambda_im, m_ssm_log_dt, m_ssm_b_re, m_ssm_b_im, m_ssm_c_re, m_ssm_c_im, m_ssm_d, m_ssm_glu_w, m_ssm_glu_b, m_w_attn_branch, m_w_ssm_branch, m_w_out, m_norm_ffn_g, m_w_ffn_in, m_w_ffn_out, v_norm_mix_g, v_w_in, v_gate_bias, v_q_norm_g, v_k_norm_g, v_attn_sinks, v_ssm_lambda_re, v_ssm_lambda_im, v_ssm_log_dt, v_ssm_b_re, v_ssm_b_im, v_ssm_c_re, v_ssm_c_im, v_ssm_d, v_ssm_glu_w, v_ssm_glu_b, v_w_attn_branch, v_w_ssm_branch, v_w_out, v_norm_ffn_g, v_w_ffn_in, v_w_ffn_out):
    given = dict(locals())
    wts = {n: given[n] for n in WEIGHTS}
    mom = {n: given["m_" + n] for n in WEIGHTS}
    var = {n: given["v_" + n] for n in WEIGHTS}
    depth = w_in.shape[0]
    xs = x[0]
    target = loss_target[0]

    sched = _Schedule()
    shards = [{n: _shard_to_send(n, wts[n][l]) for n in BIG} for l in range(depth)]
    full = [_Weights() for _ in range(depth)]

    def gather(layer, names, send_host, pass_host):
        sent = sched.ride(send_host, _gather_send, names, shards[layer])
        full[layer].sources.append(sched.ride(pass_host, _gather_pass, names, sent))

    first = MIXER_WEIGHTS[:1]
    full[0].sources.append(dict(zip(first, _all_gather([shards[0][n] for n in first], "gather_w_in_0"))))
    gather(0, MIXER_WEIGHTS[1:], "in_proj_0", "attn_fwd_0")
    gather(0, ("w_ffn_in",), "attn_fwd_0", "ssm_fwd_0")
    gather(0, ("w_ffn_out",), "ssm_fwd_0", "merge_fwd_0")
    for l in range(1, depth):
        gather(l, first, f"ffn_fwd_{l - 1}", f"ffn_out_{l - 1}")
        gather(l, MIXER_WEIGHTS[1:], f"ffn_out_{l - 1}", f"in_proj_{l}")
        gather(l, ("w_ffn_out",), f"in_proj_{l}", f"attn_fwd_{l}")
        gather(l, ("w_ffn_in",), f"attn_fwd_{l}", f"ssm_fwd_{l}")

    small, disc_vjp = [], []
    for l in range(depth):
        s = {n: wts[n][l].reshape(1, -1) for n in ("norm_mix_g", "gate_bias", "q_norm_g", "k_norm_g", "attn_sinks",
                                                   "ssm_d", "ssm_glu_b", "norm_ffn_g")}
        disc, vjp = jax.vjp(_ssm_discretize, *[wts[n][l] for n in ("ssm_lambda_re", "ssm_lambda_im", "ssm_log_dt",
                                                                  "ssm_b_re", "ssm_b_im", "ssm_c_re", "ssm_c_im")])
        s["ssm16"] = (disc[0], disc[1]) + tuple(d.astype(BF16) for d in disc[2:])
        small.append(s)
        disc_vjp.append(vjp)

    act, saved = xs, []
    for l in range(depth):
        act, sv = _layer_fwd(act, full[l], small[l], str(l), sched)
        saved.append(sv)
    dact, loss_local = _loss_grad(act, target, "loss_head")

    out = {"grad": {}, "delta": {}, "new_m": {}, "new_v": {}}
    results = {n: None for n in BIG}
    core = lax.axis_index("c").astype(jnp.int32).reshape(1)
    chip = (2 * lax.axis_index("x") + lax.axis_index("y")).astype(jnp.int32).reshape(1)
    state = {n: [a.transpose(0, 2, 1) if n == "w_in" else a for a in (wts[n], mom[n], var[n])] for n in BIG}
    small_grads = [None] * depth
    for l in reversed(range(depth)):
        scattered = []

        def scatter(names, grads, host):
            parts = [_disassemble(n, grads[n]) for n in names]
            theirs = _pair_exchange(parts, f"grad_pair_exchange_{host}")
            sums = {n: _add_pair(core, p, t, f"grad_pair_sum_{n}_{host}") for n, p, t in zip(names, parts, theirs)}
            scattered.append((sums, sched.ride(host, _scatter_send, names, sums)))

        dact, gs = _layer_bwd(dact, saved[l], full[l], small[l], str(l), sched, scatter)
        (gs["ssm_lambda_re"], gs["ssm_lambda_im"], gs["ssm_log_dt"], gs["ssm_b_re"], gs["ssm_b_im"], gs["ssm_c_re"],
         gs["ssm_c_im"]) = disc_vjp[l](tuple(gs.pop("ssm_disc")))
        small_grads[l] = gs
        for sums, got in scattered:
            for n in got:
                results[n] = _adamw_shard(chip, state[n][0], state[n][1], state[n][2], l,
                                          _scattered_pieces(sums[n], got[n]), results[n], f"adamw_{n}_{l}")
    loss = lax.psum(loss_local, ("x", "y", "c"))
    for n in BIG:
        for kind, res in zip(("grad", "delta", "new_m", "new_v"), results[n]):
            out[kind][n] = res.transpose(0, 2, 1) if n == "w_in" else res

    like = [wts[n] for n in SMALL]
    g_small = _pack([jnp.stack([small_grads[l][n].reshape(wts[n].shape[1:]) for l in range(depth)]) for n in SMALL])
    (gathered_small,) = _all_gather([g_small], "gather_small_grads")
    res = _adamw_small(_pack(like), _pack([mom[n] for n in SMALL]), _pack([var[n] for n in SMALL]), gathered_small,
                       "adamw_small")
    for kind, packed in zip(("grad", "delta", "new_m", "new_v"), res):
        for n, a in zip(SMALL, _unpack(packed, like)):
            out[kind][n] = a

    grad_x = dact.reshape(x.shape)
    return (loss, grad_x, *[out["grad"][n] for n in WEIGHTS], *[out["delta"][n] for n in WEIGHTS],
            *[out["new_m"][n] for n in WEIGHTS], *[out["new_v"][n] for n in WEIGHTS])
```

```python
import functools
import math

import jax
import jax.numpy as jnp
from jax import lax
from jax.experimental import pallas as pl
from jax.experimental.pallas import tpu as pltpu

F32, BF16 = jnp.float32, jnp.bfloat16
MESH = pl.DeviceIdType.MESH

D_MODEL = 2048
HEAD_DIM = 64
N_Q_HEADS = 16
N_KV_HEADS = 4
GQA_GROUP = N_Q_HEADS // N_KV_HEADS
ATTN_WIDTH = N_Q_HEADS * HEAD_DIM
KV_WIDTH = N_KV_HEADS * HEAD_DIM
WINDOW = 128
BLOCK = 128
SSM_WIDTH = D_MODEL // 2
SSM_GROUP_CH = 16
SSM_GROUPS = SSM_WIDTH // SSM_GROUP_CH
SSM_STATE = 64
D_FF = 5632
OFF_K = ATTN_WIDTH
OFF_V = OFF_K + KV_WIDTH
OFF_U = OFF_V + KV_WIDTH
OFF_G = OFF_U + SSM_WIDTH
IN_WIDTH = OFF_G + 2 * D_MODEL
RMS_EPS = 1e-6
ATTN_SCALE = HEAD_DIM ** -0.5
NEG_BIG = -1e30

SSM_NGB = 4
SSM_GB_CH = SSM_WIDTH // SSM_NGB
SSM_GB_ST = SSM_GROUPS * SSM_STATE // SSM_NGB
SUBLANES = 8
LANES = 128
SSM_TT = 512

ADAM_LR = 0.001
ADAM_B1 = 0.9
ADAM_B2 = 0.999
ADAM_EPS = 1e-08
ADAM_WD = 0.01
ADAM_STEP = 10

N_DEV = 8
VMEM_LIMIT_BYTES = 52 * 1024 * 1024

BIG = ("w_in", "ssm_glu_w", "w_attn_branch", "w_ssm_branch", "w_out", "w_ffn_in", "w_ffn_out")
COL_SHARDED = ("w_in", "w_attn_branch", "w_ssm_branch", "w_ffn_in")
FFN_WEIGHTS = ("w_ffn_in", "w_ffn_out")
MIXER_WEIGHTS = ("w_in", "ssm_glu_w", "w_attn_branch", "w_ssm_branch", "w_out")
SMALL = ("norm_mix_g", "gate_bias", "q_norm_g", "k_norm_g", "attn_sinks", "ssm_lambda_re", "ssm_lambda_im",
         "ssm_log_dt", "ssm_b_re", "ssm_b_im", "ssm_c_re", "ssm_c_im", "ssm_d", "ssm_glu_b", "norm_ffn_g")
WEIGHTS = ("norm_mix_g", "w_in", "gate_bias", "q_norm_g", "k_norm_g", "attn_sinks", "ssm_lambda_re", "ssm_lambda_im",
           "ssm_log_dt", "ssm_b_re", "ssm_b_im", "ssm_c_re", "ssm_c_im", "ssm_d", "ssm_glu_w", "ssm_glu_b",
           "w_attn_branch", "w_ssm_branch", "w_out", "norm_ffn_g", "w_ffn_in", "w_ffn_out")


def _pick(n, target, mult=LANES):
    best = None
    for t in range(mult, min(n, target) + 1, mult):
        if n % t == 0:
            best = t
    return n if best is None else best


def _params(*sem):
    return pltpu.CompilerParams(dimension_semantics=sem, vmem_limit_bytes=VMEM_LIMIT_BYTES)


def _sigmoid(v):
    return 1.0 / (1.0 + jnp.exp(-v))


_GELU_C = math.sqrt(2.0 / math.pi)


def _gelu(v):
    return 0.5 * v * (1.0 + jnp.tanh(_GELU_C * (v + 0.044715 * v * v * v)))


def _gelu_grad(v):
    t = jnp.tanh(_GELU_C * (v + 0.044715 * v * v * v))
    return 0.5 * (1.0 + t) + 0.5 * v * (1.0 - t * t) * _GELU_C * (1.0 + 3.0 * 0.044715 * v * v)


_DN = {"nn": (((1,), (0,)), ((), ())), "nt": (((1,), (1,)), ((), ())), "tn": (((0,), (0,)), ((), ()))}


def _dot(a, b, dims="nn"):
    return lax.dot_general(a, b, _DN[dims], preferred_element_type=F32)


class _Comm:
    def __init__(self, args, out_shapes, sems, copies, aliases=None):
        self.args, self.out_shapes, self.sems = list(args), list(out_shapes), list(sems)
        self.copies, self.aliases = copies, dict(aliases or {})


def _call(body, name, grid, in_specs, out_specs, out_shape, scratch_shapes, semantics, args, comm=None):
    in_specs, out_specs, out_shape = list(in_specs), list(out_specs), list(out_shape)
    scratch_shapes = list(scratch_shapes)
    if comm is None:
        res = pl.pallas_call(body, name=name, grid=grid, in_specs=in_specs, out_specs=out_specs, out_shape=out_shape,
                             scratch_shapes=scratch_shapes, compiler_params=_params(*semantics))(*args)
        return list(res), []
    n_in, n_out, n_scr = len(in_specs), len(out_specs), len(scratch_shapes)
    n_cin, n_cout = len(comm.args), len(comm.out_shapes)

    def carrying(*refs):
        ins, cin = refs[:n_in], refs[n_in:n_in + n_cin]
        o0 = n_in + n_cin
        outs, cout = refs[o0:o0 + n_out], refs[o0 + n_out:o0 + n_out + n_cout]
        s0 = o0 + n_out + n_cout
        scr, sems = refs[s0:s0 + n_scr], refs[s0 + n_scr:]
        first = functools.reduce(jnp.logical_and, [pl.program_id(d) == 0 for d in range(len(grid))])
        last = functools.reduce(jnp.logical_and, [pl.program_id(d) == grid[d] - 1 for d in range(len(grid))])

        @pl.when(first)
        def _():
            sends, _, local = comm.copies(cin, cout, sems, False)
            for cp in local + sends:
                cp.start()

        body(*ins, *outs, *scr)

        @pl.when(last)
        def _():
            sends, recvs, local = comm.copies(cin, cout, sems, True)
            for cp in recvs:
                cp.wait_recv()
            for cp in sends:
                cp.wait_send()
            for cp in local:
                cp.wait()

    res = pl.pallas_call(
        carrying, name=name, grid=grid, in_specs=in_specs + [_ANY] * n_cin, out_specs=out_specs + [_ANY] * n_cout,
        out_shape=out_shape + comm.out_shapes, scratch_shapes=scratch_shapes + comm.sems,
        input_output_aliases={n_in + i: n_out + o for i, o in comm.aliases.items()},
        compiler_params=_params(*["arbitrary"] * len(grid)))(*args, *comm.args)
    return list(res[:n_out]), list(res[n_out:])


def _mm(a, b, dims, out_dtype, name, residual=None, tm=1024, tn=1024, tk=2048, comm=None):
    if dims == "tn":
        K, M = a.shape
    else:
        M, K = a.shape
    N = b.shape[0] if dims == "nt" else b.shape[1]
    tm, tn, tk = _pick(M, tm), _pick(N, tn), _pick(K, tk)
    nk = K // tk
    has_res = residual is not None

    def finish(out, refs):
        if has_res:
            out = out + refs[2][...].astype(F32)
        refs[-2][...] = out.astype(out_dtype)

    def body_single(*refs):
        finish(_dot(refs[0][...].astype(BF16), refs[1][...].astype(BF16), dims), refs)

    def body_multi(*refs):
        acc_ref = refs[-1]
        k = pl.program_id(2)

        @pl.when(k == 0)
        def _():
            acc_ref[...] = jnp.zeros_like(acc_ref)

        acc_ref[...] += _dot(refs[0][...].astype(BF16), refs[1][...].astype(BF16), dims)

        @pl.when(k == nk - 1)
        def _():
            finish(acc_ref[...], refs)

    a_spec = (pl.BlockSpec((tk, tm), lambda i, j, k: (k, i)) if dims == "tn"
              else pl.BlockSpec((tm, tk), lambda i, j, k: (i, k)))
    b_spec = (pl.BlockSpec((tn, tk), lambda i, j, k: (j, k)) if dims == "nt"
              else pl.BlockSpec((tk, tn), lambda i, j, k: (k, j)))
    o_spec = pl.BlockSpec((tm, tn), lambda i, j, k: (i, j))
    in_specs = [a_spec, b_spec] + ([o_spec] if has_res else [])
    args = (a, b) + ((residual,) if has_res else ())
    (out,), extra = _call(
        body_single if nk == 1 else body_multi, name, (M // tm, N // tn, nk), in_specs, [o_spec],
        [jax.ShapeDtypeStruct((M, N), out_dtype)], [pltpu.VMEM((tm, tn) if nk > 1 else (SUBLANES, LANES), F32)],
        ("parallel", "parallel", "arbitrary"), args, comm)
    return out if comm is None else (out, extra)


PIECE_BLOCK = 512


def _piece_blocks(pieces):
    counts = [p.shape[1] // PIECE_BLOCK for p in pieces]
    return counts, [sum(counts[:i]) for i in range(len(counts))]


def _mm_cols_nn(pieces, b, name, comm=None, tm=1024, tn=1024):
    M, N = pieces[0].shape[0], b.shape[1]
    tm, tn = _pick(M, tm), _pick(N, tn)
    counts, starts = _piece_blocks(pieces)
    n, nk = len(pieces), sum(counts)

    def body(*refs):
        b_ref, o_ref, acc_ref = refs[n:]
        k = pl.program_id(2)

        @pl.when(k == 0)
        def _():
            acc_ref[...] = jnp.zeros_like(acc_ref)

        for a_ref, c, s in zip(refs[:n], counts, starts):
            @pl.when((k >= s) & (k < s + c))
            def _(a_ref=a_ref):
                acc_ref[...] += _dot(a_ref[...], b_ref[...])

        @pl.when(k == nk - 1)
        def _():
            o_ref[...] = acc_ref[...]

    a_specs = [pl.BlockSpec((tm, PIECE_BLOCK), functools.partial(lambda s, c, i, j, k: (i, jnp.clip(k - s, 0, c - 1)), s, c))
               for c, s in zip(counts, starts)]
    b_spec = pl.BlockSpec((PIECE_BLOCK, tn), lambda i, j, k: (k, j))
    o_spec = pl.BlockSpec((tm, tn), lambda i, j, k: (i, j))
    (out,), extra = _call(body, name, (M // tm, N // tn, nk), a_specs + [b_spec], [o_spec],
                          [jax.ShapeDtypeStruct((M, N), F32)], [pltpu.VMEM((tm, tn), F32)],
                          ("parallel", "parallel", "arbitrary"), (*pieces, b), comm)
    return out if comm is None else (out, extra)


def _mm_cols_tn(pieces, b, name, tn=1024, tk=2048):
    K, N = b.shape
    tn, tk = _pick(N, tn), _pick(K, tk)
    counts, starts = _piece_blocks(pieces)
    n, nk = len(pieces), K // tk

    def body(*refs):
        b_ref, o_ref, acc_ref = refs[n:]
        i, k = pl.program_id(0), pl.program_id(2)

        @pl.when(k == 0)
        def _():
            acc_ref[...] = jnp.zeros_like(acc_ref)

        for a_ref, c, s in zip(refs[:n], counts, starts):
            @pl.when((i >= s) & (i < s + c))
            def _(a_ref=a_ref):
                acc_ref[...] += _dot(a_ref[...], b_ref[...], "tn")

        @pl.when(k == nk - 1)
        def _():
            o_ref[...] = acc_ref[...].astype(BF16)

    def a_index(s, c, i, j, k):
        mine = (i >= s) & (i < s + c)
        return jnp.where(mine, k, 0), jnp.clip(i - s, 0, c - 1)

    a_specs = [pl.BlockSpec((tk, PIECE_BLOCK), functools.partial(a_index, s, c)) for c, s in zip(counts, starts)]
    b_spec = pl.BlockSpec((tk, tn), lambda i, j, k: (k, j))
    o_spec = pl.BlockSpec((PIECE_BLOCK, tn), lambda i, j, k: (i, j))
    (out,), _ = _call(body, name, (sum(counts), N // tn, nk), a_specs + [b_spec], [o_spec],
                      [jax.ShapeDtypeStruct((sum(counts) * PIECE_BLOCK, N), BF16)],
                      [pltpu.VMEM((PIECE_BLOCK, tn), F32)], ("parallel", "parallel", "arbitrary"), (*pieces, b))
    return out


def _rms_fwd(x, g, name):
    T, D = x.shape
    tr = _pick(T, 256, SUBLANES)

    def body(x_ref, g_ref, o_ref):
        xf = x_ref[...]
        r = lax.rsqrt(jnp.mean(xf * xf, axis=-1, keepdims=True) + RMS_EPS)
        o_ref[...] = (xf * r * g_ref[...]).astype(BF16)

    return pl.pallas_call(
        body, name=name, grid=(T // tr,),
        in_specs=[pl.BlockSpec((tr, D), lambda i: (i, 0)), pl.BlockSpec((1, D), lambda i: (0, 0))],
        out_specs=pl.BlockSpec((tr, D), lambda i: (i, 0)), out_shape=jax.ShapeDtypeStruct((T, D), BF16),
        compiler_params=_params("parallel"))(x, g)


def _rms_bwd(dh, x, g, dres, name):
    T, D = x.shape
    tr = _pick(T, 256, SUBLANES)

    def body(dh_ref, x_ref, g_ref, dres_ref, dx_ref, dg_ref):
        @pl.when(pl.program_id(0) == 0)
        def _():
            dg_ref[...] = jnp.zeros_like(dg_ref)

        xf = x_ref[...]
        r = lax.rsqrt(jnp.mean(xf * xf, axis=-1, keepdims=True) + RMS_EPS)
        xhat = xf * r
        dhv = dh_ref[...]
        dxh = dhv * g_ref[...]
        dx_ref[...] = dres_ref[...] + r * (dxh - xhat * jnp.mean(dxh * xhat, axis=-1, keepdims=True))
        dg_ref[...] += jnp.sum(dhv * xhat, axis=0, keepdims=True)

    row = pl.BlockSpec((tr, D), lambda i: (i, 0))
    vec = pl.BlockSpec((1, D), lambda i: (0, 0))
    return pl.pallas_call(
        body, name=name, grid=(T // tr,), in_specs=[row, row, vec, row], out_specs=[row, vec],
        out_shape=[jax.ShapeDtypeStruct((T, D), F32), jax.ShapeDtypeStruct((1, D), F32)],
        compiler_params=_params("arbitrary"))(dh, x, g, dres)


def _loss_grad(y, target, name):
    T, D = y.shape
    tr = _pick(T, 256, SUBLANES)

    def body(y_ref, t_ref, dx_ref, loss_ref):
        @pl.when(pl.program_id(0) == 0)
        def _():
            loss_ref[...] = jnp.zeros_like(loss_ref)

        err = y_ref[...] - t_ref[...]
        dx_ref[...] = err * (1.0 / D)
        loss_ref[...] += jnp.sum(jnp.mean(err * err, axis=-1, keepdims=True), axis=0, keepdims=True) * 0.5

    row = pl.BlockSpec((tr, D), lambda i: (i, 0))
    one = pl.BlockSpec((1, 1), lambda i: (0, 0))
    dx, loss = pl.pallas_call(
        body, name=name, grid=(T // tr,), in_specs=[row, row], out_specs=[row, one],
        out_shape=[jax.ShapeDtypeStruct((T, D), F32), jax.ShapeDtypeStruct((1, 1), F32)],
        compiler_params=_params("arbitrary"))(y, target)
    return dx, loss[0, 0]


_STACK = GQA_GROUP * BLOCK


def _attn_mask(n):
    row = lax.broadcasted_iota(jnp.int32, (_STACK, 2 * BLOCK), 0)
    col = lax.broadcasted_iota(jnp.int32, (_STACK, 2 * BLOCK), 1)
    dist = (row % BLOCK) - col + BLOCK
    valid = (dist >= 0) & (dist < WINDOW) & ((col >= BLOCK) | (n > 0))
    return dist.astype(F32), valid


def _per_row_head(kv, sk_ref):
    heads = [kv * GQA_GROUP + g for g in range(GQA_GROUP)]
    slope = jnp.concatenate([jnp.full((BLOCK, 1), 2.0 ** (-8.0 * (h + 1) / N_Q_HEADS), F32) for h in heads], axis=0)
    sink = jnp.concatenate([jnp.broadcast_to(sk_ref[h:h + 1, 0:1], (BLOCK, 1)) for h in heads], axis=0)
    return slope, sink


def _head_cols(h):
    return slice(h * HEAD_DIM, (h + 1) * HEAD_DIM)


def _stack_heads(x, kv):
    return jnp.concatenate([x[:, _head_cols(kv * GQA_GROUP + g)] for g in range(GQA_GROUP)], axis=0)


def _head_norm(v, gain):
    r = lax.rsqrt(jnp.mean(v * v, axis=-1, keepdims=True) + RMS_EPS)
    vhat = v * r
    return r, vhat, vhat * gain


def _attn_probs(qn16, kn16, slope, dist, valid, sink):
    s = _dot(qn16, kn16, "nt") * ATTN_SCALE - slope * dist
    s = jnp.where(valid, s, NEG_BIG)
    m = jnp.maximum(jnp.max(s, axis=-1, keepdims=True), sink)
    p = jnp.exp(s - m)
    ps = jnp.exp(sink - m)
    den = jnp.sum(p, axis=-1, keepdims=True) + ps
    return p, ps, den


def _attn_specs():
    wide = pl.BlockSpec((BLOCK, ATTN_WIDTH), lambda n: (n, 0))
    cur = [pl.BlockSpec((BLOCK, KV_WIDTH), functools.partial(lambda c, n: (n, c), off // KV_WIDTH))
           for off in (OFF_K, OFF_V)]
    prev = [pl.BlockSpec((BLOCK, KV_WIDTH), functools.partial(lambda c, n: (jnp.maximum(n - 1, 0), c), off // KV_WIDTH))
            for off in (OFF_K, OFF_V)]
    gain = pl.BlockSpec((1, HEAD_DIM), lambda n: (0, 0))
    sink = pl.BlockSpec((N_Q_HEADS, LANES), lambda n: (0, 0))
    return wide, [cur[0], prev[0], cur[1], prev[1]], gain, sink


def _sink_rows(sinks):
    return jnp.broadcast_to(sinks.reshape(N_Q_HEADS, 1), (N_Q_HEADS, LANES))


def _attn_fwd(q, k, v, qg, kg, sinks, name, comm=None):
    T = q.shape[1]

    def body(q_ref, kc_ref, kp_ref, vc_ref, vp_ref, qg_ref, kg_ref, sk_ref, o_ref):
        dist, valid = _attn_mask(pl.program_id(1))
        row_head = pl.program_id(0) * GQA_GROUP + lax.broadcasted_iota(jnp.int32, (_STACK, 1), 0) // BLOCK
        slope = jnp.exp((row_head + 1).astype(F32) * (-8.0 / N_Q_HEADS * math.log(2.0)))
        first_head = pl.program_id(0) * GQA_GROUP
        sink = jnp.concatenate([jnp.broadcast_to(sk_ref[pl.ds(first_head + g, 1), 0:1], (BLOCK, 1))
                                for g in range(GQA_GROUP)], axis=0)
        kk = jnp.concatenate([kp_ref[...], kc_ref[...]], axis=0)
        _, _, kn = _head_norm(kk, kg_ref[...])
        v16 = jnp.concatenate([vp_ref[...], vc_ref[...]], axis=0).astype(BF16)
        _, _, qn = _head_norm(q_ref[...].reshape(_STACK, HEAD_DIM), qg_ref[...])
        p, _, den = _attn_probs(qn.astype(BF16), kn.astype(BF16), slope, dist, valid, sink)
        o_ref[...] = _dot((p / den).astype(BF16), v16).reshape(GQA_GROUP, BLOCK, HEAD_DIM).astype(BF16)

    q_spec = pl.BlockSpec((GQA_GROUP, BLOCK, HEAD_DIM), lambda h, n: (h, n, 0))
    cur = pl.BlockSpec((None, BLOCK, HEAD_DIM), lambda h, n: (h, n, 0))
    prev = pl.BlockSpec((None, BLOCK, HEAD_DIM), lambda h, n: (h, jnp.maximum(n - 1, 0), 0))
    gain = pl.BlockSpec((1, HEAD_DIM), lambda h, n: (0, 0))
    sink = pl.BlockSpec((N_Q_HEADS, LANES), lambda h, n: (0, 0))
    (out,), extra = _call(
        body, name, (N_KV_HEADS, T // BLOCK), [q_spec, cur, prev, cur, prev, gain, gain, sink], [q_spec],
        [jax.ShapeDtypeStruct((N_Q_HEADS, T, HEAD_DIM), BF16)], [], ("parallel", "parallel"),
        (q, k, k, v, v, qg, kg, _sink_rows(sinks)), comm)
    return out if comm is None else (out, extra)


def _attn_bwd(z, do, qg, kg, sinks, name, comm=None):
    T = z.shape[0]

    def body(q_ref, kc_ref, kp_ref, vc_ref, vp_ref, do_ref, qg_ref, kg_ref, sk_ref,
             dq_ref, dk_ref, dv_ref, dqg_ref, dkg_ref, dsk_ref):
        n = pl.program_id(0)

        @pl.when(n == 0)
        def _():
            for ref in (dk_ref, dv_ref, dqg_ref, dkg_ref, dsk_ref):
                ref[...] = jnp.zeros_like(ref)

        dist, valid = _attn_mask(n)
        rows = pl.ds(pl.multiple_of(n * BLOCK, BLOCK), BLOCK)
        before = pl.ds(pl.multiple_of(jnp.maximum(n - 1, 0) * BLOCK, BLOCK), BLOCK)
        q = q_ref[...]
        do = do_ref[...]
        kk = jnp.concatenate([kp_ref[...], kc_ref[...]], axis=0)
        vv = jnp.concatenate([vp_ref[...], vc_ref[...]], axis=0)
        for kv in range(N_KV_HEADS):
            slope, sink = _per_row_head(kv, sk_ref)
            rk, khat, kn = _head_norm(kk[:, _head_cols(kv)], kg_ref[...])
            kn16 = kn.astype(BF16)
            v16 = vv[:, _head_cols(kv)].astype(BF16)
            rq, qhat, qn = _head_norm(_stack_heads(q, kv), qg_ref[...])
            qn16 = qn.astype(BF16)
            p, ps, den = _attn_probs(qn16, kn16, slope, dist, valid, sink)
            pn = p / den
            do16 = _stack_heads(do, kv).astype(BF16)
            dp = _dot(do16, v16, "nt")
            delta = jnp.sum(pn * dp, axis=-1, keepdims=True)
            ds16 = (pn * (dp - delta)).astype(BF16)
            sink_pull = ps / den * delta
            dqn = _dot(ds16, kn16) * ATTN_SCALE
            dkn = _dot(ds16, qn16, "tn") * ATTN_SCALE
            dv = _dot(pn.astype(BF16), do16, "tn")
            dqh = dqn * qg_ref[...]
            dq = rq * (dqh - qhat * jnp.mean(dqh * qhat, axis=-1, keepdims=True))
            for g in range(GQA_GROUP):
                h = kv * GQA_GROUP + g
                dsink = jnp.sum(sink_pull[g * BLOCK:(g + 1) * BLOCK], axis=0, keepdims=True)
                dsk_ref[h:h + 1, :] -= jnp.broadcast_to(dsink, (1, LANES))
                dq_ref[:, _head_cols(h)] = dq[g * BLOCK:(g + 1) * BLOCK].astype(BF16)
            dqg_ref[...] += jnp.sum(dqn * qhat, axis=0, keepdims=True)
            dkg_ref[...] += jnp.sum(dkn * khat, axis=0, keepdims=True)
            dkh = dkn * kg_ref[...]
            dk = rk * (dkh - khat * jnp.mean(dkh * khat, axis=-1, keepdims=True))
            dk_ref[rows, _head_cols(kv)] += dk[BLOCK:]
            dv_ref[rows, _head_cols(kv)] += dv[BLOCK:]
            dk_ref[before, _head_cols(kv)] += dk[:BLOCK]
            dv_ref[before, _head_cols(kv)] += dv[:BLOCK]

    wide, kv_specs, gain, sink = _attn_specs()
    whole = pl.BlockSpec((T, KV_WIDTH), lambda n: (0, 0))
    outs, extra = _call(
        body, name, (T // BLOCK,), [wide] + kv_specs + [wide, gain, gain, sink],
        [wide, whole, whole, gain, gain, sink],
        [jax.ShapeDtypeStruct((T, ATTN_WIDTH), BF16), jax.ShapeDtypeStruct((T, KV_WIDTH), F32),
         jax.ShapeDtypeStruct((T, KV_WIDTH), F32), jax.ShapeDtypeStruct((1, HEAD_DIM), F32),
         jax.ShapeDtypeStruct((1, HEAD_DIM), F32), jax.ShapeDtypeStruct((N_Q_HEADS, LANES), F32)],
        [], ("arbitrary",), (z, z, z, z, z, do, qg, kg, _sink_rows(sinks)), comm)
    return outs if comm is None else (outs, extra)


def _ssm_discretize(lam_re, lam_im, log_dt, b_re, b_im, c_re, c_im):
    dt = jnp.exp(log_dt)[:, None]
    mag = jnp.exp(lam_re * dt)
    ar = mag * jnp.cos(lam_im * dt)
    ai = mag * jnp.sin(lam_im * dt)
    den = lam_re * lam_re + lam_im * lam_im
    fr = ((ar - 1.0) * lam_re + ai * lam_im) / den
    fi = (ai * lam_re - (ar - 1.0) * lam_im) / den
    bbar_r = fr[:, :, None] * b_re - fi[:, :, None] * b_im
    bbar_i = fr[:, :, None] * b_im + fi[:, :, None] * b_re
    gl = SSM_GROUPS // SSM_NGB
    eye = jnp.eye(gl, dtype=F32)

    def tiles(a):
        return a.reshape(SSM_NGB, SUBLANES, LANES)

    def bdiag(bb):
        return jnp.einsum("bgph,gk->bghkp", bb.reshape(SSM_NGB, gl, SSM_STATE, SSM_GROUP_CH), eye).reshape(
            SSM_NGB, SSM_GB_CH, SSM_GB_ST)

    def cdiag(cc):
        return jnp.einsum("bghp,gk->bgpkh", cc.reshape(SSM_NGB, gl, SSM_GROUP_CH, SSM_STATE), eye).reshape(
            SSM_NGB, SSM_GB_ST, SSM_GB_CH)

    return tiles(ar), tiles(ai), bdiag(bbar_r), bdiag(bbar_i), cdiag(c_re), cdiag(c_im)


def _to_time_major(dst, val, tt, first_row=0):
    for j in range(SUBLANES):
        dst[pl.ds(first_row + j, tt, stride=SUBLANES), :] = val[:, j * LANES:(j + 1) * LANES]


def _from_time_major(dst, src, tt):
    for j in range(SUBLANES):
        dst[:, j * LANES:(j + 1) * LANES] = src[pl.ds(j, tt, stride=SUBLANES), :]


def _ssm_fwd(z, ar, ai, bbr, bbi, cbr, cbi, dskip, name, comm=None):
    T = z.shape[0]
    tt = min(SSM_TT, T)
    nt = T // tt

    def body(u_ref, ar_ref, ai_ref, br_ref, bi_ref, cr_ref, ci_ref, d_ref, y_ref, sr_ref, si_ref,
             tmr, tmi, car_r, car_i):
        @pl.when(pl.program_id(1) == 0)
        def _():
            car_r[...] = jnp.zeros_like(car_r)
            car_i[...] = jnp.zeros_like(car_i)

        u = u_ref[...]
        u16 = u.astype(BF16)
        _to_time_major(tmr, _dot(u16, br_ref[...]), tt)
        _to_time_major(tmi, _dot(u16, bi_ref[...]), tt)
        a_r = ar_ref[...]
        a_i = ai_ref[...]

        def step(t, carry):
            s_r, s_i = carry
            rows = pl.ds(pl.multiple_of(t * SUBLANES, SUBLANES), SUBLANES)
            n_r = a_r * s_r - a_i * s_i + tmr[rows, :]
            n_i = a_r * s_i + a_i * s_r + tmi[rows, :]
            tmr[rows, :] = n_r
            tmi[rows, :] = n_i
            return n_r, n_i

        s_r, s_i = lax.fori_loop(0, tt, step, (car_r[...], car_i[...]), unroll=8)
        car_r[...] = s_r
        car_i[...] = s_i
        _from_time_major(sr_ref, tmr, tt)
        _from_time_major(si_ref, tmi, tt)
        y_ref[...] = (_dot(sr_ref[...].astype(BF16), cr_ref[...]) - _dot(si_ref[...].astype(BF16), ci_ref[...])
                      + d_ref[...] * u)

    u_spec = pl.BlockSpec((tt, SSM_GB_CH), lambda b, t: (t, OFF_U // SSM_GB_CH + b))
    a_spec = pl.BlockSpec((None, SUBLANES, LANES), lambda b, t: (b, 0, 0))
    b_spec = pl.BlockSpec((None, SSM_GB_CH, SSM_GB_ST), lambda b, t: (b, 0, 0))
    c_spec = pl.BlockSpec((None, SSM_GB_ST, SSM_GB_CH), lambda b, t: (b, 0, 0))
    d_spec = pl.BlockSpec((1, SSM_GB_CH), lambda b, t: (0, b))
    y_spec = pl.BlockSpec((tt, SSM_GB_CH), lambda b, t: (t, b))
    s_spec = pl.BlockSpec((tt, SSM_GB_ST), lambda b, t: (t, b))
    n_state = SSM_NGB * SSM_GB_ST
    outs, extra = _call(
        body, name, (SSM_NGB, nt), [u_spec, a_spec, a_spec, b_spec, b_spec, c_spec, c_spec, d_spec],
        [y_spec, s_spec, s_spec],
        [jax.ShapeDtypeStruct((T, SSM_WIDTH), F32), jax.ShapeDtypeStruct((T, n_state), F32),
         jax.ShapeDtypeStruct((T, n_state), F32)],
        [pltpu.VMEM((tt * SUBLANES, LANES), F32), pltpu.VMEM((tt * SUBLANES, LANES), F32),
         pltpu.VMEM((SUBLANES, LANES), F32), pltpu.VMEM((SUBLANES, LANES), F32)],
        ("parallel", "arbitrary"), (z, ar, ai, bbr, bbi, cbr, cbi, dskip), comm)
    return outs if comm is None else (outs, extra)


def _ssm_bwd(dy, z, s_r, s_i, ar, ai, bbr, bbi, cbr, cbi, dskip, name, comm=None):
    T = z.shape[0]
    tt = min(SSM_TT, T)
    nt = T // tt
    per8 = tt // SUBLANES

    def body(dy_ref, u_ref, sr_ref, si_ref, srp_ref, sip_ref, ar_ref, ai_ref, br_ref, bi_ref, cr_ref, ci_ref, d_ref,
             du_ref, dar_ref, dai_ref, dbr_ref, dbi_ref, dcr_ref, dci_ref, dd_ref,
             tmr, tmi, smr, smi, natr, nati, car_r, car_i):
        tb = pl.program_id(1)
        first_block = tb == nt - 1

        @pl.when(tb == 0)
        def _():
            for ref in (car_r, car_i, dar_ref, dai_ref, dbr_ref, dbi_ref, dcr_ref, dci_ref, dd_ref):
                ref[...] = jnp.zeros_like(ref)

        dy = dy_ref[...]
        dy16 = dy.astype(BF16)
        u = u_ref[...]
        u16 = u.astype(BF16)
        _to_time_major(tmr, _dot(dy16, cr_ref[...], "nt"), tt)
        _to_time_major(tmi, -_dot(dy16, ci_ref[...], "nt"), tt)
        _to_time_major(smr, sr_ref[...], tt, first_row=SUBLANES)
        _to_time_major(smi, si_ref[...], tt, first_row=SUBLANES)
        keep = jnp.where(first_block, 0.0, 1.0)
        for j in range(SUBLANES):
            smr[j:j + 1, :] = srp_ref[SUBLANES - 1:SUBLANES, j * LANES:(j + 1) * LANES] * keep
            smi[j:j + 1, :] = sip_ref[SUBLANES - 1:SUBLANES, j * LANES:(j + 1) * LANES] * keep
        a_r = ar_ref[...]
        a_i = ai_ref[...]

        def step(i, carry):
            n_r, n_i, da_r, da_i = carry
            rows = pl.ds(pl.multiple_of((tt - 1 - i) * SUBLANES, SUBLANES), SUBLANES)
            g_r = tmr[rows, :] + a_r * n_r + a_i * n_i
            g_i = tmi[rows, :] - a_i * n_r + a_r * n_i
            tmr[rows, :] = g_r
            tmi[rows, :] = g_i
            p_r = smr[rows, :]
            p_i = smi[rows, :]
            return g_r, g_i, da_r + g_r * p_r + g_i * p_i, da_i - g_r * p_i + g_i * p_r

        zero = jnp.zeros((SUBLANES, LANES), F32)
        n_r, n_i, da_r, da_i = lax.fori_loop(0, tt, step, (car_r[...], car_i[...], zero, zero), unroll=8)
        car_r[...] = n_r
        car_i[...] = n_i
        dar_ref[...] += da_r
        dai_ref[...] += da_i
        _from_time_major(natr, tmr, tt)
        _from_time_major(nati, tmi, tt)
        dbu_r16 = natr[...].astype(BF16)
        dbu_i16 = nati[...].astype(BF16)
        du_ref[...] = (_dot(dbu_r16, br_ref[...], "nt") + _dot(dbu_i16, bi_ref[...], "nt")
                       + d_ref[...] * dy).astype(BF16)
        dbr_ref[...] += _dot(u16, dbu_r16, "tn")
        dbi_ref[...] += _dot(u16, dbu_i16, "tn")
        dcr_ref[...] += _dot(sr_ref[...].astype(BF16), dy16, "tn")
        dci_ref[...] -= _dot(si_ref[...].astype(BF16), dy16, "tn")
        dd_ref[...] += jnp.sum(dy * u, axis=0, keepdims=True)

    def rev(t):
        return nt - 1 - t

    dy_spec = pl.BlockSpec((tt, SSM_GB_CH), lambda b, t: (rev(t), b))
    u_spec = pl.BlockSpec((tt, SSM_GB_CH), lambda b, t: (rev(t), OFF_U // SSM_GB_CH + b))
    s_spec = pl.BlockSpec((tt, SSM_GB_ST), lambda b, t: (rev(t), b))
    sp_spec = pl.BlockSpec((SUBLANES, SSM_GB_ST), lambda b, t: (jnp.maximum(rev(t) * per8 - 1, 0), b))
    a_spec = pl.BlockSpec((None, SUBLANES, LANES), lambda b, t: (b, 0, 0))
    b_spec = pl.BlockSpec((None, SSM_GB_CH, SSM_GB_ST), lambda b, t: (b, 0, 0))
    c_spec = pl.BlockSpec((None, SSM_GB_ST, SSM_GB_CH), lambda b, t: (b, 0, 0))
    d_spec = pl.BlockSpec((1, SSM_GB_CH), lambda b, t: (0, b))
    tm_shape = pltpu.VMEM((tt * SUBLANES, LANES), F32)
    sm_shape = pltpu.VMEM(((tt + 1) * SUBLANES, LANES), F32)
    nat_shape = pltpu.VMEM((tt, SSM_GB_ST), F32)
    tile = pltpu.VMEM((SUBLANES, LANES), F32)
    outs, extra = _call(
        body, name, (SSM_NGB, nt),
        [dy_spec, u_spec, s_spec, s_spec, sp_spec, sp_spec, a_spec, a_spec, b_spec, b_spec, c_spec, c_spec, d_spec],
        [dy_spec, a_spec, a_spec, b_spec, b_spec, c_spec, c_spec, d_spec],
        [jax.ShapeDtypeStruct((T, SSM_WIDTH), BF16),
         jax.ShapeDtypeStruct((SSM_NGB, SUBLANES, LANES), F32),
         jax.ShapeDtypeStruct((SSM_NGB, SUBLANES, LANES), F32),
         jax.ShapeDtypeStruct((SSM_NGB, SSM_GB_CH, SSM_GB_ST), F32),
         jax.ShapeDtypeStruct((SSM_NGB, SSM_GB_CH, SSM_GB_ST), F32),
         jax.ShapeDtypeStruct((SSM_NGB, SSM_GB_ST, SSM_GB_CH), F32),
         jax.ShapeDtypeStruct((SSM_NGB, SSM_GB_ST, SSM_GB_CH), F32),
         jax.ShapeDtypeStruct((1, SSM_WIDTH), F32)],
        [tm_shape, tm_shape, sm_shape, sm_shape, nat_shape, nat_shape, tile, tile], ("parallel", "arbitrary"),
        (dy, z, s_r, s_i, s_r, s_i, ar, ai, bbr, bbi, cbr, cbi, dskip), comm)
    return outs if comm is None else (outs, extra)


def _glu_fwd(y, w, b, name):
    T, W = y.shape
    tm = _pick(T, 512)

    def body(y_ref, w_ref, b_ref, pre_ref, y3_ref):
        y2 = _gelu(y_ref[...])
        pre = _dot(y2.astype(BF16), w_ref[...]) + b_ref[...]
        pre_ref[...] = pre
        y3_ref[...] = (y2 * _sigmoid(pre)).astype(BF16)

    row = pl.BlockSpec((tm, W), lambda i: (i, 0))
    return pl.pallas_call(
        body, name=name, grid=(T // tm,),
        in_specs=[row, pl.BlockSpec((W, W), lambda i: (0, 0)), pl.BlockSpec((1, W), lambda i: (0, 0))],
        out_specs=[row, row], out_shape=[jax.ShapeDtypeStruct((T, W), F32), jax.ShapeDtypeStruct((T, W), BF16)],
        compiler_params=_params("parallel"))(y, w, b)


def _glu_bwd_gate(dy3, y, pre, name):
    T, W = y.shape
    tm = _pick(T, 512)

    def body(dy3_ref, y_ref, pre_ref, dpre_ref, t1_ref, y2_ref, db_ref):
        @pl.when(pl.program_id(0) == 0)
        def _():
            db_ref[...] = jnp.zeros_like(db_ref)

        y2 = _gelu(y_ref[...])
        sg = _sigmoid(pre_ref[...])
        dy3 = dy3_ref[...]
        dpre = dy3 * y2 * sg * (1.0 - sg)
        dpre_ref[...] = dpre.astype(BF16)
        t1_ref[...] = dy3 * sg
        y2_ref[...] = y2.astype(BF16)
        db_ref[...] += jnp.sum(dpre, axis=0, keepdims=True)

    row = pl.BlockSpec((tm, W), lambda i: (i, 0))
    vec = pl.BlockSpec((1, W), lambda i: (0, 0))
    return pl.pallas_call(
        body, name=name, grid=(T // tm,), in_specs=[row, row, row], out_specs=[row, row, row, vec],
        out_shape=[jax.ShapeDtypeStruct((T, W), BF16), jax.ShapeDtypeStruct((T, W), F32),
                   jax.ShapeDtypeStruct((T, W), BF16), jax.ShapeDtypeStruct((1, W), F32)],
        compiler_params=_params("arbitrary"))(dy3, y, pre)


def _glu_bwd_in(dpre, w, t1, y, name):
    T, W = y.shape
    tm = _pick(T, 512)

    def body(dpre_ref, w_ref, t1_ref, y_ref, dy_ref):
        dy_ref[...] = (_dot(dpre_ref[...], w_ref[...], "nt") + t1_ref[...]) * _gelu_grad(y_ref[...])

    row = pl.BlockSpec((tm, W), lambda i: (i, 0))
    return pl.pallas_call(
        body, name=name, grid=(T // tm,), in_specs=[row, pl.BlockSpec((W, W), lambda i: (0, 0)), row, row],
        out_specs=row, out_shape=jax.ShapeDtypeStruct((T, W), F32),
        compiler_params=_params("parallel"))(dpre, w, t1, y)


def _merge_fwd(ya, y3, wa, ws, z, bias, name, comm=None):
    T, W = ya.shape
    D = wa.shape[1]
    tm, tn = _pick(T, 512), _pick(D, 512)

    def body(ya_ref, y3_ref, wa_ref, ws_ref, za_ref, zs_ref, ba_ref, bs_ref, a_ref, b_ref, m_ref):
        a = _dot(ya_ref[...], wa_ref[...])
        b = _dot(y3_ref[...], ws_ref[...])
        a_ref[...] = a
        b_ref[...] = b
        m_ref[...] = (_sigmoid(za_ref[...] + ba_ref[...]) * a + _sigmoid(zs_ref[...] + bs_ref[...]) * b).astype(BF16)

    act = pl.BlockSpec((tm, W), lambda i, j: (i, 0))
    wgt = pl.BlockSpec((W, tn), lambda i, j: (0, j))
    za = pl.BlockSpec((tm, tn), lambda i, j: (i, OFF_G // tn + j))
    zs = pl.BlockSpec((tm, tn), lambda i, j: (i, (OFF_G + D) // tn + j))
    ba = pl.BlockSpec((1, tn), lambda i, j: (0, j))
    bs = pl.BlockSpec((1, tn), lambda i, j: (0, D // tn + j))
    out = pl.BlockSpec((tm, tn), lambda i, j: (i, j))
    outs, extra = _call(
        body, name, (T // tm, D // tn), [act, act, wgt, wgt, za, zs, ba, bs], [out, out, out],
        [jax.ShapeDtypeStruct((T, D), F32), jax.ShapeDtypeStruct((T, D), F32), jax.ShapeDtypeStruct((T, D), BF16)],
        [], ("parallel", "parallel"), (ya, y3, wa, ws, z, z, bias, bias), comm)
    return outs if comm is None else (outs, extra)


def _merge_bwd(dm, a, b, z, bias, name):
    T, D = dm.shape
    tm, tn = _pick(T, 512), _pick(D, 512)

    def body(dm_ref, a_ref, b_ref, za_ref, zs_ref, ba_ref, bs_ref, da_ref, db_ref, dza_ref, dzs_ref, dba_ref, dbs_ref):
        @pl.when(pl.program_id(1) == 0)
        def _():
            dba_ref[...] = jnp.zeros_like(dba_ref)
            dbs_ref[...] = jnp.zeros_like(dbs_ref)

        dm = dm_ref[...]
        sa = _sigmoid(za_ref[...] + ba_ref[...])
        ss = _sigmoid(zs_ref[...] + bs_ref[...])
        da_ref[...] = (dm * sa).astype(BF16)
        db_ref[...] = (dm * ss).astype(BF16)
        dza = dm * a_ref[...] * sa * (1.0 - sa)
        dzs = dm * b_ref[...] * ss * (1.0 - ss)
        dza_ref[...] = dza.astype(BF16)
        dzs_ref[...] = dzs.astype(BF16)
        dba_ref[...] += jnp.sum(dza, axis=0, keepdims=True)
        dbs_ref[...] += jnp.sum(dzs, axis=0, keepdims=True)

    blk = pl.BlockSpec((tm, tn), lambda j, i: (i, j))
    za = pl.BlockSpec((tm, tn), lambda j, i: (i, OFF_G // tn + j))
    zs = pl.BlockSpec((tm, tn), lambda j, i: (i, (OFF_G + D) // tn + j))
    ba = pl.BlockSpec((1, tn), lambda j, i: (0, j))
    bs = pl.BlockSpec((1, tn), lambda j, i: (0, D // tn + j))
    big = jax.ShapeDtypeStruct((T, D), BF16)
    vec = jax.ShapeDtypeStruct((1, D), F32)
    return pl.pallas_call(
        body, name=name, grid=(D // tn, T // tm), in_specs=[blk, blk, blk, za, zs, ba, bs],
        out_specs=[blk, blk, blk, blk, ba, ba], out_shape=[big, big, big, big, vec, vec],
        compiler_params=_params("parallel", "arbitrary"))(dm, a, b, z, z, bias, bias)


_HALF = N_DEV // 2


def _wgu_block(d):
    return d // 2, d % 2


def _ffn_fwd(h, wgu, name, comm=None):
    T, D = h.shape
    n = wgu.shape[3]
    F = _HALF * n
    tm = _pick(T, 512)

    def body(h_ref, wg_ref, wu_ref, g_ref, u_ref, act_ref):
        hv = h_ref[...]
        g = _dot(hv, wg_ref[...])
        u = _dot(hv, wu_ref[...])
        g_ref[...] = g
        u_ref[...] = u
        act_ref[...] = (g * _sigmoid(g) * u).astype(BF16)

    wg = pl.BlockSpec((None, None, D, n), lambda j, i: (*_wgu_block(j), 0, 0))
    wu = pl.BlockSpec((None, None, D, n), lambda j, i: (*_wgu_block(j + _HALF), 0, 0))
    out = pl.BlockSpec((tm, n), lambda j, i: (i, j))
    outs, extra = _call(
        body, name, (_HALF, T // tm), [pl.BlockSpec((tm, D), lambda j, i: (i, 0)), wg, wu], [out, out, out],
        [jax.ShapeDtypeStruct((T, F), F32), jax.ShapeDtypeStruct((T, F), F32), jax.ShapeDtypeStruct((T, F), BF16)],
        [], ("parallel", "parallel"), (h, wgu, wgu), comm)
    return outs if comm is None else (outs, extra)


def _ffn_bwd_in(dg, du, wgu, name, comm=None):
    T, F = dg.shape
    D, n = wgu.shape[2], wgu.shape[3]
    tm, tn = _pick(T, 1024), _pick(D, 1024)

    def body(dg_ref, du_ref, w_ref, o_ref, acc_ref):
        k = pl.program_id(2)

        @pl.when(k == 0)
        def _():
            acc_ref[...] = jnp.zeros_like(acc_ref)

        @pl.when(k < _HALF)
        def _():
            acc_ref[...] += _dot(dg_ref[...], w_ref[...], "nt")

        @pl.when(k >= _HALF)
        def _():
            acc_ref[...] += _dot(du_ref[...], w_ref[...], "nt")

        @pl.when(k == N_DEV - 1)
        def _():
            o_ref[...] = acc_ref[...]

    dg_spec = pl.BlockSpec((tm, n), lambda i, j, k: (i, jnp.minimum(k, _HALF - 1)))
    du_spec = pl.BlockSpec((tm, n), lambda i, j, k: (i, jnp.maximum(k - _HALF, 0)))
    w_spec = pl.BlockSpec((None, None, tn, n), lambda i, j, k: (*_wgu_block(k), j, 0))
    o_spec = pl.BlockSpec((tm, tn), lambda i, j, k: (i, j))
    (out,), extra = _call(
        body, name, (T // tm, D // tn, N_DEV), [dg_spec, du_spec, w_spec], [o_spec],
        [jax.ShapeDtypeStruct((T, D), F32)], [pltpu.VMEM((tm, tn), F32)], ("parallel", "parallel", "arbitrary"),
        (dg, du, wgu), comm)
    return out if comm is None else (out, extra)


def _ffn_bwd_w(h, dg, du, name, comm=None):
    T, D = h.shape
    n = dg.shape[1] // _HALF
    tm, tk = _pick(D, 1024), _pick(T, 2048)
    nk = T // tk

    def body(h_ref, dg_ref, du_ref, o_ref, acc_ref):
        j, k = pl.program_id(0), pl.program_id(2)

        @pl.when(k == 0)
        def _():
            acc_ref[...] = jnp.zeros_like(acc_ref)

        @pl.when(j < _HALF)
        def _():
            acc_ref[...] += _dot(h_ref[...], dg_ref[...], "tn")

        @pl.when(j >= _HALF)
        def _():
            acc_ref[...] += _dot(h_ref[...], du_ref[...], "tn")

        @pl.when(k == nk - 1)
        def _():
            o_ref[...] = acc_ref[...].astype(BF16)

    h_spec = pl.BlockSpec((tk, tm), lambda j, i, k: (k, i))
    dg_spec = pl.BlockSpec((tk, n), lambda j, i, k: (jnp.where(j < _HALF, k, nk - 1), jnp.minimum(j, _HALF - 1)))
    du_spec = pl.BlockSpec((tk, n), lambda j, i, k: (jnp.where(j >= _HALF, k, 0), jnp.maximum(j - _HALF, 0)))
    o_spec = pl.BlockSpec((None, None, tm, n), lambda j, i, k: (*_wgu_block(j), i, 0))
    (out,), extra = _call(
        body, name, (N_DEV, D // tm, nk), [h_spec, dg_spec, du_spec], [o_spec],
        [jax.ShapeDtypeStruct((_HALF, 2, D, n), BF16)], [pltpu.VMEM((tm, n), F32)],
        ("parallel", "parallel", "arbitrary"), (h, dg, du), comm)
    return out if comm is None else (out, extra)


def _ffn_bwd_act(dx, wo, g, u, name, comm=None):
    T, D = dx.shape
    F = wo.shape[0]
    tm, tn = _pick(T, 1024), _pick(F, 512)

    def body(dx_ref, wo_ref, g_ref, u_ref, dg_ref, du_ref):
        dact = _dot(dx_ref[...].astype(BF16), wo_ref[...], "nt")
        gv = g_ref[...]
        sg = _sigmoid(gv)
        dg_ref[...] = (dact * u_ref[...] * sg * (1.0 + gv * (1.0 - sg))).astype(BF16)
        du_ref[...] = (dact * gv * sg).astype(BF16)

    out = pl.BlockSpec((tm, tn), lambda i, j: (i, j))
    big = jax.ShapeDtypeStruct((T, F), BF16)
    outs, extra = _call(
        body, name, (T // tm, F // tn),
        [pl.BlockSpec((tm, D), lambda i, j: (i, 0)), pl.BlockSpec((tn, D), lambda i, j: (j, 0)), out, out],
        [out, out], [big, big], [], ("parallel", "parallel"), (dx, wo, g, u), comm)
    return outs if comm is None else (outs, extra)


def _place():
    x, y, c = lax.axis_index("x"), lax.axis_index("y"), lax.axis_index("c")
    other_chips = [(1 - x, y), (x, 1 - y), (1 - x, 1 - y)]
    return x, y, c, 2 * x + y, other_chips


_ANY = pl.BlockSpec(memory_space=pl.ANY)
_N_COPIES = 7


def _all_gather(shards, name):
    n = len(shards)

    def body(*refs):
        ins, outs = refs[:n], refs[n:2 * n]
        send_sems, recv_sems, local_sems = refs[2 * n:]
        x, y, c, chip, other_chips = _place()
        sibling = (x, y, 1 - c)

        def remote(src, dst, a, j, dev):
            return pltpu.make_async_remote_copy(src_ref=src, dst_ref=dst, send_sem=send_sems.at[a, j],
                                                recv_sem=recv_sems.at[a, j], device_id=dev, device_id_type=MESH)

        sends, local = [], []
        for a in range(n):
            mine = outs[a].at[chip, c]
            local.append(pltpu.make_async_copy(ins[a], mine, local_sems.at[a]))
            local[a].start()
            for j, (ox, oy) in enumerate(other_chips):
                sends.append(remote(ins[a], mine, a, 1 + j, (ox, oy, c)))
                sends[-1].start()
            sends.append(remote(ins[a], mine, a, 0, sibling))
            sends[-1].start()
        for a in range(n):
            for j, (ox, oy) in enumerate(other_chips):
                slot = outs[a].at[2 * ox + oy, c]
                remote(ins[a], slot, a, 1 + j, (ox, oy, c)).wait_recv()
                sends.append(remote(slot, slot, a, 4 + j, sibling))
                sends[-1].start()
        for a in range(n):
            remote(ins[a], outs[a].at[chip, 1 - c], a, 0, sibling).wait_recv()
            for j, (ox, oy) in enumerate(other_chips):
                remote(ins[a], outs[a].at[2 * ox + oy, 1 - c], a, 4 + j, sibling).wait_recv()
        for cp in sends:
            cp.wait_send()
        for a in range(n):
            local[a].wait()

    return pl.pallas_call(
        body, name=name, in_specs=[_ANY] * n, out_specs=[_ANY] * n,
        out_shape=[jax.ShapeDtypeStruct((4, 2) + s.shape, s.dtype) for s in shards],
        scratch_shapes=[pltpu.SemaphoreType.DMA((n, _N_COPIES)), pltpu.SemaphoreType.DMA((n, _N_COPIES)),
                        pltpu.SemaphoreType.DMA((n,))])(*shards)


def _pair_exchange(parts, name):
    n = len(parts)

    def body(*refs):
        ins, theirs = refs[:n], refs[n:2 * n]
        send_sems, recv_sems = refs[2 * n:]
        x, y, c, _, _ = _place()
        sends = []
        for a in range(n):
            for k in range(4):
                sends.append(pltpu.make_async_remote_copy(
                    src_ref=ins[a].at[k, 1 - c], dst_ref=theirs[a].at[k], send_sem=send_sems.at[a, k],
                    recv_sem=recv_sems.at[a, k], device_id=(x, y, 1 - c), device_id_type=MESH))
                sends[-1].start()
        for cp in sends:
            cp.wait_recv()
            cp.wait_send()

    return pl.pallas_call(
        body, name=name, in_specs=[_ANY] * n, out_specs=[_ANY] * n,
        out_shape=[jax.ShapeDtypeStruct((4,) + p.shape[2:], p.dtype) for p in parts],
        scratch_shapes=[pltpu.SemaphoreType.DMA((n, 4)), pltpu.SemaphoreType.DMA((n, 4))])(*parts)


def _chip_exchange(parts, name):
    n = len(parts)

    def body(*refs):
        ins, got = refs[:n], refs[n:2 * n]
        send_sems, recv_sems = refs[2 * n:]
        _, _, c, _, other_chips = _place()
        sends = []
        for a in range(n):
            for j, (ox, oy) in enumerate(other_chips):
                sends.append(pltpu.make_async_remote_copy(
                    src_ref=ins[a].at[2 * ox + oy], dst_ref=got[a].at[j], send_sem=send_sems.at[a, j],
                    recv_sem=recv_sems.at[a, j], device_id=(ox, oy, c), device_id_type=MESH))
                sends[-1].start()
        for cp in sends:
            cp.wait_recv()
            cp.wait_send()

    return pl.pallas_call(
        body, name=name, in_specs=[_ANY] * n, out_specs=[_ANY] * n,
        out_shape=[jax.ShapeDtypeStruct((3,) + p.shape[1:], p.dtype) for p in parts],
        scratch_shapes=[pltpu.SemaphoreType.DMA((n, 3)), pltpu.SemaphoreType.DMA((n, 3))])(*parts)


def _remote(src, dst, send_sems, recv_sems, a, j, dev):
    return pltpu.make_async_remote_copy(src_ref=src, dst_ref=dst, send_sem=send_sems.at[a, j],
                                        recv_sem=recv_sems.at[a, j], device_id=dev, device_id_type=MESH)


def _gather_send(shards):
    n = len(shards)

    def copies(cin, cout, sems, arriving):
        send_sems, recv_sems, local_sems = sems
        x, y, c, chip, other_chips = _place()
        sibling = (x, y, 1 - c)
        peers = [(0, sibling, (chip, 1 - c))] + [(1 + j, (ox, oy, c), (2 * ox + oy, c))
                                                 for j, (ox, oy) in enumerate(other_chips)]
        sends, recvs, local = [], [], []
        for a in range(n):
            mine = cout[a].at[chip, c]
            local.append(pltpu.make_async_copy(cin[a], mine, local_sems.at[a]))
            for j, dev, slot in peers:
                sends.append(_remote(cin[a], mine, send_sems, recv_sems, a, j, dev))
                if arriving:
                    recvs.append(_remote(cin[a], cout[a].at[slot], send_sems, recv_sems, a, j, dev))
        return sends, recvs, local

    return _Comm(shards, [jax.ShapeDtypeStruct((4, 2) + s.shape, s.dtype) for s in shards],
                 [pltpu.SemaphoreType.DMA((n, 4)), pltpu.SemaphoreType.DMA((n, 4)), pltpu.SemaphoreType.DMA((n,))],
                 copies)


def _gather_pass(gathered):
    n = len(gathered)

    def copies(cin, cout, sems, arriving):
        send_sems, recv_sems = sems
        x, y, c, _, other_chips = _place()
        sibling = (x, y, 1 - c)
        sends, recvs = [], []
        for a in range(n):
            for j, (ox, oy) in enumerate(other_chips):
                k = 2 * ox + oy
                sends.append(_remote(cout[a].at[k, c], cout[a].at[k, c], send_sems, recv_sems, a, j, sibling))
                if arriving:
                    recvs.append(_remote(cout[a].at[k, c], cout[a].at[k, 1 - c], send_sems, recv_sems, a, j, sibling))
        return sends, recvs, []

    return _Comm(gathered, [jax.ShapeDtypeStruct(g.shape, g.dtype) for g in gathered],
                 [pltpu.SemaphoreType.DMA((n, 3)), pltpu.SemaphoreType.DMA((n, 3))], copies,
                 aliases={i: i for i in range(n)})


def _scatter_send(sums):
    n = len(sums)

    def copies(cin, cout, sems, arriving):
        send_sems, recv_sems = sems
        _, _, c, _, other_chips = _place()
        sends = [_remote(cin[a].at[2 * ox + oy], cout[a].at[j], send_sems, recv_sems, a, j, (ox, oy, c))
                 for a in range(n) for j, (ox, oy) in enumerate(other_chips)]
        return sends, sends, []

    return _Comm(sums, [jax.ShapeDtypeStruct((3,) + s.shape[1:], s.dtype) for s in sums],
                 [pltpu.SemaphoreType.DMA((n, 3)), pltpu.SemaphoreType.DMA((n, 3))], copies)


def _join(comms):
    if len(comms) == 1:
        return comms[0]
    args, outs, sems, aliases, spans = [], [], [], {}, []
    for cm in comms:
        spans.append((len(args), len(outs), len(sems)))
        aliases.update({len(args) + i: len(outs) + o for i, o in cm.aliases.items()})
        args, outs, sems = args + cm.args, outs + cm.out_shapes, sems + cm.sems

    def copies(cin, cout, sem_refs, arriving):
        sends, recvs, local = [], [], []
        for cm, (a0, o0, s0) in zip(comms, spans):
            part = cm.copies(cin[a0:a0 + len(cm.args)], cout[o0:o0 + len(cm.out_shapes)],
                             sem_refs[s0:s0 + len(cm.sems)], arriving)
            sends, recvs, local = sends + part[0], recvs + part[1], local + part[2]
        return sends, recvs, local

    return _Comm(args, outs, sems, copies, aliases)


class _Schedule:
    def __init__(self):
        self.rides = {}

    def ride(self, host, make, names, operands):
        results = {}
        self.rides.setdefault(host, []).append((make, names, operands, results))
        return results

    def carry(self, host, fn, *args, **kwargs):
        rides = self.rides.get(host)
        if not rides:
            return fn(*args, host, **kwargs)
        comm = _join([make([operands[n] for n in names]) for make, names, operands, _ in rides])
        out, extra = fn(*args, host, comm=comm, **kwargs)
        for _, names, _, results in rides:
            results.update(zip(names, extra[:len(names)]))
            extra = extra[len(names):]
        return out


def _add_pair(core, parts, theirs, name):
    k, _, R, C = parts.shape
    tr = _pick(R, 512, 16)

    def body(core_ref, a_ref, b_ref, o_ref):
        o_ref[...] = (a_ref[...].astype(F32) + b_ref[...].astype(F32)).astype(BF16)

    blk = pl.BlockSpec((None, tr, C), lambda s, i, core_ref: (s, i, 0))
    grid_spec = pltpu.PrefetchScalarGridSpec(
        num_scalar_prefetch=1, grid=(k, R // tr),
        in_specs=[pl.BlockSpec((None, None, tr, C), lambda s, i, core_ref: (s, core_ref[0], i, 0)), blk],
        out_specs=blk)
    return pl.pallas_call(
        body, name=name, grid_spec=grid_spec, out_shape=jax.ShapeDtypeStruct(theirs.shape, BF16),
        compiler_params=_params("parallel", "parallel"))(core, parts, theirs)


def _adamw_math(w, g, m, v):
    m = ADAM_B1 * m + (1.0 - ADAM_B1) * g
    v = ADAM_B2 * v + (1.0 - ADAM_B2) * (g * g)
    m_hat = m / (1.0 - ADAM_B1 ** ADAM_STEP)
    v_hat = v / (1.0 - ADAM_B2 ** ADAM_STEP)
    delta = -ADAM_LR * (m_hat / (jnp.sqrt(v_hat) + ADAM_EPS) + ADAM_WD * w)
    return delta, m, v


def _scattered_pieces(sums, got):
    return [("own", sums)] + [("peer%d" % j, got) for j in range(3)]


def _piece_spec(kind, tr, C):
    if kind == "own":
        return pl.BlockSpec((None, tr, C), lambda i, chip_ref: (chip_ref[0], i, 0))
    if kind == "plain":
        return pl.BlockSpec((tr, C), lambda i, chip_ref: (i, 0))
    return pl.BlockSpec((None, tr, C), functools.partial(lambda j, i, chip_ref: (j, i, 0), int(kind[-1])))


def _adamw_shard(chip, w, m, v, layer, pieces, so_far, name):
    L, R, C = w.shape
    tr = _pick(R, 256, 16)
    n_p = len(pieces)

    def body(chip_ref, w_ref, m_ref, v_ref, *rest):
        g = rest[0][...].astype(F32)
        for p in rest[1:n_p]:
            g = g + p[...].astype(F32)
        delta, nm, nv = _adamw_math(w_ref[...], g, m_ref[...], v_ref[...])
        g_ref, d_ref, nm_ref, nv_ref = rest[-4:]
        g_ref[...] = g
        d_ref[...] = delta
        nm_ref[...] = nm
        nv_ref[...] = nv

    state = pl.BlockSpec((None, tr, C), lambda i, chip_ref: (layer, i, 0))
    carried = [] if so_far is None else list(so_far)
    first_carried = 1 + 3 + n_p
    grid_spec = pltpu.PrefetchScalarGridSpec(
        num_scalar_prefetch=1, grid=(R // tr,),
        in_specs=[state] * 3 + [_piece_spec(kind, tr, C) for kind, _ in pieces] + [_ANY] * len(carried),
        out_specs=[state] * 4)
    return pl.pallas_call(
        body, name=name, grid_spec=grid_spec, out_shape=[jax.ShapeDtypeStruct((L, R, C), F32)] * 4,
        input_output_aliases={first_carried + t: t for t in range(len(carried))},
        compiler_params=_params("parallel"))(chip, w, m, v, *[a for _, a in pieces], *carried)


def _adamw_small(w, m, v, gathered, name):
    R = w.shape[0]
    tr = _pick(R, 512, SUBLANES)

    def body(w_ref, m_ref, v_ref, gg_ref, g_ref, d_ref, nm_ref, nv_ref):
        g = gg_ref[0, 0]
        for k in range(4):
            for c in range(2):
                if c or k:
                    g = g + gg_ref[k, c]
        delta, nm, nv = _adamw_math(w_ref[...], g, m_ref[...], v_ref[...])
        g_ref[...] = g
        d_ref[...] = delta
        nm_ref[...] = nm
        nv_ref[...] = nv

    row = pl.BlockSpec((tr, LANES), lambda i: (i, 0))
    out = jax.ShapeDtypeStruct((R, LANES), F32)
    return pl.pallas_call(
        body, name=name, grid=(R // tr,),
        in_specs=[row, row, row, pl.BlockSpec((4, 2, tr, LANES), lambda i: (0, 0, i, 0))], out_specs=[row] * 4,
        out_shape=[out] * 4, compiler_params=_params("parallel"))(w, m, v, gathered)


def _layer_fwd(x, w, s, tag, sched):
    h = _rms_fwd(x, s["norm_mix_g"], f"rms_mix_{tag}")
    z = sched.carry(f"in_proj_{tag}", _mm, h, w["w_in_t"], "nt", F32, tn=512)
    q, k, v = (z[:, a:b].reshape(z.shape[0], -1, HEAD_DIM).transpose(1, 0, 2)
               for a, b in ((0, OFF_K), (OFF_K, OFF_V), (OFF_V, OFF_U)))
    ya = sched.carry(f"attn_fwd_{tag}", _attn_fwd, q, k, v, s["q_norm_g"], s["k_norm_g"], s["attn_sinks"])
    ya = ya.transpose(1, 0, 2).reshape(z.shape[0], ATTN_WIDTH)
    y, s_r, s_i = sched.carry(f"ssm_fwd_{tag}", _ssm_fwd, z, *s["ssm16"], s["ssm_d"])
    pre, y3 = _glu_fwd(y, w["ssm_glu_w"], s["ssm_glu_b"], f"glu_fwd_{tag}")
    a, b, merged = sched.carry(f"merge_fwd_{tag}", _merge_fwd, ya, y3, w["w_attn_branch"], w["w_ssm_branch"], z,
                               s["gate_bias"])
    x1 = _mm(merged, w["w_out"], "nn", F32, f"out_proj_{tag}", residual=x)
    h2 = _rms_fwd(x1, s["norm_ffn_g"], f"rms_ffn_{tag}")
    g, u, act = sched.carry(f"ffn_fwd_{tag}", _ffn_fwd, h2, w["w_ffn_in"])
    x2 = sched.carry(f"ffn_out_{tag}", _mm, act, w["w_ffn_out"], "nn", F32, residual=x1)
    saved = dict(x=x, h=h, z=z, ya=ya, y=y, s_r=s_r, s_i=s_i, pre=pre, y3=y3, a=a, b=b, merged=merged,
                 x1=x1, h2=h2, g=g, u=u, act=act)
    return x2, saved


def _layer_bwd(dx2, sv, w, s, tag, sched, scatter):
    gw, gs = {}, {}
    dg16, du16 = _ffn_bwd_act(dx2, w["w_ffn_out"], sv["g"], sv["u"], f"ffn_bwd_act_{tag}")
    gw["w_ffn_out"] = _mm(sv["act"], dx2, "tn", BF16, f"dw_ffn_out_{tag}", tm=1408)
    dh2 = _ffn_bwd_in(dg16, du16, w["w_ffn_in"], f"dh2_{tag}")
    gw["w_ffn_in"] = _ffn_bwd_w(sv["h2"], dg16, du16, f"dw_ffn_in_{tag}")
    scatter(FFN_WEIGHTS[:1], gw, f"attn_bwd_{tag}")
    dx1, gs["norm_ffn_g"] = _rms_bwd(dh2, sv["x1"], s["norm_ffn_g"], dx2, f"rms_ffn_bwd_{tag}")
    dm = _mm(dx1, w["w_out"], "nt", F32, f"dmerged_{tag}")
    gw["w_out"] = _mm(sv["merged"], dx1, "tn", BF16, f"dw_out_{tag}")
    da16, db16, dza, dzs, dba, dbs = _merge_bwd(dm, sv["a"], sv["b"], sv["z"], s["gate_bias"], f"merge_bwd_{tag}")
    gs["gate_bias"] = jnp.concatenate([dba, dbs], axis=1)
    dya = _mm(da16, w["w_attn_branch"], "nt", F32, f"dya_{tag}")
    gw["w_attn_branch"] = _mm(sv["ya"], da16, "tn", BF16, f"dw_attn_branch_{tag}")
    dy3 = _mm(db16, w["w_ssm_branch"], "nt", F32, f"dy3_{tag}")
    gw["w_ssm_branch"] = _mm(sv["y3"], db16, "tn", BF16, f"dw_ssm_branch_{tag}")
    dpre16, t1, y2_16, gs["ssm_glu_b"] = _glu_bwd_gate(dy3, sv["y"], sv["pre"], f"glu_bwd_gate_{tag}")
    dy = _glu_bwd_in(dpre16, w["ssm_glu_w"], t1, sv["y"], f"glu_bwd_in_{tag}")
    gw["ssm_glu_w"] = _mm(y2_16, dpre16, "tn", BF16, f"dw_glu_{tag}")
    scatter(FFN_WEIGHTS[1:] + MIXER_WEIGHTS[1:], gw, f"ssm_bwd_{tag}")
    du_ssm, *gs["ssm_disc"], gs["ssm_d"] = sched.carry(f"ssm_bwd_{tag}", _ssm_bwd, dy, sv["z"], sv["s_r"], sv["s_i"],
                                                       *s["ssm16"], s["ssm_d"])
    dq, dk, dv, gs["q_norm_g"], gs["k_norm_g"], dsk = sched.carry(
        f"attn_bwd_{tag}", _attn_bwd, sv["z"], dya, s["q_norm_g"], s["k_norm_g"], s["attn_sinks"])
    gs["attn_sinks"] = dsk[:, 0].reshape(1, N_Q_HEADS)
    dz = [dq, jnp.concatenate([dk, dv], axis=1).astype(BF16), du_ssm, dza, dzs]
    gw["w_in"] = _mm_cols_tn(dz, sv["h"], f"dw_in_t_{tag}", tn=2048)
    scatter(MIXER_WEIGHTS[:1], gw, f"dh_{tag}")
    dh = sched.carry(f"dh_{tag}", _mm_cols_nn, dz, w["w_in_t"])
    dx, gs["norm_mix_g"] = _rms_bwd(dh, sv["x"], s["norm_mix_g"], dx1, f"rms_mix_bwd_{tag}")
    return dx, gs


def _shard_to_send(name, shard):
    return (shard.T if name == "w_in" else shard).astype(BF16)


def _assemble(name, gathered):
    if name == "w_ffn_in":
        return gathered
    if name in COL_SHARDED and name != "w_in":
        rows = gathered.shape[2]
        return gathered.transpose(2, 0, 1, 3).reshape(rows, -1)
    return gathered.reshape(-1, gathered.shape[3])


class _Weights:
    def __init__(self):
        self.sources, self.ready = [], {}

    def __getitem__(self, name):
        if name not in self.ready:
            key = "w_in" if name == "w_in_t" else name
            (gathered,) = [src[key] for src in self.sources if key in src]
            self.ready[name] = _assemble(key, gathered)
        return self.ready[name]


def _disassemble(name, grad):
    if name == "w_ffn_in":
        return grad
    if name in COL_SHARDED and name != "w_in":
        rows, cols = grad.shape
        return grad.reshape(rows, 4, 2, cols // N_DEV).transpose(1, 2, 0, 3)
    rows, cols = grad.shape
    return grad.reshape(4, 2, rows // N_DEV, cols)


def _pack(arrays):
    flat = jnp.concatenate([a.reshape(-1) for a in arrays])
    pad = (-flat.shape[0]) % (SUBLANES * LANES)
    return jnp.pad(flat, (0, pad)).reshape(-1, LANES)


def _unpack(packed, like):
    flat, out, off = packed.reshape(-1), [], 0
    for a in like:
        out.append(flat[off:off + a.size].reshape(a.shape))
        off += a.size
    return out


def kernel(x, norm_mix_g, w_in, gate_bias, q_norm_g, k_norm_g, attn_sinks, ssm_lambda_re, ssm_lambda_im, ssm_log_dt, ssm_b_re, ssm_b_im, ssm_c_re, ssm_c_im, ssm_d, ssm_glu_w, ssm_glu_b, w_attn_branch, w_ssm_branch, w_out, norm_ffn_g, w_ffn_in, w_ffn_out, loss_target, m_norm_mix_g, m_w_in, m_gate_bias, m_q_norm_g, m_k_norm_g, m_attn_sinks, m_ssm_lambda_re, m_ssm_lambda_im, m_ssm_log_dt, m_ssm_b_re, m_ssm_b_im, m_ssm_c_re, m_ssm_c_im, m_ssm_d, m_ssm_glu_w, m_ssm_glu_b, m_w_attn_branch, m_w_ssm_branch, m_w_out, m_norm_ffn_g, m_w_ffn_in, m_w_ffn_out, v_norm_mix_g, v_w_in, v_gate_bias, v_q_norm_g, v_k_norm_g, v_attn_sinks, v_ssm_lambda_re, v_ssm_lambda_im, v_ssm_log_dt, v_ssm_b_re, v_ssm_b_im, v_ssm_c_re, v_ssm_c_im, v_ssm_d, v_ssm_glu_w, v_ssm_glu_b, v_w_attn_branch, v_w_ssm_branch, v_w_out, v_norm_ffn_g, v_w_ffn_in, v_w_ffn_out):
    given = dict(locals())
    wts = {n: given[n] for n in WEIGHTS}
    mom = {n: given["m_" + n] for n in WEIGHTS}
    var = {n: given["v_" + n] for n in WEIGHTS}
    depth = w_in.shape[0]
    xs = x[0]
    target = loss_target[0]

    sched = _Schedule()
    shards = [{n: _shard_to_send(n, wts[n][l]) for n in BIG} for l in range(depth)]
    full = [_Weights() for _ in range(depth)]

    def gather(layer, names, send_host, pass_host):
        sent = sched.ride(send_host, _gather_send, names, shards[layer])
        full[layer].sources.append(sched.ride(pass_host, _gather_pass, names, sent))

    first = MIXER_WEIGHTS[:1]
    full[0].sources.append(dict(zip(first, _all_gather([shards[0][n] for n in first], "gather_w_in_0"))))
    gather(0, MIXER_WEIGHTS[1:], "in_proj_0", "attn_fwd_0")
    gather(0, ("w_ffn_in",), "attn_fwd_0", "ssm_fwd_0")
    gather(0, ("w_ffn_out",), "ssm_fwd_0", "merge_fwd_0")
    for l in range(1, depth):
        gather(l, first, f"ffn_fwd_{l - 1}", f"ffn_out_{l - 1}")
        gather(l, MIXER_WEIGHTS[1:], f"ffn_out_{l - 1}", f"in_proj_{l}")
        gather(l, ("w_ffn_out",), f"in_proj_{l}", f"attn_fwd_{l}")
        gather(l, ("w_ffn_in",), f"attn_fwd_{l}", f"ssm_fwd_{l}")

    small, disc_vjp = [], []
    for l in range(depth):
        s = {n: wts[n][l].reshape(1, -1) for n in ("norm_mix_g", "gate_bias", "q_norm_g", "k_norm_g", "attn_sinks",
                                                   "ssm_d", "ssm_glu_b", "norm_ffn_g")}
        disc, vjp = jax.vjp(_ssm_discretize, *[wts[n][l] for n in ("ssm_lambda_re", "ssm_lambda_im", "ssm_log_dt",
                                                                  "ssm_b_re", "ssm_b_im", "ssm_c_re", "ssm_c_im")])
        s["ssm16"] = (disc[0], disc[1]) + tuple(d.astype(BF16) for d in disc[2:])
        small.append(s)
        disc_vjp.append(vjp)

    act, saved = xs, []
    for l in range(depth):
        act, sv = _layer_fwd(act, full[l], small[l], str(l), sched)
        saved.append(sv)
    dact, loss_local = _loss_grad(act, target, "loss_head")

    out = {"grad": {}, "delta": {}, "new_m": {}, "new_v": {}}
    results = {n: None for n in BIG}
    core = lax.axis_index("c").astype(jnp.int32).reshape(1)
    chip = (2 * lax.axis_index("x") + lax.axis_index("y")).astype(jnp.int32).reshape(1)
    state = {n: [a.transpose(0, 2, 1) if n == "w_in" else a for a in (wts[n], mom[n], var[n])] for n in BIG}
    small_grads = [None] * depth
    for l in reversed(range(depth)):
        scattered = []

        def scatter(names, grads, host):
            parts = [_disassemble(n, grads[n]) for n in names]
            theirs = _pair_exchange(parts, f"grad_pair_exchange_{host}")
            sums = {n: _add_pair(core, p, t, f"grad_pair_sum_{n}_{host}") for n, p, t in zip(names, parts, theirs)}
            scattered.append((sums, sched.ride(host, _scatter_send, names, sums)))

        dact, gs = _layer_bwd(dact, saved[l], full[l], small[l], str(l), sched, scatter)
        (gs["ssm_lambda_re"], gs["ssm_lambda_im"], gs["ssm_log_dt"], gs["ssm_b_re"], gs["ssm_b_im"], gs["ssm_c_re"],
         gs["ssm_c_im"]) = disc_vjp[l](tuple(gs.pop("ssm_disc")))
        small_grads[l] = gs
        for sums, got in scattered:
            for n in got:
                results[n] = _adamw_shard(chip, state[n][0], state[n][1], state[n][2], l,
                                          _scattered_pieces(sums[n], got[n]), results[n], f"adamw_{n}_{l}")
    loss = lax.psum(loss_local, ("x", "y", "c"))
    for n in BIG:
        for kind, res in zip(("grad", "delta", "new_m", "new_v"), results[n]):
            out[kind][n] = res.transpose(0, 2, 1) if n == "w_in" else res

    like = [wts[n] for n in SMALL]
    g_small = _pack([jnp.stack([small_grads[l][n].reshape(wts[n].shape[1:]) for l in range(depth)]) for n in SMALL])
    (gathered_small,) = _all_gather([g_small], "gather_small_grads")
    res = _adamw_small(_pack(like), _pack([mom[n] for n in SMALL]), _pack([var[n] for n in SMALL]), gathered_small,
                       "adamw_small")
    for kind, packed in zip(("grad", "delta", "new_m", "new_v"), res):
        for n, a in zip(SMALL, _unpack(packed, like)):
            out[kind][n] = a

    grad_x = dact.reshape(x.shape)
    return (loss, grad_x, *[out["grad"][n] for n in WEIGHTS], *[out["delta"][n] for n in WEIGHTS],
            *[out["new_m"][n] for n in WEIGHTS], *[out["new_v"][n] for n in WEIGHTS])
```

```python
import functools
import math

import jax
import jax.numpy as jnp
from jax import lax
from jax.experimental import pallas as pl
from jax.experimental.pallas import tpu as pltpu

F32, BF16 = jnp.float32, jnp.bfloat16
MESH = pl.DeviceIdType.MESH

D_MODEL = 2048
HEAD_DIM = 64
N_Q_HEADS = 16
N_KV_HEADS = 4
GQA_GROUP = N_Q_HEADS // N_KV_HEADS
ATTN_WIDTH = N_Q_HEADS * HEAD_DIM
KV_WIDTH = N_KV_HEADS * HEAD_DIM
WINDOW = 128
BLOCK = 128
SSM_WIDTH = D_MODEL // 2
SSM_GROUP_CH = 16
SSM_GROUPS = SSM_WIDTH // SSM_GROUP_CH
SSM_STATE = 64
D_FF = 5632
OFF_K = ATTN_WIDTH
OFF_V = OFF_K + KV_WIDTH
OFF_U = OFF_V + KV_WIDTH
OFF_G = OFF_U + SSM_WIDTH
IN_WIDTH = OFF_G + 2 * D_MODEL
RMS_EPS = 1e-6
ATTN_SCALE = HEAD_DIM ** -0.5
NEG_BIG = -1e30

SSM_NGB = 4
SSM_GB_CH = SSM_WIDTH // SSM_NGB
SSM_GB_ST = SSM_GROUPS * SSM_STATE // SSM_NGB
SUBLANES = 8
LANES = 128
SSM_TT = 512

ADAM_LR = 0.001
ADAM_B1 = 0.9
ADAM_B2 = 0.999
ADAM_EPS = 1e-08
ADAM_WD = 0.01
ADAM_STEP = 10

N_DEV = 8
VMEM_LIMIT_BYTES = 52 * 1024 * 1024

BIG = ("w_in", "ssm_glu_w", "w_attn_branch", "w_ssm_branch", "w_out", "w_ffn_in", "w_ffn_out")
COL_SHARDED = ("w_in", "w_attn_branch", "w_ssm_branch", "w_ffn_in")
FFN_WEIGHTS = ("w_ffn_in", "w_ffn_out")
MIXER_WEIGHTS = ("w_in", "ssm_glu_w", "w_attn_branch", "w_ssm_branch", "w_out")
SMALL = ("norm_mix_g", "gate_bias", "q_norm_g", "k_norm_g", "attn_sinks", "ssm_lambda_re", "ssm_lambda_im",
         "ssm_log_dt", "ssm_b_re", "ssm_b_im", "ssm_c_re", "ssm_c_im", "ssm_d", "ssm_glu_b", "norm_ffn_g")
WEIGHTS = ("norm_mix_g", "w_in", "gate_bias", "q_norm_g", "k_norm_g", "attn_sinks", "ssm_lambda_re", "ssm_lambda_im",
           "ssm_log_dt", "ssm_b_re", "ssm_b_im", "ssm_c_re", "ssm_c_im", "ssm_d", "ssm_glu_w", "ssm_glu_b",
           "w_attn_branch", "w_ssm_branch", "w_out", "norm_ffn_g", "w_ffn_in", "w_ffn_out")


def _pick(n, target, mult=LANES):
    best = None
    for t in range(mult, min(n, target) + 1, mult):
        if n % t == 0:
            best = t
    return n if best is None else best


def _params(*sem):
    return pltpu.CompilerParams(dimension_semantics=sem, vmem_limit_bytes=VMEM_LIMIT_BYTES)


def _sigmoid(v):
    return 1.0 / (1.0 + jnp.exp(-v))


_GELU_C = math.sqrt(2.0 / math.pi)


def _gelu(v):
    return 0.5 * v * (1.0 + jnp.tanh(_GELU_C * (v + 0.044715 * v * v * v)))


def _gelu_grad(v):
    t = jnp.tanh(_GELU_C * (v + 0.044715 * v * v * v))
    return 0.5 * (1.0 + t) + 0.5 * v * (1.0 - t * t) * _GELU_C * (1.0 + 3.0 * 0.044715 * v * v)


_DN = {"nn": (((1,), (0,)), ((), ())), "nt": (((1,), (1,)), ((), ())), "tn": (((0,), (0,)), ((), ()))}


def _dot(a, b, dims="nn"):
    return lax.dot_general(a, b, _DN[dims], preferred_element_type=F32)


class _Comm:
    def __init__(self, args, out_shapes, sems, copies, aliases=None):
        self.args, self.out_shapes, self.sems = list(args), list(out_shapes), list(sems)
        self.copies, self.aliases = copies, dict(aliases or {})


def _call(body, name, grid, in_specs, out_specs, out_shape, scratch_shapes, semantics, args, comm=None):
    in_specs, out_specs, out_shape = list(in_specs), list(out_specs), list(out_shape)
    scratch_shapes = list(scratch_shapes)
    if comm is None:
        res = pl.pallas_call(body, name=name, grid=grid, in_specs=in_specs, out_specs=out_specs, out_shape=out_shape,
                             scratch_shapes=scratch_shapes, compiler_params=_params(*semantics))(*args)
        return list(res), []
    n_in, n_out, n_scr = len(in_specs), len(out_specs), len(scratch_shapes)
    n_cin, n_cout = len(comm.args), len(comm.out_shapes)

    def carrying(*refs):
        ins, cin = refs[:n_in], refs[n_in:n_in + n_cin]
        o0 = n_in + n_cin
        outs, cout = refs[o0:o0 + n_out], refs[o0 + n_out:o0 + n_out + n_cout]
        s0 = o0 + n_out + n_cout
        scr, sems = refs[s0:s0 + n_scr], refs[s0 + n_scr:]
        first = functools.reduce(jnp.logical_and, [pl.program_id(d) == 0 for d in range(len(grid))])
        last = functools.reduce(jnp.logical_and, [pl.program_id(d) == grid[d] - 1 for d in range(len(grid))])

        @pl.when(first)
        def _():
            sends, _, local = comm.copies(cin, cout, sems, False)
            for cp in local + sends:
                cp.start()

        body(*ins, *outs, *scr)

        @pl.when(last)
        def _():
            sends, recvs, local = comm.copies(cin, cout, sems, True)
            for cp in recvs:
                cp.wait_recv()
            for cp in sends:
                cp.wait_send()
            for cp in local:
                cp.wait()

    res = pl.pallas_call(
        carrying, name=name, grid=grid, in_specs=in_specs + [_ANY] * n_cin, out_specs=out_specs + [_ANY] * n_cout,
        out_shape=out_shape + comm.out_shapes, scratch_shapes=scratch_shapes + comm.sems,
        input_output_aliases={n_in + i: n_out + o for i, o in comm.aliases.items()},
        compiler_params=_params(*["arbitrary"] * len(grid)))(*args, *comm.args)
    return list(res[:n_out]), list(res[n_out:])


def _mm(a, b, dims, out_dtype, name, residual=None, tm=1024, tn=1024, tk=2048, comm=None):
    if dims == "tn":
        K, M = a.shape
    else:
        M, K = a.shape
    N = b.shape[0] if dims == "nt" else b.shape[1]
    tm, tn, tk = _pick(M, tm), _pick(N, tn), _pick(K, tk)
    nk = K // tk
    has_res = residual is not None

    def finish(out, refs):
        if has_res:
            out = out + refs[2][...].astype(F32)
        refs[-2][...] = out.astype(out_dtype)

    def body_single(*refs):
        finish(_dot(refs[0][...].astype(BF16), refs[1][...].astype(BF16), dims), refs)

    def body_multi(*refs):
        acc_ref = refs[-1]
        k = pl.program_id(2)

        @pl.when(k == 0)
        def _():
            acc_ref[...] = jnp.zeros_like(acc_ref)

        acc_ref[...] += _dot(refs[0][...].astype(BF16), refs[1][...].astype(BF16), dims)

        @pl.when(k == nk - 1)
        def _():
            finish(acc_ref[...], refs)

    a_spec = (pl.BlockSpec((tk, tm), lambda i, j, k: (k, i)) if dims == "tn"
              else pl.BlockSpec((tm, tk), lambda i, j, k: (i, k)))
    b_spec = (pl.BlockSpec((tn, tk), lambda i, j, k: (j, k)) if dims == "nt"
              else pl.BlockSpec((tk, tn), lambda i, j, k: (k, j)))
    o_spec = pl.BlockSpec((tm, tn), lambda i, j, k: (i, j))
    in_specs = [a_spec, b_spec] + ([o_spec] if has_res else [])
    args = (a, b) + ((residual,) if has_res else ())
    (out,), extra = _call(
        body_single if nk == 1 else body_multi, name, (M // tm, N // tn, nk), in_specs, [o_spec],
        [jax.ShapeDtypeStruct((M, N), out_dtype)], [pltpu.VMEM((tm, tn) if nk > 1 else (SUBLANES, LANES), F32)],
        ("parallel", "parallel", "arbitrary"), args, comm)
    return out if comm is None else (out, extra)


PIECE_BLOCK = 512


def _piece_blocks(pieces):
    counts = [p.shape[1] // PIECE_BLOCK for p in pieces]
    return counts, [sum(counts[:i]) for i in range(len(counts))]


def _mm_cols_nn(pieces, b, name, comm=None, tm=1024, tn=1024):
    M, N = pieces[0].shape[0], b.shape[1]
    tm, tn = _pick(M, tm), _pick(N, tn)
    counts, starts = _piece_blocks(pieces)
    n, nk = len(pieces), sum(counts)

    def body(*refs):
        b_ref, o_ref, acc_ref = refs[n:]
        k = pl.program_id(2)

        @pl.when(k == 0)
        def _():
            acc_ref[...] = jnp.zeros_like(acc_ref)

        for a_ref, c, s in zip(refs[:n], counts, starts):
            @pl.when((k >= s) & (k < s + c))
            def _(a_ref=a_ref):
                acc_ref[...] += _dot(a_ref[...], b_ref[...])

        @pl.when(k == nk - 1)
        def _():
            o_ref[...] = acc_ref[...]

    a_specs = [pl.BlockSpec((tm, PIECE_BLOCK), functools.partial(lambda s, c, i, j, k: (i, jnp.clip(k - s, 0, c - 1)), s, c))
               for c, s in zip(counts, starts)]
    b_spec = pl.BlockSpec((PIECE_BLOCK, tn), lambda i, j, k: (k, j))
    o_spec = pl.BlockSpec((tm, tn), lambda i, j, k: (i, j))
    (out,), extra = _call(body, name, (M // tm, N // tn, nk), a_specs + [b_spec], [o_spec],
                          [jax.ShapeDtypeStruct((M, N), F32)], [pltpu.VMEM((tm, tn), F32)],
                          ("parallel", "parallel", "arbitrary"), (*pieces, b), comm)
    return out if comm is None else (out, extra)


def _mm_cols_tn(pieces, b, name, tn=1024, tk=2048):
    K, N = b.shape
    tn, tk = _pick(N, tn), _pick(K, tk)
    counts, starts = _piece_blocks(pieces)
    n, nk = len(pieces), K // tk

    def body(*refs):
        b_ref, o_ref, acc_ref = refs[n:]
        i, k = pl.program_id(0), pl.program_id(2)

        @pl.when(k == 0)
        def _():
            acc_ref[...] = jnp.zeros_like(acc_ref)

        for a_ref, c, s in zip(refs[:n], counts, starts):
            @pl.when((i >= s) & (i < s + c))
            def _(a_ref=a_ref):
                acc_ref[...] += _dot(a_ref[...], b_ref[...], "tn")

        @pl.when(k == nk - 1)
        def _():
            o_ref[...] = acc_ref[...].astype(BF16)

    def a_index(s, c, i, j, k):
        mine = (i >= s) & (i < s + c)
        return jnp.where(mine, k, 0), jnp.clip(i - s, 0, c - 1)

    a_specs = [pl.BlockSpec((tk, PIECE_BLOCK), functools.partial(a_index, s, c)) for c, s in zip(counts, starts)]
    b_spec = pl.BlockSpec((tk, tn), lambda i, j, k: (k, j))
    o_spec = pl.BlockSpec((PIECE_BLOCK, tn), lambda i, j, k: (i, j))
    (out,), _ = _call(body, name, (sum(counts), N // tn, nk), a_specs + [b_spec], [o_spec],
                      [jax.ShapeDtypeStruct((sum(counts) * PIECE_BLOCK, N), BF16)],
                      [pltpu.VMEM((PIECE_BLOCK, tn), F32)], ("parallel", "parallel", "arbitrary"), (*pieces, b))
    return out


def _rms_fwd(x, g, name):
    T, D = x.shape
    tr = _pick(T, 256, SUBLANES)

    def body(x_ref, g_ref, o_ref):
        xf = x_ref[...]
        r = lax.rsqrt(jnp.mean(xf * xf, axis=-1, keepdims=True) + RMS_EPS)
        o_ref[...] = (xf * r * g_ref[...]).astype(BF16)

    return pl.pallas_call(
        body, name=name, grid=(T // tr,),
        in_specs=[pl.BlockSpec((tr, D), lambda i: (i, 0)), pl.BlockSpec((1, D), lambda i: (0, 0))],
        out_specs=pl.BlockSpec((tr, D), lambda i: (i, 0)), out_shape=jax.ShapeDtypeStruct((T, D), BF16),
        compiler_params=_params("parallel"))(x, g)


def _rms_bwd(dh, x, g, dres, name):
    T, D = x.shape
    tr = _pick(T, 256, SUBLANES)

    def body(dh_ref, x_ref, g_ref, dres_ref, dx_ref, dg_ref):
        @pl.when(pl.program_id(0) == 0)
        def _():
            dg_ref[...] = jnp.zeros_like(dg_ref)

        xf = x_ref[...]
        r = lax.rsqrt(jnp.mean(xf * xf, axis=-1, keepdims=True) + RMS_EPS)
        xhat = xf * r
        dhv = dh_ref[...]
        dxh = dhv * g_ref[...]
        dx_ref[...] = dres_ref[...] + r * (dxh - xhat * jnp.mean(dxh * xhat, axis=-1, keepdims=True))
        dg_ref[...] += jnp.sum(dhv * xhat, axis=0, keepdims=True)

    row = pl.BlockSpec((tr, D), lambda i: (i, 0))
    vec = pl.BlockSpec((1, D), lambda i: (0, 0))
    return pl.pallas_call(
        body, name=name, grid=(T // tr,), in_specs=[row, row, vec, row], out_specs=[row, vec],
        out_shape=[jax.ShapeDtypeStruct((T, D), F32), jax.ShapeDtypeStruct((1, D), F32)],
        compiler_params=_params("arbitrary"))(dh, x, g, dres)


def _loss_grad(y, target, name):
    T, D = y.shape
    tr = _pick(T, 256, SUBLANES)

    def body(y_ref, t_ref, dx_ref, loss_ref):
        @pl.when(pl.program_id(0) == 0)
        def _():
            loss_ref[...] = jnp.zeros_like(loss_ref)

        err = y_ref[...] - t_ref[...]
        dx_ref[...] = err * (1.0 / D)
        loss_ref[...] += jnp.sum(jnp.mean(err * err, axis=-1, keepdims=True), axis=0, keepdims=True) * 0.5

    row = pl.BlockSpec((tr, D), lambda i: (i, 0))
    one = pl.BlockSpec((1, 1), lambda i: (0, 0))
    dx, loss = pl.pallas_call(
        body, name=name, grid=(T // tr,), in_specs=[row, row], out_specs=[row, one],
        out_shape=[jax.ShapeDtypeStruct((T, D), F32), jax.ShapeDtypeStruct((1, 1), F32)],
        compiler_params=_params("arbitrary"))(y, target)
    return dx, loss[0, 0]


_STACK = GQA_GROUP * BLOCK


def _attn_mask(n):
    row = lax.broadcasted_iota(jnp.int32, (_STACK, 2 * BLOCK), 0)
    col = lax.broadcasted_iota(jnp.int32, (_STACK, 2 * BLOCK), 1)
    dist = (row % BLOCK) - col + BLOCK
    valid = (dist >= 0) & (dist < WINDOW) & ((col >= BLOCK) | (n > 0))
    return dist.astype(F32), valid


def _per_row_head(kv, sk_ref):
    heads = [kv * GQA_GROUP + g for g in range(GQA_GROUP)]
    slope = jnp.concatenate([jnp.full((BLOCK, 1), 2.0 ** (-8.0 * (h + 1) / N_Q_HEADS), F32) for h in heads], axis=0)
    sink = jnp.concatenate([jnp.broadcast_to(sk_ref[h:h + 1, 0:1], (BLOCK, 1)) for h in heads], axis=0)
    return slope, sink


def _head_cols(h):
    return slice(h * HEAD_DIM, (h + 1) * HEAD_DIM)


def _stack_heads(x, kv):
    return jnp.concatenate([x[:, _head_cols(kv * GQA_GROUP + g)] for g in range(GQA_GROUP)], axis=0)


def _head_norm(v, gain):
    r = lax.rsqrt(jnp.mean(v * v, axis=-1, keepdims=True) + RMS_EPS)
    vhat = v * r
    return r, vhat, vhat * gain


def _attn_probs(qn16, kn16, slope, dist, valid, sink):
    s = _dot(qn16, kn16, "nt") * ATTN_SCALE - slope * dist
    s = jnp.where(valid, s, NEG_BIG)
    m = jnp.maximum(jnp.max(s, axis=-1, keepdims=True), sink)
    p = jnp.exp(s - m)
    ps = jnp.exp(sink - m)
    den = jnp.sum(p, axis=-1, keepdims=True) + ps
    return p, ps, den


def _attn_specs():
    wide = pl.BlockSpec((BLOCK, ATTN_WIDTH), lambda n: (n, 0))
    cur = [pl.BlockSpec((BLOCK, KV_WIDTH), functools.partial(lambda c, n: (n, c), off // KV_WIDTH))
           for off in (OFF_K, OFF_V)]
    prev = [pl.BlockSpec((BLOCK, KV_WIDTH), functools.partial(lambda c, n: (jnp.maximum(n - 1, 0), c), off // KV_WIDTH))
            for off in (OFF_K, OFF_V)]
    gain = pl.BlockSpec((1, HEAD_DIM), lambda n: (0, 0))
    sink = pl.BlockSpec((N_Q_HEADS, LANES), lambda n: (0, 0))
    return wide, [cur[0], prev[0], cur[1], prev[1]], gain, sink


def _sink_rows(sinks):
    return jnp.broadcast_to(sinks.reshape(N_Q_HEADS, 1), (N_Q_HEADS, LANES))


def _attn_fwd(q, k, v, qg, kg, sinks, name, comm=None):
    T = q.shape[1]

    def body(q_ref, kc_ref, kp_ref, vc_ref, vp_ref, qg_ref, kg_ref, sk_ref, o_ref):
        dist, valid = _attn_mask(pl.program_id(1))
        row_head = pl.program_id(0) * GQA_GROUP + lax.broadcasted_iota(jnp.int32, (_STACK, 1), 0) // BLOCK
        slope = jnp.exp((row_head + 1).astype(F32) * (-8.0 / N_Q_HEADS * math.log(2.0)))
        first_head = pl.program_id(0) * GQA_GROUP
        sink = jnp.concatenate([jnp.broadcast_to(sk_ref[pl.ds(first_head + g, 1), 0:1], (BLOCK, 1))
                                for g in range(GQA_GROUP)], axis=0)
        kk = jnp.concatenate([kp_ref[...], kc_ref[...]], axis=0)
        _, _, kn = _head_norm(kk, kg_ref[...])
        v16 = jnp.concatenate([vp_ref[...], vc_ref[...]], axis=0).astype(BF16)
        _, _, qn = _head_norm(q_ref[...].reshape(_STACK, HEAD_DIM), qg_ref[...])
        p, _, den = _attn_probs(qn.astype(BF16), kn.astype(BF16), slope, dist, valid, sink)
        o_ref[...] = _dot((p / den).astype(BF16), v16).reshape(GQA_GROUP, BLOCK, HEAD_DIM).astype(BF16)

    q_spec = pl.BlockSpec((GQA_GROUP, BLOCK, HEAD_DIM), lambda h, n: (h, n, 0))
    cur = pl.BlockSpec((None, BLOCK, HEAD_DIM), lambda h, n: (h, n, 0))
    prev = pl.BlockSpec((None, BLOCK, HEAD_DIM), lambda h, n: (h, jnp.maximum(n - 1, 0), 0))
    gain = pl.BlockSpec((1, HEAD_DIM), lambda h, n: (0, 0))
    sink = pl.BlockSpec((N_Q_HEADS, LANES), lambda h, n: (0, 0))
    (out,), extra = _call(
        body, name, (N_KV_HEADS, T // BLOCK), [q_spec, cur, prev, cur, prev, gain, gain, sink], [q_spec],
        [jax.ShapeDtypeStruct((N_Q_HEADS, T, HEAD_DIM), BF16)], [], ("parallel", "parallel"),
        (q, k, k, v, v, qg, kg, _sink_rows(sinks)), comm)
    return out if comm is None else (out, extra)


def _attn_bwd(z, do, qg, kg, sinks, name, comm=None):
    T = z.shape[0]

    def body(q_ref, kc_ref, kp_ref, vc_ref, vp_ref, do_ref, qg_ref, kg_ref, sk_ref,
             dq_ref, dk_ref, dv_ref, dqg_ref, dkg_ref, dsk_ref):
        n = pl.program_id(0)

        @pl.when(n == 0)
        def _():
            for ref in (dk_ref, dv_ref, dqg_ref, dkg_ref, dsk_ref):
                ref[...] = jnp.zeros_like(ref)

        dist, valid = _attn_mask(n)
        rows = pl.ds(pl.multiple_of(n * BLOCK, BLOCK), BLOCK)
        before = pl.ds(pl.multiple_of(jnp.maximum(n - 1, 0) * BLOCK, BLOCK), BLOCK)
        q = q_ref[...]
        do = do_ref[...]
        kk = jnp.concatenate([kp_ref[...], kc_ref[...]], axis=0)
        vv = jnp.concatenate([vp_ref[...], vc_ref[...]], axis=0)
        for kv in range(N_KV_HEADS):
            slope, sink = _per_row_head(kv, sk_ref)
            rk, khat, kn = _head_norm(kk[:, _head_cols(kv)], kg_ref[...])
            kn16 = kn.astype(BF16)
            v16 = vv[:, _head_cols(kv)].astype(BF16)
            rq, qhat, qn = _head_norm(_stack_heads(q, kv), qg_ref[...])
            qn16 = qn.astype(BF16)
            p, ps, den = _attn_probs(qn16, kn16, slope, dist, valid, sink)
            pn = p / den
            do16 = _stack_heads(do, kv).astype(BF16)
            dp = _dot(do16, v16, "nt")
            delta = jnp.sum(pn * dp, axis=-1, keepdims=True)
            ds16 = (pn * (dp - delta)).astype(BF16)
            sink_pull = ps / den * delta
            dqn = _dot(ds16, kn16) * ATTN_SCALE
            dkn = _dot(ds16, qn16, "tn") * ATTN_SCALE
            dv = _dot(pn.astype(BF16), do16, "tn")
            dqh = dqn * qg_ref[...]
            dq = rq * (dqh - qhat * jnp.mean(dqh * qhat, axis=-1, keepdims=True))
            for g in range(GQA_GROUP):
                h = kv * GQA_GROUP + g
                dsink = jnp.sum(sink_pull[g * BLOCK:(g + 1) * BLOCK], axis=0, keepdims=True)
                dsk_ref[h:h + 1, :] -= jnp.broadcast_to(dsink, (1, LANES))
                dq_ref[:, _head_cols(h)] = dq[g * BLOCK:(g + 1) * BLOCK].astype(BF16)
            dqg_ref[...] += jnp.sum(dqn * qhat, axis=0, keepdims=True)
            dkg_ref[...] += jnp.sum(dkn * khat, axis=0, keepdims=True)
            dkh = dkn * kg_ref[...]
            dk = rk * (dkh - khat * jnp.mean(dkh * khat, axis=-1, keepdims=True))
            dk_ref[rows, _head_cols(kv)] += dk[BLOCK:]
            dv_ref[rows, _head_cols(kv)] += dv[BLOCK:]
            dk_ref[before, _head_cols(kv)] += dk[:BLOCK]
            dv_ref[before, _head_cols(kv)] += dv[:BLOCK]

    wide, kv_specs, gain, sink = _attn_specs()
    whole = pl.BlockSpec((T, KV_WIDTH), lambda n: (0, 0))
    outs, extra = _call(
        body, name, (T // BLOCK,), [wide] + kv_specs + [wide, gain, gain, sink],
        [wide, whole, whole, gain, gain, sink],
        [jax.ShapeDtypeStruct((T, ATTN_WIDTH), BF16), jax.ShapeDtypeStruct((T, KV_WIDTH), F32),
         jax.ShapeDtypeStruct((T, KV_WIDTH), F32), jax.ShapeDtypeStruct((1, HEAD_DIM), F32),
         jax.ShapeDtypeStruct((1, HEAD_DIM), F32), jax.ShapeDtypeStruct((N_Q_HEADS, LANES), F32)],
        [], ("arbitrary",), (z, z, z, z, z, do, qg, kg, _sink_rows(sinks)), comm)
    return outs if comm is None else (outs, extra)


def _ssm_discretize(lam_re, lam_im, log_dt, b_re, b_im, c_re, c_im):
    dt = jnp.exp(log_dt)[:, None]
    mag = jnp.exp(lam_re * dt)
    ar = mag * jnp.cos(lam_im * dt)
    ai = mag * jnp.sin(lam_im * dt)
    den = lam_re * lam_re + lam_im * lam_im
    fr = ((ar - 1.0) * lam_re + ai * lam_im) / den
    fi = (ai * lam_re - (ar - 1.0) * lam_im) / den
    bbar_r = fr[:, :, None] * b_re - fi[:, :, None] * b_im
    bbar_i = fr[:, :, None] * b_im + fi[:, :, None] * b_re
    gl = SSM_GROUPS // SSM_NGB
    eye = jnp.eye(gl, dtype=F32)

    def tiles(a):
        return a.reshape(SSM_NGB, SUBLANES, LANES)

    def bdiag(bb):
        return jnp.einsum("bgph,gk->bghkp", bb.reshape(SSM_NGB, gl, SSM_STATE, SSM_GROUP_CH), eye).reshape(
            SSM_NGB, SSM_GB_CH, SSM_GB_ST)

    def cdiag(cc):
        return jnp.einsum("bghp,gk->bgpkh", cc.reshape(SSM_NGB, gl, SSM_GROUP_CH, SSM_STATE), eye).reshape(
            SSM_NGB, SSM_GB_ST, SSM_GB_CH)

    return tiles(ar), tiles(ai), bdiag(bbar_r), bdiag(bbar_i), cdiag(c_re), cdiag(c_im)


def _to_time_major(dst, val, tt, first_row=0):
    for j in range(SUBLANES):
        dst[pl.ds(first_row + j, tt, stride=SUBLANES), :] = val[:, j * LANES:(j + 1) * LANES]


def _from_time_major(dst, src, tt):
    for j in range(SUBLANES):
        dst[:, j * LANES:(j + 1) * LANES] = src[pl.ds(j, tt, stride=SUBLANES), :]


def _ssm_fwd(z, ar, ai, bbr, bbi, cbr, cbi, dskip, name, comm=None):
    T = z.shape[0]
    tt = min(SSM_TT, T)
    nt = T // tt

    def body(u_ref, ar_ref, ai_ref, br_ref, bi_ref, cr_ref, ci_ref, d_ref, y_ref, sr_ref, si_ref,
             tmr, tmi, car_r, car_i):
        @pl.when(pl.program_id(1) == 0)
        def _():
            car_r[...] = jnp.zeros_like(car_r)
            car_i[...] = jnp.zeros_like(car_i)

        u = u_ref[...]
        u16 = u.astype(BF16)
        _to_time_major(tmr, _dot(u16, br_ref[...]), tt)
        _to_time_major(tmi, _dot(u16, bi_ref[...]), tt)
        a_r = ar_ref[...]
        a_i = ai_ref[...]

        def step(t, carry):
            s_r, s_i = carry
            rows = pl.ds(pl.multiple_of(t * SUBLANES, SUBLANES), SUBLANES)
            n_r = a_r * s_r - a_i * s_i + tmr[rows, :]
            n_i = a_r * s_i + a_i * s_r + tmi[rows, :]
            tmr[rows, :] = n_r
            tmi[rows, :] = n_i
            return n_r, n_i

        s_r, s_i = lax.fori_loop(0, tt, step, (car_r[...], car_i[...]), unroll=8)
        car_r[...] = s_r
        car_i[...] = s_i
        _from_time_major(sr_ref, tmr, tt)
        _from_time_major(si_ref, tmi, tt)
        y_ref[...] = (_dot(sr_ref[...].astype(BF16), cr_ref[...]) - _dot(si_ref[...].astype(BF16), ci_ref[...])
                      + d_ref[...] * u)

    u_spec = pl.BlockSpec((tt, SSM_GB_CH), lambda b, t: (t, OFF_U // SSM_GB_CH + b))
    a_spec = pl.BlockSpec((None, SUBLANES, LANES), lambda b, t: (b, 0, 0))
    b_spec = pl.BlockSpec((None, SSM_GB_CH, SSM_GB_ST), lambda b, t: (b, 0, 0))
    c_spec = pl.BlockSpec((None, SSM_GB_ST, SSM_GB_CH), lambda b, t: (b, 0, 0))
    d_spec = pl.BlockSpec((1, SSM_GB_CH), lambda b, t: (0, b))
    y_spec = pl.BlockSpec((tt, SSM_GB_CH), lambda b, t: (t, b))
    s_spec = pl.BlockSpec((tt, SSM_GB_ST), lambda b, t: (t, b))
    n_state = SSM_NGB * SSM_GB_ST
    outs, extra = _call(
        body, name, (SSM_NGB, nt), [u_spec, a_spec, a_spec, b_spec, b_spec, c_spec, c_spec, d_spec],
        [y_spec, s_spec, s_spec],
        [jax.ShapeDtypeStruct((T, SSM_WIDTH), F32), jax.ShapeDtypeStruct((T, n_state), F32),
         jax.ShapeDtypeStruct((T, n_state), F32)],
        [pltpu.VMEM((tt * SUBLANES, LANES), F32), pltpu.VMEM((tt * SUBLANES, LANES), F32),
         pltpu.VMEM((SUBLANES, LANES), F32), pltpu.VMEM((SUBLANES, LANES), F32)],
        ("parallel", "arbitrary"), (z, ar, ai, bbr, bbi, cbr, cbi, dskip), comm)
    return outs if comm is None else (outs, extra)


def _ssm_bwd(dy, z, s_r, s_i, ar, ai, bbr, bbi, cbr, cbi, dskip, name, comm=None):
    T = z.shape[0]
    tt = min(SSM_TT, T)
    nt = T // tt
    per8 = tt // SUBLANES

    def body(dy_ref, u_ref, sr_ref, si_ref, srp_ref, sip_ref, ar_ref, ai_ref, br_ref, bi_ref, cr_ref, ci_ref, d_ref,
             du_ref, dar_ref, dai_ref, dbr_ref, dbi_ref, dcr_ref, dci_ref, dd_ref,
             tmr, tmi, smr, smi, natr, nati, car_r, car_i):
        tb = pl.program_id(1)
        first_block = tb == nt - 1

        @pl.when(tb == 0)
        def _():
            for ref in (car_r, car_i, dar_ref, dai_ref, dbr_ref, dbi_ref, dcr_ref, dci_ref, dd_ref):
                ref[...] = jnp.zeros_like(ref)

        dy = dy_ref[...]
        dy16 = dy.astype(BF16)
        u = u_ref[...]
        u16 = u.astype(BF16)
        _to_time_major(tmr, _dot(dy16, cr_ref[...], "nt"), tt)
        _to_time_major(tmi, -_dot(dy16, ci_ref[...], "nt"), tt)
        _to_time_major(smr, sr_ref[...], tt, first_row=SUBLANES)
        _to_time_major(smi, si_ref[...], tt, first_row=SUBLANES)
        keep = jnp.where(first_block, 0.0, 1.0)
        for j in range(SUBLANES):
            smr[j:j + 1, :] = srp_ref[SUBLANES - 1:SUBLANES, j * LANES:(j + 1) * LANES] * keep
            smi[j:j + 1, :] = sip_ref[SUBLANES - 1:SUBLANES, j * LANES:(j + 1) * LANES] * keep
        a_r = ar_ref[...]
        a_i = ai_ref[...]

        def step(i, carry):
            n_r, n_i, da_r, da_i = carry
            rows = pl.ds(pl.multiple_of((tt - 1 - i) * SUBLANES, SUBLANES), SUBLANES)
            g_r = tmr[rows, :] + a_r * n_r + a_i * n_i
            g_i = tmi[rows, :] - a_i * n_r + a_r * n_i
            tmr[rows, :] = g_r
            tmi[rows, :] = g_i
            p_r = smr[rows, :]
            p_i = smi[rows, :]
            return g_r, g_i, da_r + g_r * p_r + g_i * p_i, da_i - g_r * p_i + g_i * p_r

        zero = jnp.zeros((SUBLANES, LANES), F32)
        n_r, n_i, da_r, da_i = lax.fori_loop(0, tt, step, (car_r[...], car_i[...], zero, zero), unroll=8)
        car_r[...] = n_r
        car_i[...] = n_i
        dar_ref[...] += da_r
        dai_ref[...] += da_i
        _from_time_major(natr, tmr, tt)
        _from_time_major(nati, tmi, tt)
        dbu_r16 = natr[...].astype(BF16)
        dbu_i16 = nati[...].astype(BF16)
        du_ref[...] = (_dot(dbu_r16, br_ref[...], "nt") + _dot(dbu_i16, bi_ref[...], "nt")
                       + d_ref[...] * dy).astype(BF16)
        dbr_ref[...] += _dot(u16, dbu_r16, "tn")
        dbi_ref[...] += _dot(u16, dbu_i16, "tn")
        dcr_ref[...] += _dot(sr_ref[...].astype(BF16), dy16, "tn")
        dci_ref[...] -= _dot(si_ref[...].astype(BF16), dy16, "tn")
        dd_ref[...] += jnp.sum(dy * u, axis=0, keepdims=True)

    def rev(t):
        return nt - 1 - t

    dy_spec = pl.BlockSpec((tt, SSM_GB_CH), lambda b, t: (rev(t), b))
    u_spec = pl.BlockSpec((tt, SSM_GB_CH), lambda b, t: (rev(t), OFF_U // SSM_GB_CH + b))
    s_spec = pl.BlockSpec((tt, SSM_GB_ST), lambda b, t: (rev(t), b))
    sp_spec = pl.BlockSpec((SUBLANES, SSM_GB_ST), lambda b, t: (jnp.maximum(rev(t) * per8 - 1, 0), b))
    a_spec = pl.BlockSpec((None, SUBLANES, LANES), lambda b, t: (b, 0, 0))
    b_spec = pl.BlockSpec((None, SSM_GB_CH, SSM_GB_ST), lambda b, t: (b, 0, 0))
    c_spec = pl.BlockSpec((None, SSM_GB_ST, SSM_GB_CH), lambda b, t: (b, 0, 0))
    d_spec = pl.BlockSpec((1, SSM_GB_CH), lambda b, t: (0, b))
    tm_shape = pltpu.VMEM((tt * SUBLANES, LANES), F32)
    sm_shape = pltpu.VMEM(((tt + 1) * SUBLANES, LANES), F32)
    nat_shape = pltpu.VMEM((tt, SSM_GB_ST), F32)
    tile = pltpu.VMEM((SUBLANES, LANES), F32)
    outs, extra = _call(
        body, name, (SSM_NGB, nt),
        [dy_spec, u_spec, s_spec, s_spec, sp_spec, sp_spec, a_spec, a_spec, b_spec, b_spec, c_spec, c_spec, d_spec],
        [dy_spec, a_spec, a_spec, b_spec, b_spec, c_spec, c_spec, d_spec],
        [jax.ShapeDtypeStruct((T, SSM_WIDTH), BF16),
         jax.ShapeDtypeStruct((SSM_NGB, SUBLANES, LANES), F32),
         jax.ShapeDtypeStruct((SSM_NGB, SUBLANES, LANES), F32),
         jax.ShapeDtypeStruct((SSM_NGB, SSM_GB_CH, SSM_GB_ST), F32),
         jax.ShapeDtypeStruct((SSM_NGB, SSM_GB_CH, SSM_GB_ST), F32),
         jax.ShapeDtypeStruct((SSM_NGB, SSM_GB_ST, SSM_GB_CH), F32),
         jax.ShapeDtypeStruct((SSM_NGB, SSM_GB_ST, SSM_GB_CH), F32),
         jax.ShapeDtypeStruct((1, SSM_WIDTH), F32)],
        [tm_shape, tm_shape, sm_shape, sm_shape, nat_shape, nat_shape, tile, tile], ("parallel", "arbitrary"),
        (dy, z, s_r, s_i, s_r, s_i, ar, ai, bbr, bbi, cbr, cbi, dskip), comm)
    return outs if comm is None else (outs, extra)


def _glu_fwd(y, w, b, name):
    T, W = y.shape
    tm = _pick(T, 512)

    def body(y_ref, w_ref, b_ref, pre_ref, y3_ref):
        y2 = _gelu(y_ref[...])
        pre = _dot(y2.astype(BF16), w_ref[...]) + b_ref[...]
        pre_ref[...] = pre
        y3_ref[...] = (y2 * _sigmoid(pre)).astype(BF16)

    row = pl.BlockSpec((tm, W), lambda i: (i, 0))
    return pl.pallas_call(
        body, name=name, grid=(T // tm,),
        in_specs=[row, pl.BlockSpec((W, W), lambda i: (0, 0)), pl.BlockSpec((1, W), lambda i: (0, 0))],
        out_specs=[row, row], out_shape=[jax.ShapeDtypeStruct((T, W), F32), jax.ShapeDtypeStruct((T, W), BF16)],
        compiler_params=_params("parallel"))(y, w, b)


def _glu_bwd_gate(dy3, y, pre, name):
    T, W = y.shape
    tm = _pick(T, 512)

    def body(dy3_ref, y_ref, pre_ref, dpre_ref, t1_ref, y2_ref, db_ref):
        @pl.when(pl.program_id(0) == 0)
        def _():
            db_ref[...] = jnp.zeros_like(db_ref)

        y2 = _gelu(y_ref[...])
        sg = _sigmoid(pre_ref[...])
        dy3 = dy3_ref[...]
        dpre = dy3 * y2 * sg * (1.0 - sg)
        dpre_ref[...] = dpre.astype(BF16)
        t1_ref[...] = dy3 * sg
        y2_ref[...] = y2.astype(BF16)
        db_ref[...] += jnp.sum(dpre, axis=0, keepdims=True)

    row = pl.BlockSpec((tm, W), lambda i: (i, 0))
    vec = pl.BlockSpec((1, W), lambda i: (0, 0))
    return pl.pallas_call(
        body, name=name, grid=(T // tm,), in_specs=[row, row, row], out_specs=[row, row, row, vec],
        out_shape=[jax.ShapeDtypeStruct((T, W), BF16), jax.ShapeDtypeStruct((T, W), F32),
                   jax.ShapeDtypeStruct((T, W), BF16), jax.ShapeDtypeStruct((1, W), F32)],
        compiler_params=_params("arbitrary"))(dy3, y, pre)


def _glu_bwd_in(dpre, w, t1, y, name):
    T, W = y.shape
    tm = _pick(T, 512)

    def body(dpre_ref, w_ref, t1_ref, y_ref, dy_ref):
        dy_ref[...] = (_dot(dpre_ref[...], w_ref[...], "nt") + t1_ref[...]) * _gelu_grad(y_ref[...])

    row = pl.BlockSpec((tm, W), lambda i: (i, 0))
    return pl.pallas_call(
        body, name=name, grid=(T // tm,), in_specs=[row, pl.BlockSpec((W, W), lambda i: (0, 0)), row, row],
        out_specs=row, out_shape=jax.ShapeDtypeStruct((T, W), F32),
        compiler_params=_params("parallel"))(dpre, w, t1, y)


def _merge_fwd(ya, y3, wa, ws, z, bias, name, comm=None):
    T, W = ya.shape
    D = wa.shape[1]
    tm, tn = _pick(T, 1024), _pick(D, 512)

    def body(ya_ref, y3_ref, wa_ref, ws_ref, za_ref, zs_ref, ba_ref, bs_ref, a_ref, b_ref, m_ref):
        a = _dot(ya_ref[...], wa_ref[...])
        b = _dot(y3_ref[...], ws_ref[...])
        a_ref[...] = a
        b_ref[...] = b
        m_ref[...] = (_sigmoid(za_ref[...] + ba_ref[...]) * a + _sigmoid(zs_ref[...] + bs_ref[...]) * b).astype(BF16)

    act = pl.BlockSpec((tm, W), lambda i, j: (i, 0))
    wgt = pl.BlockSpec((W, tn), lambda i, j: (0, j))
    za = pl.BlockSpec((tm, tn), lambda i, j: (i, OFF_G // tn + j))
    zs = pl.BlockSpec((tm, tn), lambda i, j: (i, (OFF_G + D) // tn + j))
    ba = pl.BlockSpec((1, tn), lambda i, j: (0, j))
    bs = pl.BlockSpec((1, tn), lambda i, j: (0, D // tn + j))
    out = pl.BlockSpec((tm, tn), lambda i, j: (i, j))
    outs, extra = _call(
        body, name, (T // tm, D // tn), [act, act, wgt, wgt, za, zs, ba, bs], [out, out, out],
        [jax.ShapeDtypeStruct((T, D), F32), jax.ShapeDtypeStruct((T, D), F32), jax.ShapeDtypeStruct((T, D), BF16)],
        [], ("parallel", "parallel"), (ya, y3, wa, ws, z, z, bias, bias), comm)
    return outs if comm is None else (outs, extra)


def _merge_bwd(dm, a, b, z, bias, name):
    T, D = dm.shape
    tm, tn = _pick(T, 512), _pick(D, 512)

    def body(dm_ref, a_ref, b_ref, za_ref, zs_ref, ba_ref, bs_ref, da_ref, db_ref, dza_ref, dzs_ref, dba_ref, dbs_ref):
        @pl.when(pl.program_id(1) == 0)
        def _():
            dba_ref[...] = jnp.zeros_like(dba_ref)
            dbs_ref[...] = jnp.zeros_like(dbs_ref)

        dm = dm_ref[...]
        sa = _sigmoid(za_ref[...] + ba_ref[...])
        ss = _sigmoid(zs_ref[...] + bs_ref[...])
        da_ref[...] = (dm * sa).astype(BF16)
        db_ref[...] = (dm * ss).astype(BF16)
        dza = dm * a_ref[...] * sa * (1.0 - sa)
        dzs = dm * b_ref[...] * ss * (1.0 - ss)
        dza_ref[...] = dza.astype(BF16)
        dzs_ref[...] = dzs.astype(BF16)
        dba_ref[...] += jnp.sum(dza, axis=0, keepdims=True)
        dbs_ref[...] += jnp.sum(dzs, axis=0, keepdims=True)

    blk = pl.BlockSpec((tm, tn), lambda j, i: (i, j))
    za = pl.BlockSpec((tm, tn), lambda j, i: (i, OFF_G // tn + j))
    zs = pl.BlockSpec((tm, tn), lambda j, i: (i, (OFF_G + D) // tn + j))
    ba = pl.BlockSpec((1, tn), lambda j, i: (0, j))
    bs = pl.BlockSpec((1, tn), lambda j, i: (0, D // tn + j))
    big = jax.ShapeDtypeStruct((T, D), BF16)
    vec = jax.ShapeDtypeStruct((1, D), F32)
    return pl.pallas_call(
        body, name=name, grid=(D // tn, T // tm), in_specs=[blk, blk, blk, za, zs, ba, bs],
        out_specs=[blk, blk, blk, blk, ba, ba], out_shape=[big, big, big, big, vec, vec],
        compiler_params=_params("parallel", "arbitrary"))(dm, a, b, z, z, bias, bias)


_HALF = N_DEV // 2


def _wgu_block(d):
    return d // 2, d % 2


def _ffn_fwd(h, wgu, name, comm=None):
    T, D = h.shape
    n = wgu.shape[3]
    F = _HALF * n
    tm = _pick(T, 512)

    def body(h_ref, wg_ref, wu_ref, g_ref, u_ref, act_ref):
        hv = h_ref[...]
        g = _dot(hv, wg_ref[...])
        u = _dot(hv, wu_ref[...])
        g_ref[...] = g
        u_ref[...] = u
        act_ref[...] = (g * _sigmoid(g) * u).astype(BF16)

    wg = pl.BlockSpec((None, None, D, n), lambda j, i: (*_wgu_block(j), 0, 0))
    wu = pl.BlockSpec((None, None, D, n), lambda j, i: (*_wgu_block(j + _HALF), 0, 0))
    out = pl.BlockSpec((tm, n), lambda j, i: (i, j))
    outs, extra = _call(
        body, name, (_HALF, T // tm), [pl.BlockSpec((tm, D), lambda j, i: (i, 0)), wg, wu], [out, out, out],
        [jax.ShapeDtypeStruct((T, F), F32), jax.ShapeDtypeStruct((T, F), F32), jax.ShapeDtypeStruct((T, F), BF16)],
        [], ("parallel", "parallel"), (h, wgu, wgu), comm)
    return outs if comm is None else (outs, extra)


def _ffn_bwd_in(dg, du, wgu, name, comm=None):
    T, F = dg.shape
    D, n = wgu.shape[2], wgu.shape[3]
    tm, tn = _pick(T, 1024), _pick(D, 1024)

    def body(dg_ref, du_ref, w_ref, o_ref, acc_ref):
        k = pl.program_id(2)

        @pl.when(k == 0)
        def _():
            acc_ref[...] = jnp.zeros_like(acc_ref)

        @pl.when(k < _HALF)
        def _():
            acc_ref[...] += _dot(dg_ref[...], w_ref[...], "nt")

        @pl.when(k >= _HALF)
        def _():
            acc_ref[...] += _dot(du_ref[...], w_ref[...], "nt")

        @pl.when(k == N_DEV - 1)
        def _():
            o_ref[...] = acc_ref[...]

    dg_spec = pl.BlockSpec((tm, n), lambda i, j, k: (i, jnp.minimum(k, _HALF - 1)))
    du_spec = pl.BlockSpec((tm, n), lambda i, j, k: (i, jnp.maximum(k - _HALF, 0)))
    w_spec = pl.BlockSpec((None, None, tn, n), lambda i, j, k: (*_wgu_block(k), j, 0))
    o_spec = pl.BlockSpec((tm, tn), lambda i, j, k: (i, j))
    (out,), extra = _call(
        body, name, (T // tm, D // tn, N_DEV), [dg_spec, du_spec, w_spec], [o_spec],
        [jax.ShapeDtypeStruct((T, D), F32)], [pltpu.VMEM((tm, tn), F32)], ("parallel", "parallel", "arbitrary"),
        (dg, du, wgu), comm)
    return out if comm is None else (out, extra)


def _ffn_bwd_w(h, dg, du, name, comm=None):
    T, D = h.shape
    n = dg.shape[1] // _HALF
    tm, tk = _pick(D, 1024), _pick(T, 2048)
    nk = T // tk

    def body(h_ref, dg_ref, du_ref, o_ref, acc_ref):
        j, k = pl.program_id(0), pl.program_id(2)

        @pl.when(k == 0)
        def _():
            acc_ref[...] = jnp.zeros_like(acc_ref)

        @pl.when(j < _HALF)
        def _():
            acc_ref[...] += _dot(h_ref[...], dg_ref[...], "tn")

        @pl.when(j >= _HALF)
        def _():
            acc_ref[...] += _dot(h_ref[...], du_ref[...], "tn")

        @pl.when(k == nk - 1)
        def _():
            o_ref[...] = acc_ref[...].astype(BF16)

    h_spec = pl.BlockSpec((tk, tm), lambda j, i, k: (k, i))
    dg_spec = pl.BlockSpec((tk, n), lambda j, i, k: (jnp.where(j < _HALF, k, nk - 1), jnp.minimum(j, _HALF - 1)))
    du_spec = pl.BlockSpec((tk, n), lambda j, i, k: (jnp.where(j >= _HALF, k, 0), jnp.maximum(j - _HALF, 0)))
    o_spec = pl.BlockSpec((None, None, tm, n), lambda j, i, k: (*_wgu_block(j), i, 0))
    (out,), extra = _call(
        body, name, (N_DEV, D // tm, nk), [h_spec, dg_spec, du_spec], [o_spec],
        [jax.ShapeDtypeStruct((_HALF, 2, D, n), BF16)], [pltpu.VMEM((tm, n), F32)],
        ("parallel", "parallel", "arbitrary"), (h, dg, du), comm)
    return out if comm is None else (out, extra)


def _ffn_bwd_act(dx, wo, g, u, name, comm=None):
    T, D = dx.shape
    F = wo.shape[0]
    tm, tn = _pick(T, 512), _pick(F, 1408)

    def body(dx_ref, wo_ref, g_ref, u_ref, dg_ref, du_ref):
        dact = _dot(dx_ref[...].astype(BF16), wo_ref[...], "nt")
        gv = g_ref[...]
        sg = _sigmoid(gv)
        dg_ref[...] = (dact * u_ref[...] * sg * (1.0 + gv * (1.0 - sg))).astype(BF16)
        du_ref[...] = (dact * gv * sg).astype(BF16)

    out = pl.BlockSpec((tm, tn), lambda i, j: (i, j))
    big = jax.ShapeDtypeStruct((T, F), BF16)
    outs, extra = _call(
        body, name, (T // tm, F // tn),
        [pl.BlockSpec((tm, D), lambda i, j: (i, 0)), pl.BlockSpec((tn, D), lambda i, j: (j, 0)), out, out],
        [out, out], [big, big], [], ("parallel", "parallel"), (dx, wo, g, u), comm)
    return outs if comm is None else (outs, extra)


def _place():
    x, y, c = lax.axis_index("x"), lax.axis_index("y"), lax.axis_index("c")
    other_chips = [(1 - x, y), (x, 1 - y), (1 - x, 1 - y)]
    return x, y, c, 2 * x + y, other_chips


_ANY = pl.BlockSpec(memory_space=pl.ANY)
_N_COPIES = 7


def _all_gather(shards, name):
    n = len(shards)

    def body(*refs):
        ins, outs = refs[:n], refs[n:2 * n]
        send_sems, recv_sems, local_sems = refs[2 * n:]
        x, y, c, chip, other_chips = _place()
        sibling = (x, y, 1 - c)

        def remote(src, dst, a, j, dev):
            return pltpu.make_async_remote_copy(src_ref=src, dst_ref=dst, send_sem=send_sems.at[a, j],
                                                recv_sem=recv_sems.at[a, j], device_id=dev, device_id_type=MESH)

        sends, local = [], []
        for a in range(n):
            mine = outs[a].at[chip, c]
            local.append(pltpu.make_async_copy(ins[a], mine, local_sems.at[a]))
            local[a].start()
            for j, (ox, oy) in enumerate(other_chips):
                sends.append(remote(ins[a], mine, a, 1 + j, (ox, oy, c)))
                sends[-1].start()
            sends.append(remote(ins[a], mine, a, 0, sibling))
            sends[-1].start()
        for a in range(n):
            for j, (ox, oy) in enumerate(other_chips):
                slot = outs[a].at[2 * ox + oy, c]
                remote(ins[a], slot, a, 1 + j, (ox, oy, c)).wait_recv()
                sends.append(remote(slot, slot, a, 4 + j, sibling))
                sends[-1].start()
        for a in range(n):
            remote(ins[a], outs[a].at[chip, 1 - c], a, 0, sibling).wait_recv()
            for j, (ox, oy) in enumerate(other_chips):
                remote(ins[a], outs[a].at[2 * ox + oy, 1 - c], a, 4 + j, sibling).wait_recv()
        for cp in sends:
            cp.wait_send()
        for a in range(n):
            local[a].wait()

    return pl.pallas_call(
        body, name=name, in_specs=[_ANY] * n, out_specs=[_ANY] * n,
        out_shape=[jax.ShapeDtypeStruct((4, 2) + s.shape, s.dtype) for s in shards],
        scratch_shapes=[pltpu.SemaphoreType.DMA((n, _N_COPIES)), pltpu.SemaphoreType.DMA((n, _N_COPIES)),
                        pltpu.SemaphoreType.DMA((n,))])(*shards)


def _pair_exchange(parts, name):
    n = len(parts)

    def body(*refs):
        ins, theirs = refs[:n], refs[n:2 * n]
        send_sems, recv_sems = refs[2 * n:]
        x, y, c, _, _ = _place()
        sends = []
        for a in range(n):
            for k in range(4):
                sends.append(pltpu.make_async_remote_copy(
                    src_ref=ins[a].at[k, 1 - c], dst_ref=theirs[a].at[k], send_sem=send_sems.at[a, k],
                    recv_sem=recv_sems.at[a, k], device_id=(x, y, 1 - c), device_id_type=MESH))
                sends[-1].start()
        for cp in sends:
            cp.wait_recv()
            cp.wait_send()

    return pl.pallas_call(
        body, name=name, in_specs=[_ANY] * n, out_specs=[_ANY] * n,
        out_shape=[jax.ShapeDtypeStruct((4,) + p.shape[2:], p.dtype) for p in parts],
        scratch_shapes=[pltpu.SemaphoreType.DMA((n, 4)), pltpu.SemaphoreType.DMA((n, 4))])(*parts)


def _chip_exchange(parts, name):
    n = len(parts)

    def body(*refs):
        ins, got = refs[:n], refs[n:2 * n]
        send_sems, recv_sems = refs[2 * n:]
        _, _, c, _, other_chips = _place()
        sends = []
        for a in range(n):
            for j, (ox, oy) in enumerate(other_chips):
                sends.append(pltpu.make_async_remote_copy(
                    src_ref=ins[a].at[2 * ox + oy], dst_ref=got[a].at[j], send_sem=send_sems.at[a, j],
                    recv_sem=recv_sems.at[a, j], device_id=(ox, oy, c), device_id_type=MESH))
                sends[-1].start()
        for cp in sends:
            cp.wait_recv()
            cp.wait_send()

    return pl.pallas_call(
        body, name=name, in_specs=[_ANY] * n, out_specs=[_ANY] * n,
        out_shape=[jax.ShapeDtypeStruct((3,) + p.shape[1:], p.dtype) for p in parts],
        scratch_shapes=[pltpu.SemaphoreType.DMA((n, 3)), pltpu.SemaphoreType.DMA((n, 3))])(*parts)


def _remote(src, dst, send_sems, recv_sems, a, j, dev):
    return pltpu.make_async_remote_copy(src_ref=src, dst_ref=dst, send_sem=send_sems.at[a, j],
                                        recv_sem=recv_sems.at[a, j], device_id=dev, device_id_type=MESH)


def _gather_send(shards):
    n = len(shards)

    def copies(cin, cout, sems, arriving):
        send_sems, recv_sems, local_sems = sems
        x, y, c, chip, other_chips = _place()
        sibling = (x, y, 1 - c)
        peers = [(0, sibling, (chip, 1 - c))] + [(1 + j, (ox, oy, c), (2 * ox + oy, c))
                                                 for j, (ox, oy) in enumerate(other_chips)]
        sends, recvs, local = [], [], []
        for a in range(n):
            mine = cout[a].at[chip, c]
            local.append(pltpu.make_async_copy(cin[a], mine, local_sems.at[a]))
            for j, dev, slot in peers:
                sends.append(_remote(cin[a], mine, send_sems, recv_sems, a, j, dev))
                if arriving:
                    recvs.append(_remote(cin[a], cout[a].at[slot], send_sems, recv_sems, a, j, dev))
        return sends, recvs, local

    return _Comm(shards, [jax.ShapeDtypeStruct((4, 2) + s.shape, s.dtype) for s in shards],
                 [pltpu.SemaphoreType.DMA((n, 4)), pltpu.SemaphoreType.DMA((n, 4)), pltpu.SemaphoreType.DMA((n,))],
                 copies)


def _gather_pass(gathered):
    n = len(gathered)

    def copies(cin, cout, sems, arriving):
        send_sems, recv_sems = sems
        x, y, c, _, other_chips = _place()
        sibling = (x, y, 1 - c)
        sends, recvs = [], []
        for a in range(n):
            for j, (ox, oy) in enumerate(other_chips):
                k = 2 * ox + oy
                sends.append(_remote(cout[a].at[k, c], cout[a].at[k, c], send_sems, recv_sems, a, j, sibling))
                if arriving:
                    recvs.append(_remote(cout[a].at[k, c], cout[a].at[k, 1 - c], send_sems, recv_sems, a, j, sibling))
        return sends, recvs, []

    return _Comm(gathered, [jax.ShapeDtypeStruct(g.shape, g.dtype) for g in gathered],
                 [pltpu.SemaphoreType.DMA((n, 3)), pltpu.SemaphoreType.DMA((n, 3))], copies,
                 aliases={i: i for i in range(n)})


def _scatter_send(sums):
    n = len(sums)

    def copies(cin, cout, sems, arriving):
        send_sems, recv_sems = sems
        _, _, c, _, other_chips = _place()
        sends = [_remote(cin[a].at[2 * ox + oy], cout[a].at[j], send_sems, recv_sems, a, j, (ox, oy, c))
                 for a in range(n) for j, (ox, oy) in enumerate(other_chips)]
        return sends, sends, []

    return _Comm(sums, [jax.ShapeDtypeStruct((3,) + s.shape[1:], s.dtype) for s in sums],
                 [pltpu.SemaphoreType.DMA((n, 3)), pltpu.SemaphoreType.DMA((n, 3))], copies)


def _join(comms):
    if len(comms) == 1:
        return comms[0]
    args, outs, sems, aliases, spans = [], [], [], {}, []
    for cm in comms:
        spans.append((len(args), len(outs), len(sems)))
        aliases.update({len(args) + i: len(outs) + o for i, o in cm.aliases.items()})
        args, outs, sems = args + cm.args, outs + cm.out_shapes, sems + cm.sems

    def copies(cin, cout, sem_refs, arriving):
        sends, recvs, local = [], [], []
        for cm, (a0, o0, s0) in zip(comms, spans):
            part = cm.copies(cin[a0:a0 + len(cm.args)], cout[o0:o0 + len(cm.out_shapes)],
                             sem_refs[s0:s0 + len(cm.sems)], arriving)
            sends, recvs, local = sends + part[0], recvs + part[1], local + part[2]
        return sends, recvs, local

    return _Comm(args, outs, sems, copies, aliases)


class _Schedule:
    def __init__(self):
        self.rides = {}

    def ride(self, host, make, names, operands):
        results = {}
        self.rides.setdefault(host, []).append((make, names, operands, results))
        return results

    def carry(self, host, fn, *args, **kwargs):
        rides = self.rides.get(host)
        if not rides:
            return fn(*args, host, **kwargs)
        comm = _join([make([operands[n] for n in names]) for make, names, operands, _ in rides])
        out, extra = fn(*args, host, comm=comm, **kwargs)
        for _, names, _, results in rides:
            results.update(zip(names, extra[:len(names)]))
            extra = extra[len(names):]
        return out


def _add_pair(core, parts, theirs, name):
    k, _, R, C = parts.shape
    tr = _pick(R, 512, 16)

    def body(core_ref, a_ref, b_ref, o_ref):
        o_ref[...] = (a_ref[...].astype(F32) + b_ref[...].astype(F32)).astype(BF16)

    blk = pl.BlockSpec((None, tr, C), lambda s, i, core_ref: (s, i, 0))
    grid_spec = pltpu.PrefetchScalarGridSpec(
        num_scalar_prefetch=1, grid=(k, R // tr),
        in_specs=[pl.BlockSpec((None, None, tr, C), lambda s, i, core_ref: (s, core_ref[0], i, 0)), blk],
        out_specs=blk)
    return pl.pallas_call(
        body, name=name, grid_spec=grid_spec, out_shape=jax.ShapeDtypeStruct(theirs.shape, BF16),
        compiler_params=_params("parallel", "parallel"))(core, parts, theirs)


def _adamw_math(w, g, m, v):
    m = ADAM_B1 * m + (1.0 - ADAM_B1) * g
    v = ADAM_B2 * v + (1.0 - ADAM_B2) * (g * g)
    m_hat = m / (1.0 - ADAM_B1 ** ADAM_STEP)
    v_hat = v / (1.0 - ADAM_B2 ** ADAM_STEP)
    delta = -ADAM_LR * (m_hat / (jnp.sqrt(v_hat) + ADAM_EPS) + ADAM_WD * w)
    return delta, m, v


def _scattered_pieces(sums, got):
    return [("own", sums)] + [("peer%d" % j, got) for j in range(3)]


def _piece_spec(kind, tr, C):
    if kind == "own":
        return pl.BlockSpec((None, tr, C), lambda i, chip_ref: (chip_ref[0], i, 0))
    if kind == "plain":
        return pl.BlockSpec((tr, C), lambda i, chip_ref: (i, 0))
    return pl.BlockSpec((None, tr, C), functools.partial(lambda j, i, chip_ref: (j, i, 0), int(kind[-1])))


def _adamw_shard(chip, w, m, v, layer, pieces, so_far, name):
    L, R, C = w.shape
    tr = _pick(R, 256, 16)
    n_p = len(pieces)

    def body(chip_ref, w_ref, m_ref, v_ref, *rest):
        g = rest[0][...].astype(F32)
        for p in rest[1:n_p]:
            g = g + p[...].astype(F32)
        delta, nm, nv = _adamw_math(w_ref[...], g, m_ref[...], v_ref[...])
        g_ref, d_ref, nm_ref, nv_ref = rest[-4:]
        g_ref[...] = g
        d_ref[...] = delta
        nm_ref[...] = nm
        nv_ref[...] = nv

    state = pl.BlockSpec((None, tr, C), lambda i, chip_ref: (layer, i, 0))
    carried = [] if so_far is None else list(so_far)
    first_carried = 1 + 3 + n_p
    grid_spec = pltpu.PrefetchScalarGridSpec(
        num_scalar_prefetch=1, grid=(R // tr,),
        in_specs=[state] * 3 + [_piece_spec(kind, tr, C) for kind, _ in pieces] + [_ANY] * len(carried),
        out_specs=[state] * 4)
    return pl.pallas_call(
        body, name=name, grid_spec=grid_spec, out_shape=[jax.ShapeDtypeStruct((L, R, C), F32)] * 4,
        input_output_aliases={first_carried + t: t for t in range(len(carried))},
        compiler_params=_params("parallel"))(chip, w, m, v, *[a for _, a in pieces], *carried)


def _adamw_small(w, m, v, gathered, name):
    R = w.shape[0]
    tr = _pick(R, 512, SUBLANES)

    def body(w_ref, m_ref, v_ref, gg_ref, g_ref, d_ref, nm_ref, nv_ref):
        g = gg_ref[0, 0]
        for k in range(4):
            for c in range(2):
                if c or k:
                    g = g + gg_ref[k, c]
        delta, nm, nv = _adamw_math(w_ref[...], g, m_ref[...], v_ref[...])
        g_ref[...] = g
        d_ref[...] = delta
        nm_ref[...] = nm
        nv_ref[...] = nv

    row = pl.BlockSpec((tr, LANES), lambda i: (i, 0))
    out = jax.ShapeDtypeStruct((R, LANES), F32)
    return pl.pallas_call(
        body, name=name, grid=(R // tr,),
        in_specs=[row, row, row, pl.BlockSpec((4, 2, tr, LANES), lambda i: (0, 0, i, 0))], out_specs=[row] * 4,
        out_shape=[out] * 4, compiler_params=_params("parallel"))(w, m, v, gathered)


def _layer_fwd(x, w, s, tag, sched):
    h = _rms_fwd(x, s["norm_mix_g"], f"rms_mix_{tag}")
    z = sched.carry(f"in_proj_{tag}", _mm, h, w["w_in_t"], "nt", F32, tn=512)
    q, k, v = (z[:, a:b].reshape(z.shape[0], -1, HEAD_DIM).transpose(1, 0, 2)
               for a, b in ((0, OFF_K), (OFF_K, OFF_V), (OFF_V, OFF_U)))
    ya = sched.carry(f"attn_fwd_{tag}", _attn_fwd, q, k, v, s["q_norm_g"], s["k_norm_g"], s["attn_sinks"])
    ya = ya.transpose(1, 0, 2).reshape(z.shape[0], ATTN_WIDTH)
    y, s_r, s_i = sched.carry(f"ssm_fwd_{tag}", _ssm_fwd, z, *s["ssm16"], s["ssm_d"])
    pre, y3 = _glu_fwd(y, w["ssm_glu_w"], s["ssm_glu_b"], f"glu_fwd_{tag}")
    a, b, merged = sched.carry(f"merge_fwd_{tag}", _merge_fwd, ya, y3, w["w_attn_branch"], w["w_ssm_branch"], z,
                               s["gate_bias"])
    x1 = _mm(merged, w["w_out"], "nn", F32, f"out_proj_{tag}", residual=x)
    h2 = _rms_fwd(x1, s["norm_ffn_g"], f"rms_ffn_{tag}")
    g, u, act = sched.carry(f"ffn_fwd_{tag}", _ffn_fwd, h2, w["w_ffn_in"])
    x2 = sched.carry(f"ffn_out_{tag}", _mm, act, w["w_ffn_out"], "nn", F32, residual=x1)
    saved = dict(x=x, h=h, z=z, ya=ya, y=y, s_r=s_r, s_i=s_i, pre=pre, y3=y3, a=a, b=b, merged=merged,
                 x1=x1, h2=h2, g=g, u=u, act=act)
    return x2, saved


def _layer_bwd(dx2, sv, w, s, tag, sched, scatter):
    gw, gs = {}, {}
    dg16, du16 = _ffn_bwd_act(dx2, w["w_ffn_out"], sv["g"], sv["u"], f"ffn_bwd_act_{tag}")
    gw["w_ffn_out"] = _mm(sv["act"], dx2, "tn", BF16, f"dw_ffn_out_{tag}", tm=1408)
    dh2 = _ffn_bwd_in(dg16, du16, w["w_ffn_in"], f"dh2_{tag}")
    gw["w_ffn_in"] = _ffn_bwd_w(sv["h2"], dg16, du16, f"dw_ffn_in_{tag}")
    scatter(FFN_WEIGHTS[:1], gw, f"attn_bwd_{tag}")
    dx1, gs["norm_ffn_g"] = _rms_bwd(dh2, sv["x1"], s["norm_ffn_g"], dx2, f"rms_ffn_bwd_{tag}")
    dm = _mm(dx1, w["w_out"], "nt", F32, f"dmerged_{tag}")
    gw["w_out"] = _mm(sv["merged"], dx1, "tn", BF16, f"dw_out_{tag}")
    da16, db16, dza, dzs, dba, dbs = _merge_bwd(dm, sv["a"], sv["b"], sv["z"], s["gate_bias"], f"merge_bwd_{tag}")
    gs["gate_bias"] = jnp.concatenate([dba, dbs], axis=1)
    dya = _mm(da16, w["w_attn_branch"], "nt", F32, f"dya_{tag}")
    gw["w_attn_branch"] = _mm(sv["ya"], da16, "tn", BF16, f"dw_attn_branch_{tag}")
    dy3 = _mm(db16, w["w_ssm_branch"], "nt", F32, f"dy3_{tag}")
    gw["w_ssm_branch"] = _mm(sv["y3"], db16, "tn", BF16, f"dw_ssm_branch_{tag}")
    dpre16, t1, y2_16, gs["ssm_glu_b"] = _glu_bwd_gate(dy3, sv["y"], sv["pre"], f"glu_bwd_gate_{tag}")
    dy = _glu_bwd_in(dpre16, w["ssm_glu_w"], t1, sv["y"], f"glu_bwd_in_{tag}")
    gw["ssm_glu_w"] = _mm(y2_16, dpre16, "tn", BF16, f"dw_glu_{tag}")
    scatter(FFN_WEIGHTS[1:] + MIXER_WEIGHTS[1:], gw, f"ssm_bwd_{tag}")
    du_ssm, *gs["ssm_disc"], gs["ssm_d"] = sched.carry(f"ssm_bwd_{tag}", _ssm_bwd, dy, sv["z"], sv["s_r"], sv["s_i"],
                                                       *s["ssm16"], s["ssm_d"])
    dq, dk, dv, gs["q_norm_g"], gs["k_norm_g"], dsk = sched.carry(
        f"attn_bwd_{tag}", _attn_bwd, sv["z"], dya, s["q_norm_g"], s["k_norm_g"], s["attn_sinks"])
    gs["attn_sinks"] = dsk[:, 0].reshape(1, N_Q_HEADS)
    dz = [dq, jnp.concatenate([dk, dv], axis=1).astype(BF16), du_ssm, dza, dzs]
    gw["w_in"] = _mm_cols_tn(dz, sv["h"], f"dw_in_t_{tag}", tn=2048)
    scatter(MIXER_WEIGHTS[:1], gw, f"dh_{tag}")
    dh = sched.carry(f"dh_{tag}", _mm_cols_nn, dz, w["w_in_t"])
    dx, gs["norm_mix_g"] = _rms_bwd(dh, sv["x"], s["norm_mix_g"], dx1, f"rms_mix_bwd_{tag}")
    return dx, gs


def _shard_to_send(name, shard):
    return (shard.T if name == "w_in" else shard).astype(BF16)


def _assemble(name, gathered):
    if name == "w_ffn_in":
        return gathered
    if name in COL_SHARDED and name != "w_in":
        rows = gathered.shape[2]
        return gathered.transpose(2, 0, 1, 3).reshape(rows, -1)
    return gathered.reshape(-1, gathered.shape[3])


class _Weights:
    def __init__(self):
        self.sources, self.ready = [], {}

    def __getitem__(self, name):
        if name not in self.ready:
            key = "w_in" if name == "w_in_t" else name
            (gathered,) = [src[key] for src in self.sources if key in src]
            self.ready[name] = _assemble(key, gathered)
        return self.ready[name]


def _disassemble(name, grad):
    if name == "w_ffn_in":
        return grad
    if name in COL_SHARDED and name != "w_in":
        rows, cols = grad.shape
        return grad.reshape(rows, 4, 2, cols // N_DEV).transpose(1, 2, 0, 3)
    rows, cols = grad.shape
    return grad.reshape(4, 2, rows // N_DEV, cols)


def _pack(arrays):
    flat = jnp.concatenate([a.reshape(-1) for a in arrays])
    pad = (-flat.shape[0]) % (SUBLANES * LANES)
    return jnp.pad(flat, (0, pad)).reshape(-1, LANES)


def _unpack(packed, like):
    flat, out, off = packed.reshape(-1), [], 0
    for a in like:
        out.append(flat[off:off + a.size].reshape(a.shape))
        off += a.size
    return out


def kernel(x, norm_mix_g, w_in, gate_bias, q_norm_g, k_norm_g, attn_sinks, ssm_lambda_re, ssm_lambda_im, ssm_log_dt, ssm_b_re, ssm_b_im, ssm_c_re, ssm_c_im, ssm_d, ssm_glu_w, ssm_glu_b, w_attn_branch, w_ssm_branch, w_out, norm_ffn_g, w_ffn_in, w_ffn_out, loss_target, m_norm_mix_g, m_w_in, m_gate_bias, m_q_norm_g, m_k_norm_g, m_attn_sinks, m_ssm_lambda_re, m_ssm_lambda_im, m_ssm_log_dt, m_ssm_b_re, m_ssm_b_im, m_ssm_c_re, m_ssm_c_im, m_ssm_d, m_ssm_glu_w, m_ssm_glu_b, m_w_attn_branch, m_w_ssm_branch, m_w_out, m_norm_ffn_g, m_w_ffn_in, m_w_ffn_out, v_norm_mix_g, v_w_in, v_gate_bias, v_q_norm_g, v_k_norm_g, v_attn_sinks, v_ssm_lambda_re, v_ssm_lambda_im, v_ssm_log_dt, v_ssm_b_re, v_ssm_b_im, v_ssm_c_re, v_ssm_c_im, v_ssm_d, v_ssm_glu_w, v_ssm_glu_b, v_w_attn_branch, v_w_ssm_branch, v_w_out, v_norm_ffn_g, v_w_ffn_in, v_w_ffn_out):
    given = dict(locals())
    wts = {n: given[n] for n in WEIGHTS}
    mom = {n: given["m_" + n] for n in WEIGHTS}
    var = {n: given["v_" + n] for n in WEIGHTS}
    depth = w_in.shape[0]
    xs = x[0]
    target = loss_target[0]

    sched = _Schedule()
    shards = [{n: _shard_to_send(n, wts[n][l]) for n in BIG} for l in range(depth)]
    full = [_Weights() for _ in range(depth)]

    def gather(layer, names, send_host, pass_host):
        sent = sched.ride(send_host, _gather_send, names, shards[layer])
        full[layer].sources.append(sched.ride(pass_host, _gather_pass, names, sent))

    first = MIXER_WEIGHTS[:1]
    full[0].sources.append(dict(zip(first, _all_gather([shards[0][n] for n in first], "gather_w_in_0"))))
    gather(0, MIXER_WEIGHTS[1:], "in_proj_0", "attn_fwd_0")
    gather(0, ("w_ffn_in",), "attn_fwd_0", "ssm_fwd_0")
    gather(0, ("w_ffn_out",), "ssm_fwd_0", "merge_fwd_0")
    for l in range(1, depth):
        gather(l, first, f"ffn_fwd_{l - 1}", f"ffn_out_{l - 1}")
        gather(l, MIXER_WEIGHTS[1:], f"ffn_out_{l - 1}", f"in_proj_{l}")
        gather(l, ("w_ffn_out",), f"in_proj_{l}", f"attn_fwd_{l}")
        gather(l, ("w_ffn_in",), f"attn_fwd_{l}", f"ssm_fwd_{l}")

    small, disc_vjp = [], []
    for l in range(depth):
        s = {n: wts[n][l].reshape(1, -1) for n in ("norm_mix_g", "gate_bias", "q_norm_g", "k_norm_g", "attn_sinks",
                                                   "ssm_d", "ssm_glu_b", "norm_ffn_g")}
        disc, vjp = jax.vjp(_ssm_discretize, *[wts[n][l] for n in ("ssm_lambda_re", "ssm_lambda_im", "ssm_log_dt",
                                                                  "ssm_b_re", "ssm_b_im", "ssm_c_re", "ssm_c_im")])
        s["ssm16"] = (disc[0], disc[1]) + tuple(d.astype(BF16) for d in disc[2:])
        small.append(s)
        disc_vjp.append(vjp)

    act, saved = xs, []
    for l in range(depth):
        act, sv = _layer_fwd(act, full[l], small[l], str(l), sched)
        saved.append(sv)
    dact, loss_local = _loss_grad(act, target, "loss_head")

    out = {"grad": {}, "delta": {}, "new_m": {}, "new_v": {}}
    results = {n: None for n in BIG}
    core = lax.axis_index("c").astype(jnp.int32).reshape(1)
    chip = (2 * lax.axis_index("x") + lax.axis_index("y")).astype(jnp.int32).reshape(1)
    state = {n: [a.transpose(0, 2, 1) if n == "w_in" else a for a in (wts[n], mom[n], var[n])] for n in BIG}
    small_grads = [None] * depth
    for l in reversed(range(depth)):
        scattered = []

        def scatter(names, grads, host):
            parts = [_disassemble(n, grads[n]) for n in names]
            theirs = _pair_exchange(parts, f"grad_pair_exchange_{host}")
            sums = {n: _add_pair(core, p, t, f"grad_pair_sum_{n}_{host}") for n, p, t in zip(names, parts, theirs)}
            scattered.append((sums, sched.ride(host, _scatter_send, names, sums)))

        dact, gs = _layer_bwd(dact, saved[l], full[l], small[l], str(l), sched, scatter)
        (gs["ssm_lambda_re"], gs["ssm_lambda_im"], gs["ssm_log_dt"], gs["ssm_b_re"], gs["ssm_b_im"], gs["ssm_c_re"],
         gs["ssm_c_im"]) = disc_vjp[l](tuple(gs.pop("ssm_disc")))
        small_grads[l] = gs
        for sums, got in scattered:
            for n in got:
                results[n] = _adamw_shard(chip, state[n][0], state[n][1], state[n][2], l,
                                          _scattered_pieces(sums[n], got[n]), results[n], f"adamw_{n}_{l}")
    loss = lax.psum(loss_local, ("x", "y", "c"))
    for n in BIG:
        for kind, res in zip(("grad", "delta", "new_m", "new_v"), results[n]):
            out[kind][n] = res.transpose(0, 2, 1) if n == "w_in" else res

    like = [wts[n] for n in SMALL]
    g_small = _pack([jnp.stack([small_grads[l][n].reshape(wts[n].shape[1:]) for l in range(depth)]) for n in SMALL])
    (gathered_small,) = _all_gather([g_small], "gather_small_grads")
    res = _adamw_small(_pack(like), _pack([mom[n] for n in SMALL]), _pack([var[n] for n in SMALL]), gathered_small,
                       "adamw_small")
    for kind, packed in zip(("grad", "delta", "new_m", "new_v"), res):
        for n, a in zip(SMALL, _unpack(packed, like)):
            out[kind][n] = a

    grad_x = dact.reshape(x.shape)
    return (loss, grad_x, *[out["grad"][n] for n in WEIGHTS], *[out["delta"][n] for n in WEIGHTS],
            *[out["new_m"][n] for n in WEIGHTS], *[out["new_v"][n] for n in WEIGHTS])
```

```python
import functools
import math

import jax
import jax.numpy as jnp
from jax import lax
from jax.experimental import pallas as pl
from jax.experimental.pallas import tpu as pltpu

F32, BF16 = jnp.float32, jnp.bfloat16
MESH = pl.DeviceIdType.MESH

D_MODEL = 2048
HEAD_DIM = 64
N_Q_HEADS = 16
N_KV_HEADS = 4
GQA_GROUP = N_Q_HEADS // N_KV_HEADS
ATTN_WIDTH = N_Q_HEADS * HEAD_DIM
KV_WIDTH = N_KV_HEADS * HEAD_DIM
WINDOW = 128
BLOCK = 128
SSM_WIDTH = D_MODEL // 2
SSM_GROUP_CH = 16
SSM_GROUPS = SSM_WIDTH // SSM_GROUP_CH
SSM_STATE = 64
D_FF = 5632
OFF_K = ATTN_WIDTH
OFF_V = OFF_K + KV_WIDTH
OFF_U = OFF_V + KV_WIDTH
OFF_G = OFF_U + SSM_WIDTH
IN_WIDTH = OFF_G + 2 * D_MODEL
RMS_EPS = 1e-6
ATTN_SCALE = HEAD_DIM ** -0.5
NEG_BIG = -1e30

SSM_NGB = 4
SSM_GB_CH = SSM_WIDTH // SSM_NGB
SSM_GB_ST = SSM_GROUPS * SSM_STATE // SSM_NGB
SUBLANES = 8
LANES = 128
SSM_TT = 512

ADAM_LR = 0.001
ADAM_B1 = 0.9
ADAM_B2 = 0.999
ADAM_EPS = 1e-08
ADAM_WD = 0.01
ADAM_STEP = 10

N_DEV = 8
VMEM_LIMIT_BYTES = 52 * 1024 * 1024

BIG = ("w_in", "ssm_glu_w", "w_attn_branch", "w_ssm_branch", "w_out", "w_ffn_in", "w_ffn_out")
COL_SHARDED = ("w_in", "w_attn_branch", "w_ssm_branch", "w_ffn_in")
FFN_WEIGHTS = ("w_ffn_in", "w_ffn_out")
MIXER_WEIGHTS = ("w_in", "ssm_glu_w", "w_attn_branch", "w_ssm_branch", "w_out")
SMALL = ("norm_mix_g", "gate_bias", "q_norm_g", "k_norm_g", "attn_sinks", "ssm_lambda_re", "ssm_lambda_im",
         "ssm_log_dt", "ssm_b_re", "ssm_b_im", "ssm_c_re", "ssm_c_im", "ssm_d", "ssm_glu_b", "norm_ffn_g")
WEIGHTS = ("norm_mix_g", "w_in", "gate_bias", "q_norm_g", "k_norm_g", "attn_sinks", "ssm_lambda_re", "ssm_lambda_im",
           "ssm_log_dt", "ssm_b_re", "ssm_b_im", "ssm_c_re", "ssm_c_im", "ssm_d", "ssm_glu_w", "ssm_glu_b",
           "w_attn_branch", "w_ssm_branch", "w_out", "norm_ffn_g", "w_ffn_in", "w_ffn_out")


def _pick(n, target, mult=LANES):
    best = None
    for t in range(mult, min(n, target) + 1, mult):
        if n % t == 0:
            best = t
    return n if best is None else best


def _params(*sem):
    return pltpu.CompilerParams(dimension_semantics=sem, vmem_limit_bytes=VMEM_LIMIT_BYTES)


def _sigmoid(v):
    return 1.0 / (1.0 + jnp.exp(-v))


_GELU_C = math.sqrt(2.0 / math.pi)


def _gelu(v):
    return 0.5 * v * (1.0 + jnp.tanh(_GELU_C * (v + 0.044715 * v * v * v)))


def _gelu_grad(v):
    t = jnp.tanh(_GELU_C * (v + 0.044715 * v * v * v))
    return 0.5 * (1.0 + t) + 0.5 * v * (1.0 - t * t) * _GELU_C * (1.0 + 3.0 * 0.044715 * v * v)


_DN = {"nn": (((1,), (0,)), ((), ())), "nt": (((1,), (1,)), ((), ())), "tn": (((0,), (0,)), ((), ()))}


def _dot(a, b, dims="nn"):
    return lax.dot_general(a, b, _DN[dims], preferred_element_type=F32)


class _Comm:
    def __init__(self, args, out_shapes, sems, copies, aliases=None):
        self.args, self.out_shapes, self.sems = list(args), list(out_shapes), list(sems)
        self.copies, self.aliases = copies, dict(aliases or {})


def _call(body, name, grid, in_specs, out_specs, out_shape, scratch_shapes, semantics, args, comm=None):
    in_specs, out_specs, out_shape = list(in_specs), list(out_specs), list(out_shape)
    scratch_shapes = list(scratch_shapes)
    if comm is None:
        res = pl.pallas_call(body, name=name, grid=grid, in_specs=in_specs, out_specs=out_specs, out_shape=out_shape,
                             scratch_shapes=scratch_shapes, compiler_params=_params(*semantics))(*args)
        return list(res), []
    n_in, n_out, n_scr = len(in_specs), len(out_specs), len(scratch_shapes)
    n_cin, n_cout = len(comm.args), len(comm.out_shapes)

    def carrying(*refs):
        ins, cin = refs[:n_in], refs[n_in:n_in + n_cin]
        o0 = n_in + n_cin
        outs, cout = refs[o0:o0 + n_out], refs[o0 + n_out:o0 + n_out + n_cout]
        s0 = o0 + n_out + n_cout
        scr, sems = refs[s0:s0 + n_scr], refs[s0 + n_scr:]
        first = functools.reduce(jnp.logical_and, [pl.program_id(d) == 0 for d in range(len(grid))])
        last = functools.reduce(jnp.logical_and, [pl.program_id(d) == grid[d] - 1 for d in range(len(grid))])

        @pl.when(first)
        def _():
            sends, _, local = comm.copies(cin, cout, sems, False)
            for cp in local + sends:
                cp.start()

        body(*ins, *outs, *scr)

        @pl.when(last)
        def _():
            sends, recvs, local = comm.copies(cin, cout, sems, True)
            for cp in recvs:
                cp.wait_recv()
            for cp in sends:
                cp.wait_send()
            for cp in local:
                cp.wait()

    res = pl.pallas_call(
        carrying, name=name, grid=grid, in_specs=in_specs + [_ANY] * n_cin, out_specs=out_specs + [_ANY] * n_cout,
        out_shape=out_shape + comm.out_shapes, scratch_shapes=scratch_shapes + comm.sems,
        input_output_aliases={n_in + i: n_out + o for i, o in comm.aliases.items()},
        compiler_params=_params(*["arbitrary"] * len(grid)))(*args, *comm.args)
    return list(res[:n_out]), list(res[n_out:])


def _mm(a, b, dims, out_dtype, name, residual=None, tm=1024, tn=1024, tk=2048, comm=None):
    if dims == "tn":
        K, M = a.shape
    else:
        M, K = a.shape
    N = b.shape[0] if dims == "nt" else b.shape[1]
    tm, tn, tk = _pick(M, tm), _pick(N, tn), _pick(K, tk)
    nk = K // tk
    has_res = residual is not None

    def finish(out, refs):
        if has_res:
            out = out + refs[2][...].astype(F32)
        refs[-2][...] = out.astype(out_dtype)

    def body_single(*refs):
        finish(_dot(refs[0][...].astype(BF16), refs[1][...].astype(BF16), dims), refs)

    def body_multi(*refs):
        acc_ref = refs[-1]
        k = pl.program_id(2)

        @pl.when(k == 0)
        def _():
            acc_ref[...] = jnp.zeros_like(acc_ref)

        acc_ref[...] += _dot(refs[0][...].astype(BF16), refs[1][...].astype(BF16), dims)

        @pl.when(k == nk - 1)
        def _():
            finish(acc_ref[...], refs)

    a_spec = (pl.BlockSpec((tk, tm), lambda i, j, k: (k, i)) if dims == "tn"
              else pl.BlockSpec((tm, tk), lambda i, j, k: (i, k)))
    b_spec = (pl.BlockSpec((tn, tk), lambda i, j, k: (j, k)) if dims == "nt"
              else pl.BlockSpec((tk, tn), lambda i, j, k: (k, j)))
    o_spec = pl.BlockSpec((tm, tn), lambda i, j, k: (i, j))
    in_specs = [a_spec, b_spec] + ([o_spec] if has_res else [])
    args = (a, b) + ((residual,) if has_res else ())
    (out,), extra = _call(
        body_single if nk == 1 else body_multi, name, (M // tm, N // tn, nk), in_specs, [o_spec],
        [jax.ShapeDtypeStruct((M, N), out_dtype)], [pltpu.VMEM((tm, tn) if nk > 1 else (SUBLANES, LANES), F32)],
        ("parallel", "parallel", "arbitrary"), args, comm)
    return out if comm is None else (out, extra)


PIECE_BLOCK = 512


def _piece_blocks(pieces):
    counts = [p.shape[1] // PIECE_BLOCK for p in pieces]
    return counts, [sum(counts[:i]) for i in range(len(counts))]


def _mm_cols_nn(pieces, b, name, comm=None, tm=1024, tn=1024):
    M, N = pieces[0].shape[0], b.shape[1]
    tm, tn = _pick(M, tm), _pick(N, tn)
    counts, starts = _piece_blocks(pieces)
    n, nk = len(pieces), sum(counts)

    def body(*refs):
        b_ref, o_ref, acc_ref = refs[n:]
        k = pl.program_id(2)

        @pl.when(k == 0)
        def _():
            acc_ref[...] = jnp.zeros_like(acc_ref)

        for a_ref, c, s in zip(refs[:n], counts, starts):
            @pl.when((k >= s) & (k < s + c))
            def _(a_ref=a_ref):
                acc_ref[...] += _dot(a_ref[...], b_ref[...])

        @pl.when(k == nk - 1)
        def _():
            o_ref[...] = acc_ref[...]

    a_specs = [pl.BlockSpec((tm, PIECE_BLOCK), functools.partial(lambda s, c, i, j, k: (i, jnp.clip(k - s, 0, c - 1)), s, c))
               for c, s in zip(counts, starts)]
    b_spec = pl.BlockSpec((PIECE_BLOCK, tn), lambda i, j, k: (k, j))
    o_spec = pl.BlockSpec((tm, tn), lambda i, j, k: (i, j))
    (out,), extra = _call(body, name, (M // tm, N // tn, nk), a_specs + [b_spec], [o_spec],
                          [jax.ShapeDtypeStruct((M, N), F32)], [pltpu.VMEM((tm, tn), F32)],
                          ("parallel", "parallel", "arbitrary"), (*pieces, b), comm)
    return out if comm is None else (out, extra)


def _mm_cols_tn(pieces, b, name, tn=1024, tk=2048):
    K, N = b.shape
    tn, tk = _pick(N, tn), _pick(K, tk)
    counts, starts = _piece_blocks(pieces)
    n, nk = len(pieces), K // tk

    def body(*refs):
        b_ref, o_ref, acc_ref = refs[n:]
        i, k = pl.program_id(0), pl.program_id(2)

        @pl.when(k == 0)
        def _():
            acc_ref[...] = jnp.zeros_like(acc_ref)

        for a_ref, c, s in zip(refs[:n], counts, starts):
            @pl.when((i >= s) & (i < s + c))
            def _(a_ref=a_ref):
                acc_ref[...] += _dot(a_ref[...], b_ref[...], "tn")

        @pl.when(k == nk - 1)
        def _():
            o_ref[...] = acc_ref[...].astype(BF16)

    def a_index(s, c, i, j, k):
        mine = (i >= s) & (i < s + c)
        return jnp.where(mine, k, 0), jnp.clip(i - s, 0, c - 1)

    a_specs = [pl.BlockSpec((tk, PIECE_BLOCK), functools.partial(a_index, s, c)) for c, s in zip(counts, starts)]
    b_spec = pl.BlockSpec((tk, tn), lambda i, j, k: (k, j))
    o_spec = pl.BlockSpec((PIECE_BLOCK, tn), lambda i, j, k: (i, j))
    (out,), _ = _call(body, name, (sum(counts), N // tn, nk), a_specs + [b_spec], [o_spec],
                      [jax.ShapeDtypeStruct((sum(counts) * PIECE_BLOCK, N), BF16)],
                      [pltpu.VMEM((PIECE_BLOCK, tn), F32)], ("parallel", "parallel", "arbitrary"), (*pieces, b))
    return out


def _rms_fwd(x, g, name):
    T, D = x.shape
    tr = _pick(T, 256, SUBLANES)

    def body(x_ref, g_ref, o_ref):
        xf = x_ref[...]
        r = lax.rsqrt(jnp.mean(xf * xf, axis=-1, keepdims=True) + RMS_EPS)
        o_ref[...] = (xf * r * g_ref[...]).astype(BF16)

    return pl.pallas_call(
        body, name=name, grid=(T // tr,),
        in_specs=[pl.BlockSpec((tr, D), lambda i: (i, 0)), pl.BlockSpec((1, D), lambda i: (0, 0))],
        out_specs=pl.BlockSpec((tr, D), lambda i: (i, 0)), out_shape=jax.ShapeDtypeStruct((T, D), BF16),
        compiler_params=_params("parallel"))(x, g)


def _rms_bwd(dh, x, g, dres, name):
    T, D = x.shape
    tr = _pick(T, 256, SUBLANES)

    def body(dh_ref, x_ref, g_ref, dres_ref, dx_ref, dg_ref):
        @pl.when(pl.program_id(0) == 0)
        def _():
            dg_ref[...] = jnp.zeros_like(dg_ref)

        xf = x_ref[...]
        r = lax.rsqrt(jnp.mean(xf * xf, axis=-1, keepdims=True) + RMS_EPS)
        xhat = xf * r
        dhv = dh_ref[...]
        dxh = dhv * g_ref[...]
        dx_ref[...] = dres_ref[...] + r * (dxh - xhat * jnp.mean(dxh * xhat, axis=-1, keepdims=True))
        dg_ref[...] += jnp.sum(dhv * xhat, axis=0, keepdims=True)

    row = pl.BlockSpec((tr, D), lambda i: (i, 0))
    vec = pl.BlockSpec((1, D), lambda i: (0, 0))
    return pl.pallas_call(
        body, name=name, grid=(T // tr,), in_specs=[row, row, vec, row], out_specs=[row, vec],
        out_shape=[jax.ShapeDtypeStruct((T, D), F32), jax.ShapeDtypeStruct((1, D), F32)],
        compiler_params=_params("arbitrary"))(dh, x, g, dres)


def _loss_grad(y, target, name):
    T, D = y.shape
    tr = _pick(T, 256, SUBLANES)

    def body(y_ref, t_ref, dx_ref, loss_ref):
        @pl.when(pl.program_id(0) == 0)
        def _():
            loss_ref[...] = jnp.zeros_like(loss_ref)

        err = y_ref[...] - t_ref[...]
        dx_ref[...] = err * (1.0 / D)
        loss_ref[...] += jnp.sum(jnp.mean(err * err, axis=-1, keepdims=True), axis=0, keepdims=True) * 0.5

    row = pl.BlockSpec((tr, D), lambda i: (i, 0))
    one = pl.BlockSpec((1, 1), lambda i: (0, 0))
    dx, loss = pl.pallas_call(
        body, name=name, grid=(T // tr,), in_specs=[row, row], out_specs=[row, one],
        out_shape=[jax.ShapeDtypeStruct((T, D), F32), jax.ShapeDtypeStruct((1, 1), F32)],
        compiler_params=_params("arbitrary"))(y, target)
    return dx, loss[0, 0]


_STACK = GQA_GROUP * BLOCK


def _attn_mask(n):
    row = lax.broadcasted_iota(jnp.int32, (_STACK, 2 * BLOCK), 0)
    col = lax.broadcasted_iota(jnp.int32, (_STACK, 2 * BLOCK), 1)
    dist = (row % BLOCK) - col + BLOCK
    valid = (dist >= 0) & (dist < WINDOW) & ((col >= BLOCK) | (n > 0))
    return dist.astype(F32), valid


def _per_row_head(kv, sk_ref):
    heads = [kv * GQA_GROUP + g for g in range(GQA_GROUP)]
    slope = jnp.concatenate([jnp.full((BLOCK, 1), 2.0 ** (-8.0 * (h + 1) / N_Q_HEADS), F32) for h in heads], axis=0)
    sink = jnp.concatenate([jnp.broadcast_to(sk_ref[h:h + 1, 0:1], (BLOCK, 1)) for h in heads], axis=0)
    return slope, sink


def _head_cols(h):
    return slice(h * HEAD_DIM, (h + 1) * HEAD_DIM)


def _stack_heads(x, kv):
    return jnp.concatenate([x[:, _head_cols(kv * GQA_GROUP + g)] for g in range(GQA_GROUP)], axis=0)


def _head_norm(v, gain):
    r = lax.rsqrt(jnp.mean(v * v, axis=-1, keepdims=True) + RMS_EPS)
    vhat = v * r
    return r, vhat, vhat * gain


def _attn_probs(qn16, kn16, slope, dist, valid, sink):
    s = _dot(qn16, kn16, "nt") * ATTN_SCALE - slope * dist
    s = jnp.where(valid, s, NEG_BIG)
    m = jnp.maximum(jnp.max(s, axis=-1, keepdims=True), sink)
    p = jnp.exp(s - m)
    ps = jnp.exp(sink - m)
    den = jnp.sum(p, axis=-1, keepdims=True) + ps
    return p, ps, den


def _attn_specs():
    wide = pl.BlockSpec((BLOCK, ATTN_WIDTH), lambda n: (n, 0))
    cur = [pl.BlockSpec((BLOCK, KV_WIDTH), functools.partial(lambda c, n: (n, c), off // KV_WIDTH))
           for off in (OFF_K, OFF_V)]
    prev = [pl.BlockSpec((BLOCK, KV_WIDTH), functools.partial(lambda c, n: (jnp.maximum(n - 1, 0), c), off // KV_WIDTH))
            for off in (OFF_K, OFF_V)]
    gain = pl.BlockSpec((1, HEAD_DIM), lambda n: (0, 0))
    sink = pl.BlockSpec((N_Q_HEADS, LANES), lambda n: (0, 0))
    return wide, [cur[0], prev[0], cur[1], prev[1]], gain, sink


def _sink_rows(sinks):
    return jnp.broadcast_to(sinks.reshape(N_Q_HEADS, 1), (N_Q_HEADS, LANES))


def _attn_fwd(q, k, v, qg, kg, sinks, name, comm=None):
    T = q.shape[1]

    def body(q_ref, kc_ref, kp_ref, vc_ref, vp_ref, qg_ref, kg_ref, sk_ref, o_ref):
        dist, valid = _attn_mask(pl.program_id(1))
        row_head = pl.program_id(0) * GQA_GROUP + lax.broadcasted_iota(jnp.int32, (_STACK, 1), 0) // BLOCK
        slope = jnp.exp((row_head + 1).astype(F32) * (-8.0 / N_Q_HEADS * math.log(2.0)))
        first_head = pl.program_id(0) * GQA_GROUP
        sink = jnp.concatenate([jnp.broadcast_to(sk_ref[pl.ds(first_head + g, 1), 0:1], (BLOCK, 1))
                                for g in range(GQA_GROUP)], axis=0)
        kk = jnp.concatenate([kp_ref[...], kc_ref[...]], axis=0)
        _, _, kn = _head_norm(kk, kg_ref[...])
        v16 = jnp.concatenate([vp_ref[...], vc_ref[...]], axis=0).astype(BF16)
        _, _, qn = _head_norm(q_ref[...].reshape(_STACK, HEAD_DIM), qg_ref[...])
        p, _, den = _attn_probs(qn.astype(BF16), kn.astype(BF16), slope, dist, valid, sink)
        o_ref[...] = _dot((p / den).astype(BF16), v16).reshape(GQA_GROUP, BLOCK, HEAD_DIM).astype(BF16)

    q_spec = pl.BlockSpec((GQA_GROUP, BLOCK, HEAD_DIM), lambda h, n: (h, n, 0))
    cur = pl.BlockSpec((None, BLOCK, HEAD_DIM), lambda h, n: (h, n, 0))
    prev = pl.BlockSpec((None, BLOCK, HEAD_DIM), lambda h, n: (h, jnp.maximum(n - 1, 0), 0))
    gain = pl.BlockSpec((1, HEAD_DIM), lambda h, n: (0, 0))
    sink = pl.BlockSpec((N_Q_HEADS, LANES), lambda h, n: (0, 0))
    (out,), extra = _call(
        body, name, (N_KV_HEADS, T // BLOCK), [q_spec, cur, prev, cur, prev, gain, gain, sink], [q_spec],
        [jax.ShapeDtypeStruct((N_Q_HEADS, T, HEAD_DIM), BF16)], [], ("parallel", "parallel"),
        (q, k, k, v, v, qg, kg, _sink_rows(sinks)), comm)
    return out if comm is None else (out, extra)


def _attn_bwd(z, do, qg, kg, sinks, name, comm=None):
    T = z.shape[0]

    def body(q_ref, kc_ref, kp_ref, vc_ref, vp_ref, do_ref, qg_ref, kg_ref, sk_ref,
             dq_ref, dk_ref, dv_ref, dqg_ref, dkg_ref, dsk_ref):
        n = pl.program_id(0)

        @pl.when(n == 0)
        def _():
            for ref in (dk_ref, dv_ref, dqg_ref, dkg_ref, dsk_ref):
                ref[...] = jnp.zeros_like(ref)

        dist, valid = _attn_mask(n)
        rows = pl.ds(pl.multiple_of(n * BLOCK, BLOCK), BLOCK)
        before = pl.ds(pl.multiple_of(jnp.maximum(n - 1, 0) * BLOCK, BLOCK), BLOCK)
        q = q_ref[...]
        do = do_ref[...]
        kk = jnp.concatenate([kp_ref[...], kc_ref[...]], axis=0)
        vv = jnp.concatenate([vp_ref[...], vc_ref[...]], axis=0)
        for kv in range(N_KV_HEADS):
            slope, sink = _per_row_head(kv, sk_ref)
            rk, khat, kn = _head_norm(kk[:, _head_cols(kv)], kg_ref[...])
            kn16 = kn.astype(BF16)
            v16 = vv[:, _head_cols(kv)].astype(BF16)
            rq, qhat, qn = _head_norm(_stack_heads(q, kv), qg_ref[...])
            qn16 = qn.astype(BF16)
            p, ps, den = _attn_probs(qn16, kn16, slope, dist, valid, sink)
            pn = p / den
            do16 = _stack_heads(do, kv).astype(BF16)
            dp = _dot(do16, v16, "nt")
            delta = jnp.sum(pn * dp, axis=-1, keepdims=True)
            ds16 = (pn * (dp - delta)).astype(BF16)
            sink_pull = ps / den * delta
            dqn = _dot(ds16, kn16) * ATTN_SCALE
            dkn = _dot(ds16, qn16, "tn") * ATTN_SCALE
            dv = _dot(pn.astype(BF16), do16, "tn")
            dqh = dqn * qg_ref[...]
            dq = rq * (dqh - qhat * jnp.mean(dqh * qhat, axis=-1, keepdims=True))
            for g in range(GQA_GROUP):
                h = kv * GQA_GROUP + g
                dsink = jnp.sum(sink_pull[g * BLOCK:(g + 1) * BLOCK], axis=0, keepdims=True)
                dsk_ref[h:h + 1, :] -= jnp.broadcast_to(dsink, (1, LANES))
                dq_ref[:, _head_cols(h)] = dq[g * BLOCK:(g + 1) * BLOCK].astype(BF16)
            dqg_ref[...] += jnp.sum(dqn * qhat, axis=0, keepdims=True)
            dkg_ref[...] += jnp.sum(dkn * khat, axis=0, keepdims=True)
            dkh = dkn * kg_ref[...]
            dk = rk * (dkh - khat * jnp.mean(dkh * khat, axis=-1, keepdims=True))
            dk_ref[rows, _head_cols(kv)] += dk[BLOCK:]
            dv_ref[rows, _head_cols(kv)] += dv[BLOCK:]
            dk_ref[before, _head_cols(kv)] += dk[:BLOCK]
            dv_ref[before, _head_cols(kv)] += dv[:BLOCK]

    wide, kv_specs, gain, sink = _attn_specs()
    whole = pl.BlockSpec((T, KV_WIDTH), lambda n: (0, 0))
    outs, extra = _call(
        body, name, (T // BLOCK,), [wide] + kv_specs + [wide, gain, gain, sink],
        [wide, whole, whole, gain, gain, sink],
        [jax.ShapeDtypeStruct((T, ATTN_WIDTH), BF16), jax.ShapeDtypeStruct((T, KV_WIDTH), F32),
         jax.ShapeDtypeStruct((T, KV_WIDTH), F32), jax.ShapeDtypeStruct((1, HEAD_DIM), F32),
         jax.ShapeDtypeStruct((1, HEAD_DIM), F32), jax.ShapeDtypeStruct((N_Q_HEADS, LANES), F32)],
        [], ("arbitrary",), (z, z, z, z, z, do, qg, kg, _sink_rows(sinks)), comm)
    return outs if comm is None else (outs, extra)


def _ssm_discretize(lam_re, lam_im, log_dt, b_re, b_im, c_re, c_im):
    dt = jnp.exp(log_dt)[:, None]
    mag = jnp.exp(lam_re * dt)
    ar = mag * jnp.cos(lam_im * dt)
    ai = mag * jnp.sin(lam_im * dt)
    den = lam_re * lam_re + lam_im * lam_im
    fr = ((ar - 1.0) * lam_re + ai * lam_im) / den
    fi = (ai * lam_re - (ar - 1.0) * lam_im) / den
    bbar_r = fr[:, :, None] * b_re - fi[:, :, None] * b_im
    bbar_i = fr[:, :, None] * b_im + fi[:, :, None] * b_re
    gl = SSM_GROUPS // SSM_NGB
    eye = jnp.eye(gl, dtype=F32)

    def tiles(a):
        return a.reshape(SSM_NGB, SUBLANES, LANES)

    def bdiag(bb):
        return jnp.einsum("bgph,gk->bghkp", bb.reshape(SSM_NGB, gl, SSM_STATE, SSM_GROUP_CH), eye).reshape(
            SSM_NGB, SSM_GB_CH, SSM_GB_ST)

    def cdiag(cc):
        return jnp.einsum("bghp,gk->bgpkh", cc.reshape(SSM_NGB, gl, SSM_GROUP_CH, SSM_STATE), eye).reshape(
            SSM_NGB, SSM_GB_ST, SSM_GB_CH)

    return tiles(ar), tiles(ai), bdiag(bbar_r), bdiag(bbar_i), cdiag(c_re), cdiag(c_im)


def _to_time_major(dst, val, tt, first_row=0):
    for j in range(SUBLANES):
        dst[pl.ds(first_row + j, tt, stride=SUBLANES), :] = val[:, j * LANES:(j + 1) * LANES]


def _from_time_major(dst, src, tt):
    for j in range(SUBLANES):
        dst[:, j * LANES:(j + 1) * LANES] = src[pl.ds(j, tt, stride=SUBLANES), :]


def _ssm_fwd(z, ar, ai, bbr, bbi, cbr, cbi, dskip, name, comm=None):
    T = z.shape[0]
    tt = min(SSM_TT, T)
    nt = T // tt

    def body(u_ref, ar_ref, ai_ref, br_ref, bi_ref, cr_ref, ci_ref, d_ref, y_ref, sr_ref, si_ref,
             tmr, tmi, car_r, car_i):
        @pl.when(pl.program_id(1) == 0)
        def _():
            car_r[...] = jnp.zeros_like(car_r)
            car_i[...] = jnp.zeros_like(car_i)

        u = u_ref[...]
        u16 = u.astype(BF16)
        _to_time_major(tmr, _dot(u16, br_ref[...]), tt)
        _to_time_major(tmi, _dot(u16, bi_ref[...]), tt)
        a_r = ar_ref[...]
        a_i = ai_ref[...]

        def step(t, carry):
            s_r, s_i = carry
            rows = pl.ds(pl.multiple_of(t * SUBLANES, SUBLANES), SUBLANES)
            n_r = a_r * s_r - a_i * s_i + tmr[rows, :]
            n_i = a_r * s_i + a_i * s_r + tmi[rows, :]
            tmr[rows, :] = n_r
            tmi[rows, :] = n_i
            return n_r, n_i

        s_r, s_i = lax.fori_loop(0, tt, step, (car_r[...], car_i[...]), unroll=8)
        car_r[...] = s_r
        car_i[...] = s_i
        _from_time_major(sr_ref, tmr, tt)
        _from_time_major(si_ref, tmi, tt)
        y_ref[...] = (_dot(sr_ref[...].astype(BF16), cr_ref[...]) - _dot(si_ref[...].astype(BF16), ci_ref[...])
                      + d_ref[...] * u)

    u_spec = pl.BlockSpec((tt, SSM_GB_CH), lambda b, t: (t, OFF_U // SSM_GB_CH + b))
    a_spec = pl.BlockSpec((None, SUBLANES, LANES), lambda b, t: (b, 0, 0))
    b_spec = pl.BlockSpec((None, SSM_GB_CH, SSM_GB_ST), lambda b, t: (b, 0, 0))
    c_spec = pl.BlockSpec((None, SSM_GB_ST, SSM_GB_CH), lambda b, t: (b, 0, 0))
    d_spec = pl.BlockSpec((1, SSM_GB_CH), lambda b, t: (0, b))
    y_spec = pl.BlockSpec((tt, SSM_GB_CH), lambda b, t: (t, b))
    s_spec = pl.BlockSpec((tt, SSM_GB_ST), lambda b, t: (t, b))
    n_state = SSM_NGB * SSM_GB_ST
    outs, extra = _call(
        body, name, (SSM_NGB, nt), [u_spec, a_spec, a_spec, b_spec, b_spec, c_spec, c_spec, d_spec],
        [y_spec, s_spec, s_spec],
        [jax.ShapeDtypeStruct((T, SSM_WIDTH), F32), jax.ShapeDtypeStruct((T, n_state), F32),
         jax.ShapeDtypeStruct((T, n_state), F32)],
        [pltpu.VMEM((tt * SUBLANES, LANES), F32), pltpu.VMEM((tt * SUBLANES, LANES), F32),
         pltpu.VMEM((SUBLANES, LANES), F32), pltpu.VMEM((SUBLANES, LANES), F32)],
        ("parallel", "arbitrary"), (z, ar, ai, bbr, bbi, cbr, cbi, dskip), comm)
    return outs if comm is None else (outs, extra)


def _ssm_bwd(dy, z, s_r, s_i, ar, ai, bbr, bbi, cbr, cbi, dskip, name, comm=None):
    T = z.shape[0]
    tt = min(SSM_TT, T)
    nt = T // tt
    per8 = tt // SUBLANES

    def body(dy_ref, u_ref, sr_ref, si_ref, srp_ref, sip_ref, ar_ref, ai_ref, br_ref, bi_ref, cr_ref, ci_ref, d_ref,
             du_ref, dar_ref, dai_ref, dbr_ref, dbi_ref, dcr_ref, dci_ref, dd_ref,
             tmr, tmi, smr, smi, natr, nati, car_r, car_i):
        tb = pl.program_id(1)
        first_block = tb == nt - 1

        @pl.when(tb == 0)
        def _():
            for ref in (car_r, car_i, dar_ref, dai_ref, dbr_ref, dbi_ref, dcr_ref, dci_ref, dd_ref):
                ref[...] = jnp.zeros_like(ref)

        dy = dy_ref[...]
        dy16 = dy.astype(BF16)
        u = u_ref[...]
        u16 = u.astype(BF16)
        _to_time_major(tmr, _dot(dy16, cr_ref[...], "nt"), tt)
        _to_time_major(tmi, -_dot(dy16, ci_ref[...], "nt"), tt)
        _to_time_major(smr, sr_ref[...], tt, first_row=SUBLANES)
        _to_time_major(smi, si_ref[...], tt, first_row=SUBLANES)
        keep = jnp.where(first_block, 0.0, 1.0)
        for j in range(SUBLANES):
            smr[j:j + 1, :] = srp_ref[SUBLANES - 1:SUBLANES, j * LANES:(j + 1) * LANES] * keep
            smi[j:j + 1, :] = sip_ref[SUBLANES - 1:SUBLANES, j * LANES:(j + 1) * LANES] * keep
        a_r = ar_ref[...]
        a_i = ai_ref[...]

        def step(i, carry):
            n_r, n_i, da_r, da_i = carry
            rows = pl.ds(pl.multiple_of((tt - 1 - i) * SUBLANES, SUBLANES), SUBLANES)
            g_r = tmr[rows, :] + a_r * n_r + a_i * n_i
            g_i = tmi[rows, :] - a_i * n_r + a_r * n_i
            tmr[rows, :] = g_r
            tmi[rows, :] = g_i
            p_r = smr[rows, :]
            p_i = smi[rows, :]
            return g_r, g_i, da_r + g_r * p_r + g_i * p_i, da_i - g_r * p_i + g_i * p_r

        zero = jnp.zeros((SUBLANES, LANES), F32)
        n_r, n_i, da_r, da_i = lax.fori_loop(0, tt, step, (car_r[...], car_i[...], zero, zero), unroll=8)
        car_r[...] = n_r
        car_i[...] = n_i
        dar_ref[...] += da_r
        dai_ref[...] += da_i
        _from_time_major(natr, tmr, tt)
        _from_time_major(nati, tmi, tt)
        dbu_r16 = natr[...].astype(BF16)
        dbu_i16 = nati[...].astype(BF16)
        du_ref[...] = (_dot(dbu_r16, br_ref[...], "nt") + _dot(dbu_i16, bi_ref[...], "nt")
                       + d_ref[...] * dy).astype(BF16)
        dbr_ref[...] += _dot(u16, dbu_r16, "tn")
        dbi_ref[...] += _dot(u16, dbu_i16, "tn")
        dcr_ref[...] += _dot(sr_ref[...].astype(BF16), dy16, "tn")
        dci_ref[...] -= _dot(si_ref[...].astype(BF16), dy16, "tn")
        dd_ref[...] += jnp.sum(dy * u, axis=0, keepdims=True)

    def rev(t):
        return nt - 1 - t

    dy_spec = pl.BlockSpec((tt, SSM_GB_CH), lambda b, t: (rev(t), b))
    u_spec = pl.BlockSpec((tt, SSM_GB_CH), lambda b, t: (rev(t), OFF_U // SSM_GB_CH + b))
    s_spec = pl.BlockSpec((tt, SSM_GB_ST), lambda b, t: (rev(t), b))
    sp_spec = pl.BlockSpec((SUBLANES, SSM_GB_ST), lambda b, t: (jnp.maximum(rev(t) * per8 - 1, 0), b))
    a_spec = pl.BlockSpec((None, SUBLANES, LANES), lambda b, t: (b, 0, 0))
    b_spec = pl.BlockSpec((None, SSM_GB_CH, SSM_GB_ST), lambda b, t: (b, 0, 0))
    c_spec = pl.BlockSpec((None, SSM_GB_ST, SSM_GB_CH), lambda b, t: (b, 0, 0))
    d_spec = pl.BlockSpec((1, SSM_GB_CH), lambda b, t: (0, b))
    tm_shape = pltpu.VMEM((tt * SUBLANES, LANES), F32)
    sm_shape = pltpu.VMEM(((tt + 1) * SUBLANES, LANES), F32)
    nat_shape = pltpu.VMEM((tt, SSM_GB_ST), F32)
    tile = pltpu.VMEM((SUBLANES, LANES), F32)
    outs, extra = _call(
        body, name, (SSM_NGB, nt),
        [dy_spec, u_spec, s_spec, s_spec, sp_spec, sp_spec, a_spec, a_spec, b_spec, b_spec, c_spec, c_spec, d_spec],
        [dy_spec, a_spec, a_spec, b_spec, b_spec, c_spec, c_spec, d_spec],
        [jax.ShapeDtypeStruct((T, SSM_WIDTH), BF16),
         jax.ShapeDtypeStruct((SSM_NGB, SUBLANES, LANES), F32),
         jax.ShapeDtypeStruct((SSM_NGB, SUBLANES, LANES), F32),
         jax.ShapeDtypeStruct((SSM_NGB, SSM_GB_CH, SSM_GB_ST), F32),
         jax.ShapeDtypeStruct((SSM_NGB, SSM_GB_CH, SSM_GB_ST), F32),
         jax.ShapeDtypeStruct((SSM_NGB, SSM_GB_ST, SSM_GB_CH), F32),
         jax.ShapeDtypeStruct((SSM_NGB, SSM_GB_ST, SSM_GB_CH), F32),
         jax.ShapeDtypeStruct((1, SSM_WIDTH), F32)],
        [tm_shape, tm_shape, sm_shape, sm_shape, nat_shape, nat_shape, tile, tile], ("parallel", "arbitrary"),
        (dy, z, s_r, s_i, s_r, s_i, ar, ai, bbr, bbi, cbr, cbi, dskip), comm)
    return outs if comm is None else (outs, extra)


def _glu_fwd(y, w, b, name):
    T, W = y.shape
    tm = _pick(T, 512)

    def body(y_ref, w_ref, b_ref, pre_ref, y3_ref):
        y2 = _gelu(y_ref[...])
        pre = _dot(y2.astype(BF16), w_ref[...]) + b_ref[...]
        pre_ref[...] = pre
        y3_ref[...] = (y2 * _sigmoid(pre)).astype(BF16)

    row = pl.BlockSpec((tm, W), lambda i: (i, 0))
    return pl.pallas_call(
        body, name=name, grid=(T // tm,),
        in_specs=[row, pl.BlockSpec((W, W), lambda i: (0, 0)), pl.BlockSpec((1, W), lambda i: (0, 0))],
        out_specs=[row, row], out_shape=[jax.ShapeDtypeStruct((T, W), F32), jax.ShapeDtypeStruct((T, W), BF16)],
        compiler_params=_params("parallel"))(y, w, b)


def _glu_bwd_gate(dy3, y, pre, name):
    T, W = y.shape
    tm = _pick(T, 512)

    def body(dy3_ref, y_ref, pre_ref, dpre_ref, t1_ref, y2_ref, db_ref):
        @pl.when(pl.program_id(0) == 0)
        def _():
            db_ref[...] = jnp.zeros_like(db_ref)

        y2 = _gelu(y_ref[...])
        sg = _sigmoid(pre_ref[...])
        dy3 = dy3_ref[...]
        dpre = dy3 * y2 * sg * (1.0 - sg)
        dpre_ref[...] = dpre.astype(BF16)
        t1_ref[...] = dy3 * sg
        y2_ref[...] = y2.astype(BF16)
        db_ref[...] += jnp.sum(dpre, axis=0, keepdims=True)

    row = pl.BlockSpec((tm, W), lambda i: (i, 0))
    vec = pl.BlockSpec((1, W), lambda i: (0, 0))
    return pl.pallas_call(
        body, name=name, grid=(T // tm,), in_specs=[row, row, row], out_specs=[row, row, row, vec],
        out_shape=[jax.ShapeDtypeStruct((T, W), BF16), jax.ShapeDtypeStruct((T, W), F32),
                   jax.ShapeDtypeStruct((T, W), BF16), jax.ShapeDtypeStruct((1, W), F32)],
        compiler_params=_params("arbitrary"))(dy3, y, pre)


def _glu_bwd_in(dpre, w, t1, y, name):
    T, W = y.shape
    tm = _pick(T, 512)

    def body(dpre_ref, w_ref, t1_ref, y_ref, dy_ref):
        dy_ref[...] = (_dot(dpre_ref[...], w_ref[...], "nt") + t1_ref[...]) * _gelu_grad(y_ref[...])

    row = pl.BlockSpec((tm, W), lambda i: (i, 0))
    return pl.pallas_call(
        body, name=name, grid=(T // tm,), in_specs=[row, pl.BlockSpec((W, W), lambda i: (0, 0)), row, row],
        out_specs=row, out_shape=jax.ShapeDtypeStruct((T, W), F32),
        compiler_params=_params("parallel"))(dpre, w, t1, y)


def _merge_fwd(ya, y3, wa, ws, z, bias, name, comm=None):
    T, W = ya.shape
    D = wa.shape[1]
    tm, tn = _pick(T, 1024), _pick(D, 512)

    def body(ya_ref, y3_ref, wa_ref, ws_ref, za_ref, zs_ref, ba_ref, bs_ref, a_ref, b_ref, m_ref):
        a = _dot(ya_ref[...], wa_ref[...])
        b = _dot(y3_ref[...], ws_ref[...])
        a_ref[...] = a
        b_ref[...] = b
        m_ref[...] = (_sigmoid(za_ref[...] + ba_ref[...]) * a + _sigmoid(zs_ref[...] + bs_ref[...]) * b).astype(BF16)

    act = pl.BlockSpec((tm, W), lambda i, j: (i, 0))
    wgt = pl.BlockSpec((W, tn), lambda i, j: (0, j))
    za = pl.BlockSpec((tm, tn), lambda i, j: (i, OFF_G // tn + j))
    zs = pl.BlockSpec((tm, tn), lambda i, j: (i, (OFF_G + D) // tn + j))
    ba = pl.BlockSpec((1, tn), lambda i, j: (0, j))
    bs = pl.BlockSpec((1, tn), lambda i, j: (0, D // tn + j))
    out = pl.BlockSpec((tm, tn), lambda i, j: (i, j))
    outs, extra = _call(
        body, name, (T // tm, D // tn), [act, act, wgt, wgt, za, zs, ba, bs], [out, out, out],
        [jax.ShapeDtypeStruct((T, D), F32), jax.ShapeDtypeStruct((T, D), F32), jax.ShapeDtypeStruct((T, D), BF16)],
        [], ("parallel", "parallel"), (ya, y3, wa, ws, z, z, bias, bias), comm)
    return outs if comm is None else (outs, extra)


def _merge_bwd(dm, a, b, z, bias, name):
    T, D = dm.shape
    tm, tn = _pick(T, 512), _pick(D, 512)

    def body(dm_ref, a_ref, b_ref, za_ref, zs_ref, ba_ref, bs_ref, da_ref, db_ref, dza_ref, dzs_ref, dba_ref, dbs_ref):
        @pl.when(pl.program_id(1) == 0)
        def _():
            dba_ref[...] = jnp.zeros_like(dba_ref)
            dbs_ref[...] = jnp.zeros_like(dbs_ref)

        dm = dm_ref[...]
        sa = _sigmoid(za_ref[...] + ba_ref[...])
        ss = _sigmoid(zs_ref[...] + bs_ref[...])
        da_ref[...] = (dm * sa).astype(BF16)
        db_ref[...] = (dm * ss).astype(BF16)
        dza = dm * a_ref[...] * sa * (1.0 - sa)
        dzs = dm * b_ref[...] * ss * (1.0 - ss)
        dza_ref[...] = dza.astype(BF16)
        dzs_ref[...] = dzs.astype(BF16)
        dba_ref[...] += jnp.sum(dza, axis=0, keepdims=True)
        dbs_ref[...] += jnp.sum(dzs, axis=0, keepdims=True)

    blk = pl.BlockSpec((tm, tn), lambda j, i: (i, j))
    za = pl.BlockSpec((tm, tn), lambda j, i: (i, OFF_G // tn + j))
    zs = pl.BlockSpec((tm, tn), lambda j, i: (i, (OFF_G + D) // tn + j))
    ba = pl.BlockSpec((1, tn), lambda j, i: (0, j))
    bs = pl.BlockSpec((1, tn), lambda j, i: (0, D // tn + j))
    big = jax.ShapeDtypeStruct((T, D), BF16)
    vec = jax.ShapeDtypeStruct((1, D), F32)
    return pl.pallas_call(
        body, name=name, grid=(D // tn, T // tm), in_specs=[blk, blk, blk, za, zs, ba, bs],
        out_specs=[blk, blk, blk, blk, ba, ba], out_shape=[big, big, big, big, vec, vec],
        compiler_params=_params("parallel", "arbitrary"))(dm, a, b, z, z, bias, bias)


_HALF = N_DEV // 2


def _wgu_block(d):
    return d // 2, d % 2


def _ffn_fwd(h, wgu, name, comm=None):
    T, D = h.shape
    n = wgu.shape[3]
    F = _HALF * n
    tm = _pick(T, 512)

    def body(h_ref, wg_ref, wu_ref, g_ref, u_ref, act_ref):
        hv = h_ref[...]
        g = _dot(hv, wg_ref[...])
        u = _dot(hv, wu_ref[...])
        g_ref[...] = g
        u_ref[...] = u
        act_ref[...] = (g * _sigmoid(g) * u).astype(BF16)

    wg = pl.BlockSpec((None, None, D, n), lambda j, i: (*_wgu_block(j), 0, 0))
    wu = pl.BlockSpec((None, None, D, n), lambda j, i: (*_wgu_block(j + _HALF), 0, 0))
    out = pl.BlockSpec((tm, n), lambda j, i: (i, j))
    outs, extra = _call(
        body, name, (_HALF, T // tm), [pl.BlockSpec((tm, D), lambda j, i: (i, 0)), wg, wu], [out, out, out],
        [jax.ShapeDtypeStruct((T, F), F32), jax.ShapeDtypeStruct((T, F), F32), jax.ShapeDtypeStruct((T, F), BF16)],
        [], ("parallel", "parallel"), (h, wgu, wgu), comm)
    return outs if comm is None else (outs, extra)


def _ffn_bwd_in(dg, du, wgu, name, comm=None):
    T, F = dg.shape
    D, n = wgu.shape[2], wgu.shape[3]
    tm, tn = _pick(T, 1024), _pick(D, 2048)

    def body(dg_ref, du_ref, w_ref, o_ref, acc_ref):
        k = pl.program_id(2)

        @pl.when(k == 0)
        def _():
            acc_ref[...] = jnp.zeros_like(acc_ref)

        @pl.when(k < _HALF)
        def _():
            acc_ref[...] += _dot(dg_ref[...], w_ref[...], "nt")

        @pl.when(k >= _HALF)
        def _():
            acc_ref[...] += _dot(du_ref[...], w_ref[...], "nt")

        @pl.when(k == N_DEV - 1)
        def _():
            o_ref[...] = acc_ref[...]

    dg_spec = pl.BlockSpec((tm, n), lambda i, j, k: (i, jnp.minimum(k, _HALF - 1)))
    du_spec = pl.BlockSpec((tm, n), lambda i, j, k: (i, jnp.maximum(k - _HALF, 0)))
    w_spec = pl.BlockSpec((None, None, tn, n), lambda i, j, k: (*_wgu_block(k), j, 0))
    o_spec = pl.BlockSpec((tm, tn), lambda i, j, k: (i, j))
    (out,), extra = _call(
        body, name, (T // tm, D // tn, N_DEV), [dg_spec, du_spec, w_spec], [o_spec],
        [jax.ShapeDtypeStruct((T, D), F32)], [pltpu.VMEM((tm, tn), F32)], ("parallel", "parallel", "arbitrary"),
        (dg, du, wgu), comm)
    return out if comm is None else (out, extra)


def _ffn_bwd_w(h, dg, du, name, comm=None):
    T, D = h.shape
    n = dg.shape[1] // _HALF
    tm, tk = _pick(D, 1024), _pick(T, 2048)
    nk = T // tk

    def body(h_ref, dg_ref, du_ref, o_ref, acc_ref):
        j, k = pl.program_id(0), pl.program_id(2)

        @pl.when(k == 0)
        def _():
            acc_ref[...] = jnp.zeros_like(acc_ref)

        @pl.when(j < _HALF)
        def _():
            acc_ref[...] += _dot(h_ref[...], dg_ref[...], "tn")

        @pl.when(j >= _HALF)
        def _():
            acc_ref[...] += _dot(h_ref[...], du_ref[...], "tn")

        @pl.when(k == nk - 1)
        def _():
            o_ref[...] = acc_ref[...].astype(BF16)

    h_spec = pl.BlockSpec((tk, tm), lambda j, i, k: (k, i))
    dg_spec = pl.BlockSpec((tk, n), lambda j, i, k: (jnp.where(j < _HALF, k, nk - 1), jnp.minimum(j, _HALF - 1)))
    du_spec = pl.BlockSpec((tk, n), lambda j, i, k: (jnp.where(j >= _HALF, k, 0), jnp.maximum(j - _HALF, 0)))
    o_spec = pl.BlockSpec((None, None, tm, n), lambda j, i, k: (*_wgu_block(j), i, 0))
    (out,), extra = _call(
        body, name, (N_DEV, D // tm, nk), [h_spec, dg_spec, du_spec], [o_spec],
        [jax.ShapeDtypeStruct((_HALF, 2, D, n), BF16)], [pltpu.VMEM((tm, n), F32)],
        ("parallel", "parallel", "arbitrary"), (h, dg, du), comm)
    return out if comm is None else (out, extra)


def _ffn_bwd_act(dx, wo, g, u, name, comm=None):
    T, D = dx.shape
    F = wo.shape[0]
    tm, tn = _pick(T, 512), _pick(F, 1408)

    def body(dx_ref, wo_ref, g_ref, u_ref, dg_ref, du_ref):
        dact = _dot(dx_ref[...].astype(BF16), wo_ref[...], "nt")
        gv = g_ref[...]
        sg = _sigmoid(gv)
        dg_ref[...] = (dact * u_ref[...] * sg * (1.0 + gv * (1.0 - sg))).astype(BF16)
        du_ref[...] = (dact * gv * sg).astype(BF16)

    out = pl.BlockSpec((tm, tn), lambda i, j: (i, j))
    big = jax.ShapeDtypeStruct((T, F), BF16)
    outs, extra = _call(
        body, name, (T // tm, F // tn),
        [pl.BlockSpec((tm, D), lambda i, j: (i, 0)), pl.BlockSpec((tn, D), lambda i, j: (j, 0)), out, out],
        [out, out], [big, big], [], ("parallel", "parallel"), (dx, wo, g, u), comm)
    return outs if comm is None else (outs, extra)


def _place():
    x, y, c = lax.axis_index("x"), lax.axis_index("y"), lax.axis_index("c")
    other_chips = [(1 - x, y), (x, 1 - y), (1 - x, 1 - y)]
    return x, y, c, 2 * x + y, other_chips


_ANY = pl.BlockSpec(memory_space=pl.ANY)
_N_COPIES = 7


def _all_gather(shards, name):
    n = len(shards)

    def body(*refs):
        ins, outs = refs[:n], refs[n:2 * n]
        send_sems, recv_sems, local_sems = refs[2 * n:]
        x, y, c, chip, other_chips = _place()
        sibling = (x, y, 1 - c)

        def remote(src, dst, a, j, dev):
            return pltpu.make_async_remote_copy(src_ref=src, dst_ref=dst, send_sem=send_sems.at[a, j],
                                                recv_sem=recv_sems.at[a, j], device_id=dev, device_id_type=MESH)

        sends, local = [], []
        for a in range(n):
            mine = outs[a].at[chip, c]
            local.append(pltpu.make_async_copy(ins[a], mine, local_sems.at[a]))
            local[a].start()
            for j, (ox, oy) in enumerate(other_chips):
                sends.append(remote(ins[a], mine, a, 1 + j, (ox, oy, c)))
                sends[-1].start()
            sends.append(remote(ins[a], mine, a, 0, sibling))
            sends[-1].start()
        for a in range(n):
            for j, (ox, oy) in enumerate(other_chips):
                slot = outs[a].at[2 * ox + oy, c]
                remote(ins[a], slot, a, 1 + j, (ox, oy, c)).wait_recv()
                sends.append(remote(slot, slot, a, 4 + j, sibling))
                sends[-1].start()
        for a in range(n):
            remote(ins[a], outs[a].at[chip, 1 - c], a, 0, sibling).wait_recv()
            for j, (ox, oy) in enumerate(other_chips):
                remote(ins[a], outs[a].at[2 * ox + oy, 1 - c], a, 4 + j, sibling).wait_recv()
        for cp in sends:
            cp.wait_send()
        for a in range(n):
            local[a].wait()

    return pl.pallas_call(
        body, name=name, in_specs=[_ANY] * n, out_specs=[_ANY] * n,
        out_shape=[jax.ShapeDtypeStruct((4, 2) + s.shape, s.dtype) for s in shards],
        scratch_shapes=[pltpu.SemaphoreType.DMA((n, _N_COPIES)), pltpu.SemaphoreType.DMA((n, _N_COPIES)),
                        pltpu.SemaphoreType.DMA((n,))])(*shards)


def _pair_exchange(parts, name):
    n = len(parts)

    def body(*refs):
        ins, theirs = refs[:n], refs[n:2 * n]
        send_sems, recv_sems = refs[2 * n:]
        x, y, c, _, _ = _place()
        sends = []
        for a in range(n):
            for k in range(4):
                sends.append(pltpu.make_async_remote_copy(
                    src_ref=ins[a].at[k, 1 - c], dst_ref=theirs[a].at[k], send_sem=send_sems.at[a, k],
                    recv_sem=recv_sems.at[a, k], device_id=(x, y, 1 - c), device_id_type=MESH))
                sends[-1].start()
        for cp in sends:
            cp.wait_recv()
            cp.wait_send()

    return pl.pallas_call(
        body, name=name, in_specs=[_ANY] * n, out_specs=[_ANY] * n,
        out_shape=[jax.ShapeDtypeStruct((4,) + p.shape[2:], p.dtype) for p in parts],
        scratch_shapes=[pltpu.SemaphoreType.DMA((n, 4)), pltpu.SemaphoreType.DMA((n, 4))])(*parts)


def _chip_exchange(parts, name):
    n = len(parts)

    def body(*refs):
        ins, got = refs[:n], refs[n:2 * n]
        send_sems, recv_sems = refs[2 * n:]
        _, _, c, _, other_chips = _place()
        sends = []
        for a in range(n):
            for j, (ox, oy) in enumerate(other_chips):
                sends.append(pltpu.make_async_remote_copy(
                    src_ref=ins[a].at[2 * ox + oy], dst_ref=got[a].at[j], send_sem=send_sems.at[a, j],
                    recv_sem=recv_sems.at[a, j], device_id=(ox, oy, c), device_id_type=MESH))
                sends[-1].start()
        for cp in sends:
            cp.wait_recv()
            cp.wait_send()

    return pl.pallas_call(
        body, name=name, in_specs=[_ANY] * n, out_specs=[_ANY] * n,
        out_shape=[jax.ShapeDtypeStruct((3,) + p.shape[1:], p.dtype) for p in parts],
        scratch_shapes=[pltpu.SemaphoreType.DMA((n, 3)), pltpu.SemaphoreType.DMA((n, 3))])(*parts)


def _remote(src, dst, send_sems, recv_sems, a, j, dev):
    return pltpu.make_async_remote_copy(src_ref=src, dst_ref=dst, send_sem=send_sems.at[a, j],
                                        recv_sem=recv_sems.at[a, j], device_id=dev, device_id_type=MESH)


def _gather_send(shards):
    n = len(shards)

    def copies(cin, cout, sems, arriving):
        send_sems, recv_sems, local_sems = sems
        x, y, c, chip, other_chips = _place()
        sibling = (x, y, 1 - c)
        peers = [(0, sibling, (chip, 1 - c))] + [(1 + j, (ox, oy, c), (2 * ox + oy, c))
                                                 for j, (ox, oy) in enumerate(other_chips)]
        sends, recvs, local = [], [], []
        for a in range(n):
            mine = cout[a].at[chip, c]
            local.append(pltpu.make_async_copy(cin[a], mine, local_sems.at[a]))
            for j, dev, slot in peers:
                sends.append(_remote(cin[a], mine, send_sems, recv_sems, a, j, dev))
                if arriving:
                    recvs.append(_remote(cin[a], cout[a].at[slot], send_sems, recv_sems, a, j, dev))
        return sends, recvs, local

    return _Comm(shards, [jax.ShapeDtypeStruct((4, 2) + s.shape, s.dtype) for s in shards],
                 [pltpu.SemaphoreType.DMA((n, 4)), pltpu.SemaphoreType.DMA((n, 4)), pltpu.SemaphoreType.DMA((n,))],
                 copies)


def _gather_pass(gathered):
    n = len(gathered)

    def copies(cin, cout, sems, arriving):
        send_sems, recv_sems = sems
        x, y, c, _, other_chips = _place()
        sibling = (x, y, 1 - c)
        sends, recvs = [], []
        for a in range(n):
            for j, (ox, oy) in enumerate(other_chips):
                k = 2 * ox + oy
                sends.append(_remote(cout[a].at[k, c], cout[a].at[k, c], send_sems, recv_sems, a, j, sibling))
                if arriving:
                    recvs.append(_remote(cout[a].at[k, c], cout[a].at[k, 1 - c], send_sems, recv_sems, a, j, sibling))
        return sends, recvs, []

    return _Comm(gathered, [jax.ShapeDtypeStruct(g.shape, g.dtype) for g in gathered],
                 [pltpu.SemaphoreType.DMA((n, 3)), pltpu.SemaphoreType.DMA((n, 3))], copies,
                 aliases={i: i for i in range(n)})


def _scatter_send(sums):
    n = len(sums)

    def copies(cin, cout, sems, arriving):
        send_sems, recv_sems = sems
        _, _, c, _, other_chips = _place()
        sends = [_remote(cin[a].at[2 * ox + oy], cout[a].at[j], send_sems, recv_sems, a, j, (ox, oy, c))
                 for a in range(n) for j, (ox, oy) in enumerate(other_chips)]
        return sends, sends, []

    return _Comm(sums, [jax.ShapeDtypeStruct((3,) + s.shape[1:], s.dtype) for s in sums],
                 [pltpu.SemaphoreType.DMA((n, 3)), pltpu.SemaphoreType.DMA((n, 3))], copies)


def _join(comms):
    if len(comms) == 1:
        return comms[0]
    args, outs, sems, aliases, spans = [], [], [], {}, []
    for cm in comms:
        spans.append((len(args), len(outs), len(sems)))
        aliases.update({len(args) + i: len(outs) + o for i, o in cm.aliases.items()})
        args, outs, sems = args + cm.args, outs + cm.out_shapes, sems + cm.sems

    def copies(cin, cout, sem_refs, arriving):
        sends, recvs, local = [], [], []
        for cm, (a0, o0, s0) in zip(comms, spans):
            part = cm.copies(cin[a0:a0 + len(cm.args)], cout[o0:o0 + len(cm.out_shapes)],
                             sem_refs[s0:s0 + len(cm.sems)], arriving)
            sends, recvs, local = sends + part[0], recvs + part[1], local + part[2]
        return sends, recvs, local

    return _Comm(args, outs, sems, copies, aliases)


class _Schedule:
    def __init__(self):
        self.rides = {}

    def ride(self, host, make, names, operands):
        results = {}
        self.rides.setdefault(host, []).append((make, names, operands, results))
        return results

    def carry(self, host, fn, *args, **kwargs):
        rides = self.rides.get(host)
        if not rides:
            return fn(*args, host, **kwargs)
        comm = _join([make([operands[n] for n in names]) for make, names, operands, _ in rides])
        out, extra = fn(*args, host, comm=comm, **kwargs)
        for _, names, _, results in rides:
            results.update(zip(names, extra[:len(names)]))
            extra = extra[len(names):]
        return out


def _add_pair(core, parts, theirs, name):
    k, _, R, C = parts.shape
    tr = _pick(R, 512, 16)

    def body(core_ref, a_ref, b_ref, o_ref):
        o_ref[...] = (a_ref[...].astype(F32) + b_ref[...].astype(F32)).astype(BF16)

    blk = pl.BlockSpec((None, tr, C), lambda s, i, core_ref: (s, i, 0))
    grid_spec = pltpu.PrefetchScalarGridSpec(
        num_scalar_prefetch=1, grid=(k, R // tr),
        in_specs=[pl.BlockSpec((None, None, tr, C), lambda s, i, core_ref: (s, core_ref[0], i, 0)), blk],
        out_specs=blk)
    return pl.pallas_call(
        body, name=name, grid_spec=grid_spec, out_shape=jax.ShapeDtypeStruct(theirs.shape, BF16),
        compiler_params=_params("parallel", "parallel"))(core, parts, theirs)


def _adamw_math(w, g, m, v):
    m = ADAM_B1 * m + (1.0 - ADAM_B1) * g
    v = ADAM_B2 * v + (1.0 - ADAM_B2) * (g * g)
    m_hat = m / (1.0 - ADAM_B1 ** ADAM_STEP)
    v_hat = v / (1.0 - ADAM_B2 ** ADAM_STEP)
    delta = -ADAM_LR * (m_hat / (jnp.sqrt(v_hat) + ADAM_EPS) + ADAM_WD * w)
    return delta, m, v


def _scattered_pieces(sums, got):
    return [("own", sums)] + [("peer%d" % j, got) for j in range(3)]


def _piece_spec(kind, tr, C):
    if kind == "own":
        return pl.BlockSpec((None, tr, C), lambda i, chip_ref: (chip_ref[0], i, 0))
    if kind == "plain":
        return pl.BlockSpec((tr, C), lambda i, chip_ref: (i, 0))
    return pl.BlockSpec((None, tr, C), functools.partial(lambda j, i, chip_ref: (j, i, 0), int(kind[-1])))


def _adamw_shard(chip, w, m, v, layer, pieces, so_far, name):
    L, R, C = w.shape
    tr = _pick(R, 256, 16)
    n_p = len(pieces)

    def body(chip_ref, w_ref, m_ref, v_ref, *rest):
        g = rest[0][...].astype(F32)
        for p in rest[1:n_p]:
            g = g + p[...].astype(F32)
        delta, nm, nv = _adamw_math(w_ref[...], g, m_ref[...], v_ref[...])
        g_ref, d_ref, nm_ref, nv_ref = rest[-4:]
        g_ref[...] = g
        d_ref[...] = delta
        nm_ref[...] = nm
        nv_ref[...] = nv

    state = pl.BlockSpec((None, tr, C), lambda i, chip_ref: (layer, i, 0))
    carried = [] if so_far is None else list(so_far)
    first_carried = 1 + 3 + n_p
    grid_spec = pltpu.PrefetchScalarGridSpec(
        num_scalar_prefetch=1, grid=(R // tr,),
        in_specs=[state] * 3 + [_piece_spec(kind, tr, C) for kind, _ in pieces] + [_ANY] * len(carried),
        out_specs=[state] * 4)
    return pl.pallas_call(
        body, name=name, grid_spec=grid_spec, out_shape=[jax.ShapeDtypeStruct((L, R, C), F32)] * 4,
        input_output_aliases={first_carried + t: t for t in range(len(carried))},
        compiler_params=_params("parallel"))(chip, w, m, v, *[a for _, a in pieces], *carried)


def _adamw_small(w, m, v, gathered, name):
    R = w.shape[0]
    tr = _pick(R, 512, SUBLANES)

    def body(w_ref, m_ref, v_ref, gg_ref, g_ref, d_ref, nm_ref, nv_ref):
        g = gg_ref[0, 0]
        for k in range(4):
            for c in range(2):
                if c or k:
                    g = g + gg_ref[k, c]
        delta, nm, nv = _adamw_math(w_ref[...], g, m_ref[...], v_ref[...])
        g_ref[...] = g
        d_ref[...] = delta
        nm_ref[...] = nm
        nv_ref[...] = nv

    row = pl.BlockSpec((tr, LANES), lambda i: (i, 0))
    out = jax.ShapeDtypeStruct((R, LANES), F32)
    return pl.pallas_call(
        body, name=name, grid=(R // tr,),
        in_specs=[row, row, row, pl.BlockSpec((4, 2, tr, LANES), lambda i: (0, 0, i, 0))], out_specs=[row] * 4,
        out_shape=[out] * 4, compiler_params=_params("parallel"))(w, m, v, gathered)


def _layer_fwd(x, w, s, tag, sched):
    h = _rms_fwd(x, s["norm_mix_g"], f"rms_mix_{tag}")
    z = sched.carry(f"in_proj_{tag}", _mm, h, w["w_in_t"], "nt", F32, tn=512)
    q, k, v = (z[:, a:b].reshape(z.shape[0], -1, HEAD_DIM).transpose(1, 0, 2)
               for a, b in ((0, OFF_K), (OFF_K, OFF_V), (OFF_V, OFF_U)))
    ya = sched.carry(f"attn_fwd_{tag}", _attn_fwd, q, k, v, s["q_norm_g"], s["k_norm_g"], s["attn_sinks"])
    ya = ya.transpose(1, 0, 2).reshape(z.shape[0], ATTN_WIDTH)
    y, s_r, s_i = sched.carry(f"ssm_fwd_{tag}", _ssm_fwd, z, *s["ssm16"], s["ssm_d"])
    pre, y3 = _glu_fwd(y, w["ssm_glu_w"], s["ssm_glu_b"], f"glu_fwd_{tag}")
    a, b, merged = sched.carry(f"merge_fwd_{tag}", _merge_fwd, ya, y3, w["w_attn_branch"], w["w_ssm_branch"], z,
                               s["gate_bias"])
    x1 = _mm(merged, w["w_out"], "nn", F32, f"out_proj_{tag}", residual=x)
    h2 = _rms_fwd(x1, s["norm_ffn_g"], f"rms_ffn_{tag}")
    g, u, act = sched.carry(f"ffn_fwd_{tag}", _ffn_fwd, h2, w["w_ffn_in"])
    x2 = sched.carry(f"ffn_out_{tag}", _mm, act, w["w_ffn_out"], "nn", F32, residual=x1)
    saved = dict(x=x, h=h, z=z, ya=ya, y=y, s_r=s_r, s_i=s_i, pre=pre, y3=y3, a=a, b=b, merged=merged,
                 x1=x1, h2=h2, g=g, u=u, act=act)
    return x2, saved


def _layer_bwd(dx2, sv, w, s, tag, sched, scatter):
    gw, gs = {}, {}
    dg16, du16 = _ffn_bwd_act(dx2, w["w_ffn_out"], sv["g"], sv["u"], f"ffn_bwd_act_{tag}")
    gw["w_ffn_out"] = _mm(sv["act"], dx2, "tn", BF16, f"dw_ffn_out_{tag}", tm=1408)
    dh2 = _ffn_bwd_in(dg16, du16, w["w_ffn_in"], f"dh2_{tag}")
    gw["w_ffn_in"] = _ffn_bwd_w(sv["h2"], dg16, du16, f"dw_ffn_in_{tag}")
    scatter(FFN_WEIGHTS[:1], gw, f"attn_bwd_{tag}")
    dx1, gs["norm_ffn_g"] = _rms_bwd(dh2, sv["x1"], s["norm_ffn_g"], dx2, f"rms_ffn_bwd_{tag}")
    dm = _mm(dx1, w["w_out"], "nt", F32, f"dmerged_{tag}")
    gw["w_out"] = _mm(sv["merged"], dx1, "tn", BF16, f"dw_out_{tag}")
    da16, db16, dza, dzs, dba, dbs = _merge_bwd(dm, sv["a"], sv["b"], sv["z"], s["gate_bias"], f"merge_bwd_{tag}")
    gs["gate_bias"] = jnp.concatenate([dba, dbs], axis=1)
    dya = _mm(da16, w["w_attn_branch"], "nt", F32, f"dya_{tag}")
    gw["w_attn_branch"] = _mm(sv["ya"], da16, "tn", BF16, f"dw_attn_branch_{tag}")
    dy3 = _mm(db16, w["w_ssm_branch"], "nt", F32, f"dy3_{tag}")
    gw["w_ssm_branch"] = _mm(sv["y3"], db16, "tn", BF16, f"dw_ssm_branch_{tag}")
    dpre16, t1, y2_16, gs["ssm_glu_b"] = _glu_bwd_gate(dy3, sv["y"], sv["pre"], f"glu_bwd_gate_{tag}")
    dy = _glu_bwd_in(dpre16, w["ssm_glu_w"], t1, sv["y"], f"glu_bwd_in_{tag}")
    gw["ssm_glu_w"] = _mm(y2_16, dpre16, "tn", BF16, f"dw_glu_{tag}")
    scatter(FFN_WEIGHTS[1:] + MIXER_WEIGHTS[1:], gw, f"ssm_bwd_{tag}")
    du_ssm, *gs["ssm_disc"], gs["ssm_d"] = sched.carry(f"ssm_bwd_{tag}", _ssm_bwd, dy, sv["z"], sv["s_r"], sv["s_i"],
                                                       *s["ssm16"], s["ssm_d"])
    dq, dk, dv, gs["q_norm_g"], gs["k_norm_g"], dsk = sched.carry(
        f"attn_bwd_{tag}", _attn_bwd, sv["z"], dya, s["q_norm_g"], s["k_norm_g"], s["attn_sinks"])
    gs["attn_sinks"] = dsk[:, 0].reshape(1, N_Q_HEADS)
    dz = [dq, jnp.concatenate([dk, dv], axis=1).astype(BF16), du_ssm, dza, dzs]
    gw["w_in"] = _mm_cols_tn(dz, sv["h"], f"dw_in_t_{tag}", tn=2048)
    scatter(MIXER_WEIGHTS[:1], gw, f"dh_{tag}")
    dh = sched.carry(f"dh_{tag}", _mm_cols_nn, dz, w["w_in_t"])
    dx, gs["norm_mix_g"] = _rms_bwd(dh, sv["x"], s["norm_mix_g"], dx1, f"rms_mix_bwd_{tag}")
    return dx, gs


def _shard_to_send(name, shard):
    return (shard.T if name == "w_in" else shard).astype(BF16)


def _assemble(name, gathered):
    if name == "w_ffn_in":
        return gathered
    if name in COL_SHARDED and name != "w_in":
        rows = gathered.shape[2]
        return gathered.transpose(2, 0, 1, 3).reshape(rows, -1)
    return gathered.reshape(-1, gathered.shape[3])


class _Weights:
    def __init__(self):
        self.sources, self.ready = [], {}

    def __getitem__(self, name):
        if name not in self.ready:
            key = "w_in" if name == "w_in_t" else name
            (gathered,) = [src[key] for src in self.sources if key in src]
            self.ready[name] = _assemble(key, gathered)
        return self.ready[name]


def _disassemble(name, grad):
    if name == "w_ffn_in":
        return grad
    if name in COL_SHARDED and name != "w_in":
        rows, cols = grad.shape
        return grad.reshape(rows, 4, 2, cols // N_DEV).transpose(1, 2, 0, 3)
    rows, cols = grad.shape
    return grad.reshape(4, 2, rows // N_DEV, cols)


def _pack(arrays):
    flat = jnp.concatenate([a.reshape(-1) for a in arrays])
    pad = (-flat.shape[0]) % (SUBLANES * LANES)
    return jnp.pad(flat, (0, pad)).reshape(-1, LANES)


def _unpack(packed, like):
    flat, out, off = packed.reshape(-1), [], 0
    for a in like:
        out.append(flat[off:off + a.size].reshape(a.shape))
        off += a.size
    return out


def kernel(x, norm_mix_g, w_in, gate_bias, q_norm_g, k_norm_g, attn_sinks, ssm_lambda_re, ssm_lambda_im, ssm_log_dt, ssm_b_re, ssm_b_im, ssm_c_re, ssm_c_im, ssm_d, ssm_glu_w, ssm_glu_b, w_attn_branch, w_ssm_branch, w_out, norm_ffn_g, w_ffn_in, w_ffn_out, loss_target, m_norm_mix_g, m_w_in, m_gate_bias, m_q_norm_g, m_k_norm_g, m_attn_sinks, m_ssm_lambda_re, m_ssm_lambda_im, m_ssm_log_dt, m_ssm_b_re, m_ssm_b_im, m_ssm_c_re, m_ssm_c_im, m_ssm_d, m_ssm_glu_w, m_ssm_glu_b, m_w_attn_branch, m_w_ssm_branch, m_w_out, m_norm_ffn_g, m_w_ffn_in, m_w_ffn_out, v_norm_mix_g, v_w_in, v_gate_bias, v_q_norm_g, v_k_norm_g, v_attn_sinks, v_ssm_lambda_re, v_ssm_lambda_im, v_ssm_log_dt, v_ssm_b_re, v_ssm_b_im, v_ssm_c_re, v_ssm_c_im, v_ssm_d, v_ssm_glu_w, v_ssm_glu_b, v_w_attn_branch, v_w_ssm_branch, v_w_out, v_norm_ffn_g, v_w_ffn_in, v_w_ffn_out):
    given = dict(locals())
    wts = {n: given[n] for n in WEIGHTS}
    mom = {n: given["m_" + n] for n in WEIGHTS}
    var = {n: given["v_" + n] for n in WEIGHTS}
    depth = w_in.shape[0]
    xs = x[0]
    target = loss_target[0]

    sched = _Schedule()
    shards = [{n: _shard_to_send(n, wts[n][l]) for n in BIG} for l in range(depth)]
    full = [_Weights() for _ in range(depth)]

    def gather(layer, names, send_host, pass_host):
        sent = sched.ride(send_host, _gather_send, names, shards[layer])
        full[layer].sources.append(sched.ride(pass_host, _gather_pass, names, sent))

    first = MIXER_WEIGHTS[:1]
    full[0].sources.append(dict(zip(first, _all_gather([shards[0][n] for n in first], "gather_w_in_0"))))
    gather(0, MIXER_WEIGHTS[1:], "in_proj_0", "attn_fwd_0")
    gather(0, ("w_ffn_in",), "attn_fwd_0", "ssm_fwd_0")
    gather(0, ("w_ffn_out",), "ssm_fwd_0", "merge_fwd_0")
    for l in range(1, depth):
        gather(l, first, f"ffn_fwd_{l - 1}", f"ffn_out_{l - 1}")
        gather(l, MIXER_WEIGHTS[1:], f"ffn_out_{l - 1}", f"in_proj_{l}")
        gather(l, ("w_ffn_out",), f"in_proj_{l}", f"attn_fwd_{l}")
        gather(l, ("w_ffn_in",), f"attn_fwd_{l}", f"ssm_fwd_{l}")

    small, disc_vjp = [], []
    for l in range(depth):
        s = {n: wts[n][l].reshape(1, -1) for n in ("norm_mix_g", "gate_bias", "q_norm_g", "k_norm_g", "attn_sinks",
                                                   "ssm_d", "ssm_glu_b", "norm_ffn_g")}
        disc, vjp = jax.vjp(_ssm_discretize, *[wts[n][l] for n in ("ssm_lambda_re", "ssm_lambda_im", "ssm_log_dt",
                                                                  "ssm_b_re", "ssm_b_im", "ssm_c_re", "ssm_c_im")])
        s["ssm16"] = (disc[0], disc[1]) + tuple(d.astype(BF16) for d in disc[2:])
        small.append(s)
        disc_vjp.append(vjp)

    act, saved = xs, []
    for l in range(depth):
        act, sv = _layer_fwd(act, full[l], small[l], str(l), sched)
        saved.append(sv)
    dact, loss_local = _loss_grad(act, target, "loss_head")

    out = {"grad": {}, "delta": {}, "new_m": {}, "new_v": {}}
    results = {n: None for n in BIG}
    core = lax.axis_index("c").astype(jnp.int32).reshape(1)
    chip = (2 * lax.axis_index("x") + lax.axis_index("y")).astype(jnp.int32).reshape(1)
    state = {n: [a.transpose(0, 2, 1) if n == "w_in" else a for a in (wts[n], mom[n], var[n])] for n in BIG}
    small_grads = [None] * depth
    for l in reversed(range(depth)):
        scattered = []

        def scatter(names, grads, host):
            parts = [_disassemble(n, grads[n]) for n in names]
            theirs = _pair_exchange(parts, f"grad_pair_exchange_{host}")
            sums = {n: _add_pair(core, p, t, f"grad_pair_sum_{n}_{host}") for n, p, t in zip(names, parts, theirs)}
            scattered.append((sums, sched.ride(host, _scatter_send, names, sums)))

        dact, gs = _layer_bwd(dact, saved[l], full[l], small[l], str(l), sched, scatter)
        (gs["ssm_lambda_re"], gs["ssm_lambda_im"], gs["ssm_log_dt"], gs["ssm_b_re"], gs["ssm_b_im"], gs["ssm_c_re"],
         gs["ssm_c_im"]) = disc_vjp[l](tuple(gs.pop("ssm_disc")))
        small_grads[l] = gs
        for sums, got in scattered:
            for n in got:
                results[n] = _adamw_shard(chip, state[n][0], state[n][1], state[n][2], l,
                                          _scattered_pieces(sums[n], got[n]), results[n], f"adamw_{n}_{l}")
    loss = lax.psum(loss_local, ("x", "y", "c"))
    for n in BIG:
        for kind, res in zip(("grad", "delta", "new_m", "new_v"), results[n]):
            out[kind][n] = res.transpose(0, 2, 1) if n == "w_in" else res

    like = [wts[n] for n in SMALL]
    g_small = _pack([jnp.stack([small_grads[l][n].reshape(wts[n].shape[1:]) for l in range(depth)]) for n in SMALL])
    (gathered_small,) = _all_gather([g_small], "gather_small_grads")
    res = _adamw_small(_pack(like), _pack([mom[n] for n in SMALL]), _pack([var[n] for n in SMALL]), gathered_small,
                       "adamw_small")
    for kind, packed in zip(("grad", "delta", "new_m", "new_v"), res):
        for n, a in zip(SMALL, _unpack(packed, like)):
            out[kind][n] = a

    grad_x = dact.reshape(x.shape)
    return (loss, grad_x, *[out["grad"][n] for n in WEIGHTS], *[out["delta"][n] for n in WEIGHTS],
            *[out["new_m"][n] for n in WEIGHTS], *[out["new_v"][n] for n in WEIGHTS])
```
